```python
import jax, jax.numpy as jnp
from jax import lax
import numpy as np

D_MODEL = 1024
BATCH = 32
SEQ = 2048
DEPTH = 2

D_MIX = 2 * D_MODEL
D_SSD = D_MIX // 2
SSD_HEAD_DIM = 64
SSD_HEADS = D_SSD // SSD_HEAD_DIM
SSD_GROUPS = 2
SSD_STATE = 128
SSD_CONV = 4
CHUNK = 128
MLA_HEADS = 8
QK_NOPE = 128
QK_ROPE = 64
V_DIM = 128
D_ATT = MLA_HEADS * V_DIM
Q_RANK = 384
KV_RANK = 256
ROPE_BASE = 10000.0
Q_BLOCK = 128
D_FF = 2816
FF_CONV = 3
EPS = 1e-6

D_XBC = D_SSD + 2 * SSD_GROUPS * SSD_STATE
IN_SIZES = [D_SSD, D_XBC, SSD_HEADS, Q_RANK, KV_RANK, QK_ROPE]
IN_SPLITS = np.cumsum(IN_SIZES)[:-1].tolist()
D_IN = int(sum(IN_SIZES))

kernel_name = "hybrid_ssd_mla_convglu_adaln"


def rms_norm(x, g):
    xf = x.astype(jnp.float32)
    y = xf * lax.rsqrt(jnp.mean(xf * xf, axis=-1, keepdims=True) + EPS)
    return y.astype(x.dtype) * g


def modulate(h, shift, scale):
    return h * (1 + scale[:, None, :]) + shift[:, None, :]


def causal_depthwise_conv(u, w, b):
    k = w.shape[0]
    out = lax.conv_general_dilated(
        u, w[:, None, :].astype(u.dtype), window_strides=(1,), padding=[(k - 1, 0)],
        dimension_numbers=('NWC', 'WIO', 'NWC'), feature_group_count=u.shape[-1])
    return out + b


def apply_rope(t, cos, sin):
    t1, t2 = jnp.split(t, 2, axis=-1)
    return jnp.concatenate([t1 * cos - t2 * sin, t2 * cos + t1 * sin], axis=-1)


def ssd_chunked_scan(xs, dt, a_head, bm, cm):
    bsz, s, h, p = xs.shape
    nc = s // CHUNK
    k = h // SSD_GROUPS
    x_dt = (xs * dt[..., None]).reshape(bsz, nc, CHUNK, SSD_GROUPS, k, p)
    a = (dt * a_head).reshape(bsz, nc, CHUNK, SSD_GROUPS, k).transpose(0, 1, 3, 4, 2)
    bm = bm.reshape(bsz, nc, CHUNK, SSD_GROUPS, SSD_STATE)
    cm = cm.reshape(bsz, nc, CHUNK, SSD_GROUPS, SSD_STATE)
    a_cs = jnp.cumsum(a, axis=-1)
    causal = jnp.asarray(np.tril(np.ones((CHUNK, CHUNK), dtype=bool)))
    seg = a_cs[..., :, None] - a_cs[..., None, :]
    decay_in = jnp.exp(jnp.where(causal, seg, -jnp.inf))
    cb = jnp.einsum('bclgn,bcsgn->bcgls', cm, bm)
    y_diag = jnp.einsum('bcgkls,bcsgkp->bclgkp', cb[:, :, :, None] * decay_in, x_dt)
    decay_to_end = jnp.exp(a_cs[..., -1:] - a_cs)
    chunk_states = jnp.einsum('bclgn,bcgkl,bclgkp->bcgkpn', bm, decay_to_end, x_dt)
    chunk_decay = jnp.exp(a_cs[..., -1])

    def step(state, inp):
        st, dec = inp
        return state * dec[..., None, None] + st, state

    init = jnp.zeros((bsz, SSD_GROUPS, k, p, SSD_STATE), chunk_states.dtype)
    _, prev = lax.scan(step, init, (jnp.moveaxis(chunk_states, 1, 0), jnp.moveaxis(chunk_decay, 1, 0)))
    prev = jnp.moveaxis(prev, 0, 1)
    y_off = jnp.einsum('bclgn,bcgkpn,bcgkl->bclgkp', cm, prev, jnp.exp(a_cs))
    return (y_diag + y_off).reshape(bsz, s, h, p)


def ssd_mixer(z, xbc, dt_raw, conv_w, conv_b, dt_bias, a_log, d_skip, ssd_norm):
    bsz, s, _ = z.shape
    xbc = jax.nn.silu(causal_depthwise_conv(xbc, conv_w, conv_b))
    xs, bm, cm = jnp.split(xbc, [D_SSD, D_SSD + SSD_GROUPS * SSD_STATE], axis=-1)
    xs = xs.reshape(bsz, s, SSD_HEADS, SSD_HEAD_DIM)
    bm = bm.reshape(bsz, s, SSD_GROUPS, SSD_STATE)
    cm = cm.reshape(bsz, s, SSD_GROUPS, SSD_STATE)
    dt = jax.nn.softplus(dt_raw + dt_bias)
    a_head = -jnp.exp(a_log)
    y = ssd_chunked_scan(xs, dt, a_head, bm, cm) + xs * d_skip[:, None]
    y = y.reshape(bsz, s, D_SSD) * jax.nn.silu(z)
    yg = y.reshape(bsz, s, SSD_GROUPS, D_SSD // SSD_GROUPS).astype(jnp.float32)
    yg = yg * lax.rsqrt(jnp.mean(yg * yg, axis=-1, keepdims=True) + EPS)
    return yg.reshape(bsz, s, D_SSD).astype(y.dtype) * ssd_norm


def mla_mixer(cq, ckv, k_rope_raw, cos, sin, q_norm, w_uq, kv_norm, w_ukv, attn_norm):
    bsz, s, _ = cq.shape
    q = (rms_norm(cq, q_norm) @ w_uq).reshape(bsz, s, MLA_HEADS, QK_NOPE + QK_ROPE)
    q_nope, q_rope = jnp.split(q, [QK_NOPE], axis=-1)
    q_rope = apply_rope(q_rope, cos[:, :, None, :], sin[:, :, None, :])
    kv = (rms_norm(ckv, kv_norm) @ w_ukv).reshape(bsz, s, MLA_HEADS, QK_NOPE + V_DIM)
    k_nope, v = jnp.split(kv, [QK_NOPE], axis=-1)
    k_rope = apply_rope(k_rope_raw, cos, sin)
    scale = (QK_NOPE + QK_ROPE) ** -0.5
    outs = []
    for i in range(s // Q_BLOCK):
        q0, q1 = i * Q_BLOCK, (i + 1) * Q_BLOCK
        sc = (jnp.einsum('bqhd,bkhd->bhqk', q_nope[:, q0:q1], k_nope[:, :q1])
              + jnp.einsum('bqhr,bkr->bhqk', q_rope[:, q0:q1], k_rope[:, :q1]))
        sc = sc.astype(jnp.float32) * scale
        mask = jnp.asarray(np.arange(q0, q1)[:, None] >= np.arange(q1)[None, :])
        probs = jax.nn.softmax(jnp.where(mask, sc, -jnp.inf), axis=-1).astype(v.dtype)
        outs.append(jnp.einsum('bhqk,bkhd->bqhd', probs, v[:, :q1]))
    o = jnp.concatenate(outs, axis=1).reshape(bsz, s, D_ATT)
    return rms_norm(o, attn_norm)


def conv_glu_ffn(h, w_up, conv_w, conv_b, w_down):
    u = causal_depthwise_conv(h @ w_up, conv_w, conv_b)
    gate, val = jnp.split(u, 2, axis=-1)
    return (jax.nn.silu(gate) * val) @ w_down


def _fwd_setup_inputs(seed: int = 0) -> dict:
    key = jax.random.key(seed)
    ks = jax.random.split(key, 32)

    def nrm(k, shape, scale):
        return jax.random.normal(k, shape, jnp.float32) * scale

    def gain(k, shape):
        return 1.0 + nrm(k, shape, 0.02)

    L = DEPTH
    x = nrm(ks[0], (BATCH, SEQ, D_MODEL), 1.0)
    c = nrm(ks[1], (BATCH, D_MODEL), 1.0)
    offsets = jax.random.randint(ks[2], (BATCH, 1), 0, 4096, dtype=jnp.int32)
    positions = (offsets + jnp.arange(SEQ, dtype=jnp.int32)[None, :]).astype(jnp.int32)
    dt0 = jnp.exp(jax.random.uniform(ks[9], (L, SSD_HEADS), jnp.float32, np.log(1e-3), np.log(1e-1)))
    dt_bias = dt0 + jnp.log(-jnp.expm1(-dt0))
    a_log = jnp.log(jax.random.uniform(ks[10], (L, SSD_HEADS), jnp.float32, 1.0, 16.0))
    return {
        'x': x, 'c': c, 'positions': positions,
        'w_ada': nrm(ks[3], (L, D_MODEL, 6 * D_MODEL), D_MODEL ** -0.5),
        'b_ada': nrm(ks[4], (L, 6 * D_MODEL), 0.02),
        'norm_mix': gain(ks[5], (L, D_MODEL)),
        'w_in': nrm(ks[6], (L, D_MODEL, D_IN), D_MODEL ** -0.5),
        'conv_w': nrm(ks[7], (L, SSD_CONV, D_XBC), SSD_CONV ** -0.5),
        'conv_b': nrm(ks[8], (L, D_XBC), 0.02),
        'dt_bias': dt_bias,
        'a_log': a_log,
        'd_skip': gain(ks[11], (L, SSD_HEADS)),
        'ssd_norm': gain(ks[12], (L, D_SSD)),
        'q_norm': gain(ks[13], (L, Q_RANK)),
        'w_uq': nrm(ks[14], (L, Q_RANK, MLA_HEADS * (QK_NOPE + QK_ROPE)), Q_RANK ** -0.5),
        'kv_norm': gain(ks[15], (L, KV_RANK)),
        'w_ukv': nrm(ks[16], (L, KV_RANK, MLA_HEADS * (QK_NOPE + V_DIM)), KV_RANK ** -0.5),
        'attn_norm': gain(ks[17], (L, D_ATT)),
        'w_out': nrm(ks[18], (L, D_MIX, D_MODEL), D_MIX ** -0.5),
        'norm_mlp': gain(ks[19], (L, D_MODEL)),
        'w_up': nrm(ks[20], (L, D_MODEL, 2 * D_FF), D_MODEL ** -0.5),
        'conv_ff_w': nrm(ks[21], (L, FF_CONV, 2 * D_FF), FF_CONV ** -0.5),
        'conv_ff_b': nrm(ks[22], (L, 2 * D_FF), 0.02),
        'w_down': nrm(ks[23], (L, D_FF, D_MODEL), D_FF ** -0.5),
        'final_norm': gain(ks[24], (D_MODEL,)),
    }


def _fwd_reference(x, c, positions, w_ada, b_ada, norm_mix, w_in, conv_w, conv_b, dt_bias, a_log,
              d_skip, ssd_norm, q_norm, w_uq, kv_norm, w_ukv, attn_norm, w_out, norm_mlp,
              w_up, conv_ff_w, conv_ff_b, w_down, final_norm):
    inv_freq = jnp.asarray(1.0 / (ROPE_BASE ** (np.arange(0, QK_ROPE, 2, dtype=np.float32) / QK_ROPE)))
    angles = positions.astype(jnp.float32)[..., None] * inv_freq
    cos = jnp.cos(angles).astype(x.dtype)
    sin = jnp.sin(angles).astype(x.dtype)
    c_act = jax.nn.silu(c)
    for l in range(DEPTH):
        mod = c_act @ w_ada[l] + b_ada[l]
        sh1, sc1, g1, sh2, sc2, g2 = jnp.split(mod, 6, axis=-1)
        h = modulate(rms_norm(x, norm_mix[l]), sh1, sc1)
        z, xbc, dt_raw, cq, ckv, kr = jnp.split(h @ w_in[l], IN_SPLITS, axis=-1)
        y_ssd = ssd_mixer(z, xbc, dt_raw, conv_w[l], conv_b[l], dt_bias[l], a_log[l], d_skip[l], ssd_norm[l])
        y_att = mla_mixer(cq, ckv, kr, cos, sin, q_norm[l], w_uq[l], kv_norm[l], w_ukv[l], attn_norm[l])
        y = jnp.concatenate([y_ssd, y_att], axis=-1) @ w_out[l]
        x = x + g1[:, None, :] * y
        h = modulate(rms_norm(x, norm_mlp[l]), sh2, sc2)
        x = x + g2[:, None, :] * conv_glu_ffn(h, w_up[l], conv_ff_w[l], conv_ff_b[l], w_down[l])
    return rms_norm(x, final_norm)


import jax as _jax
import jax.numpy as _jnp

TWIN_FORMAT = 'train_step'
FWD_PARAMS = ['x', 'c', 'positions', 'w_ada', 'b_ada', 'norm_mix', 'w_in', 'conv_w', 'conv_b', 'dt_bias', 'a_log', 'd_skip', 'ssd_norm', 'q_norm', 'w_uq', 'kv_norm', 'w_ukv', 'attn_norm', 'w_out', 'norm_mlp', 'w_up', 'conv_ff_w', 'conv_ff_b', 'w_down', 'final_norm']
TWIN_WEIGHTS = ['w_ada', 'b_ada', 'norm_mix', 'w_in', 'conv_w', 'conv_b', 'dt_bias', 'a_log', 'd_skip', 'ssd_norm', 'q_norm', 'w_uq', 'kv_norm', 'w_ukv', 'attn_norm', 'w_out', 'norm_mlp', 'w_up', 'conv_ff_w', 'conv_ff_b', 'w_down', 'final_norm']
TWIN_DIFF_INPUT = 'x'
TWIN_INPUTS = ['x', 'c', 'positions', 'w_ada', 'b_ada', 'norm_mix', 'w_in', 'conv_w', 'conv_b', 'dt_bias', 'a_log', 'd_skip', 'ssd_norm', 'q_norm', 'w_uq', 'kv_norm', 'w_ukv', 'attn_norm', 'w_out', 'norm_mlp', 'w_up', 'conv_ff_w', 'conv_ff_b', 'w_down', 'final_norm', 'loss_target', 'm_w_ada', 'm_b_ada', 'm_norm_mix', 'm_w_in', 'm_conv_w', 'm_conv_b', 'm_dt_bias', 'm_a_log', 'm_d_skip', 'm_ssd_norm', 'm_q_norm', 'm_w_uq', 'm_kv_norm', 'm_w_ukv', 'm_attn_norm', 'm_w_out', 'm_norm_mlp', 'm_w_up', 'm_conv_ff_w', 'm_conv_ff_b', 'm_w_down', 'm_final_norm', 'v_w_ada', 'v_b_ada', 'v_norm_mix', 'v_w_in', 'v_conv_w', 'v_conv_b', 'v_dt_bias', 'v_a_log', 'v_d_skip', 'v_ssd_norm', 'v_q_norm', 'v_w_uq', 'v_kv_norm', 'v_w_ukv', 'v_attn_norm', 'v_w_out', 'v_norm_mlp', 'v_w_up', 'v_conv_ff_w', 'v_conv_ff_b', 'v_w_down', 'v_final_norm']
TWIN_OUTPUTS = ['loss', 'grad_x', 'grad_w_ada', 'grad_b_ada', 'grad_norm_mix', 'grad_w_in', 'grad_conv_w', 'grad_conv_b', 'grad_dt_bias', 'grad_a_log', 'grad_d_skip', 'grad_ssd_norm', 'grad_q_norm', 'grad_w_uq', 'grad_kv_norm', 'grad_w_ukv', 'grad_attn_norm', 'grad_w_out', 'grad_norm_mlp', 'grad_w_up', 'grad_conv_ff_w', 'grad_conv_ff_b', 'grad_w_down', 'grad_final_norm', 'delta_w_ada', 'delta_b_ada', 'delta_norm_mix', 'delta_w_in', 'delta_conv_w', 'delta_conv_b', 'delta_dt_bias', 'delta_a_log', 'delta_d_skip', 'delta_ssd_norm', 'delta_q_norm', 'delta_w_uq', 'delta_kv_norm', 'delta_w_ukv', 'delta_attn_norm', 'delta_w_out', 'delta_norm_mlp', 'delta_w_up', 'delta_conv_ff_w', 'delta_conv_ff_b', 'delta_w_down', 'delta_final_norm', 'new_m_w_ada', 'new_m_b_ada', 'new_m_norm_mix', 'new_m_w_in', 'new_m_conv_w', 'new_m_conv_b', 'new_m_dt_bias', 'new_m_a_log', 'new_m_d_skip', 'new_m_ssd_norm', 'new_m_q_norm', 'new_m_w_uq', 'new_m_kv_norm', 'new_m_w_ukv', 'new_m_attn_norm', 'new_m_w_out', 'new_m_norm_mlp', 'new_m_w_up', 'new_m_conv_ff_w', 'new_m_conv_ff_b', 'new_m_w_down', 'new_m_final_norm', 'new_v_w_ada', 'new_v_b_ada', 'new_v_norm_mix', 'new_v_w_in', 'new_v_conv_w', 'new_v_conv_b', 'new_v_dt_bias', 'new_v_a_log', 'new_v_d_skip', 'new_v_ssd_norm', 'new_v_q_norm', 'new_v_w_uq', 'new_v_kv_norm', 'new_v_w_ukv', 'new_v_attn_norm', 'new_v_w_out', 'new_v_norm_mlp', 'new_v_w_up', 'new_v_conv_ff_w', 'new_v_conv_ff_b', 'new_v_w_down', 'new_v_final_norm']
TWIN_LEAF_KINDS = {'loss': 'loss', 'grad_x': 'grad_x', 'grad_w_ada': 'grad_w', 'grad_b_ada': 'grad_w', 'grad_norm_mix': 'grad_w', 'grad_w_in': 'grad_w', 'grad_conv_w': 'grad_w', 'grad_conv_b': 'grad_w', 'grad_dt_bias': 'grad_w', 'grad_a_log': 'grad_w', 'grad_d_skip': 'grad_w', 'grad_ssd_norm': 'grad_w', 'grad_q_norm': 'grad_w', 'grad_w_uq': 'grad_w', 'grad_kv_norm': 'grad_w', 'grad_w_ukv': 'grad_w', 'grad_attn_norm': 'grad_w', 'grad_w_out': 'grad_w', 'grad_norm_mlp': 'grad_w', 'grad_w_up': 'grad_w', 'grad_conv_ff_w': 'grad_w', 'grad_conv_ff_b': 'grad_w', 'grad_w_down': 'grad_w', 'grad_final_norm': 'grad_w', 'delta_w_ada': 'delta_w', 'delta_b_ada': 'delta_w', 'delta_norm_mix': 'delta_w', 'delta_w_in': 'delta_w', 'delta_conv_w': 'delta_w', 'delta_conv_b': 'delta_w', 'delta_dt_bias': 'delta_w', 'delta_a_log': 'delta_w', 'delta_d_skip': 'delta_w', 'delta_ssd_norm': 'delta_w', 'delta_q_norm': 'delta_w', 'delta_w_uq': 'delta_w', 'delta_kv_norm': 'delta_w', 'delta_w_ukv': 'delta_w', 'delta_attn_norm': 'delta_w', 'delta_w_out': 'delta_w', 'delta_norm_mlp': 'delta_w', 'delta_w_up': 'delta_w', 'delta_conv_ff_w': 'delta_w', 'delta_conv_ff_b': 'delta_w', 'delta_w_down': 'delta_w', 'delta_final_norm': 'delta_w', 'new_m_w_ada': 'new_m', 'new_m_b_ada': 'new_m', 'new_m_norm_mix': 'new_m', 'new_m_w_in': 'new_m', 'new_m_conv_w': 'new_m', 'new_m_conv_b': 'new_m', 'new_m_dt_bias': 'new_m', 'new_m_a_log': 'new_m', 'new_m_d_skip': 'new_m', 'new_m_ssd_norm': 'new_m', 'new_m_q_norm': 'new_m', 'new_m_w_uq': 'new_m', 'new_m_kv_norm': 'new_m', 'new_m_w_ukv': 'new_m', 'new_m_attn_norm': 'new_m', 'new_m_w_out': 'new_m', 'new_m_norm_mlp': 'new_m', 'new_m_w_up': 'new_m', 'new_m_conv_ff_w': 'new_m', 'new_m_conv_ff_b': 'new_m', 'new_m_w_down': 'new_m', 'new_m_final_norm': 'new_m', 'new_v_w_ada': 'new_v', 'new_v_b_ada': 'new_v', 'new_v_norm_mix': 'new_v', 'new_v_w_in': 'new_v', 'new_v_conv_w': 'new_v', 'new_v_conv_b': 'new_v', 'new_v_dt_bias': 'new_v', 'new_v_a_log': 'new_v', 'new_v_d_skip': 'new_v', 'new_v_ssd_norm': 'new_v', 'new_v_q_norm': 'new_v', 'new_v_w_uq': 'new_v', 'new_v_kv_norm': 'new_v', 'new_v_w_ukv': 'new_v', 'new_v_attn_norm': 'new_v', 'new_v_w_out': 'new_v', 'new_v_norm_mlp': 'new_v', 'new_v_w_up': 'new_v', 'new_v_conv_ff_w': 'new_v', 'new_v_conv_ff_b': 'new_v', 'new_v_w_down': 'new_v', 'new_v_final_norm': 'new_v'}


def _forward(args):
    return _fwd_reference(*[args[k] for k in FWD_PARAMS])


def _output_shape():
    out = _jax.eval_shape(lambda: _forward(_fwd_setup_inputs(0)))
    return out.shape, out.dtype

N_MICROBATCH = 1
ADAM_LR = 0.001
ADAM_B1 = 0.9
ADAM_B2 = 0.999
ADAM_EPS = 1e-08
ADAM_WD = 0.01
ADAM_STEP = 10
PER_EXAMPLE_BATCH_AXIS = {'x': 0, 'c': 0, 'positions': 0, 'loss_target': 0}
SHARED_INPUTS = []
_WEIGHT_DTYPES = {'w_ada': _jnp.float32, 'b_ada': _jnp.float32, 'norm_mix': _jnp.float32, 'w_in': _jnp.float32, 'conv_w': _jnp.float32, 'conv_b': _jnp.float32, 'dt_bias': _jnp.float32, 'a_log': _jnp.float32, 'd_skip': _jnp.float32, 'ssd_norm': _jnp.float32, 'q_norm': _jnp.float32, 'w_uq': _jnp.float32, 'kv_norm': _jnp.float32, 'w_ukv': _jnp.float32, 'attn_norm': _jnp.float32, 'w_out': _jnp.float32, 'norm_mlp': _jnp.float32, 'w_up': _jnp.float32, 'conv_ff_w': _jnp.float32, 'conv_ff_b': _jnp.float32, 'w_down': _jnp.float32, 'final_norm': _jnp.float32}
MOMENT_SCALE = {'w_ada': 1.394340e-01, 'b_ada': 2.483226e-01, 'norm_mix': 1.383304e-01, 'w_in': 1.111959e-01, 'conv_w': 8.270623e-02, 'conv_b': 7.853962e-02, 'dt_bias': 3.428578e-01, 'a_log': 4.117839e-01, 'd_skip': 2.820932e-01, 'ssd_norm': 9.667099e-02, 'q_norm': 4.274740e-02, 'w_uq': 2.235746e-02, 'kv_norm': 2.791257e-01, 'w_ukv': 1.042146e-01, 'attn_norm': 1.490389e-01, 'w_out': 1.808330e-01, 'norm_mlp': 1.748871e-01, 'w_up': 7.918433e-02, 'conv_ff_w': 7.975490e-02, 'conv_ff_b': 6.495392e-02, 'w_down': 1.337060e-01, 'final_norm': 6.674094e+01}


def _to_microbatches(a, axis):
    t = _jnp.moveaxis(a, axis, 0)
    t = t.reshape((N_MICROBATCH, t.shape[0] // N_MICROBATCH) + t.shape[1:])
    return _jnp.moveaxis(t, 1, axis + 1)


def setup_inputs(seed: int = 0) -> dict:
    inp = _fwd_setup_inputs(seed)
    key = _jax.random.fold_in(_jax.random.key(seed), 7919)
    shape, _ = _output_shape()
    out = dict(inp)
    out["loss_target"] = _jax.random.normal(_jax.random.fold_in(key, 0), shape, _jnp.float32)
    for i, name in enumerate(TWIN_WEIGHTS):
        w = inp[name].astype(_jnp.float32)
        if MOMENT_SCALE is None:
            s = _jnp.sqrt(_jnp.mean(_jnp.square(w)) + 1e-30)
        else:
            s = MOMENT_SCALE[name]
        km, kv = _jax.random.split(_jax.random.fold_in(key, i + 1))
        out[name] = w
        out["m_" + name] = s * _jax.random.normal(km, w.shape, _jnp.float32)
        out["v_" + name] = (s * s) * _jax.random.uniform(kv, w.shape, _jnp.float32, 0.5, 1.5)
    if N_MICROBATCH > 1:
        for name, axis in PER_EXAMPLE_BATCH_AXIS.items():
            out[name] = _to_microbatches(out[name], axis)
    return {'x': out['x'], 'c': out['c'], 'positions': out['positions'], 'w_ada': out['w_ada'], 'b_ada': out['b_ada'], 'norm_mix': out['norm_mix'], 'w_in': out['w_in'], 'conv_w': out['conv_w'], 'conv_b': out['conv_b'], 'dt_bias': out['dt_bias'], 'a_log': out['a_log'], 'd_skip': out['d_skip'], 'ssd_norm': out['ssd_norm'], 'q_norm': out['q_norm'], 'w_uq': out['w_uq'], 'kv_norm': out['kv_norm'], 'w_ukv': out['w_ukv'], 'attn_norm': out['attn_norm'], 'w_out': out['w_out'], 'norm_mlp': out['norm_mlp'], 'w_up': out['w_up'], 'conv_ff_w': out['conv_ff_w'], 'conv_ff_b': out['conv_ff_b'], 'w_down': out['w_down'], 'final_norm': out['final_norm'], 'loss_target': out['loss_target'], 'm_w_ada': out['m_w_ada'], 'm_b_ada': out['m_b_ada'], 'm_norm_mix': out['m_norm_mix'], 'm_w_in': out['m_w_in'], 'm_conv_w': out['m_conv_w'], 'm_conv_b': out['m_conv_b'], 'm_dt_bias': out['m_dt_bias'], 'm_a_log': out['m_a_log'], 'm_d_skip': out['m_d_skip'], 'm_ssd_norm': out['m_ssd_norm'], 'm_q_norm': out['m_q_norm'], 'm_w_uq': out['m_w_uq'], 'm_kv_norm': out['m_kv_norm'], 'm_w_ukv': out['m_w_ukv'], 'm_attn_norm': out['m_attn_norm'], 'm_w_out': out['m_w_out'], 'm_norm_mlp': out['m_norm_mlp'], 'm_w_up': out['m_w_up'], 'm_conv_ff_w': out['m_conv_ff_w'], 'm_conv_ff_b': out['m_conv_ff_b'], 'm_w_down': out['m_w_down'], 'm_final_norm': out['m_final_norm'], 'v_w_ada': out['v_w_ada'], 'v_b_ada': out['v_b_ada'], 'v_norm_mix': out['v_norm_mix'], 'v_w_in': out['v_w_in'], 'v_conv_w': out['v_conv_w'], 'v_conv_b': out['v_conv_b'], 'v_dt_bias': out['v_dt_bias'], 'v_a_log': out['v_a_log'], 'v_d_skip': out['v_d_skip'], 'v_ssd_norm': out['v_ssd_norm'], 'v_q_norm': out['v_q_norm'], 'v_w_uq': out['v_w_uq'], 'v_kv_norm': out['v_kv_norm'], 'v_w_ukv': out['v_w_ukv'], 'v_attn_norm': out['v_attn_norm'], 'v_w_out': out['v_w_out'], 'v_norm_mlp': out['v_norm_mlp'], 'v_w_up': out['v_w_up'], 'v_conv_ff_w': out['v_conv_ff_w'], 'v_conv_ff_b': out['v_conv_ff_b'], 'v_w_down': out['v_w_down'], 'v_final_norm': out['v_final_norm']}


def _loss(weights, diff, rest, loss_target):
    with _jax.named_scope("forward"):
        args = {**rest, TWIN_DIFF_INPUT: diff, **{k: w.astype(_WEIGHT_DTYPES[k]) for k, w in weights.items()}}
        y = _forward(args)
    with _jax.named_scope("loss_head"):
        err = _jnp.square(y.astype(_jnp.float32) - loss_target)
        return 0.5 * _jnp.sum(_jnp.mean(err, axis=-1)) if err.ndim else 0.5 * err


def _adamw(w, g, m, v):
    m = ADAM_B1 * m + (1.0 - ADAM_B1) * g
    v = ADAM_B2 * v + (1.0 - ADAM_B2) * _jnp.square(g)
    m_hat = m / (1.0 - ADAM_B1 ** ADAM_STEP)
    v_hat = v / (1.0 - ADAM_B2 ** ADAM_STEP)
    delta = -ADAM_LR * (m_hat / (_jnp.sqrt(v_hat) + ADAM_EPS) + ADAM_WD * w)
    return delta, m, v


def reference(x, c, positions, w_ada, b_ada, norm_mix, w_in, conv_w, conv_b, dt_bias, a_log, d_skip, ssd_norm, q_norm, w_uq, kv_norm, w_ukv, attn_norm, w_out, norm_mlp, w_up, conv_ff_w, conv_ff_b, w_down, final_norm, loss_target, m_w_ada, m_b_ada, m_norm_mix, m_w_in, m_conv_w, m_conv_b, m_dt_bias, m_a_log, m_d_skip, m_ssd_norm, m_q_norm, m_w_uq, m_kv_norm, m_w_ukv, m_attn_norm, m_w_out, m_norm_mlp, m_w_up, m_conv_ff_w, m_conv_ff_b, m_w_down, m_final_norm, v_w_ada, v_b_ada, v_norm_mix, v_w_in, v_conv_w, v_conv_b, v_dt_bias, v_a_log, v_d_skip, v_ssd_norm, v_q_norm, v_w_uq, v_kv_norm, v_w_ukv, v_attn_norm, v_w_out, v_norm_mlp, v_w_up, v_conv_ff_w, v_conv_ff_b, v_w_down, v_final_norm):
    given = dict(x=x, c=c, positions=positions, w_ada=w_ada, b_ada=b_ada, norm_mix=norm_mix, w_in=w_in, conv_w=conv_w, conv_b=conv_b, dt_bias=dt_bias, a_log=a_log, d_skip=d_skip, ssd_norm=ssd_norm, q_norm=q_norm, w_uq=w_uq, kv_norm=kv_norm, w_ukv=w_ukv, attn_norm=attn_norm, w_out=w_out, norm_mlp=norm_mlp, w_up=w_up, conv_ff_w=conv_ff_w, conv_ff_b=conv_ff_b, w_down=w_down, final_norm=final_norm, loss_target=loss_target, m_w_ada=m_w_ada, m_b_ada=m_b_ada, m_norm_mix=m_norm_mix, m_w_in=m_w_in, m_conv_w=m_conv_w, m_conv_b=m_conv_b, m_dt_bias=m_dt_bias, m_a_log=m_a_log, m_d_skip=m_d_skip, m_ssd_norm=m_ssd_norm, m_q_norm=m_q_norm, m_w_uq=m_w_uq, m_kv_norm=m_kv_norm, m_w_ukv=m_w_ukv, m_attn_norm=m_attn_norm, m_w_out=m_w_out, m_norm_mlp=m_norm_mlp, m_w_up=m_w_up, m_conv_ff_w=m_conv_ff_w, m_conv_ff_b=m_conv_ff_b, m_w_down=m_w_down, m_final_norm=m_final_norm, v_w_ada=v_w_ada, v_b_ada=v_b_ada, v_norm_mix=v_norm_mix, v_w_in=v_w_in, v_conv_w=v_conv_w, v_conv_b=v_conv_b, v_dt_bias=v_dt_bias, v_a_log=v_a_log, v_d_skip=v_d_skip, v_ssd_norm=v_ssd_norm, v_q_norm=v_q_norm, v_w_uq=v_w_uq, v_kv_norm=v_kv_norm, v_w_ukv=v_w_ukv, v_attn_norm=v_attn_norm, v_w_out=v_w_out, v_norm_mlp=v_norm_mlp, v_w_up=v_w_up, v_conv_ff_w=v_conv_ff_w, v_conv_ff_b=v_conv_ff_b, v_w_down=v_w_down, v_final_norm=v_final_norm)
    weights = {n: given[n] for n in TWIN_WEIGHTS}
    shared = {n: given[n] for n in SHARED_INPUTS}
    per_example = {n: given[n] for n in ['x', 'c', 'positions']}
    grad_fn = _jax.value_and_grad(_loss, argnums=(0, 1))

    def one_microbatch(ex, loss_target):
        ex = dict(ex)
        diff = ex.pop(TWIN_DIFF_INPUT)
        return grad_fn(weights, diff, {**shared, **ex}, loss_target)

    if N_MICROBATCH == 1:
        loss, (grad_w, grad_x) = one_microbatch(per_example, given["loss_target"])
    else:
        def body(carry, xs):
            loss_sum, grad_sum = carry
            l_k, (gw_k, gx_k) = one_microbatch(xs[0], xs[1])
            with _jax.named_scope("update"):
                return (loss_sum + l_k, _jax.tree.map(_jnp.add, grad_sum, gw_k)), gx_k

        init = (_jnp.zeros((), _jnp.float32), _jax.tree.map(_jnp.zeros_like, weights))
        (loss, grad_w), grad_x = _jax.lax.scan(body, init, (per_example, given["loss_target"]))
    with _jax.named_scope("update"):
        delta_w, new_m, new_v = {}, {}, {}
        for n in TWIN_WEIGHTS:
            delta_w[n], new_m[n], new_v[n] = _adamw(weights[n], grad_w[n], given["m_" + n], given["v_" + n])
    return (loss, grad_x, *[grad_w[n] for n in TWIN_WEIGHTS], *[delta_w[n] for n in TWIN_WEIGHTS],
            *[new_m[n] for n in TWIN_WEIGHTS], *[new_v[n] for n in TWIN_WEIGHTS])
```

```python
import functools
from typing import NamedTuple

import numpy as np
import jax
import jax.numpy as jnp
from jax import lax
from jax.experimental import pallas as pl
from jax.experimental.pallas import tpu as pltpu

F32 = jnp.float32
BF16 = jnp.bfloat16
MXU_DTYPE = jnp.bfloat16
WIRE_DTYPE = jnp.bfloat16
HIGHEST = lax.Precision.HIGHEST

D_MODEL = 1024
DEPTH = 2
D_SSD = 1024
SSD_HEADS = 16
SSD_HEAD_DIM = 64
SSD_GROUPS = 2
SSD_STATE = 128
SSD_CONV = 4
CHUNK = 128
MLA_HEADS = 8
QK_NOPE = 128
QK_ROPE = 64
V_DIM = 128
D_ATT = MLA_HEADS * V_DIM
Q_RANK = 384
KV_RANK = 256
ROPE_BASE = 10000.0
D_FF = 2816
FF_CONV = 3
EPS = 1e-6
D_XBC = D_SSD + 2 * SSD_GROUPS * SSD_STATE
D_IN = D_SSD + D_XBC + SSD_HEADS + Q_RANK + KV_RANK + QK_ROPE
ADAM_LR, ADAM_B1, ADAM_B2, ADAM_EPS, ADAM_WD, ADAM_STEP = 0.001, 0.9, 0.999, 1e-08, 0.01, 10

LANE = 128
N_CHIPS = 4
N_DEV = 8

OFF_Z, OFF_XBC, OFF_CKV, OFF_KR, OFF_DT, OFF_CQ = 0, 1024, 2560, 2816, 2944, 3072
N_PROJ = 3456
Q_HEAD_PAD = 256
VMEM_LIMIT = 56 * 1024 * 1024


def _cparams(n_axes):
    return pltpu.CompilerParams(dimension_semantics=("arbitrary",) * n_axes, vmem_limit_bytes=VMEM_LIMIT)


def _pick(n, prefs):
    for p in prefs:
        if n % p == 0:
            return p
    return n


def _silu(x):
    return x * jax.nn.sigmoid(x)


def _rms(x, g):
    return x * lax.rsqrt(jnp.mean(x * x, axis=-1, keepdims=True) + EPS) * g


def _softplus(x):
    return jnp.maximum(x, 0.0) + jnp.log(1.0 + jnp.exp(-jnp.abs(x)))


class Col(NamedTuple):
    arr: jax.Array
    width: int
    blk: int


def _as_col(a):
    return a if isinstance(a, Col) else Col(a, a.shape[-1], 0)


def _row_specs(tiled, pbatch, shared, tile):
    specs = [pl.BlockSpec((1, tile, t.width), lambda b, i, blk=t.blk: (b, i, blk)) for t in tiled]
    specs += [pl.BlockSpec((1, 1, p.shape[-1]), lambda b, i: (b, 0, 0)) for p in pbatch]
    specs += [pl.BlockSpec(s.shape, lambda b, i: (0, 0)) for s in shared]
    return specs


def _row_vals(refs, nt, npb):
    return [r[0].astype(F32) for r in refs[: nt + npb]] + [r[...].astype(F32) for r in refs[nt + npb :]]


def rowwise_fwd(name, fn, tiled, pbatch, shared, outs, tile=512):
    tiled = [_as_col(t) for t in tiled]
    B, S = tiled[0].arr.shape[:2]
    tile = min(tile, S)
    nt, npb, nsh = len(tiled), len(pbatch), len(shared)
    n_in = nt + npb + nsh

    def body(*refs):
        res = fn(*_row_vals(refs[:n_in], nt, npb))
        for r, v in zip(refs[n_in:], res):
            r[0] = v.astype(r.dtype)

    return pl.pallas_call(
        body,
        name=name,
        grid=(B, S // tile),
        in_specs=_row_specs(tiled, pbatch, shared, tile),
        out_specs=[pl.BlockSpec((1, tile, w), lambda b, i: (b, i, 0)) for w, _ in outs],
        out_shape=[jax.ShapeDtypeStruct((B, S, w), dt) for w, dt in outs],
        compiler_params=_cparams(2),
    )(*[t.arr for t in tiled], *pbatch, *shared)


def rowwise_bwd(name, fn, tiled, pbatch, shared, cts, grad_tiled, grad_shared, adds=None, tile=256):
    tiled = [_as_col(t) for t in tiled]
    adds = adds or {}
    B, S = tiled[0].arr.shape[:2]
    tile = min(tile, S)
    nt, npb, nsh = len(tiled), len(pbatch), len(shared)
    n_in = nt + npb + nsh
    add_idx = sorted(adds)
    gt = [i for i in range(nt) if grad_tiled[i]]
    gs = [i for i in range(nsh) if grad_shared[i]]
    diff = gt + [nt + i for i in range(npb)] + [nt + npb + i for i in gs]
    n_ct, n_add = len(cts), len(add_idx)

    def body(*refs):
        vals = _row_vals(refs[:n_in], nt, npb)
        ct_v = tuple(r[0].astype(F32) for r in refs[n_in : n_in + n_ct])
        add_v = {i: r[0].astype(F32) for i, r in zip(add_idx, refs[n_in + n_ct : n_in + n_ct + n_add])}
        out_refs = refs[n_in + n_ct + n_add :]

        def f(*dargs):
            full = list(vals)
            for k, i in enumerate(diff):
                full[i] = dargs[k]
            return tuple(fn(*full))

        _, vjp = jax.vjp(f, *[vals[i] for i in diff])
        grads = vjp(ct_v)
        b, i = pl.program_id(0), pl.program_id(1)
        k = 0
        for idx in gt:
            g = grads[k]
            if idx in add_v:
                g = g + add_v[idx]
            out_refs[k][0] = g.astype(out_refs[k].dtype)
            k += 1
        for _ in range(npb):
            r, g = out_refs[k], grads[k]

            @pl.when(i == 0)
            def _(r=r, g=g):
                r[0] = g

            @pl.when(i > 0)
            def _(r=r, g=g):
                r[0] += g

            k += 1
        for _ in gs:
            r, g = out_refs[k], grads[k]

            @pl.when((i == 0) & (b == 0))
            def _(r=r, g=g):
                r[...] = g

            @pl.when((i > 0) | (b > 0))
            def _(r=r, g=g):
                r[...] += g

            k += 1

    out_specs = [pl.BlockSpec((1, tile, tiled[i].width), lambda b, i: (b, i, 0)) for i in gt]
    out_shape = [jax.ShapeDtypeStruct((B, S, tiled[i].width), F32) for i in gt]
    out_specs += [pl.BlockSpec((1, 1, p.shape[-1]), lambda b, i: (b, 0, 0)) for p in pbatch]
    out_shape += [jax.ShapeDtypeStruct(p.shape, F32) for p in pbatch]
    out_specs += [pl.BlockSpec(shared[i].shape, lambda b, i: (0, 0)) for i in gs]
    out_shape += [jax.ShapeDtypeStruct(shared[i].shape, F32) for i in gs]
    in_specs = _row_specs(tiled, pbatch, shared, tile)
    in_specs += [pl.BlockSpec((1, tile, c.shape[-1]), lambda b, i: (b, i, 0)) for c in cts]
    in_specs += [pl.BlockSpec((1, tile, adds[i].shape[-1]), lambda b, i: (b, i, 0)) for i in add_idx]
    return pl.pallas_call(
        body,
        name=name,
        grid=(B, S // tile),
        in_specs=in_specs,
        out_specs=out_specs,
        out_shape=out_shape,
        compiler_params=_cparams(2),
    )(*[t.arr for t in tiled], *pbatch, *shared, *cts, *[adds[i] for i in add_idx])


def fn_prenorm(x, sc, sh, g, bsc, bsh):
    return (_rms(x, g) * (1.0 + sc + bsc) + (sh + bsh),)


def fn_resid_prenorm(x, y, gate, sc, sh, g, bgate, bsc, bsh):
    x1 = x + (gate + bgate) * y
    return x1, _rms(x1, g) * (1.0 + sc + bsc) + (sh + bsh)


def fn_mix(yscan, z, o, ssd_norm, attn_norm):
    y = yscan * _silu(z)
    half = D_SSD // SSD_GROUPS
    first = lax.broadcasted_iota(jnp.int32, y.shape, 1) < half
    sq = y * y
    m0 = jnp.sum(jnp.where(first, sq, 0.0), axis=-1, keepdims=True) / half
    m1 = jnp.sum(jnp.where(first, 0.0, sq), axis=-1, keepdims=True) / half
    r = jnp.where(first, lax.rsqrt(m0 + EPS), lax.rsqrt(m1 + EPS))
    return (jnp.concatenate([y * r * ssd_norm, _rms(o, attn_norm)], axis=-1),)


def fn_mla_prep(cq, ckv, kr, cos_t, sin_t, q_norm, kv_norm, rot):
    krr = kr * cos_t + jnp.dot(kr, rot, precision=HIGHEST, preferred_element_type=F32) * sin_t
    return _rms(cq, q_norm), _rms(ckv, kv_norm), krr


def final_fwdbwd(x1, ff, target, gate, fnorm, bgate, tile=256):
    B, S, D = x1.shape
    tile = min(tile, S)

    def body(x_ref, f_ref, t_ref, g_ref, n_ref, bg_ref, loss_ref, dx_ref, df_ref, dg_ref, dn_ref, dbg_ref):
        b, i = pl.program_id(0), pl.program_id(1)
        tgt = t_ref[0]

        def f(x, y, gate, fn, bg):
            yf = _rms(x + (gate + bg) * y, fn)
            return 0.5 * jnp.sum(jnp.mean(jnp.square(yf - tgt), axis=-1, keepdims=True), axis=0, keepdims=True)

        val, vjp = jax.vjp(f, x_ref[0], f_ref[0], g_ref[0], n_ref[...], bg_ref[...])
        dx, dff, dg, dn, dbg = vjp(jnp.ones((1, 1), F32))
        dx_ref[0] = dx
        df_ref[0] = dff
        lane_loss = jnp.broadcast_to(val, (1, LANE))

        @pl.when(i == 0)
        def _():
            dg_ref[0] = dg

        @pl.when(i > 0)
        def _():
            dg_ref[0] += dg

        @pl.when((i == 0) & (b == 0))
        def _():
            dn_ref[...] = dn
            dbg_ref[...] = dbg
            loss_ref[...] = lane_loss

        @pl.when((i > 0) | (b > 0))
        def _():
            dn_ref[...] += dn
            dbg_ref[...] += dbg
            loss_ref[...] += lane_loss

    tok = pl.BlockSpec((1, tile, D), lambda b, i: (b, i, 0))
    pb = pl.BlockSpec((1, 1, D), lambda b, i: (b, 0, 0))
    sh = pl.BlockSpec((1, D), lambda b, i: (0, 0))
    return pl.pallas_call(
        body,
        name="final_loss",
        grid=(B, S // tile),
        in_specs=[tok, tok, tok, pb, sh, sh],
        out_specs=[pl.BlockSpec((1, LANE), lambda b, i: (0, 0)), tok, tok, pb, sh, sh],
        out_shape=[
            jax.ShapeDtypeStruct((1, LANE), F32),
            jax.ShapeDtypeStruct((B, S, D), F32),
            jax.ShapeDtypeStruct((B, S, D), F32),
            jax.ShapeDtypeStruct((B, 1, D), F32),
            jax.ShapeDtypeStruct((1, D), F32),
            jax.ShapeDtypeStruct((1, D), F32),
        ],
        compiler_params=_cparams(2),
    )(x1, ff, target, gate, fnorm, bgate)


def matmul(name, a, b, ta=False, tb=False, out_dtype=F32, a_act=None):
    lead = None
    if a.ndim == 3 and not ta:
        lead = a.shape[:2]
    if a.ndim == 3:
        a = a.reshape(-1, a.shape[-1])
    if b.ndim == 3:
        b = b.reshape(-1, b.shape[-1])
    K, M = a.shape if ta else a.shape[::-1]
    N = b.shape[0] if tb else b.shape[1]
    assert (b.shape[1] if tb else b.shape[0]) == K, (name, a.shape, b.shape)
    tm = _pick(M, (1024, 512, 384, 256, 128))
    tn = _pick(N, (512, 384, 256, 128))
    tk = _pick(K, (512, 384, 256, 128))
    nk = K // tk
    dims = (((0 if ta else 1,), (1 if tb else 0,)), ((), ()))

    def body(a_ref, b_ref, o_ref, acc):
        k = pl.program_id(2)
        av = a_ref[...]
        if a_act is not None:
            av = a_act(av.astype(F32))
        part = lax.dot_general(av.astype(MXU_DTYPE), b_ref[...].astype(MXU_DTYPE), dims, preferred_element_type=F32)

        @pl.when(k == 0)
        def _():
            acc[...] = part

        @pl.when(k > 0)
        def _():
            acc[...] += part

        @pl.when(k == nk - 1)
        def _():
            o_ref[...] = acc[...].astype(o_ref.dtype)

    a_spec = pl.BlockSpec((tk, tm), lambda i, j, k: (k, i)) if ta else pl.BlockSpec((tm, tk), lambda i, j, k: (i, k))
    b_spec = pl.BlockSpec((tn, tk), lambda i, j, k: (j, k)) if tb else pl.BlockSpec((tk, tn), lambda i, j, k: (k, j))
    out = pl.pallas_call(
        body,
        name=name,
        grid=(M // tm, N // tn, nk),
        in_specs=[a_spec, b_spec],
        out_specs=pl.BlockSpec((tm, tn), lambda i, j, k: (i, j)),
        out_shape=jax.ShapeDtypeStruct((M, N), out_dtype),
        scratch_shapes=[pltpu.VMEM((tm, tn), F32)],
        compiler_params=_cparams(3),
    )(a, b)
    return out.reshape(*lead, N) if lead is not None else out


def _shift_down(u, d):
    if d == 0:
        return u
    t = lax.broadcasted_iota(jnp.int32, u.shape, 0)
    return jnp.where(t >= d, pltpu.roll(u, d, 0), 0.0)


def _shift_up(u, d):
    if d == 0:
        return u
    s = u.shape[0]
    t = lax.broadcasted_iota(jnp.int32, u.shape, 0)
    return jnp.where(t < s - d, pltpu.roll(u, s - d, 0), 0.0)


def _conv(u, w, b):
    k = w.shape[0]
    out = b + w[k - 1 : k, :] * u
    for j in range(k - 1):
        out = out + w[j : j + 1, :] * _shift_down(u, k - 1 - j)
    return out


def _conv_bwd(u, w, dc):
    k = w.shape[0]
    du = w[k - 1 : k, :] * dc
    dws = []
    for j in range(k - 1):
        du = du + w[j : j + 1, :] * _shift_up(dc, k - 1 - j)
        dws.append(jnp.sum(dc * _shift_down(u, k - 1 - j), axis=0, keepdims=True))
    dws.append(jnp.sum(dc * u, axis=0, keepdims=True))
    return du, jnp.concatenate(dws, axis=0), jnp.sum(dc, axis=0, keepdims=True)


def conv_silu_fwd(proj, w, b):
    B, S, _ = proj.shape
    k, c = w.shape
    blk0 = OFF_XBC // LANE

    def body(u_ref, w_ref, b_ref, o_ref):
        o_ref[0] = _silu(_conv(u_ref[0], w_ref[...], b_ref[...]))

    return pl.pallas_call(
        body,
        name="conv_silu_fwd",
        grid=(B, c // LANE),
        in_specs=[
            pl.BlockSpec((1, S, LANE), lambda bi, j: (bi, 0, blk0 + j)),
            pl.BlockSpec((k, LANE), lambda bi, j: (0, j)),
            pl.BlockSpec((1, LANE), lambda bi, j: (0, j)),
        ],
        out_specs=pl.BlockSpec((1, S, LANE), lambda bi, j: (bi, 0, j)),
        out_shape=jax.ShapeDtypeStruct((B, S, c), F32),
        compiler_params=_cparams(2),
    )(proj, w, b)


def conv_silu_bwd(proj, w, b, dout):
    B, S, _ = proj.shape
    k, c = w.shape
    blk0 = OFF_XBC // LANE

    def body(u_ref, w_ref, b_ref, d_ref, du_ref, dw_ref, db_ref):
        bi = pl.program_id(1)
        u, wv = u_ref[0], w_ref[...]
        cv = _conv(u, wv, b_ref[...])
        sg = jax.nn.sigmoid(cv)
        dc = d_ref[0] * (sg * (1.0 + cv * (1.0 - sg)))
        du, dw, db = _conv_bwd(u, wv, dc)
        du_ref[0] = du

        @pl.when(bi == 0)
        def _():
            dw_ref[...] = dw
            db_ref[...] = db

        @pl.when(bi > 0)
        def _():
            dw_ref[...] += dw
            db_ref[...] += db

    return pl.pallas_call(
        body,
        name="conv_silu_bwd",
        grid=(c // LANE, B),
        in_specs=[
            pl.BlockSpec((1, S, LANE), lambda j, bi: (bi, 0, blk0 + j)),
            pl.BlockSpec((k, LANE), lambda j, bi: (0, j)),
            pl.BlockSpec((1, LANE), lambda j, bi: (0, j)),
            pl.BlockSpec((1, S, LANE), lambda j, bi: (bi, 0, j)),
        ],
        out_specs=[
            pl.BlockSpec((1, S, LANE), lambda j, bi: (bi, 0, j)),
            pl.BlockSpec((k, LANE), lambda j, bi: (0, j)),
            pl.BlockSpec((1, LANE), lambda j, bi: (0, j)),
        ],
        out_shape=[
            jax.ShapeDtypeStruct((B, S, c), F32),
            jax.ShapeDtypeStruct((k, c), F32),
            jax.ShapeDtypeStruct((1, c), F32),
        ],
        compiler_params=_cparams(2),
    )(proj, w, b, dout)


def conv_glu_fwd(u, w, b):
    B, S, c2 = u.shape
    k = w.shape[0]

    def body(u_ref, w_ref, b_ref, o_ref):
        cv = _conv(u_ref[0], w_ref[...], b_ref[...])
        o_ref[0] = (_silu(cv[:, :LANE]) * cv[:, LANE:]).astype(o_ref.dtype)

    return pl.pallas_call(
        body,
        name="conv_glu_fwd",
        grid=(B, c2 // (2 * LANE)),
        in_specs=[
            pl.BlockSpec((1, S, 2 * LANE), lambda bi, j: (bi, 0, j)),
            pl.BlockSpec((k, 2 * LANE), lambda bi, j: (0, j)),
            pl.BlockSpec((1, 2 * LANE), lambda bi, j: (0, j)),
        ],
        out_specs=pl.BlockSpec((1, S, LANE), lambda bi, j: (bi, 0, j)),
        out_shape=jax.ShapeDtypeStruct((B, S, c2 // 2), MXU_DTYPE),
        compiler_params=_cparams(2),
    )(u, w, b)


def conv_glu_bwd(u, w, b, da):
    B, S, c2 = u.shape
    k = w.shape[0]

    def body(u_ref, w_ref, b_ref, d_ref, du_ref, dw_ref, db_ref):
        bi = pl.program_id(1)
        uv, wv = u_ref[0], w_ref[...]
        cv = _conv(uv, wv, b_ref[...])
        gate, val = cv[:, :LANE], cv[:, LANE:]
        sg = jax.nn.sigmoid(gate)
        dav = d_ref[0].astype(F32)
        dc = jnp.concatenate([dav * val * (sg * (1.0 + gate * (1.0 - sg))), dav * gate * sg], axis=1)
        du, dw, db = _conv_bwd(uv, wv, dc)
        du_ref[0] = du

        @pl.when(bi == 0)
        def _():
            dw_ref[...] = dw
            db_ref[...] = db

        @pl.when(bi > 0)
        def _():
            dw_ref[...] += dw
            db_ref[...] += db

    return pl.pallas_call(
        body,
        name="conv_glu_bwd",
        grid=(c2 // (2 * LANE), B),
        in_specs=[
            pl.BlockSpec((1, S, 2 * LANE), lambda j, bi: (bi, 0, j)),
            pl.BlockSpec((k, 2 * LANE), lambda j, bi: (0, j)),
            pl.BlockSpec((1, 2 * LANE), lambda j, bi: (0, j)),
            pl.BlockSpec((1, S, LANE), lambda j, bi: (bi, 0, j)),
        ],
        out_specs=[
            pl.BlockSpec((1, S, 2 * LANE), lambda j, bi: (bi, 0, j)),
            pl.BlockSpec((k, 2 * LANE), lambda j, bi: (0, j)),
            pl.BlockSpec((1, 2 * LANE), lambda j, bi: (0, j)),
        ],
        out_shape=[
            jax.ShapeDtypeStruct((B, S, c2), F32),
            jax.ShapeDtypeStruct((k, c2), F32),
            jax.ShapeDtypeStruct((1, c2), F32),
        ],
        compiler_params=_cparams(2),
    )(u, w, b, da)


N_PAIR = SSD_HEADS // SSD_GROUPS // 2


def _ssd_chunk(g, prev, xs, bm, cm, dtraw, dt_bias, a_log, d_skip):
    L = CHUNK
    lane = lax.broadcasted_iota(jnp.int32, (1, LANE), 1)
    sub = lax.broadcasted_iota(jnp.int32, (LANE, 1), 0)
    row = lax.broadcasted_iota(jnp.int32, (L, L), 0)
    col = lax.broadcasted_iota(jnp.int32, (L, L), 1)
    tri = (row >= col).astype(F32)
    first = lane < SSD_HEAD_DIM

    dt = _softplus(dtraw + dt_bias)
    a = dt * (-jnp.exp(a_log))
    acs = jnp.dot(tri, a, precision=HIGHEST, preferred_element_type=F32)
    acs_t = acs.T
    a_end = jnp.sum(a, axis=0, keepdims=True)
    cb = lax.dot_general(cm.astype(MXU_DTYPE), bm.astype(MXU_DTYPE), (((1,), (1,)), ((), ())), preferred_element_type=F32)

    def lane_of(v, h):
        return jnp.sum(jnp.where(lane == h, v, 0.0), axis=1, keepdims=True)

    def expand(v, ha):
        return jnp.where(first, lane_of(v, ha), lane_of(v, ha + 1))

    ys, news = [], []
    for j in range(N_PAIR):
        ha = g * (SSD_HEADS // SSD_GROUPS) + 2 * j
        x = xs[j]
        dt_e, acs_e, end_e = expand(dt, ha), expand(acs, ha), expand(a_end, ha)
        xdt = x * dt_e
        y = jnp.dot(cm.astype(MXU_DTYPE), prev[j].astype(MXU_DTYPE), preferred_element_type=F32) * jnp.exp(acs_e)
        st = lax.dot_general(
            bm.astype(MXU_DTYPE), (xdt * jnp.exp(end_e - acs_e)).astype(MXU_DTYPE), (((0,), (0,)), ((), ())), preferred_element_type=F32
        )
        news.append(prev[j] * jnp.exp(end_e) + st)
        for hh in range(2):
            h = ha + hh
            seg = lane_of(acs, h) - jnp.sum(jnp.where(sub == h, acs_t, 0.0), axis=0, keepdims=True)
            decay = jnp.exp(jnp.where(row >= col, seg, -jnp.inf))
            xh = jnp.where(first if hh == 0 else jnp.logical_not(first), xdt, 0.0)
            y = y + jnp.dot((cb * decay).astype(MXU_DTYPE), xh.astype(MXU_DTYPE), preferred_element_type=F32)
        ys.append(y + x * expand(d_skip, ha))
    return ys, news


def _ssd_specs(nc, order):
    gw = D_SSD // SSD_GROUPS
    bm0, cm0 = D_SSD // SSD_STATE, D_SSD // SSD_STATE + SSD_GROUPS
    par = pl.BlockSpec((1, LANE), lambda *ids: (0, 0))
    return [
        pl.BlockSpec((1, CHUNK, gw), lambda *ids: (order(*ids)[0], order(*ids)[2], order(*ids)[1])),
        pl.BlockSpec((1, CHUNK, SSD_STATE), lambda *ids: (order(*ids)[0], order(*ids)[2], bm0 + order(*ids)[1])),
        pl.BlockSpec((1, CHUNK, SSD_STATE), lambda *ids: (order(*ids)[0], order(*ids)[2], cm0 + order(*ids)[1])),
        pl.BlockSpec((1, CHUNK, LANE), lambda *ids: (order(*ids)[0], order(*ids)[2], OFF_DT // LANE)),
        par,
        par,
        par,
    ]


def ssd_fwd(xbc_act, proj, dt_bias, a_log, d_skip):
    B, S, _ = xbc_act.shape
    nc = S // CHUNK
    gw = D_SSD // SSD_GROUPS

    def body(xs_ref, bm_ref, cm_ref, dt_ref, db_ref, al_ref, ds_ref, y_ref, st_ref, state):
        g, c = pl.program_id(1), pl.program_id(2)

        @pl.when(c == 0)
        def _():
            state[...] = jnp.zeros_like(state)

        prev = [state[j] for j in range(N_PAIR)]
        for j in range(N_PAIR):
            st_ref[0, 0, 0, j] = prev[j]
        xs = [xs_ref[0, :, j * LANE : (j + 1) * LANE] for j in range(N_PAIR)]
        ys, news = _ssd_chunk(g, prev, xs, bm_ref[0], cm_ref[0], dt_ref[0], db_ref[...], al_ref[...], ds_ref[...])
        for j in range(N_PAIR):
            y_ref[0, :, j * LANE : (j + 1) * LANE] = ys[j]
            state[j] = news[j]

    return pl.pallas_call(
        body,
        name="ssd_fwd",
        grid=(B, SSD_GROUPS, nc),
        in_specs=_ssd_specs(nc, lambda b, g, c: (b, g, c)),
        out_specs=[
            pl.BlockSpec((1, CHUNK, gw), lambda b, g, c: (b, c, g)),
            pl.BlockSpec((1, 1, 1, N_PAIR, SSD_STATE, LANE), lambda b, g, c: (b, g, c, 0, 0, 0)),
        ],
        out_shape=[
            jax.ShapeDtypeStruct((B, S, D_SSD), F32),
            jax.ShapeDtypeStruct((B, SSD_GROUPS, nc, N_PAIR, SSD_STATE, LANE), F32),
        ],
        scratch_shapes=[pltpu.VMEM((N_PAIR, SSD_STATE, LANE), F32)],
        compiler_params=_cparams(3),
    )(xbc_act, xbc_act, xbc_act, proj, dt_bias, a_log, d_skip)


def ssd_bwd(xbc_act, proj, dt_bias, a_log, d_skip, states, dy):
    B, S, _ = xbc_act.shape
    nc = S // CHUNK
    gw = D_SSD // SSD_GROUPS
    order = lambda b, cr, g: (b, g, nc - 1 - cr)

    def body(xs_ref, bm_ref, cm_ref, dt_ref, db_ref, al_ref, ds_ref, st_ref, dy_ref,
             dxs_ref, dbm_ref, dcm_ref, ddt_ref, ddb_ref, dal_ref, dds_ref, dstate):
        b, cr, g = pl.program_id(0), pl.program_id(1), pl.program_id(2)

        @pl.when(cr == 0)
        def _():
            dstate[g] = jnp.zeros((N_PAIR, SSD_STATE, LANE), F32)

        prev = [st_ref[0, 0, 0, j] for j in range(N_PAIR)]
        xs = [xs_ref[0, :, j * LANE : (j + 1) * LANE] for j in range(N_PAIR)]

        def f(prev, xs, bm, cm, dtraw, dt_bias, a_log, d_skip):
            return _ssd_chunk(g, prev, xs, bm, cm, dtraw, dt_bias, a_log, d_skip)

        _, vjp = jax.vjp(f, prev, xs, bm_ref[0], cm_ref[0], dt_ref[0], db_ref[...], al_ref[...], ds_ref[...])
        ct_y = [dy_ref[0, :, j * LANE : (j + 1) * LANE] for j in range(N_PAIR)]
        ct_s = [dstate[g, j] for j in range(N_PAIR)]
        dprev, dxs, dbm, dcm, ddt, ddb, dal, dds = vjp((ct_y, ct_s))
        for j in range(N_PAIR):
            dstate[g, j] = dprev[j]
            dxs_ref[0, :, j * LANE : (j + 1) * LANE] = dxs[j]
        dbm_ref[0] = dbm
        dcm_ref[0] = dcm

        @pl.when(g == 0)
        def _():
            ddt_ref[0] = ddt

        @pl.when(g > 0)
        def _():
            ddt_ref[0] += ddt

        first = (b == 0) & (cr == 0) & (g == 0)

        @pl.when(first)
        def _():
            ddb_ref[...] = ddb
            dal_ref[...] = dal
            dds_ref[...] = dds

        @pl.when(jnp.logical_not(first))
        def _():
            ddb_ref[...] += ddb
            dal_ref[...] += dal
            dds_ref[...] += dds

    par = pl.BlockSpec((1, LANE), lambda *ids: (0, 0))
    in_specs = _ssd_specs(nc, order) + [
        pl.BlockSpec((1, 1, 1, N_PAIR, SSD_STATE, LANE), lambda b, cr, g: (b, g, nc - 1 - cr, 0, 0, 0)),
        pl.BlockSpec((1, CHUNK, gw), lambda b, cr, g: (b, nc - 1 - cr, g)),
    ]
    return pl.pallas_call(
        body,
        name="ssd_bwd",
        grid=(B, nc, SSD_GROUPS),
        in_specs=in_specs,
        out_specs=[
            pl.BlockSpec((1, CHUNK, gw), lambda b, cr, g: (b, nc - 1 - cr, g)),
            pl.BlockSpec((1, CHUNK, SSD_STATE), lambda b, cr, g: (b, nc - 1 - cr, g)),
            pl.BlockSpec((1, CHUNK, SSD_STATE), lambda b, cr, g: (b, nc - 1 - cr, g)),
            pl.BlockSpec((1, CHUNK, LANE), lambda b, cr, g: (b, nc - 1 - cr, 0)),
            par,
            par,
            par,
        ],
        out_shape=[
            jax.ShapeDtypeStruct((B, S, D_SSD), F32),
            jax.ShapeDtypeStruct((B, S, SSD_GROUPS * SSD_STATE), F32),
            jax.ShapeDtypeStruct((B, S, SSD_GROUPS * SSD_STATE), F32),
            jax.ShapeDtypeStruct((B, S, LANE), F32),
            jax.ShapeDtypeStruct((1, LANE), F32),
            jax.ShapeDtypeStruct((1, LANE), F32),
            jax.ShapeDtypeStruct((1, LANE), F32),
        ],
        scratch_shapes=[pltpu.VMEM((SSD_GROUPS, N_PAIR, SSD_STATE, LANE), F32)],
        compiler_params=_cparams(3),
    )(xbc_act, xbc_act, xbc_act, proj, dt_bias, a_log, d_skip, states, dy)


ATT_SCALE = (QK_NOPE + QK_ROPE) ** -0.5
ATT_TILE = 512


def _rope_q(q, cos_t, sin_t, rot):
    qr = q[:, LANE:]
    qr = qr * cos_t + jnp.dot(qr, rot, precision=HIGHEST, preferred_element_type=F32) * sin_t
    return jnp.concatenate([q[:, :LANE], qr], axis=1)


def _scores(qs, kn, kr, i, j, t):
    k = jnp.concatenate([kn, kr], axis=1).astype(MXU_DTYPE)
    s = lax.dot_general(qs, k, (((1,), (1,)), ((), ())), preferred_element_type=F32) * ATT_SCALE
    row = i * t + lax.broadcasted_iota(jnp.int32, s.shape, 0)
    col = j * t + lax.broadcasted_iota(jnp.int32, s.shape, 1)
    return jnp.where(row >= col, s, -jnp.inf), k


def attn_fwd(q, kv, krr, cos_t, sin_t, rot):
    B, S, _ = q.shape
    t = min(ATT_TILE, S)
    n = S // t

    def body(q_ref, c_ref, s_ref, rot_ref, k_ref, kr_ref, o_ref, lse_ref, qs, m_s, l_s, acc):
        i, j = pl.program_id(2), pl.program_id(3)

        @pl.when(j == 0)
        def _():
            qs[...] = _rope_q(q_ref[0], c_ref[0], s_ref[0], rot_ref[...]).astype(MXU_DTYPE)
            m_s[...] = jnp.full_like(m_s, -jnp.inf)
            l_s[...] = jnp.zeros_like(l_s)
            acc[...] = jnp.zeros_like(acc)

        @pl.when(j <= i)
        def _():
            kvv = k_ref[0]
            s, _ = _scores(qs[...], kvv[:, :LANE], kr_ref[0], i, j, t)
            m_new = jnp.maximum(m_s[...], jnp.max(s, axis=1, keepdims=True))
            alpha = jnp.exp(m_s[...] - m_new)
            p = jnp.exp(s - m_new)
            l_s[...] = alpha * l_s[...] + jnp.sum(p, axis=1, keepdims=True)
            acc[...] = alpha * acc[...] + jnp.dot(p.astype(MXU_DTYPE), kvv[:, LANE:].astype(MXU_DTYPE), preferred_element_type=F32)
            m_s[...] = m_new

        @pl.when(j == i)
        def _():
            o_ref[0] = acc[...] / l_s[...]
            lse_ref[0, 0] = m_s[...] + jnp.log(l_s[...])

    kv_j = lambda b, h, i, j: jnp.minimum(j, i)
    return pl.pallas_call(
        body,
        name="attn_fwd",
        grid=(B, MLA_HEADS, n, n),
        in_specs=[
            pl.BlockSpec((1, t, Q_HEAD_PAD), lambda b, h, i, j: (b, i, h)),
            pl.BlockSpec((1, t, LANE), lambda b, h, i, j: (b, i, 0)),
            pl.BlockSpec((1, t, LANE), lambda b, h, i, j: (b, i, 0)),
            pl.BlockSpec((LANE, LANE), lambda b, h, i, j: (0, 0)),
            pl.BlockSpec((1, t, 2 * LANE), lambda b, h, i, j: (b, kv_j(b, h, i, j), h)),
            pl.BlockSpec((1, t, LANE), lambda b, h, i, j: (b, kv_j(b, h, i, j), 0)),
        ],
        out_specs=[
            pl.BlockSpec((1, t, V_DIM), lambda b, h, i, j: (b, i, h)),
            pl.BlockSpec((1, 1, t, 1), lambda b, h, i, j: (b, h, i, 0)),
        ],
        out_shape=[jax.ShapeDtypeStruct((B, S, D_ATT), F32), jax.ShapeDtypeStruct((B, MLA_HEADS, S, 1), F32)],
        scratch_shapes=[
            pltpu.VMEM((t, Q_HEAD_PAD), MXU_DTYPE),
            pltpu.VMEM((t, 1), F32),
            pltpu.VMEM((t, 1), F32),
            pltpu.VMEM((t, V_DIM), F32),
        ],
        compiler_params=_cparams(4),
    )(q, cos_t, sin_t, rot, kv, krr)


def attn_bwd_dq(q, kv, krr, cos_t, sin_t, rot, rot_t, o, lse, do):
    B, S, _ = q.shape
    t = min(ATT_TILE, S)
    n = S // t

    def body(q_ref, c_ref, s_ref, rot_ref, rott_ref, k_ref, kr_ref, o_ref, lse_ref, do_ref, dq_ref, qs, delta, acc):
        i, j = pl.program_id(2), pl.program_id(3)

        @pl.when(j == 0)
        def _():
            qs[...] = _rope_q(q_ref[0], c_ref[0], s_ref[0], rot_ref[...]).astype(MXU_DTYPE)
            delta[...] = jnp.sum(o_ref[0] * do_ref[0], axis=1, keepdims=True)
            acc[...] = jnp.zeros_like(acc)

        @pl.when(j <= i)
        def _():
            kvv = k_ref[0]
            s, k = _scores(qs[...], kvv[:, :LANE], kr_ref[0], i, j, t)
            p = jnp.exp(s - lse_ref[0, 0])
            dp = lax.dot_general(
                do_ref[0].astype(MXU_DTYPE), kvv[:, LANE:].astype(MXU_DTYPE), (((1,), (1,)), ((), ())), preferred_element_type=F32
            )
            ds = p * (dp - delta[...]) * ATT_SCALE
            acc[...] += jnp.dot(ds.astype(MXU_DTYPE), k, preferred_element_type=F32)

        @pl.when(j == i)
        def _():
            a = acc[...]
            r = a[:, LANE:]
            dr = r * c_ref[0] + jnp.dot(r * s_ref[0], rott_ref[...], precision=HIGHEST, preferred_element_type=F32)
            dq_ref[0] = jnp.concatenate([a[:, :LANE], dr], axis=1)

    kv_j = lambda b, h, i, j: jnp.minimum(j, i)
    return pl.pallas_call(
        body,
        name="attn_bwd_dq",
        grid=(B, MLA_HEADS, n, n),
        in_specs=[
            pl.BlockSpec((1, t, Q_HEAD_PAD), lambda b, h, i, j: (b, i, h)),
            pl.BlockSpec((1, t, LANE), lambda b, h, i, j: (b, i, 0)),
            pl.BlockSpec((1, t, LANE), lambda b, h, i, j: (b, i, 0)),
            pl.BlockSpec((LANE, LANE), lambda b, h, i, j: (0, 0)),
            pl.BlockSpec((LANE, LANE), lambda b, h, i, j: (0, 0)),
            pl.BlockSpec((1, t, 2 * LANE), lambda b, h, i, j: (b, kv_j(b, h, i, j), h)),
            pl.BlockSpec((1, t, LANE), lambda b, h, i, j: (b, kv_j(b, h, i, j), 0)),
            pl.BlockSpec((1, t, V_DIM), lambda b, h, i, j: (b, i, h)),
            pl.BlockSpec((1, 1, t, 1), lambda b, h, i, j: (b, h, i, 0)),
            pl.BlockSpec((1, t, V_DIM), lambda b, h, i, j: (b, i, h)),
        ],
        out_specs=pl.BlockSpec((1, t, Q_HEAD_PAD), lambda b, h, i, j: (b, i, h)),
        out_shape=jax.ShapeDtypeStruct(q.shape, F32),
        scratch_shapes=[pltpu.VMEM((t, Q_HEAD_PAD), MXU_DTYPE), pltpu.VMEM((t, 1), F32), pltpu.VMEM((t, Q_HEAD_PAD), F32)],
        compiler_params=_cparams(4),
    )(q, cos_t, sin_t, rot, rot_t, kv, krr, o, lse, do)


def attn_bwd_dkv(q, kv, krr, cos_t, sin_t, rot, o, lse, do):
    B, S, _ = q.shape
    t = min(ATT_TILE, S)
    n = S // t

    def body(q_ref, c_ref, s_ref, rot_ref, k_ref, kr_ref, o_ref, lse_ref, do_ref, dkv_ref, dkr_ref, dk_acc, dv_acc):
        j, h, i = pl.program_id(1), pl.program_id(2), pl.program_id(3)

        @pl.when(i == 0)
        def _():
            dk_acc[...] = jnp.zeros_like(dk_acc)
            dv_acc[...] = jnp.zeros_like(dv_acc)

        @pl.when(i >= j)
        def _():
            qs = _rope_q(q_ref[0], c_ref[0], s_ref[0], rot_ref[...]).astype(MXU_DTYPE)
            kvv = k_ref[0]
            dov = do_ref[0]
            s, _ = _scores(qs, kvv[:, :LANE], kr_ref[0], i, j, t)
            p = jnp.exp(s - lse_ref[0, 0])
            dv_acc[...] += lax.dot_general(p.astype(MXU_DTYPE), dov.astype(MXU_DTYPE), (((0,), (0,)), ((), ())), preferred_element_type=F32)
            dp = lax.dot_general(
                dov.astype(MXU_DTYPE), kvv[:, LANE:].astype(MXU_DTYPE), (((1,), (1,)), ((), ())), preferred_element_type=F32
            )
            delta = jnp.sum(o_ref[0] * dov, axis=1, keepdims=True)
            ds = p * (dp - delta) * ATT_SCALE
            dk_acc[...] += lax.dot_general(ds.astype(MXU_DTYPE), qs, (((0,), (0,)), ((), ())), preferred_element_type=F32)

        @pl.when(i == n - 1)
        def _():
            dk = dk_acc[...]
            dkv_ref[0] = jnp.concatenate([dk[:, :LANE], dv_acc[...]], axis=1)

            @pl.when(h == 0)
            def _():
                dkr_ref[0] = dk[:, LANE:]

            @pl.when(h > 0)
            def _():
                dkr_ref[0] += dk[:, LANE:]

    q_i = lambda b, j, h, i: jnp.maximum(i, j)
    return pl.pallas_call(
        body,
        name="attn_bwd_dkv",
        grid=(B, n, MLA_HEADS, n),
        in_specs=[
            pl.BlockSpec((1, t, Q_HEAD_PAD), lambda b, j, h, i: (b, q_i(b, j, h, i), h)),
            pl.BlockSpec((1, t, LANE), lambda b, j, h, i: (b, q_i(b, j, h, i), 0)),
            pl.BlockSpec((1, t, LANE), lambda b, j, h, i: (b, q_i(b, j, h, i), 0)),
            pl.BlockSpec((LANE, LANE), lambda b, j, h, i: (0, 0)),
            pl.BlockSpec((1, t, 2 * LANE), lambda b, j, h, i: (b, j, h)),
            pl.BlockSpec((1, t, LANE), lambda b, j, h, i: (b, j, 0)),
            pl.BlockSpec((1, t, V_DIM), lambda b, j, h, i: (b, q_i(b, j, h, i), h)),
            pl.BlockSpec((1, 1, t, 1), lambda b, j, h, i: (b, h, q_i(b, j, h, i), 0)),
            pl.BlockSpec((1, t, V_DIM), lambda b, j, h, i: (b, q_i(b, j, h, i), h)),
        ],
        out_specs=[
            pl.BlockSpec((1, t, 2 * LANE), lambda b, j, h, i: (b, j, h)),
            pl.BlockSpec((1, t, LANE), lambda b, j, h, i: (b, j, 0)),
        ],
        out_shape=[jax.ShapeDtypeStruct(kv.shape, F32), jax.ShapeDtypeStruct(krr.shape, F32)],
        scratch_shapes=[pltpu.VMEM((t, Q_HEAD_PAD), F32), pltpu.VMEM((t, V_DIM), F32)],
        compiler_params=_cparams(4),
    )(q, cos_t, sin_t, rot, kv, krr, o, lse, do)


_IN_SPLITS = np.cumsum([D_SSD, D_XBC, SSD_HEADS, Q_RANK, KV_RANK]).tolist()


def _pad_last(t, n):
    return jnp.pad(t, [(0, 0)] * (t.ndim - 1) + [(0, n - t.shape[-1])])


def win_to_kernel(w):
    z, xbc, dt, cq, ckv, kr = jnp.split(w, _IN_SPLITS, axis=-1)
    return jnp.concatenate([z, xbc, ckv, _pad_last(kr, LANE), _pad_last(dt, LANE), cq], axis=-1)


def win_from_kernel(g):
    z, xbc = g[..., :OFF_XBC], g[..., OFF_XBC:OFF_CKV]
    ckv, kr = g[..., OFF_CKV:OFF_KR], g[..., OFF_KR : OFF_KR + QK_ROPE]
    dt, cq = g[..., OFF_DT : OFF_DT + SSD_HEADS], g[..., OFF_CQ:]
    return jnp.concatenate([z, xbc, dt, cq, ckv, kr], axis=-1)


def wuq_to_kernel(w):
    w = w.reshape(*w.shape[:-1], MLA_HEADS, QK_NOPE + QK_ROPE)
    return _pad_last(w, Q_HEAD_PAD).reshape(*w.shape[:-2], MLA_HEADS * Q_HEAD_PAD)


def wuq_from_kernel(g):
    g = g.reshape(*g.shape[:-1], MLA_HEADS, Q_HEAD_PAD)[..., : QK_NOPE + QK_ROPE]
    return g.reshape(*g.shape[:-2], MLA_HEADS * (QK_NOPE + QK_ROPE))


def glu_to_kernel(w):
    lead = w.shape[:-1]
    w = w.reshape(*lead, 2, D_FF // LANE, LANE)
    return jnp.swapaxes(w, -3, -2).reshape(*lead, 2 * D_FF)


def glu_from_kernel(g):
    lead = g.shape[:-1]
    g = g.reshape(*lead, D_FF // LANE, 2, LANE)
    return jnp.swapaxes(g, -3, -2).reshape(*lead, 2 * D_FF)


def _lane_pad_row(v):
    return _pad_last(v, LANE)[None, :]


def rope_tables(positions):
    inv_freq = jnp.asarray(1.0 / (ROPE_BASE ** (np.arange(0, QK_ROPE, 2, dtype=np.float32) / QK_ROPE)))
    ang = positions.astype(F32)[..., None] * inv_freq
    cos, sin = jnp.cos(ang), jnp.sin(ang)
    zeros = jnp.zeros(cos.shape[:-1] + (LANE - QK_ROPE,), F32)
    rot = np.zeros((LANE, LANE), np.float32)
    half = QK_ROPE // 2
    for j in range(half):
        rot[j + half, j] = -1.0
        rot[j, j + half] = 1.0
    return jnp.concatenate([cos, cos, zeros], -1), jnp.concatenate([sin, sin, zeros], -1), jnp.asarray(rot), jnp.asarray(rot.T)


BIG = ("w_in", "w_uq", "w_ukv", "w_out", "w_up", "w_down")


def local_step(x, target, positions, mod_raw, W):
    B, S, D = x.shape
    cos_t, sin_t, rot, rot_t = rope_tables(positions)
    row = lambda v: v.reshape(1, -1)
    Ls = []
    for l in range(DEPTH):
        mods = [mod_raw[l, :, k][:, None, :] for k in range(6)]
        bias = [row(W["b_ada"][l, k * D : (k + 1) * D]) for k in range(6)]
        Ls.append(
            dict(
                mods=mods,
                bias=bias,
                w_in=win_to_kernel(W["w_in"][l]),
                w_uq=wuq_to_kernel(W["w_uq"][l]),
                w_ukv=W["w_ukv"][l],
                w_out=W["w_out"][l],
                w_up=glu_to_kernel(W["w_up"][l]),
                w_down=W["w_down"][l],
                conv_w=W["conv_w"][l],
                conv_b=row(W["conv_b"][l]),
                conv_ff_w=glu_to_kernel(W["conv_ff_w"][l]),
                conv_ff_b=row(glu_to_kernel(W["conv_ff_b"][l])),
                dt_bias=_lane_pad_row(W["dt_bias"][l]),
                a_log=_lane_pad_row(W["a_log"][l]),
                d_skip=_lane_pad_row(W["d_skip"][l]),
                norm_mix=row(W["norm_mix"][l]),
                ssd_norm=row(W["ssd_norm"][l]),
                q_norm=row(W["q_norm"][l]),
                kv_norm=row(W["kv_norm"][l]),
                attn_norm=row(W["attn_norm"][l]),
                norm_mlp=row(W["norm_mlp"][l]),
            )
        )
    fnorm = row(W["final_norm"])

    p0 = Ls[0]
    (h1,) = rowwise_fwd(
        "prenorm_fwd", fn_prenorm, [x], [p0["mods"][1], p0["mods"][0]], [p0["norm_mix"], p0["bias"][1], p0["bias"][0]], [(D, MXU_DTYPE)]
    )
    xin = x
    for l, p in enumerate(Ls):
        s = p["saved"] = dict(xin=xin, h1=h1)
        s["proj"] = proj = matmul(f"w_in_fwd{l}", h1, p["w_in"])
        s["xbc_act"] = xbc_act = conv_silu_fwd(proj, p["conv_w"], p["conv_b"])
        s["yscan"], s["states"] = ssd_fwd(xbc_act, proj, p["dt_bias"], p["a_log"], p["d_skip"])
        mla_in = [Col(proj, Q_RANK, OFF_CQ // Q_RANK), Col(proj, KV_RANK, OFF_CKV // KV_RANK), Col(proj, LANE, OFF_KR // LANE), cos_t, sin_t]
        s["cqn"], s["ckvn"], s["krr"] = rowwise_fwd(
            f"mla_prep_fwd{l}", fn_mla_prep, mla_in, [], [p["q_norm"], p["kv_norm"], rot], [(Q_RANK, MXU_DTYPE), (KV_RANK, MXU_DTYPE), (LANE, F32)]
        )
        s["q"] = matmul(f"w_uq_fwd{l}", s["cqn"], p["w_uq"])
        s["kv"] = matmul(f"w_ukv_fwd{l}", s["ckvn"], p["w_ukv"])
        s["o"], s["lse"] = attn_fwd(s["q"], s["kv"], s["krr"], cos_t, sin_t, rot)
        (s["ycat"],) = rowwise_fwd(
            f"mix_fwd{l}", fn_mix, [s["yscan"], Col(proj, D_SSD, 0), s["o"]], [], [p["ssd_norm"], p["attn_norm"]], [(D_SSD + D_ATT, MXU_DTYPE)]
        )
        s["ymix"] = matmul(f"w_out_fwd{l}", s["ycat"], p["w_out"])
        s["x1"], s["h2"] = rowwise_fwd(
            f"mid_fwd{l}",
            fn_resid_prenorm,
            [xin, s["ymix"]],
            [p["mods"][2], p["mods"][4], p["mods"][3]],
            [p["norm_mlp"], p["bias"][2], p["bias"][4], p["bias"][3]],
            [(D, F32), (D, MXU_DTYPE)],
        )
        s["u"] = matmul(f"w_up_fwd{l}", s["h2"], p["w_up"])
        s["a"] = conv_glu_fwd(s["u"], p["conv_ff_w"], p["conv_ff_b"])
        s["ff"] = matmul(f"w_down_fwd{l}", s["a"], p["w_down"])
        if l + 1 < DEPTH:
            n = Ls[l + 1]
            xin, h1 = rowwise_fwd(
                f"join_fwd{l}",
                fn_resid_prenorm,
                [s["x1"], s["ff"]],
                [p["mods"][5], n["mods"][1], n["mods"][0]],
                [n["norm_mix"], p["bias"][5], n["bias"][1], n["bias"][0]],
                [(D, F32), (D, MXU_DTYPE)],
            )

    G = {k: [None] * DEPTH for k in W if k not in ("final_norm", "w_ada")}
    dmod = [[None] * 6 for _ in range(DEPTH)]
    dbias = [[None] * 6 for _ in range(DEPTH)]
    last = Ls[-1]
    sl = last["saved"]
    loss, dx1, dff, dmod[-1][5], G["final_norm"], dbias[-1][5] = final_fwdbwd(
        sl["x1"], sl["ff"], target, last["mods"][5], fnorm, last["bias"][5]
    )
    grad_x = None
    for l in reversed(range(DEPTH)):
        p = Ls[l]
        s = p["saved"]
        da = matmul(f"w_down_dgrad{l}", dff, p["w_down"], tb=True)
        G["w_down"][l] = matmul(f"w_down_wgrad{l}", s["a"], dff, ta=True)
        du, dcw, dcb = conv_glu_bwd(s["u"], p["conv_ff_w"], p["conv_ff_b"], da)
        G["conv_ff_w"][l], G["conv_ff_b"][l] = glu_from_kernel(dcw), glu_from_kernel(dcb)[0]
        dh2 = matmul(f"w_up_dgrad{l}", du, p["w_up"], tb=True)
        G["w_up"][l] = glu_from_kernel(matmul(f"w_up_wgrad{l}", s["h2"], du, ta=True))
        dxb, dymix, dmod[l][2], dmod[l][4], dmod[l][3], G["norm_mlp"][l], dbias[l][2], dbias[l][4], dbias[l][3] = rowwise_bwd(
            f"mid_bwd{l}",
            fn_resid_prenorm,
            [s["xin"], s["ymix"]],
            [p["mods"][2], p["mods"][4], p["mods"][3]],
            [p["norm_mlp"], p["bias"][2], p["bias"][4], p["bias"][3]],
            [dx1, dh2],
            [True, True],
            [True] * 4,
        )
        dycat = matmul(f"w_out_dgrad{l}", dymix, p["w_out"], tb=True)
        G["w_out"][l] = matmul(f"w_out_wgrad{l}", s["ycat"], dymix, ta=True)
        dyscan, dz, do, G["ssd_norm"][l], G["attn_norm"][l] = rowwise_bwd(
            f"mix_bwd{l}", fn_mix, [s["yscan"], Col(s["proj"], D_SSD, 0), s["o"]], [], [p["ssd_norm"], p["attn_norm"]], [dycat], [True] * 3, [True] * 2
        )
        dq = attn_bwd_dq(s["q"], s["kv"], s["krr"], cos_t, sin_t, rot, rot_t, s["o"], s["lse"], do)
        dkv, dkrr = attn_bwd_dkv(s["q"], s["kv"], s["krr"], cos_t, sin_t, rot, s["o"], s["lse"], do)
        dcqn = matmul(f"w_uq_dgrad{l}", dq, p["w_uq"], tb=True)
        G["w_uq"][l] = wuq_from_kernel(matmul(f"w_uq_wgrad{l}", s["cqn"], dq, ta=True))
        dckvn = matmul(f"w_ukv_dgrad{l}", dkv, p["w_ukv"], tb=True)
        G["w_ukv"][l] = matmul(f"w_ukv_wgrad{l}", s["ckvn"], dkv, ta=True)
        proj = s["proj"]
        mla_in = [Col(proj, Q_RANK, OFF_CQ // Q_RANK), Col(proj, KV_RANK, OFF_CKV // KV_RANK), Col(proj, LANE, OFF_KR // LANE), cos_t, sin_t]
        dcq, dckv, dkr, G["q_norm"][l], G["kv_norm"][l] = rowwise_bwd(
            f"mla_prep_bwd{l}",
            fn_mla_prep,
            mla_in,
            [],
            [p["q_norm"], p["kv_norm"], rot],
            [dcqn, dckvn, dkrr],
            [True, True, True, False, False],
            [True, True, False],
        )
        dxs, dbm, dcm, ddt, ddb, dal, dds = ssd_bwd(s["xbc_act"], proj, p["dt_bias"], p["a_log"], p["d_skip"], s["states"], dyscan)
        G["dt_bias"][l], G["a_log"][l], G["d_skip"][l] = ddb[0, :SSD_HEADS], dal[0, :SSD_HEADS], dds[0, :SSD_HEADS]
        dxbc, G["conv_w"][l], dcb = conv_silu_bwd(proj, p["conv_w"], p["conv_b"], jnp.concatenate([dxs, dbm, dcm], axis=-1))
        G["conv_b"][l] = dcb[0]
        dproj = jnp.concatenate([dz, dxbc, dckv, dkr, ddt, dcq], axis=-1)
        dh1 = matmul(f"w_in_dgrad{l}", dproj, p["w_in"], tb=True)
        G["w_in"][l] = win_from_kernel(matmul(f"w_in_wgrad{l}", s["h1"], dproj, ta=True))
        if l > 0:
            q = Ls[l - 1]
            sq = q["saved"]
            dx1, dff, dmod[l - 1][5], dmod[l][1], dmod[l][0], G["norm_mix"][l], dbias[l - 1][5], dbias[l][1], dbias[l][0] = rowwise_bwd(
                f"join_bwd{l - 1}",
                fn_resid_prenorm,
                [sq["x1"], sq["ff"]],
                [q["mods"][5], p["mods"][1], p["mods"][0]],
                [p["norm_mix"], q["bias"][5], p["bias"][1], p["bias"][0]],
                [dxb, dh1],
                [True, True],
                [True] * 4,
            )
        else:
            grad_x, dmod[0][1], dmod[0][0], G["norm_mix"][0], dbias[0][1], dbias[0][0] = rowwise_bwd(
                "prenorm_bwd",
                fn_prenorm,
                [x],
                [p["mods"][1], p["mods"][0]],
                [p["norm_mix"], p["bias"][1], p["bias"][0]],
                [dh1],
                [True],
                [True] * 3,
                adds={0: dxb},
            )
    for l in range(DEPTH):
        G["b_ada"][l] = jnp.concatenate([d[0] for d in dbias[l]])
        for k in ("norm_mix", "ssd_norm", "q_norm", "kv_norm", "attn_norm", "norm_mlp"):
            G[k][l] = G[k][l][0]
    grads = {k: (jnp.stack(v) if isinstance(v, list) else v[0]) for k, v in G.items()}
    dmod_raw = jnp.stack([jnp.stack([d[:, 0, :] for d in dmod[l]], axis=1) for l in range(DEPTH)])
    return loss, grad_x, grads, dmod_raw


MESH = pl.DeviceIdType.MESH
ANY = pl.BlockSpec(memory_space=pl.ANY)
PACK_W = 1024
PACK_TILE = 256


def _place():
    x, y, c = lax.axis_index("x"), lax.axis_index("y"), lax.axis_index("c")
    chips = [(1 - x, y), (x, 1 - y), (1 - x, 1 - y)]
    return x, y, c, chips


def _remote(src, dst, send_sem, recv_sem, to):
    return pltpu.make_async_remote_copy(src_ref=src, dst_ref=dst, send_sem=send_sem, recv_sem=recv_sem, device_id=to, device_id_type=MESH)


def all_gather8(name, v):
    m_per, n = v.shape

    def body(x_ref, out_ref, send_sems, recv_sems, local_sem):
        x, y, c, chips = _place()
        me, sibling = (x, y, c), (x, y, 1 - c)

        def rows(px, py, pc):
            return out_ref.at[pl.ds((4 * px + 2 * py + pc) * m_per, m_per), :]

        def copy(k, block, to, src=None):
            return _remote(rows(*block) if src is None else src, rows(*block), send_sems.at[k], recv_sems.at[k], to)

        mine = pltpu.make_async_copy(x_ref, rows(*me), local_sem)
        mine.start()
        first = [copy(0, me, sibling, src=x_ref)]
        first += [copy(1 + j, me, (*chip, c), src=x_ref) for j, chip in enumerate(chips)]
        for cp in first:
            cp.start()
        passed = [copy(4 + j, (*chip, c), sibling) for j, chip in enumerate(chips)]
        for j, chip in enumerate(chips):
            copy(1 + j, (*chip, c), me).wait_recv()
            passed[j].start()
        copy(0, sibling, me).wait_recv()
        for j, chip in enumerate(chips):
            copy(4 + j, (*chip, 1 - c), me).wait_recv()
        for cp in first + passed:
            cp.wait_send()
        mine.wait()

    return pl.pallas_call(
        body,
        name=name,
        out_shape=jax.ShapeDtypeStruct((N_DEV * m_per, n), v.dtype),
        in_specs=[pl.BlockSpec(memory_space=pltpu.VMEM)],
        out_specs=pl.BlockSpec(memory_space=pltpu.VMEM),
        scratch_shapes=[pltpu.SemaphoreType.DMA((7,)), pltpu.SemaphoreType.DMA((7,)), pltpu.SemaphoreType.DMA],
        compiler_params=pltpu.CompilerParams(vmem_limit_bytes=VMEM_LIMIT),
    )(v)


def gather_weights(pack):
    R, n = pack.shape
    rh = R // 2

    def body(x_ref, out_ref, send_sems, recv_sems, local_sem):
        x, y, c, chips = _place()
        me = 2 * x + y

        def half(chip, hc):
            return out_ref.at[chip, pl.ds(hc * rh, rh), :]

        mine = pltpu.make_async_copy(x_ref, out_ref.at[me], local_sem)
        mine.start()
        src = x_ref.at[pl.ds(c * rh, rh), :]
        first = [_remote(src, half(me, c), send_sems.at[j], recv_sems.at[j], (px, py, c)) for j, (px, py) in enumerate(chips)]
        for cp in first:
            cp.start()
        passed = []
        for j, (px, py) in enumerate(chips):
            got = half(2 * px + py, c)
            _remote(got, got, send_sems.at[j], recv_sems.at[j], (px, py, c)).wait_recv()
            cp = _remote(got, got, send_sems.at[3 + j], recv_sems.at[3 + j], (x, y, 1 - c))
            cp.start()
            passed.append(cp)
        for j, (px, py) in enumerate(chips):
            got = half(2 * px + py, 1 - c)
            _remote(got, got, send_sems.at[3 + j], recv_sems.at[3 + j], (x, y, 1 - c)).wait_recv()
        for cp in first + passed:
            cp.wait_send()
        mine.wait()

    return pl.pallas_call(
        body,
        name="gather_weights",
        out_shape=jax.ShapeDtypeStruct((N_CHIPS, R, n), pack.dtype),
        in_specs=[ANY],
        out_specs=ANY,
        scratch_shapes=[pltpu.SemaphoreType.DMA((6,)), pltpu.SemaphoreType.DMA((6,)), pltpu.SemaphoreType.DMA],
    )(pack)


def swap_halves(g):
    n_slot, R, n = g.shape
    rh = R // 2

    def body(g_ref, got_ref, send_sem, recv_sem):
        x, y, c, _ = _place()
        cp = _remote(g_ref.at[:, pl.ds((1 - c) * rh, rh), :], got_ref, send_sem, recv_sem, (x, y, 1 - c))
        cp.start()
        cp.wait()

    return pl.pallas_call(
        body,
        name="swap_halves",
        out_shape=jax.ShapeDtypeStruct((n_slot, rh, n), g.dtype),
        in_specs=[ANY],
        out_specs=ANY,
        scratch_shapes=[pltpu.SemaphoreType.DMA, pltpu.SemaphoreType.DMA],
    )(g)


def scatter_chips(buf):
    def body(s_ref, got_ref, send_sems, recv_sems):
        x, y, c, chips = _place()
        me = 2 * x + y
        cps = [_remote(s_ref.at[2 * px + py], got_ref.at[me], send_sems.at[j], recv_sems.at[j], (px, py, c)) for j, (px, py) in enumerate(chips)]
        for cp in cps:
            cp.start()
        for j, (px, py) in enumerate(chips):
            got = got_ref.at[2 * px + py]
            _remote(got, got, send_sems.at[j], recv_sems.at[j], (px, py, c)).wait_recv()
        for cp in cps:
            cp.wait_send()

    return pl.pallas_call(
        body,
        name="scatter_chips",
        out_shape=jax.ShapeDtypeStruct(buf.shape, buf.dtype),
        in_specs=[ANY],
        out_specs=ANY,
        scratch_shapes=[pltpu.SemaphoreType.DMA((3,)), pltpu.SemaphoreType.DMA((3,))],
    )(buf)


def join_halves(h):
    rh, n = h.shape

    def body(h_ref, out_ref, send_sem, recv_sem, local_sem):
        x, y, c, _ = _place()
        mine = out_ref.at[pl.ds(c * rh, rh), :]
        loc = pltpu.make_async_copy(h_ref, mine, local_sem)
        loc.start()
        cp = _remote(h_ref, mine, send_sem, recv_sem, (x, y, 1 - c))
        cp.start()
        theirs = out_ref.at[pl.ds((1 - c) * rh, rh), :]
        _remote(theirs, theirs, send_sem, recv_sem, (x, y, 1 - c)).wait_recv()
        cp.wait_send()
        loc.wait()

    return pl.pallas_call(
        body,
        name="join_halves",
        out_shape=jax.ShapeDtypeStruct((2 * rh, n), h.dtype),
        in_specs=[ANY],
        out_specs=ANY,
        scratch_shapes=[pltpu.SemaphoreType.DMA, pltpu.SemaphoreType.DMA, pltpu.SemaphoreType.DMA],
    )(h)


def chip_sum(g, got, core, chip):
    n_slot, R, n = g.shape
    rh = R // 2
    nb = rh // PACK_TILE

    def body(pos, g_ref, got_ref, wire_ref, own_ref):
        k = pl.program_id(1)
        s = g_ref[0] + got_ref[0]
        wire_ref[0] = s.astype(wire_ref.dtype)

        @pl.when(k == pos[1])
        def _():
            own_ref[...] = s

    grid_spec = pltpu.PrefetchScalarGridSpec(
        num_scalar_prefetch=1,
        grid=(nb, n_slot),
        in_specs=[
            pl.BlockSpec((1, PACK_TILE, n), lambda i, k, pos: (k, pos[0] * nb + i, 0)),
            pl.BlockSpec((1, PACK_TILE, n), lambda i, k, pos: (k, i, 0)),
        ],
        out_specs=[
            pl.BlockSpec((1, PACK_TILE, n), lambda i, k, pos: (k, i, 0)),
            pl.BlockSpec((PACK_TILE, n), lambda i, k, pos: (i, 0)),
        ],
    )
    return pl.pallas_call(
        body,
        name="chip_sum",
        grid_spec=grid_spec,
        out_shape=[jax.ShapeDtypeStruct((n_slot, rh, n), WIRE_DTYPE), jax.ShapeDtypeStruct((rh, n), F32)],
        compiler_params=_cparams(2),
    )(jnp.stack([core, chip]).astype(jnp.int32), g, got)


def mesh_sum(own, got, chip):
    rh, n = own.shape
    n_slot = got.shape[0]

    def body(pos, own_ref, *refs):
        out_ref = refs[-1]
        acc = own_ref[...]
        for k in range(n_slot):
            acc = acc + jnp.where(k != pos[0], refs[k][0].astype(F32), 0.0)
        out_ref[...] = acc

    grid_spec = pltpu.PrefetchScalarGridSpec(
        num_scalar_prefetch=1,
        grid=(rh // PACK_TILE,),
        in_specs=[pl.BlockSpec((PACK_TILE, n), lambda i, pos: (i, 0))]
        + [pl.BlockSpec((1, PACK_TILE, n), lambda i, pos, k=k: (k, i, 0)) for k in range(n_slot)],
        out_specs=pl.BlockSpec((PACK_TILE, n), lambda i, pos: (i, 0)),
    )
    return pl.pallas_call(
        body, name="mesh_sum", grid_spec=grid_spec, out_shape=jax.ShapeDtypeStruct((rh, n), F32), compiler_params=_cparams(1)
    )(jnp.stack([chip]).astype(jnp.int32), own, *([got] * n_slot))


def sum_devices(name, v):
    def body(v_ref, o_ref):
        acc = v_ref[0]
        for d in range(1, N_DEV):
            acc = acc + v_ref[d]
        o_ref[...] = acc

    return pl.pallas_call(body, name=name, out_shape=jax.ShapeDtypeStruct(v.shape[1:], F32))(v)


def adamw(name, w, g, m, v):
    shape = w.shape
    w2, g2, m2, v2 = (t.reshape(-1, shape[-1]) for t in (w, g, m, v))
    rows, n = w2.shape
    tr = _pick(rows, (256, 128, 64, 32, 16, 8))
    c1 = 1.0 / (1.0 - ADAM_B1**ADAM_STEP)
    c2 = 1.0 / (1.0 - ADAM_B2**ADAM_STEP)

    def body(w_ref, g_ref, m_ref, v_ref, d_ref, nm_ref, nv_ref):
        gv = g_ref[...]
        nm = ADAM_B1 * m_ref[...] + (1.0 - ADAM_B1) * gv
        nv = ADAM_B2 * v_ref[...] + (1.0 - ADAM_B2) * jnp.square(gv)
        d_ref[...] = -ADAM_LR * ((nm * c1) / (jnp.sqrt(nv * c2) + ADAM_EPS) + ADAM_WD * w_ref[...])
        nm_ref[...] = nm
        nv_ref[...] = nv

    spec = pl.BlockSpec((tr, n), lambda i: (i, 0))
    outs = pl.pallas_call(
        body,
        name=name,
        grid=(rows // tr,),
        in_specs=[spec] * 4,
        out_specs=[spec] * 3,
        out_shape=[jax.ShapeDtypeStruct((rows, n), F32)] * 3,
        compiler_params=_cparams(1),
    )(w2, g2, m2, v2)
    return [o.reshape(shape) for o in outs]


def ada_fwd(c_all, w_ada):
    n_tok, d = c_all.shape
    depth, _, cols = w_ada.shape
    tn = _pick(cols, (512, 384, 256, 128))

    def body(c_ref, w_ref, o_ref):
        o_ref[0] = jnp.dot(_silu(c_ref[...]).astype(MXU_DTYPE), w_ref[0].astype(MXU_DTYPE), preferred_element_type=F32)

    return pl.pallas_call(
        body,
        name="ada_fwd",
        grid=(depth, cols // tn),
        in_specs=[pl.BlockSpec((n_tok, d), lambda l, j: (0, 0)), pl.BlockSpec((1, d, tn), lambda l, j: (l, 0, j))],
        out_specs=pl.BlockSpec((1, n_tok, tn), lambda l, j: (l, 0, j)),
        out_shape=jax.ShapeDtypeStruct((depth, n_tok, cols), F32),
        compiler_params=_cparams(2),
    )(c_all, w_ada)


COL_SHARDED = ("w_in", "w_uq", "w_ukv", "w_up")


def _pack_rows(parts, rows):
    flat = jnp.concatenate([p.reshape(-1, PACK_W) for p in parts], axis=0)
    return jnp.pad(flat, ((0, rows - flat.shape[0]), (0, 0)))


def _big_rows(shards):
    n = sum(int(np.prod(shards[k].shape)) for k in BIG) // PACK_W
    return -(-n // (2 * PACK_TILE)) * 2 * PACK_TILE


def unpack_gathered(gathered, shard_shapes):
    out, r0 = {}, 0
    for k in BIG:
        shp = shard_shapes[k]
        n = int(np.prod(shp)) // PACK_W
        seg = gathered[:, r0 : r0 + n].reshape(N_CHIPS, *shp)
        r0 += n
        if k in COL_SHARDED:
            out[k] = seg.transpose(1, 2, 0, 3).reshape(shp[0], shp[1], N_CHIPS * shp[2])
        else:
            out[k] = seg.transpose(1, 0, 2, 3).reshape(shp[0], N_CHIPS * shp[1], shp[2])
    return out


def pack_full_grads(grads, shard_shapes, rows):
    parts = []
    for k in BIG:
        shp = shard_shapes[k]
        g = grads[k]
        if k in COL_SHARDED:
            g = g.reshape(shp[0], shp[1], N_CHIPS, shp[2]).transpose(2, 0, 1, 3)
        else:
            g = g.reshape(shp[0], N_CHIPS, shp[1], shp[2]).transpose(1, 0, 2, 3)
        parts.append(g.reshape(N_CHIPS, -1, PACK_W))
    flat = jnp.concatenate(parts, axis=1)
    return jnp.pad(flat, ((0, 0), (0, rows - flat.shape[1]), (0, 0)))


def unpack_shards(buf, shard_shapes):
    out, r0 = {}, 0
    for k in BIG:
        n = int(np.prod(shard_shapes[k])) // PACK_W
        out[k] = buf[r0 : r0 + n].reshape(shard_shapes[k])
        r0 += n
    return out


def _flat_pack(arrs, row_multiple=8):
    flat = jnp.concatenate([a.reshape(-1) for a in arrs])
    per = PACK_W * row_multiple
    n = -(-flat.shape[0] // per) * per
    return jnp.pad(flat, (0, n - flat.shape[0])).reshape(-1, PACK_W)


def _flat_unpack(buf, shapes):
    flat, out, o = buf.reshape(-1), [], 0
    for s in shapes:
        n = int(np.prod(s))
        out.append(flat[o : o + n].reshape(s))
        o += n
    return out


WEIGHTS = (
    "w_ada", "b_ada", "norm_mix", "w_in", "conv_w", "conv_b", "dt_bias", "a_log", "d_skip", "ssd_norm", "q_norm", "w_uq",
    "kv_norm", "w_ukv", "attn_norm", "w_out", "norm_mlp", "w_up", "conv_ff_w", "conv_ff_b", "w_down", "final_norm",
)
REPLICATED = ("b_ada", "norm_mix", "conv_b", "dt_bias", "a_log", "d_skip", "ssd_norm", "q_norm", "kv_norm", "attn_norm",
              "norm_mlp", "conv_ff_b", "final_norm")
CONV_SHARDED = ("conv_w", "conv_ff_w")


def kernel(x, c, positions, w_ada, b_ada, norm_mix, w_in, conv_w, conv_b, dt_bias, a_log, d_skip, ssd_norm, q_norm, w_uq, kv_norm, w_ukv, attn_norm, w_out, norm_mlp, w_up, conv_ff_w, conv_ff_b, w_down, final_norm, loss_target, m_w_ada, m_b_ada, m_norm_mix, m_w_in, m_conv_w, m_conv_b, m_dt_bias, m_a_log, m_d_skip, m_ssd_norm, m_q_norm, m_w_uq, m_kv_norm, m_w_ukv, m_attn_norm, m_w_out, m_norm_mlp, m_w_up, m_conv_ff_w, m_conv_ff_b, m_w_down, m_final_norm, v_w_ada, v_b_ada, v_norm_mix, v_w_in, v_conv_w, v_conv_b, v_dt_bias, v_a_log, v_d_skip, v_ssd_norm, v_q_norm, v_w_uq, v_kv_norm, v_w_ukv, v_attn_norm, v_w_out, v_norm_mlp, v_w_up, v_conv_ff_w, v_conv_ff_b, v_w_down, v_final_norm):
    loc = locals()
    Wl = {k: loc[k] for k in WEIGHTS}
    Ml = {k: loc["m_" + k] for k in WEIGHTS}
    Vl = {k: loc["v_" + k] for k in WEIGHTS}
    xi, yi, ci = lax.axis_index("x"), lax.axis_index("y"), lax.axis_index("c")
    chip = 2 * xi + yi
    dev = 2 * chip + ci
    B, S, D = x.shape
    n_tok = N_DEV * B

    c_all = all_gather8("gather_c", c.reshape(8, -1)).reshape(n_tok, D)
    mod_cols = ada_fwd(c_all, w_ada)
    cols = mod_cols.shape[-1]
    half = n_tok // 2
    mod_mine = lax.dynamic_slice_in_dim(mod_cols, ci * half, half, axis=1)
    small_in = _flat_pack([mod_mine, conv_w, conv_ff_w])
    n_mod = mod_mine.size // PACK_W
    small_all = all_gather8("gather_mod", small_in).reshape(N_CHIPS, 2, -1, PACK_W)
    mod_all = small_all[:, :, :n_mod].reshape(N_CHIPS, 2, DEPTH, half, cols).transpose(2, 1, 3, 0, 4).reshape(DEPTH, n_tok, N_CHIPS * cols)
    mod_raw = lax.dynamic_slice_in_dim(mod_all, dev * B, B, axis=1).reshape(DEPTH, B, 6, D)
    conv_parts = [_flat_unpack(small_all[k, 0, n_mod:], [conv_w.shape, conv_ff_w.shape]) for k in range(N_CHIPS)]
    conv_full = {name: jnp.concatenate([conv_parts[k][i] for k in range(N_CHIPS)], axis=-1) for i, name in enumerate(CONV_SHARDED)}

    shard_shapes = {k: Wl[k].shape for k in BIG}
    rows = _big_rows(Wl)
    gathered = gather_weights(_pack_rows([Wl[k].astype(MXU_DTYPE) for k in BIG], rows))
    W = unpack_gathered(gathered, shard_shapes)
    W.update({k: Wl[k] for k in REPLICATED})
    W.update(conv_full)

    loss_lanes, grad_x, grads, dmod_raw = local_step(x, loss_target, positions, mod_raw, W)
    loss = lax.psum(loss_lanes[0, 0], ("x", "y", "c"))

    gpack = pack_full_grads(grads, shard_shapes, rows)
    wire, own = chip_sum(gpack, swap_halves(gpack), ci, chip)
    g_shard = unpack_shards(join_halves(mesh_sum(own, scatter_chips(wire), chip)), shard_shapes)

    small_names = REPLICATED + CONV_SHARDED
    small_out = _flat_pack([grads[k] for k in small_names] + [dmod_raw])
    small_got = all_gather8("gather_small", small_out).reshape(N_DEV, -1, PACK_W)
    small_sum = _flat_unpack(sum_devices("sum_small", small_got), [grads[k].shape for k in small_names])
    G = dict(zip(small_names, small_sum))
    for name in CONV_SHARDED:
        width = Wl[name].shape[-1]
        G[name] = lax.dynamic_slice_in_dim(G[name], chip * width, width, axis=-1)
    n_small = sum(grads[k].size for k in small_names)
    dmod_all = small_got.reshape(N_DEV, -1)[:, n_small : n_small + dmod_raw.size].reshape(N_DEV, DEPTH, B, 6 * D)
    dmod_all = dmod_all.transpose(1, 0, 2, 3).reshape(DEPTH, n_tok, 6 * D)
    dmod_cols = lax.dynamic_slice_in_dim(dmod_all, chip * cols, cols, axis=-1)
    G["w_ada"] = jnp.stack([matmul(f"w_ada_wgrad{l}", c_all, dmod_cols[l], ta=True, a_act=_silu) for l in range(DEPTH)])
    G.update(g_shard)

    deltas, new_m, new_v = {}, {}, {}
    small_upd = adamw("adamw_small", *[_flat_pack([t[k] for k in REPLICATED]) for t in (Wl, G, Ml, Vl)])
    for res, t in zip((deltas, new_m, new_v), small_upd):
        res.update(zip(REPLICATED, _flat_unpack(t, [Wl[k].shape for k in REPLICATED])))
    for k in WEIGHTS:
        if k not in REPLICATED:
            deltas[k], new_m[k], new_v[k] = adamw("adamw_" + k, Wl[k], G[k], Ml[k], Vl[k])
    return (loss, grad_x, *[G[k] for k in WEIGHTS], *[deltas[k] for k in WEIGHTS], *[new_m[k] for k in WEIGHTS], *[new_v[k] for k in WEIGHTS])
```

```python
import functools
from typing import NamedTuple

import numpy as np
import jax
import jax.numpy as jnp
from jax import lax
from jax.experimental import pallas as pl
from jax.experimental.pallas import tpu as pltpu

F32 = jnp.float32
BF16 = jnp.bfloat16
MXU_DTYPE = jnp.bfloat16
WIRE_DTYPE = jnp.bfloat16
HIGHEST = lax.Precision.HIGHEST

D_MODEL = 1024
DEPTH = 2
D_SSD = 1024
SSD_HEADS = 16
SSD_HEAD_DIM = 64
SSD_GROUPS = 2
SSD_STATE = 128
SSD_CONV = 4
CHUNK = 128
MLA_HEADS = 8
QK_NOPE = 128
QK_ROPE = 64
V_DIM = 128
D_ATT = MLA_HEADS * V_DIM
Q_RANK = 384
KV_RANK = 256
ROPE_BASE = 10000.0
D_FF = 2816
FF_CONV = 3
EPS = 1e-6
D_XBC = D_SSD + 2 * SSD_GROUPS * SSD_STATE
D_IN = D_SSD + D_XBC + SSD_HEADS + Q_RANK + KV_RANK + QK_ROPE
ADAM_LR, ADAM_B1, ADAM_B2, ADAM_EPS, ADAM_WD, ADAM_STEP = 0.001, 0.9, 0.999, 1e-08, 0.01, 10

LANE = 128
N_CHIPS = 4
N_DEV = 8

OFF_Z, OFF_XBC, OFF_CKV, OFF_KR, OFF_DT, OFF_CQ = 0, 1024, 2560, 2816, 2944, 3072
N_PROJ = 3456
Q_HEAD_PAD = 256
VMEM_LIMIT = 56 * 1024 * 1024


def _cparams(n_axes):
    return pltpu.CompilerParams(dimension_semantics=("arbitrary",) * n_axes, vmem_limit_bytes=VMEM_LIMIT)


def _pick(n, prefs):
    for p in prefs:
        if n % p == 0:
            return p
    return n


def _silu(x):
    return x * jax.nn.sigmoid(x)


def _rms(x, g):
    return x * lax.rsqrt(jnp.mean(x * x, axis=-1, keepdims=True) + EPS) * g


def _softplus(x):
    return jnp.maximum(x, 0.0) + jnp.log(1.0 + jnp.exp(-jnp.abs(x)))


class Col(NamedTuple):
    arr: jax.Array
    width: int
    blk: int


def _as_col(a):
    return a if isinstance(a, Col) else Col(a, a.shape[-1], 0)


def _row_specs(tiled, pbatch, shared, tile):
    specs = [pl.BlockSpec((1, tile, t.width), lambda b, i, blk=t.blk: (b, i, blk)) for t in tiled]
    specs += [pl.BlockSpec((1, 1, p.shape[-1]), lambda b, i: (b, 0, 0)) for p in pbatch]
    specs += [pl.BlockSpec(s.shape, lambda b, i: (0, 0)) for s in shared]
    return specs


def _row_vals(refs, nt, npb):
    return [r[0].astype(F32) for r in refs[: nt + npb]] + [r[...].astype(F32) for r in refs[nt + npb :]]


def rowwise_fwd(name, fn, tiled, pbatch, shared, outs, tile=512):
    tiled = [_as_col(t) for t in tiled]
    B, S = tiled[0].arr.shape[:2]
    tile = min(tile, S)
    nt, npb, nsh = len(tiled), len(pbatch), len(shared)
    n_in = nt + npb + nsh

    def body(*refs):
        res = fn(*_row_vals(refs[:n_in], nt, npb))
        for r, v in zip(refs[n_in:], res):
            r[0] = v.astype(r.dtype)

    return pl.pallas_call(
        body,
        name=name,
        grid=(B, S // tile),
        in_specs=_row_specs(tiled, pbatch, shared, tile),
        out_specs=[pl.BlockSpec((1, tile, w), lambda b, i: (b, i, 0)) for w, _ in outs],
        out_shape=[jax.ShapeDtypeStruct((B, S, w), dt) for w, dt in outs],
        compiler_params=_cparams(2),
    )(*[t.arr for t in tiled], *pbatch, *shared)


def rowwise_bwd(name, fn, tiled, pbatch, shared, cts, grad_tiled, grad_shared, adds=None, tile=256):
    tiled = [_as_col(t) for t in tiled]
    adds = adds or {}
    B, S = tiled[0].arr.shape[:2]
    tile = min(tile, S)
    nt, npb, nsh = len(tiled), len(pbatch), len(shared)
    n_in = nt + npb + nsh
    add_idx = sorted(adds)
    gt = [i for i in range(nt) if grad_tiled[i]]
    gs = [i for i in range(nsh) if grad_shared[i]]
    diff = gt + [nt + i for i in range(npb)] + [nt + npb + i for i in gs]
    n_ct, n_add = len(cts), len(add_idx)

    def body(*refs):
        vals = _row_vals(refs[:n_in], nt, npb)
        ct_v = tuple(r[0].astype(F32) for r in refs[n_in : n_in + n_ct])
        add_v = {i: r[0].astype(F32) for i, r in zip(add_idx, refs[n_in + n_ct : n_in + n_ct + n_add])}
        out_refs = refs[n_in + n_ct + n_add :]

        def f(*dargs):
            full = list(vals)
            for k, i in enumerate(diff):
                full[i] = dargs[k]
            return tuple(fn(*full))

        _, vjp = jax.vjp(f, *[vals[i] for i in diff])
        grads = vjp(ct_v)
        b, i = pl.program_id(0), pl.program_id(1)
        k = 0
        for idx in gt:
            g = grads[k]
            if idx in add_v:
                g = g + add_v[idx]
            out_refs[k][0] = g.astype(out_refs[k].dtype)
            k += 1
        for _ in range(npb):
            r, g = out_refs[k], grads[k]

            @pl.when(i == 0)
            def _(r=r, g=g):
                r[0] = g

            @pl.when(i > 0)
            def _(r=r, g=g):
                r[0] += g

            k += 1
        for _ in gs:
            r, g = out_refs[k], grads[k]

            @pl.when((i == 0) & (b == 0))
            def _(r=r, g=g):
                r[...] = g

            @pl.when((i > 0) | (b > 0))
            def _(r=r, g=g):
                r[...] += g

            k += 1

    out_specs = [pl.BlockSpec((1, tile, tiled[i].width), lambda b, i: (b, i, 0)) for i in gt]
    out_shape = [jax.ShapeDtypeStruct((B, S, tiled[i].width), F32) for i in gt]
    out_specs += [pl.BlockSpec((1, 1, p.shape[-1]), lambda b, i: (b, 0, 0)) for p in pbatch]
    out_shape += [jax.ShapeDtypeStruct(p.shape, F32) for p in pbatch]
    out_specs += [pl.BlockSpec(shared[i].shape, lambda b, i: (0, 0)) for i in gs]
    out_shape += [jax.ShapeDtypeStruct(shared[i].shape, F32) for i in gs]
    in_specs = _row_specs(tiled, pbatch, shared, tile)
    in_specs += [pl.BlockSpec((1, tile, c.shape[-1]), lambda b, i: (b, i, 0)) for c in cts]
    in_specs += [pl.BlockSpec((1, tile, adds[i].shape[-1]), lambda b, i: (b, i, 0)) for i in add_idx]
    return pl.pallas_call(
        body,
        name=name,
        grid=(B, S // tile),
        in_specs=in_specs,
        out_specs=out_specs,
        out_shape=out_shape,
        compiler_params=_cparams(2),
    )(*[t.arr for t in tiled], *pbatch, *shared, *cts, *[adds[i] for i in add_idx])


def fn_prenorm(x, sc, sh, g, bsc, bsh):
    return (_rms(x, g) * (1.0 + sc + bsc) + (sh + bsh),)


def fn_resid_prenorm(x, y, gate, sc, sh, g, bgate, bsc, bsh):
    x1 = x + (gate + bgate) * y
    return x1, _rms(x1, g) * (1.0 + sc + bsc) + (sh + bsh)


def fn_mix(yscan, z, o, ssd_norm, attn_norm):
    y = yscan * _silu(z)
    half = D_SSD // SSD_GROUPS
    first = lax.broadcasted_iota(jnp.int32, y.shape, 1) < half
    sq = y * y
    m0 = jnp.sum(jnp.where(first, sq, 0.0), axis=-1, keepdims=True) / half
    m1 = jnp.sum(jnp.where(first, 0.0, sq), axis=-1, keepdims=True) / half
    r = jnp.where(first, lax.rsqrt(m0 + EPS), lax.rsqrt(m1 + EPS))
    return (jnp.concatenate([y * r * ssd_norm, _rms(o, attn_norm)], axis=-1),)


def fn_mla_prep(cq, ckv, kr, cos_t, sin_t, q_norm, kv_norm, rot):
    krr = kr * cos_t + jnp.dot(kr, rot, precision=HIGHEST, preferred_element_type=F32) * sin_t
    return _rms(cq, q_norm), _rms(ckv, kv_norm), krr


def final_fwdbwd(x1, ff, target, gate, fnorm, bgate, tile=256):
    B, S, D = x1.shape
    tile = min(tile, S)

    def body(x_ref, f_ref, t_ref, g_ref, n_ref, bg_ref, loss_ref, dx_ref, df_ref, dg_ref, dn_ref, dbg_ref):
        b, i = pl.program_id(0), pl.program_id(1)
        tgt = t_ref[0]

        def f(x, y, gate, fn, bg):
            yf = _rms(x + (gate + bg) * y, fn)
            return 0.5 * jnp.sum(jnp.mean(jnp.square(yf - tgt), axis=-1, keepdims=True), axis=0, keepdims=True)

        val, vjp = jax.vjp(f, x_ref[0], f_ref[0], g_ref[0], n_ref[...], bg_ref[...])
        dx, dff, dg, dn, dbg = vjp(jnp.ones((1, 1), F32))
        dx_ref[0] = dx
        df_ref[0] = dff
        lane_loss = jnp.broadcast_to(val, (1, LANE))

        @pl.when(i == 0)
        def _():
            dg_ref[0] = dg

        @pl.when(i > 0)
        def _():
            dg_ref[0] += dg

        @pl.when((i == 0) & (b == 0))
        def _():
            dn_ref[...] = dn
            dbg_ref[...] = dbg
            loss_ref[...] = lane_loss

        @pl.when((i > 0) | (b > 0))
        def _():
            dn_ref[...] += dn
            dbg_ref[...] += dbg
            loss_ref[...] += lane_loss

    tok = pl.BlockSpec((1, tile, D), lambda b, i: (b, i, 0))
    pb = pl.BlockSpec((1, 1, D), lambda b, i: (b, 0, 0))
    sh = pl.BlockSpec((1, D), lambda b, i: (0, 0))
    return pl.pallas_call(
        body,
        name="final_loss",
        grid=(B, S // tile),
        in_specs=[tok, tok, tok, pb, sh, sh],
        out_specs=[pl.BlockSpec((1, LANE), lambda b, i: (0, 0)), tok, tok, pb, sh, sh],
        out_shape=[
            jax.ShapeDtypeStruct((1, LANE), F32),
            jax.ShapeDtypeStruct((B, S, D), F32),
            jax.ShapeDtypeStruct((B, S, D), F32),
            jax.ShapeDtypeStruct((B, 1, D), F32),
            jax.ShapeDtypeStruct((1, D), F32),
            jax.ShapeDtypeStruct((1, D), F32),
        ],
        compiler_params=_cparams(2),
    )(x1, ff, target, gate, fnorm, bgate)


MATMUL_VMEM_BUDGET = 40 * 1024 * 1024
STEP_COST_BYTES = 1.2e6


def _tile_options(n, cap):
    opts = [d for d in range(LANE, min(n, cap) + 1, LANE) if n % d == 0]
    return opts or [n]


def _matmul_tiles(M, N, K, sa, sb, so):
    best = None
    for tk in [K] + [d for d in _tile_options(K, 2048) if d >= 512 and d < K]:
        nk = K // tk
        for tm in _tile_options(M, 2048):
            for tn in _tile_options(N, 1408):
                acc = 0 if (so == 4 or nk == 1) else tm * tn * 4
                vmem = 2 * (tm * tk * sa + tk * tn * sb + tm * tn * so) + acc + tm * tn * 4 + 2 * (tm * tk + tk * tn)
                if vmem > MATMUL_VMEM_BUDGET:
                    continue
                a_reads = M * K * sa * (1 if nk == 1 else N // tn)
                b_reads = K * N * sb * (1 if (nk == 1 and N == tn) else M // tm)
                steps = (M // tm) * (N // tn) * nk
                cost = a_reads + b_reads + M * N * so + steps * STEP_COST_BYTES + (2 * M * N * 4 * (nk - 1) if nk > 1 else 0) / 8
                if best is None or cost < best[0]:
                    best = (cost, tm, tn, tk)
    assert best is not None, (M, N, K)
    return best[1:]


def matmul(name, a, b, ta=False, tb=False, out_dtype=F32, a_act=None):
    lead = None
    if a.ndim == 3 and not ta:
        lead = a.shape[:2]
    if a.ndim == 3:
        a = a.reshape(-1, a.shape[-1])
    if b.ndim == 3:
        b = b.reshape(-1, b.shape[-1])
    K, M = a.shape if ta else a.shape[::-1]
    N = b.shape[0] if tb else b.shape[1]
    assert (b.shape[1] if tb else b.shape[0]) == K, (name, a.shape, b.shape)
    tm, tn, tk = _matmul_tiles(M, N, K, a.dtype.itemsize, b.dtype.itemsize, jnp.dtype(out_dtype).itemsize)
    nk = K // tk
    dims = (((0 if ta else 1,), (1 if tb else 0,)), ((), ()))
    direct = jnp.dtype(out_dtype) == jnp.dtype(F32)

    def body(a_ref, b_ref, o_ref, *scratch):
        k = pl.program_id(2)
        av = a_ref[...]
        if a_act is not None:
            av = a_act(av.astype(F32))
        part = lax.dot_general(av.astype(MXU_DTYPE), b_ref[...].astype(MXU_DTYPE), dims, preferred_element_type=F32)
        if nk == 1:
            o_ref[...] = part.astype(o_ref.dtype)
            return
        acc = o_ref if direct else scratch[0]

        @pl.when(k == 0)
        def _():
            acc[...] = part

        @pl.when(k > 0)
        def _():
            acc[...] += part

        if not direct:

            @pl.when(k == nk - 1)
            def _():
                o_ref[...] = acc[...].astype(o_ref.dtype)

    a_spec = pl.BlockSpec((tk, tm), lambda i, j, k: (k, i)) if ta else pl.BlockSpec((tm, tk), lambda i, j, k: (i, k))
    b_spec = pl.BlockSpec((tn, tk), lambda i, j, k: (j, k)) if tb else pl.BlockSpec((tk, tn), lambda i, j, k: (k, j))
    out = pl.pallas_call(
        body,
        name=name,
        grid=(M // tm, N // tn, nk),
        in_specs=[a_spec, b_spec],
        out_specs=pl.BlockSpec((tm, tn), lambda i, j, k: (i, j)),
        out_shape=jax.ShapeDtypeStruct((M, N), out_dtype),
        scratch_shapes=[] if (direct or nk == 1) else [pltpu.VMEM((tm, tn), F32)],
        compiler_params=_cparams(3),
    )(a, b)
    return out.reshape(*lead, N) if lead is not None else out


def _shift_down(u, d):
    if d == 0:
        return u
    t = lax.broadcasted_iota(jnp.int32, u.shape, 0)
    return jnp.where(t >= d, pltpu.roll(u, d, 0), 0.0)


def _shift_up(u, d):
    if d == 0:
        return u
    s = u.shape[0]
    t = lax.broadcasted_iota(jnp.int32, u.shape, 0)
    return jnp.where(t < s - d, pltpu.roll(u, s - d, 0), 0.0)


def _conv(u, w, b):
    k = w.shape[0]
    out = b + w[k - 1 : k, :] * u
    for j in range(k - 1):
        out = out + w[j : j + 1, :] * _shift_down(u, k - 1 - j)
    return out


def _conv_bwd(u, w, dc):
    k = w.shape[0]
    du = w[k - 1 : k, :] * dc
    dws = []
    for j in range(k - 1):
        du = du + w[j : j + 1, :] * _shift_up(dc, k - 1 - j)
        dws.append(jnp.sum(dc * _shift_down(u, k - 1 - j), axis=0, keepdims=True))
    dws.append(jnp.sum(dc * u, axis=0, keepdims=True))
    return du, jnp.concatenate(dws, axis=0), jnp.sum(dc, axis=0, keepdims=True)


def conv_silu_fwd(proj, w, b):
    B, S, _ = proj.shape
    k, c = w.shape
    blk0 = OFF_XBC // LANE

    def body(u_ref, w_ref, b_ref, o_ref):
        o_ref[0] = _silu(_conv(u_ref[0], w_ref[...], b_ref[...]))

    return pl.pallas_call(
        body,
        name="conv_silu_fwd",
        grid=(B, c // LANE),
        in_specs=[
            pl.BlockSpec((1, S, LANE), lambda bi, j: (bi, 0, blk0 + j)),
            pl.BlockSpec((k, LANE), lambda bi, j: (0, j)),
            pl.BlockSpec((1, LANE), lambda bi, j: (0, j)),
        ],
        out_specs=pl.BlockSpec((1, S, LANE), lambda bi, j: (bi, 0, j)),
        out_shape=jax.ShapeDtypeStruct((B, S, c), F32),
        compiler_params=_cparams(2),
    )(proj, w, b)


def conv_silu_bwd(proj, w, b, dout):
    B, S, _ = proj.shape
    k, c = w.shape
    blk0 = OFF_XBC // LANE

    def body(u_ref, w_ref, b_ref, d_ref, du_ref, dw_ref, db_ref):
        bi = pl.program_id(1)
        u, wv = u_ref[0], w_ref[...]
        cv = _conv(u, wv, b_ref[...])
        sg = jax.nn.sigmoid(cv)
        dc = d_ref[0] * (sg * (1.0 + cv * (1.0 - sg)))
        du, dw, db = _conv_bwd(u, wv, dc)
        du_ref[0] = du

        @pl.when(bi == 0)
        def _():
            dw_ref[...] = dw
            db_ref[...] = db

        @pl.when(bi > 0)
        def _():
            dw_ref[...] += dw
            db_ref[...] += db

    return pl.pallas_call(
        body,
        name="conv_silu_bwd",
        grid=(c // LANE, B),
        in_specs=[
            pl.BlockSpec((1, S, LANE), lambda j, bi: (bi, 0, blk0 + j)),
            pl.BlockSpec((k, LANE), lambda j, bi: (0, j)),
            pl.BlockSpec((1, LANE), lambda j, bi: (0, j)),
            pl.BlockSpec((1, S, LANE), lambda j, bi: (bi, 0, j)),
        ],
        out_specs=[
            pl.BlockSpec((1, S, LANE), lambda j, bi: (bi, 0, j)),
            pl.BlockSpec((k, LANE), lambda j, bi: (0, j)),
            pl.BlockSpec((1, LANE), lambda j, bi: (0, j)),
        ],
        out_shape=[
            jax.ShapeDtypeStruct((B, S, c), F32),
            jax.ShapeDtypeStruct((k, c), F32),
            jax.ShapeDtypeStruct((1, c), F32),
        ],
        compiler_params=_cparams(2),
    )(proj, w, b, dout)


def conv_glu_fwd(u, w, b):
    B, S, c2 = u.shape
    k = w.shape[0]

    def body(u_ref, w_ref, b_ref, o_ref):
        cv = _conv(u_ref[0], w_ref[...], b_ref[...])
        o_ref[0] = (_silu(cv[:, :LANE]) * cv[:, LANE:]).astype(o_ref.dtype)

    return pl.pallas_call(
        body,
        name="conv_glu_fwd",
        grid=(B, c2 // (2 * LANE)),
        in_specs=[
            pl.BlockSpec((1, S, 2 * LANE), lambda bi, j: (bi, 0, j)),
            pl.BlockSpec((k, 2 * LANE), lambda bi, j: (0, j)),
            pl.BlockSpec((1, 2 * LANE), lambda bi, j: (0, j)),
        ],
        out_specs=pl.BlockSpec((1, S, LANE), lambda bi, j: (bi, 0, j)),
        out_shape=jax.ShapeDtypeStruct((B, S, c2 // 2), MXU_DTYPE),
        compiler_params=_cparams(2),
    )(u, w, b)


def conv_glu_bwd(u, w, b, da):
    B, S, c2 = u.shape
    k = w.shape[0]

    def body(u_ref, w_ref, b_ref, d_ref, du_ref, dw_ref, db_ref):
        bi = pl.program_id(1)
        uv, wv = u_ref[0], w_ref[...]
        cv = _conv(uv, wv, b_ref[...])
        gate, val = cv[:, :LANE], cv[:, LANE:]
        sg = jax.nn.sigmoid(gate)
        dav = d_ref[0].astype(F32)
        dc = jnp.concatenate([dav * val * (sg * (1.0 + gate * (1.0 - sg))), dav * gate * sg], axis=1)
        du, dw, db = _conv_bwd(uv, wv, dc)
        du_ref[0] = du

        @pl.when(bi == 0)
        def _():
            dw_ref[...] = dw
            db_ref[...] = db

        @pl.when(bi > 0)
        def _():
            dw_ref[...] += dw
            db_ref[...] += db

    return pl.pallas_call(
        body,
        name="conv_glu_bwd",
        grid=(c2 // (2 * LANE), B),
        in_specs=[
            pl.BlockSpec((1, S, 2 * LANE), lambda j, bi: (bi, 0, j)),
            pl.BlockSpec((k, 2 * LANE), lambda j, bi: (0, j)),
            pl.BlockSpec((1, 2 * LANE), lambda j, bi: (0, j)),
            pl.BlockSpec((1, S, LANE), lambda j, bi: (bi, 0, j)),
        ],
        out_specs=[
            pl.BlockSpec((1, S, 2 * LANE), lambda j, bi: (bi, 0, j)),
            pl.BlockSpec((k, 2 * LANE), lambda j, bi: (0, j)),
            pl.BlockSpec((1, 2 * LANE), lambda j, bi: (0, j)),
        ],
        out_shape=[
            jax.ShapeDtypeStruct((B, S, c2), F32),
            jax.ShapeDtypeStruct((k, c2), F32),
            jax.ShapeDtypeStruct((1, c2), F32),
        ],
        compiler_params=_cparams(2),
    )(u, w, b, da)


N_PAIR = SSD_HEADS // SSD_GROUPS // 2


def _ssd_chunk(g, prev, xs, bm, cm, dtraw, dt_bias, a_log, d_skip):
    L = CHUNK
    lane = lax.broadcasted_iota(jnp.int32, (1, LANE), 1)
    sub = lax.broadcasted_iota(jnp.int32, (LANE, 1), 0)
    row = lax.broadcasted_iota(jnp.int32, (L, L), 0)
    col = lax.broadcasted_iota(jnp.int32, (L, L), 1)
    tri = (row >= col).astype(F32)
    first = lane < SSD_HEAD_DIM

    dt = _softplus(dtraw + dt_bias)
    a = dt * (-jnp.exp(a_log))
    acs = jnp.dot(tri, a, precision=HIGHEST, preferred_element_type=F32)
    acs_t = acs.T
    a_end = jnp.sum(a, axis=0, keepdims=True)
    cb = lax.dot_general(cm.astype(MXU_DTYPE), bm.astype(MXU_DTYPE), (((1,), (1,)), ((), ())), preferred_element_type=F32)

    def lane_of(v, h):
        return jnp.sum(jnp.where(lane == h, v, 0.0), axis=1, keepdims=True)

    def expand(v, ha):
        return jnp.where(first, lane_of(v, ha), lane_of(v, ha + 1))

    ys, news = [], []
    for j in range(N_PAIR):
        ha = g * (SSD_HEADS // SSD_GROUPS) + 2 * j
        x = xs[j]
        dt_e, acs_e, end_e = expand(dt, ha), expand(acs, ha), expand(a_end, ha)
        xdt = x * dt_e
        y = jnp.dot(cm.astype(MXU_DTYPE), prev[j].astype(MXU_DTYPE), preferred_element_type=F32) * jnp.exp(acs_e)
        st = lax.dot_general(
            bm.astype(MXU_DTYPE), (xdt * jnp.exp(end_e - acs_e)).astype(MXU_DTYPE), (((0,), (0,)), ((), ())), preferred_element_type=F32
        )
        news.append(prev[j] * jnp.exp(end_e) + st)
        for hh in range(2):
            h = ha + hh
            seg = lane_of(acs, h) - jnp.sum(jnp.where(sub == h, acs_t, 0.0), axis=0, keepdims=True)
            decay = jnp.exp(jnp.where(row >= col, seg, -jnp.inf))
            xh = jnp.where(first if hh == 0 else jnp.logical_not(first), xdt, 0.0)
            y = y + jnp.dot((cb * decay).astype(MXU_DTYPE), xh.astype(MXU_DTYPE), preferred_element_type=F32)
        ys.append(y + x * expand(d_skip, ha))
    return ys, news


def _ssd_specs(nc, order):
    gw = D_SSD // SSD_GROUPS
    bm0, cm0 = D_SSD // SSD_STATE, D_SSD // SSD_STATE + SSD_GROUPS
    par = pl.BlockSpec((1, LANE), lambda *ids: (0, 0))
    return [
        pl.BlockSpec((1, CHUNK, gw), lambda *ids: (order(*ids)[0], order(*ids)[2], order(*ids)[1])),
        pl.BlockSpec((1, CHUNK, SSD_STATE), lambda *ids: (order(*ids)[0], order(*ids)[2], bm0 + order(*ids)[1])),
        pl.BlockSpec((1, CHUNK, SSD_STATE), lambda *ids: (order(*ids)[0], order(*ids)[2], cm0 + order(*ids)[1])),
        pl.BlockSpec((1, CHUNK, LANE), lambda *ids: (order(*ids)[0], order(*ids)[2], OFF_DT // LANE)),
        par,
        par,
        par,
    ]


def ssd_fwd(xbc_act, proj, dt_bias, a_log, d_skip):
    B, S, _ = xbc_act.shape
    nc = S // CHUNK
    gw = D_SSD // SSD_GROUPS

    def body(xs_ref, bm_ref, cm_ref, dt_ref, db_ref, al_ref, ds_ref, y_ref, st_ref, state):
        g, c = pl.program_id(1), pl.program_id(2)

        @pl.when(c == 0)
        def _():
            state[...] = jnp.zeros_like(state)

        prev = [state[j] for j in range(N_PAIR)]
        for j in range(N_PAIR):
            st_ref[0, 0, 0, j] = prev[j]
        xs = [xs_ref[0, :, j * LANE : (j + 1) * LANE] for j in range(N_PAIR)]
        ys, news = _ssd_chunk(g, prev, xs, bm_ref[0], cm_ref[0], dt_ref[0], db_ref[...], al_ref[...], ds_ref[...])
        for j in range(N_PAIR):
            y_ref[0, :, j * LANE : (j + 1) * LANE] = ys[j]
            state[j] = news[j]

    return pl.pallas_call(
        body,
        name="ssd_fwd",
        grid=(B, SSD_GROUPS, nc),
        in_specs=_ssd_specs(nc, lambda b, g, c: (b, g, c)),
        out_specs=[
            pl.BlockSpec((1, CHUNK, gw), lambda b, g, c: (b, c, g)),
            pl.BlockSpec((1, 1, 1, N_PAIR, SSD_STATE, LANE), lambda b, g, c: (b, g, c, 0, 0, 0)),
        ],
        out_shape=[
            jax.ShapeDtypeStruct((B, S, D_SSD), F32),
            jax.ShapeDtypeStruct((B, SSD_GROUPS, nc, N_PAIR, SSD_STATE, LANE), F32),
        ],
        scratch_shapes=[pltpu.VMEM((N_PAIR, SSD_STATE, LANE), F32)],
        compiler_params=_cparams(3),
    )(xbc_act, xbc_act, xbc_act, proj, dt_bias, a_log, d_skip)


def ssd_bwd(xbc_act, proj, dt_bias, a_log, d_skip, states, dy):
    B, S, _ = xbc_act.shape
    nc = S // CHUNK
    gw = D_SSD // SSD_GROUPS
    order = lambda b, cr, g: (b, g, nc - 1 - cr)

    def body(xs_ref, bm_ref, cm_ref, dt_ref, db_ref, al_ref, ds_ref, st_ref, dy_ref,
             dxs_ref, dbm_ref, dcm_ref, ddt_ref, ddb_ref, dal_ref, dds_ref, dstate):
        b, cr, g = pl.program_id(0), pl.program_id(1), pl.program_id(2)

        @pl.when(cr == 0)
        def _():
            dstate[g] = jnp.zeros((N_PAIR, SSD_STATE, LANE), F32)

        prev = [st_ref[0, 0, 0, j] for j in range(N_PAIR)]
        xs = [xs_ref[0, :, j * LANE : (j + 1) * LANE] for j in range(N_PAIR)]

        def f(prev, xs, bm, cm, dtraw, dt_bias, a_log, d_skip):
            return _ssd_chunk(g, prev, xs, bm, cm, dtraw, dt_bias, a_log, d_skip)

        _, vjp = jax.vjp(f, prev, xs, bm_ref[0], cm_ref[0], dt_ref[0], db_ref[...], al_ref[...], ds_ref[...])
        ct_y = [dy_ref[0, :, j * LANE : (j + 1) * LANE] for j in range(N_PAIR)]
        ct_s = [dstate[g, j] for j in range(N_PAIR)]
        dprev, dxs, dbm, dcm, ddt, ddb, dal, dds = vjp((ct_y, ct_s))
        for j in range(N_PAIR):
            dstate[g, j] = dprev[j]
            dxs_ref[0, :, j * LANE : (j + 1) * LANE] = dxs[j]
        dbm_ref[0] = dbm
        dcm_ref[0] = dcm

        @pl.when(g == 0)
        def _():
            ddt_ref[0] = ddt

        @pl.when(g > 0)
        def _():
            ddt_ref[0] += ddt

        first = (b == 0) & (cr == 0) & (g == 0)

        @pl.when(first)
        def _():
            ddb_ref[...] = ddb
            dal_ref[...] = dal
            dds_ref[...] = dds

        @pl.when(jnp.logical_not(first))
        def _():
            ddb_ref[...] += ddb
            dal_ref[...] += dal
            dds_ref[...] += dds

    par = pl.BlockSpec((1, LANE), lambda *ids: (0, 0))
    in_specs = _ssd_specs(nc, order) + [
        pl.BlockSpec((1, 1, 1, N_PAIR, SSD_STATE, LANE), lambda b, cr, g: (b, g, nc - 1 - cr, 0, 0, 0)),
        pl.BlockSpec((1, CHUNK, gw), lambda b, cr, g: (b, nc - 1 - cr, g)),
    ]
    return pl.pallas_call(
        body,
        name="ssd_bwd",
        grid=(B, nc, SSD_GROUPS),
        in_specs=in_specs,
        out_specs=[
            pl.BlockSpec((1, CHUNK, gw), lambda b, cr, g: (b, nc - 1 - cr, g)),
            pl.BlockSpec((1, CHUNK, SSD_STATE), lambda b, cr, g: (b, nc - 1 - cr, g)),
            pl.BlockSpec((1, CHUNK, SSD_STATE), lambda b, cr, g: (b, nc - 1 - cr, g)),
            pl.BlockSpec((1, CHUNK, LANE), lambda b, cr, g: (b, nc - 1 - cr, 0)),
            par,
            par,
            par,
        ],
        out_shape=[
            jax.ShapeDtypeStruct((B, S, D_SSD), F32),
            jax.ShapeDtypeStruct((B, S, SSD_GROUPS * SSD_STATE), F32),
            jax.ShapeDtypeStruct((B, S, SSD_GROUPS * SSD_STATE), F32),
            jax.ShapeDtypeStruct((B, S, LANE), F32),
            jax.ShapeDtypeStruct((1, LANE), F32),
            jax.ShapeDtypeStruct((1, LANE), F32),
            jax.ShapeDtypeStruct((1, LANE), F32),
        ],
        scratch_shapes=[pltpu.VMEM((SSD_GROUPS, N_PAIR, SSD_STATE, LANE), F32)],
        compiler_params=_cparams(3),
    )(xbc_act, xbc_act, xbc_act, proj, dt_bias, a_log, d_skip, states, dy)


ATT_SCALE = (QK_NOPE + QK_ROPE) ** -0.5
ATT_TILE = 512


def _rope_q(q, cos_t, sin_t, rot):
    q = q.astype(F32)
    qr = q[:, LANE:]
    qr = qr * cos_t + jnp.dot(qr, rot, precision=HIGHEST, preferred_element_type=F32) * sin_t
    return jnp.concatenate([q[:, :LANE], qr], axis=1)


def _scores(qs, kn, kr, on_diagonal):
    k = jnp.concatenate([kn, kr], axis=1).astype(MXU_DTYPE)
    s = lax.dot_general(qs, k, (((1,), (1,)), ((), ())), preferred_element_type=F32) * ATT_SCALE
    if on_diagonal:
        s = jnp.where(lax.broadcasted_iota(jnp.int32, s.shape, 0) >= lax.broadcasted_iota(jnp.int32, s.shape, 1), s, -jnp.inf)
    return s, k


def _causal_tiles(q_tile, k_tile, step):
    @pl.when(k_tile < q_tile)
    def _():
        step(False)

    @pl.when(k_tile == q_tile)
    def _():
        step(True)


def attn_fwd(q, kv, krr, cos_t, sin_t, rot):
    B, S, _ = q.shape
    t = min(ATT_TILE, S)
    n = S // t

    def body(q_ref, c_ref, s_ref, rot_ref, k_ref, kr_ref, o_ref, lse_ref, qs, m_s, l_s, acc):
        i, j = pl.program_id(2), pl.program_id(3)

        @pl.when(j == 0)
        def _():
            qs[...] = _rope_q(q_ref[0], c_ref[0], s_ref[0], rot_ref[...]).astype(MXU_DTYPE)
            m_s[...] = jnp.full_like(m_s, -jnp.inf)
            l_s[...] = jnp.zeros_like(l_s)
            acc[...] = jnp.zeros_like(acc)

        def step(on_diagonal):
            kvv = k_ref[0]
            s, _ = _scores(qs[...], kvv[:, :LANE], kr_ref[0], on_diagonal)
            m_new = jnp.maximum(m_s[...], jnp.max(s, axis=1, keepdims=True))
            alpha = jnp.exp(m_s[...] - m_new)
            p = jnp.exp(s - m_new)
            l_s[...] = alpha * l_s[...] + jnp.sum(p, axis=1, keepdims=True)
            acc[...] = alpha * acc[...] + jnp.dot(p.astype(MXU_DTYPE), kvv[:, LANE:].astype(MXU_DTYPE), preferred_element_type=F32)
            m_s[...] = m_new

        _causal_tiles(i, j, step)

        @pl.when(j == i)
        def _():
            o_ref[0] = acc[...] / l_s[...]
            lse_ref[0, 0] = m_s[...] + jnp.log(l_s[...])

    kv_j = lambda b, h, i, j: jnp.minimum(j, i)
    return pl.pallas_call(
        body,
        name="attn_fwd",
        grid=(B, MLA_HEADS, n, n),
        in_specs=[
            pl.BlockSpec((1, t, Q_HEAD_PAD), lambda b, h, i, j: (b, i, h)),
            pl.BlockSpec((1, t, LANE), lambda b, h, i, j: (b, i, 0)),
            pl.BlockSpec((1, t, LANE), lambda b, h, i, j: (b, i, 0)),
            pl.BlockSpec((LANE, LANE), lambda b, h, i, j: (0, 0)),
            pl.BlockSpec((1, t, 2 * LANE), lambda b, h, i, j: (b, kv_j(b, h, i, j), h)),
            pl.BlockSpec((1, t, LANE), lambda b, h, i, j: (b, kv_j(b, h, i, j), 0)),
        ],
        out_specs=[
            pl.BlockSpec((1, t, V_DIM), lambda b, h, i, j: (b, i, h)),
            pl.BlockSpec((1, 1, t, 1), lambda b, h, i, j: (b, h, i, 0)),
        ],
        out_shape=[jax.ShapeDtypeStruct((B, S, D_ATT), F32), jax.ShapeDtypeStruct((B, MLA_HEADS, S, 1), F32)],
        scratch_shapes=[
            pltpu.VMEM((t, Q_HEAD_PAD), MXU_DTYPE),
            pltpu.VMEM((t, 1), F32),
            pltpu.VMEM((t, 1), F32),
            pltpu.VMEM((t, V_DIM), F32),
        ],
        compiler_params=_cparams(4),
    )(q, cos_t, sin_t, rot, kv, krr)


def attn_bwd_dq(q, kv, krr, cos_t, sin_t, rot, rot_t, o, lse, do):
    B, S, _ = q.shape
    t = min(ATT_TILE, S)
    n = S // t

    def body(q_ref, c_ref, s_ref, rot_ref, rott_ref, k_ref, kr_ref, o_ref, lse_ref, do_ref, dq_ref, qs, delta, acc):
        i, j = pl.program_id(2), pl.program_id(3)

        @pl.when(j == 0)
        def _():
            qs[...] = _rope_q(q_ref[0], c_ref[0], s_ref[0], rot_ref[...]).astype(MXU_DTYPE)
            delta[...] = jnp.sum(o_ref[0] * do_ref[0], axis=1, keepdims=True)
            acc[...] = jnp.zeros_like(acc)

        def step(on_diagonal):
            kvv = k_ref[0]
            s, k = _scores(qs[...], kvv[:, :LANE], kr_ref[0], on_diagonal)
            p = jnp.exp(s - lse_ref[0, 0])
            dp = lax.dot_general(
                do_ref[0].astype(MXU_DTYPE), kvv[:, LANE:].astype(MXU_DTYPE), (((1,), (1,)), ((), ())), preferred_element_type=F32
            )
            ds = p * (dp - delta[...]) * ATT_SCALE
            acc[...] += jnp.dot(ds.astype(MXU_DTYPE), k, preferred_element_type=F32)

        _causal_tiles(i, j, step)

        @pl.when(j == i)
        def _():
            a = acc[...]
            r = a[:, LANE:]
            dr = r * c_ref[0] + jnp.dot(r * s_ref[0], rott_ref[...], precision=HIGHEST, preferred_element_type=F32)
            dq_ref[0] = jnp.concatenate([a[:, :LANE], dr], axis=1)

    kv_j = lambda b, h, i, j: jnp.minimum(j, i)
    return pl.pallas_call(
        body,
        name="attn_bwd_dq",
        grid=(B, MLA_HEADS, n, n),
        in_specs=[
            pl.BlockSpec((1, t, Q_HEAD_PAD), lambda b, h, i, j: (b, i, h)),
            pl.BlockSpec((1, t, LANE), lambda b, h, i, j: (b, i, 0)),
            pl.BlockSpec((1, t, LANE), lambda b, h, i, j: (b, i, 0)),
            pl.BlockSpec((LANE, LANE), lambda b, h, i, j: (0, 0)),
            pl.BlockSpec((LANE, LANE), lambda b, h, i, j: (0, 0)),
            pl.BlockSpec((1, t, 2 * LANE), lambda b, h, i, j: (b, kv_j(b, h, i, j), h)),
            pl.BlockSpec((1, t, LANE), lambda b, h, i, j: (b, kv_j(b, h, i, j), 0)),
            pl.BlockSpec((1, t, V_DIM), lambda b, h, i, j: (b, i, h)),
            pl.BlockSpec((1, 1, t, 1), lambda b, h, i, j: (b, h, i, 0)),
            pl.BlockSpec((1, t, V_DIM), lambda b, h, i, j: (b, i, h)),
        ],
        out_specs=pl.BlockSpec((1, t, Q_HEAD_PAD), lambda b, h, i, j: (b, i, h)),
        out_shape=jax.ShapeDtypeStruct(q.shape, F32),
        scratch_shapes=[pltpu.VMEM((t, Q_HEAD_PAD), MXU_DTYPE), pltpu.VMEM((t, 1), F32), pltpu.VMEM((t, Q_HEAD_PAD), F32)],
        compiler_params=_cparams(4),
    )(q, cos_t, sin_t, rot, rot_t, kv, krr, o, lse, do)


def attn_bwd_dkv(q, kv, krr, cos_t, sin_t, rot, o, lse, do):
    B, S, _ = q.shape
    t = min(ATT_TILE, S)
    n = S // t

    def body(q_ref, c_ref, s_ref, rot_ref, k_ref, kr_ref, o_ref, lse_ref, do_ref, dkv_ref, dkr_ref, dk_acc, dv_acc):
        j, h, i = pl.program_id(1), pl.program_id(2), pl.program_id(3)

        @pl.when(i == 0)
        def _():
            dk_acc[...] = jnp.zeros_like(dk_acc)
            dv_acc[...] = jnp.zeros_like(dv_acc)

        def step(on_diagonal):
            qs = _rope_q(q_ref[0], c_ref[0], s_ref[0], rot_ref[...]).astype(MXU_DTYPE)
            kvv = k_ref[0]
            dov = do_ref[0]
            s, _ = _scores(qs, kvv[:, :LANE], kr_ref[0], on_diagonal)
            p = jnp.exp(s - lse_ref[0, 0])
            dv_acc[...] += lax.dot_general(p.astype(MXU_DTYPE), dov.astype(MXU_DTYPE), (((0,), (0,)), ((), ())), preferred_element_type=F32)
            dp = lax.dot_general(
                dov.astype(MXU_DTYPE), kvv[:, LANE:].astype(MXU_DTYPE), (((1,), (1,)), ((), ())), preferred_element_type=F32
            )
            delta = jnp.sum(o_ref[0] * dov, axis=1, keepdims=True)
            ds = p * (dp - delta) * ATT_SCALE
            dk_acc[...] += lax.dot_general(ds.astype(MXU_DTYPE), qs, (((0,), (0,)), ((), ())), preferred_element_type=F32)

        _causal_tiles(i, j, step)

        @pl.when(i == n - 1)
        def _():
            dk = dk_acc[...]
            dkv_ref[0] = jnp.concatenate([dk[:, :LANE], dv_acc[...]], axis=1)

            @pl.when(h == 0)
            def _():
                dkr_ref[0] = dk[:, LANE:]

            @pl.when(h > 0)
            def _():
                dkr_ref[0] += dk[:, LANE:]

    q_i = lambda b, j, h, i: jnp.maximum(i, j)
    return pl.pallas_call(
        body,
        name="attn_bwd_dkv",
        grid=(B, n, MLA_HEADS, n),
        in_specs=[
            pl.BlockSpec((1, t, Q_HEAD_PAD), lambda b, j, h, i: (b, q_i(b, j, h, i), h)),
            pl.BlockSpec((1, t, LANE), lambda b, j, h, i: (b, q_i(b, j, h, i), 0)),
            pl.BlockSpec((1, t, LANE), lambda b, j, h, i: (b, q_i(b, j, h, i), 0)),
            pl.BlockSpec((LANE, LANE), lambda b, j, h, i: (0, 0)),
            pl.BlockSpec((1, t, 2 * LANE), lambda b, j, h, i: (b, j, h)),
            pl.BlockSpec((1, t, LANE), lambda b, j, h, i: (b, j, 0)),
            pl.BlockSpec((1, t, V_DIM), lambda b, j, h, i: (b, q_i(b, j, h, i), h)),
            pl.BlockSpec((1, 1, t, 1), lambda b, j, h, i: (b, h, q_i(b, j, h, i), 0)),
            pl.BlockSpec((1, t, V_DIM), lambda b, j, h, i: (b, q_i(b, j, h, i), h)),
        ],
        out_specs=[
            pl.BlockSpec((1, t, 2 * LANE), lambda b, j, h, i: (b, j, h)),
            pl.BlockSpec((1, t, LANE), lambda b, j, h, i: (b, j, 0)),
        ],
        out_shape=[jax.ShapeDtypeStruct(kv.shape, F32), jax.ShapeDtypeStruct(krr.shape, F32)],
        scratch_shapes=[pltpu.VMEM((t, Q_HEAD_PAD), F32), pltpu.VMEM((t, V_DIM), F32)],
        compiler_params=_cparams(4),
    )(q, cos_t, sin_t, rot, kv, krr, o, lse, do)


_IN_SPLITS = np.cumsum([D_SSD, D_XBC, SSD_HEADS, Q_RANK, KV_RANK]).tolist()


def _pad_last(t, n):
    return jnp.pad(t, [(0, 0)] * (t.ndim - 1) + [(0, n - t.shape[-1])])


def win_to_kernel(w):
    z, xbc, dt, cq, ckv, kr = jnp.split(w, _IN_SPLITS, axis=-1)
    return jnp.concatenate([z, xbc, ckv, _pad_last(kr, LANE), _pad_last(dt, LANE), cq], axis=-1)


def win_from_kernel(g):
    z, xbc = g[..., :OFF_XBC], g[..., OFF_XBC:OFF_CKV]
    ckv, kr = g[..., OFF_CKV:OFF_KR], g[..., OFF_KR : OFF_KR + QK_ROPE]
    dt, cq = g[..., OFF_DT : OFF_DT + SSD_HEADS], g[..., OFF_CQ:]
    return jnp.concatenate([z, xbc, dt, cq, ckv, kr], axis=-1)


def wuq_to_kernel(w):
    w = w.reshape(*w.shape[:-1], MLA_HEADS, QK_NOPE + QK_ROPE)
    return _pad_last(w, Q_HEAD_PAD).reshape(*w.shape[:-2], MLA_HEADS * Q_HEAD_PAD)


def wuq_from_kernel(g):
    g = g.reshape(*g.shape[:-1], MLA_HEADS, Q_HEAD_PAD)[..., : QK_NOPE + QK_ROPE]
    return g.reshape(*g.shape[:-2], MLA_HEADS * (QK_NOPE + QK_ROPE))


def glu_to_kernel(w):
    lead = w.shape[:-1]
    w = w.reshape(*lead, 2, D_FF // LANE, LANE)
    return jnp.swapaxes(w, -3, -2).reshape(*lead, 2 * D_FF)


def glu_from_kernel(g):
    lead = g.shape[:-1]
    g = g.reshape(*lead, D_FF // LANE, 2, LANE)
    return jnp.swapaxes(g, -3, -2).reshape(*lead, 2 * D_FF)


def _lane_pad_row(v):
    return _pad_last(v, LANE)[None, :]


def rope_tables(positions):
    inv_freq = jnp.asarray(1.0 / (ROPE_BASE ** (np.arange(0, QK_ROPE, 2, dtype=np.float32) / QK_ROPE)))
    ang = positions.astype(F32)[..., None] * inv_freq
    cos, sin = jnp.cos(ang), jnp.sin(ang)
    zeros = jnp.zeros(cos.shape[:-1] + (LANE - QK_ROPE,), F32)
    rot = np.zeros((LANE, LANE), np.float32)
    half = QK_ROPE // 2
    for j in range(half):
        rot[j + half, j] = -1.0
        rot[j, j + half] = 1.0
    return jnp.concatenate([cos, cos, zeros], -1), jnp.concatenate([sin, sin, zeros], -1), jnp.asarray(rot), jnp.asarray(rot.T)


BIG = ("w_in", "w_uq", "w_ukv", "w_out", "w_up", "w_down")


def local_step(x, target, positions, mod_raw, W):
    B, S, D = x.shape
    cos_t, sin_t, rot, rot_t = rope_tables(positions)
    row = lambda v: v.reshape(1, -1)
    Ls = []
    for l in range(DEPTH):
        mods = [mod_raw[l, :, k][:, None, :] for k in range(6)]
        bias = [row(W["b_ada"][l, k * D : (k + 1) * D]) for k in range(6)]
        Ls.append(
            dict(
                mods=mods,
                bias=bias,
                w_in=win_to_kernel(W["w_in"][l]),
                w_uq=wuq_to_kernel(W["w_uq"][l]),
                w_ukv=W["w_ukv"][l],
                w_out=W["w_out"][l],
                w_up=glu_to_kernel(W["w_up"][l]),
                w_down=W["w_down"][l],
                conv_w=W["conv_w"][l],
                conv_b=row(W["conv_b"][l]),
                conv_ff_w=glu_to_kernel(W["conv_ff_w"][l]),
                conv_ff_b=row(glu_to_kernel(W["conv_ff_b"][l])),
                dt_bias=_lane_pad_row(W["dt_bias"][l]),
                a_log=_lane_pad_row(W["a_log"][l]),
                d_skip=_lane_pad_row(W["d_skip"][l]),
                norm_mix=row(W["norm_mix"][l]),
                ssd_norm=row(W["ssd_norm"][l]),
                q_norm=row(W["q_norm"][l]),
                kv_norm=row(W["kv_norm"][l]),
                attn_norm=row(W["attn_norm"][l]),
                norm_mlp=row(W["norm_mlp"][l]),
            )
        )
    fnorm = row(W["final_norm"])

    p0 = Ls[0]
    (h1,) = rowwise_fwd(
        "prenorm_fwd", fn_prenorm, [x], [p0["mods"][1], p0["mods"][0]], [p0["norm_mix"], p0["bias"][1], p0["bias"][0]], [(D, MXU_DTYPE)]
    )
    xin = x
    for l, p in enumerate(Ls):
        s = p["saved"] = dict(xin=xin, h1=h1)
        s["proj"] = proj = matmul(f"w_in_fwd{l}", h1, p["w_in"])
        s["xbc_act"] = xbc_act = conv_silu_fwd(proj, p["conv_w"], p["conv_b"])
        s["yscan"], s["states"] = ssd_fwd(xbc_act, proj, p["dt_bias"], p["a_log"], p["d_skip"])
        mla_in = [Col(proj, Q_RANK, OFF_CQ // Q_RANK), Col(proj, KV_RANK, OFF_CKV // KV_RANK), Col(proj, LANE, OFF_KR // LANE), cos_t, sin_t]
        s["cqn"], s["ckvn"], s["krr"] = rowwise_fwd(
            f"mla_prep_fwd{l}", fn_mla_prep, mla_in, [], [p["q_norm"], p["kv_norm"], rot], [(Q_RANK, MXU_DTYPE), (KV_RANK, MXU_DTYPE), (LANE, MXU_DTYPE)]
        )
        s["q"] = matmul(f"w_uq_fwd{l}", s["cqn"], p["w_uq"], out_dtype=MXU_DTYPE)
        s["kv"] = matmul(f"w_ukv_fwd{l}", s["ckvn"], p["w_ukv"], out_dtype=MXU_DTYPE)
        s["o"], s["lse"] = attn_fwd(s["q"], s["kv"], s["krr"], cos_t, sin_t, rot)
        (s["ycat"],) = rowwise_fwd(
            f"mix_fwd{l}", fn_mix, [s["yscan"], Col(proj, D_SSD, 0), s["o"]], [], [p["ssd_norm"], p["attn_norm"]], [(D_SSD + D_ATT, MXU_DTYPE)]
        )
        s["ymix"] = matmul(f"w_out_fwd{l}", s["ycat"], p["w_out"])
        s["x1"], s["h2"] = rowwise_fwd(
            f"mid_fwd{l}",
            fn_resid_prenorm,
            [xin, s["ymix"]],
            [p["mods"][2], p["mods"][4], p["mods"][3]],
            [p["norm_mlp"], p["bias"][2], p["bias"][4], p["bias"][3]],
            [(D, F32), (D, MXU_DTYPE)],
        )
        s["u"] = matmul(f"w_up_fwd{l}", s["h2"], p["w_up"])
        s["a"] = conv_glu_fwd(s["u"], p["conv_ff_w"], p["conv_ff_b"])
        s["ff"] = matmul(f"w_down_fwd{l}", s["a"], p["w_down"])
        if l + 1 < DEPTH:
            n = Ls[l + 1]
            xin, h1 = rowwise_fwd(
                f"join_fwd{l}",
                fn_resid_prenorm,
                [s["x1"], s["ff"]],
                [p["mods"][5], n["mods"][1], n["mods"][0]],
                [n["norm_mix"], p["bias"][5], n["bias"][1], n["bias"][0]],
                [(D, F32), (D, MXU_DTYPE)],
            )

    G = {k: [None] * DEPTH for k in W if k not in ("final_norm", "w_ada")}
    dmod = [[None] * 6 for _ in range(DEPTH)]
    dbias = [[None] * 6 for _ in range(DEPTH)]
    last = Ls[-1]
    sl = last["saved"]
    loss, dx1, dff, dmod[-1][5], G["final_norm"], dbias[-1][5] = final_fwdbwd(
        sl["x1"], sl["ff"], target, last["mods"][5], fnorm, last["bias"][5]
    )
    grad_x = None
    for l in reversed(range(DEPTH)):
        p = Ls[l]
        s = p["saved"]
        da = matmul(f"w_down_dgrad{l}", dff, p["w_down"], tb=True)
        G["w_down"][l] = matmul(f"w_down_wgrad{l}", s["a"], dff, ta=True)
        du, dcw, dcb = conv_glu_bwd(s["u"], p["conv_ff_w"], p["conv_ff_b"], da)
        G["conv_ff_w"][l], G["conv_ff_b"][l] = glu_from_kernel(dcw), glu_from_kernel(dcb)[0]
        dh2 = matmul(f"w_up_dgrad{l}", du, p["w_up"], tb=True)
        G["w_up"][l] = glu_from_kernel(matmul(f"w_up_wgrad{l}", s["h2"], du, ta=True))
        dxb, dymix, dmod[l][2], dmod[l][4], dmod[l][3], G["norm_mlp"][l], dbias[l][2], dbias[l][4], dbias[l][3] = rowwise_bwd(
            f"mid_bwd{l}",
            fn_resid_prenorm,
            [s["xin"], s["ymix"]],
            [p["mods"][2], p["mods"][4], p["mods"][3]],
            [p["norm_mlp"], p["bias"][2], p["bias"][4], p["bias"][3]],
            [dx1, dh2],
            [True, True],
            [True] * 4,
        )
        dycat = matmul(f"w_out_dgrad{l}", dymix, p["w_out"], tb=True)
        G["w_out"][l] = matmul(f"w_out_wgrad{l}", s["ycat"], dymix, ta=True)
        dyscan, dz, do, G["ssd_norm"][l], G["attn_norm"][l] = rowwise_bwd(
            f"mix_bwd{l}", fn_mix, [s["yscan"], Col(s["proj"], D_SSD, 0), s["o"]], [], [p["ssd_norm"], p["attn_norm"]], [dycat], [True] * 3, [True] * 2
        )
        dq = attn_bwd_dq(s["q"], s["kv"], s["krr"], cos_t, sin_t, rot, rot_t, s["o"], s["lse"], do)
        dkv, dkrr = attn_bwd_dkv(s["q"], s["kv"], s["krr"], cos_t, sin_t, rot, s["o"], s["lse"], do)
        dcqn = matmul(f"w_uq_dgrad{l}", dq, p["w_uq"], tb=True)
        G["w_uq"][l] = wuq_from_kernel(matmul(f"w_uq_wgrad{l}", s["cqn"], dq, ta=True))
        dckvn = matmul(f"w_ukv_dgrad{l}", dkv, p["w_ukv"], tb=True)
        G["w_ukv"][l] = matmul(f"w_ukv_wgrad{l}", s["ckvn"], dkv, ta=True)
        proj = s["proj"]
        mla_in = [Col(proj, Q_RANK, OFF_CQ // Q_RANK), Col(proj, KV_RANK, OFF_CKV // KV_RANK), Col(proj, LANE, OFF_KR // LANE), cos_t, sin_t]
        dcq, dckv, dkr, G["q_norm"][l], G["kv_norm"][l] = rowwise_bwd(
            f"mla_prep_bwd{l}",
            fn_mla_prep,
            mla_in,
            [],
            [p["q_norm"], p["kv_norm"], rot],
            [dcqn, dckvn, dkrr],
            [True, True, True, False, False],
            [True, True, False],
        )
        dxs, dbm, dcm, ddt, ddb, dal, dds = ssd_bwd(s["xbc_act"], proj, p["dt_bias"], p["a_log"], p["d_skip"], s["states"], dyscan)
        G["dt_bias"][l], G["a_log"][l], G["d_skip"][l] = ddb[0, :SSD_HEADS], dal[0, :SSD_HEADS], dds[0, :SSD_HEADS]
        dxbc, G["conv_w"][l], dcb = conv_silu_bwd(proj, p["conv_w"], p["conv_b"], jnp.concatenate([dxs, dbm, dcm], axis=-1))
        G["conv_b"][l] = dcb[0]
        dproj = jnp.concatenate([dz, dxbc, dckv, dkr, ddt, dcq], axis=-1)
        dh1 = matmul(f"w_in_dgrad{l}", dproj, p["w_in"], tb=True)
        G["w_in"][l] = win_from_kernel(matmul(f"w_in_wgrad{l}", s["h1"], dproj, ta=True))
        if l > 0:
            q = Ls[l - 1]
            sq = q["saved"]
            dx1, dff, dmod[l - 1][5], dmod[l][1], dmod[l][0], G["norm_mix"][l], dbias[l - 1][5], dbias[l][1], dbias[l][0] = rowwise_bwd(
                f"join_bwd{l - 1}",
                fn_resid_prenorm,
                [sq["x1"], sq["ff"]],
                [q["mods"][5], p["mods"][1], p["mods"][0]],
                [p["norm_mix"], q["bias"][5], p["bias"][1], p["bias"][0]],
                [dxb, dh1],
                [True, True],
                [True] * 4,
            )
        else:
            grad_x, dmod[0][1], dmod[0][0], G["norm_mix"][0], dbias[0][1], dbias[0][0] = rowwise_bwd(
                "prenorm_bwd",
                fn_prenorm,
                [x],
                [p["mods"][1], p["mods"][0]],
                [p["norm_mix"], p["bias"][1], p["bias"][0]],
                [dh1],
                [True],
                [True] * 3,
                adds={0: dxb},
            )
    for l in range(DEPTH):
        G["b_ada"][l] = jnp.concatenate([d[0] for d in dbias[l]])
        for k in ("norm_mix", "ssd_norm", "q_norm", "kv_norm", "attn_norm", "norm_mlp"):
            G[k][l] = G[k][l][0]
    grads = {k: (jnp.stack(v) if isinstance(v, list) else v[0]) for k, v in G.items()}
    dmod_raw = jnp.stack([jnp.stack([d[:, 0, :] for d in dmod[l]], axis=1) for l in range(DEPTH)])
    return loss, grad_x, grads, dmod_raw


MESH = pl.DeviceIdType.MESH
ANY = pl.BlockSpec(memory_space=pl.ANY)
PACK_W = 1024
PACK_TILE = 256


def _place():
    x, y, c = lax.axis_index("x"), lax.axis_index("y"), lax.axis_index("c")
    chips = [(1 - x, y), (x, 1 - y), (1 - x, 1 - y)]
    return x, y, c, chips


def _remote(src, dst, send_sem, recv_sem, to):
    return pltpu.make_async_remote_copy(src_ref=src, dst_ref=dst, send_sem=send_sem, recv_sem=recv_sem, device_id=to, device_id_type=MESH)


def all_gather8(name, v):
    m_per, n = v.shape

    def body(x_ref, out_ref, send_sems, recv_sems, local_sem):
        x, y, c, chips = _place()
        me, sibling = (x, y, c), (x, y, 1 - c)

        def rows(px, py, pc):
            return out_ref.at[pl.ds((4 * px + 2 * py + pc) * m_per, m_per), :]

        def copy(k, block, to, src=None):
            return _remote(rows(*block) if src is None else src, rows(*block), send_sems.at[k], recv_sems.at[k], to)

        mine = pltpu.make_async_copy(x_ref, rows(*me), local_sem)
        mine.start()
        first = [copy(0, me, sibling, src=x_ref)]
        first += [copy(1 + j, me, (*chip, c), src=x_ref) for j, chip in enumerate(chips)]
        for cp in first:
            cp.start()
        passed = [copy(4 + j, (*chip, c), sibling) for j, chip in enumerate(chips)]
        for j, chip in enumerate(chips):
            copy(1 + j, (*chip, c), me).wait_recv()
            passed[j].start()
        copy(0, sibling, me).wait_recv()
        for j, chip in enumerate(chips):
            copy(4 + j, (*chip, 1 - c), me).wait_recv()
        for cp in first + passed:
            cp.wait_send()
        mine.wait()

    return pl.pallas_call(
        body,
        name=name,
        out_shape=jax.ShapeDtypeStruct((N_DEV * m_per, n), v.dtype),
        in_specs=[pl.BlockSpec(memory_space=pltpu.VMEM)],
        out_specs=pl.BlockSpec(memory_space=pltpu.VMEM),
        scratch_shapes=[pltpu.SemaphoreType.DMA((7,)), pltpu.SemaphoreType.DMA((7,)), pltpu.SemaphoreType.DMA],
        compiler_params=pltpu.CompilerParams(vmem_limit_bytes=VMEM_LIMIT),
    )(v)


def gather_weights(pack):
    R, n = pack.shape
    rh = R // 2

    def body(x_ref, out_ref, send_sems, recv_sems):
        x, y, c, chips = _place()
        me = 2 * x + y

        def half(chip, hc):
            return out_ref.at[chip, pl.ds(hc * rh, rh), :]

        src = x_ref.at[pl.ds(c * rh, rh), :]
        first = [_remote(src, half(me, c), send_sems.at[j], recv_sems.at[j], (px, py, c)) for j, (px, py) in enumerate(chips)]
        for cp in first:
            cp.start()
        passed = []
        for j, (px, py) in enumerate(chips):
            got = half(2 * px + py, c)
            _remote(got, got, send_sems.at[j], recv_sems.at[j], (px, py, c)).wait_recv()
            cp = _remote(got, got, send_sems.at[3 + j], recv_sems.at[3 + j], (x, y, 1 - c))
            cp.start()
            passed.append(cp)
        for j, (px, py) in enumerate(chips):
            got = half(2 * px + py, 1 - c)
            _remote(got, got, send_sems.at[3 + j], recv_sems.at[3 + j], (x, y, 1 - c)).wait_recv()
        for cp in first + passed:
            cp.wait_send()

    return pl.pallas_call(
        body,
        name="gather_weights",
        out_shape=jax.ShapeDtypeStruct((N_CHIPS, R, n), pack.dtype),
        in_specs=[ANY],
        out_specs=ANY,
        scratch_shapes=[pltpu.SemaphoreType.DMA((6,)), pltpu.SemaphoreType.DMA((6,))],
    )(pack)


def swap_halves(g):
    n_slot, R, n = g.shape
    rh = R // 2

    def body(g_ref, got_ref, send_sem, recv_sem):
        x, y, c, _ = _place()
        cp = _remote(g_ref.at[:, pl.ds((1 - c) * rh, rh), :], got_ref, send_sem, recv_sem, (x, y, 1 - c))
        cp.start()
        cp.wait()

    return pl.pallas_call(
        body,
        name="swap_halves",
        out_shape=jax.ShapeDtypeStruct((n_slot, rh, n), g.dtype),
        in_specs=[ANY],
        out_specs=ANY,
        scratch_shapes=[pltpu.SemaphoreType.DMA, pltpu.SemaphoreType.DMA],
    )(g)


def scatter_chips(buf):
    def body(s_ref, got_ref, send_sems, recv_sems):
        x, y, c, chips = _place()
        me = 2 * x + y
        cps = [_remote(s_ref.at[2 * px + py], got_ref.at[me], send_sems.at[j], recv_sems.at[j], (px, py, c)) for j, (px, py) in enumerate(chips)]
        for cp in cps:
            cp.start()
        for j, (px, py) in enumerate(chips):
            got = got_ref.at[2 * px + py]
            _remote(got, got, send_sems.at[j], recv_sems.at[j], (px, py, c)).wait_recv()
        for cp in cps:
            cp.wait_send()

    return pl.pallas_call(
        body,
        name="scatter_chips",
        out_shape=jax.ShapeDtypeStruct(buf.shape, buf.dtype),
        in_specs=[ANY],
        out_specs=ANY,
        scratch_shapes=[pltpu.SemaphoreType.DMA((3,)), pltpu.SemaphoreType.DMA((3,))],
    )(buf)


def swap_with_sibling(h):
    def body(h_ref, got_ref, send_sem, recv_sem):
        x, y, c, _ = _place()
        cp = _remote(h_ref, got_ref, send_sem, recv_sem, (x, y, 1 - c))
        cp.start()
        cp.wait()

    return pl.pallas_call(
        body,
        name="swap_with_sibling",
        out_shape=jax.ShapeDtypeStruct(h.shape, h.dtype),
        in_specs=[ANY],
        out_specs=ANY,
        scratch_shapes=[pltpu.SemaphoreType.DMA, pltpu.SemaphoreType.DMA],
    )(h)


def chip_sum(g, got, core, chip):
    n_slot, R, n = g.shape
    rh = R // 2
    nb = rh // PACK_TILE

    def body(pos, g_ref, got_ref, wire_ref, own_ref):
        k = pl.program_id(1)
        s = g_ref[0] + got_ref[0]
        wire_ref[0] = s.astype(wire_ref.dtype)

        @pl.when(k == pos[1])
        def _():
            own_ref[...] = s

    grid_spec = pltpu.PrefetchScalarGridSpec(
        num_scalar_prefetch=1,
        grid=(nb, n_slot),
        in_specs=[
            pl.BlockSpec((1, PACK_TILE, n), lambda i, k, pos: (k, pos[0] * nb + i, 0)),
            pl.BlockSpec((1, PACK_TILE, n), lambda i, k, pos: (k, i, 0)),
        ],
        out_specs=[
            pl.BlockSpec((1, PACK_TILE, n), lambda i, k, pos: (k, i, 0)),
            pl.BlockSpec((PACK_TILE, n), lambda i, k, pos: (i, 0)),
        ],
    )
    return pl.pallas_call(
        body,
        name="chip_sum",
        grid_spec=grid_spec,
        out_shape=[jax.ShapeDtypeStruct((n_slot, rh, n), WIRE_DTYPE), jax.ShapeDtypeStruct((rh, n), F32)],
        compiler_params=_cparams(2),
    )(jnp.stack([core, chip]).astype(jnp.int32), g, got)


def mesh_sum(own, got, chip):
    rh, n = own.shape
    n_slot = got.shape[0]

    def body(pos, own_ref, *refs):
        out_ref = refs[-1]
        acc = own_ref[...]
        for k in range(n_slot):
            acc = acc + jnp.where(k != pos[0], refs[k][0].astype(F32), 0.0)
        out_ref[...] = acc

    grid_spec = pltpu.PrefetchScalarGridSpec(
        num_scalar_prefetch=1,
        grid=(rh // PACK_TILE,),
        in_specs=[pl.BlockSpec((PACK_TILE, n), lambda i, pos: (i, 0))]
        + [pl.BlockSpec((1, PACK_TILE, n), lambda i, pos, k=k: (k, i, 0)) for k in range(n_slot)],
        out_specs=pl.BlockSpec((PACK_TILE, n), lambda i, pos: (i, 0)),
    )
    return pl.pallas_call(
        body, name="mesh_sum", grid_spec=grid_spec, out_shape=jax.ShapeDtypeStruct((rh, n), F32), compiler_params=_cparams(1)
    )(jnp.stack([chip]).astype(jnp.int32), own, *([got] * n_slot))


def sum_devices(name, v):
    def body(v_ref, o_ref):
        acc = v_ref[0]
        for d in range(1, N_DEV):
            acc = acc + v_ref[d]
        o_ref[...] = acc

    return pl.pallas_call(body, name=name, out_shape=jax.ShapeDtypeStruct(v.shape[1:], F32))(v)


def adamw(name, w, g, m, v):
    shape = w.shape
    w2, g2, m2, v2 = (t.reshape(-1, shape[-1]) for t in (w, g, m, v))
    rows, n = w2.shape
    tr = _pick(rows, (256, 128, 64, 32, 16, 8))
    c1 = 1.0 / (1.0 - ADAM_B1**ADAM_STEP)
    c2 = 1.0 / (1.0 - ADAM_B2**ADAM_STEP)

    def body(w_ref, g_ref, m_ref, v_ref, d_ref, nm_ref, nv_ref):
        gv = g_ref[...]
        nm = ADAM_B1 * m_ref[...] + (1.0 - ADAM_B1) * gv
        nv = ADAM_B2 * v_ref[...] + (1.0 - ADAM_B2) * jnp.square(gv)
        d_ref[...] = -ADAM_LR * ((nm * c1) / (jnp.sqrt(nv * c2) + ADAM_EPS) + ADAM_WD * w_ref[...])
        nm_ref[...] = nm
        nv_ref[...] = nv

    spec = pl.BlockSpec((tr, n), lambda i: (i, 0))
    outs = pl.pallas_call(
        body,
        name=name,
        grid=(rows // tr,),
        in_specs=[spec] * 4,
        out_specs=[spec] * 3,
        out_shape=[jax.ShapeDtypeStruct((rows, n), F32)] * 3,
        compiler_params=_cparams(1),
    )(w2, g2, m2, v2)
    return [o.reshape(shape) for o in outs]


def ada_fwd(c_all, w_ada):
    n_tok, d = c_all.shape
    depth, _, cols = w_ada.shape
    tn = _pick(cols, (512, 384, 256, 128))

    def body(c_ref, w_ref, o_ref):
        o_ref[0] = jnp.dot(_silu(c_ref[...]).astype(MXU_DTYPE), w_ref[0].astype(MXU_DTYPE), preferred_element_type=F32)

    return pl.pallas_call(
        body,
        name="ada_fwd",
        grid=(depth, cols // tn),
        in_specs=[pl.BlockSpec((n_tok, d), lambda l, j: (0, 0)), pl.BlockSpec((1, d, tn), lambda l, j: (l, 0, j))],
        out_specs=pl.BlockSpec((1, n_tok, tn), lambda l, j: (l, 0, j)),
        out_shape=jax.ShapeDtypeStruct((depth, n_tok, cols), F32),
        compiler_params=_cparams(2),
    )(c_all, w_ada)


COL_SHARDED = ("w_in", "w_uq", "w_ukv", "w_up")


def _pack_rows(parts, rows):
    flat = jnp.concatenate([p.reshape(-1, PACK_W) for p in parts], axis=0)
    return jnp.pad(flat, ((0, rows - flat.shape[0]), (0, 0)))


def _big_rows(shards):
    n = sum(int(np.prod(shards[k].shape)) for k in BIG) // PACK_W
    return -(-n // (2 * PACK_TILE)) * 2 * PACK_TILE


def unpack_gathered(gathered, shard_shapes):
    out, r0 = {}, 0
    for k in BIG:
        shp = shard_shapes[k]
        n = int(np.prod(shp)) // PACK_W
        seg = gathered[:, r0 : r0 + n].reshape(N_CHIPS, *shp)
        r0 += n
        if k in COL_SHARDED:
            out[k] = seg.transpose(1, 2, 0, 3).reshape(shp[0], shp[1], N_CHIPS * shp[2])
        else:
            out[k] = seg.transpose(1, 0, 2, 3).reshape(shp[0], N_CHIPS * shp[1], shp[2])
    return out


def pack_full_grads(grads, shard_shapes, rows):
    parts = []
    for k in BIG:
        shp = shard_shapes[k]
        g = grads[k]
        if k in COL_SHARDED:
            g = g.reshape(shp[0], shp[1], N_CHIPS, shp[2]).transpose(2, 0, 1, 3)
        else:
            g = g.reshape(shp[0], N_CHIPS, shp[1], shp[2]).transpose(1, 0, 2, 3)
        parts.append(g.reshape(N_CHIPS, -1, PACK_W))
    flat = jnp.concatenate(parts, axis=1)
    return jnp.pad(flat, ((0, 0), (0, rows - flat.shape[1]), (0, 0)))


def unpack_shards(buf, shard_shapes):
    out, r0 = {}, 0
    for k in BIG:
        n = int(np.prod(shard_shapes[k])) // PACK_W
        out[k] = buf[r0 : r0 + n].reshape(shard_shapes[k])
        r0 += n
    return out


def _flat_pack(arrs, row_multiple=8):
    flat = jnp.concatenate([a.reshape(-1) for a in arrs])
    per = PACK_W * row_multiple
    n = -(-flat.shape[0] // per) * per
    return jnp.pad(flat, (0, n - flat.shape[0])).reshape(-1, PACK_W)


def _flat_unpack(buf, shapes):
    flat, out, o = buf.reshape(-1), [], 0
    for s in shapes:
        n = int(np.prod(s))
        out.append(flat[o : o + n].reshape(s))
        o += n
    return out


WEIGHTS = (
    "w_ada", "b_ada", "norm_mix", "w_in", "conv_w", "conv_b", "dt_bias", "a_log", "d_skip", "ssd_norm", "q_norm", "w_uq",
    "kv_norm", "w_ukv", "attn_norm", "w_out", "norm_mlp", "w_up", "conv_ff_w", "conv_ff_b", "w_down", "final_norm",
)
REPLICATED = ("b_ada", "norm_mix", "conv_b", "dt_bias", "a_log", "d_skip", "ssd_norm", "q_norm", "kv_norm", "attn_norm",
              "norm_mlp", "conv_ff_b", "final_norm")
CONV_SHARDED = ("conv_w", "conv_ff_w")


def kernel(x, c, positions, w_ada, b_ada, norm_mix, w_in, conv_w, conv_b, dt_bias, a_log, d_skip, ssd_norm, q_norm, w_uq, kv_norm, w_ukv, attn_norm, w_out, norm_mlp, w_up, conv_ff_w, conv_ff_b, w_down, final_norm, loss_target, m_w_ada, m_b_ada, m_norm_mix, m_w_in, m_conv_w, m_conv_b, m_dt_bias, m_a_log, m_d_skip, m_ssd_norm, m_q_norm, m_w_uq, m_kv_norm, m_w_ukv, m_attn_norm, m_w_out, m_norm_mlp, m_w_up, m_conv_ff_w, m_conv_ff_b, m_w_down, m_final_norm, v_w_ada, v_b_ada, v_norm_mix, v_w_in, v_conv_w, v_conv_b, v_dt_bias, v_a_log, v_d_skip, v_ssd_norm, v_q_norm, v_w_uq, v_kv_norm, v_w_ukv, v_attn_norm, v_w_out, v_norm_mlp, v_w_up, v_conv_ff_w, v_conv_ff_b, v_w_down, v_final_norm):
    loc = locals()
    Wl = {k: loc[k] for k in WEIGHTS}
    Ml = {k: loc["m_" + k] for k in WEIGHTS}
    Vl = {k: loc["v_" + k] for k in WEIGHTS}
    xi, yi, ci = lax.axis_index("x"), lax.axis_index("y"), lax.axis_index("c")
    chip = 2 * xi + yi
    dev = 2 * chip + ci
    B, S, D = x.shape
    n_tok = N_DEV * B

    c_all = all_gather8("gather_c", c.reshape(8, -1)).reshape(n_tok, D)
    mod_cols = ada_fwd(c_all, w_ada)
    cols = mod_cols.shape[-1]
    half = n_tok // 2
    mod_mine = lax.dynamic_slice_in_dim(mod_cols, ci * half, half, axis=1)
    small_in = _flat_pack([mod_mine, conv_w, conv_ff_w])
    n_mod = mod_mine.size // PACK_W
    small_all = all_gather8("gather_mod", small_in).reshape(N_CHIPS, 2, -1, PACK_W)
    mod_all = small_all[:, :, :n_mod].reshape(N_CHIPS, 2, DEPTH, half, cols).transpose(2, 1, 3, 0, 4).reshape(DEPTH, n_tok, N_CHIPS * cols)
    mod_raw = lax.dynamic_slice_in_dim(mod_all, dev * B, B, axis=1).reshape(DEPTH, B, 6, D)
    conv_parts = [_flat_unpack(small_all[k, 0, n_mod:], [conv_w.shape, conv_ff_w.shape]) for k in range(N_CHIPS)]
    conv_full = {name: jnp.concatenate([conv_parts[k][i] for k in range(N_CHIPS)], axis=-1) for i, name in enumerate(CONV_SHARDED)}

    shard_shapes = {k: Wl[k].shape for k in BIG}
    rows = _big_rows(Wl)
    pack = _pack_rows([Wl[k].astype(MXU_DTYPE) for k in BIG], rows)
    gathered = lax.dynamic_update_slice_in_dim(gather_weights(pack), pack[None], chip, axis=0)
    W = unpack_gathered(gathered, shard_shapes)
    W.update({k: Wl[k] for k in REPLICATED})
    W.update(conv_full)

    loss_lanes, grad_x, grads, dmod_raw = local_step(x, loss_target, positions, mod_raw, W)
    loss = lax.psum(loss_lanes[0, 0], ("x", "y", "c"))

    gpack = pack_full_grads(grads, shard_shapes, rows)
    wire, own = chip_sum(gpack, swap_halves(gpack), ci, chip)
    mine = mesh_sum(own, scatter_chips(wire), chip)
    theirs = swap_with_sibling(mine)
    both = jnp.concatenate([jnp.where(ci == 0, mine, theirs), jnp.where(ci == 0, theirs, mine)], axis=0)
    g_shard = unpack_shards(both, shard_shapes)

    small_names = REPLICATED + CONV_SHARDED
    small_out = _flat_pack([grads[k] for k in small_names] + [dmod_raw])
    small_got = all_gather8("gather_small", small_out).reshape(N_DEV, -1, PACK_W)
    small_sum = _flat_unpack(sum_devices("sum_small", small_got), [grads[k].shape for k in small_names])
    G = dict(zip(small_names, small_sum))
    for name in CONV_SHARDED:
        width = Wl[name].shape[-1]
        G[name] = lax.dynamic_slice_in_dim(G[name], chip * width, width, axis=-1)
    n_small = sum(grads[k].size for k in small_names)
    dmod_all = small_got.reshape(N_DEV, -1)[:, n_small : n_small + dmod_raw.size].reshape(N_DEV, DEPTH, B, 6 * D)
    dmod_all = dmod_all.transpose(1, 0, 2, 3).reshape(DEPTH, n_tok, 6 * D)
    dmod_cols = lax.dynamic_slice_in_dim(dmod_all, chip * cols, cols, axis=-1)
    G["w_ada"] = jnp.stack([matmul(f"w_ada_wgrad{l}", c_all, dmod_cols[l], ta=True, a_act=_silu) for l in range(DEPTH)])
    G.update(g_shard)

    deltas, new_m, new_v = {}, {}, {}
    small_upd = adamw("adamw_small", *[_flat_pack([t[k] for k in REPLICATED]) for t in (Wl, G, Ml, Vl)])
    for res, t in zip((deltas, new_m, new_v), small_upd):
        res.update(zip(REPLICATED, _flat_unpack(t, [Wl[k].shape for k in REPLICATED])))
    for k in WEIGHTS:
        if k not in REPLICATED:
            deltas[k], new_m[k], new_v[k] = adamw("adamw_" + k, Wl[k], G[k], Ml[k], Vl[k])
    return (loss, grad_x, *[G[k] for k in WEIGHTS], *[deltas[k] for k in WEIGHTS], *[new_m[k] for k in WEIGHTS], *[new_v[k] for k in WEIGHTS])
```

```python
import functools
from typing import NamedTuple

import numpy as np
import jax
import jax.numpy as jnp
from jax import lax
from jax.experimental import pallas as pl
from jax.experimental.pallas import tpu as pltpu

F32 = jnp.float32
BF16 = jnp.bfloat16
MXU_DTYPE = jnp.bfloat16
WIRE_DTYPE = jnp.bfloat16
HIGHEST = lax.Precision.HIGHEST

D_MODEL = 1024
DEPTH = 2
D_SSD = 1024
SSD_HEADS = 16
SSD_HEAD_DIM = 64
SSD_GROUPS = 2
SSD_STATE = 128
SSD_CONV = 4
CHUNK = 128
MLA_HEADS = 8
QK_NOPE = 128
QK_ROPE = 64
V_DIM = 128
D_ATT = MLA_HEADS * V_DIM
Q_RANK = 384
KV_RANK = 256
ROPE_BASE = 10000.0
D_FF = 2816
FF_CONV = 3
EPS = 1e-6
D_XBC = D_SSD + 2 * SSD_GROUPS * SSD_STATE
D_IN = D_SSD + D_XBC + SSD_HEADS + Q_RANK + KV_RANK + QK_ROPE
ADAM_LR, ADAM_B1, ADAM_B2, ADAM_EPS, ADAM_WD, ADAM_STEP = 0.001, 0.9, 0.999, 1e-08, 0.01, 10

LANE = 128
N_CHIPS = 4
N_DEV = 8

OFF_Z, OFF_XBC, OFF_CKV, OFF_KR, OFF_DT, OFF_CQ = 0, 1024, 2560, 2816, 2944, 3072
N_PROJ = 3456
Q_HEAD_PAD = 256
VMEM_LIMIT = 56 * 1024 * 1024


def _cparams(n_axes):
    return pltpu.CompilerParams(dimension_semantics=("arbitrary",) * n_axes, vmem_limit_bytes=VMEM_LIMIT)


def _pick(n, prefs):
    for p in prefs:
        if n % p == 0:
            return p
    return n


def _silu(x):
    return x * jax.nn.sigmoid(x)


def _rms(x, g):
    return x * lax.rsqrt(jnp.mean(x * x, axis=-1, keepdims=True) + EPS) * g


def _softplus(x):
    return jnp.maximum(x, 0.0) + jnp.log(1.0 + jnp.exp(-jnp.abs(x)))


class Col(NamedTuple):
    arr: jax.Array
    width: int
    blk: int


def _as_col(a):
    return a if isinstance(a, Col) else Col(a, a.shape[-1], 0)


def _row_specs(tiled, pbatch, shared, tile):
    specs = [pl.BlockSpec((1, tile, t.width), lambda b, i, blk=t.blk: (b, i, blk)) for t in tiled]
    specs += [pl.BlockSpec((1, 1, p.shape[-1]), lambda b, i: (b, 0, 0)) for p in pbatch]
    specs += [pl.BlockSpec(s.shape, lambda b, i: (0, 0)) for s in shared]
    return specs


def _row_vals(refs, nt, npb):
    return [r[0].astype(F32) for r in refs[: nt + npb]] + [r[...].astype(F32) for r in refs[nt + npb :]]


def rowwise_fwd(name, fn, tiled, pbatch, shared, outs, tile=512):
    tiled = [_as_col(t) for t in tiled]
    B, S = tiled[0].arr.shape[:2]
    tile = min(tile, S)
    nt, npb, nsh = len(tiled), len(pbatch), len(shared)
    n_in = nt + npb + nsh

    def body(*refs):
        res = fn(*_row_vals(refs[:n_in], nt, npb))
        for r, v in zip(refs[n_in:], res):
            r[0] = v.astype(r.dtype)

    return pl.pallas_call(
        body,
        name=name,
        grid=(B, S // tile),
        in_specs=_row_specs(tiled, pbatch, shared, tile),
        out_specs=[pl.BlockSpec((1, tile, w), lambda b, i: (b, i, 0)) for w, _ in outs],
        out_shape=[jax.ShapeDtypeStruct((B, S, w), dt) for w, dt in outs],
        compiler_params=_cparams(2),
    )(*[t.arr for t in tiled], *pbatch, *shared)


def rowwise_bwd(name, fn, tiled, pbatch, shared, cts, grad_tiled, grad_shared, adds=None, tile=256):
    tiled = [_as_col(t) for t in tiled]
    adds = adds or {}
    B, S = tiled[0].arr.shape[:2]
    tile = min(tile, S)
    nt, npb, nsh = len(tiled), len(pbatch), len(shared)
    n_in = nt + npb + nsh
    add_idx = sorted(adds)
    gt = [i for i in range(nt) if grad_tiled[i]]
    gs = [i for i in range(nsh) if grad_shared[i]]
    diff = gt + [nt + i for i in range(npb)] + [nt + npb + i for i in gs]
    n_ct, n_add = len(cts), len(add_idx)

    def body(*refs):
        vals = _row_vals(refs[:n_in], nt, npb)
        ct_v = tuple(r[0].astype(F32) for r in refs[n_in : n_in + n_ct])
        add_v = {i: r[0].astype(F32) for i, r in zip(add_idx, refs[n_in + n_ct : n_in + n_ct + n_add])}
        out_refs = refs[n_in + n_ct + n_add :]

        def f(*dargs):
            full = list(vals)
            for k, i in enumerate(diff):
                full[i] = dargs[k]
            return tuple(fn(*full))

        _, vjp = jax.vjp(f, *[vals[i] for i in diff])
        grads = vjp(ct_v)
        b, i = pl.program_id(0), pl.program_id(1)
        k = 0
        for idx in gt:
            g = grads[k]
            if idx in add_v:
                g = g + add_v[idx]
            out_refs[k][0] = g.astype(out_refs[k].dtype)
            k += 1
        for _ in range(npb):
            r, g = out_refs[k], grads[k]

            @pl.when(i == 0)
            def _(r=r, g=g):
                r[0] = g

            @pl.when(i > 0)
            def _(r=r, g=g):
                r[0] += g

            k += 1
        for _ in gs:
            r, g = out_refs[k], grads[k]

            @pl.when((i == 0) & (b == 0))
            def _(r=r, g=g):
                r[...] = g

            @pl.when((i > 0) | (b > 0))
            def _(r=r, g=g):
                r[...] += g

            k += 1

    out_specs = [pl.BlockSpec((1, tile, tiled[i].width), lambda b, i: (b, i, 0)) for i in gt]
    out_shape = [jax.ShapeDtypeStruct((B, S, tiled[i].width), F32) for i in gt]
    out_specs += [pl.BlockSpec((1, 1, p.shape[-1]), lambda b, i: (b, 0, 0)) for p in pbatch]
    out_shape += [jax.ShapeDtypeStruct(p.shape, F32) for p in pbatch]
    out_specs += [pl.BlockSpec(shared[i].shape, lambda b, i: (0, 0)) for i in gs]
    out_shape += [jax.ShapeDtypeStruct(shared[i].shape, F32) for i in gs]
    in_specs = _row_specs(tiled, pbatch, shared, tile)
    in_specs += [pl.BlockSpec((1, tile, c.shape[-1]), lambda b, i: (b, i, 0)) for c in cts]
    in_specs += [pl.BlockSpec((1, tile, adds[i].shape[-1]), lambda b, i: (b, i, 0)) for i in add_idx]
    return pl.pallas_call(
        body,
        name=name,
        grid=(B, S // tile),
        in_specs=in_specs,
        out_specs=out_specs,
        out_shape=out_shape,
        compiler_params=_cparams(2),
    )(*[t.arr for t in tiled], *pbatch, *shared, *cts, *[adds[i] for i in add_idx])


def fn_prenorm(x, sc, sh, g, bsc, bsh):
    return (_rms(x, g) * (1.0 + sc + bsc) + (sh + bsh),)


def fn_resid_prenorm(x, y, gate, sc, sh, g, bgate, bsc, bsh):
    x1 = x + (gate + bgate) * y
    return x1, _rms(x1, g) * (1.0 + sc + bsc) + (sh + bsh)


def fn_mix(yscan, z, o, ssd_norm, attn_norm):
    y = yscan * _silu(z)
    half = D_SSD // SSD_GROUPS
    first = lax.broadcasted_iota(jnp.int32, y.shape, 1) < half
    sq = y * y
    m0 = jnp.sum(jnp.where(first, sq, 0.0), axis=-1, keepdims=True) / half
    m1 = jnp.sum(jnp.where(first, 0.0, sq), axis=-1, keepdims=True) / half
    r = jnp.where(first, lax.rsqrt(m0 + EPS), lax.rsqrt(m1 + EPS))
    return (jnp.concatenate([y * r * ssd_norm, _rms(o, attn_norm)], axis=-1),)


def fn_mla_prep(cq, ckv, kr, cos_t, sin_t, q_norm, kv_norm, rot):
    krr = kr * cos_t + jnp.dot(kr, rot, precision=HIGHEST, preferred_element_type=F32) * sin_t
    return _rms(cq, q_norm), _rms(ckv, kv_norm), krr


def final_fwdbwd(x1, ff, target, gate, fnorm, bgate, tile=256):
    B, S, D = x1.shape
    tile = min(tile, S)

    def body(x_ref, f_ref, t_ref, g_ref, n_ref, bg_ref, loss_ref, dx_ref, df_ref, dg_ref, dn_ref, dbg_ref):
        b, i = pl.program_id(0), pl.program_id(1)
        tgt = t_ref[0]

        def f(x, y, gate, fn, bg):
            yf = _rms(x + (gate + bg) * y, fn)
            return 0.5 * jnp.sum(jnp.mean(jnp.square(yf - tgt), axis=-1, keepdims=True), axis=0, keepdims=True)

        val, vjp = jax.vjp(f, x_ref[0], f_ref[0], g_ref[0], n_ref[...], bg_ref[...])
        dx, dff, dg, dn, dbg = vjp(jnp.ones((1, 1), F32))
        dx_ref[0] = dx
        df_ref[0] = dff
        lane_loss = jnp.broadcast_to(val, (1, LANE))

        @pl.when(i == 0)
        def _():
            dg_ref[0] = dg

        @pl.when(i > 0)
        def _():
            dg_ref[0] += dg

        @pl.when((i == 0) & (b == 0))
        def _():
            dn_ref[...] = dn
            dbg_ref[...] = dbg
            loss_ref[...] = lane_loss

        @pl.when((i > 0) | (b > 0))
        def _():
            dn_ref[...] += dn
            dbg_ref[...] += dbg
            loss_ref[...] += lane_loss

    tok = pl.BlockSpec((1, tile, D), lambda b, i: (b, i, 0))
    pb = pl.BlockSpec((1, 1, D), lambda b, i: (b, 0, 0))
    sh = pl.BlockSpec((1, D), lambda b, i: (0, 0))
    return pl.pallas_call(
        body,
        name="final_loss",
        grid=(B, S // tile),
        in_specs=[tok, tok, tok, pb, sh, sh],
        out_specs=[pl.BlockSpec((1, LANE), lambda b, i: (0, 0)), tok, tok, pb, sh, sh],
        out_shape=[
            jax.ShapeDtypeStruct((1, LANE), F32),
            jax.ShapeDtypeStruct((B, S, D), F32),
            jax.ShapeDtypeStruct((B, S, D), F32),
            jax.ShapeDtypeStruct((B, 1, D), F32),
            jax.ShapeDtypeStruct((1, D), F32),
            jax.ShapeDtypeStruct((1, D), F32),
        ],
        compiler_params=_cparams(2),
    )(x1, ff, target, gate, fnorm, bgate)


MATMUL_VMEM_BUDGET = 40 * 1024 * 1024
STEP_COST_BYTES = 1.2e6


def _tile_options(n, cap):
    opts = [d for d in range(LANE, min(n, cap) + 1, LANE) if n % d == 0]
    return opts or [n]


def _matmul_tiles(M, N, K, sa, sb, so):
    best = None
    for tk in [K] + [d for d in _tile_options(K, 2048) if d >= 512 and d < K]:
        nk = K // tk
        for tm in _tile_options(M, 2048):
            for tn in _tile_options(N, 1408):
                acc = 0 if (so == 4 or nk == 1) else tm * tn * 4
                vmem = 2 * (tm * tk * sa + tk * tn * sb + tm * tn * so) + acc + tm * tn * 4 + 2 * (tm * tk + tk * tn)
                if vmem > MATMUL_VMEM_BUDGET:
                    continue
                a_reads = M * K * sa * (1 if nk == 1 else N // tn)
                b_reads = K * N * sb * (1 if (nk == 1 and N == tn) else M // tm)
                steps = (M // tm) * (N // tn) * nk
                cost = a_reads + b_reads + M * N * so + steps * STEP_COST_BYTES + (2 * M * N * 4 * (nk - 1) if nk > 1 else 0) / 8
                if best is None or cost < best[0]:
                    best = (cost, tm, tn, tk)
    assert best is not None, (M, N, K)
    return best[1:]


def matmul(name, a, b, ta=False, tb=False, out_dtype=F32, a_act=None):
    lead = None
    if a.ndim == 3 and not ta:
        lead = a.shape[:2]
    if a.ndim == 3:
        a = a.reshape(-1, a.shape[-1])
    if b.ndim == 3:
        b = b.reshape(-1, b.shape[-1])
    K, M = a.shape if ta else a.shape[::-1]
    N = b.shape[0] if tb else b.shape[1]
    assert (b.shape[1] if tb else b.shape[0]) == K, (name, a.shape, b.shape)
    tm, tn, tk = _matmul_tiles(M, N, K, a.dtype.itemsize, b.dtype.itemsize, jnp.dtype(out_dtype).itemsize)
    nk = K // tk
    dims = (((0 if ta else 1,), (1 if tb else 0,)), ((), ()))
    direct = jnp.dtype(out_dtype) == jnp.dtype(F32)

    def body(a_ref, b_ref, o_ref, *scratch):
        k = pl.program_id(2)
        av = a_ref[...]
        if a_act is not None:
            av = a_act(av.astype(F32))
        part = lax.dot_general(av.astype(MXU_DTYPE), b_ref[...].astype(MXU_DTYPE), dims, preferred_element_type=F32)
        if nk == 1:
            o_ref[...] = part.astype(o_ref.dtype)
            return
        acc = o_ref if direct else scratch[0]

        @pl.when(k == 0)
        def _():
            acc[...] = part

        @pl.when(k > 0)
        def _():
            acc[...] += part

        if not direct:

            @pl.when(k == nk - 1)
            def _():
                o_ref[...] = acc[...].astype(o_ref.dtype)

    a_spec = pl.BlockSpec((tk, tm), lambda i, j, k: (k, i)) if ta else pl.BlockSpec((tm, tk), lambda i, j, k: (i, k))
    b_spec = pl.BlockSpec((tn, tk), lambda i, j, k: (j, k)) if tb else pl.BlockSpec((tk, tn), lambda i, j, k: (k, j))
    out = pl.pallas_call(
        body,
        name=name,
        grid=(M // tm, N // tn, nk),
        in_specs=[a_spec, b_spec],
        out_specs=pl.BlockSpec((tm, tn), lambda i, j, k: (i, j)),
        out_shape=jax.ShapeDtypeStruct((M, N), out_dtype),
        scratch_shapes=[] if (direct or nk == 1) else [pltpu.VMEM((tm, tn), F32)],
        compiler_params=_cparams(3),
    )(a, b)
    return out.reshape(*lead, N) if lead is not None else out


def _shift_down(u, d):
    if d == 0:
        return u
    t = lax.broadcasted_iota(jnp.int32, u.shape, 0)
    return jnp.where(t >= d, pltpu.roll(u, d, 0), 0.0)


def _shift_up(u, d):
    if d == 0:
        return u
    s = u.shape[0]
    t = lax.broadcasted_iota(jnp.int32, u.shape, 0)
    return jnp.where(t < s - d, pltpu.roll(u, s - d, 0), 0.0)


def _conv(u, w, b):
    k = w.shape[0]
    out = b + w[k - 1 : k, :] * u
    for j in range(k - 1):
        out = out + w[j : j + 1, :] * _shift_down(u, k - 1 - j)
    return out


def _conv_bwd(u, w, dc):
    k = w.shape[0]
    du = w[k - 1 : k, :] * dc
    dws = []
    for j in range(k - 1):
        du = du + w[j : j + 1, :] * _shift_up(dc, k - 1 - j)
        dws.append(jnp.sum(dc * _shift_down(u, k - 1 - j), axis=0, keepdims=True))
    dws.append(jnp.sum(dc * u, axis=0, keepdims=True))
    return du, jnp.concatenate(dws, axis=0), jnp.sum(dc, axis=0, keepdims=True)


def conv_silu_fwd(proj, w, b):
    B, S, _ = proj.shape
    k, c = w.shape
    blk0 = OFF_XBC // LANE

    def body(u_ref, w_ref, b_ref, o_ref):
        o_ref[0] = _silu(_conv(u_ref[0], w_ref[...], b_ref[...]))

    return pl.pallas_call(
        body,
        name="conv_silu_fwd",
        grid=(B, c // LANE),
        in_specs=[
            pl.BlockSpec((1, S, LANE), lambda bi, j: (bi, 0, blk0 + j)),
            pl.BlockSpec((k, LANE), lambda bi, j: (0, j)),
            pl.BlockSpec((1, LANE), lambda bi, j: (0, j)),
        ],
        out_specs=pl.BlockSpec((1, S, LANE), lambda bi, j: (bi, 0, j)),
        out_shape=jax.ShapeDtypeStruct((B, S, c), F32),
        compiler_params=_cparams(2),
    )(proj, w, b)


def conv_silu_bwd(proj, w, b, dout):
    B, S, _ = proj.shape
    k, c = w.shape
    blk0 = OFF_XBC // LANE

    def body(u_ref, w_ref, b_ref, d_ref, du_ref, dw_ref, db_ref):
        bi = pl.program_id(1)
        u, wv = u_ref[0], w_ref[...]
        cv = _conv(u, wv, b_ref[...])
        sg = jax.nn.sigmoid(cv)
        dc = d_ref[0] * (sg * (1.0 + cv * (1.0 - sg)))
        du, dw, db = _conv_bwd(u, wv, dc)
        du_ref[0] = du

        @pl.when(bi == 0)
        def _():
            dw_ref[...] = dw
            db_ref[...] = db

        @pl.when(bi > 0)
        def _():
            dw_ref[...] += dw
            db_ref[...] += db

    return pl.pallas_call(
        body,
        name="conv_silu_bwd",
        grid=(c // LANE, B),
        in_specs=[
            pl.BlockSpec((1, S, LANE), lambda j, bi: (bi, 0, blk0 + j)),
            pl.BlockSpec((k, LANE), lambda j, bi: (0, j)),
            pl.BlockSpec((1, LANE), lambda j, bi: (0, j)),
            pl.BlockSpec((1, S, LANE), lambda j, bi: (bi, 0, j)),
        ],
        out_specs=[
            pl.BlockSpec((1, S, LANE), lambda j, bi: (bi, 0, j)),
            pl.BlockSpec((k, LANE), lambda j, bi: (0, j)),
            pl.BlockSpec((1, LANE), lambda j, bi: (0, j)),
        ],
        out_shape=[
            jax.ShapeDtypeStruct((B, S, c), F32),
            jax.ShapeDtypeStruct((k, c), F32),
            jax.ShapeDtypeStruct((1, c), F32),
        ],
        compiler_params=_cparams(2),
    )(proj, w, b, dout)


def conv_glu_fwd(u, w, b):
    B, S, c2 = u.shape
    k = w.shape[0]

    def body(u_ref, w_ref, b_ref, o_ref):
        cv = _conv(u_ref[0], w_ref[...], b_ref[...])
        o_ref[0] = (_silu(cv[:, :LANE]) * cv[:, LANE:]).astype(o_ref.dtype)

    return pl.pallas_call(
        body,
        name="conv_glu_fwd",
        grid=(B, c2 // (2 * LANE)),
        in_specs=[
            pl.BlockSpec((1, S, 2 * LANE), lambda bi, j: (bi, 0, j)),
            pl.BlockSpec((k, 2 * LANE), lambda bi, j: (0, j)),
            pl.BlockSpec((1, 2 * LANE), lambda bi, j: (0, j)),
        ],
        out_specs=pl.BlockSpec((1, S, LANE), lambda bi, j: (bi, 0, j)),
        out_shape=jax.ShapeDtypeStruct((B, S, c2 // 2), MXU_DTYPE),
        compiler_params=_cparams(2),
    )(u, w, b)


def conv_glu_bwd(u, w, b, da):
    B, S, c2 = u.shape
    k = w.shape[0]

    def body(u_ref, w_ref, b_ref, d_ref, du_ref, dw_ref, db_ref):
        bi = pl.program_id(1)
        uv, wv = u_ref[0], w_ref[...]
        cv = _conv(uv, wv, b_ref[...])
        gate, val = cv[:, :LANE], cv[:, LANE:]
        sg = jax.nn.sigmoid(gate)
        dav = d_ref[0].astype(F32)
        dc = jnp.concatenate([dav * val * (sg * (1.0 + gate * (1.0 - sg))), dav * gate * sg], axis=1)
        du, dw, db = _conv_bwd(uv, wv, dc)
        du_ref[0] = du

        @pl.when(bi == 0)
        def _():
            dw_ref[...] = dw
            db_ref[...] = db

        @pl.when(bi > 0)
        def _():
            dw_ref[...] += dw
            db_ref[...] += db

    return pl.pallas_call(
        body,
        name="conv_glu_bwd",
        grid=(c2 // (2 * LANE), B),
        in_specs=[
            pl.BlockSpec((1, S, 2 * LANE), lambda j, bi: (bi, 0, j)),
            pl.BlockSpec((k, 2 * LANE), lambda j, bi: (0, j)),
            pl.BlockSpec((1, 2 * LANE), lambda j, bi: (0, j)),
            pl.BlockSpec((1, S, LANE), lambda j, bi: (bi, 0, j)),
        ],
        out_specs=[
            pl.BlockSpec((1, S, 2 * LANE), lambda j, bi: (bi, 0, j)),
            pl.BlockSpec((k, 2 * LANE), lambda j, bi: (0, j)),
            pl.BlockSpec((1, 2 * LANE), lambda j, bi: (0, j)),
        ],
        out_shape=[
            jax.ShapeDtypeStruct((B, S, c2), F32),
            jax.ShapeDtypeStruct((k, c2), F32),
            jax.ShapeDtypeStruct((1, c2), F32),
        ],
        compiler_params=_cparams(2),
    )(u, w, b, da)


N_PAIR = SSD_HEADS // SSD_GROUPS // 2


def _ssd_chunk(g, prev, xs, bm, cm, dtraw, dt_bias, a_log, d_skip):
    L = CHUNK
    lane = lax.broadcasted_iota(jnp.int32, (1, LANE), 1)
    sub = lax.broadcasted_iota(jnp.int32, (LANE, 1), 0)
    row = lax.broadcasted_iota(jnp.int32, (L, L), 0)
    col = lax.broadcasted_iota(jnp.int32, (L, L), 1)
    tri = (row >= col).astype(F32)
    first = lane < SSD_HEAD_DIM

    dt = _softplus(dtraw + dt_bias)
    a = dt * (-jnp.exp(a_log))
    acs = jnp.dot(tri, a, precision=HIGHEST, preferred_element_type=F32)
    acs_t = acs.T
    a_end = jnp.sum(a, axis=0, keepdims=True)
    cb = lax.dot_general(cm.astype(MXU_DTYPE), bm.astype(MXU_DTYPE), (((1,), (1,)), ((), ())), preferred_element_type=F32)

    def lane_of(v, h):
        return jnp.sum(jnp.where(lane == h, v, 0.0), axis=1, keepdims=True)

    def expand(v, ha):
        return jnp.where(first, lane_of(v, ha), lane_of(v, ha + 1))

    ys, news = [], []
    for j in range(N_PAIR):
        ha = g * (SSD_HEADS // SSD_GROUPS) + 2 * j
        x = xs[j]
        dt_e, acs_e, end_e = expand(dt, ha), expand(acs, ha), expand(a_end, ha)
        xdt = x * dt_e
        y = jnp.dot(cm.astype(MXU_DTYPE), prev[j].astype(MXU_DTYPE), preferred_element_type=F32) * jnp.exp(acs_e)
        st = lax.dot_general(
            bm.astype(MXU_DTYPE), (xdt * jnp.exp(end_e - acs_e)).astype(MXU_DTYPE), (((0,), (0,)), ((), ())), preferred_element_type=F32
        )
        news.append(prev[j] * jnp.exp(end_e) + st)
        for hh in range(2):
            h = ha + hh
            seg = lane_of(acs, h) - jnp.sum(jnp.where(sub == h, acs_t, 0.0), axis=0, keepdims=True)
            decay = jnp.exp(jnp.where(row >= col, seg, -jnp.inf))
            xh = jnp.where(first if hh == 0 else jnp.logical_not(first), xdt, 0.0)
            y = y + jnp.dot((cb * decay).astype(MXU_DTYPE), xh.astype(MXU_DTYPE), preferred_element_type=F32)
        ys.append(y + x * expand(d_skip, ha))
    return ys, news


def _ssd_specs(nc, order):
    gw = D_SSD // SSD_GROUPS
    bm0, cm0 = D_SSD // SSD_STATE, D_SSD // SSD_STATE + SSD_GROUPS
    par = pl.BlockSpec((1, LANE), lambda *ids: (0, 0))
    return [
        pl.BlockSpec((1, CHUNK, gw), lambda *ids: (order(*ids)[0], order(*ids)[2], order(*ids)[1])),
        pl.BlockSpec((1, CHUNK, SSD_STATE), lambda *ids: (order(*ids)[0], order(*ids)[2], bm0 + order(*ids)[1])),
        pl.BlockSpec((1, CHUNK, SSD_STATE), lambda *ids: (order(*ids)[0], order(*ids)[2], cm0 + order(*ids)[1])),
        pl.BlockSpec((1, CHUNK, LANE), lambda *ids: (order(*ids)[0], order(*ids)[2], OFF_DT // LANE)),
        par,
        par,
        par,
    ]


def ssd_fwd(xbc_act, proj, dt_bias, a_log, d_skip):
    B, S, _ = xbc_act.shape
    nc = S // CHUNK
    gw = D_SSD // SSD_GROUPS

    def body(xs_ref, bm_ref, cm_ref, dt_ref, db_ref, al_ref, ds_ref, y_ref, st_ref, state):
        g, c = pl.program_id(1), pl.program_id(2)

        @pl.when(c == 0)
        def _():
            state[...] = jnp.zeros_like(state)

        prev = [state[j] for j in range(N_PAIR)]
        for j in range(N_PAIR):
            st_ref[0, 0, 0, j] = prev[j]
        xs = [xs_ref[0, :, j * LANE : (j + 1) * LANE] for j in range(N_PAIR)]
        ys, news = _ssd_chunk(g, prev, xs, bm_ref[0], cm_ref[0], dt_ref[0], db_ref[...], al_ref[...], ds_ref[...])
        for j in range(N_PAIR):
            y_ref[0, :, j * LANE : (j + 1) * LANE] = ys[j]
            state[j] = news[j]

    return pl.pallas_call(
        body,
        name="ssd_fwd",
        grid=(B, SSD_GROUPS, nc),
        in_specs=_ssd_specs(nc, lambda b, g, c: (b, g, c)),
        out_specs=[
            pl.BlockSpec((1, CHUNK, gw), lambda b, g, c: (b, c, g)),
            pl.BlockSpec((1, 1, 1, N_PAIR, SSD_STATE, LANE), lambda b, g, c: (b, g, c, 0, 0, 0)),
        ],
        out_shape=[
            jax.ShapeDtypeStruct((B, S, D_SSD), F32),
            jax.ShapeDtypeStruct((B, SSD_GROUPS, nc, N_PAIR, SSD_STATE, LANE), F32),
        ],
        scratch_shapes=[pltpu.VMEM((N_PAIR, SSD_STATE, LANE), F32)],
        compiler_params=_cparams(3),
    )(xbc_act, xbc_act, xbc_act, proj, dt_bias, a_log, d_skip)


def ssd_bwd(xbc_act, proj, dt_bias, a_log, d_skip, states, dy):
    B, S, _ = xbc_act.shape
    nc = S // CHUNK
    gw = D_SSD // SSD_GROUPS
    order = lambda b, cr, g: (b, g, nc - 1 - cr)

    def body(xs_ref, bm_ref, cm_ref, dt_ref, db_ref, al_ref, ds_ref, st_ref, dy_ref,
             dxs_ref, dbm_ref, dcm_ref, ddt_ref, ddb_ref, dal_ref, dds_ref, dstate):
        b, cr, g = pl.program_id(0), pl.program_id(1), pl.program_id(2)

        @pl.when(cr == 0)
        def _():
            dstate[g] = jnp.zeros((N_PAIR, SSD_STATE, LANE), F32)

        prev = [st_ref[0, 0, 0, j] for j in range(N_PAIR)]
        xs = [xs_ref[0, :, j * LANE : (j + 1) * LANE] for j in range(N_PAIR)]

        def f(prev, xs, bm, cm, dtraw, dt_bias, a_log, d_skip):
            return _ssd_chunk(g, prev, xs, bm, cm, dtraw, dt_bias, a_log, d_skip)

        _, vjp = jax.vjp(f, prev, xs, bm_ref[0], cm_ref[0], dt_ref[0], db_ref[...], al_ref[...], ds_ref[...])
        ct_y = [dy_ref[0, :, j * LANE : (j + 1) * LANE] for j in range(N_PAIR)]
        ct_s = [dstate[g, j] for j in range(N_PAIR)]
        dprev, dxs, dbm, dcm, ddt, ddb, dal, dds = vjp((ct_y, ct_s))
        for j in range(N_PAIR):
            dstate[g, j] = dprev[j]
            dxs_ref[0, :, j * LANE : (j + 1) * LANE] = dxs[j]
        dbm_ref[0] = dbm
        dcm_ref[0] = dcm

        @pl.when(g == 0)
        def _():
            ddt_ref[0] = ddt

        @pl.when(g > 0)
        def _():
            ddt_ref[0] += ddt

        first = (b == 0) & (cr == 0) & (g == 0)

        @pl.when(first)
        def _():
            ddb_ref[...] = ddb
            dal_ref[...] = dal
            dds_ref[...] = dds

        @pl.when(jnp.logical_not(first))
        def _():
            ddb_ref[...] += ddb
            dal_ref[...] += dal
            dds_ref[...] += dds

    par = pl.BlockSpec((1, LANE), lambda *ids: (0, 0))
    in_specs = _ssd_specs(nc, order) + [
        pl.BlockSpec((1, 1, 1, N_PAIR, SSD_STATE, LANE), lambda b, cr, g: (b, g, nc - 1 - cr, 0, 0, 0)),
        pl.BlockSpec((1, CHUNK, gw), lambda b, cr, g: (b, nc - 1 - cr, g)),
    ]
    return pl.pallas_call(
        body,
        name="ssd_bwd",
        grid=(B, nc, SSD_GROUPS),
        in_specs=in_specs,
        out_specs=[
            pl.BlockSpec((1, CHUNK, gw), lambda b, cr, g: (b, nc - 1 - cr, g)),
            pl.BlockSpec((1, CHUNK, SSD_STATE), lambda b, cr, g: (b, nc - 1 - cr, g)),
            pl.BlockSpec((1, CHUNK, SSD_STATE), lambda b, cr, g: (b, nc - 1 - cr, g)),
            pl.BlockSpec((1, CHUNK, LANE), lambda b, cr, g: (b, nc - 1 - cr, 0)),
            par,
            par,
            par,
        ],
        out_shape=[
            jax.ShapeDtypeStruct((B, S, D_SSD), F32),
            jax.ShapeDtypeStruct((B, S, SSD_GROUPS * SSD_STATE), F32),
            jax.ShapeDtypeStruct((B, S, SSD_GROUPS * SSD_STATE), F32),
            jax.ShapeDtypeStruct((B, S, LANE), F32),
            jax.ShapeDtypeStruct((1, LANE), F32),
            jax.ShapeDtypeStruct((1, LANE), F32),
            jax.ShapeDtypeStruct((1, LANE), F32),
        ],
        scratch_shapes=[pltpu.VMEM((SSD_GROUPS, N_PAIR, SSD_STATE, LANE), F32)],
        compiler_params=_cparams(3),
    )(xbc_act, xbc_act, xbc_act, proj, dt_bias, a_log, d_skip, states, dy)


ATT_SCALE = (QK_NOPE + QK_ROPE) ** -0.5
ATT_TILE = 512


ATT_HEADS_PER_STEP = 2
LOG2E = 1.4426950408889634
Q_PRESCALE = ATT_SCALE * LOG2E


def fn_rope_q(q, cos_t, sin_t, rot):
    parts = []
    for h in range(MLA_HEADS):
        qr = q[:, h * Q_HEAD_PAD + LANE : (h + 1) * Q_HEAD_PAD]
        qr = qr * cos_t + jnp.dot(qr, rot, precision=HIGHEST, preferred_element_type=F32) * sin_t
        parts += [q[:, h * Q_HEAD_PAD : h * Q_HEAD_PAD + LANE] * Q_PRESCALE, qr * Q_PRESCALE]
    return (jnp.concatenate(parts, axis=1),)


def _head_cols(ref, hh, width):
    return ref[0, :, hh * width : (hh + 1) * width]


def _causal_mask(s):
    return jnp.where(lax.broadcasted_iota(jnp.int32, s.shape, 0) >= lax.broadcasted_iota(jnp.int32, s.shape, 1), s, -jnp.inf)


def _nt(a, b):
    return lax.dot_general(a, b, (((1,), (1,)), ((), ())), preferred_element_type=F32)


def _tn(a, b):
    return lax.dot_general(a, b, (((0,), (0,)), ((), ())), preferred_element_type=F32)


def _per_tile(idx, n, variant):
    for v in range(n):

        @pl.when(idx == v)
        def _(v=v):
            variant(v)


def attention_fwd(qs, kv, krr):
    B, S, _ = qs.shape
    t = min(ATT_TILE, S)
    n = S // t
    hps = ATT_HEADS_PER_STEP

    def body(q_ref, k_ref, kr_ref, o_ref, lse_ref):
        def variant(v):
            spans = ([(0, v * t, False)] if v else []) + [(v * t, (v + 1) * t, True)]
            for hh in range(hps):
                q = _head_cols(q_ref, hh, Q_HEAD_PAD)
                c0 = hh * 2 * LANE
                ss = []
                for a, b, diag in spans:
                    s = _nt(q, jnp.concatenate([k_ref[0, a:b, c0 : c0 + LANE], kr_ref[0, a:b]], axis=1))
                    ss.append(_causal_mask(s) if diag else s)
                m = functools.reduce(jnp.maximum, [jnp.max(s, axis=1, keepdims=True) for s in ss])
                ps = [jnp.exp2(s - m) for s in ss]
                l = functools.reduce(jnp.add, [jnp.sum(p, axis=1, keepdims=True) for p in ps])
                acc = functools.reduce(
                    jnp.add,
                    [jnp.dot(p.astype(MXU_DTYPE), k_ref[0, a:b, c0 + LANE : c0 + 2 * LANE], preferred_element_type=F32) for p, (a, b, _) in zip(ps, spans)],
                )
                o_ref[0, :, hh * V_DIM : (hh + 1) * V_DIM] = acc / l
                lse_ref[0, hh] = m + jnp.log2(l)

        _per_tile(pl.program_id(2), n, variant)

    return pl.pallas_call(
        body,
        name="attention_fwd",
        grid=(B, MLA_HEADS // hps, n),
        in_specs=[
            pl.BlockSpec((1, t, hps * Q_HEAD_PAD), lambda b, h, i: (b, i, h)),
            pl.BlockSpec((1, S, hps * 2 * LANE), lambda b, h, i: (b, 0, h)),
            pl.BlockSpec((1, S, LANE), lambda b, h, i: (b, 0, 0)),
        ],
        out_specs=[
            pl.BlockSpec((1, t, hps * V_DIM), lambda b, h, i: (b, i, h)),
            pl.BlockSpec((1, hps, t, 1), lambda b, h, i: (b, h, i, 0)),
        ],
        out_shape=[jax.ShapeDtypeStruct((B, S, D_ATT), F32), jax.ShapeDtypeStruct((B, MLA_HEADS, S, 1), F32)],
        compiler_params=_cparams(3),
    )(qs, kv, krr)


def attention_bwd_dq(qs, kv, krr, cos_t, sin_t, rot_t, o, lse, do):
    B, S, _ = qs.shape
    t = min(ATT_TILE, S)
    n = S // t
    hps = ATT_HEADS_PER_STEP

    def body(q_ref, c_ref, s_ref, rott_ref, k_ref, kr_ref, o_ref, lse_ref, do_ref, dq_ref):
        def variant(v):
            spans = ([(0, v * t, False)] if v else []) + [(v * t, (v + 1) * t, True)]
            for hh in range(hps):
                q = _head_cols(q_ref, hh, Q_HEAD_PAD)
                dov = _head_cols(do_ref, hh, V_DIM)
                delta = jnp.sum(_head_cols(o_ref, hh, V_DIM) * dov, axis=1, keepdims=True)
                dob = dov.astype(MXU_DTYPE)
                c0 = hh * 2 * LANE
                acc = None
                for a, b, diag in spans:
                    k = jnp.concatenate([k_ref[0, a:b, c0 : c0 + LANE], kr_ref[0, a:b]], axis=1)
                    s = _nt(q, k)
                    p = jnp.exp2((_causal_mask(s) if diag else s) - lse_ref[0, hh])
                    ds = p * (_nt(dob, k_ref[0, a:b, c0 + LANE : c0 + 2 * LANE]) - delta) * ATT_SCALE
                    part = jnp.dot(ds.astype(MXU_DTYPE), k, preferred_element_type=F32)
                    acc = part if acc is None else acc + part
                r = acc[:, LANE:]
                dr = r * c_ref[0] + jnp.dot(r * s_ref[0], rott_ref[...], precision=HIGHEST, preferred_element_type=F32)
                dq_ref[0, :, hh * Q_HEAD_PAD : (hh + 1) * Q_HEAD_PAD] = jnp.concatenate([acc[:, :LANE], dr], axis=1)

        _per_tile(pl.program_id(2), n, variant)

    return pl.pallas_call(
        body,
        name="attention_bwd_dq",
        grid=(B, MLA_HEADS // hps, n),
        in_specs=[
            pl.BlockSpec((1, t, hps * Q_HEAD_PAD), lambda b, h, i: (b, i, h)),
            pl.BlockSpec((1, t, LANE), lambda b, h, i: (b, i, 0)),
            pl.BlockSpec((1, t, LANE), lambda b, h, i: (b, i, 0)),
            pl.BlockSpec((LANE, LANE), lambda b, h, i: (0, 0)),
            pl.BlockSpec((1, S, hps * 2 * LANE), lambda b, h, i: (b, 0, h)),
            pl.BlockSpec((1, S, LANE), lambda b, h, i: (b, 0, 0)),
            pl.BlockSpec((1, t, hps * V_DIM), lambda b, h, i: (b, i, h)),
            pl.BlockSpec((1, hps, t, 1), lambda b, h, i: (b, h, i, 0)),
            pl.BlockSpec((1, t, hps * V_DIM), lambda b, h, i: (b, i, h)),
        ],
        out_specs=pl.BlockSpec((1, t, hps * Q_HEAD_PAD), lambda b, h, i: (b, i, h)),
        out_shape=jax.ShapeDtypeStruct(qs.shape, F32),
        compiler_params=_cparams(3),
    )(qs, cos_t, sin_t, rot_t, kv, krr, o, lse, do)


def attention_bwd_dkv(qs, kv, krr, o, lse, do):
    B, S, _ = qs.shape
    t = min(ATT_TILE, S)
    n = S // t
    hps = ATT_HEADS_PER_STEP

    def body(q_ref, k_ref, kr_ref, o_ref, lse_ref, do_ref, dkv_ref, dkr_ref):
        h = pl.program_id(2)

        def variant(v):
            spans = [(v * t, (v + 1) * t, True)] + ([((v + 1) * t, S, False)] if v + 1 < n else [])
            dkr = None
            for hh in range(hps):
                c0 = hh * 2 * LANE
                k = jnp.concatenate([k_ref[0, :, c0 : c0 + LANE], kr_ref[0]], axis=1)
                vals = k_ref[0, :, c0 + LANE : c0 + 2 * LANE]
                dk = dv = None
                for a, b, diag in spans:
                    q = q_ref[0, a:b, hh * Q_HEAD_PAD : (hh + 1) * Q_HEAD_PAD]
                    dov = do_ref[0, a:b, hh * V_DIM : (hh + 1) * V_DIM]
                    dob = dov.astype(MXU_DTYPE)
                    s = _nt(q, k)
                    p = jnp.exp2((_causal_mask(s) if diag else s) - lse_ref[0, hh, a:b])
                    delta = jnp.sum(o_ref[0, a:b, hh * V_DIM : (hh + 1) * V_DIM] * dov, axis=1, keepdims=True)
                    ds2 = p * (_nt(dob, vals) - delta) * (1.0 / LOG2E)
                    dv_part, dk_part = _tn(p.astype(MXU_DTYPE), dob), _tn(ds2.astype(MXU_DTYPE), q)
                    dv = dv_part if dv is None else dv + dv_part
                    dk = dk_part if dk is None else dk + dk_part
                dkv_ref[0, :, c0 : c0 + 2 * LANE] = jnp.concatenate([dk[:, :LANE], dv], axis=1)
                dkr = dk[:, LANE:] if dkr is None else dkr + dk[:, LANE:]

            @pl.when(h == 0)
            def _():
                dkr_ref[0] = dkr

            @pl.when(h > 0)
            def _():
                dkr_ref[0] += dkr

        _per_tile(pl.program_id(1), n, variant)

    return pl.pallas_call(
        body,
        name="attention_bwd_dkv",
        grid=(B, n, MLA_HEADS // hps),
        in_specs=[
            pl.BlockSpec((1, S, hps * Q_HEAD_PAD), lambda b, j, h: (b, 0, h)),
            pl.BlockSpec((1, t, hps * 2 * LANE), lambda b, j, h: (b, j, h)),
            pl.BlockSpec((1, t, LANE), lambda b, j, h: (b, j, 0)),
            pl.BlockSpec((1, S, hps * V_DIM), lambda b, j, h: (b, 0, h)),
            pl.BlockSpec((1, hps, S, 1), lambda b, j, h: (b, h, 0, 0)),
            pl.BlockSpec((1, S, hps * V_DIM), lambda b, j, h: (b, 0, h)),
        ],
        out_specs=[
            pl.BlockSpec((1, t, hps * 2 * LANE), lambda b, j, h: (b, j, h)),
            pl.BlockSpec((1, t, LANE), lambda b, j, h: (b, j, 0)),
        ],
        out_shape=[jax.ShapeDtypeStruct(kv.shape, F32), jax.ShapeDtypeStruct(krr.shape, F32)],
        compiler_params=_cparams(3),
    )(qs, kv, krr, o, lse, do)


_IN_SPLITS = np.cumsum([D_SSD, D_XBC, SSD_HEADS, Q_RANK, KV_RANK]).tolist()


def _pad_last(t, n):
    return jnp.pad(t, [(0, 0)] * (t.ndim - 1) + [(0, n - t.shape[-1])])


def win_to_kernel(w):
    z, xbc, dt, cq, ckv, kr = jnp.split(w, _IN_SPLITS, axis=-1)
    return jnp.concatenate([z, xbc, ckv, _pad_last(kr, LANE), _pad_last(dt, LANE), cq], axis=-1)


def win_from_kernel(g):
    z, xbc = g[..., :OFF_XBC], g[..., OFF_XBC:OFF_CKV]
    ckv, kr = g[..., OFF_CKV:OFF_KR], g[..., OFF_KR : OFF_KR + QK_ROPE]
    dt, cq = g[..., OFF_DT : OFF_DT + SSD_HEADS], g[..., OFF_CQ:]
    return jnp.concatenate([z, xbc, dt, cq, ckv, kr], axis=-1)


def wuq_to_kernel(w):
    w = w.reshape(*w.shape[:-1], MLA_HEADS, QK_NOPE + QK_ROPE)
    return _pad_last(w, Q_HEAD_PAD).reshape(*w.shape[:-2], MLA_HEADS * Q_HEAD_PAD)


def wuq_from_kernel(g):
    g = g.reshape(*g.shape[:-1], MLA_HEADS, Q_HEAD_PAD)[..., : QK_NOPE + QK_ROPE]
    return g.reshape(*g.shape[:-2], MLA_HEADS * (QK_NOPE + QK_ROPE))


def glu_to_kernel(w):
    lead = w.shape[:-1]
    w = w.reshape(*lead, 2, D_FF // LANE, LANE)
    return jnp.swapaxes(w, -3, -2).reshape(*lead, 2 * D_FF)


def glu_from_kernel(g):
    lead = g.shape[:-1]
    g = g.reshape(*lead, D_FF // LANE, 2, LANE)
    return jnp.swapaxes(g, -3, -2).reshape(*lead, 2 * D_FF)


def _lane_pad_row(v):
    return _pad_last(v, LANE)[None, :]


def rope_tables(positions):
    inv_freq = jnp.asarray(1.0 / (ROPE_BASE ** (np.arange(0, QK_ROPE, 2, dtype=np.float32) / QK_ROPE)))
    ang = positions.astype(F32)[..., None] * inv_freq
    cos, sin = jnp.cos(ang), jnp.sin(ang)
    zeros = jnp.zeros(cos.shape[:-1] + (LANE - QK_ROPE,), F32)
    rot = np.zeros((LANE, LANE), np.float32)
    half = QK_ROPE // 2
    for j in range(half):
        rot[j + half, j] = -1.0
        rot[j, j + half] = 1.0
    return jnp.concatenate([cos, cos, zeros], -1), jnp.concatenate([sin, sin, zeros], -1), jnp.asarray(rot), jnp.asarray(rot.T)


BIG = ("w_in", "w_uq", "w_ukv", "w_out", "w_up", "w_down")


def local_step(x, target, positions, mod_raw, W):
    B, S, D = x.shape
    cos_t, sin_t, rot, rot_t = rope_tables(positions)
    row = lambda v: v.reshape(1, -1)
    Ls = []
    for l in range(DEPTH):
        mods = [mod_raw[l, :, k][:, None, :] for k in range(6)]
        bias = [row(W["b_ada"][l, k * D : (k + 1) * D]) for k in range(6)]
        Ls.append(
            dict(
                mods=mods,
                bias=bias,
                w_in=win_to_kernel(W["w_in"][l]),
                w_uq=wuq_to_kernel(W["w_uq"][l]),
                w_ukv=W["w_ukv"][l],
                w_out=W["w_out"][l],
                w_up=glu_to_kernel(W["w_up"][l]),
                w_down=W["w_down"][l],
                conv_w=W["conv_w"][l],
                conv_b=row(W["conv_b"][l]),
                conv_ff_w=glu_to_kernel(W["conv_ff_w"][l]),
                conv_ff_b=row(glu_to_kernel(W["conv_ff_b"][l])),
                dt_bias=_lane_pad_row(W["dt_bias"][l]),
                a_log=_lane_pad_row(W["a_log"][l]),
                d_skip=_lane_pad_row(W["d_skip"][l]),
                norm_mix=row(W["norm_mix"][l]),
                ssd_norm=row(W["ssd_norm"][l]),
                q_norm=row(W["q_norm"][l]),
                kv_norm=row(W["kv_norm"][l]),
                attn_norm=row(W["attn_norm"][l]),
                norm_mlp=row(W["norm_mlp"][l]),
            )
        )
    fnorm = row(W["final_norm"])

    p0 = Ls[0]
    (h1,) = rowwise_fwd(
        "prenorm_fwd", fn_prenorm, [x], [p0["mods"][1], p0["mods"][0]], [p0["norm_mix"], p0["bias"][1], p0["bias"][0]], [(D, MXU_DTYPE)]
    )
    xin = x
    for l, p in enumerate(Ls):
        s = p["saved"] = dict(xin=xin, h1=h1)
        s["proj"] = proj = matmul(f"w_in_fwd{l}", h1, p["w_in"])
        s["xbc_act"] = xbc_act = conv_silu_fwd(proj, p["conv_w"], p["conv_b"])
        s["yscan"], s["states"] = ssd_fwd(xbc_act, proj, p["dt_bias"], p["a_log"], p["d_skip"])
        mla_in = [Col(proj, Q_RANK, OFF_CQ // Q_RANK), Col(proj, KV_RANK, OFF_CKV // KV_RANK), Col(proj, LANE, OFF_KR // LANE), cos_t, sin_t]
        s["cqn"], s["ckvn"], s["krr"] = rowwise_fwd(
            f"mla_prep_fwd{l}", fn_mla_prep, mla_in, [], [p["q_norm"], p["kv_norm"], rot], [(Q_RANK, MXU_DTYPE), (KV_RANK, MXU_DTYPE), (LANE, MXU_DTYPE)]
        )
        q = matmul(f"w_uq_fwd{l}", s["cqn"], p["w_uq"], out_dtype=MXU_DTYPE)
        (s["qs"],) = rowwise_fwd(f"rope_q_fwd{l}", fn_rope_q, [q, cos_t, sin_t], [], [rot], [(q.shape[-1], MXU_DTYPE)])
        s["kv"] = matmul(f"w_ukv_fwd{l}", s["ckvn"], p["w_ukv"], out_dtype=MXU_DTYPE)
        s["o"], s["lse"] = attention_fwd(s["qs"], s["kv"], s["krr"])
        (s["ycat"],) = rowwise_fwd(
            f"mix_fwd{l}", fn_mix, [s["yscan"], Col(proj, D_SSD, 0), s["o"]], [], [p["ssd_norm"], p["attn_norm"]], [(D_SSD + D_ATT, MXU_DTYPE)]
        )
        s["ymix"] = matmul(f"w_out_fwd{l}", s["ycat"], p["w_out"])
        s["x1"], s["h2"] = rowwise_fwd(
            f"mid_fwd{l}",
            fn_resid_prenorm,
            [xin, s["ymix"]],
            [p["mods"][2], p["mods"][4], p["mods"][3]],
            [p["norm_mlp"], p["bias"][2], p["bias"][4], p["bias"][3]],
            [(D, F32), (D, MXU_DTYPE)],
        )
        s["u"] = matmul(f"w_up_fwd{l}", s["h2"], p["w_up"])
        s["a"] = conv_glu_fwd(s["u"], p["conv_ff_w"], p["conv_ff_b"])
        s["ff"] = matmul(f"w_down_fwd{l}", s["a"], p["w_down"])
        if l + 1 < DEPTH:
            n = Ls[l + 1]
            xin, h1 = rowwise_fwd(
                f"join_fwd{l}",
                fn_resid_prenorm,
                [s["x1"], s["ff"]],
                [p["mods"][5], n["mods"][1], n["mods"][0]],
                [n["norm_mix"], p["bias"][5], n["bias"][1], n["bias"][0]],
                [(D, F32), (D, MXU_DTYPE)],
            )

    G = {k: [None] * DEPTH for k in W if k not in ("final_norm", "w_ada")}
    dmod = [[None] * 6 for _ in range(DEPTH)]
    dbias = [[None] * 6 for _ in range(DEPTH)]
    last = Ls[-1]
    sl = last["saved"]
    loss, dx1, dff, dmod[-1][5], G["final_norm"], dbias[-1][5] = final_fwdbwd(
        sl["x1"], sl["ff"], target, last["mods"][5], fnorm, last["bias"][5]
    )
    grad_x = None
    for l in reversed(range(DEPTH)):
        p = Ls[l]
        s = p["saved"]
        da = matmul(f"w_down_dgrad{l}", dff, p["w_down"], tb=True)
        G["w_down"][l] = matmul(f"w_down_wgrad{l}", s["a"], dff, ta=True)
        du, dcw, dcb = conv_glu_bwd(s["u"], p["conv_ff_w"], p["conv_ff_b"], da)
        G["conv_ff_w"][l], G["conv_ff_b"][l] = glu_from_kernel(dcw), glu_from_kernel(dcb)[0]
        dh2 = matmul(f"w_up_dgrad{l}", du, p["w_up"], tb=True)
        G["w_up"][l] = glu_from_kernel(matmul(f"w_up_wgrad{l}", s["h2"], du, ta=True))
        dxb, dymix, dmod[l][2], dmod[l][4], dmod[l][3], G["norm_mlp"][l], dbias[l][2], dbias[l][4], dbias[l][3] = rowwise_bwd(
            f"mid_bwd{l}",
            fn_resid_prenorm,
            [s["xin"], s["ymix"]],
            [p["mods"][2], p["mods"][4], p["mods"][3]],
            [p["norm_mlp"], p["bias"][2], p["bias"][4], p["bias"][3]],
            [dx1, dh2],
            [True, True],
            [True] * 4,
        )
        dycat = matmul(f"w_out_dgrad{l}", dymix, p["w_out"], tb=True)
        G["w_out"][l] = matmul(f"w_out_wgrad{l}", s["ycat"], dymix, ta=True)
        dyscan, dz, do, G["ssd_norm"][l], G["attn_norm"][l] = rowwise_bwd(
            f"mix_bwd{l}", fn_mix, [s["yscan"], Col(s["proj"], D_SSD, 0), s["o"]], [], [p["ssd_norm"], p["attn_norm"]], [dycat], [True] * 3, [True] * 2
        )
        dq = attention_bwd_dq(s["qs"], s["kv"], s["krr"], cos_t, sin_t, rot_t, s["o"], s["lse"], do)
        dkv, dkrr = attention_bwd_dkv(s["qs"], s["kv"], s["krr"], s["o"], s["lse"], do)
        dcqn = matmul(f"w_uq_dgrad{l}", dq, p["w_uq"], tb=True)
        G["w_uq"][l] = wuq_from_kernel(matmul(f"w_uq_wgrad{l}", s["cqn"], dq, ta=True))
        dckvn = matmul(f"w_ukv_dgrad{l}", dkv, p["w_ukv"], tb=True)
        G["w_ukv"][l] = matmul(f"w_ukv_wgrad{l}", s["ckvn"], dkv, ta=True)
        proj = s["proj"]
        mla_in = [Col(proj, Q_RANK, OFF_CQ // Q_RANK), Col(proj, KV_RANK, OFF_CKV // KV_RANK), Col(proj, LANE, OFF_KR // LANE), cos_t, sin_t]
        dcq, dckv, dkr, G["q_norm"][l], G["kv_norm"][l] = rowwise_bwd(
            f"mla_prep_bwd{l}",
            fn_mla_prep,
            mla_in,
            [],
            [p["q_norm"], p["kv_norm"], rot],
            [dcqn, dckvn, dkrr],
            [True, True, True, False, False],
            [True, True, False],
        )
        dxs, dbm, dcm, ddt, ddb, dal, dds = ssd_bwd(s["xbc_act"], proj, p["dt_bias"], p["a_log"], p["d_skip"], s["states"], dyscan)
        G["dt_bias"][l], G["a_log"][l], G["d_skip"][l] = ddb[0, :SSD_HEADS], dal[0, :SSD_HEADS], dds[0, :SSD_HEADS]
        dxbc, G["conv_w"][l], dcb = conv_silu_bwd(proj, p["conv_w"], p["conv_b"], jnp.concatenate([dxs, dbm, dcm], axis=-1))
        G["conv_b"][l] = dcb[0]
        dproj = jnp.concatenate([dz, dxbc, dckv, dkr, ddt, dcq], axis=-1)
        dh1 = matmul(f"w_in_dgrad{l}", dproj, p["w_in"], tb=True)
        G["w_in"][l] = win_from_kernel(matmul(f"w_in_wgrad{l}", s["h1"], dproj, ta=True))
        if l > 0:
            q = Ls[l - 1]
            sq = q["saved"]
            dx1, dff, dmod[l - 1][5], dmod[l][1], dmod[l][0], G["norm_mix"][l], dbias[l - 1][5], dbias[l][1], dbias[l][0] = rowwise_bwd(
                f"join_bwd{l - 1}",
                fn_resid_prenorm,
                [sq["x1"], sq["ff"]],
                [q["mods"][5], p["mods"][1], p["mods"][0]],
                [p["norm_mix"], q["bias"][5], p["bias"][1], p["bias"][0]],
                [dxb, dh1],
                [True, True],
                [True] * 4,
            )
        else:
            grad_x, dmod[0][1], dmod[0][0], G["norm_mix"][0], dbias[0][1], dbias[0][0] = rowwise_bwd(
                "prenorm_bwd",
                fn_prenorm,
                [x],
                [p["mods"][1], p["mods"][0]],
                [p["norm_mix"], p["bias"][1], p["bias"][0]],
                [dh1],
                [True],
                [True] * 3,
                adds={0: dxb},
            )
    for l in range(DEPTH):
        G["b_ada"][l] = jnp.concatenate([d[0] for d in dbias[l]])
        for k in ("norm_mix", "ssd_norm", "q_norm", "kv_norm", "attn_norm", "norm_mlp"):
            G[k][l] = G[k][l][0]
    grads = {k: (jnp.stack(v) if isinstance(v, list) else v[0]) for k, v in G.items()}
    dmod_raw = jnp.stack([jnp.stack([d[:, 0, :] for d in dmod[l]], axis=1) for l in range(DEPTH)])
    return loss, grad_x, grads, dmod_raw


MESH = pl.DeviceIdType.MESH
ANY = pl.BlockSpec(memory_space=pl.ANY)
PACK_W = 1024
PACK_TILE = 256


def _place():
    x, y, c = lax.axis_index("x"), lax.axis_index("y"), lax.axis_index("c")
    chips = [(1 - x, y), (x, 1 - y), (1 - x, 1 - y)]
    return x, y, c, chips


def _remote(src, dst, send_sem, recv_sem, to):
    return pltpu.make_async_remote_copy(src_ref=src, dst_ref=dst, send_sem=send_sem, recv_sem=recv_sem, device_id=to, device_id_type=MESH)


def all_gather8(name, v):
    m_per, n = v.shape

    def body(x_ref, out_ref, send_sems, recv_sems, local_sem):
        x, y, c, chips = _place()
        me, sibling = (x, y, c), (x, y, 1 - c)

        def rows(px, py, pc):
            return out_ref.at[pl.ds((4 * px + 2 * py + pc) * m_per, m_per), :]

        def copy(k, block, to, src=None):
            return _remote(rows(*block) if src is None else src, rows(*block), send_sems.at[k], recv_sems.at[k], to)

        mine = pltpu.make_async_copy(x_ref, rows(*me), local_sem)
        mine.start()
        first = [copy(0, me, sibling, src=x_ref)]
        first += [copy(1 + j, me, (*chip, c), src=x_ref) for j, chip in enumerate(chips)]
        for cp in first:
            cp.start()
        passed = [copy(4 + j, (*chip, c), sibling) for j, chip in enumerate(chips)]
        for j, chip in enumerate(chips):
            copy(1 + j, (*chip, c), me).wait_recv()
            passed[j].start()
        copy(0, sibling, me).wait_recv()
        for j, chip in enumerate(chips):
            copy(4 + j, (*chip, 1 - c), me).wait_recv()
        for cp in first + passed:
            cp.wait_send()
        mine.wait()

    return pl.pallas_call(
        body,
        name=name,
        out_shape=jax.ShapeDtypeStruct((N_DEV * m_per, n), v.dtype),
        in_specs=[pl.BlockSpec(memory_space=pltpu.VMEM)],
        out_specs=pl.BlockSpec(memory_space=pltpu.VMEM),
        scratch_shapes=[pltpu.SemaphoreType.DMA((7,)), pltpu.SemaphoreType.DMA((7,)), pltpu.SemaphoreType.DMA],
        compiler_params=pltpu.CompilerParams(vmem_limit_bytes=VMEM_LIMIT),
    )(v)


def gather_weights(pack):
    R, n = pack.shape
    rh = R // 2

    def body(x_ref, out_ref, send_sems, recv_sems):
        x, y, c, chips = _place()
        me = 2 * x + y

        def half(chip, hc):
            return out_ref.at[chip, pl.ds(hc * rh, rh), :]

        src = x_ref.at[pl.ds(c * rh, rh), :]
        first = [_remote(src, half(me, c), send_sems.at[j], recv_sems.at[j], (px, py, c)) for j, (px, py) in enumerate(chips)]
        for cp in first:
            cp.start()
        passed = []
        for j, (px, py) in enumerate(chips):
            got = half(2 * px + py, c)
            _remote(got, got, send_sems.at[j], recv_sems.at[j], (px, py, c)).wait_recv()
            cp = _remote(got, got, send_sems.at[3 + j], recv_sems.at[3 + j], (x, y, 1 - c))
            cp.start()
            passed.append(cp)
        for j, (px, py) in enumerate(chips):
            got = half(2 * px + py, 1 - c)
            _remote(got, got, send_sems.at[3 + j], recv_sems.at[3 + j], (x, y, 1 - c)).wait_recv()
        for cp in first + passed:
            cp.wait_send()

    return pl.pallas_call(
        body,
        name="gather_weights",
        out_shape=jax.ShapeDtypeStruct((N_CHIPS, R, n), pack.dtype),
        in_specs=[ANY],
        out_specs=ANY,
        scratch_shapes=[pltpu.SemaphoreType.DMA((6,)), pltpu.SemaphoreType.DMA((6,))],
    )(pack)


def swap_halves(g):
    n_slot, R, n = g.shape
    rh = R // 2

    def body(g_ref, got_ref, send_sem, recv_sem):
        x, y, c, _ = _place()
        cp = _remote(g_ref.at[:, pl.ds((1 - c) * rh, rh), :], got_ref, send_sem, recv_sem, (x, y, 1 - c))
        cp.start()
        cp.wait()

    return pl.pallas_call(
        body,
        name="swap_halves",
        out_shape=jax.ShapeDtypeStruct((n_slot, rh, n), g.dtype),
        in_specs=[ANY],
        out_specs=ANY,
        scratch_shapes=[pltpu.SemaphoreType.DMA, pltpu.SemaphoreType.DMA],
    )(g)


def scatter_chips(buf):
    def body(s_ref, got_ref, send_sems, recv_sems):
        x, y, c, chips = _place()
        me = 2 * x + y
        cps = [_remote(s_ref.at[2 * px + py], got_ref.at[me], send_sems.at[j], recv_sems.at[j], (px, py, c)) for j, (px, py) in enumerate(chips)]
        for cp in cps:
            cp.start()
        for j, (px, py) in enumerate(chips):
            got = got_ref.at[2 * px + py]
            _remote(got, got, send_sems.at[j], recv_sems.at[j], (px, py, c)).wait_recv()
        for cp in cps:
            cp.wait_send()

    return pl.pallas_call(
        body,
        name="scatter_chips",
        out_shape=jax.ShapeDtypeStruct(buf.shape, buf.dtype),
        in_specs=[ANY],
        out_specs=ANY,
        scratch_shapes=[pltpu.SemaphoreType.DMA((3,)), pltpu.SemaphoreType.DMA((3,))],
    )(buf)


def swap_with_sibling(h):
    def body(h_ref, got_ref, send_sem, recv_sem):
        x, y, c, _ = _place()
        cp = _remote(h_ref, got_ref, send_sem, recv_sem, (x, y, 1 - c))
        cp.start()
        cp.wait()

    return pl.pallas_call(
        body,
        name="swap_with_sibling",
        out_shape=jax.ShapeDtypeStruct(h.shape, h.dtype),
        in_specs=[ANY],
        out_specs=ANY,
        scratch_shapes=[pltpu.SemaphoreType.DMA, pltpu.SemaphoreType.DMA],
    )(h)


def chip_sum(g, got, core, chip):
    n_slot, R, n = g.shape
    rh = R // 2
    nb = rh // PACK_TILE

    def body(pos, g_ref, got_ref, wire_ref, own_ref):
        k = pl.program_id(1)
        s = g_ref[0] + got_ref[0]
        wire_ref[0] = s.astype(wire_ref.dtype)

        @pl.when(k == pos[1])
        def _():
            own_ref[...] = s

    grid_spec = pltpu.PrefetchScalarGridSpec(
        num_scalar_prefetch=1,
        grid=(nb, n_slot),
        in_specs=[
            pl.BlockSpec((1, PACK_TILE, n), lambda i, k, pos: (k, pos[0] * nb + i, 0)),
            pl.BlockSpec((1, PACK_TILE, n), lambda i, k, pos: (k, i, 0)),
        ],
        out_specs=[
            pl.BlockSpec((1, PACK_TILE, n), lambda i, k, pos: (k, i, 0)),
            pl.BlockSpec((PACK_TILE, n), lambda i, k, pos: (i, 0)),
        ],
    )
    return pl.pallas_call(
        body,
        name="chip_sum",
        grid_spec=grid_spec,
        out_shape=[jax.ShapeDtypeStruct((n_slot, rh, n), WIRE_DTYPE), jax.ShapeDtypeStruct((rh, n), F32)],
        compiler_params=_cparams(2),
    )(jnp.stack([core, chip]).astype(jnp.int32), g, got)


def mesh_sum(own, got, chip):
    rh, n = own.shape
    n_slot = got.shape[0]

    def body(pos, own_ref, *refs):
        out_ref = refs[-1]
        acc = own_ref[...]
        for k in range(n_slot):
            acc = acc + jnp.where(k != pos[0], refs[k][0].astype(F32), 0.0)
        out_ref[...] = acc

    grid_spec = pltpu.PrefetchScalarGridSpec(
        num_scalar_prefetch=1,
        grid=(rh // PACK_TILE,),
        in_specs=[pl.BlockSpec((PACK_TILE, n), lambda i, pos: (i, 0))]
        + [pl.BlockSpec((1, PACK_TILE, n), lambda i, pos, k=k: (k, i, 0)) for k in range(n_slot)],
        out_specs=pl.BlockSpec((PACK_TILE, n), lambda i, pos: (i, 0)),
    )
    return pl.pallas_call(
        body, name="mesh_sum", grid_spec=grid_spec, out_shape=jax.ShapeDtypeStruct((rh, n), F32), compiler_params=_cparams(1)
    )(jnp.stack([chip]).astype(jnp.int32), own, *([got] * n_slot))


def sum_devices(name, v):
    def body(v_ref, o_ref):
        acc = v_ref[0]
        for d in range(1, N_DEV):
            acc = acc + v_ref[d]
        o_ref[...] = acc

    return pl.pallas_call(body, name=name, out_shape=jax.ShapeDtypeStruct(v.shape[1:], F32))(v)


def adamw(name, w, g, m, v):
    shape = w.shape
    w2, g2, m2, v2 = (t.reshape(-1, shape[-1]) for t in (w, g, m, v))
    rows, n = w2.shape
    tr = _pick(rows, (256, 128, 64, 32, 16, 8))
    c1 = 1.0 / (1.0 - ADAM_B1**ADAM_STEP)
    c2 = 1.0 / (1.0 - ADAM_B2**ADAM_STEP)

    def body(w_ref, g_ref, m_ref, v_ref, d_ref, nm_ref, nv_ref):
        gv = g_ref[...]
        nm = ADAM_B1 * m_ref[...] + (1.0 - ADAM_B1) * gv
        nv = ADAM_B2 * v_ref[...] + (1.0 - ADAM_B2) * jnp.square(gv)
        d_ref[...] = -ADAM_LR * ((nm * c1) / (jnp.sqrt(nv * c2) + ADAM_EPS) + ADAM_WD * w_ref[...])
        nm_ref[...] = nm
        nv_ref[...] = nv

    spec = pl.BlockSpec((tr, n), lambda i: (i, 0))
    outs = pl.pallas_call(
        body,
        name=name,
        grid=(rows // tr,),
        in_specs=[spec] * 4,
        out_specs=[spec] * 3,
        out_shape=[jax.ShapeDtypeStruct((rows, n), F32)] * 3,
        compiler_params=_cparams(1),
    )(w2, g2, m2, v2)
    return [o.reshape(shape) for o in outs]


def ada_fwd(c_all, w_ada):
    n_tok, d = c_all.shape
    depth, _, cols = w_ada.shape
    tn = _pick(cols, (512, 384, 256, 128))

    def body(c_ref, w_ref, o_ref):
        o_ref[0] = jnp.dot(_silu(c_ref[...]).astype(MXU_DTYPE), w_ref[0].astype(MXU_DTYPE), preferred_element_type=F32)

    return pl.pallas_call(
        body,
        name="ada_fwd",
        grid=(depth, cols // tn),
        in_specs=[pl.BlockSpec((n_tok, d), lambda l, j: (0, 0)), pl.BlockSpec((1, d, tn), lambda l, j: (l, 0, j))],
        out_specs=pl.BlockSpec((1, n_tok, tn), lambda l, j: (l, 0, j)),
        out_shape=jax.ShapeDtypeStruct((depth, n_tok, cols), F32),
        compiler_params=_cparams(2),
    )(c_all, w_ada)


COL_SHARDED = ("w_in", "w_uq", "w_ukv", "w_up")


def _pack_rows(parts, rows):
    flat = jnp.concatenate([p.reshape(-1, PACK_W) for p in parts], axis=0)
    return jnp.pad(flat, ((0, rows - flat.shape[0]), (0, 0)))


def _big_rows(shards):
    n = sum(int(np.prod(shards[k].shape)) for k in BIG) // PACK_W
    return -(-n // (2 * PACK_TILE)) * 2 * PACK_TILE


def unpack_gathered(gathered, shard_shapes):
    out, r0 = {}, 0
    for k in BIG:
        shp = shard_shapes[k]
        n = int(np.prod(shp)) // PACK_W
        seg = gathered[:, r0 : r0 + n].reshape(N_CHIPS, *shp)
        r0 += n
        if k in COL_SHARDED:
            out[k] = seg.transpose(1, 2, 0, 3).reshape(shp[0], shp[1], N_CHIPS * shp[2])
        else:
            out[k] = seg.transpose(1, 0, 2, 3).reshape(shp[0], N_CHIPS * shp[1], shp[2])
    return out


def pack_full_grads(grads, shard_shapes, rows):
    parts = []
    for k in BIG:
        shp = shard_shapes[k]
        g = grads[k]
        if k in COL_SHARDED:
            g = g.reshape(shp[0], shp[1], N_CHIPS, shp[2]).transpose(2, 0, 1, 3)
        else:
            g = g.reshape(shp[0], N_CHIPS, shp[1], shp[2]).transpose(1, 0, 2, 3)
        parts.append(g.reshape(N_CHIPS, -1, PACK_W))
    flat = jnp.concatenate(parts, axis=1)
    return jnp.pad(flat, ((0, 0), (0, rows - flat.shape[1]), (0, 0)))


def unpack_shards(buf, shard_shapes):
    out, r0 = {}, 0
    for k in BIG:
        n = int(np.prod(shard_shapes[k])) // PACK_W
        out[k] = buf[r0 : r0 + n].reshape(shard_shapes[k])
        r0 += n
    return out


def _flat_pack(arrs, row_multiple=8):
    flat = jnp.concatenate([a.reshape(-1) for a in arrs])
    per = PACK_W * row_multiple
    n = -(-flat.shape[0] // per) * per
    return jnp.pad(flat, (0, n - flat.shape[0])).reshape(-1, PACK_W)


def _flat_unpack(buf, shapes):
    flat, out, o = buf.reshape(-1), [], 0
    for s in shapes:
        n = int(np.prod(s))
        out.append(flat[o : o + n].reshape(s))
        o += n
    return out


WEIGHTS = (
    "w_ada", "b_ada", "norm_mix", "w_in", "conv_w", "conv_b", "dt_bias", "a_log", "d_skip", "ssd_norm", "q_norm", "w_uq",
    "kv_norm", "w_ukv", "attn_norm", "w_out", "norm_mlp", "w_up", "conv_ff_w", "conv_ff_b", "w_down", "final_norm",
)
REPLICATED = ("b_ada", "norm_mix", "conv_b", "dt_bias", "a_log", "d_skip", "ssd_norm", "q_norm", "kv_norm", "attn_norm",
              "norm_mlp", "conv_ff_b", "final_norm")
CONV_SHARDED = ("conv_w", "conv_ff_w")


def kernel(x, c, positions, w_ada, b_ada, norm_mix, w_in, conv_w, conv_b, dt_bias, a_log, d_skip, ssd_norm, q_norm, w_uq, kv_norm, w_ukv, attn_norm, w_out, norm_mlp, w_up, conv_ff_w, conv_ff_b, w_down, final_norm, loss_target, m_w_ada, m_b_ada, m_norm_mix, m_w_in, m_conv_w, m_conv_b, m_dt_bias, m_a_log, m_d_skip, m_ssd_norm, m_q_norm, m_w_uq, m_kv_norm, m_w_ukv, m_attn_norm, m_w_out, m_norm_mlp, m_w_up, m_conv_ff_w, m_conv_ff_b, m_w_down, m_final_norm, v_w_ada, v_b_ada, v_norm_mix, v_w_in, v_conv_w, v_conv_b, v_dt_bias, v_a_log, v_d_skip, v_ssd_norm, v_q_norm, v_w_uq, v_kv_norm, v_w_ukv, v_attn_norm, v_w_out, v_norm_mlp, v_w_up, v_conv_ff_w, v_conv_ff_b, v_w_down, v_final_norm):
    loc = locals()
    Wl = {k: loc[k] for k in WEIGHTS}
    Ml = {k: loc["m_" + k] for k in WEIGHTS}
    Vl = {k: loc["v_" + k] for k in WEIGHTS}
    xi, yi, ci = lax.axis_index("x"), lax.axis_index("y"), lax.axis_index("c")
    chip = 2 * xi + yi
    dev = 2 * chip + ci
    B, S, D = x.shape
    n_tok = N_DEV * B

    c_all = all_gather8("gather_c", c.reshape(8, -1)).reshape(n_tok, D)
    mod_cols = ada_fwd(c_all, w_ada)
    cols = mod_cols.shape[-1]
    half = n_tok // 2
    mod_mine = lax.dynamic_slice_in_dim(mod_cols, ci * half, half, axis=1)
    small_in = _flat_pack([mod_mine, conv_w, conv_ff_w])
    n_mod = mod_mine.size // PACK_W
    small_all = all_gather8("gather_mod", small_in).reshape(N_CHIPS, 2, -1, PACK_W)
    mod_all = small_all[:, :, :n_mod].reshape(N_CHIPS, 2, DEPTH, half, cols).transpose(2, 1, 3, 0, 4).reshape(DEPTH, n_tok, N_CHIPS * cols)
    mod_raw = lax.dynamic_slice_in_dim(mod_all, dev * B, B, axis=1).reshape(DEPTH, B, 6, D)
    conv_parts = [_flat_unpack(small_all[k, 0, n_mod:], [conv_w.shape, conv_ff_w.shape]) for k in range(N_CHIPS)]
    conv_full = {name: jnp.concatenate([conv_parts[k][i] for k in range(N_CHIPS)], axis=-1) for i, name in enumerate(CONV_SHARDED)}

    shard_shapes = {k: Wl[k].shape for k in BIG}
    rows = _big_rows(Wl)
    pack = _pack_rows([Wl[k].astype(MXU_DTYPE) for k in BIG], rows)
    gathered = lax.dynamic_update_slice_in_dim(gather_weights(pack), pack[None], chip, axis=0)
    W = unpack_gathered(gathered, shard_shapes)
    W.update({k: Wl[k] for k in REPLICATED})
    W.update(conv_full)

    loss_lanes, grad_x, grads, dmod_raw = local_step(x, loss_target, positions, mod_raw, W)
    loss = lax.psum(loss_lanes[0, 0], ("x", "y", "c"))

    gpack = pack_full_grads(grads, shard_shapes, rows)
    wire, own = chip_sum(gpack, swap_halves(gpack), ci, chip)
    mine = mesh_sum(own, scatter_chips(wire), chip)
    theirs = swap_with_sibling(mine)
    both = jnp.concatenate([jnp.where(ci == 0, mine, theirs), jnp.where(ci == 0, theirs, mine)], axis=0)
    g_shard = unpack_shards(both, shard_shapes)

    small_names = REPLICATED + CONV_SHARDED
    small_out = _flat_pack([grads[k] for k in small_names] + [dmod_raw])
    small_got = all_gather8("gather_small", small_out).reshape(N_DEV, -1, PACK_W)
    small_sum = _flat_unpack(sum_devices("sum_small", small_got), [grads[k].shape for k in small_names])
    G = dict(zip(small_names, small_sum))
    for name in CONV_SHARDED:
        width = Wl[name].shape[-1]
        G[name] = lax.dynamic_slice_in_dim(G[name], chip * width, width, axis=-1)
    n_small = sum(grads[k].size for k in small_names)
    dmod_all = small_got.reshape(N_DEV, -1)[:, n_small : n_small + dmod_raw.size].reshape(N_DEV, DEPTH, B, 6 * D)
    dmod_all = dmod_all.transpose(1, 0, 2, 3).reshape(DEPTH, n_tok, 6 * D)
    dmod_cols = lax.dynamic_slice_in_dim(dmod_all, chip * cols, cols, axis=-1)
    G["w_ada"] = jnp.stack([matmul(f"w_ada_wgrad{l}", c_all, dmod_cols[l], ta=True, a_act=_silu) for l in range(DEPTH)])
    G.update(g_shard)

    deltas, new_m, new_v = {}, {}, {}
    small_upd = adamw("adamw_small", *[_flat_pack([t[k] for k in REPLICATED]) for t in (Wl, G, Ml, Vl)])
    for res, t in zip((deltas, new_m, new_v), small_upd):
        res.update(zip(REPLICATED, _flat_unpack(t, [Wl[k].shape for k in REPLICATED])))
    for k in WEIGHTS:
        if k not in REPLICATED:
            deltas[k], new_m[k], new_v[k] = adamw("adamw_" + k, Wl[k], G[k], Ml[k], Vl[k])
    return (loss, grad_x, *[G[k] for k in WEIGHTS], *[deltas[k] for k in WEIGHTS], *[new_m[k] for k in WEIGHTS], *[new_v[k] for k in WEIGHTS])
```

```python
import functools
from typing import NamedTuple

import numpy as np
import jax
import jax.numpy as jnp
from jax import lax
from jax.experimental import pallas as pl
from jax.experimental.pallas import tpu as pltpu

F32 = jnp.float32
BF16 = jnp.bfloat16
MXU_DTYPE = jnp.bfloat16
WIRE_DTYPE = jnp.bfloat16
HIGHEST = lax.Precision.HIGHEST

D_MODEL = 1024
DEPTH = 2
D_SSD = 1024
SSD_HEADS = 16
SSD_HEAD_DIM = 64
SSD_GROUPS = 2
SSD_STATE = 128
SSD_CONV = 4
CHUNK = 128
MLA_HEADS = 8
QK_NOPE = 128
QK_ROPE = 64
V_DIM = 128
D_ATT = MLA_HEADS * V_DIM
Q_RANK = 384
KV_RANK = 256
ROPE_BASE = 10000.0
D_FF = 2816
FF_CONV = 3
EPS = 1e-6
D_XBC = D_SSD + 2 * SSD_GROUPS * SSD_STATE
D_IN = D_SSD + D_XBC + SSD_HEADS + Q_RANK + KV_RANK + QK_ROPE
ADAM_LR, ADAM_B1, ADAM_B2, ADAM_EPS, ADAM_WD, ADAM_STEP = 0.001, 0.9, 0.999, 1e-08, 0.01, 10

LANE = 128
N_CHIPS = 4
N_DEV = 8

OFF_Z, OFF_XBC, OFF_CKV, OFF_KR, OFF_DT, OFF_CQ = 0, 1024, 2560, 2816, 2944, 3072
N_PROJ = 3456
Q_HEAD_PAD = 256
VMEM_LIMIT = 56 * 1024 * 1024


def _cparams(n_axes):
    return pltpu.CompilerParams(dimension_semantics=("arbitrary",) * n_axes, vmem_limit_bytes=VMEM_LIMIT)


def _pick(n, prefs):
    for p in prefs:
        if n % p == 0:
            return p
    return n


def _silu(x):
    return x * jax.nn.sigmoid(x)


def _rms(x, g):
    return x * lax.rsqrt(jnp.mean(x * x, axis=-1, keepdims=True) + EPS) * g


def _softplus(x):
    return jnp.maximum(x, 0.0) + jnp.log(1.0 + jnp.exp(-jnp.abs(x)))


class Col(NamedTuple):
    arr: jax.Array
    width: int
    blk: int


def _as_col(a):
    return a if isinstance(a, Col) else Col(a, a.shape[-1], 0)


def _row_specs(tiled, pbatch, shared, tile):
    specs = [pl.BlockSpec((1, tile, t.width), lambda b, i, blk=t.blk: (b, i, blk)) for t in tiled]
    specs += [pl.BlockSpec((1, 1, p.shape[-1]), lambda b, i: (b, 0, 0)) for p in pbatch]
    specs += [pl.BlockSpec(s.shape, lambda b, i: (0, 0)) for s in shared]
    return specs


def _row_vals(refs, nt, npb):
    return [r[0].astype(F32) for r in refs[: nt + npb]] + [r[...].astype(F32) for r in refs[nt + npb :]]


def rowwise_fwd(name, fn, tiled, pbatch, shared, outs, tile=512):
    tiled = [_as_col(t) for t in tiled]
    B, S = tiled[0].arr.shape[:2]
    tile = min(tile, S)
    nt, npb, nsh = len(tiled), len(pbatch), len(shared)
    n_in = nt + npb + nsh

    def body(*refs):
        res = fn(*_row_vals(refs[:n_in], nt, npb))
        for r, v in zip(refs[n_in:], res):
            r[0] = v.astype(r.dtype)

    return pl.pallas_call(
        body,
        name=name,
        grid=(B, S // tile),
        in_specs=_row_specs(tiled, pbatch, shared, tile),
        out_specs=[pl.BlockSpec((1, tile, w), lambda b, i: (b, i, 0)) for w, _ in outs],
        out_shape=[jax.ShapeDtypeStruct((B, S, w), dt) for w, dt in outs],
        compiler_params=_cparams(2),
    )(*[t.arr for t in tiled], *pbatch, *shared)


def rowwise_bwd(name, fn, tiled, pbatch, shared, cts, grad_tiled, grad_shared, adds=None, tile=256, mxu_only=()):
    tiled = [_as_col(t) for t in tiled]
    adds = adds or {}
    B, S = tiled[0].arr.shape[:2]
    tile = min(tile, S)
    nt, npb, nsh = len(tiled), len(pbatch), len(shared)
    n_in = nt + npb + nsh
    add_idx = sorted(adds)
    gt = [i for i in range(nt) if grad_tiled[i]]
    gs = [i for i in range(nsh) if grad_shared[i]]
    diff = gt + [nt + i for i in range(npb)] + [nt + npb + i for i in gs]
    n_ct, n_add = len(cts), len(add_idx)

    def body(*refs):
        vals = _row_vals(refs[:n_in], nt, npb)
        ct_v = tuple(r[0].astype(F32) for r in refs[n_in : n_in + n_ct])
        add_v = {i: r[0].astype(F32) for i, r in zip(add_idx, refs[n_in + n_ct : n_in + n_ct + n_add])}
        out_refs = refs[n_in + n_ct + n_add :]

        def f(*dargs):
            full = list(vals)
            for k, i in enumerate(diff):
                full[i] = dargs[k]
            return tuple(fn(*full))

        _, vjp = jax.vjp(f, *[vals[i] for i in diff])
        grads = vjp(ct_v)
        b, i = pl.program_id(0), pl.program_id(1)
        k = 0
        for idx in gt:
            g = grads[k]
            if idx in add_v:
                g = g + add_v[idx]
            out_refs[k][0] = g.astype(out_refs[k].dtype)
            k += 1
        for _ in range(npb):
            r, g = out_refs[k], grads[k]

            @pl.when(i == 0)
            def _(r=r, g=g):
                r[0] = g

            @pl.when(i > 0)
            def _(r=r, g=g):
                r[0] += g

            k += 1
        for _ in gs:
            r, g = out_refs[k], grads[k]

            @pl.when((i == 0) & (b == 0))
            def _(r=r, g=g):
                r[...] = g

            @pl.when((i > 0) | (b > 0))
            def _(r=r, g=g):
                r[...] += g

            k += 1

    out_specs = [pl.BlockSpec((1, tile, tiled[i].width), lambda b, i: (b, i, 0)) for i in gt]
    out_shape = [jax.ShapeDtypeStruct((B, S, tiled[i].width), MXU_DTYPE if i in mxu_only else F32) for i in gt]
    out_specs += [pl.BlockSpec((1, 1, p.shape[-1]), lambda b, i: (b, 0, 0)) for p in pbatch]
    out_shape += [jax.ShapeDtypeStruct(p.shape, F32) for p in pbatch]
    out_specs += [pl.BlockSpec(shared[i].shape, lambda b, i: (0, 0)) for i in gs]
    out_shape += [jax.ShapeDtypeStruct(shared[i].shape, F32) for i in gs]
    in_specs = _row_specs(tiled, pbatch, shared, tile)
    in_specs += [pl.BlockSpec((1, tile, c.shape[-1]), lambda b, i: (b, i, 0)) for c in cts]
    in_specs += [pl.BlockSpec((1, tile, adds[i].shape[-1]), lambda b, i: (b, i, 0)) for i in add_idx]
    return pl.pallas_call(
        body,
        name=name,
        grid=(B, S // tile),
        in_specs=in_specs,
        out_specs=out_specs,
        out_shape=out_shape,
        compiler_params=_cparams(2),
    )(*[t.arr for t in tiled], *pbatch, *shared, *cts, *[adds[i] for i in add_idx])


def fn_prenorm(x, sc, sh, g, bsc, bsh):
    return (_rms(x, g) * (1.0 + sc + bsc) + (sh + bsh),)


def fn_resid_prenorm(x, y, gate, sc, sh, g, bgate, bsc, bsh):
    x1 = x + (gate + bgate) * y
    return x1, _rms(x1, g) * (1.0 + sc + bsc) + (sh + bsh)


def fn_mix(yscan, z, o, ssd_norm, attn_norm):
    y = yscan * _silu(z)
    half = D_SSD // SSD_GROUPS
    first = lax.broadcasted_iota(jnp.int32, y.shape, 1) < half
    sq = y * y
    m0 = jnp.sum(jnp.where(first, sq, 0.0), axis=-1, keepdims=True) / half
    m1 = jnp.sum(jnp.where(first, 0.0, sq), axis=-1, keepdims=True) / half
    r = jnp.where(first, lax.rsqrt(m0 + EPS), lax.rsqrt(m1 + EPS))
    return (jnp.concatenate([y * r * ssd_norm, _rms(o, attn_norm)], axis=-1),)


def fn_mla_prep(cq, ckv, kr, cos_t, sin_t, q_norm, kv_norm, rot):
    krr = kr * cos_t + jnp.dot(kr, rot, precision=HIGHEST, preferred_element_type=F32) * sin_t
    return _rms(cq, q_norm), _rms(ckv, kv_norm), krr


def final_fwdbwd(x1, ff, target, gate, fnorm, bgate, tile=256):
    B, S, D = x1.shape
    tile = min(tile, S)

    def body(x_ref, f_ref, t_ref, g_ref, n_ref, bg_ref, loss_ref, dx_ref, df_ref, dg_ref, dn_ref, dbg_ref):
        b, i = pl.program_id(0), pl.program_id(1)
        tgt = t_ref[0]

        def f(x, y, gate, fn, bg):
            yf = _rms(x + (gate + bg) * y, fn)
            return 0.5 * jnp.sum(jnp.mean(jnp.square(yf - tgt), axis=-1, keepdims=True), axis=0, keepdims=True)

        val, vjp = jax.vjp(f, x_ref[0], f_ref[0], g_ref[0], n_ref[...], bg_ref[...])
        dx, dff, dg, dn, dbg = vjp(jnp.ones((1, 1), F32))
        dx_ref[0] = dx
        df_ref[0] = dff.astype(df_ref.dtype)
        lane_loss = jnp.broadcast_to(val, (1, LANE))

        @pl.when(i == 0)
        def _():
            dg_ref[0] = dg

        @pl.when(i > 0)
        def _():
            dg_ref[0] += dg

        @pl.when((i == 0) & (b == 0))
        def _():
            dn_ref[...] = dn
            dbg_ref[...] = dbg
            loss_ref[...] = lane_loss

        @pl.when((i > 0) | (b > 0))
        def _():
            dn_ref[...] += dn
            dbg_ref[...] += dbg
            loss_ref[...] += lane_loss

    tok = pl.BlockSpec((1, tile, D), lambda b, i: (b, i, 0))
    pb = pl.BlockSpec((1, 1, D), lambda b, i: (b, 0, 0))
    sh = pl.BlockSpec((1, D), lambda b, i: (0, 0))
    return pl.pallas_call(
        body,
        name="final_loss",
        grid=(B, S // tile),
        in_specs=[tok, tok, tok, pb, sh, sh],
        out_specs=[pl.BlockSpec((1, LANE), lambda b, i: (0, 0)), tok, tok, pb, sh, sh],
        out_shape=[
            jax.ShapeDtypeStruct((1, LANE), F32),
            jax.ShapeDtypeStruct((B, S, D), F32),
            jax.ShapeDtypeStruct((B, S, D), MXU_DTYPE),
            jax.ShapeDtypeStruct((B, 1, D), F32),
            jax.ShapeDtypeStruct((1, D), F32),
            jax.ShapeDtypeStruct((1, D), F32),
        ],
        compiler_params=_cparams(2),
    )(x1, ff, target, gate, fnorm, bgate)


MATMUL_VMEM_BUDGET = 40 * 1024 * 1024
STEP_COST_BYTES = 1.2e6
MXU_DIM = 256
MXU_FLOPS_PER_BYTE = 280.0


def _tile_options(n, cap):
    opts = [d for d in range(LANE, min(n, cap) + 1, LANE) if n % d == 0]
    return opts or [n]


def _matmul_tiles(M, N, K, sa, sb, so):
    best = None
    for tk in [K] + [d for d in _tile_options(K, 2048) if d >= 512 and d < K]:
        nk = K // tk
        for tm in _tile_options(M, 2048):
            for tn in _tile_options(N, 1408):
                acc = 0 if (so == 4 or nk == 1) else tm * tn * 4
                vmem = 2 * (tm * tk * sa + tk * tn * sb + tm * tn * so) + acc + tm * tn * 4 + 2 * (tm * tk + tk * tn)
                if vmem > MATMUL_VMEM_BUDGET:
                    continue
                a_reads = M * K * sa * (1 if nk == 1 else N // tn)
                b_reads = K * N * sb * (1 if (nk == 1 and N == tn) else M // tm)
                steps = (M // tm) * (N // tn) * nk
                hbm = a_reads + b_reads + M * N * so + (2 * M * N * 4 * (nk - 1) if nk > 1 else 0) / 8
                fill = (-(-tn // MXU_DIM) * MXU_DIM / tn) * (-(-tk // MXU_DIM) * MXU_DIM / tk)
                mxu = 2.0 * M * N * K * fill / MXU_FLOPS_PER_BYTE
                cost = max(hbm, mxu) + steps * STEP_COST_BYTES
                if best is None or cost < best[0]:
                    best = (cost, tm, tn, tk)
    assert best is not None, (M, N, K)
    return best[1:]


def matmul(name, a, b, ta=False, tb=False, out_dtype=F32, a_act=None):
    lead = None
    if a.ndim == 3 and not ta:
        lead = a.shape[:2]
    if a.ndim == 3:
        a = a.reshape(-1, a.shape[-1])
    if b.ndim == 3:
        b = b.reshape(-1, b.shape[-1])
    K, M = a.shape if ta else a.shape[::-1]
    N = b.shape[0] if tb else b.shape[1]
    assert (b.shape[1] if tb else b.shape[0]) == K, (name, a.shape, b.shape)
    tm, tn, tk = _matmul_tiles(M, N, K, a.dtype.itemsize, b.dtype.itemsize, jnp.dtype(out_dtype).itemsize)
    nk = K // tk
    dims = (((0 if ta else 1,), (1 if tb else 0,)), ((), ()))
    direct = jnp.dtype(out_dtype) == jnp.dtype(F32)

    def body(a_ref, b_ref, o_ref, *scratch):
        k = pl.program_id(2)
        av = a_ref[...]
        if a_act is not None:
            av = a_act(av.astype(F32))
        part = lax.dot_general(av.astype(MXU_DTYPE), b_ref[...].astype(MXU_DTYPE), dims, preferred_element_type=F32)
        if nk == 1:
            o_ref[...] = part.astype(o_ref.dtype)
            return
        acc = o_ref if direct else scratch[0]

        @pl.when(k == 0)
        def _():
            acc[...] = part

        @pl.when(k > 0)
        def _():
            acc[...] += part

        if not direct:

            @pl.when(k == nk - 1)
            def _():
                o_ref[...] = acc[...].astype(o_ref.dtype)

    a_spec = pl.BlockSpec((tk, tm), lambda i, j, k: (k, i)) if ta else pl.BlockSpec((tm, tk), lambda i, j, k: (i, k))
    b_spec = pl.BlockSpec((tn, tk), lambda i, j, k: (j, k)) if tb else pl.BlockSpec((tk, tn), lambda i, j, k: (k, j))
    out = pl.pallas_call(
        body,
        name=name,
        grid=(M // tm, N // tn, nk),
        in_specs=[a_spec, b_spec],
        out_specs=pl.BlockSpec((tm, tn), lambda i, j, k: (i, j)),
        out_shape=jax.ShapeDtypeStruct((M, N), out_dtype),
        scratch_shapes=[] if (direct or nk == 1) else [pltpu.VMEM((tm, tn), F32)],
        compiler_params=_cparams(3),
    )(a, b)
    return out.reshape(*lead, N) if lead is not None else out


SUBLANES = 8


def _shift_down(u, d):
    if d == 0:
        return u
    r = pltpu.roll(u, d, 0)
    t = lax.broadcasted_iota(jnp.int32, (SUBLANES, u.shape[1]), 0)
    return jnp.concatenate([jnp.where(t >= d, r[:SUBLANES], 0.0), r[SUBLANES:]], axis=0)


def _shift_up(u, d):
    if d == 0:
        return u
    s = u.shape[0]
    r = pltpu.roll(u, s - d, 0)
    t = lax.broadcasted_iota(jnp.int32, (SUBLANES, u.shape[1]), 0)
    return jnp.concatenate([r[: s - SUBLANES], jnp.where(t < SUBLANES - d, r[s - SUBLANES :], 0.0)], axis=0)


def _conv(u, w, b):
    k = w.shape[0]
    out = b + w[k - 1 : k, :] * u
    for j in range(k - 1):
        out = out + w[j : j + 1, :] * _shift_down(u, k - 1 - j)
    return out


def _conv_bwd(u, w, dc):
    k = w.shape[0]
    du = w[k - 1 : k, :] * dc
    dws = []
    for j in range(k - 1):
        du = du + w[j : j + 1, :] * _shift_up(dc, k - 1 - j)
        dws.append(jnp.sum(dc * _shift_down(u, k - 1 - j), axis=0, keepdims=True))
    dws.append(jnp.sum(dc * u, axis=0, keepdims=True))
    return du, jnp.concatenate(dws, axis=0), jnp.sum(dc, axis=0, keepdims=True)


def conv_silu_fwd(proj, w, b):
    B, S, _ = proj.shape
    k, c = w.shape
    blk0 = OFF_XBC // LANE

    def body(u_ref, w_ref, b_ref, o_ref):
        o_ref[0] = _silu(_conv(u_ref[0], w_ref[...], b_ref[...]))

    return pl.pallas_call(
        body,
        name="conv_silu_fwd",
        grid=(B, c // LANE),
        in_specs=[
            pl.BlockSpec((1, S, LANE), lambda bi, j: (bi, 0, blk0 + j)),
            pl.BlockSpec((k, LANE), lambda bi, j: (0, j)),
            pl.BlockSpec((1, LANE), lambda bi, j: (0, j)),
        ],
        out_specs=pl.BlockSpec((1, S, LANE), lambda bi, j: (bi, 0, j)),
        out_shape=jax.ShapeDtypeStruct((B, S, c), F32),
        compiler_params=_cparams(2),
    )(proj, w, b)


def conv_silu_bwd(proj, w, b, douts):
    B, S, _ = proj.shape
    k, c = w.shape
    blk0 = OFF_XBC // LANE
    ends = np.cumsum([d.shape[-1] // LANE for d in douts]).tolist()
    starts = [0] + ends[:-1]

    def body(u_ref, w_ref, b_ref, *refs):
        d_refs, (du_ref, dw_ref, db_ref) = refs[: len(douts)], refs[len(douts) :]
        j, bi = pl.program_id(0), pl.program_id(1)
        u, wv = u_ref[0], w_ref[...]
        cv = _conv(u, wv, b_ref[...])
        sg = jax.nn.sigmoid(cv)
        dout = d_refs[-1][0]
        for r in reversed(range(len(douts) - 1)):
            dout = jnp.where(j < ends[r], d_refs[r][0], dout)
        dc = dout * (sg * (1.0 + cv * (1.0 - sg)))
        du, dw, db = _conv_bwd(u, wv, dc)
        du_ref[0] = du.astype(du_ref.dtype)

        @pl.when(bi == 0)
        def _():
            dw_ref[...] = dw
            db_ref[...] = db

        @pl.when(bi > 0)
        def _():
            dw_ref[...] += dw
            db_ref[...] += db

    return pl.pallas_call(
        body,
        name="conv_silu_bwd",
        grid=(c // LANE, B),
        in_specs=[
            pl.BlockSpec((1, S, LANE), lambda j, bi: (bi, 0, blk0 + j)),
            pl.BlockSpec((k, LANE), lambda j, bi: (0, j)),
            pl.BlockSpec((1, LANE), lambda j, bi: (0, j)),
        ]
        + [
            pl.BlockSpec((1, S, LANE), lambda j, bi, lo=lo, hi=hi: (bi, 0, jnp.clip(j, lo, hi - 1) - lo))
            for lo, hi in zip(starts, ends)
        ],
        out_specs=[
            pl.BlockSpec((1, S, LANE), lambda j, bi: (bi, 0, j)),
            pl.BlockSpec((k, LANE), lambda j, bi: (0, j)),
            pl.BlockSpec((1, LANE), lambda j, bi: (0, j)),
        ],
        out_shape=[
            jax.ShapeDtypeStruct((B, S, c), MXU_DTYPE),
            jax.ShapeDtypeStruct((k, c), F32),
            jax.ShapeDtypeStruct((1, c), F32),
        ],
        compiler_params=_cparams(2),
    )(proj, w, b, *douts)


def conv_glu_fwd(u, w, b):
    B, S, c2 = u.shape
    k = w.shape[0]

    def body(u_ref, w_ref, b_ref, o_ref):
        cv = _conv(u_ref[0], w_ref[...], b_ref[...])
        o_ref[0] = (_silu(cv[:, :LANE]) * cv[:, LANE:]).astype(o_ref.dtype)

    return pl.pallas_call(
        body,
        name="conv_glu_fwd",
        grid=(B, c2 // (2 * LANE)),
        in_specs=[
            pl.BlockSpec((1, S, 2 * LANE), lambda bi, j: (bi, 0, j)),
            pl.BlockSpec((k, 2 * LANE), lambda bi, j: (0, j)),
            pl.BlockSpec((1, 2 * LANE), lambda bi, j: (0, j)),
        ],
        out_specs=pl.BlockSpec((1, S, LANE), lambda bi, j: (bi, 0, j)),
        out_shape=jax.ShapeDtypeStruct((B, S, c2 // 2), MXU_DTYPE),
        compiler_params=_cparams(2),
    )(u, w, b)


def conv_glu_bwd(u, w, b, da):
    B, S, c2 = u.shape
    k = w.shape[0]

    def body(u_ref, w_ref, b_ref, d_ref, du_ref, dw_ref, db_ref):
        bi = pl.program_id(1)
        uv, wv = u_ref[0], w_ref[...]
        cv = _conv(uv, wv, b_ref[...])
        gate, val = cv[:, :LANE], cv[:, LANE:]
        sg = jax.nn.sigmoid(gate)
        dav = d_ref[0].astype(F32)
        dc = jnp.concatenate([dav * val * (sg * (1.0 + gate * (1.0 - sg))), dav * gate * sg], axis=1)
        du, dw, db = _conv_bwd(uv, wv, dc)
        du_ref[0] = du.astype(du_ref.dtype)

        @pl.when(bi == 0)
        def _():
            dw_ref[...] = dw
            db_ref[...] = db

        @pl.when(bi > 0)
        def _():
            dw_ref[...] += dw
            db_ref[...] += db

    return pl.pallas_call(
        body,
        name="conv_glu_bwd",
        grid=(c2 // (2 * LANE), B),
        in_specs=[
            pl.BlockSpec((1, S, 2 * LANE), lambda j, bi: (bi, 0, j)),
            pl.BlockSpec((k, 2 * LANE), lambda j, bi: (0, j)),
            pl.BlockSpec((1, 2 * LANE), lambda j, bi: (0, j)),
            pl.BlockSpec((1, S, LANE), lambda j, bi: (bi, 0, j)),
        ],
        out_specs=[
            pl.BlockSpec((1, S, 2 * LANE), lambda j, bi: (bi, 0, j)),
            pl.BlockSpec((k, 2 * LANE), lambda j, bi: (0, j)),
            pl.BlockSpec((1, 2 * LANE), lambda j, bi: (0, j)),
        ],
        out_shape=[
            jax.ShapeDtypeStruct((B, S, c2), MXU_DTYPE),
            jax.ShapeDtypeStruct((k, c2), F32),
            jax.ShapeDtypeStruct((1, c2), F32),
        ],
        compiler_params=_cparams(2),
    )(u, w, b, da)


N_PAIR = SSD_HEADS // SSD_GROUPS // 2


def _ssd_chunk(g, prev, xs, bm, cm, dtraw, dt_bias, a_log, d_skip):
    L = CHUNK
    lane = lax.broadcasted_iota(jnp.int32, (1, LANE), 1)
    sub = lax.broadcasted_iota(jnp.int32, (LANE, 1), 0)
    row = lax.broadcasted_iota(jnp.int32, (L, L), 0)
    col = lax.broadcasted_iota(jnp.int32, (L, L), 1)
    tri = (row >= col).astype(F32)
    first = lane < SSD_HEAD_DIM

    dt = _softplus(dtraw + dt_bias)
    a = dt * (-jnp.exp(a_log))
    acs = jnp.dot(tri, a, precision=HIGHEST, preferred_element_type=F32)
    acs_t = acs.T
    a_end = jnp.sum(a, axis=0, keepdims=True)
    cb = lax.dot_general(cm.astype(MXU_DTYPE), bm.astype(MXU_DTYPE), (((1,), (1,)), ((), ())), preferred_element_type=F32)

    def lane_of(v, h):
        return jnp.sum(jnp.where(lane == h, v, 0.0), axis=1, keepdims=True)

    def expand(v, ha):
        return jnp.where(first, lane_of(v, ha), lane_of(v, ha + 1))

    ys, news = [], []
    for j in range(N_PAIR):
        ha = g * (SSD_HEADS // SSD_GROUPS) + 2 * j
        x = xs[j]
        dt_e, acs_e, end_e = expand(dt, ha), expand(acs, ha), expand(a_end, ha)
        xdt = x * dt_e
        y = jnp.dot(cm.astype(MXU_DTYPE), prev[j].astype(MXU_DTYPE), preferred_element_type=F32) * jnp.exp(acs_e)
        st = lax.dot_general(
            bm.astype(MXU_DTYPE), (xdt * jnp.exp(end_e - acs_e)).astype(MXU_DTYPE), (((0,), (0,)), ((), ())), preferred_element_type=F32
        )
        news.append(prev[j] * jnp.exp(end_e) + st)
        for hh in range(2):
            h = ha + hh
            seg = lane_of(acs, h) - jnp.sum(jnp.where(sub == h, acs_t, 0.0), axis=0, keepdims=True)
            decay = jnp.exp(jnp.where(row >= col, seg, -jnp.inf))
            xh = jnp.where(first if hh == 0 else jnp.logical_not(first), xdt, 0.0)
            y = y + jnp.dot((cb * decay).astype(MXU_DTYPE), xh.astype(MXU_DTYPE), preferred_element_type=F32)
        ys.append(y + x * expand(d_skip, ha))
    return ys, news


def _ssd_specs(nc, order):
    gw = D_SSD // SSD_GROUPS
    bm0, cm0 = D_SSD // SSD_STATE, D_SSD // SSD_STATE + SSD_GROUPS
    par = pl.BlockSpec((1, LANE), lambda *ids: (0, 0))
    return [
        pl.BlockSpec((1, CHUNK, gw), lambda *ids: (order(*ids)[0], order(*ids)[2], order(*ids)[1])),
        pl.BlockSpec((1, CHUNK, SSD_STATE), lambda *ids: (order(*ids)[0], order(*ids)[2], bm0 + order(*ids)[1])),
        pl.BlockSpec((1, CHUNK, SSD_STATE), lambda *ids: (order(*ids)[0], order(*ids)[2], cm0 + order(*ids)[1])),
        pl.BlockSpec((1, CHUNK, LANE), lambda *ids: (order(*ids)[0], order(*ids)[2], OFF_DT // LANE)),
        par,
        par,
        par,
    ]


def ssd_fwd(xbc_act, proj, dt_bias, a_log, d_skip):
    B, S, _ = xbc_act.shape
    nc = S // CHUNK
    gw = D_SSD // SSD_GROUPS

    def body(xs_ref, bm_ref, cm_ref, dt_ref, db_ref, al_ref, ds_ref, y_ref, st_ref, state):
        g, c = pl.program_id(1), pl.program_id(2)

        @pl.when(c == 0)
        def _():
            state[...] = jnp.zeros_like(state)

        prev = [state[j] for j in range(N_PAIR)]
        for j in range(N_PAIR):
            st_ref[0, 0, 0, j] = prev[j]
        xs = [xs_ref[0, :, j * LANE : (j + 1) * LANE] for j in range(N_PAIR)]
        ys, news = _ssd_chunk(g, prev, xs, bm_ref[0], cm_ref[0], dt_ref[0], db_ref[...], al_ref[...], ds_ref[...])
        for j in range(N_PAIR):
            y_ref[0, :, j * LANE : (j + 1) * LANE] = ys[j]
            state[j] = news[j]

    return pl.pallas_call(
        body,
        name="ssd_fwd",
        grid=(B, SSD_GROUPS, nc),
        in_specs=_ssd_specs(nc, lambda b, g, c: (b, g, c)),
        out_specs=[
            pl.BlockSpec((1, CHUNK, gw), lambda b, g, c: (b, c, g)),
            pl.BlockSpec((1, 1, 1, N_PAIR, SSD_STATE, LANE), lambda b, g, c: (b, g, c, 0, 0, 0)),
        ],
        out_shape=[
            jax.ShapeDtypeStruct((B, S, D_SSD), F32),
            jax.ShapeDtypeStruct((B, SSD_GROUPS, nc, N_PAIR, SSD_STATE, LANE), F32),
        ],
        scratch_shapes=[pltpu.VMEM((N_PAIR, SSD_STATE, LANE), F32)],
        compiler_params=_cparams(3),
    )(xbc_act, xbc_act, xbc_act, proj, dt_bias, a_log, d_skip)


def ssd_bwd(xbc_act, proj, dt_bias, a_log, d_skip, states, dy):
    B, S, _ = xbc_act.shape
    nc = S // CHUNK
    gw = D_SSD // SSD_GROUPS
    order = lambda b, cr, g: (b, g, nc - 1 - cr)

    def body(xs_ref, bm_ref, cm_ref, dt_ref, db_ref, al_ref, ds_ref, st_ref, dy_ref,
             dxs_ref, dbm_ref, dcm_ref, ddt_ref, ddb_ref, dal_ref, dds_ref, dstate):
        b, cr, g = pl.program_id(0), pl.program_id(1), pl.program_id(2)

        @pl.when(cr == 0)
        def _():
            dstate[g] = jnp.zeros((N_PAIR, SSD_STATE, LANE), F32)

        prev = [st_ref[0, 0, 0, j] for j in range(N_PAIR)]
        xs = [xs_ref[0, :, j * LANE : (j + 1) * LANE] for j in range(N_PAIR)]

        def f(prev, xs, bm, cm, dtraw, dt_bias, a_log, d_skip):
            return _ssd_chunk(g, prev, xs, bm, cm, dtraw, dt_bias, a_log, d_skip)

        _, vjp = jax.vjp(f, prev, xs, bm_ref[0], cm_ref[0], dt_ref[0], db_ref[...], al_ref[...], ds_ref[...])
        ct_y = [dy_ref[0, :, j * LANE : (j + 1) * LANE] for j in range(N_PAIR)]
        ct_s = [dstate[g, j] for j in range(N_PAIR)]
        dprev, dxs, dbm, dcm, ddt, ddb, dal, dds = vjp((ct_y, ct_s))
        for j in range(N_PAIR):
            dstate[g, j] = dprev[j]
            dxs_ref[0, :, j * LANE : (j + 1) * LANE] = dxs[j]
        dbm_ref[0] = dbm
        dcm_ref[0] = dcm

        @pl.when(g == 0)
        def _():
            ddt_ref[0] = ddt

        @pl.when(g > 0)
        def _():
            ddt_ref[0] += ddt

        first = (b == 0) & (cr == 0) & (g == 0)

        @pl.when(first)
        def _():
            ddb_ref[...] = ddb
            dal_ref[...] = dal
            dds_ref[...] = dds

        @pl.when(jnp.logical_not(first))
        def _():
            ddb_ref[...] += ddb
            dal_ref[...] += dal
            dds_ref[...] += dds

    par = pl.BlockSpec((1, LANE), lambda *ids: (0, 0))
    in_specs = _ssd_specs(nc, order) + [
        pl.BlockSpec((1, 1, 1, N_PAIR, SSD_STATE, LANE), lambda b, cr, g: (b, g, nc - 1 - cr, 0, 0, 0)),
        pl.BlockSpec((1, CHUNK, gw), lambda b, cr, g: (b, nc - 1 - cr, g)),
    ]
    return pl.pallas_call(
        body,
        name="ssd_bwd",
        grid=(B, nc, SSD_GROUPS),
        in_specs=in_specs,
        out_specs=[
            pl.BlockSpec((1, CHUNK, gw), lambda b, cr, g: (b, nc - 1 - cr, g)),
            pl.BlockSpec((1, CHUNK, SSD_STATE), lambda b, cr, g: (b, nc - 1 - cr, g)),
            pl.BlockSpec((1, CHUNK, SSD_STATE), lambda b, cr, g: (b, nc - 1 - cr, g)),
            pl.BlockSpec((1, CHUNK, LANE), lambda b, cr, g: (b, nc - 1 - cr, 0)),
            par,
            par,
            par,
        ],
        out_shape=[
            jax.ShapeDtypeStruct((B, S, D_SSD), F32),
            jax.ShapeDtypeStruct((B, S, SSD_GROUPS * SSD_STATE), F32),
            jax.ShapeDtypeStruct((B, S, SSD_GROUPS * SSD_STATE), F32),
            jax.ShapeDtypeStruct((B, S, LANE), F32),
            jax.ShapeDtypeStruct((1, LANE), F32),
            jax.ShapeDtypeStruct((1, LANE), F32),
            jax.ShapeDtypeStruct((1, LANE), F32),
        ],
        scratch_shapes=[pltpu.VMEM((SSD_GROUPS, N_PAIR, SSD_STATE, LANE), F32)],
        compiler_params=_cparams(3),
    )(xbc_act, xbc_act, xbc_act, proj, dt_bias, a_log, d_skip, states, dy)


ATT_SCALE = (QK_NOPE + QK_ROPE) ** -0.5
ATT_TILE = 512


ATT_HEADS_PER_STEP = 2
LOG2E = 1.4426950408889634
Q_PRESCALE = ATT_SCALE * LOG2E


def fn_rope_q(q, cos_t, sin_t, rot):
    parts = []
    for h in range(MLA_HEADS):
        qr = q[:, h * Q_HEAD_PAD + LANE : (h + 1) * Q_HEAD_PAD]
        qr = qr * cos_t + jnp.dot(qr, rot, precision=HIGHEST, preferred_element_type=F32) * sin_t
        parts += [q[:, h * Q_HEAD_PAD : h * Q_HEAD_PAD + LANE] * Q_PRESCALE, qr * Q_PRESCALE]
    return (jnp.concatenate(parts, axis=1),)


def _head_cols(ref, hh, width):
    return ref[0, :, hh * width : (hh + 1) * width]


def _causal_mask(s):
    return jnp.where(lax.broadcasted_iota(jnp.int32, s.shape, 0) >= lax.broadcasted_iota(jnp.int32, s.shape, 1), s, -jnp.inf)


def _nt(a, b):
    return lax.dot_general(a, b, (((1,), (1,)), ((), ())), preferred_element_type=F32)


def _tn(a, b):
    return lax.dot_general(a, b, (((0,), (0,)), ((), ())), preferred_element_type=F32)


def _per_tile(idx, n, variant):
    for v in range(n):

        @pl.when(idx == v)
        def _(v=v):
            variant(v)


def attention_fwd(qs, kv, krr):
    B, S, _ = qs.shape
    t = min(ATT_TILE, S)
    n = S // t
    hps = ATT_HEADS_PER_STEP

    def body(q_ref, k_ref, kr_ref, o_ref, lse_ref):
        def variant(v):
            spans = ([(0, v * t, False)] if v else []) + [(v * t, (v + 1) * t, True)]
            for hh in range(hps):
                q = _head_cols(q_ref, hh, Q_HEAD_PAD)
                c0 = hh * 2 * LANE
                ss = []
                for a, b, diag in spans:
                    s = _nt(q, jnp.concatenate([k_ref[0, a:b, c0 : c0 + LANE], kr_ref[0, a:b]], axis=1))
                    ss.append(_causal_mask(s) if diag else s)
                m = functools.reduce(jnp.maximum, [jnp.max(s, axis=1, keepdims=True) for s in ss])
                ps = [jnp.exp2(s - m) for s in ss]
                l = functools.reduce(jnp.add, [jnp.sum(p, axis=1, keepdims=True) for p in ps])
                acc = functools.reduce(
                    jnp.add,
                    [jnp.dot(p.astype(MXU_DTYPE), k_ref[0, a:b, c0 + LANE : c0 + 2 * LANE], preferred_element_type=F32) for p, (a, b, _) in zip(ps, spans)],
                )
                o_ref[0, :, hh * V_DIM : (hh + 1) * V_DIM] = acc / l
                lse_ref[0, hh] = m + jnp.log2(l)

        _per_tile(pl.program_id(2), n, variant)

    return pl.pallas_call(
        body,
        name="attention_fwd",
        grid=(B, MLA_HEADS // hps, n),
        in_specs=[
            pl.BlockSpec((1, t, hps * Q_HEAD_PAD), lambda b, h, i: (b, i, h)),
            pl.BlockSpec((1, S, hps * 2 * LANE), lambda b, h, i: (b, 0, h)),
            pl.BlockSpec((1, S, LANE), lambda b, h, i: (b, 0, 0)),
        ],
        out_specs=[
            pl.BlockSpec((1, t, hps * V_DIM), lambda b, h, i: (b, i, h)),
            pl.BlockSpec((1, hps, t, 1), lambda b, h, i: (b, h, i, 0)),
        ],
        out_shape=[jax.ShapeDtypeStruct((B, S, D_ATT), F32), jax.ShapeDtypeStruct((B, MLA_HEADS, S, 1), F32)],
        compiler_params=_cparams(3),
    )(qs, kv, krr)


def attention_bwd_dq(qs, kv, krr, cos_t, sin_t, rot_t, o, lse, do):
    B, S, _ = qs.shape
    t = min(ATT_TILE, S)
    n = S // t
    hps = ATT_HEADS_PER_STEP

    def body(q_ref, c_ref, s_ref, rott_ref, k_ref, kr_ref, o_ref, lse_ref, do_ref, dq_ref):
        def variant(v):
            spans = ([(0, v * t, False)] if v else []) + [(v * t, (v + 1) * t, True)]
            for hh in range(hps):
                q = _head_cols(q_ref, hh, Q_HEAD_PAD)
                dov = _head_cols(do_ref, hh, V_DIM)
                delta = jnp.sum(_head_cols(o_ref, hh, V_DIM) * dov, axis=1, keepdims=True)
                dob = dov.astype(MXU_DTYPE)
                c0 = hh * 2 * LANE
                acc = None
                for a, b, diag in spans:
                    k = jnp.concatenate([k_ref[0, a:b, c0 : c0 + LANE], kr_ref[0, a:b]], axis=1)
                    s = _nt(q, k)
                    p = jnp.exp2((_causal_mask(s) if diag else s) - lse_ref[0, hh])
                    ds = p * (_nt(dob, k_ref[0, a:b, c0 + LANE : c0 + 2 * LANE]) - delta) * ATT_SCALE
                    part = jnp.dot(ds.astype(MXU_DTYPE), k, preferred_element_type=F32)
                    acc = part if acc is None else acc + part
                r = acc[:, LANE:]
                dr = r * c_ref[0] + jnp.dot(r * s_ref[0], rott_ref[...], precision=HIGHEST, preferred_element_type=F32)
                dq_ref[0, :, hh * Q_HEAD_PAD : (hh + 1) * Q_HEAD_PAD] = jnp.concatenate([acc[:, :LANE], dr], axis=1).astype(dq_ref.dtype)

        _per_tile(pl.program_id(2), n, variant)

    return pl.pallas_call(
        body,
        name="attention_bwd_dq",
        grid=(B, MLA_HEADS // hps, n),
        in_specs=[
            pl.BlockSpec((1, t, hps * Q_HEAD_PAD), lambda b, h, i: (b, i, h)),
            pl.BlockSpec((1, t, LANE), lambda b, h, i: (b, i, 0)),
            pl.BlockSpec((1, t, LANE), lambda b, h, i: (b, i, 0)),
            pl.BlockSpec((LANE, LANE), lambda b, h, i: (0, 0)),
            pl.BlockSpec((1, S, hps * 2 * LANE), lambda b, h, i: (b, 0, h)),
            pl.BlockSpec((1, S, LANE), lambda b, h, i: (b, 0, 0)),
            pl.BlockSpec((1, t, hps * V_DIM), lambda b, h, i: (b, i, h)),
            pl.BlockSpec((1, hps, t, 1), lambda b, h, i: (b, h, i, 0)),
            pl.BlockSpec((1, t, hps * V_DIM), lambda b, h, i: (b, i, h)),
        ],
        out_specs=pl.BlockSpec((1, t, hps * Q_HEAD_PAD), lambda b, h, i: (b, i, h)),
        out_shape=jax.ShapeDtypeStruct(qs.shape, MXU_DTYPE),
        compiler_params=_cparams(3),
    )(qs, cos_t, sin_t, rot_t, kv, krr, o, lse, do)


def attention_bwd_dkv(qs, kv, krr, o, lse, do):
    B, S, _ = qs.shape
    t = min(ATT_TILE, S)
    n = S // t
    hps = ATT_HEADS_PER_STEP

    def body(q_ref, k_ref, kr_ref, o_ref, lse_ref, do_ref, dkv_ref, dkr_ref):
        h = pl.program_id(2)

        def variant(v):
            spans = [(v * t, (v + 1) * t, True)] + ([((v + 1) * t, S, False)] if v + 1 < n else [])
            dkr = None
            for hh in range(hps):
                c0 = hh * 2 * LANE
                k = jnp.concatenate([k_ref[0, :, c0 : c0 + LANE], kr_ref[0]], axis=1)
                vals = k_ref[0, :, c0 + LANE : c0 + 2 * LANE]
                dk = dv = None
                for a, b, diag in spans:
                    q = q_ref[0, a:b, hh * Q_HEAD_PAD : (hh + 1) * Q_HEAD_PAD]
                    dov = do_ref[0, a:b, hh * V_DIM : (hh + 1) * V_DIM]
                    dob = dov.astype(MXU_DTYPE)
                    s = _nt(q, k)
                    p = jnp.exp2((_causal_mask(s) if diag else s) - lse_ref[0, hh, a:b])
                    delta = jnp.sum(o_ref[0, a:b, hh * V_DIM : (hh + 1) * V_DIM] * dov, axis=1, keepdims=True)
                    ds2 = p * (_nt(dob, vals) - delta) * (1.0 / LOG2E)
                    dv_part, dk_part = _tn(p.astype(MXU_DTYPE), dob), _tn(ds2.astype(MXU_DTYPE), q)
                    dv = dv_part if dv is None else dv + dv_part
                    dk = dk_part if dk is None else dk + dk_part
                dkv_ref[0, :, c0 : c0 + 2 * LANE] = jnp.concatenate([dk[:, :LANE], dv], axis=1).astype(dkv_ref.dtype)
                dkr = dk[:, LANE:] if dkr is None else dkr + dk[:, LANE:]

            @pl.when(h == 0)
            def _():
                dkr_ref[0] = dkr

            @pl.when(h > 0)
            def _():
                dkr_ref[0] += dkr

        _per_tile(pl.program_id(1), n, variant)

    return pl.pallas_call(
        body,
        name="attention_bwd_dkv",
        grid=(B, n, MLA_HEADS // hps),
        in_specs=[
            pl.BlockSpec((1, S, hps * Q_HEAD_PAD), lambda b, j, h: (b, 0, h)),
            pl.BlockSpec((1, t, hps * 2 * LANE), lambda b, j, h: (b, j, h)),
            pl.BlockSpec((1, t, LANE), lambda b, j, h: (b, j, 0)),
            pl.BlockSpec((1, S, hps * V_DIM), lambda b, j, h: (b, 0, h)),
            pl.BlockSpec((1, hps, S, 1), lambda b, j, h: (b, h, 0, 0)),
            pl.BlockSpec((1, S, hps * V_DIM), lambda b, j, h: (b, 0, h)),
        ],
        out_specs=[
            pl.BlockSpec((1, t, hps * 2 * LANE), lambda b, j, h: (b, j, h)),
            pl.BlockSpec((1, t, LANE), lambda b, j, h: (b, j, 0)),
        ],
        out_shape=[jax.ShapeDtypeStruct(kv.shape, MXU_DTYPE), jax.ShapeDtypeStruct(krr.shape, F32)],
        compiler_params=_cparams(3),
    )(qs, kv, krr, o, lse, do)


_IN_SPLITS = np.cumsum([D_SSD, D_XBC, SSD_HEADS, Q_RANK, KV_RANK]).tolist()


def _pad_last(t, n):
    return jnp.pad(t, [(0, 0)] * (t.ndim - 1) + [(0, n - t.shape[-1])])


def win_to_kernel(w):
    z, xbc, dt, cq, ckv, kr = jnp.split(w, _IN_SPLITS, axis=-1)
    return jnp.concatenate([z, xbc, ckv, _pad_last(kr, LANE), _pad_last(dt, LANE), cq], axis=-1)


def win_from_kernel(g):
    z, xbc = g[..., :OFF_XBC], g[..., OFF_XBC:OFF_CKV]
    ckv, kr = g[..., OFF_CKV:OFF_KR], g[..., OFF_KR : OFF_KR + QK_ROPE]
    dt, cq = g[..., OFF_DT : OFF_DT + SSD_HEADS], g[..., OFF_CQ:]
    return jnp.concatenate([z, xbc, dt, cq, ckv, kr], axis=-1)


def wuq_to_kernel(w):
    w = w.reshape(*w.shape[:-1], MLA_HEADS, QK_NOPE + QK_ROPE)
    return _pad_last(w, Q_HEAD_PAD).reshape(*w.shape[:-2], MLA_HEADS * Q_HEAD_PAD)


def wuq_from_kernel(g):
    g = g.reshape(*g.shape[:-1], MLA_HEADS, Q_HEAD_PAD)[..., : QK_NOPE + QK_ROPE]
    return g.reshape(*g.shape[:-2], MLA_HEADS * (QK_NOPE + QK_ROPE))


def glu_to_kernel(w):
    lead = w.shape[:-1]
    w = w.reshape(*lead, 2, D_FF // LANE, LANE)
    return jnp.swapaxes(w, -3, -2).reshape(*lead, 2 * D_FF)


def glu_from_kernel(g):
    lead = g.shape[:-1]
    g = g.reshape(*lead, D_FF // LANE, 2, LANE)
    return jnp.swapaxes(g, -3, -2).reshape(*lead, 2 * D_FF)


def _lane_pad_row(v):
    return _pad_last(v, LANE)[None, :]


def rope_tables(positions):
    inv_freq = jnp.asarray(1.0 / (ROPE_BASE ** (np.arange(0, QK_ROPE, 2, dtype=np.float32) / QK_ROPE)))
    ang = positions.astype(F32)[..., None] * inv_freq
    cos, sin = jnp.cos(ang), jnp.sin(ang)
    zeros = jnp.zeros(cos.shape[:-1] + (LANE - QK_ROPE,), F32)
    rot = np.zeros((LANE, LANE), np.float32)
    half = QK_ROPE // 2
    for j in range(half):
        rot[j + half, j] = -1.0
        rot[j, j + half] = 1.0
    return jnp.concatenate([cos, cos, zeros], -1), jnp.concatenate([sin, sin, zeros], -1), jnp.asarray(rot), jnp.asarray(rot.T)


BIG = ("w_in", "w_uq", "w_ukv", "w_out", "w_up", "w_down")


def local_step(x, target, positions, mod_raw, W):
    B, S, D = x.shape
    cos_t, sin_t, rot, rot_t = rope_tables(positions)
    row = lambda v: v.reshape(1, -1)
    Ls = []
    for l in range(DEPTH):
        mods = [mod_raw[l, :, k][:, None, :] for k in range(6)]
        bias = [row(W["b_ada"][l, k * D : (k + 1) * D]) for k in range(6)]
        Ls.append(
            dict(
                mods=mods,
                bias=bias,
                w_in=win_to_kernel(W["w_in"][l]),
                w_uq=wuq_to_kernel(W["w_uq"][l]),
                w_ukv=W["w_ukv"][l],
                w_out=W["w_out"][l],
                w_up=W["w_up"][l],
                w_down=W["w_down"][l],
                conv_w=W["conv_w"][l],
                conv_b=row(W["conv_b"][l]),
                conv_ff_w=glu_to_kernel(W["conv_ff_w"][l]),
                conv_ff_b=row(glu_to_kernel(W["conv_ff_b"][l])),
                dt_bias=_lane_pad_row(W["dt_bias"][l]),
                a_log=_lane_pad_row(W["a_log"][l]),
                d_skip=_lane_pad_row(W["d_skip"][l]),
                norm_mix=row(W["norm_mix"][l]),
                ssd_norm=row(W["ssd_norm"][l]),
                q_norm=row(W["q_norm"][l]),
                kv_norm=row(W["kv_norm"][l]),
                attn_norm=row(W["attn_norm"][l]),
                norm_mlp=row(W["norm_mlp"][l]),
            )
        )
    fnorm = row(W["final_norm"])

    p0 = Ls[0]
    (h1,) = rowwise_fwd(
        "prenorm_fwd", fn_prenorm, [x], [p0["mods"][1], p0["mods"][0]], [p0["norm_mix"], p0["bias"][1], p0["bias"][0]], [(D, MXU_DTYPE)]
    )
    xin = x
    for l, p in enumerate(Ls):
        s = p["saved"] = dict(xin=xin, h1=h1)
        s["proj"] = proj = matmul(f"w_in_fwd{l}", h1, p["w_in"])
        s["xbc_act"] = xbc_act = conv_silu_fwd(proj, p["conv_w"], p["conv_b"])
        s["yscan"], s["states"] = ssd_fwd(xbc_act, proj, p["dt_bias"], p["a_log"], p["d_skip"])
        mla_in = [Col(proj, Q_RANK, OFF_CQ // Q_RANK), Col(proj, KV_RANK, OFF_CKV // KV_RANK), Col(proj, LANE, OFF_KR // LANE), cos_t, sin_t]
        s["cqn"], s["ckvn"], s["krr"] = rowwise_fwd(
            f"mla_prep_fwd{l}", fn_mla_prep, mla_in, [], [p["q_norm"], p["kv_norm"], rot], [(Q_RANK, MXU_DTYPE), (KV_RANK, MXU_DTYPE), (LANE, MXU_DTYPE)]
        )
        q = matmul(f"w_uq_fwd{l}", s["cqn"], p["w_uq"], out_dtype=MXU_DTYPE)
        (s["qs"],) = rowwise_fwd(f"rope_q_fwd{l}", fn_rope_q, [q, cos_t, sin_t], [], [rot], [(q.shape[-1], MXU_DTYPE)])
        s["kv"] = matmul(f"w_ukv_fwd{l}", s["ckvn"], p["w_ukv"], out_dtype=MXU_DTYPE)
        s["o"], s["lse"] = attention_fwd(s["qs"], s["kv"], s["krr"])
        (s["ycat"],) = rowwise_fwd(
            f"mix_fwd{l}", fn_mix, [s["yscan"], Col(proj, D_SSD, 0), s["o"]], [], [p["ssd_norm"], p["attn_norm"]], [(D_SSD + D_ATT, MXU_DTYPE)]
        )
        s["ymix"] = matmul(f"w_out_fwd{l}", s["ycat"], p["w_out"])
        s["x1"], s["h2"] = rowwise_fwd(
            f"mid_fwd{l}",
            fn_resid_prenorm,
            [xin, s["ymix"]],
            [p["mods"][2], p["mods"][4], p["mods"][3]],
            [p["norm_mlp"], p["bias"][2], p["bias"][4], p["bias"][3]],
            [(D, F32), (D, MXU_DTYPE)],
        )
        s["u"] = matmul(f"w_up_fwd{l}", s["h2"], p["w_up"])
        s["a"] = conv_glu_fwd(s["u"], p["conv_ff_w"], p["conv_ff_b"])
        s["ff"] = matmul(f"w_down_fwd{l}", s["a"], p["w_down"])
        if l + 1 < DEPTH:
            n = Ls[l + 1]
            xin, h1 = rowwise_fwd(
                f"join_fwd{l}",
                fn_resid_prenorm,
                [s["x1"], s["ff"]],
                [p["mods"][5], n["mods"][1], n["mods"][0]],
                [n["norm_mix"], p["bias"][5], n["bias"][1], n["bias"][0]],
                [(D, F32), (D, MXU_DTYPE)],
            )

    G = {k: [None] * DEPTH for k in W if k not in ("final_norm", "w_ada")}
    dmod = [[None] * 6 for _ in range(DEPTH)]
    dbias = [[None] * 6 for _ in range(DEPTH)]
    last = Ls[-1]
    sl = last["saved"]
    loss, dx1, dff, dmod[-1][5], G["final_norm"], dbias[-1][5] = final_fwdbwd(
        sl["x1"], sl["ff"], target, last["mods"][5], fnorm, last["bias"][5]
    )
    grad_x = None
    for l in reversed(range(DEPTH)):
        p = Ls[l]
        s = p["saved"]
        da = matmul(f"w_down_dgrad{l}", dff, p["w_down"], tb=True)
        G["w_down"][l] = matmul(f"w_down_wgrad{l}", s["a"], dff, ta=True)
        du, dcw, dcb = conv_glu_bwd(s["u"], p["conv_ff_w"], p["conv_ff_b"], da)
        G["conv_ff_w"][l], G["conv_ff_b"][l] = glu_from_kernel(dcw), glu_from_kernel(dcb)[0]
        dh2 = matmul(f"w_up_dgrad{l}", du, p["w_up"], tb=True)
        G["w_up"][l] = matmul(f"w_up_wgrad{l}", s["h2"], du, ta=True)
        dxb, dymix, dmod[l][2], dmod[l][4], dmod[l][3], G["norm_mlp"][l], dbias[l][2], dbias[l][4], dbias[l][3] = rowwise_bwd(
            f"mid_bwd{l}",
            fn_resid_prenorm,
            [s["xin"], s["ymix"]],
            [p["mods"][2], p["mods"][4], p["mods"][3]],
            [p["norm_mlp"], p["bias"][2], p["bias"][4], p["bias"][3]],
            [dx1, dh2],
            [True, True],
            [True] * 4,
            mxu_only=(1,),
        )
        dycat = matmul(f"w_out_dgrad{l}", dymix, p["w_out"], tb=True)
        G["w_out"][l] = matmul(f"w_out_wgrad{l}", s["ycat"], dymix, ta=True)
        dyscan, dz, do, G["ssd_norm"][l], G["attn_norm"][l] = rowwise_bwd(
            f"mix_bwd{l}", fn_mix, [s["yscan"], Col(s["proj"], D_SSD, 0), s["o"]], [], [p["ssd_norm"], p["attn_norm"]], [dycat], [True] * 3, [True] * 2, mxu_only=(1,)
        )
        dq = attention_bwd_dq(s["qs"], s["kv"], s["krr"], cos_t, sin_t, rot_t, s["o"], s["lse"], do)
        dkv, dkrr = attention_bwd_dkv(s["qs"], s["kv"], s["krr"], s["o"], s["lse"], do)
        dcqn = matmul(f"w_uq_dgrad{l}", dq, p["w_uq"], tb=True)
        G["w_uq"][l] = wuq_from_kernel(matmul(f"w_uq_wgrad{l}", s["cqn"], dq, ta=True))
        dckvn = matmul(f"w_ukv_dgrad{l}", dkv, p["w_ukv"], tb=True)
        G["w_ukv"][l] = matmul(f"w_ukv_wgrad{l}", s["ckvn"], dkv, ta=True)
        proj = s["proj"]
        mla_in = [Col(proj, Q_RANK, OFF_CQ // Q_RANK), Col(proj, KV_RANK, OFF_CKV // KV_RANK), Col(proj, LANE, OFF_KR // LANE), cos_t, sin_t]
        dcq, dckv, dkr, G["q_norm"][l], G["kv_norm"][l] = rowwise_bwd(
            f"mla_prep_bwd{l}",
            fn_mla_prep,
            mla_in,
            [],
            [p["q_norm"], p["kv_norm"], rot],
            [dcqn, dckvn, dkrr],
            [True, True, True, False, False],
            [True, True, False],
            mxu_only=(0, 1, 2),
        )
        dxs, dbm, dcm, ddt, ddb, dal, dds = ssd_bwd(s["xbc_act"], proj, p["dt_bias"], p["a_log"], p["d_skip"], s["states"], dyscan)
        G["dt_bias"][l], G["a_log"][l], G["d_skip"][l] = ddb[0, :SSD_HEADS], dal[0, :SSD_HEADS], dds[0, :SSD_HEADS]
        dxbc, G["conv_w"][l], dcb = conv_silu_bwd(proj, p["conv_w"], p["conv_b"], [dxs, dbm, dcm])
        G["conv_b"][l] = dcb[0]
        dproj = jnp.concatenate([dz, dxbc, dckv, dkr, ddt.astype(MXU_DTYPE), dcq], axis=-1)
        dh1 = matmul(f"w_in_dgrad{l}", dproj, p["w_in"], tb=True)
        G["w_in"][l] = win_from_kernel(matmul(f"w_in_wgrad{l}", s["h1"], dproj, ta=True))
        if l > 0:
            q = Ls[l - 1]
            sq = q["saved"]
            dx1, dff, dmod[l - 1][5], dmod[l][1], dmod[l][0], G["norm_mix"][l], dbias[l - 1][5], dbias[l][1], dbias[l][0] = rowwise_bwd(
                f"join_bwd{l - 1}",
                fn_resid_prenorm,
                [sq["x1"], sq["ff"]],
                [q["mods"][5], p["mods"][1], p["mods"][0]],
                [p["norm_mix"], q["bias"][5], p["bias"][1], p["bias"][0]],
                [dxb, dh1],
                [True, True],
                [True] * 4,
                mxu_only=(1,),
            )
        else:
            grad_x, dmod[0][1], dmod[0][0], G["norm_mix"][0], dbias[0][1], dbias[0][0] = rowwise_bwd(
                "prenorm_bwd",
                fn_prenorm,
                [x],
                [p["mods"][1], p["mods"][0]],
                [p["norm_mix"], p["bias"][1], p["bias"][0]],
                [dh1],
                [True],
                [True] * 3,
                adds={0: dxb},
            )
    for l in range(DEPTH):
        G["b_ada"][l] = jnp.concatenate([d[0] for d in dbias[l]])
        for k in ("norm_mix", "ssd_norm", "q_norm", "kv_norm", "attn_norm", "norm_mlp"):
            G[k][l] = G[k][l][0]
    grads = {k: (jnp.stack(v) if isinstance(v, list) else v[0]) for k, v in G.items()}
    dmod_raw = jnp.stack([jnp.stack([d[:, 0, :] for d in dmod[l]], axis=1) for l in range(DEPTH)])
    return loss, grad_x, grads, dmod_raw


MESH = pl.DeviceIdType.MESH
ANY = pl.BlockSpec(memory_space=pl.ANY)
PACK_W = 1024
PACK_TILE = 256


def _place():
    x, y, c = lax.axis_index("x"), lax.axis_index("y"), lax.axis_index("c")
    chips = [(1 - x, y), (x, 1 - y), (1 - x, 1 - y)]
    return x, y, c, chips


def _remote(src, dst, send_sem, recv_sem, to):
    return pltpu.make_async_remote_copy(src_ref=src, dst_ref=dst, send_sem=send_sem, recv_sem=recv_sem, device_id=to, device_id_type=MESH)


def all_gather8(name, v):
    m_per, n = v.shape

    def body(x_ref, out_ref, send_sems, recv_sems, local_sem):
        x, y, c, chips = _place()
        me, sibling = (x, y, c), (x, y, 1 - c)

        def rows(px, py, pc):
            return out_ref.at[pl.ds((4 * px + 2 * py + pc) * m_per, m_per), :]

        def copy(k, block, to, src=None):
            return _remote(rows(*block) if src is None else src, rows(*block), send_sems.at[k], recv_sems.at[k], to)

        mine = pltpu.make_async_copy(x_ref, rows(*me), local_sem)
        mine.start()
        first = [copy(0, me, sibling, src=x_ref)]
        first += [copy(1 + j, me, (*chip, c), src=x_ref) for j, chip in enumerate(chips)]
        for cp in first:
            cp.start()
        passed = [copy(4 + j, (*chip, c), sibling) for j, chip in enumerate(chips)]
        for j, chip in enumerate(chips):
            copy(1 + j, (*chip, c), me).wait_recv()
            passed[j].start()
        copy(0, sibling, me).wait_recv()
        for j, chip in enumerate(chips):
            copy(4 + j, (*chip, 1 - c), me).wait_recv()
        for cp in first + passed:
            cp.wait_send()
        mine.wait()

    return pl.pallas_call(
        body,
        name=name,
        out_shape=jax.ShapeDtypeStruct((N_DEV * m_per, n), v.dtype),
        in_specs=[pl.BlockSpec(memory_space=pltpu.VMEM)],
        out_specs=pl.BlockSpec(memory_space=pltpu.VMEM),
        scratch_shapes=[pltpu.SemaphoreType.DMA((7,)), pltpu.SemaphoreType.DMA((7,)), pltpu.SemaphoreType.DMA],
        compiler_params=pltpu.CompilerParams(vmem_limit_bytes=VMEM_LIMIT),
    )(v)


def gather_weights(pack):
    R, n = pack.shape
    rh = R // 2

    def body(x_ref, out_ref, send_sems, recv_sems):
        x, y, c, chips = _place()
        me = 2 * x + y

        def half(chip, hc):
            return out_ref.at[chip, pl.ds(hc * rh, rh), :]

        src = x_ref.at[pl.ds(c * rh, rh), :]
        first = [_remote(src, half(me, c), send_sems.at[j], recv_sems.at[j], (px, py, c)) for j, (px, py) in enumerate(chips)]
        for cp in first:
            cp.start()
        passed = []
        for j, (px, py) in enumerate(chips):
            got = half(2 * px + py, c)
            _remote(got, got, send_sems.at[j], recv_sems.at[j], (px, py, c)).wait_recv()
            cp = _remote(got, got, send_sems.at[3 + j], recv_sems.at[3 + j], (x, y, 1 - c))
            cp.start()
            passed.append(cp)
        for j, (px, py) in enumerate(chips):
            got = half(2 * px + py, 1 - c)
            _remote(got, got, send_sems.at[3 + j], recv_sems.at[3 + j], (x, y, 1 - c)).wait_recv()
        for cp in first + passed:
            cp.wait_send()

    return pl.pallas_call(
        body,
        name="gather_weights",
        out_shape=jax.ShapeDtypeStruct((N_CHIPS, R, n), pack.dtype),
        in_specs=[ANY],
        out_specs=ANY,
        scratch_shapes=[pltpu.SemaphoreType.DMA((6,)), pltpu.SemaphoreType.DMA((6,))],
    )(pack)


def swap_halves(g):
    n_slot, R, n = g.shape
    rh = R // 2

    def body(g_ref, got_ref, send_sem, recv_sem):
        x, y, c, _ = _place()
        cp = _remote(g_ref.at[:, pl.ds((1 - c) * rh, rh), :], got_ref, send_sem, recv_sem, (x, y, 1 - c))
        cp.start()
        cp.wait()

    return pl.pallas_call(
        body,
        name="swap_halves",
        out_shape=jax.ShapeDtypeStruct((n_slot, rh, n), g.dtype),
        in_specs=[ANY],
        out_specs=ANY,
        scratch_shapes=[pltpu.SemaphoreType.DMA, pltpu.SemaphoreType.DMA],
    )(g)


def scatter_chips(buf):
    def body(s_ref, got_ref, send_sems, recv_sems):
        x, y, c, chips = _place()
        me = 2 * x + y
        cps = [_remote(s_ref.at[2 * px + py], got_ref.at[me], send_sems.at[j], recv_sems.at[j], (px, py, c)) for j, (px, py) in enumerate(chips)]
        for cp in cps:
            cp.start()
        for j, (px, py) in enumerate(chips):
            got = got_ref.at[2 * px + py]
            _remote(got, got, send_sems.at[j], recv_sems.at[j], (px, py, c)).wait_recv()
        for cp in cps:
            cp.wait_send()

    return pl.pallas_call(
        body,
        name="scatter_chips",
        out_shape=jax.ShapeDtypeStruct(buf.shape, buf.dtype),
        in_specs=[ANY],
        out_specs=ANY,
        scratch_shapes=[pltpu.SemaphoreType.DMA((3,)), pltpu.SemaphoreType.DMA((3,))],
    )(buf)


def swap_with_sibling(h):
    def body(h_ref, got_ref, send_sem, recv_sem):
        x, y, c, _ = _place()
        cp = _remote(h_ref, got_ref, send_sem, recv_sem, (x, y, 1 - c))
        cp.start()
        cp.wait()

    return pl.pallas_call(
        body,
        name="swap_with_sibling",
        out_shape=jax.ShapeDtypeStruct(h.shape, h.dtype),
        in_specs=[ANY],
        out_specs=ANY,
        scratch_shapes=[pltpu.SemaphoreType.DMA, pltpu.SemaphoreType.DMA],
    )(h)


def chip_sum(g, got, core, chip):
    n_slot, R, n = g.shape
    rh = R // 2
    nb = rh // PACK_TILE

    def body(pos, g_ref, got_ref, wire_ref, own_ref):
        k = pl.program_id(1)
        s = g_ref[0] + got_ref[0]
        wire_ref[0] = s.astype(wire_ref.dtype)

        @pl.when(k == pos[1])
        def _():
            own_ref[...] = s

    grid_spec = pltpu.PrefetchScalarGridSpec(
        num_scalar_prefetch=1,
        grid=(nb, n_slot),
        in_specs=[
            pl.BlockSpec((1, PACK_TILE, n), lambda i, k, pos: (k, pos[0] * nb + i, 0)),
            pl.BlockSpec((1, PACK_TILE, n), lambda i, k, pos: (k, i, 0)),
        ],
        out_specs=[
            pl.BlockSpec((1, PACK_TILE, n), lambda i, k, pos: (k, i, 0)),
            pl.BlockSpec((PACK_TILE, n), lambda i, k, pos: (i, 0)),
        ],
    )
    return pl.pallas_call(
        body,
        name="chip_sum",
        grid_spec=grid_spec,
        out_shape=[jax.ShapeDtypeStruct((n_slot, rh, n), WIRE_DTYPE), jax.ShapeDtypeStruct((rh, n), F32)],
        compiler_params=_cparams(2),
    )(jnp.stack([core, chip]).astype(jnp.int32), g, got)


def mesh_sum(own, got, chip):
    rh, n = own.shape
    n_slot = got.shape[0]

    def body(pos, own_ref, *refs):
        out_ref = refs[-1]
        acc = own_ref[...]
        for k in range(n_slot):
            acc = acc + jnp.where(k != pos[0], refs[k][0].astype(F32), 0.0)
        out_ref[...] = acc

    grid_spec = pltpu.PrefetchScalarGridSpec(
        num_scalar_prefetch=1,
        grid=(rh // PACK_TILE,),
        in_specs=[pl.BlockSpec((PACK_TILE, n), lambda i, pos: (i, 0))]
        + [pl.BlockSpec((1, PACK_TILE, n), lambda i, pos, k=k: (k, i, 0)) for k in range(n_slot)],
        out_specs=pl.BlockSpec((PACK_TILE, n), lambda i, pos: (i, 0)),
    )
    return pl.pallas_call(
        body, name="mesh_sum", grid_spec=grid_spec, out_shape=jax.ShapeDtypeStruct((rh, n), F32), compiler_params=_cparams(1)
    )(jnp.stack([chip]).astype(jnp.int32), own, *([got] * n_slot))


def sum_devices(name, v):
    def body(v_ref, o_ref):
        acc = v_ref[0]
        for d in range(1, N_DEV):
            acc = acc + v_ref[d]
        o_ref[...] = acc

    return pl.pallas_call(body, name=name, out_shape=jax.ShapeDtypeStruct(v.shape[1:], F32))(v)


def adamw(name, w, g, m, v):
    shape = w.shape
    w2, g2, m2, v2 = (t.reshape(-1, shape[-1]) for t in (w, g, m, v))
    rows, n = w2.shape
    tr = _pick(rows, (256, 128, 64, 32, 16, 8))
    c1 = 1.0 / (1.0 - ADAM_B1**ADAM_STEP)
    c2 = 1.0 / (1.0 - ADAM_B2**ADAM_STEP)

    def body(w_ref, g_ref, m_ref, v_ref, d_ref, nm_ref, nv_ref):
        gv = g_ref[...]
        nm = ADAM_B1 * m_ref[...] + (1.0 - ADAM_B1) * gv
        nv = ADAM_B2 * v_ref[...] + (1.0 - ADAM_B2) * jnp.square(gv)
        d_ref[...] = -ADAM_LR * ((nm * c1) / (jnp.sqrt(nv * c2) + ADAM_EPS) + ADAM_WD * w_ref[...])
        nm_ref[...] = nm
        nv_ref[...] = nv

    spec = pl.BlockSpec((tr, n), lambda i: (i, 0))
    outs = pl.pallas_call(
        body,
        name=name,
        grid=(rows // tr,),
        in_specs=[spec] * 4,
        out_specs=[spec] * 3,
        out_shape=[jax.ShapeDtypeStruct((rows, n), F32)] * 3,
        compiler_params=_cparams(1),
    )(w2, g2, m2, v2)
    return [o.reshape(shape) for o in outs]


def ada_fwd(c_all, w_ada):
    n_tok, d = c_all.shape
    depth, _, cols = w_ada.shape
    tn = _pick(cols, (512, 384, 256, 128))

    def body(c_ref, w_ref, o_ref):
        o_ref[0] = jnp.dot(_silu(c_ref[...]).astype(MXU_DTYPE), w_ref[0].astype(MXU_DTYPE), preferred_element_type=F32)

    return pl.pallas_call(
        body,
        name="ada_fwd",
        grid=(depth, cols // tn),
        in_specs=[pl.BlockSpec((n_tok, d), lambda l, j: (0, 0)), pl.BlockSpec((1, d, tn), lambda l, j: (l, 0, j))],
        out_specs=pl.BlockSpec((1, n_tok, tn), lambda l, j: (l, 0, j)),
        out_shape=jax.ShapeDtypeStruct((depth, n_tok, cols), F32),
        compiler_params=_cparams(2),
    )(c_all, w_ada)


COL_SHARDED = ("w_in", "w_uq", "w_ukv", "w_up")


def _pack_rows(parts, rows):
    parts = [p.reshape(-1, PACK_W) for p in parts]
    fill = jnp.zeros((rows - sum(p.shape[0] for p in parts), PACK_W), parts[0].dtype)
    return jnp.concatenate(parts + [fill], axis=0)


def _big_rows(shards):
    n = sum(int(np.prod(shards[k].shape)) for k in BIG) // PACK_W
    return -(-n // (2 * PACK_TILE)) * 2 * PACK_TILE


def unpack_gathered(gathered, shard_shapes):
    out, r0 = {}, 0
    for k in BIG:
        shp = shard_shapes[k]
        n = int(np.prod(shp)) // PACK_W
        seg = gathered[:, r0 : r0 + n].reshape(N_CHIPS, *shp)
        r0 += n
        if k == "w_up":
            seg = seg.reshape(2, 2, shp[0], shp[1], shp[2] // LANE, LANE)
            out[k] = seg.transpose(2, 3, 1, 4, 0, 5).reshape(shp[0], shp[1], N_CHIPS * shp[2])
        elif k in COL_SHARDED:
            out[k] = seg.transpose(1, 2, 0, 3).reshape(shp[0], shp[1], N_CHIPS * shp[2])
        else:
            out[k] = seg.transpose(1, 0, 2, 3).reshape(shp[0], N_CHIPS * shp[1], shp[2])
    return out


def pack_full_grads(grads, shard_shapes, rows):
    parts = []
    for k in BIG:
        shp = shard_shapes[k]
        g = grads[k]
        if k == "w_up":
            g = g.reshape(shp[0], shp[1], 2, shp[2] // LANE, 2, LANE).transpose(4, 2, 0, 1, 3, 5)
        elif k in COL_SHARDED:
            g = g.reshape(shp[0], shp[1], N_CHIPS, shp[2]).transpose(2, 0, 1, 3)
        else:
            g = g.reshape(shp[0], N_CHIPS, shp[1], shp[2]).transpose(1, 0, 2, 3)
        parts.append(g.reshape(N_CHIPS, -1, PACK_W))
    fill = jnp.zeros((N_CHIPS, rows - sum(p.shape[1] for p in parts), PACK_W), F32)
    return jnp.concatenate(parts + [fill], axis=1)


def unpack_shards(buf, shard_shapes):
    out, r0 = {}, 0
    for k in BIG:
        n = int(np.prod(shard_shapes[k])) // PACK_W
        out[k] = buf[r0 : r0 + n].reshape(shard_shapes[k])
        r0 += n
    return out


def _flat_pack(arrs, row_multiple=8):
    flat = jnp.concatenate([a.reshape(-1) for a in arrs])
    per = PACK_W * row_multiple
    n = -(-flat.shape[0] // per) * per
    return jnp.pad(flat, (0, n - flat.shape[0])).reshape(-1, PACK_W)


def _flat_unpack(buf, shapes):
    flat, out, o = buf.reshape(-1), [], 0
    for s in shapes:
        n = int(np.prod(s))
        out.append(flat[o : o + n].reshape(s))
        o += n
    return out


WEIGHTS = (
    "w_ada", "b_ada", "norm_mix", "w_in", "conv_w", "conv_b", "dt_bias", "a_log", "d_skip", "ssd_norm", "q_norm", "w_uq",
    "kv_norm", "w_ukv", "attn_norm", "w_out", "norm_mlp", "w_up", "conv_ff_w", "conv_ff_b", "w_down", "final_norm",
)
REPLICATED = ("b_ada", "norm_mix", "conv_b", "dt_bias", "a_log", "d_skip", "ssd_norm", "q_norm", "kv_norm", "attn_norm",
              "norm_mlp", "conv_ff_b", "final_norm")
CONV_SHARDED = ("conv_w", "conv_ff_w")


def kernel(x, c, positions, w_ada, b_ada, norm_mix, w_in, conv_w, conv_b, dt_bias, a_log, d_skip, ssd_norm, q_norm, w_uq, kv_norm, w_ukv, attn_norm, w_out, norm_mlp, w_up, conv_ff_w, conv_ff_b, w_down, final_norm, loss_target, m_w_ada, m_b_ada, m_norm_mix, m_w_in, m_conv_w, m_conv_b, m_dt_bias, m_a_log, m_d_skip, m_ssd_norm, m_q_norm, m_w_uq, m_kv_norm, m_w_ukv, m_attn_norm, m_w_out, m_norm_mlp, m_w_up, m_conv_ff_w, m_conv_ff_b, m_w_down, m_final_norm, v_w_ada, v_b_ada, v_norm_mix, v_w_in, v_conv_w, v_conv_b, v_dt_bias, v_a_log, v_d_skip, v_ssd_norm, v_q_norm, v_w_uq, v_kv_norm, v_w_ukv, v_attn_norm, v_w_out, v_norm_mlp, v_w_up, v_conv_ff_w, v_conv_ff_b, v_w_down, v_final_norm):
    loc = locals()
    Wl = {k: loc[k] for k in WEIGHTS}
    Ml = {k: loc["m_" + k] for k in WEIGHTS}
    Vl = {k: loc["v_" + k] for k in WEIGHTS}
    xi, yi, ci = lax.axis_index("x"), lax.axis_index("y"), lax.axis_index("c")
    chip = 2 * xi + yi
    dev = 2 * chip + ci
    B, S, D = x.shape
    n_tok = N_DEV * B

    c_all = all_gather8("gather_c", c.reshape(8, -1)).reshape(n_tok, D)
    mod_cols = ada_fwd(c_all, w_ada)
    cols = mod_cols.shape[-1]
    half = n_tok // 2
    mod_mine = lax.dynamic_slice_in_dim(mod_cols, ci * half, half, axis=1)
    small_in = _flat_pack([mod_mine, conv_w, conv_ff_w])
    n_mod = mod_mine.size // PACK_W
    small_all = all_gather8("gather_mod", small_in).reshape(N_CHIPS, 2, -1, PACK_W)
    mod_all = small_all[:, :, :n_mod].reshape(N_CHIPS, 2, DEPTH, half, cols).transpose(2, 1, 3, 0, 4).reshape(DEPTH, n_tok, N_CHIPS * cols)
    mod_raw = lax.dynamic_slice_in_dim(mod_all, dev * B, B, axis=1).reshape(DEPTH, B, 6, D)
    conv_parts = [_flat_unpack(small_all[k, 0, n_mod:], [conv_w.shape, conv_ff_w.shape]) for k in range(N_CHIPS)]
    conv_full = {name: jnp.concatenate([conv_parts[k][i] for k in range(N_CHIPS)], axis=-1) for i, name in enumerate(CONV_SHARDED)}

    shard_shapes = {k: Wl[k].shape for k in BIG}
    rows = _big_rows(Wl)
    pack = _pack_rows([Wl[k].astype(MXU_DTYPE) for k in BIG], rows)
    gathered = lax.dynamic_update_slice_in_dim(gather_weights(pack), pack[None], chip, axis=0)
    W = unpack_gathered(gathered, shard_shapes)
    W.update({k: Wl[k] for k in REPLICATED})
    W.update(conv_full)

    loss_lanes, grad_x, grads, dmod_raw = local_step(x, loss_target, positions, mod_raw, W)
    loss = lax.psum(loss_lanes[0, 0], ("x", "y", "c"))

    gpack = pack_full_grads(grads, shard_shapes, rows)
    wire, own = chip_sum(gpack, swap_halves(gpack), ci, chip)
    mine = mesh_sum(own, scatter_chips(wire), chip)
    theirs = swap_with_sibling(mine)
    both = jnp.concatenate([jnp.where(ci == 0, mine, theirs), jnp.where(ci == 0, theirs, mine)], axis=0)
    g_shard = unpack_shards(both, shard_shapes)

    small_names = REPLICATED + CONV_SHARDED
    small_out = _flat_pack([grads[k] for k in small_names] + [dmod_raw])
    small_got = all_gather8("gather_small", small_out).reshape(N_DEV, -1, PACK_W)
    small_sum = _flat_unpack(sum_devices("sum_small", small_got), [grads[k].shape for k in small_names])
    G = dict(zip(small_names, small_sum))
    for name in CONV_SHARDED:
        width = Wl[name].shape[-1]
        G[name] = lax.dynamic_slice_in_dim(G[name], chip * width, width, axis=-1)
    n_small = sum(grads[k].size for k in small_names)
    dmod_all = small_got.reshape(N_DEV, -1)[:, n_small : n_small + dmod_raw.size].reshape(N_DEV, DEPTH, B, 6 * D)
    dmod_all = dmod_all.transpose(1, 0, 2, 3).reshape(DEPTH, n_tok, 6 * D)
    dmod_cols = lax.dynamic_slice_in_dim(dmod_all, chip * cols, cols, axis=-1)
    G["w_ada"] = jnp.stack([matmul(f"w_ada_wgrad{l}", c_all, dmod_cols[l], ta=True, a_act=_silu) for l in range(DEPTH)])
    G.update(g_shard)

    deltas, new_m, new_v = {}, {}, {}
    small_upd = adamw("adamw_small", *[_flat_pack([t[k] for k in REPLICATED]) for t in (Wl, G, Ml, Vl)])
    for res, t in zip((deltas, new_m, new_v), small_upd):
        res.update(zip(REPLICATED, _flat_unpack(t, [Wl[k].shape for k in REPLICATED])))
    for k in WEIGHTS:
        if k not in REPLICATED:
            deltas[k], new_m[k], new_v[k] = adamw("adamw_" + k, Wl[k], G[k], Ml[k], Vl[k])
    return (loss, grad_x, *[G[k] for k in WEIGHTS], *[deltas[k] for k in WEIGHTS], *[new_m[k] for k in WEIGHTS], *[new_v[k] for k in WEIGHTS])
```

```python
import functools
from typing import NamedTuple

import numpy as np
import jax
import jax.numpy as jnp
from jax import lax
from jax.experimental import pallas as pl
from jax.experimental.pallas import tpu as pltpu

F32 = jnp.float32
BF16 = jnp.bfloat16
MXU_DTYPE = jnp.bfloat16
WIRE_DTYPE = jnp.bfloat16
HIGHEST = lax.Precision.HIGHEST

D_MODEL = 1024
DEPTH = 2
D_SSD = 1024
SSD_HEADS = 16
SSD_HEAD_DIM = 64
SSD_GROUPS = 2
SSD_STATE = 128
SSD_CONV = 4
CHUNK = 128
MLA_HEADS = 8
QK_NOPE = 128
QK_ROPE = 64
V_DIM = 128
D_ATT = MLA_HEADS * V_DIM
Q_RANK = 384
KV_RANK = 256
ROPE_BASE = 10000.0
D_FF = 2816
FF_CONV = 3
EPS = 1e-6
D_XBC = D_SSD + 2 * SSD_GROUPS * SSD_STATE
D_IN = D_SSD + D_XBC + SSD_HEADS + Q_RANK + KV_RANK + QK_ROPE
ADAM_LR, ADAM_B1, ADAM_B2, ADAM_EPS, ADAM_WD, ADAM_STEP = 0.001, 0.9, 0.999, 1e-08, 0.01, 10

LANE = 128
N_CHIPS = 4
N_DEV = 8

OFF_Z, OFF_XBC, OFF_CKV, OFF_KR, OFF_DT, OFF_CQ = 0, 1024, 2560, 2816, 2944, 3072
N_PROJ = 3456
Q_HEAD_PAD = 256
VMEM_LIMIT = 56 * 1024 * 1024


def _cparams(n_axes):
    return pltpu.CompilerParams(dimension_semantics=("arbitrary",) * n_axes, vmem_limit_bytes=VMEM_LIMIT)


def _pick(n, prefs):
    for p in prefs:
        if n % p == 0:
            return p
    return n


def _silu(x):
    return x * jax.nn.sigmoid(x)


def _rms(x, g):
    return x * lax.rsqrt(jnp.mean(x * x, axis=-1, keepdims=True) + EPS) * g


def _softplus(x):
    return jnp.maximum(x, 0.0) + jnp.log(1.0 + jnp.exp(-jnp.abs(x)))


class Col(NamedTuple):
    arr: jax.Array
    width: int
    blk: int


def _as_col(a):
    return a if isinstance(a, Col) else Col(a, a.shape[-1], 0)


def _row_specs(tiled, pbatch, shared, tile):
    specs = [pl.BlockSpec((1, tile, t.width), lambda b, i, blk=t.blk: (b, i, blk)) for t in tiled]
    specs += [pl.BlockSpec((1, 1, p.shape[-1]), lambda b, i: (b, 0, 0)) for p in pbatch]
    specs += [pl.BlockSpec(s.shape, lambda b, i: (0, 0)) for s in shared]
    return specs


def _row_vals(refs, nt, npb):
    return [r[0].astype(F32) for r in refs[: nt + npb]] + [r[...].astype(F32) for r in refs[nt + npb :]]


def rowwise_fwd(name, fn, tiled, pbatch, shared, outs, tile=512):
    tiled = [_as_col(t) for t in tiled]
    B, S = tiled[0].arr.shape[:2]
    tile = min(tile, S)
    nt, npb, nsh = len(tiled), len(pbatch), len(shared)
    n_in = nt + npb + nsh

    def body(*refs):
        res = fn(*_row_vals(refs[:n_in], nt, npb))
        for r, v in zip(refs[n_in:], res):
            r[0] = v.astype(r.dtype)

    return pl.pallas_call(
        body,
        name=name,
        grid=(B, S // tile),
        in_specs=_row_specs(tiled, pbatch, shared, tile),
        out_specs=[pl.BlockSpec((1, tile, w), lambda b, i: (b, i, 0)) for w, _ in outs],
        out_shape=[jax.ShapeDtypeStruct((B, S, w), dt) for w, dt in outs],
        compiler_params=_cparams(2),
    )(*[t.arr for t in tiled], *pbatch, *shared)


def rowwise_bwd(name, fn, tiled, pbatch, shared, cts, grad_tiled, grad_shared, adds=None, tile=256, mxu_only=()):
    tiled = [_as_col(t) for t in tiled]
    adds = adds or {}
    B, S = tiled[0].arr.shape[:2]
    tile = min(tile, S)
    nt, npb, nsh = len(tiled), len(pbatch), len(shared)
    n_in = nt + npb + nsh
    add_idx = sorted(adds)
    gt = [i for i in range(nt) if grad_tiled[i]]
    gs = [i for i in range(nsh) if grad_shared[i]]
    diff = gt + [nt + i for i in range(npb)] + [nt + npb + i for i in gs]
    n_ct, n_add = len(cts), len(add_idx)

    def body(*refs):
        vals = _row_vals(refs[:n_in], nt, npb)
        ct_v = tuple(r[0].astype(F32) for r in refs[n_in : n_in + n_ct])
        add_v = {i: r[0].astype(F32) for i, r in zip(add_idx, refs[n_in + n_ct : n_in + n_ct + n_add])}
        out_refs = refs[n_in + n_ct + n_add :]

        def f(*dargs):
            full = list(vals)
            for k, i in enumerate(diff):
                full[i] = dargs[k]
            return tuple(fn(*full))

        _, vjp = jax.vjp(f, *[vals[i] for i in diff])
        grads = vjp(ct_v)
        b, i = pl.program_id(0), pl.program_id(1)
        k = 0
        for idx in gt:
            g = grads[k]
            if idx in add_v:
                g = g + add_v[idx]
            out_refs[k][0] = g.astype(out_refs[k].dtype)
            k += 1
        for _ in range(npb):
            r, g = out_refs[k], grads[k]

            @pl.when(i == 0)
            def _(r=r, g=g):
                r[0] = g

            @pl.when(i > 0)
            def _(r=r, g=g):
                r[0] += g

            k += 1
        for _ in gs:
            r, g = out_refs[k], grads[k]

            @pl.when((i == 0) & (b == 0))
            def _(r=r, g=g):
                r[...] = g

            @pl.when((i > 0) | (b > 0))
            def _(r=r, g=g):
                r[...] += g

            k += 1

    out_specs = [pl.BlockSpec((1, tile, tiled[i].width), lambda b, i: (b, i, 0)) for i in gt]
    out_shape = [jax.ShapeDtypeStruct((B, S, tiled[i].width), MXU_DTYPE if i in mxu_only else F32) for i in gt]
    out_specs += [pl.BlockSpec((1, 1, p.shape[-1]), lambda b, i: (b, 0, 0)) for p in pbatch]
    out_shape += [jax.ShapeDtypeStruct(p.shape, F32) for p in pbatch]
    out_specs += [pl.BlockSpec(shared[i].shape, lambda b, i: (0, 0)) for i in gs]
    out_shape += [jax.ShapeDtypeStruct(shared[i].shape, F32) for i in gs]
    in_specs = _row_specs(tiled, pbatch, shared, tile)
    in_specs += [pl.BlockSpec((1, tile, c.shape[-1]), lambda b, i: (b, i, 0)) for c in cts]
    in_specs += [pl.BlockSpec((1, tile, adds[i].shape[-1]), lambda b, i: (b, i, 0)) for i in add_idx]
    return pl.pallas_call(
        body,
        name=name,
        grid=(B, S // tile),
        in_specs=in_specs,
        out_specs=out_specs,
        out_shape=out_shape,
        compiler_params=_cparams(2),
    )(*[t.arr for t in tiled], *pbatch, *shared, *cts, *[adds[i] for i in add_idx])


def fn_prenorm(x, sc, sh, g, bsc, bsh):
    return (_rms(x, g) * (1.0 + sc + bsc) + (sh + bsh),)


def fn_resid_prenorm(x, y, gate, sc, sh, g, bgate, bsc, bsh):
    x1 = x + (gate + bgate) * y
    return x1, _rms(x1, g) * (1.0 + sc + bsc) + (sh + bsh)


def fn_mix(yscan, z, o, ssd_norm, attn_norm):
    y = yscan * _silu(z)
    half = D_SSD // SSD_GROUPS
    first = lax.broadcasted_iota(jnp.int32, y.shape, 1) < half
    sq = y * y
    m0 = jnp.sum(jnp.where(first, sq, 0.0), axis=-1, keepdims=True) / half
    m1 = jnp.sum(jnp.where(first, 0.0, sq), axis=-1, keepdims=True) / half
    r = jnp.where(first, lax.rsqrt(m0 + EPS), lax.rsqrt(m1 + EPS))
    return (jnp.concatenate([y * r * ssd_norm, _rms(o, attn_norm)], axis=-1),)


def fn_mla_prep(cq, ckv, kr, cos_t, sin_t, q_norm, kv_norm, rot):
    krr = kr * cos_t + jnp.dot(kr, rot, precision=HIGHEST, preferred_element_type=F32) * sin_t
    return _rms(cq, q_norm), _rms(ckv, kv_norm), krr


def final_fwdbwd(x1, ff, target, gate, fnorm, bgate, tile=256):
    B, S, D = x1.shape
    tile = min(tile, S)

    def body(x_ref, f_ref, t_ref, g_ref, n_ref, bg_ref, loss_ref, dx_ref, df_ref, dg_ref, dn_ref, dbg_ref):
        b, i = pl.program_id(0), pl.program_id(1)
        tgt = t_ref[0]

        def f(x, y, gate, fn, bg):
            yf = _rms(x + (gate + bg) * y, fn)
            return 0.5 * jnp.sum(jnp.mean(jnp.square(yf - tgt), axis=-1, keepdims=True), axis=0, keepdims=True)

        val, vjp = jax.vjp(f, x_ref[0], f_ref[0], g_ref[0], n_ref[...], bg_ref[...])
        dx, dff, dg, dn, dbg = vjp(jnp.ones((1, 1), F32))
        dx_ref[0] = dx
        df_ref[0] = dff.astype(df_ref.dtype)
        lane_loss = jnp.broadcast_to(val, (1, LANE))

        @pl.when(i == 0)
        def _():
            dg_ref[0] = dg

        @pl.when(i > 0)
        def _():
            dg_ref[0] += dg

        @pl.when((i == 0) & (b == 0))
        def _():
            dn_ref[...] = dn
            dbg_ref[...] = dbg
            loss_ref[...] = lane_loss

        @pl.when((i > 0) | (b > 0))
        def _():
            dn_ref[...] += dn
            dbg_ref[...] += dbg
            loss_ref[...] += lane_loss

    tok = pl.BlockSpec((1, tile, D), lambda b, i: (b, i, 0))
    pb = pl.BlockSpec((1, 1, D), lambda b, i: (b, 0, 0))
    sh = pl.BlockSpec((1, D), lambda b, i: (0, 0))
    return pl.pallas_call(
        body,
        name="final_loss",
        grid=(B, S // tile),
        in_specs=[tok, tok, tok, pb, sh, sh],
        out_specs=[pl.BlockSpec((1, LANE), lambda b, i: (0, 0)), tok, tok, pb, sh, sh],
        out_shape=[
            jax.ShapeDtypeStruct((1, LANE), F32),
            jax.ShapeDtypeStruct((B, S, D), F32),
            jax.ShapeDtypeStruct((B, S, D), MXU_DTYPE),
            jax.ShapeDtypeStruct((B, 1, D), F32),
            jax.ShapeDtypeStruct((1, D), F32),
            jax.ShapeDtypeStruct((1, D), F32),
        ],
        compiler_params=_cparams(2),
    )(x1, ff, target, gate, fnorm, bgate)


MATMUL_VMEM_BUDGET = 40 * 1024 * 1024
STEP_COST_BYTES = 1.2e6
MXU_DIM = 256
MXU_FLOPS_PER_BYTE = 280.0


def _tile_options(n, cap):
    opts = [d for d in range(LANE, min(n, cap) + 1, LANE) if n % d == 0]
    return opts or [n]


def _matmul_tiles(M, N, K, sa, sb, so):
    best = None
    for tk in [K] + [d for d in _tile_options(K, 2048) if d >= 512 and d < K]:
        nk = K // tk
        for tm in _tile_options(M, 2048):
            for tn in _tile_options(N, 1408):
                acc = 0 if (so == 4 or nk == 1) else tm * tn * 4
                vmem = 2 * (tm * tk * sa + tk * tn * sb + tm * tn * so) + acc + tm * tn * 4 + 2 * (tm * tk + tk * tn)
                if vmem > MATMUL_VMEM_BUDGET:
                    continue
                a_reads = M * K * sa * (1 if nk == 1 else N // tn)
                b_reads = K * N * sb * (1 if (nk == 1 and N == tn) else M // tm)
                steps = (M // tm) * (N // tn) * nk
                hbm = a_reads + b_reads + M * N * so + (2 * M * N * 4 * (nk - 1) if nk > 1 else 0) / 8
                fill = (-(-tn // MXU_DIM) * MXU_DIM / tn) * (-(-tk // MXU_DIM) * MXU_DIM / tk)
                mxu = 2.0 * M * N * K * fill / MXU_FLOPS_PER_BYTE
                cost = max(hbm, mxu) + steps * STEP_COST_BYTES
                if best is None or cost < best[0]:
                    best = (cost, tm, tn, tk)
    assert best is not None, (M, N, K)
    return best[1:]


class Layer(NamedTuple):
    arr: jax.Array
    l: int


def matmul(name, a, b, ta=False, tb=False, out_dtype=F32, a_act=None):
    lead = None
    if a.ndim == 3 and not ta:
        lead = a.shape[:2]
    if a.ndim == 3:
        a = a.reshape(-1, a.shape[-1])
    layer = None
    if isinstance(b, Layer):
        b, layer = b
        b_rows, b_cols = b.shape[1:]
    else:
        if b.ndim == 3:
            b = b.reshape(-1, b.shape[-1])
        b_rows, b_cols = b.shape
    K, M = a.shape if ta else a.shape[::-1]
    N = b_rows if tb else b_cols
    assert (b_cols if tb else b_rows) == K, (name, a.shape, b.shape)
    tm, tn, tk = _matmul_tiles(M, N, K, a.dtype.itemsize, b.dtype.itemsize, jnp.dtype(out_dtype).itemsize)
    nk = K // tk
    dims = (((0 if ta else 1,), (1 if tb else 0,)), ((), ()))
    direct = jnp.dtype(out_dtype) == jnp.dtype(F32)

    def body(a_ref, b_ref, o_ref, *scratch):
        k = pl.program_id(2)
        av = a_ref[...]
        if a_act is not None:
            av = a_act(av.astype(F32))
        part = lax.dot_general(av.astype(MXU_DTYPE), b_ref[...].astype(MXU_DTYPE), dims, preferred_element_type=F32)
        if nk == 1:
            o_ref[...] = part.astype(o_ref.dtype)
            return
        acc = o_ref if direct else scratch[0]

        @pl.when(k == 0)
        def _():
            acc[...] = part

        @pl.when(k > 0)
        def _():
            acc[...] += part

        if not direct:

            @pl.when(k == nk - 1)
            def _():
                o_ref[...] = acc[...].astype(o_ref.dtype)

    a_spec = pl.BlockSpec((tk, tm), lambda i, j, k: (k, i)) if ta else pl.BlockSpec((tm, tk), lambda i, j, k: (i, k))
    b_block, b_index = ((tn, tk), lambda i, j, k: (j, k)) if tb else ((tk, tn), lambda i, j, k: (k, j))
    if layer is None:
        b_spec = pl.BlockSpec(b_block, b_index)
    else:
        b_spec = pl.BlockSpec((None, *b_block), lambda i, j, k: (layer, *b_index(i, j, k)))
    out = pl.pallas_call(
        body,
        name=name,
        grid=(M // tm, N // tn, nk),
        in_specs=[a_spec, b_spec],
        out_specs=pl.BlockSpec((tm, tn), lambda i, j, k: (i, j)),
        out_shape=jax.ShapeDtypeStruct((M, N), out_dtype),
        scratch_shapes=[] if (direct or nk == 1) else [pltpu.VMEM((tm, tn), F32)],
        compiler_params=_cparams(3),
    )(a, b)
    return out.reshape(*lead, N) if lead is not None else out


SUBLANES = 8


def _shift_down(u, d):
    if d == 0:
        return u
    r = pltpu.roll(u, d, 0)
    t = lax.broadcasted_iota(jnp.int32, (SUBLANES, u.shape[1]), 0)
    return jnp.concatenate([jnp.where(t >= d, r[:SUBLANES], 0.0), r[SUBLANES:]], axis=0)


def _shift_up(u, d):
    if d == 0:
        return u
    s = u.shape[0]
    r = pltpu.roll(u, s - d, 0)
    t = lax.broadcasted_iota(jnp.int32, (SUBLANES, u.shape[1]), 0)
    return jnp.concatenate([r[: s - SUBLANES], jnp.where(t < SUBLANES - d, r[s - SUBLANES :], 0.0)], axis=0)


def _conv(u, w, b):
    k = w.shape[0]
    out = b + w[k - 1 : k, :] * u
    for j in range(k - 1):
        out = out + w[j : j + 1, :] * _shift_down(u, k - 1 - j)
    return out


def _conv_bwd(u, w, dc):
    k = w.shape[0]
    du = w[k - 1 : k, :] * dc
    dws = []
    for j in range(k - 1):
        du = du + w[j : j + 1, :] * _shift_up(dc, k - 1 - j)
        dws.append(jnp.sum(dc * _shift_down(u, k - 1 - j), axis=0, keepdims=True))
    dws.append(jnp.sum(dc * u, axis=0, keepdims=True))
    return du, jnp.concatenate(dws, axis=0), jnp.sum(dc, axis=0, keepdims=True)


def conv_silu_fwd(proj, w, b):
    B, S, _ = proj.shape
    k, c = w.shape
    blk0 = OFF_XBC // LANE

    def body(u_ref, w_ref, b_ref, o_ref):
        o_ref[0] = _silu(_conv(u_ref[0], w_ref[...], b_ref[...]))

    return pl.pallas_call(
        body,
        name="conv_silu_fwd",
        grid=(B, c // LANE),
        in_specs=[
            pl.BlockSpec((1, S, LANE), lambda bi, j: (bi, 0, blk0 + j)),
            pl.BlockSpec((k, LANE), lambda bi, j: (0, j)),
            pl.BlockSpec((1, LANE), lambda bi, j: (0, j)),
        ],
        out_specs=pl.BlockSpec((1, S, LANE), lambda bi, j: (bi, 0, j)),
        out_shape=jax.ShapeDtypeStruct((B, S, c), F32),
        compiler_params=_cparams(2),
    )(proj, w, b)


def conv_silu_bwd(proj, w, b, douts):
    B, S, _ = proj.shape
    k, c = w.shape
    blk0 = OFF_XBC // LANE
    ends = np.cumsum([d.shape[-1] // LANE for d in douts]).tolist()
    starts = [0] + ends[:-1]

    def body(u_ref, w_ref, b_ref, *refs):
        d_refs, (du_ref, dw_ref, db_ref) = refs[: len(douts)], refs[len(douts) :]
        j, bi = pl.program_id(0), pl.program_id(1)
        u, wv = u_ref[0], w_ref[...]
        cv = _conv(u, wv, b_ref[...])
        sg = jax.nn.sigmoid(cv)
        dout = d_refs[-1][0]
        for r in reversed(range(len(douts) - 1)):
            dout = jnp.where(j < ends[r], d_refs[r][0], dout)
        dc = dout * (sg * (1.0 + cv * (1.0 - sg)))
        du, dw, db = _conv_bwd(u, wv, dc)
        du_ref[0] = du.astype(du_ref.dtype)

        @pl.when(bi == 0)
        def _():
            dw_ref[...] = dw
            db_ref[...] = db

        @pl.when(bi > 0)
        def _():
            dw_ref[...] += dw
            db_ref[...] += db

    return pl.pallas_call(
        body,
        name="conv_silu_bwd",
        grid=(c // LANE, B),
        in_specs=[
            pl.BlockSpec((1, S, LANE), lambda j, bi: (bi, 0, blk0 + j)),
            pl.BlockSpec((k, LANE), lambda j, bi: (0, j)),
            pl.BlockSpec((1, LANE), lambda j, bi: (0, j)),
        ]
        + [
            pl.BlockSpec((1, S, LANE), lambda j, bi, lo=lo, hi=hi: (bi, 0, jnp.clip(j, lo, hi - 1) - lo))
            for lo, hi in zip(starts, ends)
        ],
        out_specs=[
            pl.BlockSpec((1, S, LANE), lambda j, bi: (bi, 0, j)),
            pl.BlockSpec((k, LANE), lambda j, bi: (0, j)),
            pl.BlockSpec((1, LANE), lambda j, bi: (0, j)),
        ],
        out_shape=[
            jax.ShapeDtypeStruct((B, S, c), MXU_DTYPE),
            jax.ShapeDtypeStruct((k, c), F32),
            jax.ShapeDtypeStruct((1, c), F32),
        ],
        compiler_params=_cparams(2),
    )(proj, w, b, *douts)


def conv_glu_fwd(u, w, b):
    B, S, c2 = u.shape
    k = w.shape[0]

    def body(u_ref, w_ref, b_ref, o_ref):
        cv = _conv(u_ref[0], w_ref[...], b_ref[...])
        o_ref[0] = (_silu(cv[:, :LANE]) * cv[:, LANE:]).astype(o_ref.dtype)

    return pl.pallas_call(
        body,
        name="conv_glu_fwd",
        grid=(B, c2 // (2 * LANE)),
        in_specs=[
            pl.BlockSpec((1, S, 2 * LANE), lambda bi, j: (bi, 0, j)),
            pl.BlockSpec((k, 2 * LANE), lambda bi, j: (0, j)),
            pl.BlockSpec((1, 2 * LANE), lambda bi, j: (0, j)),
        ],
        out_specs=pl.BlockSpec((1, S, LANE), lambda bi, j: (bi, 0, j)),
        out_shape=jax.ShapeDtypeStruct((B, S, c2 // 2), MXU_DTYPE),
        compiler_params=_cparams(2),
    )(u, w, b)


def conv_glu_bwd(u, w, b, da):
    B, S, c2 = u.shape
    k = w.shape[0]

    def body(u_ref, w_ref, b_ref, d_ref, du_ref, dw_ref, db_ref):
        bi = pl.program_id(1)
        uv, wv = u_ref[0], w_ref[...]
        cv = _conv(uv, wv, b_ref[...])
        gate, val = cv[:, :LANE], cv[:, LANE:]
        sg = jax.nn.sigmoid(gate)
        dav = d_ref[0].astype(F32)
        dc = jnp.concatenate([dav * val * (sg * (1.0 + gate * (1.0 - sg))), dav * gate * sg], axis=1)
        du, dw, db = _conv_bwd(uv, wv, dc)
        du_ref[0] = du.astype(du_ref.dtype)

        @pl.when(bi == 0)
        def _():
            dw_ref[...] = dw
            db_ref[...] = db

        @pl.when(bi > 0)
        def _():
            dw_ref[...] += dw
            db_ref[...] += db

    return pl.pallas_call(
        body,
        name="conv_glu_bwd",
        grid=(c2 // (2 * LANE), B),
        in_specs=[
            pl.BlockSpec((1, S, 2 * LANE), lambda j, bi: (bi, 0, j)),
            pl.BlockSpec((k, 2 * LANE), lambda j, bi: (0, j)),
            pl.BlockSpec((1, 2 * LANE), lambda j, bi: (0, j)),
            pl.BlockSpec((1, S, LANE), lambda j, bi: (bi, 0, j)),
        ],
        out_specs=[
            pl.BlockSpec((1, S, 2 * LANE), lambda j, bi: (bi, 0, j)),
            pl.BlockSpec((k, 2 * LANE), lambda j, bi: (0, j)),
            pl.BlockSpec((1, 2 * LANE), lambda j, bi: (0, j)),
        ],
        out_shape=[
            jax.ShapeDtypeStruct((B, S, c2), MXU_DTYPE),
            jax.ShapeDtypeStruct((k, c2), F32),
            jax.ShapeDtypeStruct((1, c2), F32),
        ],
        compiler_params=_cparams(2),
    )(u, w, b, da)


N_PAIR = SSD_HEADS // SSD_GROUPS // 2


def _ssd_chunk(prev, xs, bms, cms, dtraw, dt_bias, a_log, d_skip):
    L = CHUNK
    lane = lax.broadcasted_iota(jnp.int32, (1, LANE), 1)
    sub = lax.broadcasted_iota(jnp.int32, (LANE, 1), 0)
    row = lax.broadcasted_iota(jnp.int32, (L, L), 0)
    col = lax.broadcasted_iota(jnp.int32, (L, L), 1)
    tri = (row >= col).astype(F32)
    first = lane < SSD_HEAD_DIM

    dt = _softplus(dtraw + dt_bias)
    a = dt * (-jnp.exp(a_log))
    acs = jnp.dot(tri, a, precision=HIGHEST, preferred_element_type=F32)
    acs_t = acs.T
    a_end = jnp.sum(a, axis=0, keepdims=True)

    def lane_of(v, h):
        return jnp.sum(jnp.where(lane == h, v, 0.0), axis=1, keepdims=True)

    def expand(v, ha):
        return jnp.where(first, lane_of(v, ha), lane_of(v, ha + 1))

    ys, news = [], []
    for g in range(SSD_GROUPS):
        bm, cm = bms[g].astype(MXU_DTYPE), cms[g].astype(MXU_DTYPE)
        cb = lax.dot_general(cm, bm, (((1,), (1,)), ((), ())), preferred_element_type=F32)
        for j in range(N_PAIR):
            ha = g * (SSD_HEADS // SSD_GROUPS) + 2 * j
            x, state = xs[g * N_PAIR + j], prev[g * N_PAIR + j]
            dt_e, acs_e, end_e = expand(dt, ha), expand(acs, ha), expand(a_end, ha)
            xdt = x * dt_e
            y = jnp.dot(cm, state.astype(MXU_DTYPE), preferred_element_type=F32) * jnp.exp(acs_e)
            st = lax.dot_general(bm, (xdt * jnp.exp(end_e - acs_e)).astype(MXU_DTYPE), (((0,), (0,)), ((), ())), preferred_element_type=F32)
            news.append(state * jnp.exp(end_e) + st)
            for hh in range(2):
                h = ha + hh
                seg = lane_of(acs, h) - jnp.sum(jnp.where(sub == h, acs_t, 0.0), axis=0, keepdims=True)
                decay = jnp.exp(jnp.where(row >= col, seg, -jnp.inf))
                xh = jnp.where(first if hh == 0 else jnp.logical_not(first), xdt, 0.0)
                y = y + jnp.dot((cb * decay).astype(MXU_DTYPE), xh.astype(MXU_DTYPE), preferred_element_type=F32)
            ys.append(y + x * expand(d_skip, ha))
    return ys, news


N_PAIRS = SSD_GROUPS * N_PAIR


def _ssd_specs(chunk_of):
    bc = SSD_GROUPS * SSD_STATE
    par = pl.BlockSpec((1, LANE), lambda *ids: (0, 0))
    return [
        pl.BlockSpec((1, CHUNK, D_SSD), lambda *ids: (*chunk_of(*ids), 0)),
        pl.BlockSpec((1, CHUNK, bc), lambda *ids: (*chunk_of(*ids), D_SSD // bc)),
        pl.BlockSpec((1, CHUNK, bc), lambda *ids: (*chunk_of(*ids), D_SSD // bc + 1)),
        pl.BlockSpec((1, CHUNK, LANE), lambda *ids: (*chunk_of(*ids), OFF_DT // LANE)),
        par,
        par,
        par,
    ]


def _pair_cols(ref):
    return [ref[0, :, k * LANE : (k + 1) * LANE] for k in range(N_PAIRS)]


def _group_cols(ref):
    return [ref[0, :, g * SSD_STATE : (g + 1) * SSD_STATE] for g in range(SSD_GROUPS)]


def ssd_fwd(xbc_act, proj, dt_bias, a_log, d_skip):
    B, S, _ = xbc_act.shape
    nc = S // CHUNK

    def body(xs_ref, bm_ref, cm_ref, dt_ref, db_ref, al_ref, ds_ref, y_ref, st_ref, state):
        @pl.when(pl.program_id(1) == 0)
        def _():
            state[...] = jnp.zeros_like(state)

        prev = [state[k] for k in range(N_PAIRS)]
        for k in range(N_PAIRS):
            st_ref[0, 0, k] = prev[k]
        ys, news = _ssd_chunk(prev, _pair_cols(xs_ref), _group_cols(bm_ref), _group_cols(cm_ref), dt_ref[0], db_ref[...], al_ref[...], ds_ref[...])
        for k in range(N_PAIRS):
            y_ref[0, :, k * LANE : (k + 1) * LANE] = ys[k]
            state[k] = news[k]

    return pl.pallas_call(
        body,
        name="ssd_fwd",
        grid=(B, nc),
        in_specs=_ssd_specs(lambda b, c: (b, c)),
        out_specs=[
            pl.BlockSpec((1, CHUNK, D_SSD), lambda b, c: (b, c, 0)),
            pl.BlockSpec((1, 1, N_PAIRS, SSD_STATE, LANE), lambda b, c: (b, c, 0, 0, 0)),
        ],
        out_shape=[
            jax.ShapeDtypeStruct((B, S, D_SSD), F32),
            jax.ShapeDtypeStruct((B, nc, N_PAIRS, SSD_STATE, LANE), F32),
        ],
        scratch_shapes=[pltpu.VMEM((N_PAIRS, SSD_STATE, LANE), F32)],
        compiler_params=_cparams(2),
    )(xbc_act, xbc_act, xbc_act, proj, dt_bias, a_log, d_skip)


def ssd_bwd(xbc_act, proj, dt_bias, a_log, d_skip, states, dy):
    B, S, _ = xbc_act.shape
    nc = S // CHUNK
    bc = SSD_GROUPS * SSD_STATE
    chunk_of = lambda b, cr: (b, nc - 1 - cr)

    def body(xs_ref, bm_ref, cm_ref, dt_ref, db_ref, al_ref, ds_ref, st_ref, dy_ref,
             dxs_ref, dbm_ref, dcm_ref, ddt_ref, ddb_ref, dal_ref, dds_ref, dstate):
        b, cr = pl.program_id(0), pl.program_id(1)

        @pl.when(cr == 0)
        def _():
            dstate[...] = jnp.zeros_like(dstate)

        prev = [st_ref[0, 0, k] for k in range(N_PAIRS)]
        _, vjp = jax.vjp(
            _ssd_chunk, prev, _pair_cols(xs_ref), _group_cols(bm_ref), _group_cols(cm_ref), dt_ref[0], db_ref[...], al_ref[...], ds_ref[...]
        )
        dprev, dxs, dbm, dcm, ddt, ddb, dal, dds = vjp((_pair_cols(dy_ref), [dstate[k] for k in range(N_PAIRS)]))
        for k in range(N_PAIRS):
            dstate[k] = dprev[k]
            dxs_ref[0, :, k * LANE : (k + 1) * LANE] = dxs[k]
        for g in range(SSD_GROUPS):
            dbm_ref[0, :, g * SSD_STATE : (g + 1) * SSD_STATE] = dbm[g]
            dcm_ref[0, :, g * SSD_STATE : (g + 1) * SSD_STATE] = dcm[g]
        ddt_ref[0] = ddt
        first = (b == 0) & (cr == 0)

        @pl.when(first)
        def _():
            ddb_ref[...] = ddb
            dal_ref[...] = dal
            dds_ref[...] = dds

        @pl.when(jnp.logical_not(first))
        def _():
            ddb_ref[...] += ddb
            dal_ref[...] += dal
            dds_ref[...] += dds

    par = pl.BlockSpec((1, LANE), lambda *ids: (0, 0))
    in_specs = _ssd_specs(chunk_of) + [
        pl.BlockSpec((1, 1, N_PAIRS, SSD_STATE, LANE), lambda b, cr: (*chunk_of(b, cr), 0, 0, 0)),
        pl.BlockSpec((1, CHUNK, D_SSD), lambda b, cr: (*chunk_of(b, cr), 0)),
    ]
    return pl.pallas_call(
        body,
        name="ssd_bwd",
        grid=(B, nc),
        in_specs=in_specs,
        out_specs=[
            pl.BlockSpec((1, CHUNK, D_SSD), lambda b, cr: (*chunk_of(b, cr), 0)),
            pl.BlockSpec((1, CHUNK, bc), lambda b, cr: (*chunk_of(b, cr), 0)),
            pl.BlockSpec((1, CHUNK, bc), lambda b, cr: (*chunk_of(b, cr), 0)),
            pl.BlockSpec((1, CHUNK, LANE), lambda b, cr: (*chunk_of(b, cr), 0)),
            par,
            par,
            par,
        ],
        out_shape=[
            jax.ShapeDtypeStruct((B, S, D_SSD), F32),
            jax.ShapeDtypeStruct((B, S, bc), F32),
            jax.ShapeDtypeStruct((B, S, bc), F32),
            jax.ShapeDtypeStruct((B, S, LANE), F32),
            jax.ShapeDtypeStruct((1, LANE), F32),
            jax.ShapeDtypeStruct((1, LANE), F32),
            jax.ShapeDtypeStruct((1, LANE), F32),
        ],
        scratch_shapes=[pltpu.VMEM((N_PAIRS, SSD_STATE, LANE), F32)],
        compiler_params=_cparams(2),
    )(xbc_act, xbc_act, xbc_act, proj, dt_bias, a_log, d_skip, states, dy)


ATT_SCALE = (QK_NOPE + QK_ROPE) ** -0.5
ATT_TILE = 512


ATT_HEADS_FWD = 4
ATT_HEADS_BWD = 2
LOG2E = 1.4426950408889634
Q_PRESCALE = ATT_SCALE * LOG2E


def fn_rope_q(q, cos_t, sin_t, rot):
    parts = []
    for h in range(MLA_HEADS):
        qr = q[:, h * Q_HEAD_PAD + LANE : (h + 1) * Q_HEAD_PAD]
        qr = qr * cos_t + jnp.dot(qr, rot, precision=HIGHEST, preferred_element_type=F32) * sin_t
        parts += [q[:, h * Q_HEAD_PAD : h * Q_HEAD_PAD + LANE] * Q_PRESCALE, qr * Q_PRESCALE]
    return (jnp.concatenate(parts, axis=1),)


def _head_cols(ref, hh, width):
    return ref[0, :, hh * width : (hh + 1) * width]


def _causal_mask(s):
    return jnp.where(lax.broadcasted_iota(jnp.int32, s.shape, 0) >= lax.broadcasted_iota(jnp.int32, s.shape, 1), s, -jnp.inf)


def _nt(a, b):
    return lax.dot_general(a, b, (((1,), (1,)), ((), ())), preferred_element_type=F32)


def _tn(a, b):
    return lax.dot_general(a, b, (((0,), (0,)), ((), ())), preferred_element_type=F32)


def _per_tile(idx, n, variant):
    for v in range(n):

        @pl.when(idx == v)
        def _(v=v):
            variant(v)


def attention_fwd(qs, kv, krr):
    B, S, _ = qs.shape
    t = min(ATT_TILE, S)
    n = S // t
    hps = ATT_HEADS_FWD

    def body(q_ref, k_ref, kr_ref, o_ref, lse_ref):
        def variant(v):
            spans = ([(0, v * t, False)] if v else []) + [(v * t, (v + 1) * t, True)]
            for hh in range(hps):
                q = _head_cols(q_ref, hh, Q_HEAD_PAD)
                c0 = hh * 2 * LANE
                ss = []
                for a, b, diag in spans:
                    s = _nt(q, jnp.concatenate([k_ref[0, a:b, c0 : c0 + LANE], kr_ref[0, a:b]], axis=1))
                    ss.append(_causal_mask(s) if diag else s)
                m = functools.reduce(jnp.maximum, [jnp.max(s, axis=1, keepdims=True) for s in ss])
                ps = [jnp.exp2(s - m) for s in ss]
                l = functools.reduce(jnp.add, [jnp.sum(p, axis=1, keepdims=True) for p in ps])
                acc = functools.reduce(
                    jnp.add,
                    [jnp.dot(p.astype(MXU_DTYPE), k_ref[0, a:b, c0 + LANE : c0 + 2 * LANE], preferred_element_type=F32) for p, (a, b, _) in zip(ps, spans)],
                )
                o_ref[0, :, hh * V_DIM : (hh + 1) * V_DIM] = acc / l
                lse_ref[0, hh] = m + jnp.log2(l)

        _per_tile(pl.program_id(2), n, variant)

    return pl.pallas_call(
        body,
        name="attention_fwd",
        grid=(B, MLA_HEADS // hps, n),
        in_specs=[
            pl.BlockSpec((1, t, hps * Q_HEAD_PAD), lambda b, h, i: (b, i, h)),
            pl.BlockSpec((1, S, hps * 2 * LANE), lambda b, h, i: (b, 0, h)),
            pl.BlockSpec((1, S, LANE), lambda b, h, i: (b, 0, 0)),
        ],
        out_specs=[
            pl.BlockSpec((1, t, hps * V_DIM), lambda b, h, i: (b, i, h)),
            pl.BlockSpec((1, hps, t, 1), lambda b, h, i: (b, h, i, 0)),
        ],
        out_shape=[jax.ShapeDtypeStruct((B, S, D_ATT), F32), jax.ShapeDtypeStruct((B, MLA_HEADS, S, 1), F32)],
        compiler_params=_cparams(3),
    )(qs, kv, krr)


def attention_bwd_dq(qs, kv, krr, cos_t, sin_t, rot_t, o, lse, do):
    B, S, _ = qs.shape
    t = min(ATT_TILE, S)
    n = S // t
    hps = ATT_HEADS_BWD

    def body(q_ref, c_ref, s_ref, rott_ref, k_ref, kr_ref, o_ref, lse_ref, do_ref, dq_ref):
        def variant(v):
            spans = ([(0, v * t, False)] if v else []) + [(v * t, (v + 1) * t, True)]
            for hh in range(hps):
                q = _head_cols(q_ref, hh, Q_HEAD_PAD)
                dov = _head_cols(do_ref, hh, V_DIM)
                delta = jnp.sum(_head_cols(o_ref, hh, V_DIM) * dov, axis=1, keepdims=True)
                dob = dov.astype(MXU_DTYPE)
                c0 = hh * 2 * LANE
                acc = None
                for a, b, diag in spans:
                    k = jnp.concatenate([k_ref[0, a:b, c0 : c0 + LANE], kr_ref[0, a:b]], axis=1)
                    s = _nt(q, k)
                    p = jnp.exp2((_causal_mask(s) if diag else s) - lse_ref[0, hh])
                    ds = p * (_nt(dob, k_ref[0, a:b, c0 + LANE : c0 + 2 * LANE]) - delta) * ATT_SCALE
                    part = jnp.dot(ds.astype(MXU_DTYPE), k, preferred_element_type=F32)
                    acc = part if acc is None else acc + part
                r = acc[:, LANE:]
                dr = r * c_ref[0] + jnp.dot(r * s_ref[0], rott_ref[...], precision=HIGHEST, preferred_element_type=F32)
                dq_ref[0, :, hh * Q_HEAD_PAD : (hh + 1) * Q_HEAD_PAD] = jnp.concatenate([acc[:, :LANE], dr], axis=1).astype(dq_ref.dtype)

        _per_tile(pl.program_id(2), n, variant)

    return pl.pallas_call(
        body,
        name="attention_bwd_dq",
        grid=(B, MLA_HEADS // hps, n),
        in_specs=[
            pl.BlockSpec((1, t, hps * Q_HEAD_PAD), lambda b, h, i: (b, i, h)),
            pl.BlockSpec((1, t, LANE), lambda b, h, i: (b, i, 0)),
            pl.BlockSpec((1, t, LANE), lambda b, h, i: (b, i, 0)),
            pl.BlockSpec((LANE, LANE), lambda b, h, i: (0, 0)),
            pl.BlockSpec((1, S, hps * 2 * LANE), lambda b, h, i: (b, 0, h)),
            pl.BlockSpec((1, S, LANE), lambda b, h, i: (b, 0, 0)),
            pl.BlockSpec((1, t, hps * V_DIM), lambda b, h, i: (b, i, h)),
            pl.BlockSpec((1, hps, t, 1), lambda b, h, i: (b, h, i, 0)),
            pl.BlockSpec((1, t, hps * V_DIM), lambda b, h, i: (b, i, h)),
        ],
        out_specs=pl.BlockSpec((1, t, hps * Q_HEAD_PAD), lambda b, h, i: (b, i, h)),
        out_shape=jax.ShapeDtypeStruct(qs.shape, MXU_DTYPE),
        compiler_params=_cparams(3),
    )(qs, cos_t, sin_t, rot_t, kv, krr, o, lse, do)


def attention_bwd_dkv(qs, kv, krr, o, lse, do):
    B, S, _ = qs.shape
    t = min(ATT_TILE, S)
    n = S // t
    hps = ATT_HEADS_BWD

    def body(q_ref, k_ref, kr_ref, o_ref, lse_ref, do_ref, dkv_ref, dkr_ref):
        h = pl.program_id(2)

        def variant(v):
            spans = [(v * t, (v + 1) * t, True)] + ([((v + 1) * t, S, False)] if v + 1 < n else [])
            dkr = None
            for hh in range(hps):
                c0 = hh * 2 * LANE
                k = jnp.concatenate([k_ref[0, :, c0 : c0 + LANE], kr_ref[0]], axis=1)
                vals = k_ref[0, :, c0 + LANE : c0 + 2 * LANE]
                dk = dv = None
                for a, b, diag in spans:
                    q = q_ref[0, a:b, hh * Q_HEAD_PAD : (hh + 1) * Q_HEAD_PAD]
                    dov = do_ref[0, a:b, hh * V_DIM : (hh + 1) * V_DIM]
                    dob = dov.astype(MXU_DTYPE)
                    s = _nt(q, k)
                    p = jnp.exp2((_causal_mask(s) if diag else s) - lse_ref[0, hh, a:b])
                    delta = jnp.sum(o_ref[0, a:b, hh * V_DIM : (hh + 1) * V_DIM] * dov, axis=1, keepdims=True)
                    ds2 = p * (_nt(dob, vals) - delta) * (1.0 / LOG2E)
                    dv_part, dk_part = _tn(p.astype(MXU_DTYPE), dob), _tn(ds2.astype(MXU_DTYPE), q)
                    dv = dv_part if dv is None else dv + dv_part
                    dk = dk_part if dk is None else dk + dk_part
                dkv_ref[0, :, c0 : c0 + 2 * LANE] = jnp.concatenate([dk[:, :LANE], dv], axis=1).astype(dkv_ref.dtype)
                dkr = dk[:, LANE:] if dkr is None else dkr + dk[:, LANE:]

            @pl.when(h == 0)
            def _():
                dkr_ref[0] = dkr

            @pl.when(h > 0)
            def _():
                dkr_ref[0] += dkr

        _per_tile(pl.program_id(1), n, variant)

    return pl.pallas_call(
        body,
        name="attention_bwd_dkv",
        grid=(B, n, MLA_HEADS // hps),
        in_specs=[
            pl.BlockSpec((1, S, hps * Q_HEAD_PAD), lambda b, j, h: (b, 0, h)),
            pl.BlockSpec((1, t, hps * 2 * LANE), lambda b, j, h: (b, j, h)),
            pl.BlockSpec((1, t, LANE), lambda b, j, h: (b, j, 0)),
            pl.BlockSpec((1, S, hps * V_DIM), lambda b, j, h: (b, 0, h)),
            pl.BlockSpec((1, hps, S, 1), lambda b, j, h: (b, h, 0, 0)),
            pl.BlockSpec((1, S, hps * V_DIM), lambda b, j, h: (b, 0, h)),
        ],
        out_specs=[
            pl.BlockSpec((1, t, hps * 2 * LANE), lambda b, j, h: (b, j, h)),
            pl.BlockSpec((1, t, LANE), lambda b, j, h: (b, j, 0)),
        ],
        out_shape=[jax.ShapeDtypeStruct(kv.shape, MXU_DTYPE), jax.ShapeDtypeStruct(krr.shape, F32)],
        compiler_params=_cparams(3),
    )(qs, kv, krr, o, lse, do)


_IN_SPLITS = np.cumsum([D_SSD, D_XBC, SSD_HEADS, Q_RANK, KV_RANK]).tolist()


def _pad_last(t, n):
    return jnp.pad(t, [(0, 0)] * (t.ndim - 1) + [(0, n - t.shape[-1])])


def win_to_kernel(w):
    z, xbc, dt, cq, ckv, kr = jnp.split(w, _IN_SPLITS, axis=-1)
    return jnp.concatenate([z, xbc, ckv, _pad_last(kr, LANE), _pad_last(dt, LANE), cq], axis=-1)


def win_from_kernel(g):
    z, xbc = g[..., :OFF_XBC], g[..., OFF_XBC:OFF_CKV]
    ckv, kr = g[..., OFF_CKV:OFF_KR], g[..., OFF_KR : OFF_KR + QK_ROPE]
    dt, cq = g[..., OFF_DT : OFF_DT + SSD_HEADS], g[..., OFF_CQ:]
    return jnp.concatenate([z, xbc, dt, cq, ckv, kr], axis=-1)


def wuq_to_kernel(w):
    w = w.reshape(*w.shape[:-1], MLA_HEADS, QK_NOPE + QK_ROPE)
    return _pad_last(w, Q_HEAD_PAD).reshape(*w.shape[:-2], MLA_HEADS * Q_HEAD_PAD)


def wuq_from_kernel(g):
    g = g.reshape(*g.shape[:-1], MLA_HEADS, Q_HEAD_PAD)[..., : QK_NOPE + QK_ROPE]
    return g.reshape(*g.shape[:-2], MLA_HEADS * (QK_NOPE + QK_ROPE))


def glu_to_kernel(w):
    lead = w.shape[:-1]
    w = w.reshape(*lead, 2, D_FF // LANE, LANE)
    return jnp.swapaxes(w, -3, -2).reshape(*lead, 2 * D_FF)


def glu_from_kernel(g):
    lead = g.shape[:-1]
    g = g.reshape(*lead, D_FF // LANE, 2, LANE)
    return jnp.swapaxes(g, -3, -2).reshape(*lead, 2 * D_FF)


def _lane_pad_row(v):
    return _pad_last(v, LANE)[None, :]


def rope_tables(positions):
    inv_freq = jnp.asarray(1.0 / (ROPE_BASE ** (np.arange(0, QK_ROPE, 2, dtype=np.float32) / QK_ROPE)))
    ang = positions.astype(F32)[..., None] * inv_freq
    cos, sin = jnp.cos(ang), jnp.sin(ang)
    zeros = jnp.zeros(cos.shape[:-1] + (LANE - QK_ROPE,), F32)
    rot = np.zeros((LANE, LANE), np.float32)
    half = QK_ROPE // 2
    for j in range(half):
        rot[j + half, j] = -1.0
        rot[j, j + half] = 1.0
    return jnp.concatenate([cos, cos, zeros], -1), jnp.concatenate([sin, sin, zeros], -1), jnp.asarray(rot), jnp.asarray(rot.T)


BIG = ("w_in", "w_uq", "w_ukv", "w_out", "w_up", "w_down")


def local_step(x, target, positions, mod_raw, W):
    B, S, D = x.shape
    cos_t, sin_t, rot, rot_t = rope_tables(positions)
    row = lambda v: v.reshape(1, -1)
    w_in_k, w_uq_k = win_to_kernel(W["w_in"]), wuq_to_kernel(W["w_uq"])
    Ls = []
    for l in range(DEPTH):
        mods = [mod_raw[l, :, k][:, None, :] for k in range(6)]
        bias = [row(W["b_ada"][l, k * D : (k + 1) * D]) for k in range(6)]
        Ls.append(
            dict(
                mods=mods,
                bias=bias,
                w_in=Layer(w_in_k, l),
                w_uq=Layer(w_uq_k, l),
                w_ukv=Layer(W["w_ukv"], l),
                w_out=Layer(W["w_out"], l),
                w_up=Layer(W["w_up"], l),
                w_down=Layer(W["w_down"], l),
                conv_w=W["conv_w"][l],
                conv_b=row(W["conv_b"][l]),
                conv_ff_w=glu_to_kernel(W["conv_ff_w"][l]),
                conv_ff_b=row(glu_to_kernel(W["conv_ff_b"][l])),
                dt_bias=_lane_pad_row(W["dt_bias"][l]),
                a_log=_lane_pad_row(W["a_log"][l]),
                d_skip=_lane_pad_row(W["d_skip"][l]),
                norm_mix=row(W["norm_mix"][l]),
                ssd_norm=row(W["ssd_norm"][l]),
                q_norm=row(W["q_norm"][l]),
                kv_norm=row(W["kv_norm"][l]),
                attn_norm=row(W["attn_norm"][l]),
                norm_mlp=row(W["norm_mlp"][l]),
            )
        )
    fnorm = row(W["final_norm"])

    p0 = Ls[0]
    (h1,) = rowwise_fwd(
        "prenorm_fwd", fn_prenorm, [x], [p0["mods"][1], p0["mods"][0]], [p0["norm_mix"], p0["bias"][1], p0["bias"][0]], [(D, MXU_DTYPE)]
    )
    xin = x
    for l, p in enumerate(Ls):
        s = p["saved"] = dict(xin=xin, h1=h1)
        s["proj"] = proj = matmul(f"w_in_fwd{l}", h1, p["w_in"])
        s["xbc_act"] = xbc_act = conv_silu_fwd(proj, p["conv_w"], p["conv_b"])
        s["yscan"], s["states"] = ssd_fwd(xbc_act, proj, p["dt_bias"], p["a_log"], p["d_skip"])
        mla_in = [Col(proj, Q_RANK, OFF_CQ // Q_RANK), Col(proj, KV_RANK, OFF_CKV // KV_RANK), Col(proj, LANE, OFF_KR // LANE), cos_t, sin_t]
        s["cqn"], s["ckvn"], s["krr"] = rowwise_fwd(
            f"mla_prep_fwd{l}", fn_mla_prep, mla_in, [], [p["q_norm"], p["kv_norm"], rot], [(Q_RANK, MXU_DTYPE), (KV_RANK, MXU_DTYPE), (LANE, MXU_DTYPE)]
        )
        q = matmul(f"w_uq_fwd{l}", s["cqn"], p["w_uq"], out_dtype=MXU_DTYPE)
        (s["qs"],) = rowwise_fwd(f"rope_q_fwd{l}", fn_rope_q, [q, cos_t, sin_t], [], [rot], [(q.shape[-1], MXU_DTYPE)])
        s["kv"] = matmul(f"w_ukv_fwd{l}", s["ckvn"], p["w_ukv"], out_dtype=MXU_DTYPE)
        s["o"], s["lse"] = attention_fwd(s["qs"], s["kv"], s["krr"])
        (s["ycat"],) = rowwise_fwd(
            f"mix_fwd{l}", fn_mix, [s["yscan"], Col(proj, D_SSD, 0), s["o"]], [], [p["ssd_norm"], p["attn_norm"]], [(D_SSD + D_ATT, MXU_DTYPE)]
        )
        s["ymix"] = matmul(f"w_out_fwd{l}", s["ycat"], p["w_out"])
        s["x1"], s["h2"] = rowwise_fwd(
            f"mid_fwd{l}",
            fn_resid_prenorm,
            [xin, s["ymix"]],
            [p["mods"][2], p["mods"][4], p["mods"][3]],
            [p["norm_mlp"], p["bias"][2], p["bias"][4], p["bias"][3]],
            [(D, F32), (D, MXU_DTYPE)],
        )
        s["u"] = matmul(f"w_up_fwd{l}", s["h2"], p["w_up"])
        s["a"] = conv_glu_fwd(s["u"], p["conv_ff_w"], p["conv_ff_b"])
        s["ff"] = matmul(f"w_down_fwd{l}", s["a"], p["w_down"])
        if l + 1 < DEPTH:
            n = Ls[l + 1]
            xin, h1 = rowwise_fwd(
                f"join_fwd{l}",
                fn_resid_prenorm,
                [s["x1"], s["ff"]],
                [p["mods"][5], n["mods"][1], n["mods"][0]],
                [n["norm_mix"], p["bias"][5], n["bias"][1], n["bias"][0]],
                [(D, F32), (D, MXU_DTYPE)],
            )

    G = {k: [None] * DEPTH for k in W if k not in ("final_norm", "w_ada")}
    dmod = [[None] * 6 for _ in range(DEPTH)]
    dbias = [[None] * 6 for _ in range(DEPTH)]
    last = Ls[-1]
    sl = last["saved"]
    loss, dx1, dff, dmod[-1][5], G["final_norm"], dbias[-1][5] = final_fwdbwd(
        sl["x1"], sl["ff"], target, last["mods"][5], fnorm, last["bias"][5]
    )
    grad_x = None
    for l in reversed(range(DEPTH)):
        p = Ls[l]
        s = p["saved"]
        da = matmul(f"w_down_dgrad{l}", dff, p["w_down"], tb=True)
        G["w_down"][l] = matmul(f"w_down_wgrad{l}", s["a"], dff, ta=True)
        du, dcw, dcb = conv_glu_bwd(s["u"], p["conv_ff_w"], p["conv_ff_b"], da)
        G["conv_ff_w"][l], G["conv_ff_b"][l] = glu_from_kernel(dcw), glu_from_kernel(dcb)[0]
        dh2 = matmul(f"w_up_dgrad{l}", du, p["w_up"], tb=True)
        G["w_up"][l] = matmul(f"w_up_wgrad{l}", s["h2"], du, ta=True)
        dxb, dymix, dmod[l][2], dmod[l][4], dmod[l][3], G["norm_mlp"][l], dbias[l][2], dbias[l][4], dbias[l][3] = rowwise_bwd(
            f"mid_bwd{l}",
            fn_resid_prenorm,
            [s["xin"], s["ymix"]],
            [p["mods"][2], p["mods"][4], p["mods"][3]],
            [p["norm_mlp"], p["bias"][2], p["bias"][4], p["bias"][3]],
            [dx1, dh2],
            [True, True],
            [True] * 4,
            mxu_only=(1,),
        )
        dycat = matmul(f"w_out_dgrad{l}", dymix, p["w_out"], tb=True)
        G["w_out"][l] = matmul(f"w_out_wgrad{l}", s["ycat"], dymix, ta=True)
        dyscan, dz, do, G["ssd_norm"][l], G["attn_norm"][l] = rowwise_bwd(
            f"mix_bwd{l}", fn_mix, [s["yscan"], Col(s["proj"], D_SSD, 0), s["o"]], [], [p["ssd_norm"], p["attn_norm"]], [dycat], [True] * 3, [True] * 2, mxu_only=(1,)
        )
        dq = attention_bwd_dq(s["qs"], s["kv"], s["krr"], cos_t, sin_t, rot_t, s["o"], s["lse"], do)
        dkv, dkrr = attention_bwd_dkv(s["qs"], s["kv"], s["krr"], s["o"], s["lse"], do)
        dcqn = matmul(f"w_uq_dgrad{l}", dq, p["w_uq"], tb=True)
        G["w_uq"][l] = wuq_from_kernel(matmul(f"w_uq_wgrad{l}", s["cqn"], dq, ta=True))
        dckvn = matmul(f"w_ukv_dgrad{l}", dkv, p["w_ukv"], tb=True)
        G["w_ukv"][l] = matmul(f"w_ukv_wgrad{l}", s["ckvn"], dkv, ta=True)
        proj = s["proj"]
        mla_in = [Col(proj, Q_RANK, OFF_CQ // Q_RANK), Col(proj, KV_RANK, OFF_CKV // KV_RANK), Col(proj, LANE, OFF_KR // LANE), cos_t, sin_t]
        dcq, dckv, dkr, G["q_norm"][l], G["kv_norm"][l] = rowwise_bwd(
            f"mla_prep_bwd{l}",
            fn_mla_prep,
            mla_in,
            [],
            [p["q_norm"], p["kv_norm"], rot],
            [dcqn, dckvn, dkrr],
            [True, True, True, False, False],
            [True, True, False],
            mxu_only=(0, 1, 2),
        )
        dxs, dbm, dcm, ddt, ddb, dal, dds = ssd_bwd(s["xbc_act"], proj, p["dt_bias"], p["a_log"], p["d_skip"], s["states"], dyscan)
        G["dt_bias"][l], G["a_log"][l], G["d_skip"][l] = ddb[0, :SSD_HEADS], dal[0, :SSD_HEADS], dds[0, :SSD_HEADS]
        dxbc, G["conv_w"][l], dcb = conv_silu_bwd(proj, p["conv_w"], p["conv_b"], [dxs, dbm, dcm])
        G["conv_b"][l] = dcb[0]
        dproj = jnp.concatenate([dz, dxbc, dckv, dkr, ddt.astype(MXU_DTYPE), dcq], axis=-1)
        dh1 = matmul(f"w_in_dgrad{l}", dproj, p["w_in"], tb=True)
        G["w_in"][l] = win_from_kernel(matmul(f"w_in_wgrad{l}", s["h1"], dproj, ta=True))
        if l > 0:
            q = Ls[l - 1]
            sq = q["saved"]
            dx1, dff, dmod[l - 1][5], dmod[l][1], dmod[l][0], G["norm_mix"][l], dbias[l - 1][5], dbias[l][1], dbias[l][0] = rowwise_bwd(
                f"join_bwd{l - 1}",
                fn_resid_prenorm,
                [sq["x1"], sq["ff"]],
                [q["mods"][5], p["mods"][1], p["mods"][0]],
                [p["norm_mix"], q["bias"][5], p["bias"][1], p["bias"][0]],
                [dxb, dh1],
                [True, True],
                [True] * 4,
                mxu_only=(1,),
            )
        else:
            grad_x, dmod[0][1], dmod[0][0], G["norm_mix"][0], dbias[0][1], dbias[0][0] = rowwise_bwd(
                "prenorm_bwd",
                fn_prenorm,
                [x],
                [p["mods"][1], p["mods"][0]],
                [p["norm_mix"], p["bias"][1], p["bias"][0]],
                [dh1],
                [True],
                [True] * 3,
                adds={0: dxb},
            )
    for l in range(DEPTH):
        G["b_ada"][l] = jnp.concatenate([d[0] for d in dbias[l]])
        for k in ("norm_mix", "ssd_norm", "q_norm", "kv_norm", "attn_norm", "norm_mlp"):
            G[k][l] = G[k][l][0]
    grads = {k: (v if k in BIG else jnp.stack(v) if isinstance(v, list) else v[0]) for k, v in G.items()}
    dmod_raw = jnp.stack([jnp.stack([d[:, 0, :] for d in dmod[l]], axis=1) for l in range(DEPTH)])
    return loss, grad_x, grads, dmod_raw


MESH = pl.DeviceIdType.MESH
ANY = pl.BlockSpec(memory_space=pl.ANY)
PACK_W = 1024
PACK_TILE = 256


def _place():
    x, y, c = lax.axis_index("x"), lax.axis_index("y"), lax.axis_index("c")
    chips = [(1 - x, y), (x, 1 - y), (1 - x, 1 - y)]
    return x, y, c, chips


def _remote(src, dst, send_sem, recv_sem, to):
    return pltpu.make_async_remote_copy(src_ref=src, dst_ref=dst, send_sem=send_sem, recv_sem=recv_sem, device_id=to, device_id_type=MESH)


def all_gather8(name, v):
    m_per, n = v.shape

    def body(x_ref, out_ref, send_sems, recv_sems, local_sem):
        x, y, c, chips = _place()
        me, sibling = (x, y, c), (x, y, 1 - c)

        def rows(px, py, pc):
            return out_ref.at[pl.ds((4 * px + 2 * py + pc) * m_per, m_per), :]

        def copy(k, block, to, src=None):
            return _remote(rows(*block) if src is None else src, rows(*block), send_sems.at[k], recv_sems.at[k], to)

        mine = pltpu.make_async_copy(x_ref, rows(*me), local_sem)
        mine.start()
        first = [copy(0, me, sibling, src=x_ref)]
        first += [copy(1 + j, me, (*chip, c), src=x_ref) for j, chip in enumerate(chips)]
        for cp in first:
            cp.start()
        passed = [copy(4 + j, (*chip, c), sibling) for j, chip in enumerate(chips)]
        for j, chip in enumerate(chips):
            copy(1 + j, (*chip, c), me).wait_recv()
            passed[j].start()
        copy(0, sibling, me).wait_recv()
        for j, chip in enumerate(chips):
            copy(4 + j, (*chip, 1 - c), me).wait_recv()
        for cp in first + passed:
            cp.wait_send()
        mine.wait()

    return pl.pallas_call(
        body,
        name=name,
        out_shape=jax.ShapeDtypeStruct((N_DEV * m_per, n), v.dtype),
        in_specs=[pl.BlockSpec(memory_space=pltpu.VMEM)],
        out_specs=pl.BlockSpec(memory_space=pltpu.VMEM),
        scratch_shapes=[pltpu.SemaphoreType.DMA((7,)), pltpu.SemaphoreType.DMA((7,)), pltpu.SemaphoreType.DMA],
        compiler_params=pltpu.CompilerParams(vmem_limit_bytes=VMEM_LIMIT),
    )(v)


def gather_weights(pack):
    R, n = pack.shape
    rh = R // 2

    def body(x_ref, out_ref, send_sems, recv_sems):
        x, y, c, chips = _place()
        me = 2 * x + y

        def half(chip, hc):
            return out_ref.at[chip, pl.ds(hc * rh, rh), :]

        src = x_ref.at[pl.ds(c * rh, rh), :]
        first = [_remote(src, half(me, c), send_sems.at[j], recv_sems.at[j], (px, py, c)) for j, (px, py) in enumerate(chips)]
        for cp in first:
            cp.start()
        passed = []
        for j, (px, py) in enumerate(chips):
            got = half(2 * px + py, c)
            _remote(got, got, send_sems.at[j], recv_sems.at[j], (px, py, c)).wait_recv()
            cp = _remote(got, got, send_sems.at[3 + j], recv_sems.at[3 + j], (x, y, 1 - c))
            cp.start()
            passed.append(cp)
        for j, (px, py) in enumerate(chips):
            got = half(2 * px + py, 1 - c)
            _remote(got, got, send_sems.at[3 + j], recv_sems.at[3 + j], (x, y, 1 - c)).wait_recv()
        for cp in first + passed:
            cp.wait_send()

    return pl.pallas_call(
        body,
        name="gather_weights",
        out_shape=jax.ShapeDtypeStruct((N_CHIPS, R, n), pack.dtype),
        in_specs=[ANY],
        out_specs=ANY,
        scratch_shapes=[pltpu.SemaphoreType.DMA((6,)), pltpu.SemaphoreType.DMA((6,))],
    )(pack)


def swap_halves(g):
    n_slot, R, n = g.shape
    rh = R // 2

    def body(g_ref, got_ref, send_sem, recv_sem):
        x, y, c, _ = _place()
        cp = _remote(g_ref.at[:, pl.ds((1 - c) * rh, rh), :], got_ref, send_sem, recv_sem, (x, y, 1 - c))
        cp.start()
        cp.wait()

    return pl.pallas_call(
        body,
        name="swap_halves",
        out_shape=jax.ShapeDtypeStruct((n_slot, rh, n), g.dtype),
        in_specs=[ANY],
        out_specs=ANY,
        scratch_shapes=[pltpu.SemaphoreType.DMA, pltpu.SemaphoreType.DMA],
    )(g)


def scatter_chips(buf):
    def body(s_ref, got_ref, send_sems, recv_sems):
        x, y, c, chips = _place()
        me = 2 * x + y
        cps = [_remote(s_ref.at[2 * px + py], got_ref.at[me], send_sems.at[j], recv_sems.at[j], (px, py, c)) for j, (px, py) in enumerate(chips)]
        for cp in cps:
            cp.start()
        for j, (px, py) in enumerate(chips):
            got = got_ref.at[2 * px + py]
            _remote(got, got, send_sems.at[j], recv_sems.at[j], (px, py, c)).wait_recv()
        for cp in cps:
            cp.wait_send()

    return pl.pallas_call(
        body,
        name="scatter_chips",
        out_shape=jax.ShapeDtypeStruct(buf.shape, buf.dtype),
        in_specs=[ANY],
        out_specs=ANY,
        scratch_shapes=[pltpu.SemaphoreType.DMA((3,)), pltpu.SemaphoreType.DMA((3,))],
    )(buf)


def swap_with_sibling(h):
    def body(h_ref, got_ref, send_sem, recv_sem):
        x, y, c, _ = _place()
        cp = _remote(h_ref, got_ref, send_sem, recv_sem, (x, y, 1 - c))
        cp.start()
        cp.wait()

    return pl.pallas_call(
        body,
        name="swap_with_sibling",
        out_shape=jax.ShapeDtypeStruct(h.shape, h.dtype),
        in_specs=[ANY],
        out_specs=ANY,
        scratch_shapes=[pltpu.SemaphoreType.DMA, pltpu.SemaphoreType.DMA],
    )(h)


def chip_sum(g, got, core, chip):
    n_slot, R, n = g.shape
    rh = R // 2
    nb = rh // PACK_TILE

    def body(pos, g_ref, got_ref, wire_ref, own_ref):
        k = pl.program_id(1)
        s = g_ref[0] + got_ref[0]
        wire_ref[0] = s.astype(wire_ref.dtype)

        @pl.when(k == pos[1])
        def _():
            own_ref[...] = s

    grid_spec = pltpu.PrefetchScalarGridSpec(
        num_scalar_prefetch=1,
        grid=(nb, n_slot),
        in_specs=[
            pl.BlockSpec((1, PACK_TILE, n), lambda i, k, pos: (k, pos[0] * nb + i, 0)),
            pl.BlockSpec((1, PACK_TILE, n), lambda i, k, pos: (k, i, 0)),
        ],
        out_specs=[
            pl.BlockSpec((1, PACK_TILE, n), lambda i, k, pos: (k, i, 0)),
            pl.BlockSpec((PACK_TILE, n), lambda i, k, pos: (i, 0)),
        ],
    )
    return pl.pallas_call(
        body,
        name="chip_sum",
        grid_spec=grid_spec,
        out_shape=[jax.ShapeDtypeStruct((n_slot, rh, n), WIRE_DTYPE), jax.ShapeDtypeStruct((rh, n), F32)],
        compiler_params=_cparams(2),
    )(jnp.stack([core, chip]).astype(jnp.int32), g, got)


def mesh_sum(own, got, chip):
    rh, n = own.shape
    n_slot = got.shape[0]

    def body(pos, own_ref, *refs):
        out_ref = refs[-1]
        acc = own_ref[...]
        for k in range(n_slot):
            acc = acc + jnp.where(k != pos[0], refs[k][0].astype(F32), 0.0)
        out_ref[...] = acc

    grid_spec = pltpu.PrefetchScalarGridSpec(
        num_scalar_prefetch=1,
        grid=(rh // PACK_TILE,),
        in_specs=[pl.BlockSpec((PACK_TILE, n), lambda i, pos: (i, 0))]
        + [pl.BlockSpec((1, PACK_TILE, n), lambda i, pos, k=k: (k, i, 0)) for k in range(n_slot)],
        out_specs=pl.BlockSpec((PACK_TILE, n), lambda i, pos: (i, 0)),
    )
    return pl.pallas_call(
        body, name="mesh_sum", grid_spec=grid_spec, out_shape=jax.ShapeDtypeStruct((rh, n), F32), compiler_params=_cparams(1)
    )(jnp.stack([chip]).astype(jnp.int32), own, *([got] * n_slot))


def sum_devices(name, v):
    def body(v_ref, o_ref):
        acc = v_ref[0]
        for d in range(1, N_DEV):
            acc = acc + v_ref[d]
        o_ref[...] = acc

    return pl.pallas_call(body, name=name, out_shape=jax.ShapeDtypeStruct(v.shape[1:], F32))(v)


def adamw(name, w, g, m, v):
    shape = w.shape
    w2, g2, m2, v2 = (t.reshape(-1, shape[-1]) for t in (w, g, m, v))
    rows, n = w2.shape
    tr = _pick(rows, (256, 128, 64, 32, 16, 8))
    c1 = 1.0 / (1.0 - ADAM_B1**ADAM_STEP)
    c2 = 1.0 / (1.0 - ADAM_B2**ADAM_STEP)

    def body(w_ref, g_ref, m_ref, v_ref, d_ref, nm_ref, nv_ref):
        gv = g_ref[...]
        nm = ADAM_B1 * m_ref[...] + (1.0 - ADAM_B1) * gv
        nv = ADAM_B2 * v_ref[...] + (1.0 - ADAM_B2) * jnp.square(gv)
        d_ref[...] = -ADAM_LR * ((nm * c1) / (jnp.sqrt(nv * c2) + ADAM_EPS) + ADAM_WD * w_ref[...])
        nm_ref[...] = nm
        nv_ref[...] = nv

    spec = pl.BlockSpec((tr, n), lambda i: (i, 0))
    outs = pl.pallas_call(
        body,
        name=name,
        grid=(rows // tr,),
        in_specs=[spec] * 4,
        out_specs=[spec] * 3,
        out_shape=[jax.ShapeDtypeStruct((rows, n), F32)] * 3,
        compiler_params=_cparams(1),
    )(w2, g2, m2, v2)
    return [o.reshape(shape) for o in outs]


def ada_fwd(c_all, w_ada):
    n_tok, d = c_all.shape
    depth, _, cols = w_ada.shape
    tn = _pick(cols, (512, 384, 256, 128))

    def body(c_ref, w_ref, o_ref):
        o_ref[0] = jnp.dot(_silu(c_ref[...]).astype(MXU_DTYPE), w_ref[0].astype(MXU_DTYPE), preferred_element_type=F32)

    return pl.pallas_call(
        body,
        name="ada_fwd",
        grid=(depth, cols // tn),
        in_specs=[pl.BlockSpec((n_tok, d), lambda l, j: (0, 0)), pl.BlockSpec((1, d, tn), lambda l, j: (l, 0, j))],
        out_specs=pl.BlockSpec((1, n_tok, tn), lambda l, j: (l, 0, j)),
        out_shape=jax.ShapeDtypeStruct((depth, n_tok, cols), F32),
        compiler_params=_cparams(2),
    )(c_all, w_ada)


COL_SHARDED = ("w_in", "w_uq", "w_ukv", "w_up")


def _pack_rows(parts, rows):
    parts = [p.reshape(-1, PACK_W) for p in parts]
    fill = jnp.zeros((rows - sum(p.shape[0] for p in parts), PACK_W), parts[0].dtype)
    return jnp.concatenate(parts + [fill], axis=0)


def _big_rows(shards):
    n = sum(int(np.prod(shards[k].shape)) for k in BIG) // PACK_W
    return -(-n // (2 * PACK_TILE)) * 2 * PACK_TILE


def unpack_gathered(gathered, shard_shapes):
    out, r0 = {}, 0
    for k in BIG:
        shp = shard_shapes[k]
        n = int(np.prod(shp)) // PACK_W
        seg = gathered[:, r0 : r0 + n].reshape(N_CHIPS, *shp)
        r0 += n
        if k == "w_up":
            seg = seg.reshape(2, 2, shp[0], shp[1], shp[2] // LANE, LANE)
            out[k] = seg.transpose(2, 3, 1, 4, 0, 5).reshape(shp[0], shp[1], N_CHIPS * shp[2])
        elif k in COL_SHARDED:
            out[k] = seg.transpose(1, 2, 0, 3).reshape(shp[0], shp[1], N_CHIPS * shp[2])
        else:
            out[k] = seg.transpose(1, 0, 2, 3).reshape(shp[0], N_CHIPS * shp[1], shp[2])
    return out


def pack_full_grads(grads, shard_shapes, rows):
    parts = []
    for k in BIG:
        _, rows_k, cols_k = shard_shapes[k]
        layers = grads[k]
        if (rows_k * cols_k // PACK_W) % SUBLANES:
            layers = [jnp.concatenate(layers, axis=0)]
        for g in layers:
            if k == "w_up":
                g = g.reshape(g.shape[0], 2, cols_k // LANE, 2, LANE).transpose(3, 1, 0, 2, 4)
            elif k in COL_SHARDED:
                g = g.reshape(g.shape[0], N_CHIPS, cols_k).transpose(1, 0, 2)
            else:
                g = g.reshape(-1, N_CHIPS, rows_k, cols_k).transpose(1, 0, 2, 3)
            parts.append(g.reshape(N_CHIPS, -1, PACK_W))
    fill = jnp.zeros((N_CHIPS, rows - sum(p.shape[1] for p in parts), PACK_W), F32)
    return jnp.concatenate(parts + [fill], axis=1)


def unpack_shards(buf, shard_shapes):
    out, r0 = {}, 0
    for k in BIG:
        n = int(np.prod(shard_shapes[k])) // PACK_W
        out[k] = buf[r0 : r0 + n].reshape(shard_shapes[k])
        r0 += n
    return out


def _flat_pack(arrs, row_multiple=8):
    flat = jnp.concatenate([a.reshape(-1) for a in arrs])
    per = PACK_W * row_multiple
    n = -(-flat.shape[0] // per) * per
    return jnp.pad(flat, (0, n - flat.shape[0])).reshape(-1, PACK_W)


def _flat_unpack(buf, shapes):
    flat, out, o = buf.reshape(-1), [], 0
    for s in shapes:
        n = int(np.prod(s))
        out.append(flat[o : o + n].reshape(s))
        o += n
    return out


WEIGHTS = (
    "w_ada", "b_ada", "norm_mix", "w_in", "conv_w", "conv_b", "dt_bias", "a_log", "d_skip", "ssd_norm", "q_norm", "w_uq",
    "kv_norm", "w_ukv", "attn_norm", "w_out", "norm_mlp", "w_up", "conv_ff_w", "conv_ff_b", "w_down", "final_norm",
)
REPLICATED = ("b_ada", "norm_mix", "conv_b", "dt_bias", "a_log", "d_skip", "ssd_norm", "q_norm", "kv_norm", "attn_norm",
              "norm_mlp", "conv_ff_b", "final_norm")
CONV_SHARDED = ("conv_w", "conv_ff_w")


def kernel(x, c, positions, w_ada, b_ada, norm_mix, w_in, conv_w, conv_b, dt_bias, a_log, d_skip, ssd_norm, q_norm, w_uq, kv_norm, w_ukv, attn_norm, w_out, norm_mlp, w_up, conv_ff_w, conv_ff_b, w_down, final_norm, loss_target, m_w_ada, m_b_ada, m_norm_mix, m_w_in, m_conv_w, m_conv_b, m_dt_bias, m_a_log, m_d_skip, m_ssd_norm, m_q_norm, m_w_uq, m_kv_norm, m_w_ukv, m_attn_norm, m_w_out, m_norm_mlp, m_w_up, m_conv_ff_w, m_conv_ff_b, m_w_down, m_final_norm, v_w_ada, v_b_ada, v_norm_mix, v_w_in, v_conv_w, v_conv_b, v_dt_bias, v_a_log, v_d_skip, v_ssd_norm, v_q_norm, v_w_uq, v_kv_norm, v_w_ukv, v_attn_norm, v_w_out, v_norm_mlp, v_w_up, v_conv_ff_w, v_conv_ff_b, v_w_down, v_final_norm):
    loc = locals()
    Wl = {k: loc[k] for k in WEIGHTS}
    Ml = {k: loc["m_" + k] for k in WEIGHTS}
    Vl = {k: loc["v_" + k] for k in WEIGHTS}
    xi, yi, ci = lax.axis_index("x"), lax.axis_index("y"), lax.axis_index("c")
    chip = 2 * xi + yi
    dev = 2 * chip + ci
    B, S, D = x.shape
    n_tok = N_DEV * B

    c_all = all_gather8("gather_c", c.reshape(8, -1)).reshape(n_tok, D)
    mod_cols = ada_fwd(c_all, w_ada)
    cols = mod_cols.shape[-1]
    half = n_tok // 2
    mod_mine = lax.dynamic_slice_in_dim(mod_cols, ci * half, half, axis=1)
    small_in = _flat_pack([mod_mine, conv_w, conv_ff_w])
    n_mod = mod_mine.size // PACK_W
    small_all = all_gather8("gather_mod", small_in).reshape(N_CHIPS, 2, -1, PACK_W)
    mod_all = small_all[:, :, :n_mod].reshape(N_CHIPS, 2, DEPTH, half, cols).transpose(2, 1, 3, 0, 4).reshape(DEPTH, n_tok, N_CHIPS * cols)
    mod_raw = lax.dynamic_slice_in_dim(mod_all, dev * B, B, axis=1).reshape(DEPTH, B, 6, D)
    conv_parts = [_flat_unpack(small_all[k, 0, n_mod:], [conv_w.shape, conv_ff_w.shape]) for k in range(N_CHIPS)]
    conv_full = {name: jnp.concatenate([conv_parts[k][i] for k in range(N_CHIPS)], axis=-1) for i, name in enumerate(CONV_SHARDED)}

    shard_shapes = {k: Wl[k].shape for k in BIG}
    rows = _big_rows(Wl)
    pack = _pack_rows([Wl[k].astype(MXU_DTYPE) for k in BIG], rows)
    gathered = lax.dynamic_update_slice_in_dim(gather_weights(pack), pack[None], chip, axis=0)
    W = unpack_gathered(gathered, shard_shapes)
    W.update({k: Wl[k] for k in REPLICATED})
    W.update(conv_full)

    loss_lanes, grad_x, grads, dmod_raw = local_step(x, loss_target, positions, mod_raw, W)
    loss = lax.psum(loss_lanes[0, 0], ("x", "y", "c"))

    gpack = pack_full_grads(grads, shard_shapes, rows)
    wire, own = chip_sum(gpack, swap_halves(gpack), ci, chip)
    mine = mesh_sum(own, scatter_chips(wire), chip)
    theirs = swap_with_sibling(mine)
    both = jnp.concatenate([jnp.where(ci == 0, mine, theirs), jnp.where(ci == 0, theirs, mine)], axis=0)
    g_shard = unpack_shards(both, shard_shapes)

    small_names = REPLICATED + CONV_SHARDED
    small_out = _flat_pack([grads[k] for k in small_names] + [dmod_raw])
    small_got = all_gather8("gather_small", small_out).reshape(N_DEV, -1, PACK_W)
    small_sum = _flat_unpack(sum_devices("sum_small", small_got), [grads[k].shape for k in small_names])
    G = dict(zip(small_names, small_sum))
    for name in CONV_SHARDED:
        width = Wl[name].shape[-1]
        G[name] = lax.dynamic_slice_in_dim(G[name], chip * width, width, axis=-1)
    n_small = sum(grads[k].size for k in small_names)
    dmod_all = small_got.reshape(N_DEV, -1)[:, n_small : n_small + dmod_raw.size].reshape(N_DEV, DEPTH, B, 6 * D)
    dmod_all = dmod_all.transpose(1, 0, 2, 3).reshape(DEPTH, n_tok, 6 * D)
    dmod_cols = lax.dynamic_slice_in_dim(dmod_all, chip * cols, cols, axis=-1)
    G["w_ada"] = jnp.stack([matmul(f"w_ada_wgrad{l}", c_all, dmod_cols[l], ta=True, a_act=_silu) for l in range(DEPTH)])
    G.update(g_shard)

    deltas, new_m, new_v = {}, {}, {}
    small_upd = adamw("adamw_small", *[_flat_pack([t[k] for k in REPLICATED]) for t in (Wl, G, Ml, Vl)])
    for res, t in zip((deltas, new_m, new_v), small_upd):
        res.update(zip(REPLICATED, _flat_unpack(t, [Wl[k].shape for k in REPLICATED])))
    for k in WEIGHTS:
        if k not in REPLICATED:
            deltas[k], new_m[k], new_v[k] = adamw("adamw_" + k, Wl[k], G[k], Ml[k], Vl[k])
    return (loss, grad_x, *[G[k] for k in WEIGHTS], *[deltas[k] for k in WEIGHTS], *[new_m[k] for k in WEIGHTS], *[new_v[k] for k in WEIGHTS])
```

```python
import functools
from typing import NamedTuple

import numpy as np
import jax
import jax.numpy as jnp
from jax import lax
from jax.experimental import pallas as pl
from jax.experimental.pallas import tpu as pltpu

F32 = jnp.float32
BF16 = jnp.bfloat16
MXU_DTYPE = jnp.bfloat16
WIRE_DTYPE = jnp.bfloat16
HIGHEST = lax.Precision.HIGHEST

D_MODEL = 1024
DEPTH = 2
D_SSD = 1024
SSD_HEADS = 16
SSD_HEAD_DIM = 64
SSD_GROUPS = 2
SSD_STATE = 128
SSD_CONV = 4
CHUNK = 128
MLA_HEADS = 8
QK_NOPE = 128
QK_ROPE = 64
V_DIM = 128
D_ATT = MLA_HEADS * V_DIM
Q_RANK = 384
KV_RANK = 256
ROPE_BASE = 10000.0
D_FF = 2816
FF_CONV = 3
EPS = 1e-6
D_XBC = D_SSD + 2 * SSD_GROUPS * SSD_STATE
D_IN = D_SSD + D_XBC + SSD_HEADS + Q_RANK + KV_RANK + QK_ROPE
ADAM_LR, ADAM_B1, ADAM_B2, ADAM_EPS, ADAM_WD, ADAM_STEP = 0.001, 0.9, 0.999, 1e-08, 0.01, 10

LANE = 128
N_CHIPS = 4
N_DEV = 8

OFF_Z, OFF_XBC, OFF_CKV, OFF_KR, OFF_DT, OFF_CQ = 0, 1024, 2560, 2816, 2944, 3072
N_PROJ = 3456
Q_HEAD_PAD = 256
VMEM_LIMIT = 56 * 1024 * 1024


def _cparams(n_axes):
    return pltpu.CompilerParams(dimension_semantics=("arbitrary",) * n_axes, vmem_limit_bytes=VMEM_LIMIT)


def _pick(n, prefs):
    for p in prefs:
        if n % p == 0:
            return p
    return n


def _silu(x):
    return x * jax.nn.sigmoid(x)


def _rms(x, g):
    return x * lax.rsqrt(jnp.mean(x * x, axis=-1, keepdims=True) + EPS) * g


def _softplus(x):
    return jnp.maximum(x, 0.0) + jnp.log(1.0 + jnp.exp(-jnp.abs(x)))


class Col(NamedTuple):
    arr: jax.Array
    width: int
    blk: int


def _as_col(a):
    return a if isinstance(a, Col) else Col(a, a.shape[-1], 0)


def _row_specs(tiled, pbatch, shared, tile):
    specs = [pl.BlockSpec((1, tile, t.width), lambda b, i, blk=t.blk: (b, i, blk)) for t in tiled]
    specs += [pl.BlockSpec((1, 1, p.shape[-1]), lambda b, i: (b, 0, 0)) for p in pbatch]
    specs += [pl.BlockSpec(s.shape, lambda b, i: (0, 0)) for s in shared]
    return specs


def _row_vals(refs, nt, npb):
    return [r[0].astype(F32) for r in refs[: nt + npb]] + [r[...].astype(F32) for r in refs[nt + npb :]]


def rowwise_fwd(name, fn, tiled, pbatch, shared, outs, tile=512):
    tiled = [_as_col(t) for t in tiled]
    B, S = tiled[0].arr.shape[:2]
    tile = min(tile, S)
    nt, npb, nsh = len(tiled), len(pbatch), len(shared)
    n_in = nt + npb + nsh

    def body(*refs):
        res = fn(*_row_vals(refs[:n_in], nt, npb))
        for r, v in zip(refs[n_in:], res):
            r[0] = v.astype(r.dtype)

    return pl.pallas_call(
        body,
        name=name,
        grid=(B, S // tile),
        in_specs=_row_specs(tiled, pbatch, shared, tile),
        out_specs=[pl.BlockSpec((1, tile, w), lambda b, i: (b, i, 0)) for w, _ in outs],
        out_shape=[jax.ShapeDtypeStruct((B, S, w), dt) for w, dt in outs],
        compiler_params=_cparams(2),
    )(*[t.arr for t in tiled], *pbatch, *shared)


def rowwise_bwd(name, fn, tiled, pbatch, shared, cts, grad_tiled, grad_shared, adds=None, tile=256, mxu_only=()):
    tiled = [_as_col(t) for t in tiled]
    adds = adds or {}
    B, S = tiled[0].arr.shape[:2]
    tile = min(tile, S)
    nt, npb, nsh = len(tiled), len(pbatch), len(shared)
    n_in = nt + npb + nsh
    add_idx = sorted(adds)
    gt = [i for i in range(nt) if grad_tiled[i]]
    gs = [i for i in range(nsh) if grad_shared[i]]
    diff = gt + [nt + i for i in range(npb)] + [nt + npb + i for i in gs]
    n_ct, n_add = len(cts), len(add_idx)

    def body(*refs):
        vals = _row_vals(refs[:n_in], nt, npb)
        ct_v = tuple(r[0].astype(F32) for r in refs[n_in : n_in + n_ct])
        add_v = {i: r[0].astype(F32) for i, r in zip(add_idx, refs[n_in + n_ct : n_in + n_ct + n_add])}
        out_refs = refs[n_in + n_ct + n_add :]

        def f(*dargs):
            full = list(vals)
            for k, i in enumerate(diff):
                full[i] = dargs[k]
            return tuple(fn(*full))

        _, vjp = jax.vjp(f, *[vals[i] for i in diff])
        grads = vjp(ct_v)
        b, i = pl.program_id(0), pl.program_id(1)
        k = 0
        for idx in gt:
            g = grads[k]
            if idx in add_v:
                g = g + add_v[idx]
            out_refs[k][0] = g.astype(out_refs[k].dtype)
            k += 1
        for _ in range(npb):
            r, g = out_refs[k], grads[k]

            @pl.when(i == 0)
            def _(r=r, g=g):
                r[0] = g

            @pl.when(i > 0)
            def _(r=r, g=g):
                r[0] += g

            k += 1
        for _ in gs:
            r, g = out_refs[k], grads[k]

            @pl.when((i == 0) & (b == 0))
            def _(r=r, g=g):
                r[...] = g

            @pl.when((i > 0) | (b > 0))
            def _(r=r, g=g):
                r[...] += g

            k += 1

    out_specs = [pl.BlockSpec((1, tile, tiled[i].width), lambda b, i: (b, i, 0)) for i in gt]
    out_shape = [jax.ShapeDtypeStruct((B, S, tiled[i].width), MXU_DTYPE if i in mxu_only else F32) for i in gt]
    out_specs += [pl.BlockSpec((1, 1, p.shape[-1]), lambda b, i: (b, 0, 0)) for p in pbatch]
    out_shape += [jax.ShapeDtypeStruct(p.shape, F32) for p in pbatch]
    out_specs += [pl.BlockSpec(shared[i].shape, lambda b, i: (0, 0)) for i in gs]
    out_shape += [jax.ShapeDtypeStruct(shared[i].shape, F32) for i in gs]
    in_specs = _row_specs(tiled, pbatch, shared, tile)
    in_specs += [pl.BlockSpec((1, tile, c.shape[-1]), lambda b, i: (b, i, 0)) for c in cts]
    in_specs += [pl.BlockSpec((1, tile, adds[i].shape[-1]), lambda b, i: (b, i, 0)) for i in add_idx]
    return pl.pallas_call(
        body,
        name=name,
        grid=(B, S // tile),
        in_specs=in_specs,
        out_specs=out_specs,
        out_shape=out_shape,
        compiler_params=_cparams(2),
    )(*[t.arr for t in tiled], *pbatch, *shared, *cts, *[adds[i] for i in add_idx])


def fn_prenorm(x, sc, sh, g, bsc, bsh):
    return (_rms(x, g) * (1.0 + sc + bsc) + (sh + bsh),)


def fn_resid_prenorm(x, y, gate, sc, sh, g, bgate, bsc, bsh):
    x1 = x + (gate + bgate) * y
    return x1, _rms(x1, g) * (1.0 + sc + bsc) + (sh + bsh)


def fn_mix(yscan, z, o, ssd_norm, attn_norm):
    y = yscan * _silu(z)
    half = D_SSD // SSD_GROUPS
    first = lax.broadcasted_iota(jnp.int32, y.shape, 1) < half
    sq = y * y
    m0 = jnp.sum(jnp.where(first, sq, 0.0), axis=-1, keepdims=True) / half
    m1 = jnp.sum(jnp.where(first, 0.0, sq), axis=-1, keepdims=True) / half
    r = jnp.where(first, lax.rsqrt(m0 + EPS), lax.rsqrt(m1 + EPS))
    return (jnp.concatenate([y * r * ssd_norm, _rms(o, attn_norm)], axis=-1),)


def fn_mla_prep(cq, ckv, kr, cos_t, sin_t, q_norm, kv_norm, rot):
    krr = kr * cos_t + jnp.dot(kr, rot, precision=HIGHEST, preferred_element_type=F32) * sin_t
    return _rms(cq, q_norm), _rms(ckv, kv_norm), krr


def final_fwdbwd(x1, ff, target, gate, fnorm, bgate, tile=256):
    B, S, D = x1.shape
    tile = min(tile, S)

    def body(x_ref, f_ref, t_ref, g_ref, n_ref, bg_ref, loss_ref, dx_ref, df_ref, dg_ref, dn_ref, dbg_ref):
        b, i = pl.program_id(0), pl.program_id(1)
        tgt = t_ref[0]

        def f(x, y, gate, fn, bg):
            yf = _rms(x + (gate + bg) * y, fn)
            return 0.5 * jnp.sum(jnp.mean(jnp.square(yf - tgt), axis=-1, keepdims=True), axis=0, keepdims=True)

        val, vjp = jax.vjp(f, x_ref[0], f_ref[0], g_ref[0], n_ref[...], bg_ref[...])
        dx, dff, dg, dn, dbg = vjp(jnp.ones((1, 1), F32))
        dx_ref[0] = dx
        df_ref[0] = dff.astype(df_ref.dtype)
        lane_loss = jnp.broadcast_to(val, (1, LANE))

        @pl.when(i == 0)
        def _():
            dg_ref[0] = dg

        @pl.when(i > 0)
        def _():
            dg_ref[0] += dg

        @pl.when((i == 0) & (b == 0))
        def _():
            dn_ref[...] = dn
            dbg_ref[...] = dbg
            loss_ref[...] = lane_loss

        @pl.when((i > 0) | (b > 0))
        def _():
            dn_ref[...] += dn
            dbg_ref[...] += dbg
            loss_ref[...] += lane_loss

    tok = pl.BlockSpec((1, tile, D), lambda b, i: (b, i, 0))
    pb = pl.BlockSpec((1, 1, D), lambda b, i: (b, 0, 0))
    sh = pl.BlockSpec((1, D), lambda b, i: (0, 0))
    return pl.pallas_call(
        body,
        name="final_loss",
        grid=(B, S // tile),
        in_specs=[tok, tok, tok, pb, sh, sh],
        out_specs=[pl.BlockSpec((1, LANE), lambda b, i: (0, 0)), tok, tok, pb, sh, sh],
        out_shape=[
            jax.ShapeDtypeStruct((1, LANE), F32),
            jax.ShapeDtypeStruct((B, S, D), F32),
            jax.ShapeDtypeStruct((B, S, D), MXU_DTYPE),
            jax.ShapeDtypeStruct((B, 1, D), F32),
            jax.ShapeDtypeStruct((1, D), F32),
            jax.ShapeDtypeStruct((1, D), F32),
        ],
        compiler_params=_cparams(2),
    )(x1, ff, target, gate, fnorm, bgate)


MATMUL_VMEM_BUDGET = 40 * 1024 * 1024
STEP_COST_BYTES = 1.2e6
MXU_DIM = 256
MXU_FLOPS_PER_BYTE = 280.0


def _tile_options(n, cap):
    opts = [d for d in range(LANE, min(n, cap) + 1, LANE) if n % d == 0]
    return opts or [n]


def _matmul_tiles(M, N, K, sa, sb, so):
    best = None
    for tk in [K] + [d for d in _tile_options(K, 2048) if d >= 512 and d < K]:
        nk = K // tk
        for tm in _tile_options(M, 2048):
            for tn in _tile_options(N, 1408):
                acc = 0 if (so == 4 or nk == 1) else tm * tn * 4
                vmem = 2 * (tm * tk * sa + tk * tn * sb + tm * tn * so) + acc + tm * tn * 4 + 2 * (tm * tk + tk * tn)
                if vmem > MATMUL_VMEM_BUDGET:
                    continue
                a_reads = M * K * sa * (1 if nk == 1 else N // tn)
                b_reads = K * N * sb * (1 if (nk == 1 and N == tn) else M // tm)
                steps = (M // tm) * (N // tn) * nk
                hbm = a_reads + b_reads + M * N * so + (2 * M * N * 4 * (nk - 1) if nk > 1 else 0) / 8
                fill = (-(-tn // MXU_DIM) * MXU_DIM / tn) * (-(-tk // MXU_DIM) * MXU_DIM / tk)
                mxu = 2.0 * M * N * K * fill / MXU_FLOPS_PER_BYTE
                cost = max(hbm, mxu) + steps * STEP_COST_BYTES
                if best is None or cost < best[0]:
                    best = (cost, tm, tn, tk)
    assert best is not None, (M, N, K)
    return best[1:]


class Layer(NamedTuple):
    arr: jax.Array
    l: int


def matmul(name, a, b, ta=False, tb=False, out_dtype=F32, a_act=None):
    pieces = list(a) if isinstance(a, (list, tuple)) else [a]
    assert len(pieces) == 1 or not ta
    lead = pieces[0].shape[:2] if (pieces[0].ndim == 3 and not ta) else None
    pieces = [p.reshape(-1, p.shape[-1]) if p.ndim == 3 else p for p in pieces]
    layer = None
    if isinstance(b, Layer):
        b, layer = b
        b_rows, b_cols = b.shape[1:]
    else:
        if b.ndim == 3:
            b = b.reshape(-1, b.shape[-1])
        b_rows, b_cols = b.shape
    Kp, M = pieces[0].shape if ta else pieces[0].shape[::-1]
    N = b_rows if tb else b_cols
    assert (b_cols if tb else b_rows) == Kp * len(pieces), (name, pieces[0].shape, b.shape)
    tm, tn, tk = _matmul_tiles(M, N, Kp, pieces[0].dtype.itemsize, b.dtype.itemsize, jnp.dtype(out_dtype).itemsize)
    npk = Kp // tk
    nk = npk * len(pieces)
    dims = (((0 if ta else 1,), (1 if tb else 0,)), ((), ()))
    direct = jnp.dtype(out_dtype) == jnp.dtype(F32)

    def body(*refs):
        a_refs, b_ref, o_ref, scratch = refs[: len(pieces)], refs[len(pieces)], refs[len(pieces) + 1], refs[len(pieces) + 2 :]
        k = pl.program_id(2)

        def product(a_ref):
            av = a_ref[...]
            if a_act is not None:
                av = a_act(av.astype(F32))
            return lax.dot_general(av.astype(MXU_DTYPE), b_ref[...].astype(MXU_DTYPE), dims, preferred_element_type=F32)

        if nk == 1:
            o_ref[...] = product(a_refs[0]).astype(o_ref.dtype)
            return
        acc = o_ref if direct else scratch[0]

        @pl.when(k == 0)
        def _():
            acc[...] = product(a_refs[0])

        for p, a_ref in enumerate(a_refs):
            lo, hi = max(p * npk, 1), (p + 1) * npk
            if lo < hi:

                @pl.when((k >= lo) & (k < hi))
                def _(a_ref=a_ref):
                    acc[...] += product(a_ref)

        if not direct:

            @pl.when(k == nk - 1)
            def _():
                o_ref[...] = acc[...].astype(o_ref.dtype)

    def a_spec(p):
        kk = lambda k: jnp.clip(k - p * npk, 0, npk - 1)
        return pl.BlockSpec((tk, tm), lambda i, j, k: (kk(k), i)) if ta else pl.BlockSpec((tm, tk), lambda i, j, k: (i, kk(k)))

    b_block, b_index = ((tn, tk), lambda i, j, k: (j, k)) if tb else ((tk, tn), lambda i, j, k: (k, j))
    if layer is None:
        b_spec = pl.BlockSpec(b_block, b_index)
    else:
        b_spec = pl.BlockSpec((None, *b_block), lambda i, j, k: (layer, *b_index(i, j, k)))
    out = pl.pallas_call(
        body,
        name=name,
        grid=(M // tm, N // tn, nk),
        in_specs=[a_spec(p) for p in range(len(pieces))] + [b_spec],
        out_specs=pl.BlockSpec((tm, tn), lambda i, j, k: (i, j)),
        out_shape=jax.ShapeDtypeStruct((M, N), out_dtype),
        scratch_shapes=[] if (direct or nk == 1) else [pltpu.VMEM((tm, tn), F32)],
        compiler_params=_cparams(3),
    )(*pieces, b)
    return out.reshape(*lead, N) if lead is not None else out


SUBLANES = 8


def _shift_down(u, d):
    if d == 0:
        return u
    r = pltpu.roll(u, d, 0)
    t = lax.broadcasted_iota(jnp.int32, (SUBLANES, u.shape[1]), 0)
    return jnp.concatenate([jnp.where(t >= d, r[:SUBLANES], 0.0), r[SUBLANES:]], axis=0)


def _shift_up(u, d):
    if d == 0:
        return u
    s = u.shape[0]
    r = pltpu.roll(u, s - d, 0)
    t = lax.broadcasted_iota(jnp.int32, (SUBLANES, u.shape[1]), 0)
    return jnp.concatenate([r[: s - SUBLANES], jnp.where(t < SUBLANES - d, r[s - SUBLANES :], 0.0)], axis=0)


def _conv(u, w, b):
    k = w.shape[0]
    out = b + w[k - 1 : k, :] * u
    for j in range(k - 1):
        out = out + w[j : j + 1, :] * _shift_down(u, k - 1 - j)
    return out


def _conv_bwd(u, w, dc):
    k = w.shape[0]
    du = w[k - 1 : k, :] * dc
    dws = []
    for j in range(k - 1):
        du = du + w[j : j + 1, :] * _shift_up(dc, k - 1 - j)
        dws.append(jnp.sum(dc * _shift_down(u, k - 1 - j), axis=0, keepdims=True))
    dws.append(jnp.sum(dc * u, axis=0, keepdims=True))
    return du, jnp.concatenate(dws, axis=0), jnp.sum(dc, axis=0, keepdims=True)


def conv_silu_fwd(proj, w, b):
    B, S, _ = proj.shape
    k, c = w.shape
    blk0 = OFF_XBC // LANE

    def body(u_ref, w_ref, b_ref, o_ref):
        o_ref[0] = _silu(_conv(u_ref[0], w_ref[...], b_ref[...]))

    return pl.pallas_call(
        body,
        name="conv_silu_fwd",
        grid=(B, c // LANE),
        in_specs=[
            pl.BlockSpec((1, S, LANE), lambda bi, j: (bi, 0, blk0 + j)),
            pl.BlockSpec((k, LANE), lambda bi, j: (0, j)),
            pl.BlockSpec((1, LANE), lambda bi, j: (0, j)),
        ],
        out_specs=pl.BlockSpec((1, S, LANE), lambda bi, j: (bi, 0, j)),
        out_shape=jax.ShapeDtypeStruct((B, S, c), F32),
        compiler_params=_cparams(2),
    )(proj, w, b)


def conv_silu_bwd(proj, w, b, douts):
    B, S, _ = proj.shape
    k, c = w.shape
    blk0 = OFF_XBC // LANE
    ends = np.cumsum([d.shape[-1] // LANE for d in douts]).tolist()
    starts = [0] + ends[:-1]

    def body(u_ref, w_ref, b_ref, *refs):
        d_refs, (du_ref, dw_ref, db_ref) = refs[: len(douts)], refs[len(douts) :]
        j, bi = pl.program_id(0), pl.program_id(1)
        u, wv = u_ref[0], w_ref[...]
        cv = _conv(u, wv, b_ref[...])
        sg = jax.nn.sigmoid(cv)
        dout = d_refs[-1][0]
        for r in reversed(range(len(douts) - 1)):
            dout = jnp.where(j < ends[r], d_refs[r][0], dout)
        dc = dout * (sg * (1.0 + cv * (1.0 - sg)))
        du, dw, db = _conv_bwd(u, wv, dc)
        du_ref[0] = du.astype(du_ref.dtype)

        @pl.when(bi == 0)
        def _():
            dw_ref[...] = dw
            db_ref[...] = db

        @pl.when(bi > 0)
        def _():
            dw_ref[...] += dw
            db_ref[...] += db

    return pl.pallas_call(
        body,
        name="conv_silu_bwd",
        grid=(c // LANE, B),
        in_specs=[
            pl.BlockSpec((1, S, LANE), lambda j, bi: (bi, 0, blk0 + j)),
            pl.BlockSpec((k, LANE), lambda j, bi: (0, j)),
            pl.BlockSpec((1, LANE), lambda j, bi: (0, j)),
        ]
        + [
            pl.BlockSpec((1, S, LANE), lambda j, bi, lo=lo, hi=hi: (bi, 0, jnp.clip(j, lo, hi - 1) - lo))
            for lo, hi in zip(starts, ends)
        ],
        out_specs=[
            pl.BlockSpec((1, S, LANE), lambda j, bi: (bi, 0, j)),
            pl.BlockSpec((k, LANE), lambda j, bi: (0, j)),
            pl.BlockSpec((1, LANE), lambda j, bi: (0, j)),
        ],
        out_shape=[
            jax.ShapeDtypeStruct((B, S, c), MXU_DTYPE),
            jax.ShapeDtypeStruct((k, c), F32),
            jax.ShapeDtypeStruct((1, c), F32),
        ],
        compiler_params=_cparams(2),
    )(proj, w, b, *douts)


def _glu_specs(S, k, nb, batch_col):
    specs = []
    for off in (0, nb):
        specs.append(pl.BlockSpec((1, S, LANE), lambda *ids, off=off: (batch_col(*ids)[0], 0, off + batch_col(*ids)[1])))
    for rows in (k, 1):
        for off in (0, nb):
            specs.append(pl.BlockSpec((rows, LANE), lambda *ids, off=off: (0, off + batch_col(*ids)[1])))
    return specs


def conv_glu_fwd(u, w, b):
    B, S, c2 = u.shape
    k = w.shape[0]
    nb = c2 // 2 // LANE

    def body(ug_ref, uv_ref, wg_ref, wv_ref, bg_ref, bv_ref, o_ref):
        gate = _conv(ug_ref[0], wg_ref[...], bg_ref[...])
        val = _conv(uv_ref[0], wv_ref[...], bv_ref[...])
        o_ref[0] = (_silu(gate) * val).astype(o_ref.dtype)

    return pl.pallas_call(
        body,
        name="conv_glu_fwd",
        grid=(B, nb),
        in_specs=_glu_specs(S, k, nb, lambda bi, j: (bi, j)),
        out_specs=pl.BlockSpec((1, S, LANE), lambda bi, j: (bi, 0, j)),
        out_shape=jax.ShapeDtypeStruct((B, S, c2 // 2), MXU_DTYPE),
        compiler_params=_cparams(2),
    )(u, u, w, w, b, b)


def conv_glu_bwd(u, w, b, da):
    B, S, c2 = u.shape
    k = w.shape[0]
    nb = c2 // 2 // LANE

    def body(ug_ref, uv_ref, wg_ref, wv_ref, bg_ref, bv_ref, d_ref, dug_ref, duv_ref, dwg_ref, dwv_ref, dbg_ref, dbv_ref):
        bi = pl.program_id(1)
        ug, uv, wg, wv = ug_ref[0], uv_ref[0], wg_ref[...], wv_ref[...]
        gate, val = _conv(ug, wg, bg_ref[...]), _conv(uv, wv, bv_ref[...])
        sg = jax.nn.sigmoid(gate)
        dav = d_ref[0].astype(F32)
        dug, dwg, dbg = _conv_bwd(ug, wg, dav * val * (sg * (1.0 + gate * (1.0 - sg))))
        duv, dwv, dbv = _conv_bwd(uv, wv, dav * gate * sg)
        dug_ref[0] = dug.astype(dug_ref.dtype)
        duv_ref[0] = duv.astype(duv_ref.dtype)

        @pl.when(bi == 0)
        def _():
            dwg_ref[...], dwv_ref[...], dbg_ref[...], dbv_ref[...] = dwg, dwv, dbg, dbv

        @pl.when(bi > 0)
        def _():
            dwg_ref[...] += dwg
            dwv_ref[...] += dwv
            dbg_ref[...] += dbg
            dbv_ref[...] += dbv

    tok = pl.BlockSpec((1, S, LANE), lambda j, bi: (bi, 0, j))
    outs = pl.pallas_call(
        body,
        name="conv_glu_bwd",
        grid=(nb, B),
        in_specs=_glu_specs(S, k, nb, lambda j, bi: (bi, j)) + [tok],
        out_specs=[tok, tok] + [pl.BlockSpec((rows, LANE), lambda j, bi: (0, j)) for rows in (k, k, 1, 1)],
        out_shape=[jax.ShapeDtypeStruct((B, S, c2 // 2), MXU_DTYPE)] * 2
        + [jax.ShapeDtypeStruct((rows, c2 // 2), F32) for rows in (k, k, 1, 1)],
        compiler_params=_cparams(2),
    )(u, u, w, w, b, b, da)
    dug, duv, dwg, dwv, dbg, dbv = outs
    return dug, duv, jnp.concatenate([dwg, dwv], axis=1), jnp.concatenate([dbg, dbv], axis=1)


N_PAIR = SSD_HEADS // SSD_GROUPS // 2


def _ssd_chunk(prev, xs, bms, cms, dtraw, dt_bias, a_log, d_skip):
    L = CHUNK
    lane = lax.broadcasted_iota(jnp.int32, (1, LANE), 1)
    sub = lax.broadcasted_iota(jnp.int32, (LANE, 1), 0)
    row = lax.broadcasted_iota(jnp.int32, (L, L), 0)
    col = lax.broadcasted_iota(jnp.int32, (L, L), 1)
    tri = (row >= col).astype(F32)
    first = lane < SSD_HEAD_DIM

    dt = _softplus(dtraw + dt_bias)
    a = dt * (-jnp.exp(a_log))
    acs = jnp.dot(tri, a, precision=HIGHEST, preferred_element_type=F32)
    acs_t = acs.T
    a_end = jnp.sum(a, axis=0, keepdims=True)

    def lane_of(v, h):
        return jnp.sum(jnp.where(lane == h, v, 0.0), axis=1, keepdims=True)

    def expand(v, ha):
        return jnp.where(first, lane_of(v, ha), lane_of(v, ha + 1))

    ys, news = [], []
    for g in range(SSD_GROUPS):
        bm, cm = bms[g].astype(MXU_DTYPE), cms[g].astype(MXU_DTYPE)
        cb = lax.dot_general(cm, bm, (((1,), (1,)), ((), ())), preferred_element_type=F32)
        for j in range(N_PAIR):
            ha = g * (SSD_HEADS // SSD_GROUPS) + 2 * j
            x, state = xs[g * N_PAIR + j], prev[g * N_PAIR + j]
            dt_e, acs_e, end_e = expand(dt, ha), expand(acs, ha), expand(a_end, ha)
            xdt = x * dt_e
            y = jnp.dot(cm, state.astype(MXU_DTYPE), preferred_element_type=F32) * jnp.exp(acs_e)
            st = lax.dot_general(bm, (xdt * jnp.exp(end_e - acs_e)).astype(MXU_DTYPE), (((0,), (0,)), ((), ())), preferred_element_type=F32)
            news.append(state * jnp.exp(end_e) + st)
            for hh in range(2):
                h = ha + hh
                seg = lane_of(acs, h) - jnp.sum(jnp.where(sub == h, acs_t, 0.0), axis=0, keepdims=True)
                decay = jnp.exp(jnp.where(row >= col, seg, -jnp.inf))
                xh = jnp.where(first if hh == 0 else jnp.logical_not(first), xdt, 0.0)
                y = y + jnp.dot((cb * decay).astype(MXU_DTYPE), xh.astype(MXU_DTYPE), preferred_element_type=F32)
            ys.append(y + x * expand(d_skip, ha))
    return ys, news


N_PAIRS = SSD_GROUPS * N_PAIR


def _ssd_specs(chunk_of):
    bc = SSD_GROUPS * SSD_STATE
    par = pl.BlockSpec((1, LANE), lambda *ids: (0, 0))
    return [
        pl.BlockSpec((1, CHUNK, D_SSD), lambda *ids: (*chunk_of(*ids), 0)),
        pl.BlockSpec((1, CHUNK, bc), lambda *ids: (*chunk_of(*ids), D_SSD // bc)),
        pl.BlockSpec((1, CHUNK, bc), lambda *ids: (*chunk_of(*ids), D_SSD // bc + 1)),
        pl.BlockSpec((1, CHUNK, LANE), lambda *ids: (*chunk_of(*ids), OFF_DT // LANE)),
        par,
        par,
        par,
    ]


def _pair_cols(ref):
    return [ref[0, :, k * LANE : (k + 1) * LANE] for k in range(N_PAIRS)]


def _group_cols(ref):
    return [ref[0, :, g * SSD_STATE : (g + 1) * SSD_STATE] for g in range(SSD_GROUPS)]


def ssd_fwd(xbc_act, proj, dt_bias, a_log, d_skip):
    B, S, _ = xbc_act.shape
    nc = S // CHUNK

    def body(xs_ref, bm_ref, cm_ref, dt_ref, db_ref, al_ref, ds_ref, y_ref, st_ref, state):
        @pl.when(pl.program_id(1) == 0)
        def _():
            state[...] = jnp.zeros_like(state)

        prev = [state[k] for k in range(N_PAIRS)]
        for k in range(N_PAIRS):
            st_ref[0, 0, k] = prev[k]
        ys, news = _ssd_chunk(prev, _pair_cols(xs_ref), _group_cols(bm_ref), _group_cols(cm_ref), dt_ref[0], db_ref[...], al_ref[...], ds_ref[...])
        for k in range(N_PAIRS):
            y_ref[0, :, k * LANE : (k + 1) * LANE] = ys[k]
            state[k] = news[k]

    return pl.pallas_call(
        body,
        name="ssd_fwd",
        grid=(B, nc),
        in_specs=_ssd_specs(lambda b, c: (b, c)),
        out_specs=[
            pl.BlockSpec((1, CHUNK, D_SSD), lambda b, c: (b, c, 0)),
            pl.BlockSpec((1, 1, N_PAIRS, SSD_STATE, LANE), lambda b, c: (b, c, 0, 0, 0)),
        ],
        out_shape=[
            jax.ShapeDtypeStruct((B, S, D_SSD), F32),
            jax.ShapeDtypeStruct((B, nc, N_PAIRS, SSD_STATE, LANE), F32),
        ],
        scratch_shapes=[pltpu.VMEM((N_PAIRS, SSD_STATE, LANE), F32)],
        compiler_params=_cparams(2),
    )(xbc_act, xbc_act, xbc_act, proj, dt_bias, a_log, d_skip)


def ssd_bwd(xbc_act, proj, dt_bias, a_log, d_skip, states, dy):
    B, S, _ = xbc_act.shape
    nc = S // CHUNK
    bc = SSD_GROUPS * SSD_STATE
    chunk_of = lambda b, cr: (b, nc - 1 - cr)

    def body(xs_ref, bm_ref, cm_ref, dt_ref, db_ref, al_ref, ds_ref, st_ref, dy_ref,
             dxs_ref, dbm_ref, dcm_ref, ddt_ref, ddb_ref, dal_ref, dds_ref, dstate):
        b, cr = pl.program_id(0), pl.program_id(1)

        @pl.when(cr == 0)
        def _():
            dstate[...] = jnp.zeros_like(dstate)

        prev = [st_ref[0, 0, k] for k in range(N_PAIRS)]
        _, vjp = jax.vjp(
            _ssd_chunk, prev, _pair_cols(xs_ref), _group_cols(bm_ref), _group_cols(cm_ref), dt_ref[0], db_ref[...], al_ref[...], ds_ref[...]
        )
        dprev, dxs, dbm, dcm, ddt, ddb, dal, dds = vjp((_pair_cols(dy_ref), [dstate[k] for k in range(N_PAIRS)]))
        for k in range(N_PAIRS):
            dstate[k] = dprev[k]
            dxs_ref[0, :, k * LANE : (k + 1) * LANE] = dxs[k]
        for g in range(SSD_GROUPS):
            dbm_ref[0, :, g * SSD_STATE : (g + 1) * SSD_STATE] = dbm[g]
            dcm_ref[0, :, g * SSD_STATE : (g + 1) * SSD_STATE] = dcm[g]
        ddt_ref[0] = ddt
        first = (b == 0) & (cr == 0)

        @pl.when(first)
        def _():
            ddb_ref[...] = ddb
            dal_ref[...] = dal
            dds_ref[...] = dds

        @pl.when(jnp.logical_not(first))
        def _():
            ddb_ref[...] += ddb
            dal_ref[...] += dal
            dds_ref[...] += dds

    par = pl.BlockSpec((1, LANE), lambda *ids: (0, 0))
    in_specs = _ssd_specs(chunk_of) + [
        pl.BlockSpec((1, 1, N_PAIRS, SSD_STATE, LANE), lambda b, cr: (*chunk_of(b, cr), 0, 0, 0)),
        pl.BlockSpec((1, CHUNK, D_SSD), lambda b, cr: (*chunk_of(b, cr), 0)),
    ]
    return pl.pallas_call(
        body,
        name="ssd_bwd",
        grid=(B, nc),
        in_specs=in_specs,
        out_specs=[
            pl.BlockSpec((1, CHUNK, D_SSD), lambda b, cr: (*chunk_of(b, cr), 0)),
            pl.BlockSpec((1, CHUNK, bc), lambda b, cr: (*chunk_of(b, cr), 0)),
            pl.BlockSpec((1, CHUNK, bc), lambda b, cr: (*chunk_of(b, cr), 0)),
            pl.BlockSpec((1, CHUNK, LANE), lambda b, cr: (*chunk_of(b, cr), 0)),
            par,
            par,
            par,
        ],
        out_shape=[
            jax.ShapeDtypeStruct((B, S, D_SSD), F32),
            jax.ShapeDtypeStruct((B, S, bc), F32),
            jax.ShapeDtypeStruct((B, S, bc), F32),
            jax.ShapeDtypeStruct((B, S, LANE), F32),
            jax.ShapeDtypeStruct((1, LANE), F32),
            jax.ShapeDtypeStruct((1, LANE), F32),
            jax.ShapeDtypeStruct((1, LANE), F32),
        ],
        scratch_shapes=[pltpu.VMEM((N_PAIRS, SSD_STATE, LANE), F32)],
        compiler_params=_cparams(2),
    )(xbc_act, xbc_act, xbc_act, proj, dt_bias, a_log, d_skip, states, dy)


ATT_SCALE = (QK_NOPE + QK_ROPE) ** -0.5
ATT_TILE = 512


ATT_HEADS_FWD = 4
ATT_HEADS_BWD = 2
LOG2E = 1.4426950408889634
Q_PRESCALE = ATT_SCALE * LOG2E


def fn_rope_q(q, cos_t, sin_t, rot):
    parts = []
    for h in range(MLA_HEADS):
        qr = q[:, h * Q_HEAD_PAD + LANE : (h + 1) * Q_HEAD_PAD]
        qr = qr * cos_t + jnp.dot(qr, rot, precision=HIGHEST, preferred_element_type=F32) * sin_t
        parts += [q[:, h * Q_HEAD_PAD : h * Q_HEAD_PAD + LANE] * Q_PRESCALE, qr * Q_PRESCALE]
    return (jnp.concatenate(parts, axis=1),)


def _head_cols(ref, hh, width):
    return ref[0, :, hh * width : (hh + 1) * width]


def _causal_mask(s):
    return jnp.where(lax.broadcasted_iota(jnp.int32, s.shape, 0) >= lax.broadcasted_iota(jnp.int32, s.shape, 1), s, -jnp.inf)


def _nt(a, b):
    return lax.dot_general(a, b, (((1,), (1,)), ((), ())), preferred_element_type=F32)


def _tn(a, b):
    return lax.dot_general(a, b, (((0,), (0,)), ((), ())), preferred_element_type=F32)


def _per_tile(idx, n, variant):
    for v in range(n):

        @pl.when(idx == v)
        def _(v=v):
            variant(v)


def attention_fwd(qs, kv, krr):
    B, S, _ = qs.shape
    t = min(ATT_TILE, S)
    n = S // t
    hps = ATT_HEADS_FWD

    def body(q_ref, k_ref, kr_ref, o_ref, lse_ref):
        def variant(v):
            spans = ([(0, v * t, False)] if v else []) + [(v * t, (v + 1) * t, True)]
            for hh in range(hps):
                q = _head_cols(q_ref, hh, Q_HEAD_PAD)
                c0 = hh * 2 * LANE
                ss = []
                for a, b, diag in spans:
                    s = _nt(q, jnp.concatenate([k_ref[0, a:b, c0 : c0 + LANE], kr_ref[0, a:b]], axis=1))
                    ss.append(_causal_mask(s) if diag else s)
                m = functools.reduce(jnp.maximum, [jnp.max(s, axis=1, keepdims=True) for s in ss])
                ps = [jnp.exp2(s - m) for s in ss]
                l = functools.reduce(jnp.add, [jnp.sum(p, axis=1, keepdims=True) for p in ps])
                acc = functools.reduce(
                    jnp.add,
                    [jnp.dot(p.astype(MXU_DTYPE), k_ref[0, a:b, c0 + LANE : c0 + 2 * LANE], preferred_element_type=F32) for p, (a, b, _) in zip(ps, spans)],
                )
                o_ref[0, :, hh * V_DIM : (hh + 1) * V_DIM] = acc / l
                lse_ref[0, hh] = m + jnp.log2(l)

        _per_tile(pl.program_id(2), n, variant)

    return pl.pallas_call(
        body,
        name="attention_fwd",
        grid=(B, MLA_HEADS // hps, n),
        in_specs=[
            pl.BlockSpec((1, t, hps * Q_HEAD_PAD), lambda b, h, i: (b, i, h)),
            pl.BlockSpec((1, S, hps * 2 * LANE), lambda b, h, i: (b, 0, h)),
            pl.BlockSpec((1, S, LANE), lambda b, h, i: (b, 0, 0)),
        ],
        out_specs=[
            pl.BlockSpec((1, t, hps * V_DIM), lambda b, h, i: (b, i, h)),
            pl.BlockSpec((1, hps, t, 1), lambda b, h, i: (b, h, i, 0)),
        ],
        out_shape=[jax.ShapeDtypeStruct((B, S, D_ATT), F32), jax.ShapeDtypeStruct((B, MLA_HEADS, S, 1), F32)],
        compiler_params=_cparams(3),
    )(qs, kv, krr)


def attention_bwd_dq(qs, kv, krr, cos_t, sin_t, rot_t, o, lse, do):
    B, S, _ = qs.shape
    t = min(ATT_TILE, S)
    n = S // t
    hps = ATT_HEADS_BWD

    def body(q_ref, c_ref, s_ref, rott_ref, k_ref, kr_ref, o_ref, lse_ref, do_ref, dq_ref):
        def variant(v):
            spans = ([(0, v * t, False)] if v else []) + [(v * t, (v + 1) * t, True)]
            for hh in range(hps):
                q = _head_cols(q_ref, hh, Q_HEAD_PAD)
                dov = _head_cols(do_ref, hh, V_DIM)
                delta = jnp.sum(_head_cols(o_ref, hh, V_DIM) * dov, axis=1, keepdims=True)
                dob = dov.astype(MXU_DTYPE)
                c0 = hh * 2 * LANE
                acc = None
                for a, b, diag in spans:
                    k = jnp.concatenate([k_ref[0, a:b, c0 : c0 + LANE], kr_ref[0, a:b]], axis=1)
                    s = _nt(q, k)
                    p = jnp.exp2((_causal_mask(s) if diag else s) - lse_ref[0, hh])
                    ds = p * (_nt(dob, k_ref[0, a:b, c0 + LANE : c0 + 2 * LANE]) - delta) * ATT_SCALE
                    part = jnp.dot(ds.astype(MXU_DTYPE), k, preferred_element_type=F32)
                    acc = part if acc is None else acc + part
                r = acc[:, LANE:]
                dr = r * c_ref[0] + jnp.dot(r * s_ref[0], rott_ref[...], precision=HIGHEST, preferred_element_type=F32)
                dq_ref[0, :, hh * Q_HEAD_PAD : (hh + 1) * Q_HEAD_PAD] = jnp.concatenate([acc[:, :LANE], dr], axis=1).astype(dq_ref.dtype)

        _per_tile(pl.program_id(2), n, variant)

    return pl.pallas_call(
        body,
        name="attention_bwd_dq",
        grid=(B, MLA_HEADS // hps, n),
        in_specs=[
            pl.BlockSpec((1, t, hps * Q_HEAD_PAD), lambda b, h, i: (b, i, h)),
            pl.BlockSpec((1, t, LANE), lambda b, h, i: (b, i, 0)),
            pl.BlockSpec((1, t, LANE), lambda b, h, i: (b, i, 0)),
            pl.BlockSpec((LANE, LANE), lambda b, h, i: (0, 0)),
            pl.BlockSpec((1, S, hps * 2 * LANE), lambda b, h, i: (b, 0, h)),
            pl.BlockSpec((1, S, LANE), lambda b, h, i: (b, 0, 0)),
            pl.BlockSpec((1, t, hps * V_DIM), lambda b, h, i: (b, i, h)),
            pl.BlockSpec((1, hps, t, 1), lambda b, h, i: (b, h, i, 0)),
            pl.BlockSpec((1, t, hps * V_DIM), lambda b, h, i: (b, i, h)),
        ],
        out_specs=pl.BlockSpec((1, t, hps * Q_HEAD_PAD), lambda b, h, i: (b, i, h)),
        out_shape=jax.ShapeDtypeStruct(qs.shape, MXU_DTYPE),
        compiler_params=_cparams(3),
    )(qs, cos_t, sin_t, rot_t, kv, krr, o, lse, do)


def attention_bwd_dkv(qs, kv, krr, o, lse, do):
    B, S, _ = qs.shape
    t = min(ATT_TILE, S)
    n = S // t
    hps = ATT_HEADS_BWD

    def body(q_ref, k_ref, kr_ref, o_ref, lse_ref, do_ref, dkv_ref, dkr_ref):
        h = pl.program_id(2)

        def variant(v):
            spans = [(v * t, (v + 1) * t, True)] + ([((v + 1) * t, S, False)] if v + 1 < n else [])
            dkr = None
            for hh in range(hps):
                c0 = hh * 2 * LANE
                k = jnp.concatenate([k_ref[0, :, c0 : c0 + LANE], kr_ref[0]], axis=1)
                vals = k_ref[0, :, c0 + LANE : c0 + 2 * LANE]
                dk = dv = None
                for a, b, diag in spans:
                    q = q_ref[0, a:b, hh * Q_HEAD_PAD : (hh + 1) * Q_HEAD_PAD]
                    dov = do_ref[0, a:b, hh * V_DIM : (hh + 1) * V_DIM]
                    dob = dov.astype(MXU_DTYPE)
                    s = _nt(q, k)
                    p = jnp.exp2((_causal_mask(s) if diag else s) - lse_ref[0, hh, a:b])
                    delta = jnp.sum(o_ref[0, a:b, hh * V_DIM : (hh + 1) * V_DIM] * dov, axis=1, keepdims=True)
                    ds2 = p * (_nt(dob, vals) - delta) * (1.0 / LOG2E)
                    dv_part, dk_part = _tn(p.astype(MXU_DTYPE), dob), _tn(ds2.astype(MXU_DTYPE), q)
                    dv = dv_part if dv is None else dv + dv_part
                    dk = dk_part if dk is None else dk + dk_part
                dkv_ref[0, :, c0 : c0 + 2 * LANE] = jnp.concatenate([dk[:, :LANE], dv], axis=1).astype(dkv_ref.dtype)
                dkr = dk[:, LANE:] if dkr is None else dkr + dk[:, LANE:]

            @pl.when(h == 0)
            def _():
                dkr_ref[0] = dkr

            @pl.when(h > 0)
            def _():
                dkr_ref[0] += dkr

        _per_tile(pl.program_id(1), n, variant)

    return pl.pallas_call(
        body,
        name="attention_bwd_dkv",
        grid=(B, n, MLA_HEADS // hps),
        in_specs=[
            pl.BlockSpec((1, S, hps * Q_HEAD_PAD), lambda b, j, h: (b, 0, h)),
            pl.BlockSpec((1, t, hps * 2 * LANE), lambda b, j, h: (b, j, h)),
            pl.BlockSpec((1, t, LANE), lambda b, j, h: (b, j, 0)),
            pl.BlockSpec((1, S, hps * V_DIM), lambda b, j, h: (b, 0, h)),
            pl.BlockSpec((1, hps, S, 1), lambda b, j, h: (b, h, 0, 0)),
            pl.BlockSpec((1, S, hps * V_DIM), lambda b, j, h: (b, 0, h)),
        ],
        out_specs=[
            pl.BlockSpec((1, t, hps * 2 * LANE), lambda b, j, h: (b, j, h)),
            pl.BlockSpec((1, t, LANE), lambda b, j, h: (b, j, 0)),
        ],
        out_shape=[jax.ShapeDtypeStruct(kv.shape, MXU_DTYPE), jax.ShapeDtypeStruct(krr.shape, F32)],
        compiler_params=_cparams(3),
    )(qs, kv, krr, o, lse, do)


_IN_SPLITS = np.cumsum([D_SSD, D_XBC, SSD_HEADS, Q_RANK, KV_RANK]).tolist()


def _pad_last(t, n):
    return jnp.pad(t, [(0, 0)] * (t.ndim - 1) + [(0, n - t.shape[-1])])


def win_to_kernel(w):
    z, xbc, dt, cq, ckv, kr = jnp.split(w, _IN_SPLITS, axis=-1)
    return jnp.concatenate([z, xbc, ckv, _pad_last(kr, LANE), _pad_last(dt, LANE), cq], axis=-1)


def win_from_kernel(g):
    z, xbc = g[..., :OFF_XBC], g[..., OFF_XBC:OFF_CKV]
    ckv, kr = g[..., OFF_CKV:OFF_KR], g[..., OFF_KR : OFF_KR + QK_ROPE]
    dt, cq = g[..., OFF_DT : OFF_DT + SSD_HEADS], g[..., OFF_CQ:]
    return jnp.concatenate([z, xbc, dt, cq, ckv, kr], axis=-1)


def wuq_to_kernel(w):
    w = w.reshape(*w.shape[:-1], MLA_HEADS, QK_NOPE + QK_ROPE)
    return _pad_last(w, Q_HEAD_PAD).reshape(*w.shape[:-2], MLA_HEADS * Q_HEAD_PAD)


def wuq_from_kernel(g):
    g = g.reshape(*g.shape[:-1], MLA_HEADS, Q_HEAD_PAD)[..., : QK_NOPE + QK_ROPE]
    return g.reshape(*g.shape[:-2], MLA_HEADS * (QK_NOPE + QK_ROPE))


def _lane_pad_row(v):
    return _pad_last(v, LANE)[None, :]


def rope_tables(positions):
    inv_freq = jnp.asarray(1.0 / (ROPE_BASE ** (np.arange(0, QK_ROPE, 2, dtype=np.float32) / QK_ROPE)))
    ang = positions.astype(F32)[..., None] * inv_freq
    cos, sin = jnp.cos(ang), jnp.sin(ang)
    zeros = jnp.zeros(cos.shape[:-1] + (LANE - QK_ROPE,), F32)
    rot = np.zeros((LANE, LANE), np.float32)
    half = QK_ROPE // 2
    for j in range(half):
        rot[j + half, j] = -1.0
        rot[j, j + half] = 1.0
    return jnp.concatenate([cos, cos, zeros], -1), jnp.concatenate([sin, sin, zeros], -1), jnp.asarray(rot), jnp.asarray(rot.T)


BIG = ("w_in", "w_uq", "w_ukv", "w_out", "w_up", "w_down")


def local_step(x, target, positions, mod_raw, W):
    B, S, D = x.shape
    cos_t, sin_t, rot, rot_t = rope_tables(positions)
    row = lambda v: v.reshape(1, -1)
    w_in_k, w_uq_k = win_to_kernel(W["w_in"]), wuq_to_kernel(W["w_uq"])
    Ls = []
    for l in range(DEPTH):
        mods = [mod_raw[l, :, k][:, None, :] for k in range(6)]
        bias = [row(W["b_ada"][l, k * D : (k + 1) * D]) for k in range(6)]
        Ls.append(
            dict(
                mods=mods,
                bias=bias,
                w_in=Layer(w_in_k, l),
                w_uq=Layer(w_uq_k, l),
                w_ukv=Layer(W["w_ukv"], l),
                w_out=Layer(W["w_out"], l),
                w_up=Layer(W["w_up"], l),
                w_down=Layer(W["w_down"], l),
                conv_w=W["conv_w"][l],
                conv_b=row(W["conv_b"][l]),
                conv_ff_w=W["conv_ff_w"][l],
                conv_ff_b=row(W["conv_ff_b"][l]),
                dt_bias=_lane_pad_row(W["dt_bias"][l]),
                a_log=_lane_pad_row(W["a_log"][l]),
                d_skip=_lane_pad_row(W["d_skip"][l]),
                norm_mix=row(W["norm_mix"][l]),
                ssd_norm=row(W["ssd_norm"][l]),
                q_norm=row(W["q_norm"][l]),
                kv_norm=row(W["kv_norm"][l]),
                attn_norm=row(W["attn_norm"][l]),
                norm_mlp=row(W["norm_mlp"][l]),
            )
        )
    fnorm = row(W["final_norm"])

    p0 = Ls[0]
    (h1,) = rowwise_fwd(
        "prenorm_fwd", fn_prenorm, [x], [p0["mods"][1], p0["mods"][0]], [p0["norm_mix"], p0["bias"][1], p0["bias"][0]], [(D, MXU_DTYPE)]
    )
    xin = x
    for l, p in enumerate(Ls):
        s = p["saved"] = dict(xin=xin, h1=h1)
        s["proj"] = proj = matmul(f"w_in_fwd{l}", h1, p["w_in"])
        s["xbc_act"] = xbc_act = conv_silu_fwd(proj, p["conv_w"], p["conv_b"])
        s["yscan"], s["states"] = ssd_fwd(xbc_act, proj, p["dt_bias"], p["a_log"], p["d_skip"])
        mla_in = [Col(proj, Q_RANK, OFF_CQ // Q_RANK), Col(proj, KV_RANK, OFF_CKV // KV_RANK), Col(proj, LANE, OFF_KR // LANE), cos_t, sin_t]
        s["cqn"], s["ckvn"], s["krr"] = rowwise_fwd(
            f"mla_prep_fwd{l}", fn_mla_prep, mla_in, [], [p["q_norm"], p["kv_norm"], rot], [(Q_RANK, MXU_DTYPE), (KV_RANK, MXU_DTYPE), (LANE, MXU_DTYPE)]
        )
        q = matmul(f"w_uq_fwd{l}", s["cqn"], p["w_uq"], out_dtype=MXU_DTYPE)
        (s["qs"],) = rowwise_fwd(f"rope_q_fwd{l}", fn_rope_q, [q, cos_t, sin_t], [], [rot], [(q.shape[-1], MXU_DTYPE)])
        s["kv"] = matmul(f"w_ukv_fwd{l}", s["ckvn"], p["w_ukv"], out_dtype=MXU_DTYPE)
        s["o"], s["lse"] = attention_fwd(s["qs"], s["kv"], s["krr"])
        (s["ycat"],) = rowwise_fwd(
            f"mix_fwd{l}", fn_mix, [s["yscan"], Col(proj, D_SSD, 0), s["o"]], [], [p["ssd_norm"], p["attn_norm"]], [(D_SSD + D_ATT, MXU_DTYPE)]
        )
        s["ymix"] = matmul(f"w_out_fwd{l}", s["ycat"], p["w_out"])
        s["x1"], s["h2"] = rowwise_fwd(
            f"mid_fwd{l}",
            fn_resid_prenorm,
            [xin, s["ymix"]],
            [p["mods"][2], p["mods"][4], p["mods"][3]],
            [p["norm_mlp"], p["bias"][2], p["bias"][4], p["bias"][3]],
            [(D, F32), (D, MXU_DTYPE)],
        )
        s["u"] = matmul(f"w_up_fwd{l}", s["h2"], p["w_up"])
        s["a"] = conv_glu_fwd(s["u"], p["conv_ff_w"], p["conv_ff_b"])
        s["ff"] = matmul(f"w_down_fwd{l}", s["a"], p["w_down"])
        if l + 1 < DEPTH:
            n = Ls[l + 1]
            xin, h1 = rowwise_fwd(
                f"join_fwd{l}",
                fn_resid_prenorm,
                [s["x1"], s["ff"]],
                [p["mods"][5], n["mods"][1], n["mods"][0]],
                [n["norm_mix"], p["bias"][5], n["bias"][1], n["bias"][0]],
                [(D, F32), (D, MXU_DTYPE)],
            )

    G = {k: [None] * DEPTH for k in W if k not in ("final_norm", "w_ada")}
    dmod = [[None] * 6 for _ in range(DEPTH)]
    dbias = [[None] * 6 for _ in range(DEPTH)]
    last = Ls[-1]
    sl = last["saved"]
    loss, dx1, dff, dmod[-1][5], G["final_norm"], dbias[-1][5] = final_fwdbwd(
        sl["x1"], sl["ff"], target, last["mods"][5], fnorm, last["bias"][5]
    )
    grad_x = None
    for l in reversed(range(DEPTH)):
        p = Ls[l]
        s = p["saved"]
        da = matmul(f"w_down_dgrad{l}", dff, p["w_down"], tb=True)
        G["w_down"][l] = matmul(f"w_down_wgrad{l}", s["a"], dff, ta=True)
        du_gate, du_val, G["conv_ff_w"][l], dcb = conv_glu_bwd(s["u"], p["conv_ff_w"], p["conv_ff_b"], da)
        G["conv_ff_b"][l] = dcb[0]
        dh2 = matmul(f"w_up_dgrad{l}", [du_gate, du_val], p["w_up"], tb=True)
        G["w_up"][l] = jnp.concatenate(
            [matmul(f"w_up_wgrad{l}_{half}", s["h2"], d, ta=True) for half, d in (("gate", du_gate), ("val", du_val))], axis=1
        )
        dxb, dymix, dmod[l][2], dmod[l][4], dmod[l][3], G["norm_mlp"][l], dbias[l][2], dbias[l][4], dbias[l][3] = rowwise_bwd(
            f"mid_bwd{l}",
            fn_resid_prenorm,
            [s["xin"], s["ymix"]],
            [p["mods"][2], p["mods"][4], p["mods"][3]],
            [p["norm_mlp"], p["bias"][2], p["bias"][4], p["bias"][3]],
            [dx1, dh2],
            [True, True],
            [True] * 4,
            mxu_only=(1,),
        )
        dycat = matmul(f"w_out_dgrad{l}", dymix, p["w_out"], tb=True)
        G["w_out"][l] = matmul(f"w_out_wgrad{l}", s["ycat"], dymix, ta=True)
        dyscan, dz, do, G["ssd_norm"][l], G["attn_norm"][l] = rowwise_bwd(
            f"mix_bwd{l}", fn_mix, [s["yscan"], Col(s["proj"], D_SSD, 0), s["o"]], [], [p["ssd_norm"], p["attn_norm"]], [dycat], [True] * 3, [True] * 2, mxu_only=(1,)
        )
        dq = attention_bwd_dq(s["qs"], s["kv"], s["krr"], cos_t, sin_t, rot_t, s["o"], s["lse"], do)
        dkv, dkrr = attention_bwd_dkv(s["qs"], s["kv"], s["krr"], s["o"], s["lse"], do)
        dcqn = matmul(f"w_uq_dgrad{l}", dq, p["w_uq"], tb=True)
        G["w_uq"][l] = wuq_from_kernel(matmul(f"w_uq_wgrad{l}", s["cqn"], dq, ta=True))
        dckvn = matmul(f"w_ukv_dgrad{l}", dkv, p["w_ukv"], tb=True)
        G["w_ukv"][l] = matmul(f"w_ukv_wgrad{l}", s["ckvn"], dkv, ta=True)
        proj = s["proj"]
        mla_in = [Col(proj, Q_RANK, OFF_CQ // Q_RANK), Col(proj, KV_RANK, OFF_CKV // KV_RANK), Col(proj, LANE, OFF_KR // LANE), cos_t, sin_t]
        dcq, dckv, dkr, G["q_norm"][l], G["kv_norm"][l] = rowwise_bwd(
            f"mla_prep_bwd{l}",
            fn_mla_prep,
            mla_in,
            [],
            [p["q_norm"], p["kv_norm"], rot],
            [dcqn, dckvn, dkrr],
            [True, True, True, False, False],
            [True, True, False],
            mxu_only=(0, 1, 2),
        )
        dxs, dbm, dcm, ddt, ddb, dal, dds = ssd_bwd(s["xbc_act"], proj, p["dt_bias"], p["a_log"], p["d_skip"], s["states"], dyscan)
        G["dt_bias"][l], G["a_log"][l], G["d_skip"][l] = ddb[0, :SSD_HEADS], dal[0, :SSD_HEADS], dds[0, :SSD_HEADS]
        dxbc, G["conv_w"][l], dcb = conv_silu_bwd(proj, p["conv_w"], p["conv_b"], [dxs, dbm, dcm])
        G["conv_b"][l] = dcb[0]
        dproj = jnp.concatenate([dz, dxbc, dckv, dkr, ddt.astype(MXU_DTYPE), dcq], axis=-1)
        dh1 = matmul(f"w_in_dgrad{l}", dproj, p["w_in"], tb=True)
        G["w_in"][l] = win_from_kernel(matmul(f"w_in_wgrad{l}", s["h1"], dproj, ta=True))
        if l > 0:
            q = Ls[l - 1]
            sq = q["saved"]
            dx1, dff, dmod[l - 1][5], dmod[l][1], dmod[l][0], G["norm_mix"][l], dbias[l - 1][5], dbias[l][1], dbias[l][0] = rowwise_bwd(
                f"join_bwd{l - 1}",
                fn_resid_prenorm,
                [sq["x1"], sq["ff"]],
                [q["mods"][5], p["mods"][1], p["mods"][0]],
                [p["norm_mix"], q["bias"][5], p["bias"][1], p["bias"][0]],
                [dxb, dh1],
                [True, True],
                [True] * 4,
                mxu_only=(1,),
            )
        else:
            grad_x, dmod[0][1], dmod[0][0], G["norm_mix"][0], dbias[0][1], dbias[0][0] = rowwise_bwd(
                "prenorm_bwd",
                fn_prenorm,
                [x],
                [p["mods"][1], p["mods"][0]],
                [p["norm_mix"], p["bias"][1], p["bias"][0]],
                [dh1],
                [True],
                [True] * 3,
                adds={0: dxb},
            )
    for l in range(DEPTH):
        G["b_ada"][l] = jnp.concatenate([d[0] for d in dbias[l]])
        for k in ("norm_mix", "ssd_norm", "q_norm", "kv_norm", "attn_norm", "norm_mlp"):
            G[k][l] = G[k][l][0]
    grads = {k: (v if k in BIG else jnp.stack(v) if isinstance(v, list) else v[0]) for k, v in G.items()}
    dmod_raw = jnp.stack([jnp.stack([d[:, 0, :] for d in dmod[l]], axis=1) for l in range(DEPTH)])
    return loss, grad_x, grads, dmod_raw


MESH = pl.DeviceIdType.MESH
ANY = pl.BlockSpec(memory_space=pl.ANY)
PACK_W = 1024
PACK_TILE = 256


def _place():
    x, y, c = lax.axis_index("x"), lax.axis_index("y"), lax.axis_index("c")
    chips = [(1 - x, y), (x, 1 - y), (1 - x, 1 - y)]
    return x, y, c, chips


def _remote(src, dst, send_sem, recv_sem, to):
    return pltpu.make_async_remote_copy(src_ref=src, dst_ref=dst, send_sem=send_sem, recv_sem=recv_sem, device_id=to, device_id_type=MESH)


def all_gather8(name, v):
    m_per, n = v.shape

    def body(x_ref, out_ref, send_sems, recv_sems, local_sem):
        x, y, c, chips = _place()
        me, sibling = (x, y, c), (x, y, 1 - c)

        def rows(px, py, pc):
            return out_ref.at[pl.ds((4 * px + 2 * py + pc) * m_per, m_per), :]

        def copy(k, block, to, src=None):
            return _remote(rows(*block) if src is None else src, rows(*block), send_sems.at[k], recv_sems.at[k], to)

        mine = pltpu.make_async_copy(x_ref, rows(*me), local_sem)
        mine.start()
        first = [copy(0, me, sibling, src=x_ref)]
        first += [copy(1 + j, me, (*chip, c), src=x_ref) for j, chip in enumerate(chips)]
        for cp in first:
            cp.start()
        passed = [copy(4 + j, (*chip, c), sibling) for j, chip in enumerate(chips)]
        for j, chip in enumerate(chips):
            copy(1 + j, (*chip, c), me).wait_recv()
            passed[j].start()
        copy(0, sibling, me).wait_recv()
        for j, chip in enumerate(chips):
            copy(4 + j, (*chip, 1 - c), me).wait_recv()
        for cp in first + passed:
            cp.wait_send()
        mine.wait()

    return pl.pallas_call(
        body,
        name=name,
        out_shape=jax.ShapeDtypeStruct((N_DEV * m_per, n), v.dtype),
        in_specs=[pl.BlockSpec(memory_space=pltpu.VMEM)],
        out_specs=pl.BlockSpec(memory_space=pltpu.VMEM),
        scratch_shapes=[pltpu.SemaphoreType.DMA((7,)), pltpu.SemaphoreType.DMA((7,)), pltpu.SemaphoreType.DMA],
        compiler_params=pltpu.CompilerParams(vmem_limit_bytes=VMEM_LIMIT),
    )(v)


def gather_weights(pack):
    R, n = pack.shape
    rh = R // 2

    def body(x_ref, out_ref, send_sems, recv_sems):
        x, y, c, chips = _place()
        me = 2 * x + y

        def half(chip, hc):
            return out_ref.at[chip, pl.ds(hc * rh, rh), :]

        src = x_ref.at[pl.ds(c * rh, rh), :]
        first = [_remote(src, half(me, c), send_sems.at[j], recv_sems.at[j], (px, py, c)) for j, (px, py) in enumerate(chips)]
        for cp in first:
            cp.start()
        passed = []
        for j, (px, py) in enumerate(chips):
            got = half(2 * px + py, c)
            _remote(got, got, send_sems.at[j], recv_sems.at[j], (px, py, c)).wait_recv()
            cp = _remote(got, got, send_sems.at[3 + j], recv_sems.at[3 + j], (x, y, 1 - c))
            cp.start()
            passed.append(cp)
        for j, (px, py) in enumerate(chips):
            got = half(2 * px + py, 1 - c)
            _remote(got, got, send_sems.at[3 + j], recv_sems.at[3 + j], (x, y, 1 - c)).wait_recv()
        for cp in first + passed:
            cp.wait_send()

    return pl.pallas_call(
        body,
        name="gather_weights",
        out_shape=jax.ShapeDtypeStruct((N_CHIPS, R, n), pack.dtype),
        in_specs=[ANY],
        out_specs=ANY,
        scratch_shapes=[pltpu.SemaphoreType.DMA((6,)), pltpu.SemaphoreType.DMA((6,))],
    )(pack)


def swap_halves(g):
    n_slot, R, n = g.shape
    rh = R // 2

    def body(g_ref, got_ref, send_sem, recv_sem):
        x, y, c, _ = _place()
        cp = _remote(g_ref.at[:, pl.ds((1 - c) * rh, rh), :], got_ref, send_sem, recv_sem, (x, y, 1 - c))
        cp.start()
        cp.wait()

    return pl.pallas_call(
        body,
        name="swap_halves",
        out_shape=jax.ShapeDtypeStruct((n_slot, rh, n), g.dtype),
        in_specs=[ANY],
        out_specs=ANY,
        scratch_shapes=[pltpu.SemaphoreType.DMA, pltpu.SemaphoreType.DMA],
    )(g)


def scatter_chips(buf):
    def body(s_ref, got_ref, send_sems, recv_sems):
        x, y, c, chips = _place()
        me = 2 * x + y
        cps = [_remote(s_ref.at[2 * px + py], got_ref.at[me], send_sems.at[j], recv_sems.at[j], (px, py, c)) for j, (px, py) in enumerate(chips)]
        for cp in cps:
            cp.start()
        for j, (px, py) in enumerate(chips):
            got = got_ref.at[2 * px + py]
            _remote(got, got, send_sems.at[j], recv_sems.at[j], (px, py, c)).wait_recv()
        for cp in cps:
            cp.wait_send()

    return pl.pallas_call(
        body,
        name="scatter_chips",
        out_shape=jax.ShapeDtypeStruct(buf.shape, buf.dtype),
        in_specs=[ANY],
        out_specs=ANY,
        scratch_shapes=[pltpu.SemaphoreType.DMA((3,)), pltpu.SemaphoreType.DMA((3,))],
    )(buf)


def swap_with_sibling(h):
    def body(h_ref, got_ref, send_sem, recv_sem):
        x, y, c, _ = _place()
        cp = _remote(h_ref, got_ref, send_sem, recv_sem, (x, y, 1 - c))
        cp.start()
        cp.wait()

    return pl.pallas_call(
        body,
        name="swap_with_sibling",
        out_shape=jax.ShapeDtypeStruct(h.shape, h.dtype),
        in_specs=[ANY],
        out_specs=ANY,
        scratch_shapes=[pltpu.SemaphoreType.DMA, pltpu.SemaphoreType.DMA],
    )(h)


def chip_sum(g, got, core, chip):
    n_slot, R, n = g.shape
    rh = R // 2
    nb = rh // PACK_TILE

    def body(pos, g_ref, got_ref, wire_ref, own_ref):
        k = pl.program_id(1)
        s = g_ref[0] + got_ref[0]
        wire_ref[0] = s.astype(wire_ref.dtype)

        @pl.when(k == pos[1])
        def _():
            own_ref[...] = s

    grid_spec = pltpu.PrefetchScalarGridSpec(
        num_scalar_prefetch=1,
        grid=(nb, n_slot),
        in_specs=[
            pl.BlockSpec((1, PACK_TILE, n), lambda i, k, pos: (k, pos[0] * nb + i, 0)),
            pl.BlockSpec((1, PACK_TILE, n), lambda i, k, pos: (k, i, 0)),
        ],
        out_specs=[
            pl.BlockSpec((1, PACK_TILE, n), lambda i, k, pos: (k, i, 0)),
            pl.BlockSpec((PACK_TILE, n), lambda i, k, pos: (i, 0)),
        ],
    )
    return pl.pallas_call(
        body,
        name="chip_sum",
        grid_spec=grid_spec,
        out_shape=[jax.ShapeDtypeStruct((n_slot, rh, n), WIRE_DTYPE), jax.ShapeDtypeStruct((rh, n), F32)],
        compiler_params=_cparams(2),
    )(jnp.stack([core, chip]).astype(jnp.int32), g, got)


def mesh_sum(own, got, chip):
    rh, n = own.shape
    n_slot = got.shape[0]

    def body(pos, own_ref, *refs):
        out_ref = refs[-1]
        acc = own_ref[...]
        for k in range(n_slot):
            acc = acc + jnp.where(k != pos[0], refs[k][0].astype(F32), 0.0)
        out_ref[...] = acc

    grid_spec = pltpu.PrefetchScalarGridSpec(
        num_scalar_prefetch=1,
        grid=(rh // PACK_TILE,),
        in_specs=[pl.BlockSpec((PACK_TILE, n), lambda i, pos: (i, 0))]
        + [pl.BlockSpec((1, PACK_TILE, n), lambda i, pos, k=k: (k, i, 0)) for k in range(n_slot)],
        out_specs=pl.BlockSpec((PACK_TILE, n), lambda i, pos: (i, 0)),
    )
    return pl.pallas_call(
        body, name="mesh_sum", grid_spec=grid_spec, out_shape=jax.ShapeDtypeStruct((rh, n), F32), compiler_params=_cparams(1)
    )(jnp.stack([chip]).astype(jnp.int32), own, *([got] * n_slot))


def sum_devices(name, v):
    def body(v_ref, o_ref):
        acc = v_ref[0]
        for d in range(1, N_DEV):
            acc = acc + v_ref[d]
        o_ref[...] = acc

    return pl.pallas_call(body, name=name, out_shape=jax.ShapeDtypeStruct(v.shape[1:], F32))(v)


def adamw(name, w, g, m, v):
    shape = w.shape
    w2, g2, m2, v2 = (t.reshape(-1, shape[-1]) for t in (w, g, m, v))
    rows, n = w2.shape
    tr = _pick(rows, (256, 128, 64, 32, 16, 8))
    c1 = 1.0 / (1.0 - ADAM_B1**ADAM_STEP)
    c2 = 1.0 / (1.0 - ADAM_B2**ADAM_STEP)

    def body(w_ref, g_ref, m_ref, v_ref, d_ref, nm_ref, nv_ref):
        gv = g_ref[...]
        nm = ADAM_B1 * m_ref[...] + (1.0 - ADAM_B1) * gv
        nv = ADAM_B2 * v_ref[...] + (1.0 - ADAM_B2) * jnp.square(gv)
        d_ref[...] = -ADAM_LR * ((nm * c1) / (jnp.sqrt(nv * c2) + ADAM_EPS) + ADAM_WD * w_ref[...])
        nm_ref[...] = nm
        nv_ref[...] = nv

    spec = pl.BlockSpec((tr, n), lambda i: (i, 0))
    outs = pl.pallas_call(
        body,
        name=name,
        grid=(rows // tr,),
        in_specs=[spec] * 4,
        out_specs=[spec] * 3,
        out_shape=[jax.ShapeDtypeStruct((rows, n), F32)] * 3,
        compiler_params=_cparams(1),
    )(w2, g2, m2, v2)
    return [o.reshape(shape) for o in outs]


def ada_fwd(c_all, w_ada):
    n_tok, d = c_all.shape
    depth, _, cols = w_ada.shape
    tn = _pick(cols, (512, 384, 256, 128))

    def body(c_ref, w_ref, o_ref):
        o_ref[0] = jnp.dot(_silu(c_ref[...]).astype(MXU_DTYPE), w_ref[0].astype(MXU_DTYPE), preferred_element_type=F32)

    return pl.pallas_call(
        body,
        name="ada_fwd",
        grid=(depth, cols // tn),
        in_specs=[pl.BlockSpec((n_tok, d), lambda l, j: (0, 0)), pl.BlockSpec((1, d, tn), lambda l, j: (l, 0, j))],
        out_specs=pl.BlockSpec((1, n_tok, tn), lambda l, j: (l, 0, j)),
        out_shape=jax.ShapeDtypeStruct((depth, n_tok, cols), F32),
        compiler_params=_cparams(2),
    )(c_all, w_ada)


COL_SHARDED = ("w_in", "w_uq", "w_ukv", "w_up")


def _pack_rows(parts, rows):
    parts = [p.reshape(-1, PACK_W) for p in parts]
    fill = jnp.zeros((rows - sum(p.shape[0] for p in parts), PACK_W), parts[0].dtype)
    return jnp.concatenate(parts + [fill], axis=0)


def _big_rows(shards):
    n = sum(int(np.prod(shards[k].shape)) for k in BIG) // PACK_W
    return -(-n // (2 * PACK_TILE)) * 2 * PACK_TILE


def unpack_gathered(gathered, shard_shapes):
    out, r0 = {}, 0
    for k in BIG:
        shp = shard_shapes[k]
        n = int(np.prod(shp)) // PACK_W
        seg = gathered[:, r0 : r0 + n].reshape(N_CHIPS, *shp)
        r0 += n
        if k in COL_SHARDED:
            out[k] = seg.transpose(1, 2, 0, 3).reshape(shp[0], shp[1], N_CHIPS * shp[2])
        else:
            out[k] = seg.transpose(1, 0, 2, 3).reshape(shp[0], N_CHIPS * shp[1], shp[2])
    return out


def pack_full_grads(grads, shard_shapes, rows):
    parts = []
    for k in BIG:
        _, rows_k, cols_k = shard_shapes[k]
        layers = grads[k]
        if (rows_k * cols_k // PACK_W) % SUBLANES:
            layers = [jnp.concatenate(layers, axis=0)]
        for g in layers:
            if k in COL_SHARDED:
                g = g.reshape(g.shape[0], N_CHIPS, cols_k).transpose(1, 0, 2)
            else:
                g = g.reshape(-1, N_CHIPS, rows_k, cols_k).transpose(1, 0, 2, 3)
            parts.append(g.reshape(N_CHIPS, -1, PACK_W))
    fill = jnp.zeros((N_CHIPS, rows - sum(p.shape[1] for p in parts), PACK_W), F32)
    return jnp.concatenate(parts + [fill], axis=1)


def unpack_shards(buf, shard_shapes):
    out, r0 = {}, 0
    for k in BIG:
        n = int(np.prod(shard_shapes[k])) // PACK_W
        out[k] = buf[r0 : r0 + n].reshape(shard_shapes[k])
        r0 += n
    return out


def _flat_pack(arrs, row_multiple=8):
    flat = jnp.concatenate([a.reshape(-1) for a in arrs])
    per = PACK_W * row_multiple
    n = -(-flat.shape[0] // per) * per
    return jnp.pad(flat, (0, n - flat.shape[0])).reshape(-1, PACK_W)


def _flat_unpack(buf, shapes):
    flat, out, o = buf.reshape(-1), [], 0
    for s in shapes:
        n = int(np.prod(s))
        out.append(flat[o : o + n].reshape(s))
        o += n
    return out


WEIGHTS = (
    "w_ada", "b_ada", "norm_mix", "w_in", "conv_w", "conv_b", "dt_bias", "a_log", "d_skip", "ssd_norm", "q_norm", "w_uq",
    "kv_norm", "w_ukv", "attn_norm", "w_out", "norm_mlp", "w_up", "conv_ff_w", "conv_ff_b", "w_down", "final_norm",
)
REPLICATED = ("b_ada", "norm_mix", "conv_b", "dt_bias", "a_log", "d_skip", "ssd_norm", "q_norm", "kv_norm", "attn_norm",
              "norm_mlp", "conv_ff_b", "final_norm")
CONV_SHARDED = ("conv_w", "conv_ff_w")


def kernel(x, c, positions, w_ada, b_ada, norm_mix, w_in, conv_w, conv_b, dt_bias, a_log, d_skip, ssd_norm, q_norm, w_uq, kv_norm, w_ukv, attn_norm, w_out, norm_mlp, w_up, conv_ff_w, conv_ff_b, w_down, final_norm, loss_target, m_w_ada, m_b_ada, m_norm_mix, m_w_in, m_conv_w, m_conv_b, m_dt_bias, m_a_log, m_d_skip, m_ssd_norm, m_q_norm, m_w_uq, m_kv_norm, m_w_ukv, m_attn_norm, m_w_out, m_norm_mlp, m_w_up, m_conv_ff_w, m_conv_ff_b, m_w_down, m_final_norm, v_w_ada, v_b_ada, v_norm_mix, v_w_in, v_conv_w, v_conv_b, v_dt_bias, v_a_log, v_d_skip, v_ssd_norm, v_q_norm, v_w_uq, v_kv_norm, v_w_ukv, v_attn_norm, v_w_out, v_norm_mlp, v_w_up, v_conv_ff_w, v_conv_ff_b, v_w_down, v_final_norm):
    loc = locals()
    Wl = {k: loc[k] for k in WEIGHTS}
    Ml = {k: loc["m_" + k] for k in WEIGHTS}
    Vl = {k: loc["v_" + k] for k in WEIGHTS}
    xi, yi, ci = lax.axis_index("x"), lax.axis_index("y"), lax.axis_index("c")
    chip = 2 * xi + yi
    dev = 2 * chip + ci
    B, S, D = x.shape
    n_tok = N_DEV * B

    c_all = all_gather8("gather_c", c.reshape(8, -1)).reshape(n_tok, D)
    mod_cols = ada_fwd(c_all, w_ada)
    cols = mod_cols.shape[-1]
    half = n_tok // 2
    mod_mine = lax.dynamic_slice_in_dim(mod_cols, ci * half, half, axis=1)
    small_in = _flat_pack([mod_mine, conv_w, conv_ff_w])
    n_mod = mod_mine.size // PACK_W
    small_all = all_gather8("gather_mod", small_in).reshape(N_CHIPS, 2, -1, PACK_W)
    mod_all = small_all[:, :, :n_mod].reshape(N_CHIPS, 2, DEPTH, half, cols).transpose(2, 1, 3, 0, 4).reshape(DEPTH, n_tok, N_CHIPS * cols)
    mod_raw = lax.dynamic_slice_in_dim(mod_all, dev * B, B, axis=1).reshape(DEPTH, B, 6, D)
    conv_parts = [_flat_unpack(small_all[k, 0, n_mod:], [conv_w.shape, conv_ff_w.shape]) for k in range(N_CHIPS)]
    conv_full = {name: jnp.concatenate([conv_parts[k][i] for k in range(N_CHIPS)], axis=-1) for i, name in enumerate(CONV_SHARDED)}

    shard_shapes = {k: Wl[k].shape for k in BIG}
    rows = _big_rows(Wl)
    pack = _pack_rows([Wl[k].astype(MXU_DTYPE) for k in BIG], rows)
    gathered = lax.dynamic_update_slice_in_dim(gather_weights(pack), pack[None], chip, axis=0)
    W = unpack_gathered(gathered, shard_shapes)
    W.update({k: Wl[k] for k in REPLICATED})
    W.update(conv_full)

    loss_lanes, grad_x, grads, dmod_raw = local_step(x, loss_target, positions, mod_raw, W)
    loss = lax.psum(loss_lanes[0, 0], ("x", "y", "c"))

    gpack = pack_full_grads(grads, shard_shapes, rows)
    wire, own = chip_sum(gpack, swap_halves(gpack), ci, chip)
    mine = mesh_sum(own, scatter_chips(wire), chip)
    theirs = swap_with_sibling(mine)
    both = jnp.concatenate([jnp.where(ci == 0, mine, theirs), jnp.where(ci == 0, theirs, mine)], axis=0)
    g_shard = unpack_shards(both, shard_shapes)

    small_names = REPLICATED + CONV_SHARDED
    small_out = _flat_pack([grads[k] for k in small_names] + [dmod_raw])
    small_got = all_gather8("gather_small", small_out).reshape(N_DEV, -1, PACK_W)
    small_sum = _flat_unpack(sum_devices("sum_small", small_got), [grads[k].shape for k in small_names])
    G = dict(zip(small_names, small_sum))
    for name in CONV_SHARDED:
        width = Wl[name].shape[-1]
        G[name] = lax.dynamic_slice_in_dim(G[name], chip * width, width, axis=-1)
    n_small = sum(grads[k].size for k in small_names)
    dmod_all = small_got.reshape(N_DEV, -1)[:, n_small : n_small + dmod_raw.size].reshape(N_DEV, DEPTH, B, 6 * D)
    dmod_all = dmod_all.transpose(1, 0, 2, 3).reshape(DEPTH, n_tok, 6 * D)
    dmod_cols = lax.dynamic_slice_in_dim(dmod_all, chip * cols, cols, axis=-1)
    G["w_ada"] = jnp.stack([matmul(f"w_ada_wgrad{l}", c_all, dmod_cols[l], ta=True, a_act=_silu) for l in range(DEPTH)])
    G.update(g_shard)

    deltas, new_m, new_v = {}, {}, {}
    small_upd = adamw("adamw_small", *[_flat_pack([t[k] for k in REPLICATED]) for t in (Wl, G, Ml, Vl)])
    for res, t in zip((deltas, new_m, new_v), small_upd):
        res.update(zip(REPLICATED, _flat_unpack(t, [Wl[k].shape for k in REPLICATED])))
    for k in WEIGHTS:
        if k not in REPLICATED:
            deltas[k], new_m[k], new_v[k] = adamw("adamw_" + k, Wl[k], G[k], Ml[k], Vl[k])
    return (loss, grad_x, *[G[k] for k in WEIGHTS], *[deltas[k] for k in WEIGHTS], *[new_m[k] for k in WEIGHTS], *[new_v[k] for k in WEIGHTS])
```

```python
import functools
from typing import NamedTuple

import numpy as np
import jax
import jax.numpy as jnp
from jax import lax
from jax.experimental import pallas as pl
from jax.experimental.pallas import tpu as pltpu

F32 = jnp.float32
BF16 = jnp.bfloat16
MXU_DTYPE = jnp.bfloat16
WIRE_DTYPE = jnp.bfloat16
HIGHEST = lax.Precision.HIGHEST

D_MODEL = 1024
DEPTH = 2
D_SSD = 1024
SSD_HEADS = 16
SSD_HEAD_DIM = 64
SSD_GROUPS = 2
SSD_STATE = 128
SSD_CONV = 4
CHUNK = 128
MLA_HEADS = 8
QK_NOPE = 128
QK_ROPE = 64
V_DIM = 128
D_ATT = MLA_HEADS * V_DIM
Q_RANK = 384
KV_RANK = 256
ROPE_BASE = 10000.0
D_FF = 2816
FF_CONV = 3
EPS = 1e-6
D_XBC = D_SSD + 2 * SSD_GROUPS * SSD_STATE
D_IN = D_SSD + D_XBC + SSD_HEADS + Q_RANK + KV_RANK + QK_ROPE
ADAM_LR, ADAM_B1, ADAM_B2, ADAM_EPS, ADAM_WD, ADAM_STEP = 0.001, 0.9, 0.999, 1e-08, 0.01, 10

LANE = 128
N_CHIPS = 4
N_DEV = 8

OFF_Z, OFF_XBC, OFF_CKV, OFF_KR, OFF_DT, OFF_CQ = 0, 1024, 2560, 2816, 2944, 3072
N_PROJ = 3456
Q_HEAD_PAD = 256
VMEM_LIMIT = 56 * 1024 * 1024


def _cparams(n_axes):
    return pltpu.CompilerParams(dimension_semantics=("arbitrary",) * n_axes, vmem_limit_bytes=VMEM_LIMIT)


def _pick(n, prefs):
    for p in prefs:
        if n % p == 0:
            return p
    return n


def _silu(x):
    return x * jax.nn.sigmoid(x)


def _rms(x, g):
    return x * lax.rsqrt(jnp.mean(x * x, axis=-1, keepdims=True) + EPS) * g


def _softplus(x):
    return jnp.maximum(x, 0.0) + jnp.log(1.0 + jnp.exp(-jnp.abs(x)))


class Col(NamedTuple):
    arr: jax.Array
    width: int
    blk: int


def _as_col(a):
    return a if isinstance(a, Col) else Col(a, a.shape[-1], 0)


def _row_specs(tiled, pbatch, shared, tile):
    specs = [pl.BlockSpec((1, tile, t.width), lambda b, i, blk=t.blk: (b, i, blk)) for t in tiled]
    specs += [pl.BlockSpec((1, 1, p.shape[-1]), lambda b, i: (b, 0, 0)) for p in pbatch]
    specs += [pl.BlockSpec(s.shape, lambda b, i: (0, 0)) for s in shared]
    return specs


def _row_vals(refs, nt, npb):
    return [r[0].astype(F32) for r in refs[: nt + npb]] + [r[...].astype(F32) for r in refs[nt + npb :]]


def rowwise_fwd(name, fn, tiled, pbatch, shared, outs, tile=512):
    tiled = [_as_col(t) for t in tiled]
    B, S = tiled[0].arr.shape[:2]
    tile = min(tile, S)
    nt, npb, nsh = len(tiled), len(pbatch), len(shared)
    n_in = nt + npb + nsh

    def body(*refs):
        res = fn(*_row_vals(refs[:n_in], nt, npb))
        for r, v in zip(refs[n_in:], res):
            r[0] = v.astype(r.dtype)

    return pl.pallas_call(
        body,
        name=name,
        grid=(B, S // tile),
        in_specs=_row_specs(tiled, pbatch, shared, tile),
        out_specs=[pl.BlockSpec((1, tile, w), lambda b, i: (b, i, 0)) for w, _ in outs],
        out_shape=[jax.ShapeDtypeStruct((B, S, w), dt) for w, dt in outs],
        compiler_params=_cparams(2),
    )(*[t.arr for t in tiled], *pbatch, *shared)


def rowwise_bwd(name, fn, tiled, pbatch, shared, cts, grad_tiled, grad_shared, adds=None, tile=256, mxu_only=()):
    tiled = [_as_col(t) for t in tiled]
    adds = adds or {}
    B, S = tiled[0].arr.shape[:2]
    tile = min(tile, S)
    nt, npb, nsh = len(tiled), len(pbatch), len(shared)
    n_in = nt + npb + nsh
    add_idx = sorted(adds)
    gt = [i for i in range(nt) if grad_tiled[i]]
    gs = [i for i in range(nsh) if grad_shared[i]]
    diff = gt + [nt + i for i in range(npb)] + [nt + npb + i for i in gs]
    n_ct, n_add = len(cts), len(add_idx)

    def body(*refs):
        vals = _row_vals(refs[:n_in], nt, npb)
        ct_v = tuple(r[0].astype(F32) if c.ndim == 3 else jnp.sum(r[0].astype(F32), axis=0) for r, c in zip(refs[n_in : n_in + n_ct], cts))
        add_v = {i: r[0].astype(F32) for i, r in zip(add_idx, refs[n_in + n_ct : n_in + n_ct + n_add])}
        out_refs = refs[n_in + n_ct + n_add :]

        def f(*dargs):
            full = list(vals)
            for k, i in enumerate(diff):
                full[i] = dargs[k]
            return tuple(fn(*full))

        _, vjp = jax.vjp(f, *[vals[i] for i in diff])
        grads = vjp(ct_v)
        b, i = pl.program_id(0), pl.program_id(1)
        k = 0
        for idx in gt:
            g = grads[k]
            if idx in add_v:
                g = g + add_v[idx]
            out_refs[k][0] = g.astype(out_refs[k].dtype)
            k += 1
        for _ in range(npb):
            r, g = out_refs[k], grads[k]

            @pl.when(i == 0)
            def _(r=r, g=g):
                r[0] = g

            @pl.when(i > 0)
            def _(r=r, g=g):
                r[0] += g

            k += 1
        for _ in gs:
            r, g = out_refs[k], grads[k]

            @pl.when((i == 0) & (b == 0))
            def _(r=r, g=g):
                r[...] = g

            @pl.when((i > 0) | (b > 0))
            def _(r=r, g=g):
                r[...] += g

            k += 1

    out_specs = [pl.BlockSpec((1, tile, tiled[i].width), lambda b, i: (b, i, 0)) for i in gt]
    out_shape = [jax.ShapeDtypeStruct((B, S, tiled[i].width), MXU_DTYPE if i in mxu_only else F32) for i in gt]
    out_specs += [pl.BlockSpec((1, 1, p.shape[-1]), lambda b, i: (b, 0, 0)) for p in pbatch]
    out_shape += [jax.ShapeDtypeStruct(p.shape, F32) for p in pbatch]
    out_specs += [pl.BlockSpec(shared[i].shape, lambda b, i: (0, 0)) for i in gs]
    out_shape += [jax.ShapeDtypeStruct(shared[i].shape, F32) for i in gs]
    in_specs = _row_specs(tiled, pbatch, shared, tile)
    for c in cts:
        if c.ndim == 3:
            in_specs.append(pl.BlockSpec((1, tile, c.shape[-1]), lambda b, i: (b, i, 0)))
        else:
            in_specs.append(pl.BlockSpec((1, c.shape[1], tile, c.shape[-1]), lambda b, i: (b, 0, i, 0)))
    in_specs += [pl.BlockSpec((1, tile, adds[i].shape[-1]), lambda b, i: (b, i, 0)) for i in add_idx]
    return pl.pallas_call(
        body,
        name=name,
        grid=(B, S // tile),
        in_specs=in_specs,
        out_specs=out_specs,
        out_shape=out_shape,
        compiler_params=_cparams(2),
    )(*[t.arr for t in tiled], *pbatch, *shared, *cts, *[adds[i] for i in add_idx])


def fn_prenorm(x, sc, sh, g, bsc, bsh):
    return (_rms(x, g) * (1.0 + sc + bsc) + (sh + bsh),)


def fn_resid_prenorm(x, y, gate, sc, sh, g, bgate, bsc, bsh):
    x1 = x + (gate + bgate) * y
    return x1, _rms(x1, g) * (1.0 + sc + bsc) + (sh + bsh)


def fn_mix(yscan, z, o, ssd_norm, attn_norm):
    y = yscan * _silu(z)
    half = D_SSD // SSD_GROUPS
    first = lax.broadcasted_iota(jnp.int32, y.shape, 1) < half
    sq = y * y
    m0 = jnp.sum(jnp.where(first, sq, 0.0), axis=-1, keepdims=True) / half
    m1 = jnp.sum(jnp.where(first, 0.0, sq), axis=-1, keepdims=True) / half
    r = jnp.where(first, lax.rsqrt(m0 + EPS), lax.rsqrt(m1 + EPS))
    return (jnp.concatenate([y * r * ssd_norm, _rms(o, attn_norm)], axis=-1),)


def fn_mla_prep(cq, ckv, kr, cos_t, sin_t, q_norm, kv_norm, rot):
    krr = kr * cos_t + jnp.dot(kr, rot, precision=HIGHEST, preferred_element_type=F32) * sin_t
    return _rms(cq, q_norm), _rms(ckv, kv_norm), krr


def final_fwdbwd(x1, ff, target, gate, fnorm, bgate, tile=256):
    B, S, D = x1.shape
    tile = min(tile, S)

    def body(x_ref, f_ref, t_ref, g_ref, n_ref, bg_ref, loss_ref, dx_ref, df_ref, dg_ref, dn_ref, dbg_ref):
        b, i = pl.program_id(0), pl.program_id(1)
        tgt = t_ref[0]

        def f(x, y, gate, fn, bg):
            yf = _rms(x + (gate + bg) * y, fn)
            return 0.5 * jnp.sum(jnp.mean(jnp.square(yf - tgt), axis=-1, keepdims=True), axis=0, keepdims=True)

        val, vjp = jax.vjp(f, x_ref[0], f_ref[0], g_ref[0], n_ref[...], bg_ref[...])
        dx, dff, dg, dn, dbg = vjp(jnp.ones((1, 1), F32))
        dx_ref[0] = dx
        df_ref[0] = dff.astype(df_ref.dtype)
        lane_loss = jnp.broadcast_to(val, (1, LANE))

        @pl.when(i == 0)
        def _():
            dg_ref[0] = dg

        @pl.when(i > 0)
        def _():
            dg_ref[0] += dg

        @pl.when((i == 0) & (b == 0))
        def _():
            dn_ref[...] = dn
            dbg_ref[...] = dbg
            loss_ref[...] = lane_loss

        @pl.when((i > 0) | (b > 0))
        def _():
            dn_ref[...] += dn
            dbg_ref[...] += dbg
            loss_ref[...] += lane_loss

    tok = pl.BlockSpec((1, tile, D), lambda b, i: (b, i, 0))
    pb = pl.BlockSpec((1, 1, D), lambda b, i: (b, 0, 0))
    sh = pl.BlockSpec((1, D), lambda b, i: (0, 0))
    return pl.pallas_call(
        body,
        name="final_loss",
        grid=(B, S // tile),
        in_specs=[tok, tok, tok, pb, sh, sh],
        out_specs=[pl.BlockSpec((1, LANE), lambda b, i: (0, 0)), tok, tok, pb, sh, sh],
        out_shape=[
            jax.ShapeDtypeStruct((1, LANE), F32),
            jax.ShapeDtypeStruct((B, S, D), F32),
            jax.ShapeDtypeStruct((B, S, D), MXU_DTYPE),
            jax.ShapeDtypeStruct((B, 1, D), F32),
            jax.ShapeDtypeStruct((1, D), F32),
            jax.ShapeDtypeStruct((1, D), F32),
        ],
        compiler_params=_cparams(2),
    )(x1, ff, target, gate, fnorm, bgate)


MATMUL_VMEM_BUDGET = 40 * 1024 * 1024
STEP_COST_BYTES = 1.2e6
MXU_DIM = 256
MXU_FLOPS_PER_BYTE = 280.0


def _tile_options(n, cap):
    opts = [d for d in range(LANE, min(n, cap) + 1, LANE) if n % d == 0]
    return opts or [n]


def _matmul_tiles(M, N, K, sa, sb, so):
    best = None
    for tk in [K] + [d for d in _tile_options(K, 2048) if d >= 512 and d < K]:
        nk = K // tk
        for tm in _tile_options(M, 2048):
            for tn in _tile_options(N, 1408):
                acc = 0 if (so == 4 or nk == 1) else tm * tn * 4
                vmem = 2 * (tm * tk * sa + tk * tn * sb + tm * tn * so) + acc + tm * tn * 4 + 2 * (tm * tk + tk * tn)
                if vmem > MATMUL_VMEM_BUDGET:
                    continue
                a_reads = M * K * sa * (1 if nk == 1 else N // tn)
                b_reads = K * N * sb * (1 if (nk == 1 and N == tn) else M // tm)
                steps = (M // tm) * (N // tn) * nk
                hbm = a_reads + b_reads + M * N * so + (2 * M * N * 4 * (nk - 1) if nk > 1 else 0) / 8
                fill = (-(-tn // MXU_DIM) * MXU_DIM / tn) * (-(-tk // MXU_DIM) * MXU_DIM / tk)
                mxu = 2.0 * M * N * K * fill / MXU_FLOPS_PER_BYTE
                cost = max(hbm, mxu) + steps * STEP_COST_BYTES
                if best is None or cost < best[0]:
                    best = (cost, tm, tn, tk)
    assert best is not None, (M, N, K)
    return best[1:]


class Layer(NamedTuple):
    arr: jax.Array
    l: int


def matmul(name, a, b, ta=False, tb=False, out_dtype=F32, a_act=None):
    pieces = list(a) if isinstance(a, (list, tuple)) else [a]
    assert len(pieces) == 1 or not ta
    lead = pieces[0].shape[:2] if (pieces[0].ndim == 3 and not ta) else None
    pieces = [p.reshape(-1, p.shape[-1]) if p.ndim == 3 else p for p in pieces]
    layer = None
    if isinstance(b, Layer):
        b, layer = b
        b_rows, b_cols = b.shape[1:]
    else:
        if b.ndim == 3:
            b = b.reshape(-1, b.shape[-1])
        b_rows, b_cols = b.shape
    Kp, M = pieces[0].shape if ta else pieces[0].shape[::-1]
    N = b_rows if tb else b_cols
    assert (b_cols if tb else b_rows) == Kp * len(pieces), (name, pieces[0].shape, b.shape)
    tm, tn, tk = _matmul_tiles(M, N, Kp, pieces[0].dtype.itemsize, b.dtype.itemsize, jnp.dtype(out_dtype).itemsize)
    npk = Kp // tk
    nk = npk * len(pieces)
    dims = (((0 if ta else 1,), (1 if tb else 0,)), ((), ()))
    direct = jnp.dtype(out_dtype) == jnp.dtype(F32)

    def body(*refs):
        a_refs, b_ref, o_ref, scratch = refs[: len(pieces)], refs[len(pieces)], refs[len(pieces) + 1], refs[len(pieces) + 2 :]
        k = pl.program_id(2)

        def product(a_ref):
            av = a_ref[...]
            if a_act is not None:
                av = a_act(av.astype(F32))
            return lax.dot_general(av.astype(MXU_DTYPE), b_ref[...].astype(MXU_DTYPE), dims, preferred_element_type=F32)

        if nk == 1:
            o_ref[...] = product(a_refs[0]).astype(o_ref.dtype)
            return
        acc = o_ref if direct else scratch[0]

        @pl.when(k == 0)
        def _():
            acc[...] = product(a_refs[0])

        for p, a_ref in enumerate(a_refs):
            lo, hi = max(p * npk, 1), (p + 1) * npk
            if lo < hi:

                @pl.when((k >= lo) & (k < hi))
                def _(a_ref=a_ref):
                    acc[...] += product(a_ref)

        if not direct:

            @pl.when(k == nk - 1)
            def _():
                o_ref[...] = acc[...].astype(o_ref.dtype)

    def a_spec(p):
        kk = lambda k: jnp.clip(k - p * npk, 0, npk - 1)
        return pl.BlockSpec((tk, tm), lambda i, j, k: (kk(k), i)) if ta else pl.BlockSpec((tm, tk), lambda i, j, k: (i, kk(k)))

    b_block, b_index = ((tn, tk), lambda i, j, k: (j, k)) if tb else ((tk, tn), lambda i, j, k: (k, j))
    if layer is None:
        b_spec = pl.BlockSpec(b_block, b_index)
    else:
        b_spec = pl.BlockSpec((None, *b_block), lambda i, j, k: (layer, *b_index(i, j, k)))
    out = pl.pallas_call(
        body,
        name=name,
        grid=(M // tm, N // tn, nk),
        in_specs=[a_spec(p) for p in range(len(pieces))] + [b_spec],
        out_specs=pl.BlockSpec((tm, tn), lambda i, j, k: (i, j)),
        out_shape=jax.ShapeDtypeStruct((M, N), out_dtype),
        scratch_shapes=[] if (direct or nk == 1) else [pltpu.VMEM((tm, tn), F32)],
        compiler_params=_cparams(3),
    )(*pieces, b)
    return out.reshape(*lead, N) if lead is not None else out


SUBLANES = 8


def _shift_down(u, d):
    if d == 0:
        return u
    r = pltpu.roll(u, d, 0)
    t = lax.broadcasted_iota(jnp.int32, (SUBLANES, u.shape[1]), 0)
    return jnp.concatenate([jnp.where(t >= d, r[:SUBLANES], 0.0), r[SUBLANES:]], axis=0)


def _shift_up(u, d):
    if d == 0:
        return u
    s = u.shape[0]
    r = pltpu.roll(u, s - d, 0)
    t = lax.broadcasted_iota(jnp.int32, (SUBLANES, u.shape[1]), 0)
    return jnp.concatenate([r[: s - SUBLANES], jnp.where(t < SUBLANES - d, r[s - SUBLANES :], 0.0)], axis=0)


def _conv(u, w, b):
    k = w.shape[0]
    out = b + w[k - 1 : k, :] * u
    for j in range(k - 1):
        out = out + w[j : j + 1, :] * _shift_down(u, k - 1 - j)
    return out


def _conv_bwd(u, w, dc):
    k = w.shape[0]
    du = w[k - 1 : k, :] * dc
    dws = []
    for j in range(k - 1):
        du = du + w[j : j + 1, :] * _shift_up(dc, k - 1 - j)
        dws.append(jnp.sum(dc * _shift_down(u, k - 1 - j), axis=0, keepdims=True))
    dws.append(jnp.sum(dc * u, axis=0, keepdims=True))
    return du, jnp.concatenate(dws, axis=0), jnp.sum(dc, axis=0, keepdims=True)


def conv_silu_fwd(proj, w, b):
    B, S, _ = proj.shape
    k, c = w.shape
    blk0 = OFF_XBC // LANE

    def body(u_ref, w_ref, b_ref, o_ref):
        o_ref[0] = _silu(_conv(u_ref[0], w_ref[...], b_ref[...]))

    return pl.pallas_call(
        body,
        name="conv_silu_fwd",
        grid=(B, c // LANE),
        in_specs=[
            pl.BlockSpec((1, S, LANE), lambda bi, j: (bi, 0, blk0 + j)),
            pl.BlockSpec((k, LANE), lambda bi, j: (0, j)),
            pl.BlockSpec((1, LANE), lambda bi, j: (0, j)),
        ],
        out_specs=pl.BlockSpec((1, S, LANE), lambda bi, j: (bi, 0, j)),
        out_shape=jax.ShapeDtypeStruct((B, S, c), F32),
        compiler_params=_cparams(2),
    )(proj, w, b)


def conv_silu_bwd(proj, w, b, douts):
    B, S, _ = proj.shape
    k, c = w.shape
    blk0 = OFF_XBC // LANE
    ends = np.cumsum([d.shape[-1] // LANE for d in douts]).tolist()
    starts = [0] + ends[:-1]

    def body(u_ref, w_ref, b_ref, *refs):
        d_refs, (du_ref, dw_ref, db_ref) = refs[: len(douts)], refs[len(douts) :]
        j, bi = pl.program_id(0), pl.program_id(1)
        u, wv = u_ref[0], w_ref[...]
        cv = _conv(u, wv, b_ref[...])
        sg = jax.nn.sigmoid(cv)
        dout = d_refs[-1][0]
        for r in reversed(range(len(douts) - 1)):
            dout = jnp.where(j < ends[r], d_refs[r][0], dout)
        dc = dout * (sg * (1.0 + cv * (1.0 - sg)))
        du, dw, db = _conv_bwd(u, wv, dc)
        du_ref[0] = du.astype(du_ref.dtype)

        @pl.when(bi == 0)
        def _():
            dw_ref[...] = dw
            db_ref[...] = db

        @pl.when(bi > 0)
        def _():
            dw_ref[...] += dw
            db_ref[...] += db

    return pl.pallas_call(
        body,
        name="conv_silu_bwd",
        grid=(c // LANE, B),
        in_specs=[
            pl.BlockSpec((1, S, LANE), lambda j, bi: (bi, 0, blk0 + j)),
            pl.BlockSpec((k, LANE), lambda j, bi: (0, j)),
            pl.BlockSpec((1, LANE), lambda j, bi: (0, j)),
        ]
        + [
            pl.BlockSpec((1, S, LANE), lambda j, bi, lo=lo, hi=hi: (bi, 0, jnp.clip(j, lo, hi - 1) - lo))
            for lo, hi in zip(starts, ends)
        ],
        out_specs=[
            pl.BlockSpec((1, S, LANE), lambda j, bi: (bi, 0, j)),
            pl.BlockSpec((k, LANE), lambda j, bi: (0, j)),
            pl.BlockSpec((1, LANE), lambda j, bi: (0, j)),
        ],
        out_shape=[
            jax.ShapeDtypeStruct((B, S, c), MXU_DTYPE),
            jax.ShapeDtypeStruct((k, c), F32),
            jax.ShapeDtypeStruct((1, c), F32),
        ],
        compiler_params=_cparams(2),
    )(proj, w, b, *douts)


def _glu_specs(S, k, nb, batch_col):
    specs = []
    for off in (0, nb):
        specs.append(pl.BlockSpec((1, S, LANE), lambda *ids, off=off: (batch_col(*ids)[0], 0, off + batch_col(*ids)[1])))
    for rows in (k, 1):
        for off in (0, nb):
            specs.append(pl.BlockSpec((rows, LANE), lambda *ids, off=off: (0, off + batch_col(*ids)[1])))
    return specs


def conv_glu_fwd(u, w, b):
    B, S, c2 = u.shape
    k = w.shape[0]
    nb = c2 // 2 // LANE

    def body(ug_ref, uv_ref, wg_ref, wv_ref, bg_ref, bv_ref, o_ref):
        gate = _conv(ug_ref[0], wg_ref[...], bg_ref[...])
        val = _conv(uv_ref[0], wv_ref[...], bv_ref[...])
        o_ref[0] = (_silu(gate) * val).astype(o_ref.dtype)

    return pl.pallas_call(
        body,
        name="conv_glu_fwd",
        grid=(B, nb),
        in_specs=_glu_specs(S, k, nb, lambda bi, j: (bi, j)),
        out_specs=pl.BlockSpec((1, S, LANE), lambda bi, j: (bi, 0, j)),
        out_shape=jax.ShapeDtypeStruct((B, S, c2 // 2), MXU_DTYPE),
        compiler_params=_cparams(2),
    )(u, u, w, w, b, b)


def conv_glu_bwd(u, w, b, da):
    B, S, c2 = u.shape
    k = w.shape[0]
    nb = c2 // 2 // LANE

    def body(ug_ref, uv_ref, wg_ref, wv_ref, bg_ref, bv_ref, d_ref, dug_ref, duv_ref, dwg_ref, dwv_ref, dbg_ref, dbv_ref):
        bi = pl.program_id(1)
        ug, uv, wg, wv = ug_ref[0], uv_ref[0], wg_ref[...], wv_ref[...]
        gate, val = _conv(ug, wg, bg_ref[...]), _conv(uv, wv, bv_ref[...])
        sg = jax.nn.sigmoid(gate)
        dav = d_ref[0].astype(F32)
        dug, dwg, dbg = _conv_bwd(ug, wg, dav * val * (sg * (1.0 + gate * (1.0 - sg))))
        duv, dwv, dbv = _conv_bwd(uv, wv, dav * gate * sg)
        dug_ref[0] = dug.astype(dug_ref.dtype)
        duv_ref[0] = duv.astype(duv_ref.dtype)

        @pl.when(bi == 0)
        def _():
            dwg_ref[...], dwv_ref[...], dbg_ref[...], dbv_ref[...] = dwg, dwv, dbg, dbv

        @pl.when(bi > 0)
        def _():
            dwg_ref[...] += dwg
            dwv_ref[...] += dwv
            dbg_ref[...] += dbg
            dbv_ref[...] += dbv

    tok = pl.BlockSpec((1, S, LANE), lambda j, bi: (bi, 0, j))
    outs = pl.pallas_call(
        body,
        name="conv_glu_bwd",
        grid=(nb, B),
        in_specs=_glu_specs(S, k, nb, lambda j, bi: (bi, j)) + [tok],
        out_specs=[tok, tok] + [pl.BlockSpec((rows, LANE), lambda j, bi: (0, j)) for rows in (k, k, 1, 1)],
        out_shape=[jax.ShapeDtypeStruct((B, S, c2 // 2), MXU_DTYPE)] * 2
        + [jax.ShapeDtypeStruct((rows, c2 // 2), F32) for rows in (k, k, 1, 1)],
        compiler_params=_cparams(2),
    )(u, u, w, w, b, b, da)
    dug, duv, dwg, dwv, dbg, dbv = outs
    return dug, duv, jnp.concatenate([dwg, dwv], axis=1), jnp.concatenate([dbg, dbv], axis=1)


N_PAIR = SSD_HEADS // SSD_GROUPS // 2


def _ssd_chunk(prev, xs, bms, cms, dtraw, dt_bias, a_log, d_skip):
    L = CHUNK
    lane = lax.broadcasted_iota(jnp.int32, (1, LANE), 1)
    sub = lax.broadcasted_iota(jnp.int32, (LANE, 1), 0)
    row = lax.broadcasted_iota(jnp.int32, (L, L), 0)
    col = lax.broadcasted_iota(jnp.int32, (L, L), 1)
    tri = (row >= col).astype(F32)
    first = lane < SSD_HEAD_DIM

    dt = _softplus(dtraw + dt_bias)
    a = dt * (-jnp.exp(a_log))
    acs = jnp.dot(tri, a, precision=HIGHEST, preferred_element_type=F32)
    acs_t = acs.T
    a_end = jnp.sum(a, axis=0, keepdims=True)

    def lane_of(v, h):
        return jnp.sum(jnp.where(lane == h, v, 0.0), axis=1, keepdims=True)

    def expand(v, ha):
        return jnp.where(first, lane_of(v, ha), lane_of(v, ha + 1))

    ys, news = [], []
    for g in range(SSD_GROUPS):
        bm, cm = bms[g].astype(MXU_DTYPE), cms[g].astype(MXU_DTYPE)
        cb = lax.dot_general(cm, bm, (((1,), (1,)), ((), ())), preferred_element_type=F32)
        for j in range(N_PAIR):
            ha = g * (SSD_HEADS // SSD_GROUPS) + 2 * j
            x, state = xs[g * N_PAIR + j], prev[g * N_PAIR + j]
            dt_e, acs_e, end_e = expand(dt, ha), expand(acs, ha), expand(a_end, ha)
            xdt = x * dt_e
            y = jnp.dot(cm, state.astype(MXU_DTYPE), preferred_element_type=F32) * jnp.exp(acs_e)
            st = lax.dot_general(bm, (xdt * jnp.exp(end_e - acs_e)).astype(MXU_DTYPE), (((0,), (0,)), ((), ())), preferred_element_type=F32)
            news.append(state * jnp.exp(end_e) + st)
            for hh in range(2):
                h = ha + hh
                seg = lane_of(acs, h) - jnp.sum(jnp.where(sub == h, acs_t, 0.0), axis=0, keepdims=True)
                decay = jnp.exp(jnp.where(row >= col, seg, -jnp.inf))
                xh = jnp.where(first if hh == 0 else jnp.logical_not(first), xdt, 0.0)
                y = y + jnp.dot((cb * decay).astype(MXU_DTYPE), xh.astype(MXU_DTYPE), preferred_element_type=F32)
            ys.append(y + x * expand(d_skip, ha))
    return ys, news


N_PAIRS = SSD_GROUPS * N_PAIR


def _ssd_specs(chunk_of):
    bc = SSD_GROUPS * SSD_STATE
    par = pl.BlockSpec((1, LANE), lambda *ids: (0, 0))
    return [
        pl.BlockSpec((1, CHUNK, D_SSD), lambda *ids: (*chunk_of(*ids), 0)),
        pl.BlockSpec((1, CHUNK, bc), lambda *ids: (*chunk_of(*ids), D_SSD // bc)),
        pl.BlockSpec((1, CHUNK, bc), lambda *ids: (*chunk_of(*ids), D_SSD // bc + 1)),
        pl.BlockSpec((1, CHUNK, LANE), lambda *ids: (*chunk_of(*ids), OFF_DT // LANE)),
        par,
        par,
        par,
    ]


def _pair_cols(ref):
    return [ref[0, :, k * LANE : (k + 1) * LANE] for k in range(N_PAIRS)]


def _group_cols(ref):
    return [ref[0, :, g * SSD_STATE : (g + 1) * SSD_STATE] for g in range(SSD_GROUPS)]


def ssd_fwd(xbc_act, proj, dt_bias, a_log, d_skip):
    B, S, _ = xbc_act.shape
    nc = S // CHUNK

    def body(xs_ref, bm_ref, cm_ref, dt_ref, db_ref, al_ref, ds_ref, y_ref, st_ref, state):
        @pl.when(pl.program_id(1) == 0)
        def _():
            state[...] = jnp.zeros_like(state)

        prev = [state[k] for k in range(N_PAIRS)]
        for k in range(N_PAIRS):
            st_ref[0, 0, k] = prev[k]
        ys, news = _ssd_chunk(prev, _pair_cols(xs_ref), _group_cols(bm_ref), _group_cols(cm_ref), dt_ref[0], db_ref[...], al_ref[...], ds_ref[...])
        for k in range(N_PAIRS):
            y_ref[0, :, k * LANE : (k + 1) * LANE] = ys[k]
            state[k] = news[k]

    return pl.pallas_call(
        body,
        name="ssd_fwd",
        grid=(B, nc),
        in_specs=_ssd_specs(lambda b, c: (b, c)),
        out_specs=[
            pl.BlockSpec((1, CHUNK, D_SSD), lambda b, c: (b, c, 0)),
            pl.BlockSpec((1, 1, N_PAIRS, SSD_STATE, LANE), lambda b, c: (b, c, 0, 0, 0)),
        ],
        out_shape=[
            jax.ShapeDtypeStruct((B, S, D_SSD), F32),
            jax.ShapeDtypeStruct((B, nc, N_PAIRS, SSD_STATE, LANE), F32),
        ],
        scratch_shapes=[pltpu.VMEM((N_PAIRS, SSD_STATE, LANE), F32)],
        compiler_params=_cparams(2),
    )(xbc_act, xbc_act, xbc_act, proj, dt_bias, a_log, d_skip)


def ssd_bwd(xbc_act, proj, dt_bias, a_log, d_skip, states, dy):
    B, S, _ = xbc_act.shape
    nc = S // CHUNK
    bc = SSD_GROUPS * SSD_STATE
    chunk_of = lambda b, cr: (b, nc - 1 - cr)

    def body(xs_ref, bm_ref, cm_ref, dt_ref, db_ref, al_ref, ds_ref, st_ref, dy_ref,
             dxs_ref, dbm_ref, dcm_ref, ddt_ref, ddb_ref, dal_ref, dds_ref, dstate):
        b, cr = pl.program_id(0), pl.program_id(1)

        @pl.when(cr == 0)
        def _():
            dstate[...] = jnp.zeros_like(dstate)

        prev = [st_ref[0, 0, k] for k in range(N_PAIRS)]
        _, vjp = jax.vjp(
            _ssd_chunk, prev, _pair_cols(xs_ref), _group_cols(bm_ref), _group_cols(cm_ref), dt_ref[0], db_ref[...], al_ref[...], ds_ref[...]
        )
        dprev, dxs, dbm, dcm, ddt, ddb, dal, dds = vjp((_pair_cols(dy_ref), [dstate[k] for k in range(N_PAIRS)]))
        for k in range(N_PAIRS):
            dstate[k] = dprev[k]
            dxs_ref[0, :, k * LANE : (k + 1) * LANE] = dxs[k]
        for g in range(SSD_GROUPS):
            dbm_ref[0, :, g * SSD_STATE : (g + 1) * SSD_STATE] = dbm[g]
            dcm_ref[0, :, g * SSD_STATE : (g + 1) * SSD_STATE] = dcm[g]
        ddt_ref[0] = ddt
        first = (b == 0) & (cr == 0)

        @pl.when(first)
        def _():
            ddb_ref[...] = ddb
            dal_ref[...] = dal
            dds_ref[...] = dds

        @pl.when(jnp.logical_not(first))
        def _():
            ddb_ref[...] += ddb
            dal_ref[...] += dal
            dds_ref[...] += dds

    par = pl.BlockSpec((1, LANE), lambda *ids: (0, 0))
    in_specs = _ssd_specs(chunk_of) + [
        pl.BlockSpec((1, 1, N_PAIRS, SSD_STATE, LANE), lambda b, cr: (*chunk_of(b, cr), 0, 0, 0)),
        pl.BlockSpec((1, CHUNK, D_SSD), lambda b, cr: (*chunk_of(b, cr), 0)),
    ]
    return pl.pallas_call(
        body,
        name="ssd_bwd",
        grid=(B, nc),
        in_specs=in_specs,
        out_specs=[
            pl.BlockSpec((1, CHUNK, D_SSD), lambda b, cr: (*chunk_of(b, cr), 0)),
            pl.BlockSpec((1, CHUNK, bc), lambda b, cr: (*chunk_of(b, cr), 0)),
            pl.BlockSpec((1, CHUNK, bc), lambda b, cr: (*chunk_of(b, cr), 0)),
            pl.BlockSpec((1, CHUNK, LANE), lambda b, cr: (*chunk_of(b, cr), 0)),
            par,
            par,
            par,
        ],
        out_shape=[
            jax.ShapeDtypeStruct((B, S, D_SSD), F32),
            jax.ShapeDtypeStruct((B, S, bc), F32),
            jax.ShapeDtypeStruct((B, S, bc), F32),
            jax.ShapeDtypeStruct((B, S, LANE), F32),
            jax.ShapeDtypeStruct((1, LANE), F32),
            jax.ShapeDtypeStruct((1, LANE), F32),
            jax.ShapeDtypeStruct((1, LANE), F32),
        ],
        scratch_shapes=[pltpu.VMEM((N_PAIRS, SSD_STATE, LANE), F32)],
        compiler_params=_cparams(2),
    )(xbc_act, xbc_act, xbc_act, proj, dt_bias, a_log, d_skip, states, dy)


ATT_SCALE = (QK_NOPE + QK_ROPE) ** -0.5
ATT_TILE = 512


ATT_HEADS_FWD = 4
ATT_HEADS_BWD = 2
LOG2E = 1.4426950408889634
Q_PRESCALE = ATT_SCALE * LOG2E


def fn_rope_q(q, cos_t, sin_t, rot):
    parts = []
    for h in range(MLA_HEADS):
        qr = q[:, h * Q_HEAD_PAD + LANE : (h + 1) * Q_HEAD_PAD]
        qr = qr * cos_t + jnp.dot(qr, rot, precision=HIGHEST, preferred_element_type=F32) * sin_t
        parts += [q[:, h * Q_HEAD_PAD : h * Q_HEAD_PAD + LANE] * Q_PRESCALE, qr * Q_PRESCALE]
    return (jnp.concatenate(parts, axis=1),)


def _head_cols(ref, hh, width):
    return ref[0, :, hh * width : (hh + 1) * width]


def _causal_mask(s):
    return jnp.where(lax.broadcasted_iota(jnp.int32, s.shape, 0) >= lax.broadcasted_iota(jnp.int32, s.shape, 1), s, -jnp.inf)


def _nt(a, b):
    return lax.dot_general(a, b, (((1,), (1,)), ((), ())), preferred_element_type=F32)


def _tn(a, b):
    return lax.dot_general(a, b, (((0,), (0,)), ((), ())), preferred_element_type=F32)


def _per_tile(idx, n, variant):
    for v in range(n):

        @pl.when(idx == v)
        def _(v=v):
            variant(v)


def attention_fwd(qs, kv, krr):
    B, S, _ = qs.shape
    t = min(ATT_TILE, S)
    n = S // t
    hps = ATT_HEADS_FWD

    def body(q_ref, k_ref, kr_ref, o_ref, lse_ref):
        def variant(v):
            spans = ([(0, v * t, False)] if v else []) + [(v * t, (v + 1) * t, True)]
            for hh in range(hps):
                q = _head_cols(q_ref, hh, Q_HEAD_PAD)
                c0 = hh * 2 * LANE
                ss = []
                for a, b, diag in spans:
                    s = _nt(q, jnp.concatenate([k_ref[0, a:b, c0 : c0 + LANE], kr_ref[0, a:b]], axis=1))
                    ss.append(_causal_mask(s) if diag else s)
                m = functools.reduce(jnp.maximum, [jnp.max(s, axis=1, keepdims=True) for s in ss])
                ps = [jnp.exp2(s - m) for s in ss]
                l = functools.reduce(jnp.add, [jnp.sum(p, axis=1, keepdims=True) for p in ps])
                acc = functools.reduce(
                    jnp.add,
                    [jnp.dot(p.astype(MXU_DTYPE), k_ref[0, a:b, c0 + LANE : c0 + 2 * LANE], preferred_element_type=F32) for p, (a, b, _) in zip(ps, spans)],
                )
                o_ref[0, :, hh * V_DIM : (hh + 1) * V_DIM] = acc / l
                lse_ref[0, hh] = m + jnp.log2(l)

        _per_tile(pl.program_id(2), n, variant)

    return pl.pallas_call(
        body,
        name="attention_fwd",
        grid=(B, MLA_HEADS // hps, n),
        in_specs=[
            pl.BlockSpec((1, t, hps * Q_HEAD_PAD), lambda b, h, i: (b, i, h)),
            pl.BlockSpec((1, S, hps * 2 * LANE), lambda b, h, i: (b, 0, h)),
            pl.BlockSpec((1, S, LANE), lambda b, h, i: (b, 0, 0)),
        ],
        out_specs=[
            pl.BlockSpec((1, t, hps * V_DIM), lambda b, h, i: (b, i, h)),
            pl.BlockSpec((1, hps, t, 1), lambda b, h, i: (b, h, i, 0)),
        ],
        out_shape=[jax.ShapeDtypeStruct((B, S, D_ATT), F32), jax.ShapeDtypeStruct((B, MLA_HEADS, S, 1), F32)],
        compiler_params=_cparams(3),
    )(qs, kv, krr)


def attention_bwd(qs, kv, krr, cos_t, sin_t, rot_t, o, lse, do):
    B, S, _ = qs.shape
    t = min(ATT_TILE, S)
    n = S // t
    hps = ATT_HEADS_BWD

    def body(q_ref, c_ref, s_ref, rott_ref, k_ref, kr_ref, o_ref, lse_ref, do_ref, dq_ref, dkv_ref, dkr_ref, dk_acc, dv_acc):
        i = pl.program_id(2)

        @pl.when(i == 0)
        def _():
            dk_acc[...] = jnp.zeros_like(dk_acc)
            dv_acc[...] = jnp.zeros_like(dv_acc)

        def variant(v):
            spans = ([(0, v * t, False)] if v else []) + [(v * t, (v + 1) * t, True)]
            for hh in range(hps):
                q = _head_cols(q_ref, hh, Q_HEAD_PAD)
                dov = _head_cols(do_ref, hh, V_DIM)
                delta = jnp.sum(_head_cols(o_ref, hh, V_DIM) * dov, axis=1, keepdims=True)
                dob = dov.astype(MXU_DTYPE)
                c0 = hh * 2 * LANE
                acc = None
                for a, b, diag in spans:
                    k = jnp.concatenate([k_ref[0, a:b, c0 : c0 + LANE], kr_ref[0, a:b]], axis=1)
                    s = _nt(q, k)
                    p = jnp.exp2((_causal_mask(s) if diag else s) - lse_ref[0, hh])
                    ds = (p * (_nt(dob, k_ref[0, a:b, c0 + LANE : c0 + 2 * LANE]) - delta) * ATT_SCALE).astype(MXU_DTYPE)
                    part = jnp.dot(ds, k, preferred_element_type=F32)
                    acc = part if acc is None else acc + part
                    dk_acc[hh, a:b] += _tn(ds, q)
                    dv_acc[hh, a:b] += _tn(p.astype(MXU_DTYPE), dob)
                r = acc[:, LANE:]
                dr = r * c_ref[0] + jnp.dot(r * s_ref[0], rott_ref[...], precision=HIGHEST, preferred_element_type=F32)
                dq_ref[0, :, hh * Q_HEAD_PAD : (hh + 1) * Q_HEAD_PAD] = jnp.concatenate([acc[:, :LANE], dr], axis=1).astype(dq_ref.dtype)

        _per_tile(i, n, variant)

        @pl.when(i == n - 1)
        def _():
            dkr = None
            for hh in range(hps):
                dk = dk_acc[hh] * (1.0 / Q_PRESCALE)
                dkv_ref[0, :, hh * 2 * LANE : (hh + 1) * 2 * LANE] = jnp.concatenate([dk[:, :LANE], dv_acc[hh]], axis=1).astype(dkv_ref.dtype)
                dkr = dk[:, LANE:] if dkr is None else dkr + dk[:, LANE:]
            dkr_ref[0, 0] = dkr

    return pl.pallas_call(
        body,
        name="attention_bwd",
        grid=(B, MLA_HEADS // hps, n),
        in_specs=[
            pl.BlockSpec((1, t, hps * Q_HEAD_PAD), lambda b, h, i: (b, i, h)),
            pl.BlockSpec((1, t, LANE), lambda b, h, i: (b, i, 0)),
            pl.BlockSpec((1, t, LANE), lambda b, h, i: (b, i, 0)),
            pl.BlockSpec((LANE, LANE), lambda b, h, i: (0, 0)),
            pl.BlockSpec((1, S, hps * 2 * LANE), lambda b, h, i: (b, 0, h)),
            pl.BlockSpec((1, S, LANE), lambda b, h, i: (b, 0, 0)),
            pl.BlockSpec((1, t, hps * V_DIM), lambda b, h, i: (b, i, h)),
            pl.BlockSpec((1, hps, t, 1), lambda b, h, i: (b, h, i, 0)),
            pl.BlockSpec((1, t, hps * V_DIM), lambda b, h, i: (b, i, h)),
        ],
        out_specs=[
            pl.BlockSpec((1, t, hps * Q_HEAD_PAD), lambda b, h, i: (b, i, h)),
            pl.BlockSpec((1, S, hps * 2 * LANE), lambda b, h, i: (b, 0, h)),
            pl.BlockSpec((1, 1, S, LANE), lambda b, h, i: (b, h, 0, 0)),
        ],
        out_shape=[
            jax.ShapeDtypeStruct(qs.shape, MXU_DTYPE),
            jax.ShapeDtypeStruct(kv.shape, MXU_DTYPE),
            jax.ShapeDtypeStruct((B, MLA_HEADS // hps, S, LANE), F32),
        ],
        scratch_shapes=[pltpu.VMEM((hps, S, Q_HEAD_PAD), F32), pltpu.VMEM((hps, S, V_DIM), F32)],
        compiler_params=_cparams(3),
    )(qs, cos_t, sin_t, rot_t, kv, krr, o, lse, do)


_IN_SPLITS = np.cumsum([D_SSD, D_XBC, SSD_HEADS, Q_RANK, KV_RANK]).tolist()


def _pad_last(t, n):
    return jnp.pad(t, [(0, 0)] * (t.ndim - 1) + [(0, n - t.shape[-1])])


def win_to_kernel(w):
    z, xbc, dt, cq, ckv, kr = jnp.split(w, _IN_SPLITS, axis=-1)
    return jnp.concatenate([z, xbc, ckv, _pad_last(kr, LANE), _pad_last(dt, LANE), cq], axis=-1)


def win_from_kernel(g):
    z, xbc = g[..., :OFF_XBC], g[..., OFF_XBC:OFF_CKV]
    ckv, kr = g[..., OFF_CKV:OFF_KR], g[..., OFF_KR : OFF_KR + QK_ROPE]
    dt, cq = g[..., OFF_DT : OFF_DT + SSD_HEADS], g[..., OFF_CQ:]
    return jnp.concatenate([z, xbc, dt, cq, ckv, kr], axis=-1)


def wuq_to_kernel(w):
    w = w.reshape(*w.shape[:-1], MLA_HEADS, QK_NOPE + QK_ROPE)
    return _pad_last(w, Q_HEAD_PAD).reshape(*w.shape[:-2], MLA_HEADS * Q_HEAD_PAD)


def wuq_from_kernel(g):
    g = g.reshape(*g.shape[:-1], MLA_HEADS, Q_HEAD_PAD)[..., : QK_NOPE + QK_ROPE]
    return g.reshape(*g.shape[:-2], MLA_HEADS * (QK_NOPE + QK_ROPE))


def _lane_pad_row(v):
    return _pad_last(v, LANE)[None, :]


def rope_tables(positions):
    inv_freq = jnp.asarray(1.0 / (ROPE_BASE ** (np.arange(0, QK_ROPE, 2, dtype=np.float32) / QK_ROPE)))
    ang = positions.astype(F32)[..., None] * inv_freq
    cos, sin = jnp.cos(ang), jnp.sin(ang)
    zeros = jnp.zeros(cos.shape[:-1] + (LANE - QK_ROPE,), F32)
    rot = np.zeros((LANE, LANE), np.float32)
    half = QK_ROPE // 2
    for j in range(half):
        rot[j + half, j] = -1.0
        rot[j, j + half] = 1.0
    return jnp.concatenate([cos, cos, zeros], -1), jnp.concatenate([sin, sin, zeros], -1), jnp.asarray(rot), jnp.asarray(rot.T)


BIG = ("w_in", "w_uq", "w_ukv", "w_out", "w_up", "w_down")


def local_step(x, target, positions, mod_raw, W):
    B, S, D = x.shape
    cos_t, sin_t, rot, rot_t = rope_tables(positions)
    row = lambda v: v.reshape(1, -1)
    w_in_k, w_uq_k = win_to_kernel(W["w_in"]), wuq_to_kernel(W["w_uq"])
    Ls = []
    for l in range(DEPTH):
        mods = [mod_raw[l, :, k][:, None, :] for k in range(6)]
        bias = [row(W["b_ada"][l, k * D : (k + 1) * D]) for k in range(6)]
        Ls.append(
            dict(
                mods=mods,
                bias=bias,
                w_in=Layer(w_in_k, l),
                w_uq=Layer(w_uq_k, l),
                w_ukv=Layer(W["w_ukv"], l),
                w_out=Layer(W["w_out"], l),
                w_up=Layer(W["w_up"], l),
                w_down=Layer(W["w_down"], l),
                conv_w=W["conv_w"][l],
                conv_b=row(W["conv_b"][l]),
                conv_ff_w=W["conv_ff_w"][l],
                conv_ff_b=row(W["conv_ff_b"][l]),
                dt_bias=_lane_pad_row(W["dt_bias"][l]),
                a_log=_lane_pad_row(W["a_log"][l]),
                d_skip=_lane_pad_row(W["d_skip"][l]),
                norm_mix=row(W["norm_mix"][l]),
                ssd_norm=row(W["ssd_norm"][l]),
                q_norm=row(W["q_norm"][l]),
                kv_norm=row(W["kv_norm"][l]),
                attn_norm=row(W["attn_norm"][l]),
                norm_mlp=row(W["norm_mlp"][l]),
            )
        )
    fnorm = row(W["final_norm"])

    p0 = Ls[0]
    (h1,) = rowwise_fwd(
        "prenorm_fwd", fn_prenorm, [x], [p0["mods"][1], p0["mods"][0]], [p0["norm_mix"], p0["bias"][1], p0["bias"][0]], [(D, MXU_DTYPE)]
    )
    xin = x
    for l, p in enumerate(Ls):
        s = p["saved"] = dict(xin=xin, h1=h1)
        s["proj"] = proj = matmul(f"w_in_fwd{l}", h1, p["w_in"])
        s["xbc_act"] = xbc_act = conv_silu_fwd(proj, p["conv_w"], p["conv_b"])
        s["yscan"], s["states"] = ssd_fwd(xbc_act, proj, p["dt_bias"], p["a_log"], p["d_skip"])
        mla_in = [Col(proj, Q_RANK, OFF_CQ // Q_RANK), Col(proj, KV_RANK, OFF_CKV // KV_RANK), Col(proj, LANE, OFF_KR // LANE), cos_t, sin_t]
        s["cqn"], s["ckvn"], s["krr"] = rowwise_fwd(
            f"mla_prep_fwd{l}", fn_mla_prep, mla_in, [], [p["q_norm"], p["kv_norm"], rot], [(Q_RANK, MXU_DTYPE), (KV_RANK, MXU_DTYPE), (LANE, MXU_DTYPE)]
        )
        q = matmul(f"w_uq_fwd{l}", s["cqn"], p["w_uq"], out_dtype=MXU_DTYPE)
        (s["qs"],) = rowwise_fwd(f"rope_q_fwd{l}", fn_rope_q, [q, cos_t, sin_t], [], [rot], [(q.shape[-1], MXU_DTYPE)])
        s["kv"] = matmul(f"w_ukv_fwd{l}", s["ckvn"], p["w_ukv"], out_dtype=MXU_DTYPE)
        s["o"], s["lse"] = attention_fwd(s["qs"], s["kv"], s["krr"])
        (s["ycat"],) = rowwise_fwd(
            f"mix_fwd{l}", fn_mix, [s["yscan"], Col(proj, D_SSD, 0), s["o"]], [], [p["ssd_norm"], p["attn_norm"]], [(D_SSD + D_ATT, MXU_DTYPE)]
        )
        s["ymix"] = matmul(f"w_out_fwd{l}", s["ycat"], p["w_out"])
        s["x1"], s["h2"] = rowwise_fwd(
            f"mid_fwd{l}",
            fn_resid_prenorm,
            [xin, s["ymix"]],
            [p["mods"][2], p["mods"][4], p["mods"][3]],
            [p["norm_mlp"], p["bias"][2], p["bias"][4], p["bias"][3]],
            [(D, F32), (D, MXU_DTYPE)],
        )
        s["u"] = matmul(f"w_up_fwd{l}", s["h2"], p["w_up"])
        s["a"] = conv_glu_fwd(s["u"], p["conv_ff_w"], p["conv_ff_b"])
        s["ff"] = matmul(f"w_down_fwd{l}", s["a"], p["w_down"])
        if l + 1 < DEPTH:
            n = Ls[l + 1]
            xin, h1 = rowwise_fwd(
                f"join_fwd{l}",
                fn_resid_prenorm,
                [s["x1"], s["ff"]],
                [p["mods"][5], n["mods"][1], n["mods"][0]],
                [n["norm_mix"], p["bias"][5], n["bias"][1], n["bias"][0]],
                [(D, F32), (D, MXU_DTYPE)],
            )

    G = {k: [None] * DEPTH for k in W if k not in ("final_norm", "w_ada")}
    dmod = [[None] * 6 for _ in range(DEPTH)]
    dbias = [[None] * 6 for _ in range(DEPTH)]
    last = Ls[-1]
    sl = last["saved"]
    loss, dx1, dff, dmod[-1][5], G["final_norm"], dbias[-1][5] = final_fwdbwd(
        sl["x1"], sl["ff"], target, last["mods"][5], fnorm, last["bias"][5]
    )
    grad_x = None
    for l in reversed(range(DEPTH)):
        p = Ls[l]
        s = p["saved"]
        da = matmul(f"w_down_dgrad{l}", dff, p["w_down"], tb=True)
        G["w_down"][l] = matmul(f"w_down_wgrad{l}", s["a"], dff, ta=True)
        du_gate, du_val, G["conv_ff_w"][l], dcb = conv_glu_bwd(s["u"], p["conv_ff_w"], p["conv_ff_b"], da)
        G["conv_ff_b"][l] = dcb[0]
        dh2 = matmul(f"w_up_dgrad{l}", [du_gate, du_val], p["w_up"], tb=True)
        G["w_up"][l] = jnp.concatenate(
            [matmul(f"w_up_wgrad{l}_{half}", s["h2"], d, ta=True) for half, d in (("gate", du_gate), ("val", du_val))], axis=1
        )
        dxb, dymix, dmod[l][2], dmod[l][4], dmod[l][3], G["norm_mlp"][l], dbias[l][2], dbias[l][4], dbias[l][3] = rowwise_bwd(
            f"mid_bwd{l}",
            fn_resid_prenorm,
            [s["xin"], s["ymix"]],
            [p["mods"][2], p["mods"][4], p["mods"][3]],
            [p["norm_mlp"], p["bias"][2], p["bias"][4], p["bias"][3]],
            [dx1, dh2],
            [True, True],
            [True] * 4,
            mxu_only=(1,),
        )
        dycat = matmul(f"w_out_dgrad{l}", dymix, p["w_out"], tb=True)
        G["w_out"][l] = matmul(f"w_out_wgrad{l}", s["ycat"], dymix, ta=True)
        dyscan, dz, do, G["ssd_norm"][l], G["attn_norm"][l] = rowwise_bwd(
            f"mix_bwd{l}", fn_mix, [s["yscan"], Col(s["proj"], D_SSD, 0), s["o"]], [], [p["ssd_norm"], p["attn_norm"]], [dycat], [True] * 3, [True] * 2, mxu_only=(1,)
        )
        dq, dkv, dkrr = attention_bwd(s["qs"], s["kv"], s["krr"], cos_t, sin_t, rot_t, s["o"], s["lse"], do)
        dcqn = matmul(f"w_uq_dgrad{l}", dq, p["w_uq"], tb=True)
        G["w_uq"][l] = wuq_from_kernel(matmul(f"w_uq_wgrad{l}", s["cqn"], dq, ta=True))
        dckvn = matmul(f"w_ukv_dgrad{l}", dkv, p["w_ukv"], tb=True)
        G["w_ukv"][l] = matmul(f"w_ukv_wgrad{l}", s["ckvn"], dkv, ta=True)
        proj = s["proj"]
        mla_in = [Col(proj, Q_RANK, OFF_CQ // Q_RANK), Col(proj, KV_RANK, OFF_CKV // KV_RANK), Col(proj, LANE, OFF_KR // LANE), cos_t, sin_t]
        dcq, dckv, dkr, G["q_norm"][l], G["kv_norm"][l] = rowwise_bwd(
            f"mla_prep_bwd{l}",
            fn_mla_prep,
            mla_in,
            [],
            [p["q_norm"], p["kv_norm"], rot],
            [dcqn, dckvn, dkrr],
            [True, True, True, False, False],
            [True, True, False],
            mxu_only=(0, 1, 2),
        )
        dxs, dbm, dcm, ddt, ddb, dal, dds = ssd_bwd(s["xbc_act"], proj, p["dt_bias"], p["a_log"], p["d_skip"], s["states"], dyscan)
        G["dt_bias"][l], G["a_log"][l], G["d_skip"][l] = ddb[0, :SSD_HEADS], dal[0, :SSD_HEADS], dds[0, :SSD_HEADS]
        dxbc, G["conv_w"][l], dcb = conv_silu_bwd(proj, p["conv_w"], p["conv_b"], [dxs, dbm, dcm])
        G["conv_b"][l] = dcb[0]
        dproj = jnp.concatenate([dz, dxbc, dckv, dkr, ddt.astype(MXU_DTYPE), dcq], axis=-1)
        dh1 = matmul(f"w_in_dgrad{l}", dproj, p["w_in"], tb=True)
        G["w_in"][l] = win_from_kernel(matmul(f"w_in_wgrad{l}", s["h1"], dproj, ta=True))
        if l > 0:
            q = Ls[l - 1]
            sq = q["saved"]
            dx1, dff, dmod[l - 1][5], dmod[l][1], dmod[l][0], G["norm_mix"][l], dbias[l - 1][5], dbias[l][1], dbias[l][0] = rowwise_bwd(
                f"join_bwd{l - 1}",
                fn_resid_prenorm,
                [sq["x1"], sq["ff"]],
                [q["mods"][5], p["mods"][1], p["mods"][0]],
                [p["norm_mix"], q["bias"][5], p["bias"][1], p["bias"][0]],
                [dxb, dh1],
                [True, True],
                [True] * 4,
                mxu_only=(1,),
            )
        else:
            grad_x, dmod[0][1], dmod[0][0], G["norm_mix"][0], dbias[0][1], dbias[0][0] = rowwise_bwd(
                "prenorm_bwd",
                fn_prenorm,
                [x],
                [p["mods"][1], p["mods"][0]],
                [p["norm_mix"], p["bias"][1], p["bias"][0]],
                [dh1],
                [True],
                [True] * 3,
                adds={0: dxb},
            )
    for l in range(DEPTH):
        G["b_ada"][l] = jnp.concatenate([d[0] for d in dbias[l]])
        for k in ("norm_mix", "ssd_norm", "q_norm", "kv_norm", "attn_norm", "norm_mlp"):
            G[k][l] = G[k][l][0]
    grads = {k: (v if k in BIG else jnp.stack(v) if isinstance(v, list) else v[0]) for k, v in G.items()}
    dmod_raw = jnp.stack([jnp.stack([d[:, 0, :] for d in dmod[l]], axis=1) for l in range(DEPTH)])
    return loss, grad_x, grads, dmod_raw


MESH = pl.DeviceIdType.MESH
ANY = pl.BlockSpec(memory_space=pl.ANY)
PACK_W = 1024
PACK_TILE = 256


def _place():
    x, y, c = lax.axis_index("x"), lax.axis_index("y"), lax.axis_index("c")
    chips = [(1 - x, y), (x, 1 - y), (1 - x, 1 - y)]
    return x, y, c, chips


def _remote(src, dst, send_sem, recv_sem, to):
    return pltpu.make_async_remote_copy(src_ref=src, dst_ref=dst, send_sem=send_sem, recv_sem=recv_sem, device_id=to, device_id_type=MESH)


def all_gather8(name, v):
    m_per, n = v.shape

    def body(x_ref, out_ref, send_sems, recv_sems, local_sem):
        x, y, c, chips = _place()
        me, sibling = (x, y, c), (x, y, 1 - c)

        def rows(px, py, pc):
            return out_ref.at[pl.ds((4 * px + 2 * py + pc) * m_per, m_per), :]

        def copy(k, block, to, src=None):
            return _remote(rows(*block) if src is None else src, rows(*block), send_sems.at[k], recv_sems.at[k], to)

        mine = pltpu.make_async_copy(x_ref, rows(*me), local_sem)
        mine.start()
        first = [copy(0, me, sibling, src=x_ref)]
        first += [copy(1 + j, me, (*chip, c), src=x_ref) for j, chip in enumerate(chips)]
        for cp in first:
            cp.start()
        passed = [copy(4 + j, (*chip, c), sibling) for j, chip in enumerate(chips)]
        for j, chip in enumerate(chips):
            copy(1 + j, (*chip, c), me).wait_recv()
            passed[j].start()
        copy(0, sibling, me).wait_recv()
        for j, chip in enumerate(chips):
            copy(4 + j, (*chip, 1 - c), me).wait_recv()
        for cp in first + passed:
            cp.wait_send()
        mine.wait()

    return pl.pallas_call(
        body,
        name=name,
        out_shape=jax.ShapeDtypeStruct((N_DEV * m_per, n), v.dtype),
        in_specs=[pl.BlockSpec(memory_space=pltpu.VMEM)],
        out_specs=pl.BlockSpec(memory_space=pltpu.VMEM),
        scratch_shapes=[pltpu.SemaphoreType.DMA((7,)), pltpu.SemaphoreType.DMA((7,)), pltpu.SemaphoreType.DMA],
        compiler_params=pltpu.CompilerParams(vmem_limit_bytes=VMEM_LIMIT),
    )(v)


def gather_weights(pack):
    R, n = pack.shape
    rh = R // 2

    def body(x_ref, out_ref, send_sems, recv_sems):
        x, y, c, chips = _place()
        me = 2 * x + y

        def half(chip, hc):
            return out_ref.at[chip, pl.ds(hc * rh, rh), :]

        src = x_ref.at[pl.ds(c * rh, rh), :]
        first = [_remote(src, half(me, c), send_sems.at[j], recv_sems.at[j], (px, py, c)) for j, (px, py) in enumerate(chips)]
        for cp in first:
            cp.start()
        passed = []
        for j, (px, py) in enumerate(chips):
            got = half(2 * px + py, c)
            _remote(got, got, send_sems.at[j], recv_sems.at[j], (px, py, c)).wait_recv()
            cp = _remote(got, got, send_sems.at[3 + j], recv_sems.at[3 + j], (x, y, 1 - c))
            cp.start()
            passed.append(cp)
        for j, (px, py) in enumerate(chips):
            got = half(2 * px + py, 1 - c)
            _remote(got, got, send_sems.at[3 + j], recv_sems.at[3 + j], (x, y, 1 - c)).wait_recv()
        for cp in first + passed:
            cp.wait_send()

    return pl.pallas_call(
        body,
        name="gather_weights",
        out_shape=jax.ShapeDtypeStruct((N_CHIPS, R, n), pack.dtype),
        in_specs=[ANY],
        out_specs=ANY,
        scratch_shapes=[pltpu.SemaphoreType.DMA((6,)), pltpu.SemaphoreType.DMA((6,))],
    )(pack)


def swap_halves(g):
    n_slot, R, n = g.shape
    rh = R // 2

    def body(g_ref, got_ref, send_sem, recv_sem):
        x, y, c, _ = _place()
        cp = _remote(g_ref.at[:, pl.ds((1 - c) * rh, rh), :], got_ref, send_sem, recv_sem, (x, y, 1 - c))
        cp.start()
        cp.wait()

    return pl.pallas_call(
        body,
        name="swap_halves",
        out_shape=jax.ShapeDtypeStruct((n_slot, rh, n), g.dtype),
        in_specs=[ANY],
        out_specs=ANY,
        scratch_shapes=[pltpu.SemaphoreType.DMA, pltpu.SemaphoreType.DMA],
    )(g)


def scatter_chips(buf):
    def body(s_ref, got_ref, send_sems, recv_sems):
        x, y, c, chips = _place()
        me = 2 * x + y
        cps = [_remote(s_ref.at[2 * px + py], got_ref.at[me], send_sems.at[j], recv_sems.at[j], (px, py, c)) for j, (px, py) in enumerate(chips)]
        for cp in cps:
            cp.start()
        for j, (px, py) in enumerate(chips):
            got = got_ref.at[2 * px + py]
            _remote(got, got, send_sems.at[j], recv_sems.at[j], (px, py, c)).wait_recv()
        for cp in cps:
            cp.wait_send()

    return pl.pallas_call(
        body,
        name="scatter_chips",
        out_shape=jax.ShapeDtypeStruct(buf.shape, buf.dtype),
        in_specs=[ANY],
        out_specs=ANY,
        scratch_shapes=[pltpu.SemaphoreType.DMA((3,)), pltpu.SemaphoreType.DMA((3,))],
    )(buf)


def swap_with_sibling(h):
    def body(h_ref, got_ref, send_sem, recv_sem):
        x, y, c, _ = _place()
        cp = _remote(h_ref, got_ref, send_sem, recv_sem, (x, y, 1 - c))
        cp.start()
        cp.wait()

    return pl.pallas_call(
        body,
        name="swap_with_sibling",
        out_shape=jax.ShapeDtypeStruct(h.shape, h.dtype),
        in_specs=[ANY],
        out_specs=ANY,
        scratch_shapes=[pltpu.SemaphoreType.DMA, pltpu.SemaphoreType.DMA],
    )(h)


def chip_sum(g, got, core, chip):
    n_slot, R, n = g.shape
    rh = R // 2
    nb = rh // PACK_TILE

    def body(pos, g_ref, got_ref, wire_ref, own_ref):
        k = pl.program_id(1)
        s = g_ref[0] + got_ref[0]
        wire_ref[0] = s.astype(wire_ref.dtype)

        @pl.when(k == pos[1])
        def _():
            own_ref[...] = s

    grid_spec = pltpu.PrefetchScalarGridSpec(
        num_scalar_prefetch=1,
        grid=(nb, n_slot),
        in_specs=[
            pl.BlockSpec((1, PACK_TILE, n), lambda i, k, pos: (k, pos[0] * nb + i, 0)),
            pl.BlockSpec((1, PACK_TILE, n), lambda i, k, pos: (k, i, 0)),
        ],
        out_specs=[
            pl.BlockSpec((1, PACK_TILE, n), lambda i, k, pos: (k, i, 0)),
            pl.BlockSpec((PACK_TILE, n), lambda i, k, pos: (i, 0)),
        ],
    )
    return pl.pallas_call(
        body,
        name="chip_sum",
        grid_spec=grid_spec,
        out_shape=[jax.ShapeDtypeStruct((n_slot, rh, n), WIRE_DTYPE), jax.ShapeDtypeStruct((rh, n), F32)],
        compiler_params=_cparams(2),
    )(jnp.stack([core, chip]).astype(jnp.int32), g, got)


def mesh_sum(own, got, chip):
    rh, n = own.shape
    n_slot = got.shape[0]

    def body(pos, own_ref, *refs):
        out_ref = refs[-1]
        acc = own_ref[...]
        for k in range(n_slot):
            acc = acc + jnp.where(k != pos[0], refs[k][0].astype(F32), 0.0)
        out_ref[...] = acc

    grid_spec = pltpu.PrefetchScalarGridSpec(
        num_scalar_prefetch=1,
        grid=(rh // PACK_TILE,),
        in_specs=[pl.BlockSpec((PACK_TILE, n), lambda i, pos: (i, 0))]
        + [pl.BlockSpec((1, PACK_TILE, n), lambda i, pos, k=k: (k, i, 0)) for k in range(n_slot)],
        out_specs=pl.BlockSpec((PACK_TILE, n), lambda i, pos: (i, 0)),
    )
    return pl.pallas_call(
        body, name="mesh_sum", grid_spec=grid_spec, out_shape=jax.ShapeDtypeStruct((rh, n), F32), compiler_params=_cparams(1)
    )(jnp.stack([chip]).astype(jnp.int32), own, *([got] * n_slot))


def sum_devices(name, v):
    def body(v_ref, o_ref):
        acc = v_ref[0]
        for d in range(1, N_DEV):
            acc = acc + v_ref[d]
        o_ref[...] = acc

    return pl.pallas_call(body, name=name, out_shape=jax.ShapeDtypeStruct(v.shape[1:], F32))(v)


def adamw(name, w, g, m, v):
    shape = w.shape
    w2, g2, m2, v2 = (t.reshape(-1, shape[-1]) for t in (w, g, m, v))
    rows, n = w2.shape
    tr = _pick(rows, (256, 128, 64, 32, 16, 8))
    c1 = 1.0 / (1.0 - ADAM_B1**ADAM_STEP)
    c2 = 1.0 / (1.0 - ADAM_B2**ADAM_STEP)

    def body(w_ref, g_ref, m_ref, v_ref, d_ref, nm_ref, nv_ref):
        gv = g_ref[...]
        nm = ADAM_B1 * m_ref[...] + (1.0 - ADAM_B1) * gv
        nv = ADAM_B2 * v_ref[...] + (1.0 - ADAM_B2) * jnp.square(gv)
        d_ref[...] = -ADAM_LR * ((nm * c1) / (jnp.sqrt(nv * c2) + ADAM_EPS) + ADAM_WD * w_ref[...])
        nm_ref[...] = nm
        nv_ref[...] = nv

    spec = pl.BlockSpec((tr, n), lambda i: (i, 0))
    outs = pl.pallas_call(
        body,
        name=name,
        grid=(rows // tr,),
        in_specs=[spec] * 4,
        out_specs=[spec] * 3,
        out_shape=[jax.ShapeDtypeStruct((rows, n), F32)] * 3,
        compiler_params=_cparams(1),
    )(w2, g2, m2, v2)
    return [o.reshape(shape) for o in outs]


def ada_fwd(c_all, w_ada):
    n_tok, d = c_all.shape
    depth, _, cols = w_ada.shape
    tn = _pick(cols, (512, 384, 256, 128))

    def body(c_ref, w_ref, o_ref):
        o_ref[0] = jnp.dot(_silu(c_ref[...]).astype(MXU_DTYPE), w_ref[0].astype(MXU_DTYPE), preferred_element_type=F32)

    return pl.pallas_call(
        body,
        name="ada_fwd",
        grid=(depth, cols // tn),
        in_specs=[pl.BlockSpec((n_tok, d), lambda l, j: (0, 0)), pl.BlockSpec((1, d, tn), lambda l, j: (l, 0, j))],
        out_specs=pl.BlockSpec((1, n_tok, tn), lambda l, j: (l, 0, j)),
        out_shape=jax.ShapeDtypeStruct((depth, n_tok, cols), F32),
        compiler_params=_cparams(2),
    )(c_all, w_ada)


COL_SHARDED = ("w_in", "w_uq", "w_ukv", "w_up")


def _pack_rows(parts, rows):
    parts = [p.reshape(-1, PACK_W) for p in parts]
    fill = jnp.zeros((rows - sum(p.shape[0] for p in parts), PACK_W), parts[0].dtype)
    return jnp.concatenate(parts + [fill], axis=0)


def _big_rows(shards):
    n = sum(int(np.prod(shards[k].shape)) for k in BIG) // PACK_W
    return -(-n // (2 * PACK_TILE)) * 2 * PACK_TILE


def unpack_gathered(gathered, shard_shapes):
    out, r0 = {}, 0
    for k in BIG:
        shp = shard_shapes[k]
        n = int(np.prod(shp)) // PACK_W
        seg = gathered[:, r0 : r0 + n].reshape(N_CHIPS, *shp)
        r0 += n
        if k in COL_SHARDED:
            out[k] = seg.transpose(1, 2, 0, 3).reshape(shp[0], shp[1], N_CHIPS * shp[2])
        else:
            out[k] = seg.transpose(1, 0, 2, 3).reshape(shp[0], N_CHIPS * shp[1], shp[2])
    return out


def pack_full_grads(grads, shard_shapes, rows):
    parts = []
    for k in BIG:
        _, rows_k, cols_k = shard_shapes[k]
        layers = grads[k]
        if (rows_k * cols_k // PACK_W) % SUBLANES:
            layers = [jnp.concatenate(layers, axis=0)]
        for g in layers:
            if k in COL_SHARDED:
                g = g.reshape(g.shape[0], N_CHIPS, cols_k).transpose(1, 0, 2)
            else:
                g = g.reshape(-1, N_CHIPS, rows_k, cols_k).transpose(1, 0, 2, 3)
            parts.append(g.reshape(N_CHIPS, -1, PACK_W))
    fill = jnp.zeros((N_CHIPS, rows - sum(p.shape[1] for p in parts), PACK_W), F32)
    return jnp.concatenate(parts + [fill], axis=1)


def unpack_shards(buf, shard_shapes):
    out, r0 = {}, 0
    for k in BIG:
        n = int(np.prod(shard_shapes[k])) // PACK_W
        out[k] = buf[r0 : r0 + n].reshape(shard_shapes[k])
        r0 += n
    return out


def _flat_pack(arrs, row_multiple=8):
    flat = jnp.concatenate([a.reshape(-1) for a in arrs])
    per = PACK_W * row_multiple
    n = -(-flat.shape[0] // per) * per
    return jnp.pad(flat, (0, n - flat.shape[0])).reshape(-1, PACK_W)


def _flat_unpack(buf, shapes):
    flat, out, o = buf.reshape(-1), [], 0
    for s in shapes:
        n = int(np.prod(s))
        out.append(flat[o : o + n].reshape(s))
        o += n
    return out


WEIGHTS = (
    "w_ada", "b_ada", "norm_mix", "w_in", "conv_w", "conv_b", "dt_bias", "a_log", "d_skip", "ssd_norm", "q_norm", "w_uq",
    "kv_norm", "w_ukv", "attn_norm", "w_out", "norm_mlp", "w_up", "conv_ff_w", "conv_ff_b", "w_down", "final_norm",
)
REPLICATED = ("b_ada", "norm_mix", "conv_b", "dt_bias", "a_log", "d_skip", "ssd_norm", "q_norm", "kv_norm", "attn_norm",
              "norm_mlp", "conv_ff_b", "final_norm")
CONV_SHARDED = ("conv_w", "conv_ff_w")


def kernel(x, c, positions, w_ada, b_ada, norm_mix, w_in, conv_w, conv_b, dt_bias, a_log, d_skip, ssd_norm, q_norm, w_uq, kv_norm, w_ukv, attn_norm, w_out, norm_mlp, w_up, conv_ff_w, conv_ff_b, w_down, final_norm, loss_target, m_w_ada, m_b_ada, m_norm_mix, m_w_in, m_conv_w, m_conv_b, m_dt_bias, m_a_log, m_d_skip, m_ssd_norm, m_q_norm, m_w_uq, m_kv_norm, m_w_ukv, m_attn_norm, m_w_out, m_norm_mlp, m_w_up, m_conv_ff_w, m_conv_ff_b, m_w_down, m_final_norm, v_w_ada, v_b_ada, v_norm_mix, v_w_in, v_conv_w, v_conv_b, v_dt_bias, v_a_log, v_d_skip, v_ssd_norm, v_q_norm, v_w_uq, v_kv_norm, v_w_ukv, v_attn_norm, v_w_out, v_norm_mlp, v_w_up, v_conv_ff_w, v_conv_ff_b, v_w_down, v_final_norm):
    loc = locals()
    Wl = {k: loc[k] for k in WEIGHTS}
    Ml = {k: loc["m_" + k] for k in WEIGHTS}
    Vl = {k: loc["v_" + k] for k in WEIGHTS}
    xi, yi, ci = lax.axis_index("x"), lax.axis_index("y"), lax.axis_index("c")
    chip = 2 * xi + yi
    dev = 2 * chip + ci
    B, S, D = x.shape
    n_tok = N_DEV * B

    c_all = all_gather8("gather_c", c.reshape(8, -1)).reshape(n_tok, D)
    mod_cols = ada_fwd(c_all, w_ada)
    cols = mod_cols.shape[-1]
    half = n_tok // 2
    mod_mine = lax.dynamic_slice_in_dim(mod_cols, ci * half, half, axis=1)
    small_in = _flat_pack([mod_mine, conv_w, conv_ff_w])
    n_mod = mod_mine.size // PACK_W
    small_all = all_gather8("gather_mod", small_in).reshape(N_CHIPS, 2, -1, PACK_W)
    mod_all = small_all[:, :, :n_mod].reshape(N_CHIPS, 2, DEPTH, half, cols).transpose(2, 1, 3, 0, 4).reshape(DEPTH, n_tok, N_CHIPS * cols)
    mod_raw = lax.dynamic_slice_in_dim(mod_all, dev * B, B, axis=1).reshape(DEPTH, B, 6, D)
    conv_parts = [_flat_unpack(small_all[k, 0, n_mod:], [conv_w.shape, conv_ff_w.shape]) for k in range(N_CHIPS)]
    conv_full = {name: jnp.concatenate([conv_parts[k][i] for k in range(N_CHIPS)], axis=-1) for i, name in enumerate(CONV_SHARDED)}

    shard_shapes = {k: Wl[k].shape for k in BIG}
    rows = _big_rows(Wl)
    pack = _pack_rows([Wl[k].astype(MXU_DTYPE) for k in BIG], rows)
    gathered = lax.dynamic_update_slice_in_dim(gather_weights(pack), pack[None], chip, axis=0)
    W = unpack_gathered(gathered, shard_shapes)
    W.update({k: Wl[k] for k in REPLICATED})
    W.update(conv_full)

    loss_lanes, grad_x, grads, dmod_raw = local_step(x, loss_target, positions, mod_raw, W)
    loss = lax.psum(loss_lanes[0, 0], ("x", "y", "c"))

    gpack = pack_full_grads(grads, shard_shapes, rows)
    wire, own = chip_sum(gpack, swap_halves(gpack), ci, chip)
    mine = mesh_sum(own, scatter_chips(wire), chip)
    theirs = swap_with_sibling(mine)
    both = jnp.concatenate([jnp.where(ci == 0, mine, theirs), jnp.where(ci == 0, theirs, mine)], axis=0)
    g_shard = unpack_shards(both, shard_shapes)

    small_names = REPLICATED + CONV_SHARDED
    small_out = _flat_pack([grads[k] for k in small_names] + [dmod_raw])
    small_got = all_gather8("gather_small", small_out).reshape(N_DEV, -1, PACK_W)
    small_sum = _flat_unpack(sum_devices("sum_small", small_got), [grads[k].shape for k in small_names])
    G = dict(zip(small_names, small_sum))
    for name in CONV_SHARDED:
        width = Wl[name].shape[-1]
        G[name] = lax.dynamic_slice_in_dim(G[name], chip * width, width, axis=-1)
    n_small = sum(grads[k].size for k in small_names)
    dmod_all = small_got.reshape(N_DEV, -1)[:, n_small : n_small + dmod_raw.size].reshape(N_DEV, DEPTH, B, 6 * D)
    dmod_all = dmod_all.transpose(1, 0, 2, 3).reshape(DEPTH, n_tok, 6 * D)
    dmod_cols = lax.dynamic_slice_in_dim(dmod_all, chip * cols, cols, axis=-1)
    G["w_ada"] = jnp.stack([matmul(f"w_ada_wgrad{l}", c_all, dmod_cols[l], ta=True, a_act=_silu) for l in range(DEPTH)])
    G.update(g_shard)

    deltas, new_m, new_v = {}, {}, {}
    small_upd = adamw("adamw_small", *[_flat_pack([t[k] for k in REPLICATED]) for t in (Wl, G, Ml, Vl)])
    for res, t in zip((deltas, new_m, new_v), small_upd):
        res.update(zip(REPLICATED, _flat_unpack(t, [Wl[k].shape for k in REPLICATED])))
    for k in WEIGHTS:
        if k not in REPLICATED:
            deltas[k], new_m[k], new_v[k] = adamw("adamw_" + k, Wl[k], G[k], Ml[k], Vl[k])
    return (loss, grad_x, *[G[k] for k in WEIGHTS], *[deltas[k] for k in WEIGHTS], *[new_m[k] for k in WEIGHTS], *[new_v[k] for k in WEIGHTS])
```

```python
import functools
from typing import NamedTuple

import numpy as np
import jax
import jax.numpy as jnp
from jax import lax
from jax.experimental import pallas as pl
from jax.experimental.pallas import tpu as pltpu

F32 = jnp.float32
BF16 = jnp.bfloat16
MXU_DTYPE = jnp.bfloat16
WIRE_DTYPE = jnp.bfloat16
HIGHEST = lax.Precision.HIGHEST

D_MODEL = 1024
DEPTH = 2
D_SSD = 1024
SSD_HEADS = 16
SSD_HEAD_DIM = 64
SSD_GROUPS = 2
SSD_STATE = 128
SSD_CONV = 4
CHUNK = 128
MLA_HEADS = 8
QK_NOPE = 128
QK_ROPE = 64
V_DIM = 128
D_ATT = MLA_HEADS * V_DIM
Q_RANK = 384
KV_RANK = 256
ROPE_BASE = 10000.0
D_FF = 2816
FF_CONV = 3
EPS = 1e-6
D_XBC = D_SSD + 2 * SSD_GROUPS * SSD_STATE
D_IN = D_SSD + D_XBC + SSD_HEADS + Q_RANK + KV_RANK + QK_ROPE
ADAM_LR, ADAM_B1, ADAM_B2, ADAM_EPS, ADAM_WD, ADAM_STEP = 0.001, 0.9, 0.999, 1e-08, 0.01, 10

LANE = 128
N_CHIPS = 4
N_DEV = 8

OFF_Z, OFF_XBC, OFF_CKV, OFF_KR, OFF_DT, OFF_CQ = 0, 1024, 2560, 2816, 2944, 3072
N_PROJ = 3456
Q_HEAD_PAD = 256
VMEM_LIMIT = 56 * 1024 * 1024


def _cparams(n_axes):
    return pltpu.CompilerParams(dimension_semantics=("arbitrary",) * n_axes, vmem_limit_bytes=VMEM_LIMIT)


def _pick(n, prefs):
    for p in prefs:
        if n % p == 0:
            return p
    return n


def _silu(x):
    return x * jax.nn.sigmoid(x)


def _rms(x, g):
    return x * lax.rsqrt(jnp.mean(x * x, axis=-1, keepdims=True) + EPS) * g


def _softplus(x):
    return jnp.maximum(x, 0.0) + jnp.log(1.0 + jnp.exp(-jnp.abs(x)))


class Col(NamedTuple):
    arr: jax.Array
    width: int
    blk: int


def _as_col(a):
    return a if isinstance(a, Col) else Col(a, a.shape[-1], 0)


def _row_specs(tiled, pbatch, shared, tile):
    specs = [pl.BlockSpec((1, tile, t.width), lambda b, i, blk=t.blk: (b, i, blk)) for t in tiled]
    specs += [pl.BlockSpec((1, 1, p.shape[-1]), lambda b, i: (b, 0, 0)) for p in pbatch]
    specs += [pl.BlockSpec(s.shape, lambda b, i: (0, 0)) for s in shared]
    return specs


def _row_vals(refs, nt, npb):
    return [r[0].astype(F32) for r in refs[: nt + npb]] + [r[...].astype(F32) for r in refs[nt + npb :]]


def rowwise_fwd(name, fn, tiled, pbatch, shared, outs, tile=512):
    tiled = [_as_col(t) for t in tiled]
    B, S = tiled[0].arr.shape[:2]
    tile = min(tile, S)
    nt, npb, nsh = len(tiled), len(pbatch), len(shared)
    n_in = nt + npb + nsh

    def body(*refs):
        res = fn(*_row_vals(refs[:n_in], nt, npb))
        for r, v in zip(refs[n_in:], res):
            r[0] = v.astype(r.dtype)

    return pl.pallas_call(
        body,
        name=name,
        grid=(B, S // tile),
        in_specs=_row_specs(tiled, pbatch, shared, tile),
        out_specs=[pl.BlockSpec((1, tile, w), lambda b, i: (b, i, 0)) for w, _ in outs],
        out_shape=[jax.ShapeDtypeStruct((B, S, w), dt) for w, dt in outs],
        compiler_params=_cparams(2),
    )(*[t.arr for t in tiled], *pbatch, *shared)


def rowwise_bwd(name, fn, tiled, pbatch, shared, cts, grad_tiled, grad_shared, adds=None, tile=256, mxu_only=()):
    tiled = [_as_col(t) for t in tiled]
    adds = adds or {}
    B, S = tiled[0].arr.shape[:2]
    tile = min(tile, S)
    nt, npb, nsh = len(tiled), len(pbatch), len(shared)
    n_in = nt + npb + nsh
    add_idx = sorted(adds)
    gt = [i for i in range(nt) if grad_tiled[i]]
    gs = [i for i in range(nsh) if grad_shared[i]]
    diff = gt + [nt + i for i in range(npb)] + [nt + npb + i for i in gs]
    n_ct, n_add = len(cts), len(add_idx)

    def body(*refs):
        vals = _row_vals(refs[:n_in], nt, npb)
        ct_v = tuple(r[0].astype(F32) if c.ndim == 3 else jnp.sum(r[0].astype(F32), axis=0) for r, c in zip(refs[n_in : n_in + n_ct], cts))
        add_v = {i: r[0].astype(F32) for i, r in zip(add_idx, refs[n_in + n_ct : n_in + n_ct + n_add])}
        out_refs = refs[n_in + n_ct + n_add :]

        def f(*dargs):
            full = list(vals)
            for k, i in enumerate(diff):
                full[i] = dargs[k]
            return tuple(fn(*full))

        _, vjp = jax.vjp(f, *[vals[i] for i in diff])
        grads = vjp(ct_v)
        b, i = pl.program_id(0), pl.program_id(1)
        k = 0
        for idx in gt:
            g = grads[k]
            if idx in add_v:
                g = g + add_v[idx]
            out_refs[k][0] = g.astype(out_refs[k].dtype)
            k += 1
        for _ in range(npb):
            r, g = out_refs[k], grads[k]

            @pl.when(i == 0)
            def _(r=r, g=g):
                r[0] = g

            @pl.when(i > 0)
            def _(r=r, g=g):
                r[0] += g

            k += 1
        for _ in gs:
            r, g = out_refs[k], grads[k]

            @pl.when((i == 0) & (b == 0))
            def _(r=r, g=g):
                r[...] = g

            @pl.when((i > 0) | (b > 0))
            def _(r=r, g=g):
                r[...] += g

            k += 1

    out_specs = [pl.BlockSpec((1, tile, tiled[i].width), lambda b, i: (b, i, 0)) for i in gt]
    out_shape = [jax.ShapeDtypeStruct((B, S, tiled[i].width), MXU_DTYPE if i in mxu_only else F32) for i in gt]
    out_specs += [pl.BlockSpec((1, 1, p.shape[-1]), lambda b, i: (b, 0, 0)) for p in pbatch]
    out_shape += [jax.ShapeDtypeStruct(p.shape, F32) for p in pbatch]
    out_specs += [pl.BlockSpec(shared[i].shape, lambda b, i: (0, 0)) for i in gs]
    out_shape += [jax.ShapeDtypeStruct(shared[i].shape, F32) for i in gs]
    in_specs = _row_specs(tiled, pbatch, shared, tile)
    for c in cts:
        if c.ndim == 3:
            in_specs.append(pl.BlockSpec((1, tile, c.shape[-1]), lambda b, i: (b, i, 0)))
        else:
            in_specs.append(pl.BlockSpec((1, c.shape[1], tile, c.shape[-1]), lambda b, i: (b, 0, i, 0)))
    in_specs += [pl.BlockSpec((1, tile, adds[i].shape[-1]), lambda b, i: (b, i, 0)) for i in add_idx]
    return pl.pallas_call(
        body,
        name=name,
        grid=(B, S // tile),
        in_specs=in_specs,
        out_specs=out_specs,
        out_shape=out_shape,
        compiler_params=_cparams(2),
    )(*[t.arr for t in tiled], *pbatch, *shared, *cts, *[adds[i] for i in add_idx])


def fn_prenorm(x, sc, sh, g, bsc, bsh):
    return (_rms(x, g) * (1.0 + sc + bsc) + (sh + bsh),)


def fn_resid_prenorm(x, y, gate, sc, sh, g, bgate, bsc, bsh):
    x1 = x + (gate + bgate) * y
    return x1, _rms(x1, g) * (1.0 + sc + bsc) + (sh + bsh)


def fn_mix(yscan, z, o, ssd_norm, attn_norm):
    y = yscan * _silu(z)
    half = D_SSD // SSD_GROUPS
    first = lax.broadcasted_iota(jnp.int32, y.shape, 1) < half
    sq = y * y
    m0 = jnp.sum(jnp.where(first, sq, 0.0), axis=-1, keepdims=True) / half
    m1 = jnp.sum(jnp.where(first, 0.0, sq), axis=-1, keepdims=True) / half
    r = jnp.where(first, lax.rsqrt(m0 + EPS), lax.rsqrt(m1 + EPS))
    return (jnp.concatenate([y * r * ssd_norm, _rms(o, attn_norm)], axis=-1),)


def fn_mla_prep(cq, ckv, kr, cos_t, sin_t, q_norm, kv_norm, rot):
    krr = kr * cos_t + jnp.dot(kr, rot, precision=HIGHEST, preferred_element_type=F32) * sin_t
    return _rms(cq, q_norm), _rms(ckv, kv_norm), krr


def final_fwdbwd(x1, ff, target, gate, fnorm, bgate, tile=256):
    B, S, D = x1.shape
    tile = min(tile, S)

    def body(x_ref, f_ref, t_ref, g_ref, n_ref, bg_ref, loss_ref, dx_ref, df_ref, dg_ref, dn_ref, dbg_ref):
        b, i = pl.program_id(0), pl.program_id(1)
        tgt = t_ref[0]

        def f(x, y, gate, fn, bg):
            yf = _rms(x + (gate + bg) * y, fn)
            return 0.5 * jnp.sum(jnp.mean(jnp.square(yf - tgt), axis=-1, keepdims=True), axis=0, keepdims=True)

        val, vjp = jax.vjp(f, x_ref[0], f_ref[0], g_ref[0], n_ref[...], bg_ref[...])
        dx, dff, dg, dn, dbg = vjp(jnp.ones((1, 1), F32))
        dx_ref[0] = dx
        df_ref[0] = dff.astype(df_ref.dtype)
        lane_loss = jnp.broadcast_to(val, (1, LANE))

        @pl.when(i == 0)
        def _():
            dg_ref[0] = dg

        @pl.when(i > 0)
        def _():
            dg_ref[0] += dg

        @pl.when((i == 0) & (b == 0))
        def _():
            dn_ref[...] = dn
            dbg_ref[...] = dbg
            loss_ref[...] = lane_loss

        @pl.when((i > 0) | (b > 0))
        def _():
            dn_ref[...] += dn
            dbg_ref[...] += dbg
            loss_ref[...] += lane_loss

    tok = pl.BlockSpec((1, tile, D), lambda b, i: (b, i, 0))
    pb = pl.BlockSpec((1, 1, D), lambda b, i: (b, 0, 0))
    sh = pl.BlockSpec((1, D), lambda b, i: (0, 0))
    return pl.pallas_call(
        body,
        name="final_loss",
        grid=(B, S // tile),
        in_specs=[tok, tok, tok, pb, sh, sh],
        out_specs=[pl.BlockSpec((1, LANE), lambda b, i: (0, 0)), tok, tok, pb, sh, sh],
        out_shape=[
            jax.ShapeDtypeStruct((1, LANE), F32),
            jax.ShapeDtypeStruct((B, S, D), F32),
            jax.ShapeDtypeStruct((B, S, D), MXU_DTYPE),
            jax.ShapeDtypeStruct((B, 1, D), F32),
            jax.ShapeDtypeStruct((1, D), F32),
            jax.ShapeDtypeStruct((1, D), F32),
        ],
        compiler_params=_cparams(2),
    )(x1, ff, target, gate, fnorm, bgate)


MATMUL_VMEM_BUDGET = 40 * 1024 * 1024
STEP_COST_BYTES = 1.2e6
MXU_DIM = 256
MXU_FLOPS_PER_BYTE = 280.0


def _tile_options(n, cap):
    opts = [d for d in range(LANE, min(n, cap) + 1, LANE) if n % d == 0]
    return opts or [n]


def _matmul_tiles(M, N, K, sa, sb, so):
    best = None
    for tk in [K] + [d for d in _tile_options(K, 2048) if d >= 512 and d < K]:
        nk = K // tk
        for tm in _tile_options(M, 2048):
            for tn in _tile_options(N, 1408):
                acc = 0 if (so == 4 or nk == 1) else tm * tn * 4
                vmem = 2 * (tm * tk * sa + tk * tn * sb + tm * tn * so) + acc + tm * tn * 4 + 2 * (tm * tk + tk * tn)
                if vmem > MATMUL_VMEM_BUDGET:
                    continue
                a_reads = M * K * sa * (1 if nk == 1 else N // tn)
                b_reads = K * N * sb * (1 if (nk == 1 and N == tn) else M // tm)
                steps = (M // tm) * (N // tn) * nk
                hbm = a_reads + b_reads + M * N * so + (2 * M * N * 4 * (nk - 1) if nk > 1 else 0) / 8
                fill = (-(-tn // MXU_DIM) * MXU_DIM / tn) * (-(-tk // MXU_DIM) * MXU_DIM / tk)
                mxu = 2.0 * M * N * K * fill / MXU_FLOPS_PER_BYTE
                cost = max(hbm, mxu) + steps * STEP_COST_BYTES
                if best is None or cost < best[0]:
                    best = (cost, tm, tn, tk)
    assert best is not None, (M, N, K)
    return best[1:]


class Layer(NamedTuple):
    arr: jax.Array
    l: int


def matmul(name, a, b, ta=False, tb=False, out_dtype=F32, a_act=None):
    pieces = list(a) if isinstance(a, (list, tuple)) else [a]
    assert len(pieces) == 1 or not ta
    lead = pieces[0].shape[:2] if (pieces[0].ndim == 3 and not ta) else None
    pieces = [p.reshape(-1, p.shape[-1]) if p.ndim == 3 else p for p in pieces]
    layer = None
    if isinstance(b, Layer):
        b, layer = b
        b_rows, b_cols = b.shape[1:]
    else:
        if b.ndim == 3:
            b = b.reshape(-1, b.shape[-1])
        b_rows, b_cols = b.shape
    Kp, M = pieces[0].shape if ta else pieces[0].shape[::-1]
    N = b_rows if tb else b_cols
    assert (b_cols if tb else b_rows) == Kp * len(pieces), (name, pieces[0].shape, b.shape)
    tm, tn, tk = _matmul_tiles(M, N, Kp, pieces[0].dtype.itemsize, b.dtype.itemsize, jnp.dtype(out_dtype).itemsize)
    npk = Kp // tk
    nk = npk * len(pieces)
    dims = (((0 if ta else 1,), (1 if tb else 0,)), ((), ()))
    direct = jnp.dtype(out_dtype) == jnp.dtype(F32)

    def body(*refs):
        a_refs, b_ref, o_ref, scratch = refs[: len(pieces)], refs[len(pieces)], refs[len(pieces) + 1], refs[len(pieces) + 2 :]
        k = pl.program_id(2)

        def product(a_ref):
            av = a_ref[...]
            if a_act is not None:
                av = a_act(av.astype(F32))
            return lax.dot_general(av.astype(MXU_DTYPE), b_ref[...].astype(MXU_DTYPE), dims, preferred_element_type=F32)

        if nk == 1:
            o_ref[...] = product(a_refs[0]).astype(o_ref.dtype)
            return
        acc = o_ref if direct else scratch[0]

        @pl.when(k == 0)
        def _():
            acc[...] = product(a_refs[0])

        for p, a_ref in enumerate(a_refs):
            lo, hi = max(p * npk, 1), (p + 1) * npk
            if lo < hi:

                @pl.when((k >= lo) & (k < hi))
                def _(a_ref=a_ref):
                    acc[...] += product(a_ref)

        if not direct:

            @pl.when(k == nk - 1)
            def _():
                o_ref[...] = acc[...].astype(o_ref.dtype)

    def a_spec(p):
        kk = lambda k: jnp.clip(k - p * npk, 0, npk - 1)
        return pl.BlockSpec((tk, tm), lambda i, j, k: (kk(k), i)) if ta else pl.BlockSpec((tm, tk), lambda i, j, k: (i, kk(k)))

    b_block, b_index = ((tn, tk), lambda i, j, k: (j, k)) if tb else ((tk, tn), lambda i, j, k: (k, j))
    if layer is None:
        b_spec = pl.BlockSpec(b_block, b_index)
    else:
        b_spec = pl.BlockSpec((None, *b_block), lambda i, j, k: (layer, *b_index(i, j, k)))
    out = pl.pallas_call(
        body,
        name=name,
        grid=(M // tm, N // tn, nk),
        in_specs=[a_spec(p) for p in range(len(pieces))] + [b_spec],
        out_specs=pl.BlockSpec((tm, tn), lambda i, j, k: (i, j)),
        out_shape=jax.ShapeDtypeStruct((M, N), out_dtype),
        scratch_shapes=[] if (direct or nk == 1) else [pltpu.VMEM((tm, tn), F32)],
        compiler_params=_cparams(3),
    )(*pieces, b)
    return out.reshape(*lead, N) if lead is not None else out


SUBLANES = 8


def _shift_down(u, d):
    if d == 0:
        return u
    r = pltpu.roll(u, d, 0)
    t = lax.broadcasted_iota(jnp.int32, (SUBLANES, u.shape[1]), 0)
    return jnp.concatenate([jnp.where(t >= d, r[:SUBLANES], 0.0), r[SUBLANES:]], axis=0)


def _shift_up(u, d):
    if d == 0:
        return u
    s = u.shape[0]
    r = pltpu.roll(u, s - d, 0)
    t = lax.broadcasted_iota(jnp.int32, (SUBLANES, u.shape[1]), 0)
    return jnp.concatenate([r[: s - SUBLANES], jnp.where(t < SUBLANES - d, r[s - SUBLANES :], 0.0)], axis=0)


def _conv(u, w, b):
    k = w.shape[0]
    out = b + w[k - 1 : k, :] * u
    for j in range(k - 1):
        out = out + w[j : j + 1, :] * _shift_down(u, k - 1 - j)
    return out


def _conv_bwd(u, w, dc):
    k = w.shape[0]
    du = w[k - 1 : k, :] * dc
    dws = []
    for j in range(k - 1):
        du = du + w[j : j + 1, :] * _shift_up(dc, k - 1 - j)
        dws.append(jnp.sum(dc * _shift_down(u, k - 1 - j), axis=0, keepdims=True))
    dws.append(jnp.sum(dc * u, axis=0, keepdims=True))
    return du, jnp.concatenate(dws, axis=0), jnp.sum(dc, axis=0, keepdims=True)


def conv_silu_fwd(proj, w, b):
    B, S, _ = proj.shape
    k, c = w.shape
    blk0 = OFF_XBC // LANE

    def body(u_ref, w_ref, b_ref, o_ref):
        o_ref[0] = _silu(_conv(u_ref[0], w_ref[...], b_ref[...]))

    return pl.pallas_call(
        body,
        name="conv_silu_fwd",
        grid=(B, c // LANE),
        in_specs=[
            pl.BlockSpec((1, S, LANE), lambda bi, j: (bi, 0, blk0 + j)),
            pl.BlockSpec((k, LANE), lambda bi, j: (0, j)),
            pl.BlockSpec((1, LANE), lambda bi, j: (0, j)),
        ],
        out_specs=pl.BlockSpec((1, S, LANE), lambda bi, j: (bi, 0, j)),
        out_shape=jax.ShapeDtypeStruct((B, S, c), F32),
        compiler_params=_cparams(2),
    )(proj, w, b)


def conv_silu_bwd(proj, w, b, douts):
    B, S, _ = proj.shape
    k, c = w.shape
    blk0 = OFF_XBC // LANE
    ends = np.cumsum([d.shape[-1] // LANE for d in douts]).tolist()
    starts = [0] + ends[:-1]

    def body(u_ref, w_ref, b_ref, *refs):
        d_refs, (du_ref, dw_ref, db_ref) = refs[: len(douts)], refs[len(douts) :]
        j, bi = pl.program_id(0), pl.program_id(1)
        u, wv = u_ref[0], w_ref[...]
        cv = _conv(u, wv, b_ref[...])
        sg = jax.nn.sigmoid(cv)
        dout = d_refs[-1][0]
        for r in reversed(range(len(douts) - 1)):
            dout = jnp.where(j < ends[r], d_refs[r][0], dout)
        dc = dout * (sg * (1.0 + cv * (1.0 - sg)))
        du, dw, db = _conv_bwd(u, wv, dc)
        du_ref[0] = du.astype(du_ref.dtype)

        @pl.when(bi == 0)
        def _():
            dw_ref[...] = dw
            db_ref[...] = db

        @pl.when(bi > 0)
        def _():
            dw_ref[...] += dw
            db_ref[...] += db

    return pl.pallas_call(
        body,
        name="conv_silu_bwd",
        grid=(c // LANE, B),
        in_specs=[
            pl.BlockSpec((1, S, LANE), lambda j, bi: (bi, 0, blk0 + j)),
            pl.BlockSpec((k, LANE), lambda j, bi: (0, j)),
            pl.BlockSpec((1, LANE), lambda j, bi: (0, j)),
        ]
        + [
            pl.BlockSpec((1, S, LANE), lambda j, bi, lo=lo, hi=hi: (bi, 0, jnp.clip(j, lo, hi - 1) - lo))
            for lo, hi in zip(starts, ends)
        ],
        out_specs=[
            pl.BlockSpec((1, S, LANE), lambda j, bi: (bi, 0, j)),
            pl.BlockSpec((k, LANE), lambda j, bi: (0, j)),
            pl.BlockSpec((1, LANE), lambda j, bi: (0, j)),
        ],
        out_shape=[
            jax.ShapeDtypeStruct((B, S, c), MXU_DTYPE),
            jax.ShapeDtypeStruct((k, c), F32),
            jax.ShapeDtypeStruct((1, c), F32),
        ],
        compiler_params=_cparams(2),
    )(proj, w, b, *douts)


def _glu_specs(S, k, nb, batch_col):
    specs = []
    for off in (0, nb):
        specs.append(pl.BlockSpec((1, S, LANE), lambda *ids, off=off: (batch_col(*ids)[0], 0, off + batch_col(*ids)[1])))
    for rows in (k, 1):
        for off in (0, nb):
            specs.append(pl.BlockSpec((rows, LANE), lambda *ids, off=off: (0, off + batch_col(*ids)[1])))
    return specs


def conv_glu_fwd(u, w, b):
    B, S, c2 = u.shape
    k = w.shape[0]
    nb = c2 // 2 // LANE

    def body(ug_ref, uv_ref, wg_ref, wv_ref, bg_ref, bv_ref, o_ref):
        gate = _conv(ug_ref[0], wg_ref[...], bg_ref[...])
        val = _conv(uv_ref[0], wv_ref[...], bv_ref[...])
        o_ref[0] = (_silu(gate) * val).astype(o_ref.dtype)

    return pl.pallas_call(
        body,
        name="conv_glu_fwd",
        grid=(B, nb),
        in_specs=_glu_specs(S, k, nb, lambda bi, j: (bi, j)),
        out_specs=pl.BlockSpec((1, S, LANE), lambda bi, j: (bi, 0, j)),
        out_shape=jax.ShapeDtypeStruct((B, S, c2 // 2), MXU_DTYPE),
        compiler_params=_cparams(2),
    )(u, u, w, w, b, b)


def conv_glu_bwd(u, w, b, da):
    B, S, c2 = u.shape
    k = w.shape[0]
    nb = c2 // 2 // LANE

    def body(ug_ref, uv_ref, wg_ref, wv_ref, bg_ref, bv_ref, d_ref, dug_ref, duv_ref, dwg_ref, dwv_ref, dbg_ref, dbv_ref):
        bi = pl.program_id(1)
        ug, uv, wg, wv = ug_ref[0], uv_ref[0], wg_ref[...], wv_ref[...]
        gate, val = _conv(ug, wg, bg_ref[...]), _conv(uv, wv, bv_ref[...])
        sg = jax.nn.sigmoid(gate)
        dav = d_ref[0].astype(F32)
        dug, dwg, dbg = _conv_bwd(ug, wg, dav * val * (sg * (1.0 + gate * (1.0 - sg))))
        duv, dwv, dbv = _conv_bwd(uv, wv, dav * gate * sg)
        dug_ref[0] = dug.astype(dug_ref.dtype)
        duv_ref[0] = duv.astype(duv_ref.dtype)

        @pl.when(bi == 0)
        def _():
            dwg_ref[...], dwv_ref[...], dbg_ref[...], dbv_ref[...] = dwg, dwv, dbg, dbv

        @pl.when(bi > 0)
        def _():
            dwg_ref[...] += dwg
            dwv_ref[...] += dwv
            dbg_ref[...] += dbg
            dbv_ref[...] += dbv

    tok = pl.BlockSpec((1, S, LANE), lambda j, bi: (bi, 0, j))
    outs = pl.pallas_call(
        body,
        name="conv_glu_bwd",
        grid=(nb, B),
        in_specs=_glu_specs(S, k, nb, lambda j, bi: (bi, j)) + [tok],
        out_specs=[tok, tok] + [pl.BlockSpec((rows, LANE), lambda j, bi: (0, j)) for rows in (k, k, 1, 1)],
        out_shape=[jax.ShapeDtypeStruct((B, S, c2 // 2), MXU_DTYPE)] * 2
        + [jax.ShapeDtypeStruct((rows, c2 // 2), F32) for rows in (k, k, 1, 1)],
        compiler_params=_cparams(2),
    )(u, u, w, w, b, b, da)
    dug, duv, dwg, dwv, dbg, dbv = outs
    return dug, duv, jnp.concatenate([dwg, dwv], axis=1), jnp.concatenate([dbg, dbv], axis=1)


N_PAIR = SSD_HEADS // SSD_GROUPS // 2


def _ssd_chunk(prev, xs, bms, cms, dtraw, dt_bias, a_log, d_skip):
    L = CHUNK
    lane = lax.broadcasted_iota(jnp.int32, (1, LANE), 1)
    sub = lax.broadcasted_iota(jnp.int32, (LANE, 1), 0)
    row = lax.broadcasted_iota(jnp.int32, (L, L), 0)
    col = lax.broadcasted_iota(jnp.int32, (L, L), 1)
    tri = (row >= col).astype(F32)
    first = lane < SSD_HEAD_DIM

    dt = _softplus(dtraw + dt_bias)
    a = dt * (-jnp.exp(a_log))
    acs = jnp.dot(tri, a, precision=HIGHEST, preferred_element_type=F32)
    acs_t = acs.T
    a_end = jnp.sum(a, axis=0, keepdims=True)

    def lane_of(v, h):
        return jnp.sum(jnp.where(lane == h, v, 0.0), axis=1, keepdims=True)

    def expand(v, ha):
        return jnp.where(first, lane_of(v, ha), lane_of(v, ha + 1))

    ys, news = [], []
    for g in range(SSD_GROUPS):
        bm, cm = bms[g].astype(MXU_DTYPE), cms[g].astype(MXU_DTYPE)
        cb = lax.dot_general(cm, bm, (((1,), (1,)), ((), ())), preferred_element_type=F32)
        for j in range(N_PAIR):
            ha = g * (SSD_HEADS // SSD_GROUPS) + 2 * j
            x, state = xs[g * N_PAIR + j], prev[g * N_PAIR + j]
            dt_e, acs_e, end_e = expand(dt, ha), expand(acs, ha), expand(a_end, ha)
            xdt = x * dt_e
            y = jnp.dot(cm, state.astype(MXU_DTYPE), preferred_element_type=F32) * jnp.exp(acs_e)
            st = lax.dot_general(bm, (xdt * jnp.exp(end_e - acs_e)).astype(MXU_DTYPE), (((0,), (0,)), ((), ())), preferred_element_type=F32)
            news.append(state * jnp.exp(end_e) + st)
            for hh in range(2):
                h = ha + hh
                seg = lane_of(acs, h) - jnp.sum(jnp.where(sub == h, acs_t, 0.0), axis=0, keepdims=True)
                decay = jnp.exp(jnp.where(row >= col, seg, -jnp.inf))
                xh = jnp.where(first if hh == 0 else jnp.logical_not(first), xdt, 0.0)
                y = y + jnp.dot((cb * decay).astype(MXU_DTYPE), xh.astype(MXU_DTYPE), preferred_element_type=F32)
            ys.append(y + x * expand(d_skip, ha))
    return ys, news


N_PAIRS = SSD_GROUPS * N_PAIR


def _ssd_specs(chunk_of):
    bc = SSD_GROUPS * SSD_STATE
    par = pl.BlockSpec((1, LANE), lambda *ids: (0, 0))
    return [
        pl.BlockSpec((1, CHUNK, D_SSD), lambda *ids: (*chunk_of(*ids), 0)),
        pl.BlockSpec((1, CHUNK, bc), lambda *ids: (*chunk_of(*ids), D_SSD // bc)),
        pl.BlockSpec((1, CHUNK, bc), lambda *ids: (*chunk_of(*ids), D_SSD // bc + 1)),
        pl.BlockSpec((1, CHUNK, LANE), lambda *ids: (*chunk_of(*ids), OFF_DT // LANE)),
        par,
        par,
        par,
    ]


def _pair_cols(ref):
    return [ref[0, :, k * LANE : (k + 1) * LANE] for k in range(N_PAIRS)]


def _group_cols(ref):
    return [ref[0, :, g * SSD_STATE : (g + 1) * SSD_STATE] for g in range(SSD_GROUPS)]


def ssd_fwd(xbc_act, proj, dt_bias, a_log, d_skip):
    B, S, _ = xbc_act.shape
    nc = S // CHUNK

    def body(xs_ref, bm_ref, cm_ref, dt_ref, db_ref, al_ref, ds_ref, y_ref, st_ref, state):
        @pl.when(pl.program_id(1) == 0)
        def _():
            state[...] = jnp.zeros_like(state)

        prev = [state[k] for k in range(N_PAIRS)]
        for k in range(N_PAIRS):
            st_ref[0, 0, k] = prev[k]
        ys, news = _ssd_chunk(prev, _pair_cols(xs_ref), _group_cols(bm_ref), _group_cols(cm_ref), dt_ref[0], db_ref[...], al_ref[...], ds_ref[...])
        for k in range(N_PAIRS):
            y_ref[0, :, k * LANE : (k + 1) * LANE] = ys[k]
            state[k] = news[k]

    return pl.pallas_call(
        body,
        name="ssd_fwd",
        grid=(B, nc),
        in_specs=_ssd_specs(lambda b, c: (b, c)),
        out_specs=[
            pl.BlockSpec((1, CHUNK, D_SSD), lambda b, c: (b, c, 0)),
            pl.BlockSpec((1, 1, N_PAIRS, SSD_STATE, LANE), lambda b, c: (b, c, 0, 0, 0)),
        ],
        out_shape=[
            jax.ShapeDtypeStruct((B, S, D_SSD), F32),
            jax.ShapeDtypeStruct((B, nc, N_PAIRS, SSD_STATE, LANE), F32),
        ],
        scratch_shapes=[pltpu.VMEM((N_PAIRS, SSD_STATE, LANE), F32)],
        compiler_params=_cparams(2),
    )(xbc_act, xbc_act, xbc_act, proj, dt_bias, a_log, d_skip)


def ssd_bwd(xbc_act, proj, dt_bias, a_log, d_skip, states, dy):
    B, S, _ = xbc_act.shape
    nc = S // CHUNK
    bc = SSD_GROUPS * SSD_STATE
    chunk_of = lambda b, cr: (b, nc - 1 - cr)

    def body(xs_ref, bm_ref, cm_ref, dt_ref, db_ref, al_ref, ds_ref, st_ref, dy_ref,
             dxs_ref, dbm_ref, dcm_ref, ddt_ref, ddb_ref, dal_ref, dds_ref, dstate):
        b, cr = pl.program_id(0), pl.program_id(1)

        @pl.when(cr == 0)
        def _():
            dstate[...] = jnp.zeros_like(dstate)

        prev = [st_ref[0, 0, k] for k in range(N_PAIRS)]
        _, vjp = jax.vjp(
            _ssd_chunk, prev, _pair_cols(xs_ref), _group_cols(bm_ref), _group_cols(cm_ref), dt_ref[0], db_ref[...], al_ref[...], ds_ref[...]
        )
        dprev, dxs, dbm, dcm, ddt, ddb, dal, dds = vjp((_pair_cols(dy_ref), [dstate[k] for k in range(N_PAIRS)]))
        for k in range(N_PAIRS):
            dstate[k] = dprev[k]
            dxs_ref[0, :, k * LANE : (k + 1) * LANE] = dxs[k]
        for g in range(SSD_GROUPS):
            dbm_ref[0, :, g * SSD_STATE : (g + 1) * SSD_STATE] = dbm[g]
            dcm_ref[0, :, g * SSD_STATE : (g + 1) * SSD_STATE] = dcm[g]
        ddt_ref[0] = ddt
        first = (b == 0) & (cr == 0)

        @pl.when(first)
        def _():
            ddb_ref[...] = ddb
            dal_ref[...] = dal
            dds_ref[...] = dds

        @pl.when(jnp.logical_not(first))
        def _():
            ddb_ref[...] += ddb
            dal_ref[...] += dal
            dds_ref[...] += dds

    par = pl.BlockSpec((1, LANE), lambda *ids: (0, 0))
    in_specs = _ssd_specs(chunk_of) + [
        pl.BlockSpec((1, 1, N_PAIRS, SSD_STATE, LANE), lambda b, cr: (*chunk_of(b, cr), 0, 0, 0)),
        pl.BlockSpec((1, CHUNK, D_SSD), lambda b, cr: (*chunk_of(b, cr), 0)),
    ]
    return pl.pallas_call(
        body,
        name="ssd_bwd",
        grid=(B, nc),
        in_specs=in_specs,
        out_specs=[
            pl.BlockSpec((1, CHUNK, D_SSD), lambda b, cr: (*chunk_of(b, cr), 0)),
            pl.BlockSpec((1, CHUNK, bc), lambda b, cr: (*chunk_of(b, cr), 0)),
            pl.BlockSpec((1, CHUNK, bc), lambda b, cr: (*chunk_of(b, cr), 0)),
            pl.BlockSpec((1, CHUNK, LANE), lambda b, cr: (*chunk_of(b, cr), 0)),
            par,
            par,
            par,
        ],
        out_shape=[
            jax.ShapeDtypeStruct((B, S, D_SSD), F32),
            jax.ShapeDtypeStruct((B, S, bc), F32),
            jax.ShapeDtypeStruct((B, S, bc), F32),
            jax.ShapeDtypeStruct((B, S, LANE), F32),
            jax.ShapeDtypeStruct((1, LANE), F32),
            jax.ShapeDtypeStruct((1, LANE), F32),
            jax.ShapeDtypeStruct((1, LANE), F32),
        ],
        scratch_shapes=[pltpu.VMEM((N_PAIRS, SSD_STATE, LANE), F32)],
        compiler_params=_cparams(2),
    )(xbc_act, xbc_act, xbc_act, proj, dt_bias, a_log, d_skip, states, dy)


ATT_SCALE = (QK_NOPE + QK_ROPE) ** -0.5
ATT_TILE = 512


ATT_HEADS_FWD = 4
ATT_HEADS_BWD = 2
LOG2E = 1.4426950408889634
Q_PRESCALE = ATT_SCALE * LOG2E


def fn_rope_q(q, cos_t, sin_t, rot):
    parts = []
    for h in range(MLA_HEADS):
        qr = q[:, h * Q_HEAD_PAD + LANE : (h + 1) * Q_HEAD_PAD]
        qr = qr * cos_t + jnp.dot(qr, rot, precision=HIGHEST, preferred_element_type=F32) * sin_t
        parts += [q[:, h * Q_HEAD_PAD : h * Q_HEAD_PAD + LANE] * Q_PRESCALE, qr * Q_PRESCALE]
    return (jnp.concatenate(parts, axis=1),)


def _head_cols(ref, hh, width):
    return ref[0, :, hh * width : (hh + 1) * width]


def _causal_mask(s):
    return jnp.where(lax.broadcasted_iota(jnp.int32, s.shape, 0) >= lax.broadcasted_iota(jnp.int32, s.shape, 1), s, -jnp.inf)


def _nt(a, b):
    return lax.dot_general(a, b, (((1,), (1,)), ((), ())), preferred_element_type=F32)


def _tn(a, b):
    return lax.dot_general(a, b, (((0,), (0,)), ((), ())), preferred_element_type=F32)


def _per_tile(idx, n, variant):
    for v in range(n):

        @pl.when(idx == v)
        def _(v=v):
            variant(v)


def attention_fwd(qs, kv, krr):
    B, S, _ = qs.shape
    t = min(ATT_TILE, S)
    n = S // t
    hps = ATT_HEADS_FWD

    def body(q_ref, k_ref, kr_ref, o_ref, lse_ref):
        def variant(v):
            spans = ([(0, v * t, False)] if v else []) + [(v * t, (v + 1) * t, True)]
            for hh in range(hps):
                q = _head_cols(q_ref, hh, Q_HEAD_PAD)
                c0 = hh * 2 * LANE
                ss = []
                for a, b, diag in spans:
                    s = _nt(q, jnp.concatenate([k_ref[0, a:b, c0 : c0 + LANE], kr_ref[0, a:b]], axis=1))
                    ss.append(_causal_mask(s) if diag else s)
                m = functools.reduce(jnp.maximum, [jnp.max(s, axis=1, keepdims=True) for s in ss])
                ps = [jnp.exp2(s - m) for s in ss]
                l = functools.reduce(jnp.add, [jnp.sum(p, axis=1, keepdims=True) for p in ps])
                acc = functools.reduce(
                    jnp.add,
                    [jnp.dot(p.astype(MXU_DTYPE), k_ref[0, a:b, c0 + LANE : c0 + 2 * LANE], preferred_element_type=F32) for p, (a, b, _) in zip(ps, spans)],
                )
                o_ref[0, :, hh * V_DIM : (hh + 1) * V_DIM] = acc / l
                lse_ref[0, hh] = m + jnp.log2(l)

        _per_tile(pl.program_id(2), n, variant)

    return pl.pallas_call(
        body,
        name="attention_fwd",
        grid=(B, MLA_HEADS // hps, n),
        in_specs=[
            pl.BlockSpec((1, t, hps * Q_HEAD_PAD), lambda b, h, i: (b, i, h)),
            pl.BlockSpec((1, S, hps * 2 * LANE), lambda b, h, i: (b, 0, h)),
            pl.BlockSpec((1, S, LANE), lambda b, h, i: (b, 0, 0)),
        ],
        out_specs=[
            pl.BlockSpec((1, t, hps * V_DIM), lambda b, h, i: (b, i, h)),
            pl.BlockSpec((1, hps, t, 1), lambda b, h, i: (b, h, i, 0)),
        ],
        out_shape=[jax.ShapeDtypeStruct((B, S, D_ATT), F32), jax.ShapeDtypeStruct((B, MLA_HEADS, S, 1), F32)],
        compiler_params=_cparams(3),
    )(qs, kv, krr)


def attention_bwd(qs, kv, krr, cos_t, sin_t, rot_t, o, lse, do):
    B, S, _ = qs.shape
    t = min(ATT_TILE, S)
    n = S // t
    hps = ATT_HEADS_BWD

    def body(q_ref, c_ref, s_ref, rott_ref, k_ref, kr_ref, o_ref, lse_ref, do_ref, dq_ref, dkv_ref, dkr_ref, dk_acc, dv_acc):
        i = pl.program_id(2)

        @pl.when(i == 0)
        def _():
            dk_acc[...] = jnp.zeros_like(dk_acc)
            dv_acc[...] = jnp.zeros_like(dv_acc)

        def variant(v):
            spans = ([(0, v * t, False)] if v else []) + [(v * t, (v + 1) * t, True)]
            for hh in range(hps):
                q = _head_cols(q_ref, hh, Q_HEAD_PAD)
                dov = _head_cols(do_ref, hh, V_DIM)
                delta = jnp.sum(_head_cols(o_ref, hh, V_DIM) * dov, axis=1, keepdims=True)
                dob = dov.astype(MXU_DTYPE)
                c0 = hh * 2 * LANE
                acc = None
                for a, b, diag in spans:
                    k = jnp.concatenate([k_ref[0, a:b, c0 : c0 + LANE], kr_ref[0, a:b]], axis=1)
                    s = _nt(q, k)
                    p = jnp.exp2((_causal_mask(s) if diag else s) - lse_ref[0, hh])
                    ds = (p * (_nt(dob, k_ref[0, a:b, c0 + LANE : c0 + 2 * LANE]) - delta) * ATT_SCALE).astype(MXU_DTYPE)
                    part = jnp.dot(ds, k, preferred_element_type=F32)
                    acc = part if acc is None else acc + part
                    dk_acc[hh, a:b] += _tn(ds, q)
                    dv_acc[hh, a:b] += _tn(p.astype(MXU_DTYPE), dob)
                r = acc[:, LANE:]
                dr = r * c_ref[0] + jnp.dot(r * s_ref[0], rott_ref[...], precision=HIGHEST, preferred_element_type=F32)
                dq_ref[0, :, hh * Q_HEAD_PAD : (hh + 1) * Q_HEAD_PAD] = jnp.concatenate([acc[:, :LANE], dr], axis=1).astype(dq_ref.dtype)

        _per_tile(i, n, variant)

        @pl.when(i == n - 1)
        def _():
            dkr = None
            for hh in range(hps):
                dk = dk_acc[hh] * (1.0 / Q_PRESCALE)
                dkv_ref[0, :, hh * 2 * LANE : (hh + 1) * 2 * LANE] = jnp.concatenate([dk[:, :LANE], dv_acc[hh]], axis=1).astype(dkv_ref.dtype)
                dkr = dk[:, LANE:] if dkr is None else dkr + dk[:, LANE:]
            dkr_ref[0, 0] = dkr

    return pl.pallas_call(
        body,
        name="attention_bwd",
        grid=(B, MLA_HEADS // hps, n),
        in_specs=[
            pl.BlockSpec((1, t, hps * Q_HEAD_PAD), lambda b, h, i: (b, i, h)),
            pl.BlockSpec((1, t, LANE), lambda b, h, i: (b, i, 0)),
            pl.BlockSpec((1, t, LANE), lambda b, h, i: (b, i, 0)),
            pl.BlockSpec((LANE, LANE), lambda b, h, i: (0, 0)),
            pl.BlockSpec((1, S, hps * 2 * LANE), lambda b, h, i: (b, 0, h)),
            pl.BlockSpec((1, S, LANE), lambda b, h, i: (b, 0, 0)),
            pl.BlockSpec((1, t, hps * V_DIM), lambda b, h, i: (b, i, h)),
            pl.BlockSpec((1, hps, t, 1), lambda b, h, i: (b, h, i, 0)),
            pl.BlockSpec((1, t, hps * V_DIM), lambda b, h, i: (b, i, h)),
        ],
        out_specs=[
            pl.BlockSpec((1, t, hps * Q_HEAD_PAD), lambda b, h, i: (b, i, h)),
            pl.BlockSpec((1, S, hps * 2 * LANE), lambda b, h, i: (b, 0, h)),
            pl.BlockSpec((1, 1, S, LANE), lambda b, h, i: (b, h, 0, 0)),
        ],
        out_shape=[
            jax.ShapeDtypeStruct(qs.shape, MXU_DTYPE),
            jax.ShapeDtypeStruct(kv.shape, MXU_DTYPE),
            jax.ShapeDtypeStruct((B, MLA_HEADS // hps, S, LANE), F32),
        ],
        scratch_shapes=[pltpu.VMEM((hps, S, Q_HEAD_PAD), F32), pltpu.VMEM((hps, S, V_DIM), F32)],
        compiler_params=_cparams(3),
    )(qs, cos_t, sin_t, rot_t, kv, krr, o, lse, do)


_IN_SPLITS = np.cumsum([D_SSD, D_XBC, SSD_HEADS, Q_RANK, KV_RANK]).tolist()


def _pad_last(t, n):
    return jnp.pad(t, [(0, 0)] * (t.ndim - 1) + [(0, n - t.shape[-1])])


def win_to_kernel(w):
    z, xbc, dt, cq, ckv, kr = jnp.split(w, _IN_SPLITS, axis=-1)
    return jnp.concatenate([z, xbc, ckv, _pad_last(kr, LANE), _pad_last(dt, LANE), cq], axis=-1)


def win_from_kernel(g):
    z, xbc = g[..., :OFF_XBC], g[..., OFF_XBC:OFF_CKV]
    ckv, kr = g[..., OFF_CKV:OFF_KR], g[..., OFF_KR : OFF_KR + QK_ROPE]
    dt, cq = g[..., OFF_DT : OFF_DT + SSD_HEADS], g[..., OFF_CQ:]
    return jnp.concatenate([z, xbc, dt, cq, ckv, kr], axis=-1)


def wuq_to_kernel(w):
    w = w.reshape(*w.shape[:-1], MLA_HEADS, QK_NOPE + QK_ROPE)
    return _pad_last(w, Q_HEAD_PAD).reshape(*w.shape[:-2], MLA_HEADS * Q_HEAD_PAD)


def wuq_from_kernel(g):
    g = g.reshape(*g.shape[:-1], MLA_HEADS, Q_HEAD_PAD)[..., : QK_NOPE + QK_ROPE]
    return g.reshape(*g.shape[:-2], MLA_HEADS * (QK_NOPE + QK_ROPE))


def _lane_pad_row(v):
    return _pad_last(v, LANE)[None, :]


def rope_tables(positions):
    inv_freq = jnp.asarray(1.0 / (ROPE_BASE ** (np.arange(0, QK_ROPE, 2, dtype=np.float32) / QK_ROPE)))
    ang = positions.astype(F32)[..., None] * inv_freq
    cos, sin = jnp.cos(ang), jnp.sin(ang)
    zeros = jnp.zeros(cos.shape[:-1] + (LANE - QK_ROPE,), F32)
    rot = np.zeros((LANE, LANE), np.float32)
    half = QK_ROPE // 2
    for j in range(half):
        rot[j + half, j] = -1.0
        rot[j, j + half] = 1.0
    return jnp.concatenate([cos, cos, zeros], -1), jnp.concatenate([sin, sin, zeros], -1), jnp.asarray(rot), jnp.asarray(rot.T)


BIG = ("w_in", "w_uq", "w_ukv", "w_out", "w_up", "w_down")


def local_step(x, target, positions, mod_raw, W):
    B, S, D = x.shape
    cos_t, sin_t, rot, rot_t = rope_tables(positions)
    row = lambda v: v.reshape(1, -1)
    w_in_k, w_uq_k = win_to_kernel(W["w_in"]), wuq_to_kernel(W["w_uq"])
    Ls = []
    for l in range(DEPTH):
        mods = [mod_raw[l, :, k][:, None, :] for k in range(6)]
        bias = [row(W["b_ada"][l, k * D : (k + 1) * D]) for k in range(6)]
        Ls.append(
            dict(
                mods=mods,
                bias=bias,
                w_in=Layer(w_in_k, l),
                w_uq=Layer(w_uq_k, l),
                w_ukv=Layer(W["w_ukv"], l),
                w_out=Layer(W["w_out"], l),
                w_up=Layer(W["w_up"], l),
                w_down=Layer(W["w_down"], l),
                conv_w=W["conv_w"][l],
                conv_b=row(W["conv_b"][l]),
                conv_ff_w=W["conv_ff_w"][l],
                conv_ff_b=row(W["conv_ff_b"][l]),
                dt_bias=_lane_pad_row(W["dt_bias"][l]),
                a_log=_lane_pad_row(W["a_log"][l]),
                d_skip=_lane_pad_row(W["d_skip"][l]),
                norm_mix=row(W["norm_mix"][l]),
                ssd_norm=row(W["ssd_norm"][l]),
                q_norm=row(W["q_norm"][l]),
                kv_norm=row(W["kv_norm"][l]),
                attn_norm=row(W["attn_norm"][l]),
                norm_mlp=row(W["norm_mlp"][l]),
            )
        )
    fnorm = row(W["final_norm"])

    p0 = Ls[0]
    (h1,) = rowwise_fwd(
        "prenorm_fwd", fn_prenorm, [x], [p0["mods"][1], p0["mods"][0]], [p0["norm_mix"], p0["bias"][1], p0["bias"][0]], [(D, MXU_DTYPE)]
    )
    xin = x
    for l, p in enumerate(Ls):
        s = p["saved"] = dict(xin=xin, h1=h1)
        s["proj"] = proj = matmul(f"w_in_fwd{l}", h1, p["w_in"])
        s["xbc_act"] = xbc_act = conv_silu_fwd(proj, p["conv_w"], p["conv_b"])
        s["yscan"], s["states"] = ssd_fwd(xbc_act, proj, p["dt_bias"], p["a_log"], p["d_skip"])
        mla_in = [Col(proj, Q_RANK, OFF_CQ // Q_RANK), Col(proj, KV_RANK, OFF_CKV // KV_RANK), Col(proj, LANE, OFF_KR // LANE), cos_t, sin_t]
        s["cqn"], s["ckvn"], s["krr"] = rowwise_fwd(
            f"mla_prep_fwd{l}", fn_mla_prep, mla_in, [], [p["q_norm"], p["kv_norm"], rot], [(Q_RANK, MXU_DTYPE), (KV_RANK, MXU_DTYPE), (LANE, MXU_DTYPE)]
        )
        q = matmul(f"w_uq_fwd{l}", s["cqn"], p["w_uq"], out_dtype=MXU_DTYPE)
        (s["qs"],) = rowwise_fwd(f"rope_q_fwd{l}", fn_rope_q, [q, cos_t, sin_t], [], [rot], [(q.shape[-1], MXU_DTYPE)])
        s["kv"] = matmul(f"w_ukv_fwd{l}", s["ckvn"], p["w_ukv"], out_dtype=MXU_DTYPE)
        s["o"], s["lse"] = attention_fwd(s["qs"], s["kv"], s["krr"])
        (s["ycat"],) = rowwise_fwd(
            f"mix_fwd{l}", fn_mix, [s["yscan"], Col(proj, D_SSD, 0), s["o"]], [], [p["ssd_norm"], p["attn_norm"]], [(D_SSD + D_ATT, MXU_DTYPE)]
        )
        s["ymix"] = matmul(f"w_out_fwd{l}", s["ycat"], p["w_out"])
        s["x1"], s["h2"] = rowwise_fwd(
            f"mid_fwd{l}",
            fn_resid_prenorm,
            [xin, s["ymix"]],
            [p["mods"][2], p["mods"][4], p["mods"][3]],
            [p["norm_mlp"], p["bias"][2], p["bias"][4], p["bias"][3]],
            [(D, F32), (D, MXU_DTYPE)],
        )
        s["u"] = matmul(f"w_up_fwd{l}", s["h2"], p["w_up"])
        s["a"] = conv_glu_fwd(s["u"], p["conv_ff_w"], p["conv_ff_b"])
        s["ff"] = matmul(f"w_down_fwd{l}", s["a"], p["w_down"])
        if l + 1 < DEPTH:
            n = Ls[l + 1]
            xin, h1 = rowwise_fwd(
                f"join_fwd{l}",
                fn_resid_prenorm,
                [s["x1"], s["ff"]],
                [p["mods"][5], n["mods"][1], n["mods"][0]],
                [n["norm_mix"], p["bias"][5], n["bias"][1], n["bias"][0]],
                [(D, F32), (D, MXU_DTYPE)],
            )

    G = {k: [None] * DEPTH for k in W if k not in ("final_norm", "w_ada")}
    dmod = [[None] * 6 for _ in range(DEPTH)]
    dbias = [[None] * 6 for _ in range(DEPTH)]
    last = Ls[-1]
    sl = last["saved"]
    loss, dx1, dff, dmod[-1][5], G["final_norm"], dbias[-1][5] = final_fwdbwd(
        sl["x1"], sl["ff"], target, last["mods"][5], fnorm, last["bias"][5]
    )
    grad_x = None
    for l in reversed(range(DEPTH)):
        p = Ls[l]
        s = p["saved"]
        da = matmul(f"w_down_dgrad{l}", dff, p["w_down"], tb=True)
        G["w_down"][l] = matmul(f"w_down_wgrad{l}", s["a"], dff, ta=True, out_dtype=WIRE_DTYPE)
        du_gate, du_val, G["conv_ff_w"][l], dcb = conv_glu_bwd(s["u"], p["conv_ff_w"], p["conv_ff_b"], da)
        G["conv_ff_b"][l] = dcb[0]
        dh2 = matmul(f"w_up_dgrad{l}", [du_gate, du_val], p["w_up"], tb=True)
        G["w_up"][l] = jnp.concatenate(
            [matmul(f"w_up_wgrad{l}_{half}", s["h2"], d, ta=True, out_dtype=WIRE_DTYPE) for half, d in (("gate", du_gate), ("val", du_val))],
            axis=1,
        )
        dxb, dymix, dmod[l][2], dmod[l][4], dmod[l][3], G["norm_mlp"][l], dbias[l][2], dbias[l][4], dbias[l][3] = rowwise_bwd(
            f"mid_bwd{l}",
            fn_resid_prenorm,
            [s["xin"], s["ymix"]],
            [p["mods"][2], p["mods"][4], p["mods"][3]],
            [p["norm_mlp"], p["bias"][2], p["bias"][4], p["bias"][3]],
            [dx1, dh2],
            [True, True],
            [True] * 4,
            mxu_only=(1,),
        )
        dycat = matmul(f"w_out_dgrad{l}", dymix, p["w_out"], tb=True)
        G["w_out"][l] = matmul(f"w_out_wgrad{l}", s["ycat"], dymix, ta=True, out_dtype=WIRE_DTYPE)
        dyscan, dz, do, G["ssd_norm"][l], G["attn_norm"][l] = rowwise_bwd(
            f"mix_bwd{l}", fn_mix, [s["yscan"], Col(s["proj"], D_SSD, 0), s["o"]], [], [p["ssd_norm"], p["attn_norm"]], [dycat], [True] * 3, [True] * 2, mxu_only=(1,)
        )
        dq, dkv, dkrr = attention_bwd(s["qs"], s["kv"], s["krr"], cos_t, sin_t, rot_t, s["o"], s["lse"], do)
        dcqn = matmul(f"w_uq_dgrad{l}", dq, p["w_uq"], tb=True)
        G["w_uq"][l] = wuq_from_kernel(matmul(f"w_uq_wgrad{l}", s["cqn"], dq, ta=True, out_dtype=WIRE_DTYPE))
        dckvn = matmul(f"w_ukv_dgrad{l}", dkv, p["w_ukv"], tb=True)
        G["w_ukv"][l] = matmul(f"w_ukv_wgrad{l}", s["ckvn"], dkv, ta=True, out_dtype=WIRE_DTYPE)
        proj = s["proj"]
        mla_in = [Col(proj, Q_RANK, OFF_CQ // Q_RANK), Col(proj, KV_RANK, OFF_CKV // KV_RANK), Col(proj, LANE, OFF_KR // LANE), cos_t, sin_t]
        dcq, dckv, dkr, G["q_norm"][l], G["kv_norm"][l] = rowwise_bwd(
            f"mla_prep_bwd{l}",
            fn_mla_prep,
            mla_in,
            [],
            [p["q_norm"], p["kv_norm"], rot],
            [dcqn, dckvn, dkrr],
            [True, True, True, False, False],
            [True, True, False],
            mxu_only=(0, 1, 2),
        )
        dxs, dbm, dcm, ddt, ddb, dal, dds = ssd_bwd(s["xbc_act"], proj, p["dt_bias"], p["a_log"], p["d_skip"], s["states"], dyscan)
        G["dt_bias"][l], G["a_log"][l], G["d_skip"][l] = ddb[0, :SSD_HEADS], dal[0, :SSD_HEADS], dds[0, :SSD_HEADS]
        dxbc, G["conv_w"][l], dcb = conv_silu_bwd(proj, p["conv_w"], p["conv_b"], [dxs, dbm, dcm])
        G["conv_b"][l] = dcb[0]
        dproj = jnp.concatenate([dz, dxbc, dckv, dkr, ddt.astype(MXU_DTYPE), dcq], axis=-1)
        dh1 = matmul(f"w_in_dgrad{l}", dproj, p["w_in"], tb=True)
        G["w_in"][l] = win_from_kernel(matmul(f"w_in_wgrad{l}", s["h1"], dproj, ta=True, out_dtype=WIRE_DTYPE))
        if l > 0:
            q = Ls[l - 1]
            sq = q["saved"]
            dx1, dff, dmod[l - 1][5], dmod[l][1], dmod[l][0], G["norm_mix"][l], dbias[l - 1][5], dbias[l][1], dbias[l][0] = rowwise_bwd(
                f"join_bwd{l - 1}",
                fn_resid_prenorm,
                [sq["x1"], sq["ff"]],
                [q["mods"][5], p["mods"][1], p["mods"][0]],
                [p["norm_mix"], q["bias"][5], p["bias"][1], p["bias"][0]],
                [dxb, dh1],
                [True, True],
                [True] * 4,
                mxu_only=(1,),
            )
        else:
            grad_x, dmod[0][1], dmod[0][0], G["norm_mix"][0], dbias[0][1], dbias[0][0] = rowwise_bwd(
                "prenorm_bwd",
                fn_prenorm,
                [x],
                [p["mods"][1], p["mods"][0]],
                [p["norm_mix"], p["bias"][1], p["bias"][0]],
                [dh1],
                [True],
                [True] * 3,
                adds={0: dxb},
            )
    for l in range(DEPTH):
        G["b_ada"][l] = jnp.concatenate([d[0] for d in dbias[l]])
        for k in ("norm_mix", "ssd_norm", "q_norm", "kv_norm", "attn_norm", "norm_mlp"):
            G[k][l] = G[k][l][0]
    grads = {k: (v if k in BIG else jnp.stack(v) if isinstance(v, list) else v[0]) for k, v in G.items()}
    dmod_raw = jnp.stack([jnp.stack([d[:, 0, :] for d in dmod[l]], axis=1) for l in range(DEPTH)])
    return loss, grad_x, grads, dmod_raw


MESH = pl.DeviceIdType.MESH
ANY = pl.BlockSpec(memory_space=pl.ANY)
PACK_W = 1024
PACK_TILE = 256


def _place():
    x, y, c = lax.axis_index("x"), lax.axis_index("y"), lax.axis_index("c")
    chips = [(1 - x, y), (x, 1 - y), (1 - x, 1 - y)]
    return x, y, c, chips


def _remote(src, dst, send_sem, recv_sem, to):
    return pltpu.make_async_remote_copy(src_ref=src, dst_ref=dst, send_sem=send_sem, recv_sem=recv_sem, device_id=to, device_id_type=MESH)


def all_gather8(name, v):
    m_per, n = v.shape

    def body(x_ref, out_ref, send_sems, recv_sems, local_sem):
        x, y, c, chips = _place()
        me, sibling = (x, y, c), (x, y, 1 - c)

        def rows(px, py, pc):
            return out_ref.at[pl.ds((4 * px + 2 * py + pc) * m_per, m_per), :]

        def copy(k, block, to, src=None):
            return _remote(rows(*block) if src is None else src, rows(*block), send_sems.at[k], recv_sems.at[k], to)

        mine = pltpu.make_async_copy(x_ref, rows(*me), local_sem)
        mine.start()
        first = [copy(0, me, sibling, src=x_ref)]
        first += [copy(1 + j, me, (*chip, c), src=x_ref) for j, chip in enumerate(chips)]
        for cp in first:
            cp.start()
        passed = [copy(4 + j, (*chip, c), sibling) for j, chip in enumerate(chips)]
        for j, chip in enumerate(chips):
            copy(1 + j, (*chip, c), me).wait_recv()
            passed[j].start()
        copy(0, sibling, me).wait_recv()
        for j, chip in enumerate(chips):
            copy(4 + j, (*chip, 1 - c), me).wait_recv()
        for cp in first + passed:
            cp.wait_send()
        mine.wait()

    return pl.pallas_call(
        body,
        name=name,
        out_shape=jax.ShapeDtypeStruct((N_DEV * m_per, n), v.dtype),
        in_specs=[pl.BlockSpec(memory_space=pltpu.VMEM)],
        out_specs=pl.BlockSpec(memory_space=pltpu.VMEM),
        scratch_shapes=[pltpu.SemaphoreType.DMA((7,)), pltpu.SemaphoreType.DMA((7,)), pltpu.SemaphoreType.DMA],
        compiler_params=pltpu.CompilerParams(vmem_limit_bytes=VMEM_LIMIT),
    )(v)


def gather_weights(pack):
    R, n = pack.shape
    rh = R // 2

    def body(x_ref, out_ref, send_sems, recv_sems):
        x, y, c, chips = _place()
        me = 2 * x + y

        def half(chip, hc):
            return out_ref.at[chip, pl.ds(hc * rh, rh), :]

        src = x_ref.at[pl.ds(c * rh, rh), :]
        first = [_remote(src, half(me, c), send_sems.at[j], recv_sems.at[j], (px, py, c)) for j, (px, py) in enumerate(chips)]
        for cp in first:
            cp.start()
        passed = []
        for j, (px, py) in enumerate(chips):
            got = half(2 * px + py, c)
            _remote(got, got, send_sems.at[j], recv_sems.at[j], (px, py, c)).wait_recv()
            cp = _remote(got, got, send_sems.at[3 + j], recv_sems.at[3 + j], (x, y, 1 - c))
            cp.start()
            passed.append(cp)
        for j, (px, py) in enumerate(chips):
            got = half(2 * px + py, 1 - c)
            _remote(got, got, send_sems.at[3 + j], recv_sems.at[3 + j], (x, y, 1 - c)).wait_recv()
        for cp in first + passed:
            cp.wait_send()

    return pl.pallas_call(
        body,
        name="gather_weights",
        out_shape=jax.ShapeDtypeStruct((N_CHIPS, R, n), pack.dtype),
        in_specs=[ANY],
        out_specs=ANY,
        scratch_shapes=[pltpu.SemaphoreType.DMA((6,)), pltpu.SemaphoreType.DMA((6,))],
    )(pack)


def swap_halves(g):
    n_slot, R, n = g.shape
    rh = R // 2

    def body(g_ref, got_ref, send_sem, recv_sem):
        x, y, c, _ = _place()
        cp = _remote(g_ref.at[:, pl.ds((1 - c) * rh, rh), :], got_ref, send_sem, recv_sem, (x, y, 1 - c))
        cp.start()
        cp.wait()

    return pl.pallas_call(
        body,
        name="swap_halves",
        out_shape=jax.ShapeDtypeStruct((n_slot, rh, n), g.dtype),
        in_specs=[ANY],
        out_specs=ANY,
        scratch_shapes=[pltpu.SemaphoreType.DMA, pltpu.SemaphoreType.DMA],
    )(g)


def scatter_chips(buf):
    def body(s_ref, got_ref, send_sems, recv_sems):
        x, y, c, chips = _place()
        me = 2 * x + y
        cps = [_remote(s_ref.at[2 * px + py], got_ref.at[me], send_sems.at[j], recv_sems.at[j], (px, py, c)) for j, (px, py) in enumerate(chips)]
        for cp in cps:
            cp.start()
        for j, (px, py) in enumerate(chips):
            got = got_ref.at[2 * px + py]
            _remote(got, got, send_sems.at[j], recv_sems.at[j], (px, py, c)).wait_recv()
        for cp in cps:
            cp.wait_send()

    return pl.pallas_call(
        body,
        name="scatter_chips",
        out_shape=jax.ShapeDtypeStruct(buf.shape, buf.dtype),
        in_specs=[ANY],
        out_specs=ANY,
        scratch_shapes=[pltpu.SemaphoreType.DMA((3,)), pltpu.SemaphoreType.DMA((3,))],
    )(buf)


def swap_with_sibling(h):
    def body(h_ref, got_ref, send_sem, recv_sem):
        x, y, c, _ = _place()
        cp = _remote(h_ref, got_ref, send_sem, recv_sem, (x, y, 1 - c))
        cp.start()
        cp.wait()

    return pl.pallas_call(
        body,
        name="swap_with_sibling",
        out_shape=jax.ShapeDtypeStruct(h.shape, h.dtype),
        in_specs=[ANY],
        out_specs=ANY,
        scratch_shapes=[pltpu.SemaphoreType.DMA, pltpu.SemaphoreType.DMA],
    )(h)


def chip_sum(g, got, core, chip):
    n_slot, R, n = g.shape
    rh = R // 2
    nb = rh // PACK_TILE

    def body(pos, g_ref, got_ref, wire_ref, own_ref):
        k = pl.program_id(1)
        s = g_ref[0].astype(F32) + got_ref[0].astype(F32)
        wire_ref[0] = s.astype(wire_ref.dtype)

        @pl.when(k == pos[1])
        def _():
            own_ref[...] = s

    grid_spec = pltpu.PrefetchScalarGridSpec(
        num_scalar_prefetch=1,
        grid=(nb, n_slot),
        in_specs=[
            pl.BlockSpec((1, PACK_TILE, n), lambda i, k, pos: (k, pos[0] * nb + i, 0)),
            pl.BlockSpec((1, PACK_TILE, n), lambda i, k, pos: (k, i, 0)),
        ],
        out_specs=[
            pl.BlockSpec((1, PACK_TILE, n), lambda i, k, pos: (k, i, 0)),
            pl.BlockSpec((PACK_TILE, n), lambda i, k, pos: (i, 0)),
        ],
    )
    return pl.pallas_call(
        body,
        name="chip_sum",
        grid_spec=grid_spec,
        out_shape=[jax.ShapeDtypeStruct((n_slot, rh, n), WIRE_DTYPE), jax.ShapeDtypeStruct((rh, n), F32)],
        compiler_params=_cparams(2),
    )(jnp.stack([core, chip]).astype(jnp.int32), g, got)


def mesh_sum(own, got, chip):
    rh, n = own.shape
    n_slot = got.shape[0]

    def body(pos, own_ref, *refs):
        out_ref = refs[-1]
        acc = own_ref[...]
        for k in range(n_slot):
            acc = acc + jnp.where(k != pos[0], refs[k][0].astype(F32), 0.0)
        out_ref[...] = acc

    grid_spec = pltpu.PrefetchScalarGridSpec(
        num_scalar_prefetch=1,
        grid=(rh // PACK_TILE,),
        in_specs=[pl.BlockSpec((PACK_TILE, n), lambda i, pos: (i, 0))]
        + [pl.BlockSpec((1, PACK_TILE, n), lambda i, pos, k=k: (k, i, 0)) for k in range(n_slot)],
        out_specs=pl.BlockSpec((PACK_TILE, n), lambda i, pos: (i, 0)),
    )
    return pl.pallas_call(
        body, name="mesh_sum", grid_spec=grid_spec, out_shape=jax.ShapeDtypeStruct((rh, n), F32), compiler_params=_cparams(1)
    )(jnp.stack([chip]).astype(jnp.int32), own, *([got] * n_slot))


def sum_devices(name, v):
    def body(v_ref, o_ref):
        acc = v_ref[0]
        for d in range(1, N_DEV):
            acc = acc + v_ref[d]
        o_ref[...] = acc

    return pl.pallas_call(body, name=name, out_shape=jax.ShapeDtypeStruct(v.shape[1:], F32))(v)


def adamw(name, w, g, m, v):
    shape = w.shape
    w2, g2, m2, v2 = (t.reshape(-1, shape[-1]) for t in (w, g, m, v))
    rows, n = w2.shape
    tr = _pick(rows, (256, 128, 64, 32, 16, 8))
    c1 = 1.0 / (1.0 - ADAM_B1**ADAM_STEP)
    c2 = 1.0 / (1.0 - ADAM_B2**ADAM_STEP)

    def body(w_ref, g_ref, m_ref, v_ref, d_ref, nm_ref, nv_ref):
        gv = g_ref[...]
        nm = ADAM_B1 * m_ref[...] + (1.0 - ADAM_B1) * gv
        nv = ADAM_B2 * v_ref[...] + (1.0 - ADAM_B2) * jnp.square(gv)
        d_ref[...] = -ADAM_LR * ((nm * c1) / (jnp.sqrt(nv * c2) + ADAM_EPS) + ADAM_WD * w_ref[...])
        nm_ref[...] = nm
        nv_ref[...] = nv

    spec = pl.BlockSpec((tr, n), lambda i: (i, 0))
    outs = pl.pallas_call(
        body,
        name=name,
        grid=(rows // tr,),
        in_specs=[spec] * 4,
        out_specs=[spec] * 3,
        out_shape=[jax.ShapeDtypeStruct((rows, n), F32)] * 3,
        compiler_params=_cparams(1),
    )(w2, g2, m2, v2)
    return [o.reshape(shape) for o in outs]


def ada_fwd(c_all, w_ada):
    n_tok, d = c_all.shape
    depth, _, cols = w_ada.shape
    tn = _pick(cols, (512, 384, 256, 128))

    def body(c_ref, w_ref, o_ref):
        o_ref[0] = jnp.dot(_silu(c_ref[...]).astype(MXU_DTYPE), w_ref[0].astype(MXU_DTYPE), preferred_element_type=F32)

    return pl.pallas_call(
        body,
        name="ada_fwd",
        grid=(depth, cols // tn),
        in_specs=[pl.BlockSpec((n_tok, d), lambda l, j: (0, 0)), pl.BlockSpec((1, d, tn), lambda l, j: (l, 0, j))],
        out_specs=pl.BlockSpec((1, n_tok, tn), lambda l, j: (l, 0, j)),
        out_shape=jax.ShapeDtypeStruct((depth, n_tok, cols), F32),
        compiler_params=_cparams(2),
    )(c_all, w_ada)


COL_SHARDED = ("w_in", "w_uq", "w_ukv", "w_up")


PART_ROWS = 16


def _part_rows(shape):
    n = int(np.prod(shape)) // PACK_W
    return -(-n // PART_ROWS) * PART_ROWS, n


def _with_fill(parts, axis, rows):
    out, used = [], 0
    for p in parts:
        pad = -p.shape[axis] % PART_ROWS
        out.append(p)
        if pad:
            out.append(jnp.zeros(p.shape[:axis] + (pad,) + p.shape[axis + 1 :], p.dtype))
        used += p.shape[axis] + pad
    out.append(jnp.zeros(parts[0].shape[:axis] + (rows - used,) + parts[0].shape[axis + 1 :], parts[0].dtype))
    return jnp.concatenate(out, axis=axis)


def _pack_rows(parts, rows):
    return _with_fill([p.reshape(-1, PACK_W) for p in parts], 0, rows)


def _big_rows(shards):
    n = sum(_part_rows(shards[k].shape)[0] for k in BIG)
    return -(-n // (2 * PACK_TILE)) * 2 * PACK_TILE


def unpack_gathered(gathered, shard_shapes):
    out, r0 = {}, 0
    for k in BIG:
        shp = shard_shapes[k]
        step, n = _part_rows(shp)
        seg = gathered[:, r0 : r0 + n].reshape(N_CHIPS, *shp)
        r0 += step
        if k in COL_SHARDED:
            out[k] = seg.transpose(1, 2, 0, 3).reshape(shp[0], shp[1], N_CHIPS * shp[2])
        else:
            out[k] = seg.transpose(1, 0, 2, 3).reshape(shp[0], N_CHIPS * shp[1], shp[2])
    return out


def pack_full_grads(grads, shard_shapes, rows):
    parts = []
    for k in BIG:
        _, rows_k, cols_k = shard_shapes[k]
        layers = grads[k]
        if (rows_k * cols_k // PACK_W) % PART_ROWS:
            layers = [jnp.concatenate(layers, axis=0)]
        for g in layers:
            if k in COL_SHARDED:
                g = g.reshape(g.shape[0], N_CHIPS, cols_k).transpose(1, 0, 2)
            else:
                g = g.reshape(-1, N_CHIPS, rows_k, cols_k).transpose(1, 0, 2, 3)
            parts.append(g.reshape(N_CHIPS, -1, PACK_W))
    return _with_fill(parts, 1, rows)


def unpack_shards(buf, shard_shapes):
    out, r0 = {}, 0
    for k in BIG:
        step, n = _part_rows(shard_shapes[k])
        out[k] = buf[r0 : r0 + n].reshape(shard_shapes[k])
        r0 += step
    return out


def _flat_pack(arrs, row_multiple=8):
    flat = jnp.concatenate([a.reshape(-1) for a in arrs])
    per = PACK_W * row_multiple
    n = -(-flat.shape[0] // per) * per
    return jnp.pad(flat, (0, n - flat.shape[0])).reshape(-1, PACK_W)


def _flat_unpack(buf, shapes):
    flat, out, o = buf.reshape(-1), [], 0
    for s in shapes:
        n = int(np.prod(s))
        out.append(flat[o : o + n].reshape(s))
        o += n
    return out


WEIGHTS = (
    "w_ada", "b_ada", "norm_mix", "w_in", "conv_w", "conv_b", "dt_bias", "a_log", "d_skip", "ssd_norm", "q_norm", "w_uq",
    "kv_norm", "w_ukv", "attn_norm", "w_out", "norm_mlp", "w_up", "conv_ff_w", "conv_ff_b", "w_down", "final_norm",
)
REPLICATED = ("b_ada", "norm_mix", "conv_b", "dt_bias", "a_log", "d_skip", "ssd_norm", "q_norm", "kv_norm", "attn_norm",
              "norm_mlp", "conv_ff_b", "final_norm")
CONV_SHARDED = ("conv_w", "conv_ff_w")


def kernel(x, c, positions, w_ada, b_ada, norm_mix, w_in, conv_w, conv_b, dt_bias, a_log, d_skip, ssd_norm, q_norm, w_uq, kv_norm, w_ukv, attn_norm, w_out, norm_mlp, w_up, conv_ff_w, conv_ff_b, w_down, final_norm, loss_target, m_w_ada, m_b_ada, m_norm_mix, m_w_in, m_conv_w, m_conv_b, m_dt_bias, m_a_log, m_d_skip, m_ssd_norm, m_q_norm, m_w_uq, m_kv_norm, m_w_ukv, m_attn_norm, m_w_out, m_norm_mlp, m_w_up, m_conv_ff_w, m_conv_ff_b, m_w_down, m_final_norm, v_w_ada, v_b_ada, v_norm_mix, v_w_in, v_conv_w, v_conv_b, v_dt_bias, v_a_log, v_d_skip, v_ssd_norm, v_q_norm, v_w_uq, v_kv_norm, v_w_ukv, v_attn_norm, v_w_out, v_norm_mlp, v_w_up, v_conv_ff_w, v_conv_ff_b, v_w_down, v_final_norm):
    loc = locals()
    Wl = {k: loc[k] for k in WEIGHTS}
    Ml = {k: loc["m_" + k] for k in WEIGHTS}
    Vl = {k: loc["v_" + k] for k in WEIGHTS}
    xi, yi, ci = lax.axis_index("x"), lax.axis_index("y"), lax.axis_index("c")
    chip = 2 * xi + yi
    dev = 2 * chip + ci
    B, S, D = x.shape
    n_tok = N_DEV * B

    c_all = all_gather8("gather_c", c.reshape(8, -1)).reshape(n_tok, D)
    mod_cols = ada_fwd(c_all, w_ada)
    cols = mod_cols.shape[-1]
    half = n_tok // 2
    mod_mine = lax.dynamic_slice_in_dim(mod_cols, ci * half, half, axis=1)
    small_in = _flat_pack([mod_mine, conv_w, conv_ff_w])
    n_mod = mod_mine.size // PACK_W
    small_all = all_gather8("gather_mod", small_in).reshape(N_CHIPS, 2, -1, PACK_W)
    mod_all = small_all[:, :, :n_mod].reshape(N_CHIPS, 2, DEPTH, half, cols).transpose(2, 1, 3, 0, 4).reshape(DEPTH, n_tok, N_CHIPS * cols)
    mod_raw = lax.dynamic_slice_in_dim(mod_all, dev * B, B, axis=1).reshape(DEPTH, B, 6, D)
    conv_parts = [_flat_unpack(small_all[k, 0, n_mod:], [conv_w.shape, conv_ff_w.shape]) for k in range(N_CHIPS)]
    conv_full = {name: jnp.concatenate([conv_parts[k][i] for k in range(N_CHIPS)], axis=-1) for i, name in enumerate(CONV_SHARDED)}

    shard_shapes = {k: Wl[k].shape for k in BIG}
    rows = _big_rows(Wl)
    pack = _pack_rows([Wl[k].astype(MXU_DTYPE) for k in BIG], rows)
    gathered = lax.dynamic_update_slice_in_dim(gather_weights(pack), pack[None], chip, axis=0)
    W = unpack_gathered(gathered, shard_shapes)
    W.update({k: Wl[k] for k in REPLICATED})
    W.update(conv_full)

    loss_lanes, grad_x, grads, dmod_raw = local_step(x, loss_target, positions, mod_raw, W)
    loss = lax.psum(loss_lanes[0, 0], ("x", "y", "c"))

    gpack = pack_full_grads(grads, shard_shapes, rows)
    wire, own = chip_sum(gpack, swap_halves(gpack), ci, chip)
    mine = mesh_sum(own, scatter_chips(wire), chip)
    theirs = swap_with_sibling(mine)
    both = jnp.concatenate([jnp.where(ci == 0, mine, theirs), jnp.where(ci == 0, theirs, mine)], axis=0)
    g_shard = unpack_shards(both, shard_shapes)

    small_names = REPLICATED + CONV_SHARDED
    small_out = _flat_pack([grads[k] for k in small_names] + [dmod_raw])
    small_got = all_gather8("gather_small", small_out).reshape(N_DEV, -1, PACK_W)
    small_sum = _flat_unpack(sum_devices("sum_small", small_got), [grads[k].shape for k in small_names])
    G = dict(zip(small_names, small_sum))
    for name in CONV_SHARDED:
        width = Wl[name].shape[-1]
        G[name] = lax.dynamic_slice_in_dim(G[name], chip * width, width, axis=-1)
    n_small = sum(grads[k].size for k in small_names)
    dmod_all = small_got.reshape(N_DEV, -1)[:, n_small : n_small + dmod_raw.size].reshape(N_DEV, DEPTH, B, 6 * D)
    dmod_all = dmod_all.transpose(1, 0, 2, 3).reshape(DEPTH, n_tok, 6 * D)
    dmod_cols = lax.dynamic_slice_in_dim(dmod_all, chip * cols, cols, axis=-1)
    G["w_ada"] = jnp.stack([matmul(f"w_ada_wgrad{l}", c_all, dmod_cols[l], ta=True, a_act=_silu) for l in range(DEPTH)])
    G.update(g_shard)

    deltas, new_m, new_v = {}, {}, {}
    small_upd = adamw("adamw_small", *[_flat_pack([t[k] for k in REPLICATED]) for t in (Wl, G, Ml, Vl)])
    for res, t in zip((deltas, new_m, new_v), small_upd):
        res.update(zip(REPLICATED, _flat_unpack(t, [Wl[k].shape for k in REPLICATED])))
    for k in WEIGHTS:
        if k not in REPLICATED:
            deltas[k], new_m[k], new_v[k] = adamw("adamw_" + k, Wl[k], G[k], Ml[k], Vl[k])
    return (loss, grad_x, *[G[k] for k in WEIGHTS], *[deltas[k] for k in WEIGHTS], *[new_m[k] for k in WEIGHTS], *[new_v[k] for k in WEIGHTS])
```

```python
import functools
from typing import NamedTuple

import numpy as np
import jax
import jax.numpy as jnp
from jax import lax
from jax.experimental import pallas as pl
from jax.experimental.pallas import tpu as pltpu

F32 = jnp.float32
BF16 = jnp.bfloat16
MXU_DTYPE = jnp.bfloat16
WIRE_DTYPE = jnp.bfloat16
HIGHEST = lax.Precision.HIGHEST

D_MODEL = 1024
DEPTH = 2
D_SSD = 1024
SSD_HEADS = 16
SSD_HEAD_DIM = 64
SSD_GROUPS = 2
SSD_STATE = 128
SSD_CONV = 4
CHUNK = 128
MLA_HEADS = 8
QK_NOPE = 128
QK_ROPE = 64
V_DIM = 128
D_ATT = MLA_HEADS * V_DIM
Q_RANK = 384
KV_RANK = 256
ROPE_BASE = 10000.0
D_FF = 2816
FF_CONV = 3
EPS = 1e-6
D_XBC = D_SSD + 2 * SSD_GROUPS * SSD_STATE
D_IN = D_SSD + D_XBC + SSD_HEADS + Q_RANK + KV_RANK + QK_ROPE
ADAM_LR, ADAM_B1, ADAM_B2, ADAM_EPS, ADAM_WD, ADAM_STEP = 0.001, 0.9, 0.999, 1e-08, 0.01, 10

LANE = 128
N_CHIPS = 4
N_DEV = 8

OFF_Z, OFF_XBC, OFF_CKV, OFF_KR, OFF_DT, OFF_CQ = 0, 1024, 2560, 2816, 2944, 3072
N_PROJ = 3456
Q_HEAD_PAD = 256
VMEM_LIMIT = 56 * 1024 * 1024


def _cparams(n_axes):
    return pltpu.CompilerParams(dimension_semantics=("arbitrary",) * n_axes, vmem_limit_bytes=VMEM_LIMIT)


def _pick(n, prefs):
    for p in prefs:
        if n % p == 0:
            return p
    return n


def _silu(x):
    return x * jax.nn.sigmoid(x)


def _rms(x, g):
    return x * lax.rsqrt(jnp.mean(x * x, axis=-1, keepdims=True) + EPS) * g


def _softplus(x):
    return jnp.maximum(x, 0.0) + jnp.log(1.0 + jnp.exp(-jnp.abs(x)))


class Col(NamedTuple):
    arr: jax.Array
    width: int
    blk: int


def _as_col(a):
    return a if isinstance(a, Col) else Col(a, a.shape[-1], 0)


def _row_specs(tiled, pbatch, shared, tile):
    specs = [pl.BlockSpec((1, tile, t.width), lambda b, i, blk=t.blk: (b, i, blk)) for t in tiled]
    specs += [pl.BlockSpec((1, 1, p.shape[-1]), lambda b, i: (b, 0, 0)) for p in pbatch]
    specs += [pl.BlockSpec(s.shape, lambda b, i: (0, 0)) for s in shared]
    return specs


def _row_vals(refs, nt, npb):
    return [r[0].astype(F32) for r in refs[: nt + npb]] + [r[...].astype(F32) for r in refs[nt + npb :]]


def rowwise_fwd(name, fn, tiled, pbatch, shared, outs, tile=512):
    tiled = [_as_col(t) for t in tiled]
    B, S = tiled[0].arr.shape[:2]
    tile = min(tile, S)
    nt, npb, nsh = len(tiled), len(pbatch), len(shared)
    n_in = nt + npb + nsh

    def body(*refs):
        res = fn(*_row_vals(refs[:n_in], nt, npb))
        for r, v in zip(refs[n_in:], res):
            r[0] = v.astype(r.dtype)

    return pl.pallas_call(
        body,
        name=name,
        grid=(B, S // tile),
        in_specs=_row_specs(tiled, pbatch, shared, tile),
        out_specs=[pl.BlockSpec((1, tile, w), lambda b, i: (b, i, 0)) for w, _ in outs],
        out_shape=[jax.ShapeDtypeStruct((B, S, w), dt) for w, dt in outs],
        compiler_params=_cparams(2),
    )(*[t.arr for t in tiled], *pbatch, *shared)


def rowwise_bwd(name, fn, tiled, pbatch, shared, cts, grad_tiled, grad_shared, adds=None, tile=256, mxu_only=()):
    tiled = [_as_col(t) for t in tiled]
    adds = adds or {}
    B, S = tiled[0].arr.shape[:2]
    tile = min(tile, S)
    nt, npb, nsh = len(tiled), len(pbatch), len(shared)
    n_in = nt + npb + nsh
    add_idx = sorted(adds)
    gt = [i for i in range(nt) if grad_tiled[i]]
    gs = [i for i in range(nsh) if grad_shared[i]]
    diff = gt + [nt + i for i in range(npb)] + [nt + npb + i for i in gs]
    n_ct, n_add = len(cts), len(add_idx)

    def body(*refs):
        vals = _row_vals(refs[:n_in], nt, npb)
        ct_v = tuple(r[0].astype(F32) if c.ndim == 3 else jnp.sum(r[0].astype(F32), axis=0) for r, c in zip(refs[n_in : n_in + n_ct], cts))
        add_v = {i: r[0].astype(F32) for i, r in zip(add_idx, refs[n_in + n_ct : n_in + n_ct + n_add])}
        out_refs = refs[n_in + n_ct + n_add :]

        def f(*dargs):
            full = list(vals)
            for k, i in enumerate(diff):
                full[i] = dargs[k]
            return tuple(fn(*full))

        _, vjp = jax.vjp(f, *[vals[i] for i in diff])
        grads = vjp(ct_v)
        b, i = pl.program_id(0), pl.program_id(1)
        k = 0
        for idx in gt:
            g = grads[k]
            if idx in add_v:
                g = g + add_v[idx]
            out_refs[k][0] = g.astype(out_refs[k].dtype)
            k += 1
        for _ in range(npb):
            r, g = out_refs[k], grads[k]

            @pl.when(i == 0)
            def _(r=r, g=g):
                r[0] = g

            @pl.when(i > 0)
            def _(r=r, g=g):
                r[0] += g

            k += 1
        for _ in gs:
            r, g = out_refs[k], grads[k]

            @pl.when((i == 0) & (b == 0))
            def _(r=r, g=g):
                r[...] = g

            @pl.when((i > 0) | (b > 0))
            def _(r=r, g=g):
                r[...] += g

            k += 1

    out_specs = [pl.BlockSpec((1, tile, tiled[i].width), lambda b, i: (b, i, 0)) for i in gt]
    out_shape = [jax.ShapeDtypeStruct((B, S, tiled[i].width), MXU_DTYPE if i in mxu_only else F32) for i in gt]
    out_specs += [pl.BlockSpec((1, 1, p.shape[-1]), lambda b, i: (b, 0, 0)) for p in pbatch]
    out_shape += [jax.ShapeDtypeStruct(p.shape, F32) for p in pbatch]
    out_specs += [pl.BlockSpec(shared[i].shape, lambda b, i: (0, 0)) for i in gs]
    out_shape += [jax.ShapeDtypeStruct(shared[i].shape, F32) for i in gs]
    in_specs = _row_specs(tiled, pbatch, shared, tile)
    for c in cts:
        if c.ndim == 3:
            in_specs.append(pl.BlockSpec((1, tile, c.shape[-1]), lambda b, i: (b, i, 0)))
        else:
            in_specs.append(pl.BlockSpec((1, c.shape[1], tile, c.shape[-1]), lambda b, i: (b, 0, i, 0)))
    in_specs += [pl.BlockSpec((1, tile, adds[i].shape[-1]), lambda b, i: (b, i, 0)) for i in add_idx]
    return pl.pallas_call(
        body,
        name=name,
        grid=(B, S // tile),
        in_specs=in_specs,
        out_specs=out_specs,
        out_shape=out_shape,
        compiler_params=_cparams(2),
    )(*[t.arr for t in tiled], *pbatch, *shared, *cts, *[adds[i] for i in add_idx])


def fn_prenorm(x, sc, sh, g, bsc, bsh):
    return (_rms(x, g) * (1.0 + sc + bsc) + (sh + bsh),)


def fn_resid_prenorm(x, y, gate, sc, sh, g, bgate, bsc, bsh):
    x1 = x + (gate + bgate) * y
    return x1, _rms(x1, g) * (1.0 + sc + bsc) + (sh + bsh)


def fn_mix(yscan, z, o, ssd_norm, attn_norm):
    y = yscan * _silu(z)
    half = D_SSD // SSD_GROUPS
    first = lax.broadcasted_iota(jnp.int32, y.shape, 1) < half
    sq = y * y
    m0 = jnp.sum(jnp.where(first, sq, 0.0), axis=-1, keepdims=True) / half
    m1 = jnp.sum(jnp.where(first, 0.0, sq), axis=-1, keepdims=True) / half
    r = jnp.where(first, lax.rsqrt(m0 + EPS), lax.rsqrt(m1 + EPS))
    return (jnp.concatenate([y * r * ssd_norm, _rms(o, attn_norm)], axis=-1),)


def fn_mla_prep(cq, ckv, kr, cos_t, sin_t, q_norm, kv_norm, rot):
    krr = kr * cos_t + jnp.dot(kr, rot, precision=HIGHEST, preferred_element_type=F32) * sin_t
    return _rms(cq, q_norm), _rms(ckv, kv_norm), krr


def final_fwdbwd(x1, ff, target, gate, fnorm, bgate, tile=256):
    B, S, D = x1.shape
    tile = min(tile, S)

    def body(x_ref, f_ref, t_ref, g_ref, n_ref, bg_ref, loss_ref, dx_ref, df_ref, dg_ref, dn_ref, dbg_ref):
        b, i = pl.program_id(0), pl.program_id(1)
        tgt = t_ref[0]

        def f(x, y, gate, fn, bg):
            yf = _rms(x + (gate + bg) * y, fn)
            return 0.5 * jnp.sum(jnp.mean(jnp.square(yf - tgt), axis=-1, keepdims=True), axis=0, keepdims=True)

        val, vjp = jax.vjp(f, x_ref[0], f_ref[0], g_ref[0], n_ref[...], bg_ref[...])
        dx, dff, dg, dn, dbg = vjp(jnp.ones((1, 1), F32))
        dx_ref[0] = dx
        df_ref[0] = dff.astype(df_ref.dtype)
        lane_loss = jnp.broadcast_to(val, (1, LANE))

        @pl.when(i == 0)
        def _():
            dg_ref[0] = dg

        @pl.when(i > 0)
        def _():
            dg_ref[0] += dg

        @pl.when((i == 0) & (b == 0))
        def _():
            dn_ref[...] = dn
            dbg_ref[...] = dbg
            loss_ref[...] = lane_loss

        @pl.when((i > 0) | (b > 0))
        def _():
            dn_ref[...] += dn
            dbg_ref[...] += dbg
            loss_ref[...] += lane_loss

    tok = pl.BlockSpec((1, tile, D), lambda b, i: (b, i, 0))
    pb = pl.BlockSpec((1, 1, D), lambda b, i: (b, 0, 0))
    sh = pl.BlockSpec((1, D), lambda b, i: (0, 0))
    return pl.pallas_call(
        body,
        name="final_loss",
        grid=(B, S // tile),
        in_specs=[tok, tok, tok, pb, sh, sh],
        out_specs=[pl.BlockSpec((1, LANE), lambda b, i: (0, 0)), tok, tok, pb, sh, sh],
        out_shape=[
            jax.ShapeDtypeStruct((1, LANE), F32),
            jax.ShapeDtypeStruct((B, S, D), F32),
            jax.ShapeDtypeStruct((B, S, D), MXU_DTYPE),
            jax.ShapeDtypeStruct((B, 1, D), F32),
            jax.ShapeDtypeStruct((1, D), F32),
            jax.ShapeDtypeStruct((1, D), F32),
        ],
        compiler_params=_cparams(2),
    )(x1, ff, target, gate, fnorm, bgate)


MATMUL_VMEM_BUDGET = 40 * 1024 * 1024
STEP_COST_BYTES = 1.2e6
MXU_DIM = 256
MXU_FLOPS_PER_BYTE = 280.0


def _tile_options(n, cap):
    opts = [d for d in range(LANE, min(n, cap) + 1, LANE) if n % d == 0]
    return opts or [n]


def _matmul_tiles(M, N, K, sa, sb, so):
    best = None
    for tk in [K] + [d for d in _tile_options(K, 2048) if d >= 512 and d < K]:
        nk = K // tk
        for tm in _tile_options(M, 2048):
            for tn in _tile_options(N, 1408):
                acc = 0 if (so == 4 or nk == 1) else tm * tn * 4
                vmem = 2 * (tm * tk * sa + tk * tn * sb + tm * tn * so) + acc + tm * tn * 4 + 2 * (tm * tk + tk * tn)
                if vmem > MATMUL_VMEM_BUDGET:
                    continue
                a_reads = M * K * sa * (1 if nk == 1 else N // tn)
                b_reads = K * N * sb * (1 if (nk == 1 and N == tn) else M // tm)
                steps = (M // tm) * (N // tn) * nk
                hbm = a_reads + b_reads + M * N * so + (2 * M * N * 4 * (nk - 1) if nk > 1 else 0) / 8
                fill = (-(-tn // MXU_DIM) * MXU_DIM / tn) * (-(-tk // MXU_DIM) * MXU_DIM / tk)
                mxu = 2.0 * M * N * K * fill / MXU_FLOPS_PER_BYTE
                cost = max(hbm, mxu) + steps * STEP_COST_BYTES
                if best is None or cost < best[0]:
                    best = (cost, tm, tn, tk)
    assert best is not None, (M, N, K)
    return best[1:]


class Layer(NamedTuple):
    arr: jax.Array
    l: int


def matmul(name, a, b, ta=False, tb=False, out_dtype=F32, a_act=None):
    pieces = list(a) if isinstance(a, (list, tuple)) else [a]
    assert len(pieces) == 1 or not ta
    lead = pieces[0].shape[:2] if (pieces[0].ndim == 3 and not ta) else None
    pieces = [p.reshape(-1, p.shape[-1]) if p.ndim == 3 else p for p in pieces]
    layer = None
    if isinstance(b, Layer):
        b, layer = b
        b_rows, b_cols = b.shape[1:]
    else:
        if b.ndim == 3:
            b = b.reshape(-1, b.shape[-1])
        b_rows, b_cols = b.shape
    Kp, M = pieces[0].shape if ta else pieces[0].shape[::-1]
    N = b_rows if tb else b_cols
    assert (b_cols if tb else b_rows) == Kp * len(pieces), (name, pieces[0].shape, b.shape)
    tm, tn, tk = _matmul_tiles(M, N, Kp, pieces[0].dtype.itemsize, b.dtype.itemsize, jnp.dtype(out_dtype).itemsize)
    npk = Kp // tk
    nk = npk * len(pieces)
    dims = (((0 if ta else 1,), (1 if tb else 0,)), ((), ()))
    direct = jnp.dtype(out_dtype) == jnp.dtype(F32)

    def body(*refs):
        a_refs, b_ref, o_ref, scratch = refs[: len(pieces)], refs[len(pieces)], refs[len(pieces) + 1], refs[len(pieces) + 2 :]
        k = pl.program_id(2)

        def product(a_ref):
            av = a_ref[...]
            if a_act is not None:
                av = a_act(av.astype(F32))
            return lax.dot_general(av.astype(MXU_DTYPE), b_ref[...].astype(MXU_DTYPE), dims, preferred_element_type=F32)

        if nk == 1:
            o_ref[...] = product(a_refs[0]).astype(o_ref.dtype)
            return
        acc = o_ref if direct else scratch[0]

        @pl.when(k == 0)
        def _():
            acc[...] = product(a_refs[0])

        for p, a_ref in enumerate(a_refs):
            lo, hi = max(p * npk, 1), (p + 1) * npk
            if lo < hi:

                @pl.when((k >= lo) & (k < hi))
                def _(a_ref=a_ref):
                    acc[...] += product(a_ref)

        if not direct:

            @pl.when(k == nk - 1)
            def _():
                o_ref[...] = acc[...].astype(o_ref.dtype)

    def a_spec(p):
        kk = lambda k: jnp.clip(k - p * npk, 0, npk - 1)
        return pl.BlockSpec((tk, tm), lambda i, j, k: (kk(k), i)) if ta else pl.BlockSpec((tm, tk), lambda i, j, k: (i, kk(k)))

    b_block, b_index = ((tn, tk), lambda i, j, k: (j, k)) if tb else ((tk, tn), lambda i, j, k: (k, j))
    if layer is None:
        b_spec = pl.BlockSpec(b_block, b_index)
    else:
        b_spec = pl.BlockSpec((None, *b_block), lambda i, j, k: (layer, *b_index(i, j, k)))
    out = pl.pallas_call(
        body,
        name=name,
        grid=(M // tm, N // tn, nk),
        in_specs=[a_spec(p) for p in range(len(pieces))] + [b_spec],
        out_specs=pl.BlockSpec((tm, tn), lambda i, j, k: (i, j)),
        out_shape=jax.ShapeDtypeStruct((M, N), out_dtype),
        scratch_shapes=[] if (direct or nk == 1) else [pltpu.VMEM((tm, tn), F32)],
        compiler_params=_cparams(3),
    )(*pieces, b)
    return out.reshape(*lead, N) if lead is not None else out


SUBLANES = 8


def _shift_down(u, d):
    if d == 0:
        return u
    r = pltpu.roll(u, d, 0)
    t = lax.broadcasted_iota(jnp.int32, (SUBLANES, u.shape[1]), 0)
    return jnp.concatenate([jnp.where(t >= d, r[:SUBLANES], 0.0), r[SUBLANES:]], axis=0)


def _shift_up(u, d):
    if d == 0:
        return u
    s = u.shape[0]
    r = pltpu.roll(u, s - d, 0)
    t = lax.broadcasted_iota(jnp.int32, (SUBLANES, u.shape[1]), 0)
    return jnp.concatenate([r[: s - SUBLANES], jnp.where(t < SUBLANES - d, r[s - SUBLANES :], 0.0)], axis=0)


def _conv(u, w, b):
    k = w.shape[0]
    out = b + w[k - 1 : k, :] * u
    for j in range(k - 1):
        out = out + w[j : j + 1, :] * _shift_down(u, k - 1 - j)
    return out


def _conv_bwd(u, w, dc):
    k = w.shape[0]
    du = w[k - 1 : k, :] * dc
    dws = []
    for j in range(k - 1):
        du = du + w[j : j + 1, :] * _shift_up(dc, k - 1 - j)
        dws.append(jnp.sum(dc * _shift_down(u, k - 1 - j), axis=0, keepdims=True))
    dws.append(jnp.sum(dc * u, axis=0, keepdims=True))
    return du, jnp.concatenate(dws, axis=0), jnp.sum(dc, axis=0, keepdims=True)


def conv_silu_fwd(proj, w, b):
    B, S, _ = proj.shape
    k, c = w.shape
    blk0 = OFF_XBC // LANE

    def body(u_ref, w_ref, b_ref, o_ref):
        o_ref[0] = _silu(_conv(u_ref[0], w_ref[...], b_ref[...]))

    return pl.pallas_call(
        body,
        name="conv_silu_fwd",
        grid=(B, c // LANE),
        in_specs=[
            pl.BlockSpec((1, S, LANE), lambda bi, j: (bi, 0, blk0 + j)),
            pl.BlockSpec((k, LANE), lambda bi, j: (0, j)),
            pl.BlockSpec((1, LANE), lambda bi, j: (0, j)),
        ],
        out_specs=pl.BlockSpec((1, S, LANE), lambda bi, j: (bi, 0, j)),
        out_shape=jax.ShapeDtypeStruct((B, S, c), F32),
        compiler_params=_cparams(2),
    )(proj, w, b)


def conv_silu_bwd(proj, w, b, douts):
    B, S, _ = proj.shape
    k, c = w.shape
    blk0 = OFF_XBC // LANE
    ends = np.cumsum([d.shape[-1] // LANE for d in douts]).tolist()
    starts = [0] + ends[:-1]

    def body(u_ref, w_ref, b_ref, *refs):
        d_refs, (du_ref, dw_ref, db_ref) = refs[: len(douts)], refs[len(douts) :]
        j, bi = pl.program_id(0), pl.program_id(1)
        u, wv = u_ref[0], w_ref[...]
        cv = _conv(u, wv, b_ref[...])
        sg = jax.nn.sigmoid(cv)
        dout = d_refs[-1][0]
        for r in reversed(range(len(douts) - 1)):
            dout = jnp.where(j < ends[r], d_refs[r][0], dout)
        dc = dout * (sg * (1.0 + cv * (1.0 - sg)))
        du, dw, db = _conv_bwd(u, wv, dc)
        du_ref[0] = du.astype(du_ref.dtype)

        @pl.when(bi == 0)
        def _():
            dw_ref[...] = dw
            db_ref[...] = db

        @pl.when(bi > 0)
        def _():
            dw_ref[...] += dw
            db_ref[...] += db

    return pl.pallas_call(
        body,
        name="conv_silu_bwd",
        grid=(c // LANE, B),
        in_specs=[
            pl.BlockSpec((1, S, LANE), lambda j, bi: (bi, 0, blk0 + j)),
            pl.BlockSpec((k, LANE), lambda j, bi: (0, j)),
            pl.BlockSpec((1, LANE), lambda j, bi: (0, j)),
        ]
        + [
            pl.BlockSpec((1, S, LANE), lambda j, bi, lo=lo, hi=hi: (bi, 0, jnp.clip(j, lo, hi - 1) - lo))
            for lo, hi in zip(starts, ends)
        ],
        out_specs=[
            pl.BlockSpec((1, S, LANE), lambda j, bi: (bi, 0, j)),
            pl.BlockSpec((k, LANE), lambda j, bi: (0, j)),
            pl.BlockSpec((1, LANE), lambda j, bi: (0, j)),
        ],
        out_shape=[
            jax.ShapeDtypeStruct((B, S, c), MXU_DTYPE),
            jax.ShapeDtypeStruct((k, c), F32),
            jax.ShapeDtypeStruct((1, c), F32),
        ],
        compiler_params=_cparams(2),
    )(proj, w, b, *douts)


def _glu_specs(S, k, nb, batch_col):
    specs = []
    for off in (0, nb):
        specs.append(pl.BlockSpec((1, S, LANE), lambda *ids, off=off: (batch_col(*ids)[0], 0, off + batch_col(*ids)[1])))
    for rows in (k, 1):
        for off in (0, nb):
            specs.append(pl.BlockSpec((rows, LANE), lambda *ids, off=off: (0, off + batch_col(*ids)[1])))
    return specs


def conv_glu_fwd(u, w, b):
    B, S, c2 = u.shape
    k = w.shape[0]
    nb = c2 // 2 // LANE

    def body(ug_ref, uv_ref, wg_ref, wv_ref, bg_ref, bv_ref, o_ref):
        gate = _conv(ug_ref[0], wg_ref[...], bg_ref[...])
        val = _conv(uv_ref[0], wv_ref[...], bv_ref[...])
        o_ref[0] = (_silu(gate) * val).astype(o_ref.dtype)

    return pl.pallas_call(
        body,
        name="conv_glu_fwd",
        grid=(B, nb),
        in_specs=_glu_specs(S, k, nb, lambda bi, j: (bi, j)),
        out_specs=pl.BlockSpec((1, S, LANE), lambda bi, j: (bi, 0, j)),
        out_shape=jax.ShapeDtypeStruct((B, S, c2 // 2), MXU_DTYPE),
        compiler_params=_cparams(2),
    )(u, u, w, w, b, b)


def conv_glu_bwd(u, w, b, da):
    B, S, c2 = u.shape
    k = w.shape[0]
    nb = c2 // 2 // LANE

    def body(ug_ref, uv_ref, wg_ref, wv_ref, bg_ref, bv_ref, d_ref, dug_ref, duv_ref, dwg_ref, dwv_ref, dbg_ref, dbv_ref):
        bi = pl.program_id(1)
        ug, uv, wg, wv = ug_ref[0], uv_ref[0], wg_ref[...], wv_ref[...]
        gate, val = _conv(ug, wg, bg_ref[...]), _conv(uv, wv, bv_ref[...])
        sg = jax.nn.sigmoid(gate)
        dav = d_ref[0].astype(F32)
        dug, dwg, dbg = _conv_bwd(ug, wg, dav * val * (sg * (1.0 + gate * (1.0 - sg))))
        duv, dwv, dbv = _conv_bwd(uv, wv, dav * gate * sg)
        dug_ref[0] = dug.astype(dug_ref.dtype)
        duv_ref[0] = duv.astype(duv_ref.dtype)

        @pl.when(bi == 0)
        def _():
            dwg_ref[...], dwv_ref[...], dbg_ref[...], dbv_ref[...] = dwg, dwv, dbg, dbv

        @pl.when(bi > 0)
        def _():
            dwg_ref[...] += dwg
            dwv_ref[...] += dwv
            dbg_ref[...] += dbg
            dbv_ref[...] += dbv

    tok = pl.BlockSpec((1, S, LANE), lambda j, bi: (bi, 0, j))
    outs = pl.pallas_call(
        body,
        name="conv_glu_bwd",
        grid=(nb, B),
        in_specs=_glu_specs(S, k, nb, lambda j, bi: (bi, j)) + [tok],
        out_specs=[tok, tok] + [pl.BlockSpec((rows, LANE), lambda j, bi: (0, j)) for rows in (k, k, 1, 1)],
        out_shape=[jax.ShapeDtypeStruct((B, S, c2 // 2), MXU_DTYPE)] * 2
        + [jax.ShapeDtypeStruct((rows, c2 // 2), F32) for rows in (k, k, 1, 1)],
        compiler_params=_cparams(2),
    )(u, u, w, w, b, b, da)
    dug, duv, dwg, dwv, dbg, dbv = outs
    return dug, duv, jnp.concatenate([dwg, dwv], axis=1), jnp.concatenate([dbg, dbv], axis=1)


N_PAIR = SSD_HEADS // SSD_GROUPS // 2


def _ssd_chunk(prev, xs, bms, cms, dtraw, dt_bias, a_log, d_skip):
    L = CHUNK
    lane = lax.broadcasted_iota(jnp.int32, (1, LANE), 1)
    sub = lax.broadcasted_iota(jnp.int32, (LANE, 1), 0)
    row = lax.broadcasted_iota(jnp.int32, (L, L), 0)
    col = lax.broadcasted_iota(jnp.int32, (L, L), 1)
    tri = (row >= col).astype(F32)
    first = lane < SSD_HEAD_DIM

    dt = _softplus(dtraw + dt_bias)
    a = dt * (-jnp.exp(a_log))
    acs = jnp.dot(tri, a, precision=HIGHEST, preferred_element_type=F32)
    acs_t = acs.T
    a_end = jnp.sum(a, axis=0, keepdims=True)

    def lane_of(v, h):
        return jnp.sum(jnp.where(lane == h, v, 0.0), axis=1, keepdims=True)

    def expand(v, ha):
        return jnp.where(first, lane_of(v, ha), lane_of(v, ha + 1))

    ys, news = [], []
    for g in range(SSD_GROUPS):
        bm, cm = bms[g].astype(MXU_DTYPE), cms[g].astype(MXU_DTYPE)
        cb = lax.dot_general(cm, bm, (((1,), (1,)), ((), ())), preferred_element_type=F32)
        for j in range(N_PAIR):
            ha = g * (SSD_HEADS // SSD_GROUPS) + 2 * j
            x, state = xs[g * N_PAIR + j], prev[g * N_PAIR + j]
            dt_e, acs_e, end_e = expand(dt, ha), expand(acs, ha), expand(a_end, ha)
            xdt = x * dt_e
            y = jnp.dot(cm, state.astype(MXU_DTYPE), preferred_element_type=F32) * jnp.exp(acs_e)
            st = lax.dot_general(bm, (xdt * jnp.exp(end_e - acs_e)).astype(MXU_DTYPE), (((0,), (0,)), ((), ())), preferred_element_type=F32)
            news.append(state * jnp.exp(end_e) + st)
            for hh in range(2):
                h = ha + hh
                seg = lane_of(acs, h) - jnp.sum(jnp.where(sub == h, acs_t, 0.0), axis=0, keepdims=True)
                decay = jnp.exp(jnp.where(row >= col, seg, -jnp.inf))
                xh = jnp.where(first if hh == 0 else jnp.logical_not(first), xdt, 0.0)
                y = y + jnp.dot((cb * decay).astype(MXU_DTYPE), xh.astype(MXU_DTYPE), preferred_element_type=F32)
            ys.append(y + x * expand(d_skip, ha))
    return ys, news


N_PAIRS = SSD_GROUPS * N_PAIR


def _ssd_specs(chunk_of):
    bc = SSD_GROUPS * SSD_STATE
    par = pl.BlockSpec((1, LANE), lambda *ids: (0, 0))
    return [
        pl.BlockSpec((1, CHUNK, D_SSD), lambda *ids: (*chunk_of(*ids), 0)),
        pl.BlockSpec((1, CHUNK, bc), lambda *ids: (*chunk_of(*ids), D_SSD // bc)),
        pl.BlockSpec((1, CHUNK, bc), lambda *ids: (*chunk_of(*ids), D_SSD // bc + 1)),
        pl.BlockSpec((1, CHUNK, LANE), lambda *ids: (*chunk_of(*ids), OFF_DT // LANE)),
        par,
        par,
        par,
    ]


def _pair_cols(ref):
    return [ref[0, :, k * LANE : (k + 1) * LANE] for k in range(N_PAIRS)]


def _group_cols(ref):
    return [ref[0, :, g * SSD_STATE : (g + 1) * SSD_STATE] for g in range(SSD_GROUPS)]


def ssd_fwd(xbc_act, proj, dt_bias, a_log, d_skip):
    B, S, _ = xbc_act.shape
    nc = S // CHUNK

    def body(xs_ref, bm_ref, cm_ref, dt_ref, db_ref, al_ref, ds_ref, y_ref, st_ref, state):
        @pl.when(pl.program_id(1) == 0)
        def _():
            state[...] = jnp.zeros_like(state)

        prev = [state[k] for k in range(N_PAIRS)]
        for k in range(N_PAIRS):
            st_ref[0, 0, k] = prev[k]
        ys, news = _ssd_chunk(prev, _pair_cols(xs_ref), _group_cols(bm_ref), _group_cols(cm_ref), dt_ref[0], db_ref[...], al_ref[...], ds_ref[...])
        for k in range(N_PAIRS):
            y_ref[0, :, k * LANE : (k + 1) * LANE] = ys[k]
            state[k] = news[k]

    return pl.pallas_call(
        body,
        name="ssd_fwd",
        grid=(B, nc),
        in_specs=_ssd_specs(lambda b, c: (b, c)),
        out_specs=[
            pl.BlockSpec((1, CHUNK, D_SSD), lambda b, c: (b, c, 0)),
            pl.BlockSpec((1, 1, N_PAIRS, SSD_STATE, LANE), lambda b, c: (b, c, 0, 0, 0)),
        ],
        out_shape=[
            jax.ShapeDtypeStruct((B, S, D_SSD), F32),
            jax.ShapeDtypeStruct((B, nc, N_PAIRS, SSD_STATE, LANE), F32),
        ],
        scratch_shapes=[pltpu.VMEM((N_PAIRS, SSD_STATE, LANE), F32)],
        compiler_params=_cparams(2),
    )(xbc_act, xbc_act, xbc_act, proj, dt_bias, a_log, d_skip)


def ssd_bwd(xbc_act, proj, dt_bias, a_log, d_skip, states, dy):
    B, S, _ = xbc_act.shape
    nc = S // CHUNK
    bc = SSD_GROUPS * SSD_STATE
    chunk_of = lambda b, cr: (b, nc - 1 - cr)

    def body(xs_ref, bm_ref, cm_ref, dt_ref, db_ref, al_ref, ds_ref, st_ref, dy_ref,
             dxs_ref, dbm_ref, dcm_ref, ddt_ref, ddb_ref, dal_ref, dds_ref, dstate):
        b, cr = pl.program_id(0), pl.program_id(1)

        @pl.when(cr == 0)
        def _():
            dstate[...] = jnp.zeros_like(dstate)

        prev = [st_ref[0, 0, k] for k in range(N_PAIRS)]
        _, vjp = jax.vjp(
            _ssd_chunk, prev, _pair_cols(xs_ref), _group_cols(bm_ref), _group_cols(cm_ref), dt_ref[0], db_ref[...], al_ref[...], ds_ref[...]
        )
        dprev, dxs, dbm, dcm, ddt, ddb, dal, dds = vjp((_pair_cols(dy_ref), [dstate[k] for k in range(N_PAIRS)]))
        for k in range(N_PAIRS):
            dstate[k] = dprev[k]
            dxs_ref[0, :, k * LANE : (k + 1) * LANE] = dxs[k]
        for g in range(SSD_GROUPS):
            dbm_ref[0, :, g * SSD_STATE : (g + 1) * SSD_STATE] = dbm[g]
            dcm_ref[0, :, g * SSD_STATE : (g + 1) * SSD_STATE] = dcm[g]
        ddt_ref[0] = ddt
        first = (b == 0) & (cr == 0)

        @pl.when(first)
        def _():
            ddb_ref[...] = ddb
            dal_ref[...] = dal
            dds_ref[...] = dds

        @pl.when(jnp.logical_not(first))
        def _():
            ddb_ref[...] += ddb
            dal_ref[...] += dal
            dds_ref[...] += dds

    par = pl.BlockSpec((1, LANE), lambda *ids: (0, 0))
    in_specs = _ssd_specs(chunk_of) + [
        pl.BlockSpec((1, 1, N_PAIRS, SSD_STATE, LANE), lambda b, cr: (*chunk_of(b, cr), 0, 0, 0)),
        pl.BlockSpec((1, CHUNK, D_SSD), lambda b, cr: (*chunk_of(b, cr), 0)),
    ]
    return pl.pallas_call(
        body,
        name="ssd_bwd",
        grid=(B, nc),
        in_specs=in_specs,
        out_specs=[
            pl.BlockSpec((1, CHUNK, D_SSD), lambda b, cr: (*chunk_of(b, cr), 0)),
            pl.BlockSpec((1, CHUNK, bc), lambda b, cr: (*chunk_of(b, cr), 0)),
            pl.BlockSpec((1, CHUNK, bc), lambda b, cr: (*chunk_of(b, cr), 0)),
            pl.BlockSpec((1, CHUNK, LANE), lambda b, cr: (*chunk_of(b, cr), 0)),
            par,
            par,
            par,
        ],
        out_shape=[
            jax.ShapeDtypeStruct((B, S, D_SSD), F32),
            jax.ShapeDtypeStruct((B, S, bc), F32),
            jax.ShapeDtypeStruct((B, S, bc), F32),
            jax.ShapeDtypeStruct((B, S, LANE), F32),
            jax.ShapeDtypeStruct((1, LANE), F32),
            jax.ShapeDtypeStruct((1, LANE), F32),
            jax.ShapeDtypeStruct((1, LANE), F32),
        ],
        scratch_shapes=[pltpu.VMEM((N_PAIRS, SSD_STATE, LANE), F32)],
        compiler_params=_cparams(2),
    )(xbc_act, xbc_act, xbc_act, proj, dt_bias, a_log, d_skip, states, dy)


ATT_SCALE = (QK_NOPE + QK_ROPE) ** -0.5
ATT_TILE = 512


ATT_HEADS_FWD = 4
ATT_HEADS_BWD = 4
LOG2E = 1.4426950408889634
Q_PRESCALE = ATT_SCALE * LOG2E


def fn_rope_q(q, cos_t, sin_t, rot):
    parts = []
    for h in range(MLA_HEADS):
        qr = q[:, h * Q_HEAD_PAD + LANE : (h + 1) * Q_HEAD_PAD]
        qr = qr * cos_t + jnp.dot(qr, rot, precision=HIGHEST, preferred_element_type=F32) * sin_t
        parts += [q[:, h * Q_HEAD_PAD : h * Q_HEAD_PAD + LANE] * Q_PRESCALE, qr * Q_PRESCALE]
    return (jnp.concatenate(parts, axis=1),)


def _head_cols(ref, hh, width):
    return ref[0, :, hh * width : (hh + 1) * width]


def _causal_mask(s):
    return jnp.where(lax.broadcasted_iota(jnp.int32, s.shape, 0) >= lax.broadcasted_iota(jnp.int32, s.shape, 1), s, -jnp.inf)


def _nt(a, b):
    return lax.dot_general(a, b, (((1,), (1,)), ((), ())), preferred_element_type=F32)


def _tn(a, b):
    return lax.dot_general(a, b, (((0,), (0,)), ((), ())), preferred_element_type=F32)


def _per_tile(idx, n, variant):
    for v in range(n):

        @pl.when(idx == v)
        def _(v=v):
            variant(v)


def attention_fwd(qs, kv, krr):
    B, S, _ = qs.shape
    t = min(ATT_TILE, S)
    n = S // t
    hps = ATT_HEADS_FWD

    def body(q_ref, k_ref, kr_ref, o_ref, lse_ref):
        def variant(v):
            spans = ([(0, v * t, False)] if v else []) + [(v * t, (v + 1) * t, True)]
            for hh in range(hps):
                q = _head_cols(q_ref, hh, Q_HEAD_PAD)
                c0 = hh * 2 * LANE
                ss = []
                for a, b, diag in spans:
                    s = _nt(q, jnp.concatenate([k_ref[0, a:b, c0 : c0 + LANE], kr_ref[0, a:b]], axis=1))
                    ss.append(_causal_mask(s) if diag else s)
                m = functools.reduce(jnp.maximum, [jnp.max(s, axis=1, keepdims=True) for s in ss])
                ps = [jnp.exp2(s - m) for s in ss]
                l = functools.reduce(jnp.add, [jnp.sum(p, axis=1, keepdims=True) for p in ps])
                acc = functools.reduce(
                    jnp.add,
                    [jnp.dot(p.astype(MXU_DTYPE), k_ref[0, a:b, c0 + LANE : c0 + 2 * LANE], preferred_element_type=F32) for p, (a, b, _) in zip(ps, spans)],
                )
                o_ref[0, :, hh * V_DIM : (hh + 1) * V_DIM] = acc / l
                lse_ref[0, hh] = m + jnp.log2(l)

        _per_tile(pl.program_id(2), n, variant)

    return pl.pallas_call(
        body,
        name="attention_fwd",
        grid=(B, MLA_HEADS // hps, n),
        in_specs=[
            pl.BlockSpec((1, t, hps * Q_HEAD_PAD), lambda b, h, i: (b, i, h)),
            pl.BlockSpec((1, S, hps * 2 * LANE), lambda b, h, i: (b, 0, h)),
            pl.BlockSpec((1, S, LANE), lambda b, h, i: (b, 0, 0)),
        ],
        out_specs=[
            pl.BlockSpec((1, t, hps * V_DIM), lambda b, h, i: (b, i, h)),
            pl.BlockSpec((1, hps, t, 1), lambda b, h, i: (b, h, i, 0)),
        ],
        out_shape=[jax.ShapeDtypeStruct((B, S, D_ATT), F32), jax.ShapeDtypeStruct((B, MLA_HEADS, S, 1), F32)],
        compiler_params=_cparams(3),
    )(qs, kv, krr)


def attention_bwd(qs, kv, krr, cos_t, sin_t, rot_t, o, lse, do):
    B, S, _ = qs.shape
    t = min(ATT_TILE, S)
    n = S // t
    hps = ATT_HEADS_BWD

    def body(q_ref, c_ref, s_ref, rott_ref, k_ref, kr_ref, o_ref, lse_ref, do_ref, dq_ref, dkv_ref, dkr_ref, dk_acc, dv_acc):
        i = pl.program_id(2)

        @pl.when(i == 0)
        def _():
            dk_acc[...] = jnp.zeros_like(dk_acc)
            dv_acc[...] = jnp.zeros_like(dv_acc)

        def variant(v):
            spans = ([(0, v * t, False)] if v else []) + [(v * t, (v + 1) * t, True)]
            for hh in range(hps):
                q = _head_cols(q_ref, hh, Q_HEAD_PAD)
                dov = _head_cols(do_ref, hh, V_DIM)
                delta = jnp.sum(_head_cols(o_ref, hh, V_DIM) * dov, axis=1, keepdims=True)
                dob = dov.astype(MXU_DTYPE)
                c0 = hh * 2 * LANE
                acc = None
                for a, b, diag in spans:
                    k = jnp.concatenate([k_ref[0, a:b, c0 : c0 + LANE], kr_ref[0, a:b]], axis=1)
                    s = _nt(q, k)
                    p = jnp.exp2((_causal_mask(s) if diag else s) - lse_ref[0, hh])
                    ds = (p * (_nt(dob, k_ref[0, a:b, c0 + LANE : c0 + 2 * LANE]) - delta) * ATT_SCALE).astype(MXU_DTYPE)
                    part = jnp.dot(ds, k, preferred_element_type=F32)
                    acc = part if acc is None else acc + part
                    dk_acc[hh, a:b] += _tn(ds, q)
                    dv_acc[hh, a:b] += _tn(p.astype(MXU_DTYPE), dob)
                r = acc[:, LANE:]
                dr = r * c_ref[0] + jnp.dot(r * s_ref[0], rott_ref[...], precision=HIGHEST, preferred_element_type=F32)
                dq_ref[0, :, hh * Q_HEAD_PAD : (hh + 1) * Q_HEAD_PAD] = jnp.concatenate([acc[:, :LANE], dr], axis=1).astype(dq_ref.dtype)

        _per_tile(i, n, variant)

        @pl.when(i == n - 1)
        def _():
            dkr = None
            for hh in range(hps):
                dk = dk_acc[hh] * (1.0 / Q_PRESCALE)
                dkv_ref[0, :, hh * 2 * LANE : (hh + 1) * 2 * LANE] = jnp.concatenate([dk[:, :LANE], dv_acc[hh]], axis=1).astype(dkv_ref.dtype)
                dkr = dk[:, LANE:] if dkr is None else dkr + dk[:, LANE:]
            dkr_ref[0, 0] = dkr

    return pl.pallas_call(
        body,
        name="attention_bwd",
        grid=(B, MLA_HEADS // hps, n),
        in_specs=[
            pl.BlockSpec((1, t, hps * Q_HEAD_PAD), lambda b, h, i: (b, i, h)),
            pl.BlockSpec((1, t, LANE), lambda b, h, i: (b, i, 0)),
            pl.BlockSpec((1, t, LANE), lambda b, h, i: (b, i, 0)),
            pl.BlockSpec((LANE, LANE), lambda b, h, i: (0, 0)),
            pl.BlockSpec((1, S, hps * 2 * LANE), lambda b, h, i: (b, 0, h)),
            pl.BlockSpec((1, S, LANE), lambda b, h, i: (b, 0, 0)),
            pl.BlockSpec((1, t, hps * V_DIM), lambda b, h, i: (b, i, h)),
            pl.BlockSpec((1, hps, t, 1), lambda b, h, i: (b, h, i, 0)),
            pl.BlockSpec((1, t, hps * V_DIM), lambda b, h, i: (b, i, h)),
        ],
        out_specs=[
            pl.BlockSpec((1, t, hps * Q_HEAD_PAD), lambda b, h, i: (b, i, h)),
            pl.BlockSpec((1, S, hps * 2 * LANE), lambda b, h, i: (b, 0, h)),
            pl.BlockSpec((1, 1, S, LANE), lambda b, h, i: (b, h, 0, 0)),
        ],
        out_shape=[
            jax.ShapeDtypeStruct(qs.shape, MXU_DTYPE),
            jax.ShapeDtypeStruct(kv.shape, MXU_DTYPE),
            jax.ShapeDtypeStruct((B, MLA_HEADS // hps, S, LANE), F32),
        ],
        scratch_shapes=[pltpu.VMEM((hps, S, Q_HEAD_PAD), F32), pltpu.VMEM((hps, S, V_DIM), F32)],
        compiler_params=_cparams(3),
    )(qs, cos_t, sin_t, rot_t, kv, krr, o, lse, do)


_IN_SPLITS = np.cumsum([D_SSD, D_XBC, SSD_HEADS, Q_RANK, KV_RANK]).tolist()


def _pad_last(t, n):
    return jnp.pad(t, [(0, 0)] * (t.ndim - 1) + [(0, n - t.shape[-1])])


def win_to_kernel(w):
    z, xbc, dt, cq, ckv, kr = jnp.split(w, _IN_SPLITS, axis=-1)
    return jnp.concatenate([z, xbc, ckv, _pad_last(kr, LANE), _pad_last(dt, LANE), cq], axis=-1)


def win_from_kernel(g):
    z, xbc = g[..., :OFF_XBC], g[..., OFF_XBC:OFF_CKV]
    ckv, kr = g[..., OFF_CKV:OFF_KR], g[..., OFF_KR : OFF_KR + QK_ROPE]
    dt, cq = g[..., OFF_DT : OFF_DT + SSD_HEADS], g[..., OFF_CQ:]
    return jnp.concatenate([z, xbc, dt, cq, ckv, kr], axis=-1)


def wuq_to_kernel(w):
    w = w.reshape(*w.shape[:-1], MLA_HEADS, QK_NOPE + QK_ROPE)
    return _pad_last(w, Q_HEAD_PAD).reshape(*w.shape[:-2], MLA_HEADS * Q_HEAD_PAD)


def wuq_from_kernel(g):
    g = g.reshape(*g.shape[:-1], MLA_HEADS, Q_HEAD_PAD)[..., : QK_NOPE + QK_ROPE]
    return g.reshape(*g.shape[:-2], MLA_HEADS * (QK_NOPE + QK_ROPE))


def _lane_pad_row(v):
    return _pad_last(v, LANE)[None, :]


def rope_tables(positions):
    inv_freq = jnp.asarray(1.0 / (ROPE_BASE ** (np.arange(0, QK_ROPE, 2, dtype=np.float32) / QK_ROPE)))
    ang = positions.astype(F32)[..., None] * inv_freq
    cos, sin = jnp.cos(ang), jnp.sin(ang)
    zeros = jnp.zeros(cos.shape[:-1] + (LANE - QK_ROPE,), F32)
    rot = np.zeros((LANE, LANE), np.float32)
    half = QK_ROPE // 2
    for j in range(half):
        rot[j + half, j] = -1.0
        rot[j, j + half] = 1.0
    return jnp.concatenate([cos, cos, zeros], -1), jnp.concatenate([sin, sin, zeros], -1), jnp.asarray(rot), jnp.asarray(rot.T)


BIG = ("w_in", "w_uq", "w_ukv", "w_out", "w_up", "w_down")


def local_step(x, target, positions, mod_raw, W):
    B, S, D = x.shape
    cos_t, sin_t, rot, rot_t = rope_tables(positions)
    row = lambda v: v.reshape(1, -1)
    w_in_k, w_uq_k = win_to_kernel(W["w_in"]), wuq_to_kernel(W["w_uq"])
    Ls = []
    for l in range(DEPTH):
        mods = [mod_raw[l, :, k][:, None, :] for k in range(6)]
        bias = [row(W["b_ada"][l, k * D : (k + 1) * D]) for k in range(6)]
        Ls.append(
            dict(
                mods=mods,
                bias=bias,
                w_in=Layer(w_in_k, l),
                w_uq=Layer(w_uq_k, l),
                w_ukv=Layer(W["w_ukv"], l),
                w_out=Layer(W["w_out"], l),
                w_up=Layer(W["w_up"], l),
                w_down=Layer(W["w_down"], l),
                conv_w=W["conv_w"][l],
                conv_b=row(W["conv_b"][l]),
                conv_ff_w=W["conv_ff_w"][l],
                conv_ff_b=row(W["conv_ff_b"][l]),
                dt_bias=_lane_pad_row(W["dt_bias"][l]),
                a_log=_lane_pad_row(W["a_log"][l]),
                d_skip=_lane_pad_row(W["d_skip"][l]),
                norm_mix=row(W["norm_mix"][l]),
                ssd_norm=row(W["ssd_norm"][l]),
                q_norm=row(W["q_norm"][l]),
                kv_norm=row(W["kv_norm"][l]),
                attn_norm=row(W["attn_norm"][l]),
                norm_mlp=row(W["norm_mlp"][l]),
            )
        )
    fnorm = row(W["final_norm"])

    p0 = Ls[0]
    (h1,) = rowwise_fwd(
        "prenorm_fwd", fn_prenorm, [x], [p0["mods"][1], p0["mods"][0]], [p0["norm_mix"], p0["bias"][1], p0["bias"][0]], [(D, MXU_DTYPE)]
    )
    xin = x
    for l, p in enumerate(Ls):
        s = p["saved"] = dict(xin=xin, h1=h1)
        s["proj"] = proj = matmul(f"w_in_fwd{l}", h1, p["w_in"])
        s["xbc_act"] = xbc_act = conv_silu_fwd(proj, p["conv_w"], p["conv_b"])
        s["yscan"], s["states"] = ssd_fwd(xbc_act, proj, p["dt_bias"], p["a_log"], p["d_skip"])
        mla_in = [Col(proj, Q_RANK, OFF_CQ // Q_RANK), Col(proj, KV_RANK, OFF_CKV // KV_RANK), Col(proj, LANE, OFF_KR // LANE), cos_t, sin_t]
        s["cqn"], s["ckvn"], s["krr"] = rowwise_fwd(
            f"mla_prep_fwd{l}", fn_mla_prep, mla_in, [], [p["q_norm"], p["kv_norm"], rot], [(Q_RANK, MXU_DTYPE), (KV_RANK, MXU_DTYPE), (LANE, MXU_DTYPE)]
        )
        q = matmul(f"w_uq_fwd{l}", s["cqn"], p["w_uq"], out_dtype=MXU_DTYPE)
        (s["qs"],) = rowwise_fwd(f"rope_q_fwd{l}", fn_rope_q, [q, cos_t, sin_t], [], [rot], [(q.shape[-1], MXU_DTYPE)])
        s["kv"] = matmul(f"w_ukv_fwd{l}", s["ckvn"], p["w_ukv"], out_dtype=MXU_DTYPE)
        s["o"], s["lse"] = attention_fwd(s["qs"], s["kv"], s["krr"])
        (s["ycat"],) = rowwise_fwd(
            f"mix_fwd{l}", fn_mix, [s["yscan"], Col(proj, D_SSD, 0), s["o"]], [], [p["ssd_norm"], p["attn_norm"]], [(D_SSD + D_ATT, MXU_DTYPE)]
        )
        s["ymix"] = matmul(f"w_out_fwd{l}", s["ycat"], p["w_out"])
        s["x1"], s["h2"] = rowwise_fwd(
            f"mid_fwd{l}",
            fn_resid_prenorm,
            [xin, s["ymix"]],
            [p["mods"][2], p["mods"][4], p["mods"][3]],
            [p["norm_mlp"], p["bias"][2], p["bias"][4], p["bias"][3]],
            [(D, F32), (D, MXU_DTYPE)],
        )
        s["u"] = matmul(f"w_up_fwd{l}", s["h2"], p["w_up"])
        s["a"] = conv_glu_fwd(s["u"], p["conv_ff_w"], p["conv_ff_b"])
        s["ff"] = matmul(f"w_down_fwd{l}", s["a"], p["w_down"])
        if l + 1 < DEPTH:
            n = Ls[l + 1]
            xin, h1 = rowwise_fwd(
                f"join_fwd{l}",
                fn_resid_prenorm,
                [s["x1"], s["ff"]],
                [p["mods"][5], n["mods"][1], n["mods"][0]],
                [n["norm_mix"], p["bias"][5], n["bias"][1], n["bias"][0]],
                [(D, F32), (D, MXU_DTYPE)],
            )

    G = {k: [None] * DEPTH for k in W if k not in ("final_norm", "w_ada")}
    dmod = [[None] * 6 for _ in range(DEPTH)]
    dbias = [[None] * 6 for _ in range(DEPTH)]
    last = Ls[-1]
    sl = last["saved"]
    loss, dx1, dff, dmod[-1][5], G["final_norm"], dbias[-1][5] = final_fwdbwd(
        sl["x1"], sl["ff"], target, last["mods"][5], fnorm, last["bias"][5]
    )
    grad_x = None
    for l in reversed(range(DEPTH)):
        p = Ls[l]
        s = p["saved"]
        da = matmul(f"w_down_dgrad{l}", dff, p["w_down"], tb=True)
        G["w_down"][l] = matmul(f"w_down_wgrad{l}", s["a"], dff, ta=True, out_dtype=WIRE_DTYPE)
        du_gate, du_val, G["conv_ff_w"][l], dcb = conv_glu_bwd(s["u"], p["conv_ff_w"], p["conv_ff_b"], da)
        G["conv_ff_b"][l] = dcb[0]
        dh2 = matmul(f"w_up_dgrad{l}", [du_gate, du_val], p["w_up"], tb=True)
        G["w_up"][l] = jnp.concatenate(
            [matmul(f"w_up_wgrad{l}_{half}", s["h2"], d, ta=True, out_dtype=WIRE_DTYPE) for half, d in (("gate", du_gate), ("val", du_val))],
            axis=1,
        )
        dxb, dymix, dmod[l][2], dmod[l][4], dmod[l][3], G["norm_mlp"][l], dbias[l][2], dbias[l][4], dbias[l][3] = rowwise_bwd(
            f"mid_bwd{l}",
            fn_resid_prenorm,
            [s["xin"], s["ymix"]],
            [p["mods"][2], p["mods"][4], p["mods"][3]],
            [p["norm_mlp"], p["bias"][2], p["bias"][4], p["bias"][3]],
            [dx1, dh2],
            [True, True],
            [True] * 4,
            mxu_only=(1,),
        )
        dycat = matmul(f"w_out_dgrad{l}", dymix, p["w_out"], tb=True)
        G["w_out"][l] = matmul(f"w_out_wgrad{l}", s["ycat"], dymix, ta=True, out_dtype=WIRE_DTYPE)
        dyscan, dz, do, G["ssd_norm"][l], G["attn_norm"][l] = rowwise_bwd(
            f"mix_bwd{l}", fn_mix, [s["yscan"], Col(s["proj"], D_SSD, 0), s["o"]], [], [p["ssd_norm"], p["attn_norm"]], [dycat], [True] * 3, [True] * 2, mxu_only=(1,)
        )
        dq, dkv, dkrr = attention_bwd(s["qs"], s["kv"], s["krr"], cos_t, sin_t, rot_t, s["o"], s["lse"], do)
        dcqn = matmul(f"w_uq_dgrad{l}", dq, p["w_uq"], tb=True)
        G["w_uq"][l] = wuq_from_kernel(matmul(f"w_uq_wgrad{l}", s["cqn"], dq, ta=True, out_dtype=WIRE_DTYPE))
        dckvn = matmul(f"w_ukv_dgrad{l}", dkv, p["w_ukv"], tb=True)
        G["w_ukv"][l] = matmul(f"w_ukv_wgrad{l}", s["ckvn"], dkv, ta=True, out_dtype=WIRE_DTYPE)
        proj = s["proj"]
        mla_in = [Col(proj, Q_RANK, OFF_CQ // Q_RANK), Col(proj, KV_RANK, OFF_CKV // KV_RANK), Col(proj, LANE, OFF_KR // LANE), cos_t, sin_t]
        dcq, dckv, dkr, G["q_norm"][l], G["kv_norm"][l] = rowwise_bwd(
            f"mla_prep_bwd{l}",
            fn_mla_prep,
            mla_in,
            [],
            [p["q_norm"], p["kv_norm"], rot],
            [dcqn, dckvn, dkrr],
            [True, True, True, False, False],
            [True, True, False],
            mxu_only=(0, 1, 2),
        )
        dxs, dbm, dcm, ddt, ddb, dal, dds = ssd_bwd(s["xbc_act"], proj, p["dt_bias"], p["a_log"], p["d_skip"], s["states"], dyscan)
        G["dt_bias"][l], G["a_log"][l], G["d_skip"][l] = ddb[0, :SSD_HEADS], dal[0, :SSD_HEADS], dds[0, :SSD_HEADS]
        dxbc, G["conv_w"][l], dcb = conv_silu_bwd(proj, p["conv_w"], p["conv_b"], [dxs, dbm, dcm])
        G["conv_b"][l] = dcb[0]
        dproj = jnp.concatenate([dz, dxbc, dckv, dkr, ddt.astype(MXU_DTYPE), dcq], axis=-1)
        dh1 = matmul(f"w_in_dgrad{l}", dproj, p["w_in"], tb=True)
        G["w_in"][l] = win_from_kernel(matmul(f"w_in_wgrad{l}", s["h1"], dproj, ta=True, out_dtype=WIRE_DTYPE))
        if l > 0:
            q = Ls[l - 1]
            sq = q["saved"]
            dx1, dff, dmod[l - 1][5], dmod[l][1], dmod[l][0], G["norm_mix"][l], dbias[l - 1][5], dbias[l][1], dbias[l][0] = rowwise_bwd(
                f"join_bwd{l - 1}",
                fn_resid_prenorm,
                [sq["x1"], sq["ff"]],
                [q["mods"][5], p["mods"][1], p["mods"][0]],
                [p["norm_mix"], q["bias"][5], p["bias"][1], p["bias"][0]],
                [dxb, dh1],
                [True, True],
                [True] * 4,
                mxu_only=(1,),
            )
        else:
            grad_x, dmod[0][1], dmod[0][0], G["norm_mix"][0], dbias[0][1], dbias[0][0] = rowwise_bwd(
                "prenorm_bwd",
                fn_prenorm,
                [x],
                [p["mods"][1], p["mods"][0]],
                [p["norm_mix"], p["bias"][1], p["bias"][0]],
                [dh1],
                [True],
                [True] * 3,
                adds={0: dxb},
            )
    for l in range(DEPTH):
        G["b_ada"][l] = jnp.concatenate([d[0] for d in dbias[l]])
        for k in ("norm_mix", "ssd_norm", "q_norm", "kv_norm", "attn_norm", "norm_mlp"):
            G[k][l] = G[k][l][0]
    grads = {k: (v if k in BIG else jnp.stack(v) if isinstance(v, list) else v[0]) for k, v in G.items()}
    dmod_raw = jnp.stack([jnp.stack([d[:, 0, :] for d in dmod[l]], axis=1) for l in range(DEPTH)])
    return loss, grad_x, grads, dmod_raw


MESH = pl.DeviceIdType.MESH
ANY = pl.BlockSpec(memory_space=pl.ANY)
PACK_W = 1024
PACK_TILE = 768


def _place():
    x, y, c = lax.axis_index("x"), lax.axis_index("y"), lax.axis_index("c")
    chips = [(1 - x, y), (x, 1 - y), (1 - x, 1 - y)]
    return x, y, c, chips


def _remote(src, dst, send_sem, recv_sem, to):
    return pltpu.make_async_remote_copy(src_ref=src, dst_ref=dst, send_sem=send_sem, recv_sem=recv_sem, device_id=to, device_id_type=MESH)


def all_gather8(name, v):
    m_per, n = v.shape

    def body(x_ref, out_ref, send_sems, recv_sems, local_sem):
        x, y, c, chips = _place()
        me, sibling = (x, y, c), (x, y, 1 - c)

        def rows(px, py, pc):
            return out_ref.at[pl.ds((4 * px + 2 * py + pc) * m_per, m_per), :]

        def copy(k, block, to, src=None):
            return _remote(rows(*block) if src is None else src, rows(*block), send_sems.at[k], recv_sems.at[k], to)

        mine = pltpu.make_async_copy(x_ref, rows(*me), local_sem)
        mine.start()
        first = [copy(0, me, sibling, src=x_ref)]
        first += [copy(1 + j, me, (*chip, c), src=x_ref) for j, chip in enumerate(chips)]
        for cp in first:
            cp.start()
        passed = [copy(4 + j, (*chip, c), sibling) for j, chip in enumerate(chips)]
        for j, chip in enumerate(chips):
            copy(1 + j, (*chip, c), me).wait_recv()
            passed[j].start()
        copy(0, sibling, me).wait_recv()
        for j, chip in enumerate(chips):
            copy(4 + j, (*chip, 1 - c), me).wait_recv()
        for cp in first + passed:
            cp.wait_send()
        mine.wait()

    return pl.pallas_call(
        body,
        name=name,
        out_shape=jax.ShapeDtypeStruct((N_DEV * m_per, n), v.dtype),
        in_specs=[pl.BlockSpec(memory_space=pltpu.VMEM)],
        out_specs=pl.BlockSpec(memory_space=pltpu.VMEM),
        scratch_shapes=[pltpu.SemaphoreType.DMA((7,)), pltpu.SemaphoreType.DMA((7,)), pltpu.SemaphoreType.DMA],
        compiler_params=pltpu.CompilerParams(vmem_limit_bytes=VMEM_LIMIT),
    )(v)


def gather_weights(pack):
    R, n = pack.shape
    rh = R // 2

    def body(x_ref, out_ref, send_sems, recv_sems):
        x, y, c, chips = _place()
        me = 2 * x + y

        def half(chip, hc):
            return out_ref.at[chip, pl.ds(hc * rh, rh), :]

        src = x_ref.at[pl.ds(c * rh, rh), :]
        first = [_remote(src, half(me, c), send_sems.at[j], recv_sems.at[j], (px, py, c)) for j, (px, py) in enumerate(chips)]
        for cp in first:
            cp.start()
        passed = []
        for j, (px, py) in enumerate(chips):
            got = half(2 * px + py, c)
            _remote(got, got, send_sems.at[j], recv_sems.at[j], (px, py, c)).wait_recv()
            cp = _remote(got, got, send_sems.at[3 + j], recv_sems.at[3 + j], (x, y, 1 - c))
            cp.start()
            passed.append(cp)
        for j, (px, py) in enumerate(chips):
            got = half(2 * px + py, 1 - c)
            _remote(got, got, send_sems.at[3 + j], recv_sems.at[3 + j], (x, y, 1 - c)).wait_recv()
        for cp in first + passed:
            cp.wait_send()

    return pl.pallas_call(
        body,
        name="gather_weights",
        out_shape=jax.ShapeDtypeStruct((N_CHIPS, R, n), pack.dtype),
        in_specs=[ANY],
        out_specs=ANY,
        scratch_shapes=[pltpu.SemaphoreType.DMA((6,)), pltpu.SemaphoreType.DMA((6,))],
    )(pack)


def swap_halves(g):
    n_slot, R, n = g.shape
    rh = R // 2

    def body(g_ref, got_ref, send_sem, recv_sem):
        x, y, c, _ = _place()
        cp = _remote(g_ref.at[:, pl.ds((1 - c) * rh, rh), :], got_ref, send_sem, recv_sem, (x, y, 1 - c))
        cp.start()
        cp.wait()

    return pl.pallas_call(
        body,
        name="swap_halves",
        out_shape=jax.ShapeDtypeStruct((n_slot, rh, n), g.dtype),
        in_specs=[ANY],
        out_specs=ANY,
        scratch_shapes=[pltpu.SemaphoreType.DMA, pltpu.SemaphoreType.DMA],
    )(g)


def scatter_chips(buf):
    def body(s_ref, got_ref, send_sems, recv_sems):
        x, y, c, chips = _place()
        me = 2 * x + y
        cps = [_remote(s_ref.at[2 * px + py], got_ref.at[me], send_sems.at[j], recv_sems.at[j], (px, py, c)) for j, (px, py) in enumerate(chips)]
        for cp in cps:
            cp.start()
        for j, (px, py) in enumerate(chips):
            got = got_ref.at[2 * px + py]
            _remote(got, got, send_sems.at[j], recv_sems.at[j], (px, py, c)).wait_recv()
        for cp in cps:
            cp.wait_send()

    return pl.pallas_call(
        body,
        name="scatter_chips",
        out_shape=jax.ShapeDtypeStruct(buf.shape, buf.dtype),
        in_specs=[ANY],
        out_specs=ANY,
        scratch_shapes=[pltpu.SemaphoreType.DMA((3,)), pltpu.SemaphoreType.DMA((3,))],
    )(buf)


def swap_with_sibling(h):
    def body(h_ref, got_ref, send_sem, recv_sem):
        x, y, c, _ = _place()
        cp = _remote(h_ref, got_ref, send_sem, recv_sem, (x, y, 1 - c))
        cp.start()
        cp.wait()

    return pl.pallas_call(
        body,
        name="swap_with_sibling",
        out_shape=jax.ShapeDtypeStruct(h.shape, h.dtype),
        in_specs=[ANY],
        out_specs=ANY,
        scratch_shapes=[pltpu.SemaphoreType.DMA, pltpu.SemaphoreType.DMA],
    )(h)


def chip_sum(g, got, core, chip):
    n_slot, R, n = g.shape
    rh = R // 2
    nb = rh // PACK_TILE

    def body(pos, g_ref, got_ref, wire_ref, own_ref):
        k = pl.program_id(1)
        s = g_ref[0].astype(F32) + got_ref[0].astype(F32)
        wire_ref[0] = s.astype(wire_ref.dtype)

        @pl.when(k == pos[1])
        def _():
            own_ref[...] = s

    grid_spec = pltpu.PrefetchScalarGridSpec(
        num_scalar_prefetch=1,
        grid=(nb, n_slot),
        in_specs=[
            pl.BlockSpec((1, PACK_TILE, n), lambda i, k, pos: (k, pos[0] * nb + i, 0)),
            pl.BlockSpec((1, PACK_TILE, n), lambda i, k, pos: (k, i, 0)),
        ],
        out_specs=[
            pl.BlockSpec((1, PACK_TILE, n), lambda i, k, pos: (k, i, 0)),
            pl.BlockSpec((PACK_TILE, n), lambda i, k, pos: (i, 0)),
        ],
    )
    return pl.pallas_call(
        body,
        name="chip_sum",
        grid_spec=grid_spec,
        out_shape=[jax.ShapeDtypeStruct((n_slot, rh, n), WIRE_DTYPE), jax.ShapeDtypeStruct((rh, n), F32)],
        compiler_params=_cparams(2),
    )(jnp.stack([core, chip]).astype(jnp.int32), g, got)


def mesh_sum(own, got, chip):
    rh, n = own.shape
    n_slot = got.shape[0]

    def body(pos, own_ref, *refs):
        out_ref = refs[-1]
        acc = own_ref[...]
        for k in range(n_slot):
            acc = acc + jnp.where(k != pos[0], refs[k][0].astype(F32), 0.0)
        out_ref[...] = acc

    grid_spec = pltpu.PrefetchScalarGridSpec(
        num_scalar_prefetch=1,
        grid=(rh // PACK_TILE,),
        in_specs=[pl.BlockSpec((PACK_TILE, n), lambda i, pos: (i, 0))]
        + [pl.BlockSpec((1, PACK_TILE, n), lambda i, pos, k=k: (k, i, 0)) for k in range(n_slot)],
        out_specs=pl.BlockSpec((PACK_TILE, n), lambda i, pos: (i, 0)),
    )
    return pl.pallas_call(
        body, name="mesh_sum", grid_spec=grid_spec, out_shape=jax.ShapeDtypeStruct((rh, n), F32), compiler_params=_cparams(1)
    )(jnp.stack([chip]).astype(jnp.int32), own, *([got] * n_slot))


def sum_devices(name, v):
    def body(v_ref, o_ref):
        acc = v_ref[0]
        for d in range(1, N_DEV):
            acc = acc + v_ref[d]
        o_ref[...] = acc

    return pl.pallas_call(body, name=name, out_shape=jax.ShapeDtypeStruct(v.shape[1:], F32))(v)


def adamw(name, w, g, m, v):
    shape = w.shape
    w2, g2, m2, v2 = (t.reshape(-1, shape[-1]) for t in (w, g, m, v))
    rows, n = w2.shape
    tr = _pick(rows, (256, 128, 64, 32, 16, 8))
    c1 = 1.0 / (1.0 - ADAM_B1**ADAM_STEP)
    c2 = 1.0 / (1.0 - ADAM_B2**ADAM_STEP)

    def body(w_ref, g_ref, m_ref, v_ref, d_ref, nm_ref, nv_ref):
        gv = g_ref[...]
        nm = ADAM_B1 * m_ref[...] + (1.0 - ADAM_B1) * gv
        nv = ADAM_B2 * v_ref[...] + (1.0 - ADAM_B2) * jnp.square(gv)
        d_ref[...] = -ADAM_LR * ((nm * c1) / (jnp.sqrt(nv * c2) + ADAM_EPS) + ADAM_WD * w_ref[...])
        nm_ref[...] = nm
        nv_ref[...] = nv

    spec = pl.BlockSpec((tr, n), lambda i: (i, 0))
    outs = pl.pallas_call(
        body,
        name=name,
        grid=(rows // tr,),
        in_specs=[spec] * 4,
        out_specs=[spec] * 3,
        out_shape=[jax.ShapeDtypeStruct((rows, n), F32)] * 3,
        compiler_params=_cparams(1),
    )(w2, g2, m2, v2)
    return [o.reshape(shape) for o in outs]


def ada_fwd(c_all, w_ada):
    n_tok, d = c_all.shape
    depth, _, cols = w_ada.shape
    tn = _pick(cols, (512, 384, 256, 128))

    def body(c_ref, w_ref, o_ref):
        o_ref[0] = jnp.dot(_silu(c_ref[...]).astype(MXU_DTYPE), w_ref[0].astype(MXU_DTYPE), preferred_element_type=F32)

    return pl.pallas_call(
        body,
        name="ada_fwd",
        grid=(depth, cols // tn),
        in_specs=[pl.BlockSpec((n_tok, d), lambda l, j: (0, 0)), pl.BlockSpec((1, d, tn), lambda l, j: (l, 0, j))],
        out_specs=pl.BlockSpec((1, n_tok, tn), lambda l, j: (l, 0, j)),
        out_shape=jax.ShapeDtypeStruct((depth, n_tok, cols), F32),
        compiler_params=_cparams(2),
    )(c_all, w_ada)


COL_SHARDED = ("w_in", "w_uq", "w_ukv", "w_up")


PART_ROWS = 16


def _part_rows(shape):
    n = int(np.prod(shape)) // PACK_W
    return -(-n // PART_ROWS) * PART_ROWS, n


def _with_fill(parts, axis, rows):
    out, used = [], 0
    for p in parts:
        pad = -p.shape[axis] % PART_ROWS
        out.append(p)
        if pad:
            out.append(jnp.zeros(p.shape[:axis] + (pad,) + p.shape[axis + 1 :], p.dtype))
        used += p.shape[axis] + pad
    out.append(jnp.zeros(parts[0].shape[:axis] + (rows - used,) + parts[0].shape[axis + 1 :], parts[0].dtype))
    return jnp.concatenate(out, axis=axis)


def _pack_rows(parts, rows):
    return _with_fill([p.reshape(-1, PACK_W) for p in parts], 0, rows)


def _big_rows(shards):
    n = sum(_part_rows(shards[k].shape)[0] for k in BIG)
    return -(-n // (2 * PACK_TILE)) * 2 * PACK_TILE


def unpack_gathered(gathered, shard_shapes):
    out, r0 = {}, 0
    for k in BIG:
        shp = shard_shapes[k]
        step, n = _part_rows(shp)
        seg = gathered[:, r0 : r0 + n].reshape(N_CHIPS, *shp)
        r0 += step
        if k in COL_SHARDED:
            out[k] = seg.transpose(1, 2, 0, 3).reshape(shp[0], shp[1], N_CHIPS * shp[2])
        else:
            out[k] = seg.transpose(1, 0, 2, 3).reshape(shp[0], N_CHIPS * shp[1], shp[2])
    return out


def pack_full_grads(grads, shard_shapes, rows):
    parts = []
    for k in BIG:
        _, rows_k, cols_k = shard_shapes[k]
        layers = grads[k]
        if (rows_k * cols_k // PACK_W) % PART_ROWS:
            layers = [jnp.concatenate(layers, axis=0)]
        for g in layers:
            if k in COL_SHARDED:
                g = g.reshape(g.shape[0], N_CHIPS, cols_k).transpose(1, 0, 2)
            else:
                g = g.reshape(-1, N_CHIPS, rows_k, cols_k).transpose(1, 0, 2, 3)
            parts.append(g.reshape(N_CHIPS, -1, PACK_W))
    return _with_fill(parts, 1, rows)


def unpack_shards(buf, shard_shapes):
    out, r0 = {}, 0
    for k in BIG:
        step, n = _part_rows(shard_shapes[k])
        out[k] = buf[r0 : r0 + n].reshape(shard_shapes[k])
        r0 += step
    return out


def _flat_pack(arrs, row_multiple=8):
    flat = jnp.concatenate([a.reshape(-1) for a in arrs])
    per = PACK_W * row_multiple
    n = -(-flat.shape[0] // per) * per
    return jnp.pad(flat, (0, n - flat.shape[0])).reshape(-1, PACK_W)


def _flat_unpack(buf, shapes):
    flat, out, o = buf.reshape(-1), [], 0
    for s in shapes:
        n = int(np.prod(s))
        out.append(flat[o : o + n].reshape(s))
        o += n
    return out


WEIGHTS = (
    "w_ada", "b_ada", "norm_mix", "w_in", "conv_w", "conv_b", "dt_bias", "a_log", "d_skip", "ssd_norm", "q_norm", "w_uq",
    "kv_norm", "w_ukv", "attn_norm", "w_out", "norm_mlp", "w_up", "conv_ff_w", "conv_ff_b", "w_down", "final_norm",
)
REPLICATED = ("b_ada", "norm_mix", "conv_b", "dt_bias", "a_log", "d_skip", "ssd_norm", "q_norm", "kv_norm", "attn_norm",
              "norm_mlp", "conv_ff_b", "final_norm")
CONV_SHARDED = ("conv_w", "conv_ff_w")


def kernel(x, c, positions, w_ada, b_ada, norm_mix, w_in, conv_w, conv_b, dt_bias, a_log, d_skip, ssd_norm, q_norm, w_uq, kv_norm, w_ukv, attn_norm, w_out, norm_mlp, w_up, conv_ff_w, conv_ff_b, w_down, final_norm, loss_target, m_w_ada, m_b_ada, m_norm_mix, m_w_in, m_conv_w, m_conv_b, m_dt_bias, m_a_log, m_d_skip, m_ssd_norm, m_q_norm, m_w_uq, m_kv_norm, m_w_ukv, m_attn_norm, m_w_out, m_norm_mlp, m_w_up, m_conv_ff_w, m_conv_ff_b, m_w_down, m_final_norm, v_w_ada, v_b_ada, v_norm_mix, v_w_in, v_conv_w, v_conv_b, v_dt_bias, v_a_log, v_d_skip, v_ssd_norm, v_q_norm, v_w_uq, v_kv_norm, v_w_ukv, v_attn_norm, v_w_out, v_norm_mlp, v_w_up, v_conv_ff_w, v_conv_ff_b, v_w_down, v_final_norm):
    loc = locals()
    Wl = {k: loc[k] for k in WEIGHTS}
    Ml = {k: loc["m_" + k] for k in WEIGHTS}
    Vl = {k: loc["v_" + k] for k in WEIGHTS}
    xi, yi, ci = lax.axis_index("x"), lax.axis_index("y"), lax.axis_index("c")
    chip = 2 * xi + yi
    dev = 2 * chip + ci
    B, S, D = x.shape
    n_tok = N_DEV * B

    c_all = all_gather8("gather_c", c.reshape(8, -1)).reshape(n_tok, D)
    mod_cols = ada_fwd(c_all, w_ada)
    cols = mod_cols.shape[-1]
    half = n_tok // 2
    mod_mine = lax.dynamic_slice_in_dim(mod_cols, ci * half, half, axis=1)
    small_in = _flat_pack([mod_mine, conv_w, conv_ff_w])
    n_mod = mod_mine.size // PACK_W
    small_all = all_gather8("gather_mod", small_in).reshape(N_CHIPS, 2, -1, PACK_W)
    mod_all = small_all[:, :, :n_mod].reshape(N_CHIPS, 2, DEPTH, half, cols).transpose(2, 1, 3, 0, 4).reshape(DEPTH, n_tok, N_CHIPS * cols)
    mod_raw = lax.dynamic_slice_in_dim(mod_all, dev * B, B, axis=1).reshape(DEPTH, B, 6, D)
    conv_parts = [_flat_unpack(small_all[k, 0, n_mod:], [conv_w.shape, conv_ff_w.shape]) for k in range(N_CHIPS)]
    conv_full = {name: jnp.concatenate([conv_parts[k][i] for k in range(N_CHIPS)], axis=-1) for i, name in enumerate(CONV_SHARDED)}

    shard_shapes = {k: Wl[k].shape for k in BIG}
    rows = _big_rows(Wl)
    pack = _pack_rows([Wl[k].astype(MXU_DTYPE) for k in BIG], rows)
    gathered = lax.dynamic_update_slice_in_dim(gather_weights(pack), pack[None], chip, axis=0)
    W = unpack_gathered(gathered, shard_shapes)
    W.update({k: Wl[k] for k in REPLICATED})
    W.update(conv_full)

    loss_lanes, grad_x, grads, dmod_raw = local_step(x, loss_target, positions, mod_raw, W)
    loss = lax.psum(loss_lanes[0, 0], ("x", "y", "c"))

    gpack = pack_full_grads(grads, shard_shapes, rows)
    wire, own = chip_sum(gpack, swap_halves(gpack), ci, chip)
    mine = mesh_sum(own, scatter_chips(wire), chip)
    theirs = swap_with_sibling(mine)
    both = jnp.concatenate([jnp.where(ci == 0, mine, theirs), jnp.where(ci == 0, theirs, mine)], axis=0)
    g_shard = unpack_shards(both, shard_shapes)

    small_names = REPLICATED + CONV_SHARDED
    small_out = _flat_pack([grads[k] for k in small_names] + [dmod_raw])
    small_got = all_gather8("gather_small", small_out).reshape(N_DEV, -1, PACK_W)
    small_sum = _flat_unpack(sum_devices("sum_small", small_got), [grads[k].shape for k in small_names])
    G = dict(zip(small_names, small_sum))
    for name in CONV_SHARDED:
        width = Wl[name].shape[-1]
        G[name] = lax.dynamic_slice_in_dim(G[name], chip * width, width, axis=-1)
    n_small = sum(grads[k].size for k in small_names)
    dmod_all = small_got.reshape(N_DEV, -1)[:, n_small : n_small + dmod_raw.size].reshape(N_DEV, DEPTH, B, 6 * D)
    dmod_all = dmod_all.transpose(1, 0, 2, 3).reshape(DEPTH, n_tok, 6 * D)
    dmod_cols = lax.dynamic_slice_in_dim(dmod_all, chip * cols, cols, axis=-1)
    G["w_ada"] = jnp.stack([matmul(f"w_ada_wgrad{l}", c_all, dmod_cols[l], ta=True, a_act=_silu) for l in range(DEPTH)])
    G.update(g_shard)

    deltas, new_m, new_v = {}, {}, {}
    small_upd = adamw("adamw_small", *[_flat_pack([t[k] for k in REPLICATED]) for t in (Wl, G, Ml, Vl)])
    for res, t in zip((deltas, new_m, new_v), small_upd):
        res.update(zip(REPLICATED, _flat_unpack(t, [Wl[k].shape for k in REPLICATED])))
    for k in WEIGHTS:
        if k not in REPLICATED:
            deltas[k], new_m[k], new_v[k] = adamw("adamw_" + k, Wl[k], G[k], Ml[k], Vl[k])
    return (loss, grad_x, *[G[k] for k in WEIGHTS], *[deltas[k] for k in WEIGHTS], *[new_m[k] for k in WEIGHTS], *[new_v[k] for k in WEIGHTS])
```

```python
import functools
from typing import NamedTuple

import numpy as np
import jax
import jax.numpy as jnp
from jax import lax
from jax.experimental import pallas as pl
from jax.experimental.pallas import tpu as pltpu

F32 = jnp.float32
BF16 = jnp.bfloat16
MXU_DTYPE = jnp.bfloat16
WIRE_DTYPE = jnp.bfloat16
HIGHEST = lax.Precision.HIGHEST

D_MODEL = 1024
DEPTH = 2
D_SSD = 1024
SSD_HEADS = 16
SSD_HEAD_DIM = 64
SSD_GROUPS = 2
SSD_STATE = 128
SSD_CONV = 4
CHUNK = 128
MLA_HEADS = 8
QK_NOPE = 128
QK_ROPE = 64
V_DIM = 128
D_ATT = MLA_HEADS * V_DIM
Q_RANK = 384
KV_RANK = 256
ROPE_BASE = 10000.0
D_FF = 2816
FF_CONV = 3
EPS = 1e-6
D_XBC = D_SSD + 2 * SSD_GROUPS * SSD_STATE
D_IN = D_SSD + D_XBC + SSD_HEADS + Q_RANK + KV_RANK + QK_ROPE
ADAM_LR, ADAM_B1, ADAM_B2, ADAM_EPS, ADAM_WD, ADAM_STEP = 0.001, 0.9, 0.999, 1e-08, 0.01, 10

LANE = 128
N_CHIPS = 4
N_DEV = 8

OFF_Z, OFF_XBC, OFF_CKV, OFF_KR, OFF_DT, OFF_CQ = 0, 1024, 2560, 2816, 2944, 3072
N_PROJ = 3456
Q_HEAD_PAD = 256
VMEM_LIMIT = 56 * 1024 * 1024


def _cparams(n_axes):
    return pltpu.CompilerParams(dimension_semantics=("arbitrary",) * n_axes, vmem_limit_bytes=VMEM_LIMIT)


def _pick(n, prefs):
    for p in prefs:
        if n % p == 0:
            return p
    return n


def _silu(x):
    return x * jax.nn.sigmoid(x)


def _rms(x, g):
    return x * lax.rsqrt(jnp.mean(x * x, axis=-1, keepdims=True) + EPS) * g


def _softplus(x):
    return jnp.maximum(x, 0.0) + jnp.log(1.0 + jnp.exp(-jnp.abs(x)))


class Col(NamedTuple):
    arr: jax.Array
    width: int
    blk: int


def _as_col(a):
    return a if isinstance(a, Col) else Col(a, a.shape[-1], 0)


def _row_specs(tiled, pbatch, shared, tile):
    specs = [pl.BlockSpec((1, tile, t.width), lambda b, i, blk=t.blk: (b, i, blk)) for t in tiled]
    specs += [pl.BlockSpec((1, 1, p.shape[-1]), lambda b, i: (b, 0, 0)) for p in pbatch]
    specs += [pl.BlockSpec(s.shape, lambda b, i: (0, 0)) for s in shared]
    return specs


def _row_vals(refs, nt, npb):
    return [r[0].astype(F32) for r in refs[: nt + npb]] + [r[...].astype(F32) for r in refs[nt + npb :]]


def rowwise_fwd(name, fn, tiled, pbatch, shared, outs, tile=512):
    tiled = [_as_col(t) for t in tiled]
    B, S = tiled[0].arr.shape[:2]
    tile = min(tile, S)
    nt, npb, nsh = len(tiled), len(pbatch), len(shared)
    n_in = nt + npb + nsh

    def body(*refs):
        res = fn(*_row_vals(refs[:n_in], nt, npb))
        for r, v in zip(refs[n_in:], res):
            r[0] = v.astype(r.dtype)

    return pl.pallas_call(
        body,
        name=name,
        grid=(B, S // tile),
        in_specs=_row_specs(tiled, pbatch, shared, tile),
        out_specs=[pl.BlockSpec((1, tile, w), lambda b, i: (b, i, 0)) for w, _ in outs],
        out_shape=[jax.ShapeDtypeStruct((B, S, w), dt) for w, dt in outs],
        compiler_params=_cparams(2),
    )(*[t.arr for t in tiled], *pbatch, *shared)


def rowwise_bwd(name, fn, tiled, pbatch, shared, cts, grad_tiled, grad_shared, adds=None, tile=256, mxu_only=()):
    tiled = [_as_col(t) for t in tiled]
    adds = adds or {}
    B, S = tiled[0].arr.shape[:2]
    tile = min(tile, S)
    nt, npb, nsh = len(tiled), len(pbatch), len(shared)
    n_in = nt + npb + nsh
    add_idx = sorted(adds)
    gt = [i for i in range(nt) if grad_tiled[i]]
    gs = [i for i in range(nsh) if grad_shared[i]]
    diff = gt + [nt + i for i in range(npb)] + [nt + npb + i for i in gs]
    n_ct, n_add = len(cts), len(add_idx)

    def body(*refs):
        vals = _row_vals(refs[:n_in], nt, npb)
        ct_v = tuple(r[0].astype(F32) if c.ndim == 3 else jnp.sum(r[0].astype(F32), axis=0) for r, c in zip(refs[n_in : n_in + n_ct], cts))
        add_v = {i: r[0].astype(F32) for i, r in zip(add_idx, refs[n_in + n_ct : n_in + n_ct + n_add])}
        out_refs = refs[n_in + n_ct + n_add :]

        def f(*dargs):
            full = list(vals)
            for k, i in enumerate(diff):
                full[i] = dargs[k]
            return tuple(fn(*full))

        _, vjp = jax.vjp(f, *[vals[i] for i in diff])
        grads = vjp(ct_v)
        b, i = pl.program_id(0), pl.program_id(1)
        k = 0
        for idx in gt:
            g = grads[k]
            if idx in add_v:
                g = g + add_v[idx]
            out_refs[k][0] = g.astype(out_refs[k].dtype)
            k += 1
        for _ in range(npb):
            r, g = out_refs[k], grads[k]

            @pl.when(i == 0)
            def _(r=r, g=g):
                r[0] = g

            @pl.when(i > 0)
            def _(r=r, g=g):
                r[0] += g

            k += 1
        for _ in gs:
            r, g = out_refs[k], grads[k]

            @pl.when((i == 0) & (b == 0))
            def _(r=r, g=g):
                r[...] = g

            @pl.when((i > 0) | (b > 0))
            def _(r=r, g=g):
                r[...] += g

            k += 1

    out_specs = [pl.BlockSpec((1, tile, tiled[i].width), lambda b, i: (b, i, 0)) for i in gt]
    out_shape = [jax.ShapeDtypeStruct((B, S, tiled[i].width), MXU_DTYPE if i in mxu_only else F32) for i in gt]
    out_specs += [pl.BlockSpec((1, 1, p.shape[-1]), lambda b, i: (b, 0, 0)) for p in pbatch]
    out_shape += [jax.ShapeDtypeStruct(p.shape, F32) for p in pbatch]
    out_specs += [pl.BlockSpec(shared[i].shape, lambda b, i: (0, 0)) for i in gs]
    out_shape += [jax.ShapeDtypeStruct(shared[i].shape, F32) for i in gs]
    in_specs = _row_specs(tiled, pbatch, shared, tile)
    for c in cts:
        if c.ndim == 3:
            in_specs.append(pl.BlockSpec((1, tile, c.shape[-1]), lambda b, i: (b, i, 0)))
        else:
            in_specs.append(pl.BlockSpec((1, c.shape[1], tile, c.shape[-1]), lambda b, i: (b, 0, i, 0)))
    in_specs += [pl.BlockSpec((1, tile, adds[i].shape[-1]), lambda b, i: (b, i, 0)) for i in add_idx]
    return pl.pallas_call(
        body,
        name=name,
        grid=(B, S // tile),
        in_specs=in_specs,
        out_specs=out_specs,
        out_shape=out_shape,
        compiler_params=_cparams(2),
    )(*[t.arr for t in tiled], *pbatch, *shared, *cts, *[adds[i] for i in add_idx])


def fn_prenorm(x, sc, sh, g, bsc, bsh):
    return (_rms(x, g) * (1.0 + sc + bsc) + (sh + bsh),)


def fn_resid_prenorm(x, y, gate, sc, sh, g, bgate, bsc, bsh):
    x1 = x + (gate + bgate) * y
    return x1, _rms(x1, g) * (1.0 + sc + bsc) + (sh + bsh)


def fn_mix(yscan, z, o, ssd_norm, attn_norm):
    y = yscan * _silu(z)
    half = D_SSD // SSD_GROUPS
    first = lax.broadcasted_iota(jnp.int32, y.shape, 1) < half
    sq = y * y
    m0 = jnp.sum(jnp.where(first, sq, 0.0), axis=-1, keepdims=True) / half
    m1 = jnp.sum(jnp.where(first, 0.0, sq), axis=-1, keepdims=True) / half
    r = jnp.where(first, lax.rsqrt(m0 + EPS), lax.rsqrt(m1 + EPS))
    return (jnp.concatenate([y * r * ssd_norm, _rms(o, attn_norm)], axis=-1),)


def fn_mla_prep(cq, ckv, kr, cos_t, sin_t, q_norm, kv_norm, rot):
    krr = kr * cos_t + jnp.dot(kr, rot, precision=HIGHEST, preferred_element_type=F32) * sin_t
    return _rms(cq, q_norm), _rms(ckv, kv_norm), krr


def final_fwdbwd(x1, ff, target, gate, fnorm, bgate, tile=256):
    B, S, D = x1.shape
    tile = min(tile, S)

    def body(x_ref, f_ref, t_ref, g_ref, n_ref, bg_ref, loss_ref, dx_ref, df_ref, dg_ref, dn_ref, dbg_ref):
        b, i = pl.program_id(0), pl.program_id(1)
        tgt = t_ref[0]

        def f(x, y, gate, fn, bg):
            yf = _rms(x + (gate + bg) * y, fn)
            return 0.5 * jnp.sum(jnp.mean(jnp.square(yf - tgt), axis=-1, keepdims=True), axis=0, keepdims=True)

        val, vjp = jax.vjp(f, x_ref[0], f_ref[0], g_ref[0], n_ref[...], bg_ref[...])
        dx, dff, dg, dn, dbg = vjp(jnp.ones((1, 1), F32))
        dx_ref[0] = dx
        df_ref[0] = dff.astype(df_ref.dtype)
        lane_loss = jnp.broadcast_to(val, (1, LANE))

        @pl.when(i == 0)
        def _():
            dg_ref[0] = dg

        @pl.when(i > 0)
        def _():
            dg_ref[0] += dg

        @pl.when((i == 0) & (b == 0))
        def _():
            dn_ref[...] = dn
            dbg_ref[...] = dbg
            loss_ref[...] = lane_loss

        @pl.when((i > 0) | (b > 0))
        def _():
            dn_ref[...] += dn
            dbg_ref[...] += dbg
            loss_ref[...] += lane_loss

    tok = pl.BlockSpec((1, tile, D), lambda b, i: (b, i, 0))
    pb = pl.BlockSpec((1, 1, D), lambda b, i: (b, 0, 0))
    sh = pl.BlockSpec((1, D), lambda b, i: (0, 0))
    return pl.pallas_call(
        body,
        name="final_loss",
        grid=(B, S // tile),
        in_specs=[tok, tok, tok, pb, sh, sh],
        out_specs=[pl.BlockSpec((1, LANE), lambda b, i: (0, 0)), tok, tok, pb, sh, sh],
        out_shape=[
            jax.ShapeDtypeStruct((1, LANE), F32),
            jax.ShapeDtypeStruct((B, S, D), F32),
            jax.ShapeDtypeStruct((B, S, D), MXU_DTYPE),
            jax.ShapeDtypeStruct((B, 1, D), F32),
            jax.ShapeDtypeStruct((1, D), F32),
            jax.ShapeDtypeStruct((1, D), F32),
        ],
        compiler_params=_cparams(2),
    )(x1, ff, target, gate, fnorm, bgate)


MATMUL_VMEM_BUDGET = 40 * 1024 * 1024
STEP_COST_BYTES = 1.2e6
MXU_DIM = 256
MXU_FLOPS_PER_BYTE = 280.0


def _tile_options(n, cap):
    opts = [d for d in range(LANE, min(n, cap) + 1, LANE) if n % d == 0]
    return opts or [n]


def _matmul_tiles(M, N, K, sa, sb, so):
    best = None
    for tk in [K] + [d for d in _tile_options(K, 2048) if d >= 512 and d < K]:
        nk = K // tk
        for tm in _tile_options(M, 2048):
            for tn in _tile_options(N, 1408):
                acc = 0 if (so == 4 or nk == 1) else tm * tn * 4
                vmem = 2 * (tm * tk * sa + tk * tn * sb + tm * tn * so) + acc + tm * tn * 4 + 2 * (tm * tk + tk * tn)
                if vmem > MATMUL_VMEM_BUDGET:
                    continue
                a_reads = M * K * sa * (1 if nk == 1 else N // tn)
                b_reads = K * N * sb * (1 if (nk == 1 and N == tn) else M // tm)
                steps = (M // tm) * (N // tn) * nk
                hbm = a_reads + b_reads + M * N * so + (2 * M * N * 4 * (nk - 1) if nk > 1 else 0) / 8
                fill = (-(-tn // MXU_DIM) * MXU_DIM / tn) * (-(-tk // MXU_DIM) * MXU_DIM / tk)
                mxu = 2.0 * M * N * K * fill / MXU_FLOPS_PER_BYTE
                cost = max(hbm, mxu) + steps * STEP_COST_BYTES
                if best is None or cost < best[0]:
                    best = (cost, tm, tn, tk)
    assert best is not None, (M, N, K)
    return best[1:]


class Layer(NamedTuple):
    arr: jax.Array
    l: int


def matmul(name, a, b, ta=False, tb=False, out_dtype=F32, a_act=None):
    pieces = list(a) if isinstance(a, (list, tuple)) else [a]
    assert len(pieces) == 1 or not ta
    lead = pieces[0].shape[:2] if (pieces[0].ndim == 3 and not ta) else None
    pieces = [p.reshape(-1, p.shape[-1]) if p.ndim == 3 else p for p in pieces]
    layer = None
    if isinstance(b, Layer):
        b, layer = b
        b_rows, b_cols = b.shape[1:]
    else:
        if b.ndim == 3:
            b = b.reshape(-1, b.shape[-1])
        b_rows, b_cols = b.shape
    Kp, M = pieces[0].shape if ta else pieces[0].shape[::-1]
    N = b_rows if tb else b_cols
    assert (b_cols if tb else b_rows) == Kp * len(pieces), (name, pieces[0].shape, b.shape)
    tm, tn, tk = _matmul_tiles(M, N, Kp, pieces[0].dtype.itemsize, b.dtype.itemsize, jnp.dtype(out_dtype).itemsize)
    npk = Kp // tk
    nk = npk * len(pieces)
    dims = (((0 if ta else 1,), (1 if tb else 0,)), ((), ()))
    direct = jnp.dtype(out_dtype) == jnp.dtype(F32)

    def body(*refs):
        a_refs, b_ref, o_ref, scratch = refs[: len(pieces)], refs[len(pieces)], refs[len(pieces) + 1], refs[len(pieces) + 2 :]
        k = pl.program_id(2)

        def product(a_ref):
            av = a_ref[...]
            if a_act is not None:
                av = a_act(av.astype(F32))
            return lax.dot_general(av.astype(MXU_DTYPE), b_ref[...].astype(MXU_DTYPE), dims, preferred_element_type=F32)

        if nk == 1:
            o_ref[...] = product(a_refs[0]).astype(o_ref.dtype)
            return
        acc = o_ref if direct else scratch[0]

        @pl.when(k == 0)
        def _():
            acc[...] = product(a_refs[0])

        for p, a_ref in enumerate(a_refs):
            lo, hi = max(p * npk, 1), (p + 1) * npk
            if lo < hi:

                @pl.when((k >= lo) & (k < hi))
                def _(a_ref=a_ref):
                    acc[...] += product(a_ref)

        if not direct:

            @pl.when(k == nk - 1)
            def _():
                o_ref[...] = acc[...].astype(o_ref.dtype)

    def a_spec(p):
        kk = lambda k: jnp.clip(k - p * npk, 0, npk - 1)
        return pl.BlockSpec((tk, tm), lambda i, j, k: (kk(k), i)) if ta else pl.BlockSpec((tm, tk), lambda i, j, k: (i, kk(k)))

    b_block, b_index = ((tn, tk), lambda i, j, k: (j, k)) if tb else ((tk, tn), lambda i, j, k: (k, j))
    if layer is None:
        b_spec = pl.BlockSpec(b_block, b_index)
    else:
        b_spec = pl.BlockSpec((None, *b_block), lambda i, j, k: (layer, *b_index(i, j, k)))
    out = pl.pallas_call(
        body,
        name=name,
        grid=(M // tm, N // tn, nk),
        in_specs=[a_spec(p) for p in range(len(pieces))] + [b_spec],
        out_specs=pl.BlockSpec((tm, tn), lambda i, j, k: (i, j)),
        out_shape=jax.ShapeDtypeStruct((M, N), out_dtype),
        scratch_shapes=[] if (direct or nk == 1) else [pltpu.VMEM((tm, tn), F32)],
        compiler_params=_cparams(3),
    )(*pieces, b)
    return out.reshape(*lead, N) if lead is not None else out


SUBLANES = 8


def _shift_down(u, d):
    if d == 0:
        return u
    r = pltpu.roll(u, d, 0)
    t = lax.broadcasted_iota(jnp.int32, (SUBLANES, u.shape[1]), 0)
    return jnp.concatenate([jnp.where(t >= d, r[:SUBLANES], 0.0), r[SUBLANES:]], axis=0)


def _shift_up(u, d):
    if d == 0:
        return u
    s = u.shape[0]
    r = pltpu.roll(u, s - d, 0)
    t = lax.broadcasted_iota(jnp.int32, (SUBLANES, u.shape[1]), 0)
    return jnp.concatenate([r[: s - SUBLANES], jnp.where(t < SUBLANES - d, r[s - SUBLANES :], 0.0)], axis=0)


def _conv(u, w, b):
    k = w.shape[0]
    out = b + w[k - 1 : k, :] * u
    for j in range(k - 1):
        out = out + w[j : j + 1, :] * _shift_down(u, k - 1 - j)
    return out


def _conv_bwd(u, w, dc):
    k = w.shape[0]
    du = w[k - 1 : k, :] * dc
    dws = []
    for j in range(k - 1):
        du = du + w[j : j + 1, :] * _shift_up(dc, k - 1 - j)
        dws.append(jnp.sum(dc * _shift_down(u, k - 1 - j), axis=0, keepdims=True))
    dws.append(jnp.sum(dc * u, axis=0, keepdims=True))
    return du, jnp.concatenate(dws, axis=0), jnp.sum(dc, axis=0, keepdims=True)


def conv_silu_fwd(proj, w, b):
    B, S, _ = proj.shape
    k, c = w.shape
    blk0 = OFF_XBC // LANE

    def body(u_ref, w_ref, b_ref, o_ref):
        o_ref[0] = _silu(_conv(u_ref[0], w_ref[...], b_ref[...]))

    return pl.pallas_call(
        body,
        name="conv_silu_fwd",
        grid=(B, c // LANE),
        in_specs=[
            pl.BlockSpec((1, S, LANE), lambda bi, j: (bi, 0, blk0 + j)),
            pl.BlockSpec((k, LANE), lambda bi, j: (0, j)),
            pl.BlockSpec((1, LANE), lambda bi, j: (0, j)),
        ],
        out_specs=pl.BlockSpec((1, S, LANE), lambda bi, j: (bi, 0, j)),
        out_shape=jax.ShapeDtypeStruct((B, S, c), F32),
        compiler_params=_cparams(2),
    )(proj, w, b)


def conv_silu_bwd(proj, w, b, douts):
    B, S, _ = proj.shape
    k, c = w.shape
    blk0 = OFF_XBC // LANE
    ends = np.cumsum([d.shape[-1] // LANE for d in douts]).tolist()
    starts = [0] + ends[:-1]

    def body(u_ref, w_ref, b_ref, *refs):
        d_refs, (du_ref, dw_ref, db_ref) = refs[: len(douts)], refs[len(douts) :]
        j, bi = pl.program_id(0), pl.program_id(1)
        u, wv = u_ref[0], w_ref[...]
        cv = _conv(u, wv, b_ref[...])
        sg = jax.nn.sigmoid(cv)
        dout = d_refs[-1][0]
        for r in reversed(range(len(douts) - 1)):
            dout = jnp.where(j < ends[r], d_refs[r][0], dout)
        dc = dout * (sg * (1.0 + cv * (1.0 - sg)))
        du, dw, db = _conv_bwd(u, wv, dc)
        du_ref[0] = du.astype(du_ref.dtype)

        @pl.when(bi == 0)
        def _():
            dw_ref[...] = dw
            db_ref[...] = db

        @pl.when(bi > 0)
        def _():
            dw_ref[...] += dw
            db_ref[...] += db

    return pl.pallas_call(
        body,
        name="conv_silu_bwd",
        grid=(c // LANE, B),
        in_specs=[
            pl.BlockSpec((1, S, LANE), lambda j, bi: (bi, 0, blk0 + j)),
            pl.BlockSpec((k, LANE), lambda j, bi: (0, j)),
            pl.BlockSpec((1, LANE), lambda j, bi: (0, j)),
        ]
        + [
            pl.BlockSpec((1, S, LANE), lambda j, bi, lo=lo, hi=hi: (bi, 0, jnp.clip(j, lo, hi - 1) - lo))
            for lo, hi in zip(starts, ends)
        ],
        out_specs=[
            pl.BlockSpec((1, S, LANE), lambda j, bi: (bi, 0, j)),
            pl.BlockSpec((k, LANE), lambda j, bi: (0, j)),
            pl.BlockSpec((1, LANE), lambda j, bi: (0, j)),
        ],
        out_shape=[
            jax.ShapeDtypeStruct((B, S, c), MXU_DTYPE),
            jax.ShapeDtypeStruct((k, c), F32),
            jax.ShapeDtypeStruct((1, c), F32),
        ],
        compiler_params=_cparams(2),
    )(proj, w, b, *douts)


def _glu_specs(S, k, nb, batch_col):
    specs = []
    for off in (0, nb):
        specs.append(pl.BlockSpec((1, S, LANE), lambda *ids, off=off: (batch_col(*ids)[0], 0, off + batch_col(*ids)[1])))
    for rows in (k, 1):
        for off in (0, nb):
            specs.append(pl.BlockSpec((rows, LANE), lambda *ids, off=off: (0, off + batch_col(*ids)[1])))
    return specs


def conv_glu_fwd(u, w, b):
    B, S, c2 = u.shape
    k = w.shape[0]
    nb = c2 // 2 // LANE

    def body(ug_ref, uv_ref, wg_ref, wv_ref, bg_ref, bv_ref, o_ref):
        gate = _conv(ug_ref[0], wg_ref[...], bg_ref[...])
        val = _conv(uv_ref[0], wv_ref[...], bv_ref[...])
        o_ref[0] = (_silu(gate) * val).astype(o_ref.dtype)

    return pl.pallas_call(
        body,
        name="conv_glu_fwd",
        grid=(B, nb),
        in_specs=_glu_specs(S, k, nb, lambda bi, j: (bi, j)),
        out_specs=pl.BlockSpec((1, S, LANE), lambda bi, j: (bi, 0, j)),
        out_shape=jax.ShapeDtypeStruct((B, S, c2 // 2), MXU_DTYPE),
        compiler_params=_cparams(2),
    )(u, u, w, w, b, b)


def conv_glu_bwd(u, w, b, da):
    B, S, c2 = u.shape
    k = w.shape[0]
    nb = c2 // 2 // LANE

    def body(ug_ref, uv_ref, wg_ref, wv_ref, bg_ref, bv_ref, d_ref, dug_ref, duv_ref, dwg_ref, dwv_ref, dbg_ref, dbv_ref):
        bi = pl.program_id(1)
        ug, uv, wg, wv = ug_ref[0], uv_ref[0], wg_ref[...], wv_ref[...]
        gate, val = _conv(ug, wg, bg_ref[...]), _conv(uv, wv, bv_ref[...])
        sg = jax.nn.sigmoid(gate)
        dav = d_ref[0].astype(F32)
        dug, dwg, dbg = _conv_bwd(ug, wg, dav * val * (sg * (1.0 + gate * (1.0 - sg))))
        duv, dwv, dbv = _conv_bwd(uv, wv, dav * gate * sg)
        dug_ref[0] = dug.astype(dug_ref.dtype)
        duv_ref[0] = duv.astype(duv_ref.dtype)

        @pl.when(bi == 0)
        def _():
            dwg_ref[...], dwv_ref[...], dbg_ref[...], dbv_ref[...] = dwg, dwv, dbg, dbv

        @pl.when(bi > 0)
        def _():
            dwg_ref[...] += dwg
            dwv_ref[...] += dwv
            dbg_ref[...] += dbg
            dbv_ref[...] += dbv

    tok = pl.BlockSpec((1, S, LANE), lambda j, bi: (bi, 0, j))
    outs = pl.pallas_call(
        body,
        name="conv_glu_bwd",
        grid=(nb, B),
        in_specs=_glu_specs(S, k, nb, lambda j, bi: (bi, j)) + [tok],
        out_specs=[tok, tok] + [pl.BlockSpec((rows, LANE), lambda j, bi: (0, j)) for rows in (k, k, 1, 1)],
        out_shape=[jax.ShapeDtypeStruct((B, S, c2 // 2), MXU_DTYPE)] * 2
        + [jax.ShapeDtypeStruct((rows, c2 // 2), F32) for rows in (k, k, 1, 1)],
        compiler_params=_cparams(2),
    )(u, u, w, w, b, b, da)
    dug, duv, dwg, dwv, dbg, dbv = outs
    return dug, duv, jnp.concatenate([dwg, dwv], axis=1), jnp.concatenate([dbg, dbv], axis=1)


N_PAIR = SSD_HEADS // SSD_GROUPS // 2


def _ssd_chunk(prev, xs, bms, cms, dtraw, dt_bias, a_log, d_skip):
    L = CHUNK
    lane = lax.broadcasted_iota(jnp.int32, (1, LANE), 1)
    sub = lax.broadcasted_iota(jnp.int32, (LANE, 1), 0)
    row = lax.broadcasted_iota(jnp.int32, (L, L), 0)
    col = lax.broadcasted_iota(jnp.int32, (L, L), 1)
    tri = (row >= col).astype(F32)
    first = lane < SSD_HEAD_DIM

    dt = _softplus(dtraw + dt_bias)
    a = dt * (-jnp.exp(a_log))
    acs = jnp.dot(tri, a, precision=HIGHEST, preferred_element_type=F32)
    acs_t = acs.T
    a_end = jnp.sum(a, axis=0, keepdims=True)

    def lane_of(v, h):
        return jnp.sum(jnp.where(lane == h, v, 0.0), axis=1, keepdims=True)

    def expand(v, ha):
        return jnp.where(first, lane_of(v, ha), lane_of(v, ha + 1))

    ys, news = [], []
    for g in range(SSD_GROUPS):
        bm, cm = bms[g].astype(MXU_DTYPE), cms[g].astype(MXU_DTYPE)
        cb = lax.dot_general(cm, bm, (((1,), (1,)), ((), ())), preferred_element_type=F32)
        for j in range(N_PAIR):
            ha = g * (SSD_HEADS // SSD_GROUPS) + 2 * j
            x, state = xs[g * N_PAIR + j], prev[g * N_PAIR + j]
            dt_e, acs_e, end_e = expand(dt, ha), expand(acs, ha), expand(a_end, ha)
            xdt = x * dt_e
            y = jnp.dot(cm, state.astype(MXU_DTYPE), preferred_element_type=F32) * jnp.exp(acs_e)
            st = lax.dot_general(bm, (xdt * jnp.exp(end_e - acs_e)).astype(MXU_DTYPE), (((0,), (0,)), ((), ())), preferred_element_type=F32)
            news.append(state * jnp.exp(end_e) + st)
            for hh in range(2):
                h = ha + hh
                seg = lane_of(acs, h) - jnp.sum(jnp.where(sub == h, acs_t, 0.0), axis=0, keepdims=True)
                decay = jnp.exp(jnp.where(row >= col, seg, -jnp.inf))
                xh = jnp.where(first if hh == 0 else jnp.logical_not(first), xdt, 0.0)
                y = y + jnp.dot((cb * decay).astype(MXU_DTYPE), xh.astype(MXU_DTYPE), preferred_element_type=F32)
            ys.append(y + x * expand(d_skip, ha))
    return ys, news


N_PAIRS = SSD_GROUPS * N_PAIR


def _ssd_specs(chunk_of):
    bc = SSD_GROUPS * SSD_STATE
    par = pl.BlockSpec((1, LANE), lambda *ids: (0, 0))
    return [
        pl.BlockSpec((1, CHUNK, D_SSD), lambda *ids: (*chunk_of(*ids), 0)),
        pl.BlockSpec((1, CHUNK, bc), lambda *ids: (*chunk_of(*ids), D_SSD // bc)),
        pl.BlockSpec((1, CHUNK, bc), lambda *ids: (*chunk_of(*ids), D_SSD // bc + 1)),
        pl.BlockSpec((1, CHUNK, LANE), lambda *ids: (*chunk_of(*ids), OFF_DT // LANE)),
        par,
        par,
        par,
    ]


def _pair_cols(ref):
    return [ref[0, :, k * LANE : (k + 1) * LANE] for k in range(N_PAIRS)]


def _group_cols(ref):
    return [ref[0, :, g * SSD_STATE : (g + 1) * SSD_STATE] for g in range(SSD_GROUPS)]


def ssd_fwd(xbc_act, proj, dt_bias, a_log, d_skip):
    B, S, _ = xbc_act.shape
    nc = S // CHUNK

    def body(xs_ref, bm_ref, cm_ref, dt_ref, db_ref, al_ref, ds_ref, y_ref, st_ref, state):
        @pl.when(pl.program_id(1) == 0)
        def _():
            state[...] = jnp.zeros_like(state)

        prev = [state[k] for k in range(N_PAIRS)]
        for k in range(N_PAIRS):
            st_ref[0, 0, k] = prev[k]
        ys, news = _ssd_chunk(prev, _pair_cols(xs_ref), _group_cols(bm_ref), _group_cols(cm_ref), dt_ref[0], db_ref[...], al_ref[...], ds_ref[...])
        for k in range(N_PAIRS):
            y_ref[0, :, k * LANE : (k + 1) * LANE] = ys[k]
            state[k] = news[k]

    return pl.pallas_call(
        body,
        name="ssd_fwd",
        grid=(B, nc),
        in_specs=_ssd_specs(lambda b, c: (b, c)),
        out_specs=[
            pl.BlockSpec((1, CHUNK, D_SSD), lambda b, c: (b, c, 0)),
            pl.BlockSpec((1, 1, N_PAIRS, SSD_STATE, LANE), lambda b, c: (b, c, 0, 0, 0)),
        ],
        out_shape=[
            jax.ShapeDtypeStruct((B, S, D_SSD), F32),
            jax.ShapeDtypeStruct((B, nc, N_PAIRS, SSD_STATE, LANE), F32),
        ],
        scratch_shapes=[pltpu.VMEM((N_PAIRS, SSD_STATE, LANE), F32)],
        compiler_params=_cparams(2),
    )(xbc_act, xbc_act, xbc_act, proj, dt_bias, a_log, d_skip)


def ssd_bwd(xbc_act, proj, dt_bias, a_log, d_skip, states, dy):
    B, S, _ = xbc_act.shape
    nc = S // CHUNK
    bc = SSD_GROUPS * SSD_STATE
    chunk_of = lambda b, cr: (b, nc - 1 - cr)

    def body(xs_ref, bm_ref, cm_ref, dt_ref, db_ref, al_ref, ds_ref, st_ref, dy_ref,
             dxs_ref, dbm_ref, dcm_ref, ddt_ref, ddb_ref, dal_ref, dds_ref, dstate):
        b, cr = pl.program_id(0), pl.program_id(1)

        @pl.when(cr == 0)
        def _():
            dstate[...] = jnp.zeros_like(dstate)

        prev = [st_ref[0, 0, k] for k in range(N_PAIRS)]
        _, vjp = jax.vjp(
            _ssd_chunk, prev, _pair_cols(xs_ref), _group_cols(bm_ref), _group_cols(cm_ref), dt_ref[0], db_ref[...], al_ref[...], ds_ref[...]
        )
        dprev, dxs, dbm, dcm, ddt, ddb, dal, dds = vjp((_pair_cols(dy_ref), [dstate[k] for k in range(N_PAIRS)]))
        for k in range(N_PAIRS):
            dstate[k] = dprev[k]
            dxs_ref[0, :, k * LANE : (k + 1) * LANE] = dxs[k]
        for g in range(SSD_GROUPS):
            dbm_ref[0, :, g * SSD_STATE : (g + 1) * SSD_STATE] = dbm[g]
            dcm_ref[0, :, g * SSD_STATE : (g + 1) * SSD_STATE] = dcm[g]
        ddt_ref[0] = ddt
        first = (b == 0) & (cr == 0)

        @pl.when(first)
        def _():
            ddb_ref[...] = ddb
            dal_ref[...] = dal
            dds_ref[...] = dds

        @pl.when(jnp.logical_not(first))
        def _():
            ddb_ref[...] += ddb
            dal_ref[...] += dal
            dds_ref[...] += dds

    par = pl.BlockSpec((1, LANE), lambda *ids: (0, 0))
    in_specs = _ssd_specs(chunk_of) + [
        pl.BlockSpec((1, 1, N_PAIRS, SSD_STATE, LANE), lambda b, cr: (*chunk_of(b, cr), 0, 0, 0)),
        pl.BlockSpec((1, CHUNK, D_SSD), lambda b, cr: (*chunk_of(b, cr), 0)),
    ]
    return pl.pallas_call(
        body,
        name="ssd_bwd",
        grid=(B, nc),
        in_specs=in_specs,
        out_specs=[
            pl.BlockSpec((1, CHUNK, D_SSD), lambda b, cr: (*chunk_of(b, cr), 0)),
            pl.BlockSpec((1, CHUNK, bc), lambda b, cr: (*chunk_of(b, cr), 0)),
            pl.BlockSpec((1, CHUNK, bc), lambda b, cr: (*chunk_of(b, cr), 0)),
            pl.BlockSpec((1, CHUNK, LANE), lambda b, cr: (*chunk_of(b, cr), 0)),
            par,
            par,
            par,
        ],
        out_shape=[
            jax.ShapeDtypeStruct((B, S, D_SSD), F32),
            jax.ShapeDtypeStruct((B, S, bc), F32),
            jax.ShapeDtypeStruct((B, S, bc), F32),
            jax.ShapeDtypeStruct((B, S, LANE), F32),
            jax.ShapeDtypeStruct((1, LANE), F32),
            jax.ShapeDtypeStruct((1, LANE), F32),
            jax.ShapeDtypeStruct((1, LANE), F32),
        ],
        scratch_shapes=[pltpu.VMEM((N_PAIRS, SSD_STATE, LANE), F32)],
        compiler_params=_cparams(2),
    )(xbc_act, xbc_act, xbc_act, proj, dt_bias, a_log, d_skip, states, dy)


ATT_SCALE = (QK_NOPE + QK_ROPE) ** -0.5
ATT_TILE = 512


ATT_HEADS_FWD = 4
ATT_HEADS_BWD = 2
LOG2E = 1.4426950408889634
Q_PRESCALE = ATT_SCALE * LOG2E


def fn_rope_q(q, cos_t, sin_t, rot):
    parts = []
    for h in range(MLA_HEADS):
        qr = q[:, h * Q_HEAD_PAD + LANE : (h + 1) * Q_HEAD_PAD]
        qr = qr * cos_t + jnp.dot(qr, rot, precision=HIGHEST, preferred_element_type=F32) * sin_t
        parts += [q[:, h * Q_HEAD_PAD : h * Q_HEAD_PAD + LANE] * Q_PRESCALE, qr * Q_PRESCALE]
    return (jnp.concatenate(parts, axis=1),)


def _head_cols(ref, hh, width):
    return ref[0, :, hh * width : (hh + 1) * width]


def _causal_mask(s):
    return jnp.where(lax.broadcasted_iota(jnp.int32, s.shape, 0) >= lax.broadcasted_iota(jnp.int32, s.shape, 1), s, -jnp.inf)


def _nt(a, b):
    return lax.dot_general(a, b, (((1,), (1,)), ((), ())), preferred_element_type=F32)


def _tn(a, b):
    return lax.dot_general(a, b, (((0,), (0,)), ((), ())), preferred_element_type=F32)


def _per_tile(idx, n, variant):
    for v in range(n):

        @pl.when(idx == v)
        def _(v=v):
            variant(v)


def attention_fwd(qs, kv, krr):
    B, S, _ = qs.shape
    t = min(ATT_TILE, S)
    n = S // t
    hps = ATT_HEADS_FWD

    def body(q_ref, k_ref, kr_ref, o_ref, lse_ref):
        def variant(v):
            spans = ([(0, v * t, False)] if v else []) + [(v * t, (v + 1) * t, True)]
            for hh in range(hps):
                q = _head_cols(q_ref, hh, Q_HEAD_PAD)
                c0 = hh * 2 * LANE
                ss = []
                for a, b, diag in spans:
                    s = _nt(q, jnp.concatenate([k_ref[0, a:b, c0 : c0 + LANE], kr_ref[0, a:b]], axis=1))
                    ss.append(_causal_mask(s) if diag else s)
                m = functools.reduce(jnp.maximum, [jnp.max(s, axis=1, keepdims=True) for s in ss])
                ps = [jnp.exp2(s - m) for s in ss]
                l = functools.reduce(jnp.add, [jnp.sum(p, axis=1, keepdims=True) for p in ps])
                acc = functools.reduce(
                    jnp.add,
                    [jnp.dot(p.astype(MXU_DTYPE), k_ref[0, a:b, c0 + LANE : c0 + 2 * LANE], preferred_element_type=F32) for p, (a, b, _) in zip(ps, spans)],
                )
                o_ref[0, :, hh * V_DIM : (hh + 1) * V_DIM] = acc / l
                lse_ref[0, hh] = m + jnp.log2(l)

        _per_tile(pl.program_id(2), n, variant)

    return pl.pallas_call(
        body,
        name="attention_fwd",
        grid=(B, MLA_HEADS // hps, n),
        in_specs=[
            pl.BlockSpec((1, t, hps * Q_HEAD_PAD), lambda b, h, i: (b, i, h)),
            pl.BlockSpec((1, S, hps * 2 * LANE), lambda b, h, i: (b, 0, h)),
            pl.BlockSpec((1, S, LANE), lambda b, h, i: (b, 0, 0)),
        ],
        out_specs=[
            pl.BlockSpec((1, t, hps * V_DIM), lambda b, h, i: (b, i, h)),
            pl.BlockSpec((1, hps, t, 1), lambda b, h, i: (b, h, i, 0)),
        ],
        out_shape=[jax.ShapeDtypeStruct((B, S, D_ATT), F32), jax.ShapeDtypeStruct((B, MLA_HEADS, S, 1), F32)],
        compiler_params=_cparams(3),
    )(qs, kv, krr)


def attention_bwd(qs, kv, krr, cos_t, sin_t, rot_t, o, lse, do):
    B, S, _ = qs.shape
    t = min(ATT_TILE, S)
    n = S // t
    hps = ATT_HEADS_BWD

    def body(q_ref, c_ref, s_ref, rott_ref, k_ref, kr_ref, o_ref, lse_ref, do_ref, dq_ref, dkv_ref, dkr_ref, dk_acc, dv_acc):
        i = pl.program_id(2)

        @pl.when(i == 0)
        def _():
            dk_acc[...] = jnp.zeros_like(dk_acc)
            dv_acc[...] = jnp.zeros_like(dv_acc)

        def variant(v):
            spans = ([(0, v * t, False)] if v else []) + [(v * t, (v + 1) * t, True)]
            for hh in range(hps):
                q = _head_cols(q_ref, hh, Q_HEAD_PAD)
                dov = _head_cols(do_ref, hh, V_DIM)
                delta = jnp.sum(_head_cols(o_ref, hh, V_DIM) * dov, axis=1, keepdims=True)
                dob = dov.astype(MXU_DTYPE)
                c0 = hh * 2 * LANE
                acc = None
                for a, b, diag in spans:
                    k = jnp.concatenate([k_ref[0, a:b, c0 : c0 + LANE], kr_ref[0, a:b]], axis=1)
                    s = _nt(q, k)
                    p = jnp.exp2((_causal_mask(s) if diag else s) - lse_ref[0, hh])
                    ds = (p * (_nt(dob, k_ref[0, a:b, c0 + LANE : c0 + 2 * LANE]) - delta) * ATT_SCALE).astype(MXU_DTYPE)
                    part = jnp.dot(ds, k, preferred_element_type=F32)
                    acc = part if acc is None else acc + part
                    dk_acc[hh, a:b] += _tn(ds, q)
                    dv_acc[hh, a:b] += _tn(p.astype(MXU_DTYPE), dob)
                r = acc[:, LANE:]
                dr = r * c_ref[0] + jnp.dot(r * s_ref[0], rott_ref[...], precision=HIGHEST, preferred_element_type=F32)
                dq_ref[0, :, hh * Q_HEAD_PAD : (hh + 1) * Q_HEAD_PAD] = jnp.concatenate([acc[:, :LANE], dr], axis=1).astype(dq_ref.dtype)

        _per_tile(i, n, variant)

        @pl.when(i == n - 1)
        def _():
            dkr = None
            for hh in range(hps):
                dk = dk_acc[hh] * (1.0 / Q_PRESCALE)
                dkv_ref[0, :, hh * 2 * LANE : (hh + 1) * 2 * LANE] = jnp.concatenate([dk[:, :LANE], dv_acc[hh]], axis=1).astype(dkv_ref.dtype)
                dkr = dk[:, LANE:] if dkr is None else dkr + dk[:, LANE:]
            dkr_ref[0, 0] = dkr

    return pl.pallas_call(
        body,
        name="attention_bwd",
        grid=(B, MLA_HEADS // hps, n),
        in_specs=[
            pl.BlockSpec((1, t, hps * Q_HEAD_PAD), lambda b, h, i: (b, i, h)),
            pl.BlockSpec((1, t, LANE), lambda b, h, i: (b, i, 0)),
            pl.BlockSpec((1, t, LANE), lambda b, h, i: (b, i, 0)),
            pl.BlockSpec((LANE, LANE), lambda b, h, i: (0, 0)),
            pl.BlockSpec((1, S, hps * 2 * LANE), lambda b, h, i: (b, 0, h)),
            pl.BlockSpec((1, S, LANE), lambda b, h, i: (b, 0, 0)),
            pl.BlockSpec((1, t, hps * V_DIM), lambda b, h, i: (b, i, h)),
            pl.BlockSpec((1, hps, t, 1), lambda b, h, i: (b, h, i, 0)),
            pl.BlockSpec((1, t, hps * V_DIM), lambda b, h, i: (b, i, h)),
        ],
        out_specs=[
            pl.BlockSpec((1, t, hps * Q_HEAD_PAD), lambda b, h, i: (b, i, h)),
            pl.BlockSpec((1, S, hps * 2 * LANE), lambda b, h, i: (b, 0, h)),
            pl.BlockSpec((1, 1, S, LANE), lambda b, h, i: (b, h, 0, 0)),
        ],
        out_shape=[
            jax.ShapeDtypeStruct(qs.shape, MXU_DTYPE),
            jax.ShapeDtypeStruct(kv.shape, MXU_DTYPE),
            jax.ShapeDtypeStruct((B, MLA_HEADS // hps, S, LANE), F32),
        ],
        scratch_shapes=[pltpu.VMEM((hps, S, Q_HEAD_PAD), F32), pltpu.VMEM((hps, S, V_DIM), F32)],
        compiler_params=_cparams(3),
    )(qs, cos_t, sin_t, rot_t, kv, krr, o, lse, do)


_IN_SPLITS = np.cumsum([D_SSD, D_XBC, SSD_HEADS, Q_RANK, KV_RANK]).tolist()


def _pad_last(t, n):
    return jnp.pad(t, [(0, 0)] * (t.ndim - 1) + [(0, n - t.shape[-1])])


def win_to_kernel(w):
    z, xbc, dt, cq, ckv, kr = jnp.split(w, _IN_SPLITS, axis=-1)
    return jnp.concatenate([z, xbc, ckv, _pad_last(kr, LANE), _pad_last(dt, LANE), cq], axis=-1)


def win_from_kernel(g):
    z, xbc = g[..., :OFF_XBC], g[..., OFF_XBC:OFF_CKV]
    ckv, kr = g[..., OFF_CKV:OFF_KR], g[..., OFF_KR : OFF_KR + QK_ROPE]
    dt, cq = g[..., OFF_DT : OFF_DT + SSD_HEADS], g[..., OFF_CQ:]
    return jnp.concatenate([z, xbc, dt, cq, ckv, kr], axis=-1)


def wuq_to_kernel(w):
    w = w.reshape(*w.shape[:-1], MLA_HEADS, QK_NOPE + QK_ROPE)
    return _pad_last(w, Q_HEAD_PAD).reshape(*w.shape[:-2], MLA_HEADS * Q_HEAD_PAD)


def wuq_from_kernel(g):
    g = g.reshape(*g.shape[:-1], MLA_HEADS, Q_HEAD_PAD)[..., : QK_NOPE + QK_ROPE]
    return g.reshape(*g.shape[:-2], MLA_HEADS * (QK_NOPE + QK_ROPE))


def _lane_pad_row(v):
    return _pad_last(v, LANE)[None, :]


def rope_tables(positions):
    inv_freq = jnp.asarray(1.0 / (ROPE_BASE ** (np.arange(0, QK_ROPE, 2, dtype=np.float32) / QK_ROPE)))
    ang = positions.astype(F32)[..., None] * inv_freq
    cos, sin = jnp.cos(ang), jnp.sin(ang)
    zeros = jnp.zeros(cos.shape[:-1] + (LANE - QK_ROPE,), F32)
    rot = np.zeros((LANE, LANE), np.float32)
    half = QK_ROPE // 2
    for j in range(half):
        rot[j + half, j] = -1.0
        rot[j, j + half] = 1.0
    return jnp.concatenate([cos, cos, zeros], -1), jnp.concatenate([sin, sin, zeros], -1), jnp.asarray(rot), jnp.asarray(rot.T)


BIG = ("w_in", "w_uq", "w_ukv", "w_out", "w_up", "w_down")


def local_step(x, target, positions, mod_raw, W):
    B, S, D = x.shape
    cos_t, sin_t, rot, rot_t = rope_tables(positions)
    row = lambda v: v.reshape(1, -1)
    w_in_k, w_uq_k = win_to_kernel(W["w_in"]), wuq_to_kernel(W["w_uq"])
    Ls = []
    for l in range(DEPTH):
        mods = [mod_raw[l, :, k][:, None, :] for k in range(6)]
        bias = [row(W["b_ada"][l, k * D : (k + 1) * D]) for k in range(6)]
        Ls.append(
            dict(
                mods=mods,
                bias=bias,
                w_in=Layer(w_in_k, l),
                w_uq=Layer(w_uq_k, l),
                w_ukv=Layer(W["w_ukv"], l),
                w_out=Layer(W["w_out"], l),
                w_up=Layer(W["w_up"], l),
                w_down=Layer(W["w_down"], l),
                conv_w=W["conv_w"][l],
                conv_b=row(W["conv_b"][l]),
                conv_ff_w=W["conv_ff_w"][l],
                conv_ff_b=row(W["conv_ff_b"][l]),
                dt_bias=_lane_pad_row(W["dt_bias"][l]),
                a_log=_lane_pad_row(W["a_log"][l]),
                d_skip=_lane_pad_row(W["d_skip"][l]),
                norm_mix=row(W["norm_mix"][l]),
                ssd_norm=row(W["ssd_norm"][l]),
                q_norm=row(W["q_norm"][l]),
                kv_norm=row(W["kv_norm"][l]),
                attn_norm=row(W["attn_norm"][l]),
                norm_mlp=row(W["norm_mlp"][l]),
            )
        )
    fnorm = row(W["final_norm"])

    p0 = Ls[0]
    (h1,) = rowwise_fwd(
        "prenorm_fwd", fn_prenorm, [x], [p0["mods"][1], p0["mods"][0]], [p0["norm_mix"], p0["bias"][1], p0["bias"][0]], [(D, MXU_DTYPE)]
    )
    xin = x
    for l, p in enumerate(Ls):
        s = p["saved"] = dict(xin=xin, h1=h1)
        s["proj"] = proj = matmul(f"w_in_fwd{l}", h1, p["w_in"])
        s["xbc_act"] = xbc_act = conv_silu_fwd(proj, p["conv_w"], p["conv_b"])
        s["yscan"], s["states"] = ssd_fwd(xbc_act, proj, p["dt_bias"], p["a_log"], p["d_skip"])
        mla_in = [Col(proj, Q_RANK, OFF_CQ // Q_RANK), Col(proj, KV_RANK, OFF_CKV // KV_RANK), Col(proj, LANE, OFF_KR // LANE), cos_t, sin_t]
        s["cqn"], s["ckvn"], s["krr"] = rowwise_fwd(
            f"mla_prep_fwd{l}", fn_mla_prep, mla_in, [], [p["q_norm"], p["kv_norm"], rot], [(Q_RANK, MXU_DTYPE), (KV_RANK, MXU_DTYPE), (LANE, MXU_DTYPE)]
        )
        q = matmul(f"w_uq_fwd{l}", s["cqn"], p["w_uq"], out_dtype=MXU_DTYPE)
        (s["qs"],) = rowwise_fwd(f"rope_q_fwd{l}", fn_rope_q, [q, cos_t, sin_t], [], [rot], [(q.shape[-1], MXU_DTYPE)])
        s["kv"] = matmul(f"w_ukv_fwd{l}", s["ckvn"], p["w_ukv"], out_dtype=MXU_DTYPE)
        s["o"], s["lse"] = attention_fwd(s["qs"], s["kv"], s["krr"])
        (s["ycat"],) = rowwise_fwd(
            f"mix_fwd{l}", fn_mix, [s["yscan"], Col(proj, D_SSD, 0), s["o"]], [], [p["ssd_norm"], p["attn_norm"]], [(D_SSD + D_ATT, MXU_DTYPE)]
        )
        s["ymix"] = matmul(f"w_out_fwd{l}", s["ycat"], p["w_out"])
        s["x1"], s["h2"] = rowwise_fwd(
            f"mid_fwd{l}",
            fn_resid_prenorm,
            [xin, s["ymix"]],
            [p["mods"][2], p["mods"][4], p["mods"][3]],
            [p["norm_mlp"], p["bias"][2], p["bias"][4], p["bias"][3]],
            [(D, F32), (D, MXU_DTYPE)],
        )
        s["u"] = matmul(f"w_up_fwd{l}", s["h2"], p["w_up"])
        s["a"] = conv_glu_fwd(s["u"], p["conv_ff_w"], p["conv_ff_b"])
        s["ff"] = matmul(f"w_down_fwd{l}", s["a"], p["w_down"])
        if l + 1 < DEPTH:
            n = Ls[l + 1]
            xin, h1 = rowwise_fwd(
                f"join_fwd{l}",
                fn_resid_prenorm,
                [s["x1"], s["ff"]],
                [p["mods"][5], n["mods"][1], n["mods"][0]],
                [n["norm_mix"], p["bias"][5], n["bias"][1], n["bias"][0]],
                [(D, F32), (D, MXU_DTYPE)],
            )

    G = {k: [None] * DEPTH for k in W if k not in ("final_norm", "w_ada")}
    dmod = [[None] * 6 for _ in range(DEPTH)]
    dbias = [[None] * 6 for _ in range(DEPTH)]
    last = Ls[-1]
    sl = last["saved"]
    loss, dx1, dff, dmod[-1][5], G["final_norm"], dbias[-1][5] = final_fwdbwd(
        sl["x1"], sl["ff"], target, last["mods"][5], fnorm, last["bias"][5]
    )
    grad_x = None
    for l in reversed(range(DEPTH)):
        p = Ls[l]
        s = p["saved"]
        da = matmul(f"w_down_dgrad{l}", dff, p["w_down"], tb=True)
        G["w_down"][l] = matmul(f"w_down_wgrad{l}", s["a"], dff, ta=True, out_dtype=WIRE_DTYPE)
        du_gate, du_val, G["conv_ff_w"][l], dcb = conv_glu_bwd(s["u"], p["conv_ff_w"], p["conv_ff_b"], da)
        G["conv_ff_b"][l] = dcb[0]
        dh2 = matmul(f"w_up_dgrad{l}", [du_gate, du_val], p["w_up"], tb=True)
        G["w_up"][l] = jnp.concatenate(
            [matmul(f"w_up_wgrad{l}_{half}", s["h2"], d, ta=True, out_dtype=WIRE_DTYPE) for half, d in (("gate", du_gate), ("val", du_val))],
            axis=1,
        )
        dxb, dymix, dmod[l][2], dmod[l][4], dmod[l][3], G["norm_mlp"][l], dbias[l][2], dbias[l][4], dbias[l][3] = rowwise_bwd(
            f"mid_bwd{l}",
            fn_resid_prenorm,
            [s["xin"], s["ymix"]],
            [p["mods"][2], p["mods"][4], p["mods"][3]],
            [p["norm_mlp"], p["bias"][2], p["bias"][4], p["bias"][3]],
            [dx1, dh2],
            [True, True],
            [True] * 4,
            mxu_only=(1,),
        )
        dycat = matmul(f"w_out_dgrad{l}", dymix, p["w_out"], tb=True)
        G["w_out"][l] = matmul(f"w_out_wgrad{l}", s["ycat"], dymix, ta=True, out_dtype=WIRE_DTYPE)
        dyscan, dz, do, G["ssd_norm"][l], G["attn_norm"][l] = rowwise_bwd(
            f"mix_bwd{l}", fn_mix, [s["yscan"], Col(s["proj"], D_SSD, 0), s["o"]], [], [p["ssd_norm"], p["attn_norm"]], [dycat], [True] * 3, [True] * 2, mxu_only=(1,)
        )
        dq, dkv, dkrr = attention_bwd(s["qs"], s["kv"], s["krr"], cos_t, sin_t, rot_t, s["o"], s["lse"], do)
        dcqn = matmul(f"w_uq_dgrad{l}", dq, p["w_uq"], tb=True)
        G["w_uq"][l] = wuq_from_kernel(matmul(f"w_uq_wgrad{l}", s["cqn"], dq, ta=True, out_dtype=WIRE_DTYPE))
        dckvn = matmul(f"w_ukv_dgrad{l}", dkv, p["w_ukv"], tb=True)
        G["w_ukv"][l] = matmul(f"w_ukv_wgrad{l}", s["ckvn"], dkv, ta=True, out_dtype=WIRE_DTYPE)
        proj = s["proj"]
        mla_in = [Col(proj, Q_RANK, OFF_CQ // Q_RANK), Col(proj, KV_RANK, OFF_CKV // KV_RANK), Col(proj, LANE, OFF_KR // LANE), cos_t, sin_t]
        dcq, dckv, dkr, G["q_norm"][l], G["kv_norm"][l] = rowwise_bwd(
            f"mla_prep_bwd{l}",
            fn_mla_prep,
            mla_in,
            [],
            [p["q_norm"], p["kv_norm"], rot],
            [dcqn, dckvn, dkrr],
            [True, True, True, False, False],
            [True, True, False],
            mxu_only=(0, 1, 2),
        )
        dxs, dbm, dcm, ddt, ddb, dal, dds = ssd_bwd(s["xbc_act"], proj, p["dt_bias"], p["a_log"], p["d_skip"], s["states"], dyscan)
        G["dt_bias"][l], G["a_log"][l], G["d_skip"][l] = ddb[0, :SSD_HEADS], dal[0, :SSD_HEADS], dds[0, :SSD_HEADS]
        dxbc, G["conv_w"][l], dcb = conv_silu_bwd(proj, p["conv_w"], p["conv_b"], [dxs, dbm, dcm])
        G["conv_b"][l] = dcb[0]
        dproj = jnp.concatenate([dz, dxbc, dckv, dkr, ddt.astype(MXU_DTYPE), dcq], axis=-1)
        dh1 = matmul(f"w_in_dgrad{l}", dproj, p["w_in"], tb=True)
        G["w_in"][l] = win_from_kernel(matmul(f"w_in_wgrad{l}", s["h1"], dproj, ta=True, out_dtype=WIRE_DTYPE))
        if l > 0:
            q = Ls[l - 1]
            sq = q["saved"]
            dx1, dff, dmod[l - 1][5], dmod[l][1], dmod[l][0], G["norm_mix"][l], dbias[l - 1][5], dbias[l][1], dbias[l][0] = rowwise_bwd(
                f"join_bwd{l - 1}",
                fn_resid_prenorm,
                [sq["x1"], sq["ff"]],
                [q["mods"][5], p["mods"][1], p["mods"][0]],
                [p["norm_mix"], q["bias"][5], p["bias"][1], p["bias"][0]],
                [dxb, dh1],
                [True, True],
                [True] * 4,
                mxu_only=(1,),
            )
        else:
            grad_x, dmod[0][1], dmod[0][0], G["norm_mix"][0], dbias[0][1], dbias[0][0] = rowwise_bwd(
                "prenorm_bwd",
                fn_prenorm,
                [x],
                [p["mods"][1], p["mods"][0]],
                [p["norm_mix"], p["bias"][1], p["bias"][0]],
                [dh1],
                [True],
                [True] * 3,
                adds={0: dxb},
            )
    for l in range(DEPTH):
        G["b_ada"][l] = jnp.concatenate([d[0] for d in dbias[l]])
        for k in ("norm_mix", "ssd_norm", "q_norm", "kv_norm", "attn_norm", "norm_mlp"):
            G[k][l] = G[k][l][0]
    grads = {k: (v if k in BIG else jnp.stack(v) if isinstance(v, list) else v[0]) for k, v in G.items()}
    dmod_raw = jnp.stack([jnp.stack([d[:, 0, :] for d in dmod[l]], axis=1) for l in range(DEPTH)])
    return loss, grad_x, grads, dmod_raw


MESH = pl.DeviceIdType.MESH
ANY = pl.BlockSpec(memory_space=pl.ANY)
PACK_W = 1024
PACK_TILE = 768


def _place():
    x, y, c = lax.axis_index("x"), lax.axis_index("y"), lax.axis_index("c")
    chips = [(1 - x, y), (x, 1 - y), (1 - x, 1 - y)]
    return x, y, c, chips


def _remote(src, dst, send_sem, recv_sem, to):
    return pltpu.make_async_remote_copy(src_ref=src, dst_ref=dst, send_sem=send_sem, recv_sem=recv_sem, device_id=to, device_id_type=MESH)


def all_gather8(name, v):
    m_per, n = v.shape

    def body(x_ref, out_ref, send_sems, recv_sems, local_sem):
        x, y, c, chips = _place()
        me, sibling = (x, y, c), (x, y, 1 - c)

        def rows(px, py, pc):
            return out_ref.at[pl.ds((4 * px + 2 * py + pc) * m_per, m_per), :]

        def copy(k, block, to, src=None):
            return _remote(rows(*block) if src is None else src, rows(*block), send_sems.at[k], recv_sems.at[k], to)

        mine = pltpu.make_async_copy(x_ref, rows(*me), local_sem)
        mine.start()
        first = [copy(0, me, sibling, src=x_ref)]
        first += [copy(1 + j, me, (*chip, c), src=x_ref) for j, chip in enumerate(chips)]
        for cp in first:
            cp.start()
        passed = [copy(4 + j, (*chip, c), sibling) for j, chip in enumerate(chips)]
        for j, chip in enumerate(chips):
            copy(1 + j, (*chip, c), me).wait_recv()
            passed[j].start()
        copy(0, sibling, me).wait_recv()
        for j, chip in enumerate(chips):
            copy(4 + j, (*chip, 1 - c), me).wait_recv()
        for cp in first + passed:
            cp.wait_send()
        mine.wait()

    return pl.pallas_call(
        body,
        name=name,
        out_shape=jax.ShapeDtypeStruct((N_DEV * m_per, n), v.dtype),
        in_specs=[pl.BlockSpec(memory_space=pltpu.VMEM)],
        out_specs=pl.BlockSpec(memory_space=pltpu.VMEM),
        scratch_shapes=[pltpu.SemaphoreType.DMA((7,)), pltpu.SemaphoreType.DMA((7,)), pltpu.SemaphoreType.DMA],
        compiler_params=pltpu.CompilerParams(vmem_limit_bytes=VMEM_LIMIT),
    )(v)


def gather_weights(pack):
    R, n = pack.shape
    rh = R // 2

    def body(x_ref, out_ref, send_sems, recv_sems):
        x, y, c, chips = _place()
        me = 2 * x + y

        def half(chip, hc):
            return out_ref.at[chip, pl.ds(hc * rh, rh), :]

        src = x_ref.at[pl.ds(c * rh, rh), :]
        first = [_remote(src, half(me, c), send_sems.at[j], recv_sems.at[j], (px, py, c)) for j, (px, py) in enumerate(chips)]
        for cp in first:
            cp.start()
        passed = []
        for j, (px, py) in enumerate(chips):
            got = half(2 * px + py, c)
            _remote(got, got, send_sems.at[j], recv_sems.at[j], (px, py, c)).wait_recv()
            cp = _remote(got, got, send_sems.at[3 + j], recv_sems.at[3 + j], (x, y, 1 - c))
            cp.start()
            passed.append(cp)
        for j, (px, py) in enumerate(chips):
            got = half(2 * px + py, 1 - c)
            _remote(got, got, send_sems.at[3 + j], recv_sems.at[3 + j], (x, y, 1 - c)).wait_recv()
        for cp in first + passed:
            cp.wait_send()

    return pl.pallas_call(
        body,
        name="gather_weights",
        out_shape=jax.ShapeDtypeStruct((N_CHIPS, R, n), pack.dtype),
        in_specs=[ANY],
        out_specs=ANY,
        scratch_shapes=[pltpu.SemaphoreType.DMA((6,)), pltpu.SemaphoreType.DMA((6,))],
    )(pack)


def swap_halves(g):
    n_slot, R, n = g.shape
    rh = R // 2

    def body(g_ref, got_ref, send_sem, recv_sem):
        x, y, c, _ = _place()
        cp = _remote(g_ref.at[:, pl.ds((1 - c) * rh, rh), :], got_ref, send_sem, recv_sem, (x, y, 1 - c))
        cp.start()
        cp.wait()

    return pl.pallas_call(
        body,
        name="swap_halves",
        out_shape=jax.ShapeDtypeStruct((n_slot, rh, n), g.dtype),
        in_specs=[ANY],
        out_specs=ANY,
        scratch_shapes=[pltpu.SemaphoreType.DMA, pltpu.SemaphoreType.DMA],
    )(g)


def scatter_chips(buf):
    def body(s_ref, got_ref, send_sems, recv_sems):
        x, y, c, chips = _place()
        me = 2 * x + y
        cps = [_remote(s_ref.at[2 * px + py], got_ref.at[me], send_sems.at[j], recv_sems.at[j], (px, py, c)) for j, (px, py) in enumerate(chips)]
        for cp in cps:
            cp.start()
        for j, (px, py) in enumerate(chips):
            got = got_ref.at[2 * px + py]
            _remote(got, got, send_sems.at[j], recv_sems.at[j], (px, py, c)).wait_recv()
        for cp in cps:
            cp.wait_send()

    return pl.pallas_call(
        body,
        name="scatter_chips",
        out_shape=jax.ShapeDtypeStruct(buf.shape, buf.dtype),
        in_specs=[ANY],
        out_specs=ANY,
        scratch_shapes=[pltpu.SemaphoreType.DMA((3,)), pltpu.SemaphoreType.DMA((3,))],
    )(buf)


def swap_with_sibling(h):
    def body(h_ref, got_ref, send_sem, recv_sem):
        x, y, c, _ = _place()
        cp = _remote(h_ref, got_ref, send_sem, recv_sem, (x, y, 1 - c))
        cp.start()
        cp.wait()

    return pl.pallas_call(
        body,
        name="swap_with_sibling",
        out_shape=jax.ShapeDtypeStruct(h.shape, h.dtype),
        in_specs=[ANY],
        out_specs=ANY,
        scratch_shapes=[pltpu.SemaphoreType.DMA, pltpu.SemaphoreType.DMA],
    )(h)


def chip_sum(g, got, core, chip):
    n_slot, R, n = g.shape
    rh = R // 2
    nb = rh // PACK_TILE

    def body(pos, g_ref, got_ref, wire_ref, own_ref):
        k = pl.program_id(1)
        s = g_ref[0].astype(F32) + got_ref[0].astype(F32)
        wire_ref[0] = s.astype(wire_ref.dtype)

        @pl.when(k == pos[1])
        def _():
            own_ref[...] = s

    grid_spec = pltpu.PrefetchScalarGridSpec(
        num_scalar_prefetch=1,
        grid=(nb, n_slot),
        in_specs=[
            pl.BlockSpec((1, PACK_TILE, n), lambda i, k, pos: (k, pos[0] * nb + i, 0)),
            pl.BlockSpec((1, PACK_TILE, n), lambda i, k, pos: (k, i, 0)),
        ],
        out_specs=[
            pl.BlockSpec((1, PACK_TILE, n), lambda i, k, pos: (k, i, 0)),
            pl.BlockSpec((PACK_TILE, n), lambda i, k, pos: (i, 0)),
        ],
    )
    return pl.pallas_call(
        body,
        name="chip_sum",
        grid_spec=grid_spec,
        out_shape=[jax.ShapeDtypeStruct((n_slot, rh, n), WIRE_DTYPE), jax.ShapeDtypeStruct((rh, n), F32)],
        compiler_params=_cparams(2),
    )(jnp.stack([core, chip]).astype(jnp.int32), g, got)


def mesh_sum(own, got, chip):
    rh, n = own.shape
    n_slot = got.shape[0]

    def body(pos, own_ref, *refs):
        out_ref = refs[-1]
        acc = own_ref[...]
        for k in range(n_slot):
            acc = acc + jnp.where(k != pos[0], refs[k][0].astype(F32), 0.0)
        out_ref[...] = acc

    grid_spec = pltpu.PrefetchScalarGridSpec(
        num_scalar_prefetch=1,
        grid=(rh // PACK_TILE,),
        in_specs=[pl.BlockSpec((PACK_TILE, n), lambda i, pos: (i, 0))]
        + [pl.BlockSpec((1, PACK_TILE, n), lambda i, pos, k=k: (k, i, 0)) for k in range(n_slot)],
        out_specs=pl.BlockSpec((PACK_TILE, n), lambda i, pos: (i, 0)),
    )
    return pl.pallas_call(
        body, name="mesh_sum", grid_spec=grid_spec, out_shape=jax.ShapeDtypeStruct((rh, n), F32), compiler_params=_cparams(1)
    )(jnp.stack([chip]).astype(jnp.int32), own, *([got] * n_slot))


def sum_devices(name, v):
    def body(v_ref, o_ref):
        acc = v_ref[0]
        for d in range(1, N_DEV):
            acc = acc + v_ref[d]
        o_ref[...] = acc

    return pl.pallas_call(body, name=name, out_shape=jax.ShapeDtypeStruct(v.shape[1:], F32))(v)


def adamw(name, w, g, m, v):
    shape = w.shape
    w2, g2, m2, v2 = (t.reshape(-1, shape[-1]) for t in (w, g, m, v))
    rows, n = w2.shape
    tr = _pick(rows, (256, 128, 64, 32, 16, 8))
    c1 = 1.0 / (1.0 - ADAM_B1**ADAM_STEP)
    c2 = 1.0 / (1.0 - ADAM_B2**ADAM_STEP)

    def body(w_ref, g_ref, m_ref, v_ref, d_ref, nm_ref, nv_ref):
        gv = g_ref[...]
        nm = ADAM_B1 * m_ref[...] + (1.0 - ADAM_B1) * gv
        nv = ADAM_B2 * v_ref[...] + (1.0 - ADAM_B2) * jnp.square(gv)
        d_ref[...] = -ADAM_LR * ((nm * c1) / (jnp.sqrt(nv * c2) + ADAM_EPS) + ADAM_WD * w_ref[...])
        nm_ref[...] = nm
        nv_ref[...] = nv

    spec = pl.BlockSpec((tr, n), lambda i: (i, 0))
    outs = pl.pallas_call(
        body,
        name=name,
        grid=(rows // tr,),
        in_specs=[spec] * 4,
        out_specs=[spec] * 3,
        out_shape=[jax.ShapeDtypeStruct((rows, n), F32)] * 3,
        compiler_params=_cparams(1),
    )(w2, g2, m2, v2)
    return [o.reshape(shape) for o in outs]


def ada_fwd(c_all, w_ada):
    n_tok, d = c_all.shape
    depth, _, cols = w_ada.shape
    tn = _pick(cols, (512, 384, 256, 128))

    def body(c_ref, w_ref, o_ref):
        o_ref[0] = jnp.dot(_silu(c_ref[...]).astype(MXU_DTYPE), w_ref[0].astype(MXU_DTYPE), preferred_element_type=F32)

    return pl.pallas_call(
        body,
        name="ada_fwd",
        grid=(depth, cols // tn),
        in_specs=[pl.BlockSpec((n_tok, d), lambda l, j: (0, 0)), pl.BlockSpec((1, d, tn), lambda l, j: (l, 0, j))],
        out_specs=pl.BlockSpec((1, n_tok, tn), lambda l, j: (l, 0, j)),
        out_shape=jax.ShapeDtypeStruct((depth, n_tok, cols), F32),
        compiler_params=_cparams(2),
    )(c_all, w_ada)


COL_SHARDED = ("w_in", "w_uq", "w_ukv", "w_up")


PART_ROWS = 16


def _part_rows(shape):
    n = int(np.prod(shape)) // PACK_W
    return -(-n // PART_ROWS) * PART_ROWS, n


def _with_fill(parts, axis, rows):
    out, used = [], 0
    for p in parts:
        pad = -p.shape[axis] % PART_ROWS
        out.append(p)
        if pad:
            out.append(jnp.zeros(p.shape[:axis] + (pad,) + p.shape[axis + 1 :], p.dtype))
        used += p.shape[axis] + pad
    out.append(jnp.zeros(parts[0].shape[:axis] + (rows - used,) + parts[0].shape[axis + 1 :], parts[0].dtype))
    return jnp.concatenate(out, axis=axis)


def _pack_rows(parts, rows):
    return _with_fill([p.reshape(-1, PACK_W) for p in parts], 0, rows)


def _big_rows(shards):
    n = sum(_part_rows(shards[k].shape)[0] for k in BIG)
    return -(-n // (2 * PACK_TILE)) * 2 * PACK_TILE


def unpack_gathered(gathered, shard_shapes):
    out, r0 = {}, 0
    for k in BIG:
        shp = shard_shapes[k]
        step, n = _part_rows(shp)
        seg = gathered[:, r0 : r0 + n].reshape(N_CHIPS, *shp)
        r0 += step
        if k in COL_SHARDED:
            out[k] = seg.transpose(1, 2, 0, 3).reshape(shp[0], shp[1], N_CHIPS * shp[2])
        else:
            out[k] = seg.transpose(1, 0, 2, 3).reshape(shp[0], N_CHIPS * shp[1], shp[2])
    return out


def pack_full_grads(grads, shard_shapes, rows):
    parts = []
    for k in BIG:
        _, rows_k, cols_k = shard_shapes[k]
        layers = grads[k]
        if (rows_k * cols_k // PACK_W) % PART_ROWS:
            layers = [jnp.concatenate(layers, axis=0)]
        for g in layers:
            if k in COL_SHARDED:
                g = g.reshape(g.shape[0], N_CHIPS, cols_k).transpose(1, 0, 2)
            else:
                g = g.reshape(-1, N_CHIPS, rows_k, cols_k).transpose(1, 0, 2, 3)
            parts.append(g.reshape(N_CHIPS, -1, PACK_W))
    return _with_fill(parts, 1, rows)


def unpack_shards(buf, shard_shapes):
    out, r0 = {}, 0
    for k in BIG:
        step, n = _part_rows(shard_shapes[k])
        out[k] = buf[r0 : r0 + n].reshape(shard_shapes[k])
        r0 += step
    return out


def _flat_pack(arrs, row_multiple=8):
    flat = jnp.concatenate([a.reshape(-1) for a in arrs])
    per = PACK_W * row_multiple
    n = -(-flat.shape[0] // per) * per
    return jnp.pad(flat, (0, n - flat.shape[0])).reshape(-1, PACK_W)


def _flat_unpack(buf, shapes):
    flat, out, o = buf.reshape(-1), [], 0
    for s in shapes:
        n = int(np.prod(s))
        out.append(flat[o : o + n].reshape(s))
        o += n
    return out


WEIGHTS = (
    "w_ada", "b_ada", "norm_mix", "w_in", "conv_w", "conv_b", "dt_bias", "a_log", "d_skip", "ssd_norm", "q_norm", "w_uq",
    "kv_norm", "w_ukv", "attn_norm", "w_out", "norm_mlp", "w_up", "conv_ff_w", "conv_ff_b", "w_down", "final_norm",
)
REPLICATED = ("b_ada", "norm_mix", "conv_b", "dt_bias", "a_log", "d_skip", "ssd_norm", "q_norm", "kv_norm", "attn_norm",
              "norm_mlp", "conv_ff_b", "final_norm")
CONV_SHARDED = ("conv_w", "conv_ff_w")


def kernel(x, c, positions, w_ada, b_ada, norm_mix, w_in, conv_w, conv_b, dt_bias, a_log, d_skip, ssd_norm, q_norm, w_uq, kv_norm, w_ukv, attn_norm, w_out, norm_mlp, w_up, conv_ff_w, conv_ff_b, w_down, final_norm, loss_target, m_w_ada, m_b_ada, m_norm_mix, m_w_in, m_conv_w, m_conv_b, m_dt_bias, m_a_log, m_d_skip, m_ssd_norm, m_q_norm, m_w_uq, m_kv_norm, m_w_ukv, m_attn_norm, m_w_out, m_norm_mlp, m_w_up, m_conv_ff_w, m_conv_ff_b, m_w_down, m_final_norm, v_w_ada, v_b_ada, v_norm_mix, v_w_in, v_conv_w, v_conv_b, v_dt_bias, v_a_log, v_d_skip, v_ssd_norm, v_q_norm, v_w_uq, v_kv_norm, v_w_ukv, v_attn_norm, v_w_out, v_norm_mlp, v_w_up, v_conv_ff_w, v_conv_ff_b, v_w_down, v_final_norm):
    loc = locals()
    Wl = {k: loc[k] for k in WEIGHTS}
    Ml = {k: loc["m_" + k] for k in WEIGHTS}
    Vl = {k: loc["v_" + k] for k in WEIGHTS}
    xi, yi, ci = lax.axis_index("x"), lax.axis_index("y"), lax.axis_index("c")
    chip = 2 * xi + yi
    dev = 2 * chip + ci
    B, S, D = x.shape
    n_tok = N_DEV * B

    c_all = all_gather8("gather_c", c.reshape(8, -1)).reshape(n_tok, D)
    mod_cols = ada_fwd(c_all, w_ada)
    cols = mod_cols.shape[-1]
    half = n_tok // 2
    mod_mine = lax.dynamic_slice_in_dim(mod_cols, ci * half, half, axis=1)
    small_in = _flat_pack([mod_mine, conv_w, conv_ff_w])
    n_mod = mod_mine.size // PACK_W
    small_all = all_gather8("gather_mod", small_in).reshape(N_CHIPS, 2, -1, PACK_W)
    mod_all = small_all[:, :, :n_mod].reshape(N_CHIPS, 2, DEPTH, half, cols).transpose(2, 1, 3, 0, 4).reshape(DEPTH, n_tok, N_CHIPS * cols)
    mod_raw = lax.dynamic_slice_in_dim(mod_all, dev * B, B, axis=1).reshape(DEPTH, B, 6, D)
    conv_parts = [_flat_unpack(small_all[k, 0, n_mod:], [conv_w.shape, conv_ff_w.shape]) for k in range(N_CHIPS)]
    conv_full = {name: jnp.concatenate([conv_parts[k][i] for k in range(N_CHIPS)], axis=-1) for i, name in enumerate(CONV_SHARDED)}

    shard_shapes = {k: Wl[k].shape for k in BIG}
    rows = _big_rows(Wl)
    pack = _pack_rows([Wl[k].astype(MXU_DTYPE) for k in BIG], rows)
    gathered = lax.dynamic_update_slice_in_dim(gather_weights(pack), pack[None], chip, axis=0)
    W = unpack_gathered(gathered, shard_shapes)
    W.update({k: Wl[k] for k in REPLICATED})
    W.update(conv_full)

    loss_lanes, grad_x, grads, dmod_raw = local_step(x, loss_target, positions, mod_raw, W)
    loss = lax.psum(loss_lanes[0, 0], ("x", "y", "c"))

    gpack = pack_full_grads(grads, shard_shapes, rows)
    wire, own = chip_sum(gpack, swap_halves(gpack), ci, chip)
    mine = mesh_sum(own, scatter_chips(wire), chip)
    theirs = swap_with_sibling(mine)
    both = jnp.concatenate([jnp.where(ci == 0, mine, theirs), jnp.where(ci == 0, theirs, mine)], axis=0)
    g_shard = unpack_shards(both, shard_shapes)

    small_names = REPLICATED + CONV_SHARDED
    small_out = _flat_pack([grads[k] for k in small_names] + [dmod_raw])
    small_got = all_gather8("gather_small", small_out).reshape(N_DEV, -1, PACK_W)
    small_sum = _flat_unpack(sum_devices("sum_small", small_got), [grads[k].shape for k in small_names])
    G = dict(zip(small_names, small_sum))
    for name in CONV_SHARDED:
        width = Wl[name].shape[-1]
        G[name] = lax.dynamic_slice_in_dim(G[name], chip * width, width, axis=-1)
    n_small = sum(grads[k].size for k in small_names)
    dmod_all = small_got.reshape(N_DEV, -1)[:, n_small : n_small + dmod_raw.size].reshape(N_DEV, DEPTH, B, 6 * D)
    dmod_all = dmod_all.transpose(1, 0, 2, 3).reshape(DEPTH, n_tok, 6 * D)
    dmod_cols = lax.dynamic_slice_in_dim(dmod_all, chip * cols, cols, axis=-1)
    G["w_ada"] = jnp.stack([matmul(f"w_ada_wgrad{l}", c_all, dmod_cols[l], ta=True, a_act=_silu) for l in range(DEPTH)])
    G.update(g_shard)

    deltas, new_m, new_v = {}, {}, {}
    small_upd = adamw("adamw_small", *[_flat_pack([t[k] for k in REPLICATED]) for t in (Wl, G, Ml, Vl)])
    for res, t in zip((deltas, new_m, new_v), small_upd):
        res.update(zip(REPLICATED, _flat_unpack(t, [Wl[k].shape for k in REPLICATED])))
    for k in WEIGHTS:
        if k not in REPLICATED:
            deltas[k], new_m[k], new_v[k] = adamw("adamw_" + k, Wl[k], G[k], Ml[k], Vl[k])
    return (loss, grad_x, *[G[k] for k in WEIGHTS], *[deltas[k] for k in WEIGHTS], *[new_m[k] for k in WEIGHTS], *[new_v[k] for k in WEIGHTS])
```

```python
import functools
from typing import NamedTuple

import numpy as np
import jax
import jax.numpy as jnp
from jax import lax
from jax.experimental import pallas as pl
from jax.experimental.pallas import tpu as pltpu

F32 = jnp.float32
BF16 = jnp.bfloat16
MXU_DTYPE = jnp.bfloat16
WIRE_DTYPE = jnp.bfloat16
HIGHEST = lax.Precision.HIGHEST

D_MODEL = 1024
DEPTH = 2
D_SSD = 1024
SSD_HEADS = 16
SSD_HEAD_DIM = 64
SSD_GROUPS = 2
SSD_STATE = 128
SSD_CONV = 4
CHUNK = 128
MLA_HEADS = 8
QK_NOPE = 128
QK_ROPE = 64
V_DIM = 128
D_ATT = MLA_HEADS * V_DIM
Q_RANK = 384
KV_RANK = 256
ROPE_BASE = 10000.0
D_FF = 2816
FF_CONV = 3
EPS = 1e-6
D_XBC = D_SSD + 2 * SSD_GROUPS * SSD_STATE
D_IN = D_SSD + D_XBC + SSD_HEADS + Q_RANK + KV_RANK + QK_ROPE
ADAM_LR, ADAM_B1, ADAM_B2, ADAM_EPS, ADAM_WD, ADAM_STEP = 0.001, 0.9, 0.999, 1e-08, 0.01, 10

LANE = 128
N_CHIPS = 4
N_DEV = 8

OFF_Z, OFF_XBC, OFF_CKV, OFF_KR, OFF_DT, OFF_CQ = 0, 1024, 2560, 2816, 2944, 3072
N_PROJ = 3456
Q_HEAD_PAD = 256
VMEM_LIMIT = 56 * 1024 * 1024


def _cparams(n_axes):
    return pltpu.CompilerParams(dimension_semantics=("arbitrary",) * n_axes, vmem_limit_bytes=VMEM_LIMIT)


def _pick(n, prefs):
    for p in prefs:
        if n % p == 0:
            return p
    return n


def _silu(x):
    return x * jax.nn.sigmoid(x)


def _rms(x, g):
    return x * lax.rsqrt(jnp.mean(x * x, axis=-1, keepdims=True) + EPS) * g


def _softplus(x):
    return jnp.maximum(x, 0.0) + jnp.log(1.0 + jnp.exp(-jnp.abs(x)))


class Col(NamedTuple):
    arr: jax.Array
    width: int
    blk: int


def _as_col(a):
    return a if isinstance(a, Col) else Col(a, a.shape[-1], 0)


def _row_specs(tiled, pbatch, shared, tile):
    specs = [pl.BlockSpec((1, tile, t.width), lambda b, i, blk=t.blk: (b, i, blk)) for t in tiled]
    specs += [pl.BlockSpec((1, 1, p.shape[-1]), lambda b, i: (b, 0, 0)) for p in pbatch]
    specs += [pl.BlockSpec(s.shape, lambda b, i: (0, 0)) for s in shared]
    return specs


def _row_vals(refs, nt, npb):
    return [r[0].astype(F32) for r in refs[: nt + npb]] + [r[...].astype(F32) for r in refs[nt + npb :]]


def rowwise_fwd(name, fn, tiled, pbatch, shared, outs, tile=512):
    tiled = [_as_col(t) for t in tiled]
    B, S = tiled[0].arr.shape[:2]
    tile = min(tile, S)
    nt, npb, nsh = len(tiled), len(pbatch), len(shared)
    n_in = nt + npb + nsh

    def body(*refs):
        res = fn(*_row_vals(refs[:n_in], nt, npb))
        for r, v in zip(refs[n_in:], res):
            r[0] = v.astype(r.dtype)

    return pl.pallas_call(
        body,
        name=name,
        grid=(B, S // tile),
        in_specs=_row_specs(tiled, pbatch, shared, tile),
        out_specs=[pl.BlockSpec((1, tile, w), lambda b, i: (b, i, 0)) for w, _ in outs],
        out_shape=[jax.ShapeDtypeStruct((B, S, w), dt) for w, dt in outs],
        compiler_params=_cparams(2),
    )(*[t.arr for t in tiled], *pbatch, *shared)


def rowwise_bwd(name, fn, tiled, pbatch, shared, cts, grad_tiled, grad_shared, adds=None, tile=256, mxu_only=()):
    tiled = [_as_col(t) for t in tiled]
    adds = adds or {}
    B, S = tiled[0].arr.shape[:2]
    tile = min(tile, S)
    nt, npb, nsh = len(tiled), len(pbatch), len(shared)
    n_in = nt + npb + nsh
    add_idx = sorted(adds)
    gt = [i for i in range(nt) if grad_tiled[i]]
    gs = [i for i in range(nsh) if grad_shared[i]]
    diff = gt + [nt + i for i in range(npb)] + [nt + npb + i for i in gs]
    n_ct, n_add = len(cts), len(add_idx)

    def body(*refs):
        vals = _row_vals(refs[:n_in], nt, npb)
        ct_v = tuple(r[0].astype(F32) if c.ndim == 3 else jnp.sum(r[0].astype(F32), axis=0) for r, c in zip(refs[n_in : n_in + n_ct], cts))
        add_v = {i: r[0].astype(F32) for i, r in zip(add_idx, refs[n_in + n_ct : n_in + n_ct + n_add])}
        out_refs = refs[n_in + n_ct + n_add :]

        def f(*dargs):
            full = list(vals)
            for k, i in enumerate(diff):
                full[i] = dargs[k]
            return tuple(fn(*full))

        _, vjp = jax.vjp(f, *[vals[i] for i in diff])
        grads = vjp(ct_v)
        b, i = pl.program_id(0), pl.program_id(1)
        k = 0
        for idx in gt:
            g = grads[k]
            if idx in add_v:
                g = g + add_v[idx]
            out_refs[k][0] = g.astype(out_refs[k].dtype)
            k += 1
        for _ in range(npb):
            r, g = out_refs[k], grads[k]

            @pl.when(i == 0)
            def _(r=r, g=g):
                r[0] = g

            @pl.when(i > 0)
            def _(r=r, g=g):
                r[0] += g

            k += 1
        for _ in gs:
            r, g = out_refs[k], grads[k]

            @pl.when((i == 0) & (b == 0))
            def _(r=r, g=g):
                r[...] = g

            @pl.when((i > 0) | (b > 0))
            def _(r=r, g=g):
                r[...] += g

            k += 1

    out_specs = [pl.BlockSpec((1, tile, tiled[i].width), lambda b, i: (b, i, 0)) for i in gt]
    out_shape = [jax.ShapeDtypeStruct((B, S, tiled[i].width), MXU_DTYPE if i in mxu_only else F32) for i in gt]
    out_specs += [pl.BlockSpec((1, 1, p.shape[-1]), lambda b, i: (b, 0, 0)) for p in pbatch]
    out_shape += [jax.ShapeDtypeStruct(p.shape, F32) for p in pbatch]
    out_specs += [pl.BlockSpec(shared[i].shape, lambda b, i: (0, 0)) for i in gs]
    out_shape += [jax.ShapeDtypeStruct(shared[i].shape, F32) for i in gs]
    in_specs = _row_specs(tiled, pbatch, shared, tile)
    for c in cts:
        if c.ndim == 3:
            in_specs.append(pl.BlockSpec((1, tile, c.shape[-1]), lambda b, i: (b, i, 0)))
        else:
            in_specs.append(pl.BlockSpec((1, c.shape[1], tile, c.shape[-1]), lambda b, i: (b, 0, i, 0)))
    in_specs += [pl.BlockSpec((1, tile, adds[i].shape[-1]), lambda b, i: (b, i, 0)) for i in add_idx]
    return pl.pallas_call(
        body,
        name=name,
        grid=(B, S // tile),
        in_specs=in_specs,
        out_specs=out_specs,
        out_shape=out_shape,
        compiler_params=_cparams(2),
    )(*[t.arr for t in tiled], *pbatch, *shared, *cts, *[adds[i] for i in add_idx])


def fn_prenorm(x, sc, sh, g, bsc, bsh):
    return (_rms(x, g) * (1.0 + sc + bsc) + (sh + bsh),)


def fn_resid_prenorm(x, y, gate, sc, sh, g, bgate, bsc, bsh):
    x1 = x + (gate + bgate) * y
    return x1, _rms(x1, g) * (1.0 + sc + bsc) + (sh + bsh)


def fn_mix(yscan, z, o, ssd_norm, attn_norm):
    y = yscan * _silu(z)
    half = D_SSD // SSD_GROUPS
    first = lax.broadcasted_iota(jnp.int32, y.shape, 1) < half
    sq = y * y
    m0 = jnp.sum(jnp.where(first, sq, 0.0), axis=-1, keepdims=True) / half
    m1 = jnp.sum(jnp.where(first, 0.0, sq), axis=-1, keepdims=True) / half
    r = jnp.where(first, lax.rsqrt(m0 + EPS), lax.rsqrt(m1 + EPS))
    return (jnp.concatenate([y * r * ssd_norm, _rms(o, attn_norm)], axis=-1),)


def fn_mla_prep(cq, ckv, kr, cos_t, sin_t, q_norm, kv_norm, rot):
    krr = kr * cos_t + jnp.dot(kr, rot, precision=HIGHEST, preferred_element_type=F32) * sin_t
    return _rms(cq, q_norm), _rms(ckv, kv_norm), krr


def final_fwdbwd(x1, ff, target, gate, fnorm, bgate, tile=256):
    B, S, D = x1.shape
    tile = min(tile, S)

    def body(x_ref, f_ref, t_ref, g_ref, n_ref, bg_ref, loss_ref, dx_ref, df_ref, dg_ref, dn_ref, dbg_ref):
        b, i = pl.program_id(0), pl.program_id(1)
        tgt = t_ref[0]

        def f(x, y, gate, fn, bg):
            yf = _rms(x + (gate + bg) * y, fn)
            return 0.5 * jnp.sum(jnp.mean(jnp.square(yf - tgt), axis=-1, keepdims=True), axis=0, keepdims=True)

        val, vjp = jax.vjp(f, x_ref[0], f_ref[0], g_ref[0], n_ref[...], bg_ref[...])
        dx, dff, dg, dn, dbg = vjp(jnp.ones((1, 1), F32))
        dx_ref[0] = dx
        df_ref[0] = dff.astype(df_ref.dtype)
        lane_loss = jnp.broadcast_to(val, (1, LANE))

        @pl.when(i == 0)
        def _():
            dg_ref[0] = dg

        @pl.when(i > 0)
        def _():
            dg_ref[0] += dg

        @pl.when((i == 0) & (b == 0))
        def _():
            dn_ref[...] = dn
            dbg_ref[...] = dbg
            loss_ref[...] = lane_loss

        @pl.when((i > 0) | (b > 0))
        def _():
            dn_ref[...] += dn
            dbg_ref[...] += dbg
            loss_ref[...] += lane_loss

    tok = pl.BlockSpec((1, tile, D), lambda b, i: (b, i, 0))
    pb = pl.BlockSpec((1, 1, D), lambda b, i: (b, 0, 0))
    sh = pl.BlockSpec((1, D), lambda b, i: (0, 0))
    return pl.pallas_call(
        body,
        name="final_loss",
        grid=(B, S // tile),
        in_specs=[tok, tok, tok, pb, sh, sh],
        out_specs=[pl.BlockSpec((1, LANE), lambda b, i: (0, 0)), tok, tok, pb, sh, sh],
        out_shape=[
            jax.ShapeDtypeStruct((1, LANE), F32),
            jax.ShapeDtypeStruct((B, S, D), F32),
            jax.ShapeDtypeStruct((B, S, D), MXU_DTYPE),
            jax.ShapeDtypeStruct((B, 1, D), F32),
            jax.ShapeDtypeStruct((1, D), F32),
            jax.ShapeDtypeStruct((1, D), F32),
        ],
        compiler_params=_cparams(2),
    )(x1, ff, target, gate, fnorm, bgate)


MATMUL_VMEM_BUDGET = 40 * 1024 * 1024
STEP_COST_BYTES = 1.2e6
MXU_DIM = 256
MXU_FLOPS_PER_BYTE = 280.0


def _tile_options(n, cap):
    opts = [d for d in range(LANE, min(n, cap) + 1, LANE) if n % d == 0]
    return opts or [n]


def _matmul_tiles(M, N, K, sa, sb, so):
    best = None
    for tk in [K] + [d for d in _tile_options(K, 2048) if d >= 512 and d < K]:
        nk = K // tk
        for tm in _tile_options(M, 2048):
            for tn in _tile_options(N, 1408):
                acc = 0 if (so == 4 or nk == 1) else tm * tn * 4
                vmem = 2 * (tm * tk * sa + tk * tn * sb + tm * tn * so) + acc + tm * tn * 4 + 2 * (tm * tk + tk * tn)
                if vmem > MATMUL_VMEM_BUDGET:
                    continue
                a_reads = M * K * sa * (1 if nk == 1 else N // tn)
                b_reads = K * N * sb * (1 if (nk == 1 and N == tn) else M // tm)
                steps = (M // tm) * (N // tn) * nk
                hbm = a_reads + b_reads + M * N * so + (2 * M * N * 4 * (nk - 1) if nk > 1 else 0) / 8
                fill = (-(-tn // MXU_DIM) * MXU_DIM / tn) * (-(-tk // MXU_DIM) * MXU_DIM / tk)
                mxu = 2.0 * M * N * K * fill / MXU_FLOPS_PER_BYTE
                cost = max(hbm, mxu) + steps * STEP_COST_BYTES
                if best is None or cost < best[0]:
                    best = (cost, tm, tn, tk)
    assert best is not None, (M, N, K)
    return best[1:]


class Layer(NamedTuple):
    arr: jax.Array
    l: int


def matmul(name, a, b, ta=False, tb=False, out_dtype=F32, a_act=None):
    pieces = list(a) if isinstance(a, (list, tuple)) else [a]
    assert len(pieces) == 1 or not ta
    lead = pieces[0].shape[:2] if (pieces[0].ndim == 3 and not ta) else None
    pieces = [p.reshape(-1, p.shape[-1]) if p.ndim == 3 else p for p in pieces]
    layer = None
    if isinstance(b, Layer):
        b, layer = b
        b_rows, b_cols = b.shape[1:]
    else:
        if b.ndim == 3:
            b = b.reshape(-1, b.shape[-1])
        b_rows, b_cols = b.shape
    Kp, M = pieces[0].shape if ta else pieces[0].shape[::-1]
    N = b_rows if tb else b_cols
    assert (b_cols if tb else b_rows) == Kp * len(pieces), (name, pieces[0].shape, b.shape)
    tm, tn, tk = _matmul_tiles(M, N, Kp, pieces[0].dtype.itemsize, b.dtype.itemsize, jnp.dtype(out_dtype).itemsize)
    npk = Kp // tk
    nk = npk * len(pieces)
    dims = (((0 if ta else 1,), (1 if tb else 0,)), ((), ()))
    direct = jnp.dtype(out_dtype) == jnp.dtype(F32)

    def body(*refs):
        a_refs, b_ref, o_ref, scratch = refs[: len(pieces)], refs[len(pieces)], refs[len(pieces) + 1], refs[len(pieces) + 2 :]
        k = pl.program_id(2)

        def product(a_ref):
            av = a_ref[...]
            if a_act is not None:
                av = a_act(av.astype(F32))
            return lax.dot_general(av.astype(MXU_DTYPE), b_ref[...].astype(MXU_DTYPE), dims, preferred_element_type=F32)

        if nk == 1:
            o_ref[...] = product(a_refs[0]).astype(o_ref.dtype)
            return
        acc = o_ref if direct else scratch[0]

        @pl.when(k == 0)
        def _():
            acc[...] = product(a_refs[0])

        for p, a_ref in enumerate(a_refs):
            lo, hi = max(p * npk, 1), (p + 1) * npk
            if lo < hi:

                @pl.when((k >= lo) & (k < hi))
                def _(a_ref=a_ref):
                    acc[...] += product(a_ref)

        if not direct:

            @pl.when(k == nk - 1)
            def _():
                o_ref[...] = acc[...].astype(o_ref.dtype)

    def a_spec(p):
        kk = lambda k: jnp.clip(k - p * npk, 0, npk - 1)
        return pl.BlockSpec((tk, tm), lambda i, j, k: (kk(k), i)) if ta else pl.BlockSpec((tm, tk), lambda i, j, k: (i, kk(k)))

    b_block, b_index = ((tn, tk), lambda i, j, k: (j, k)) if tb else ((tk, tn), lambda i, j, k: (k, j))
    if layer is None:
        b_spec = pl.BlockSpec(b_block, b_index)
    else:
        b_spec = pl.BlockSpec((None, *b_block), lambda i, j, k: (layer, *b_index(i, j, k)))
    out = pl.pallas_call(
        body,
        name=name,
        grid=(M // tm, N // tn, nk),
        in_specs=[a_spec(p) for p in range(len(pieces))] + [b_spec],
        out_specs=pl.BlockSpec((tm, tn), lambda i, j, k: (i, j)),
        out_shape=jax.ShapeDtypeStruct((M, N), out_dtype),
        scratch_shapes=[] if (direct or nk == 1) else [pltpu.VMEM((tm, tn), F32)],
        compiler_params=_cparams(3),
    )(*pieces, b)
    return out.reshape(*lead, N) if lead is not None else out


SUBLANES = 8


def _shift_down(u, d):
    if d == 0:
        return u
    r = pltpu.roll(u, d, 0)
    t = lax.broadcasted_iota(jnp.int32, (SUBLANES, u.shape[1]), 0)
    return jnp.concatenate([jnp.where(t >= d, r[:SUBLANES], 0.0), r[SUBLANES:]], axis=0)


def _shift_up(u, d):
    if d == 0:
        return u
    s = u.shape[0]
    r = pltpu.roll(u, s - d, 0)
    t = lax.broadcasted_iota(jnp.int32, (SUBLANES, u.shape[1]), 0)
    return jnp.concatenate([r[: s - SUBLANES], jnp.where(t < SUBLANES - d, r[s - SUBLANES :], 0.0)], axis=0)


def _conv(u, w, b):
    k = w.shape[0]
    out = b + w[k - 1 : k, :] * u
    for j in range(k - 1):
        out = out + w[j : j + 1, :] * _shift_down(u, k - 1 - j)
    return out


def _conv_bwd(u, w, dc):
    k = w.shape[0]
    du = w[k - 1 : k, :] * dc
    dws = []
    for j in range(k - 1):
        du = du + w[j : j + 1, :] * _shift_up(dc, k - 1 - j)
        dws.append(jnp.sum(dc * _shift_down(u, k - 1 - j), axis=0, keepdims=True))
    dws.append(jnp.sum(dc * u, axis=0, keepdims=True))
    return du, jnp.concatenate(dws, axis=0), jnp.sum(dc, axis=0, keepdims=True)


def conv_silu_fwd(proj, w, b):
    B, S, _ = proj.shape
    k, c = w.shape
    blk0 = OFF_XBC // LANE

    def body(u_ref, w_ref, b_ref, o_ref):
        o_ref[0] = _silu(_conv(u_ref[0], w_ref[...], b_ref[...]))

    return pl.pallas_call(
        body,
        name="conv_silu_fwd",
        grid=(B, c // LANE),
        in_specs=[
            pl.BlockSpec((1, S, LANE), lambda bi, j: (bi, 0, blk0 + j)),
            pl.BlockSpec((k, LANE), lambda bi, j: (0, j)),
            pl.BlockSpec((1, LANE), lambda bi, j: (0, j)),
        ],
        out_specs=pl.BlockSpec((1, S, LANE), lambda bi, j: (bi, 0, j)),
        out_shape=jax.ShapeDtypeStruct((B, S, c), F32),
        compiler_params=_cparams(2),
    )(proj, w, b)


def conv_silu_bwd(proj, w, b, douts):
    B, S, _ = proj.shape
    k, c = w.shape
    blk0 = OFF_XBC // LANE
    ends = np.cumsum([d.shape[-1] // LANE for d in douts]).tolist()
    starts = [0] + ends[:-1]

    def body(u_ref, w_ref, b_ref, *refs):
        d_refs, (du_ref, dw_ref, db_ref) = refs[: len(douts)], refs[len(douts) :]
        j, bi = pl.program_id(0), pl.program_id(1)
        u, wv = u_ref[0], w_ref[...]
        cv = _conv(u, wv, b_ref[...])
        sg = jax.nn.sigmoid(cv)
        dout = d_refs[-1][0]
        for r in reversed(range(len(douts) - 1)):
            dout = jnp.where(j < ends[r], d_refs[r][0], dout)
        dc = dout * (sg * (1.0 + cv * (1.0 - sg)))
        du, dw, db = _conv_bwd(u, wv, dc)
        du_ref[0] = du.astype(du_ref.dtype)

        @pl.when(bi == 0)
        def _():
            dw_ref[...] = dw
            db_ref[...] = db

        @pl.when(bi > 0)
        def _():
            dw_ref[...] += dw
            db_ref[...] += db

    return pl.pallas_call(
        body,
        name="conv_silu_bwd",
        grid=(c // LANE, B),
        in_specs=[
            pl.BlockSpec((1, S, LANE), lambda j, bi: (bi, 0, blk0 + j)),
            pl.BlockSpec((k, LANE), lambda j, bi: (0, j)),
            pl.BlockSpec((1, LANE), lambda j, bi: (0, j)),
        ]
        + [
            pl.BlockSpec((1, S, LANE), lambda j, bi, lo=lo, hi=hi: (bi, 0, jnp.clip(j, lo, hi - 1) - lo))
            for lo, hi in zip(starts, ends)
        ],
        out_specs=[
            pl.BlockSpec((1, S, LANE), lambda j, bi: (bi, 0, j)),
            pl.BlockSpec((k, LANE), lambda j, bi: (0, j)),
            pl.BlockSpec((1, LANE), lambda j, bi: (0, j)),
        ],
        out_shape=[
            jax.ShapeDtypeStruct((B, S, c), MXU_DTYPE),
            jax.ShapeDtypeStruct((k, c), F32),
            jax.ShapeDtypeStruct((1, c), F32),
        ],
        compiler_params=_cparams(2),
    )(proj, w, b, *douts)


def _glu_specs(S, k, nb, batch_col):
    specs = []
    for off in (0, nb):
        specs.append(pl.BlockSpec((1, S, LANE), lambda *ids, off=off: (batch_col(*ids)[0], 0, off + batch_col(*ids)[1])))
    for rows in (k, 1):
        for off in (0, nb):
            specs.append(pl.BlockSpec((rows, LANE), lambda *ids, off=off: (0, off + batch_col(*ids)[1])))
    return specs


def conv_glu_fwd(u, w, b):
    B, S, c2 = u.shape
    k = w.shape[0]
    nb = c2 // 2 // LANE

    def body(ug_ref, uv_ref, wg_ref, wv_ref, bg_ref, bv_ref, o_ref):
        gate = _conv(ug_ref[0].astype(F32), wg_ref[...], bg_ref[...])
        val = _conv(uv_ref[0].astype(F32), wv_ref[...], bv_ref[...])
        o_ref[0] = (_silu(gate) * val).astype(o_ref.dtype)

    return pl.pallas_call(
        body,
        name="conv_glu_fwd",
        grid=(B, nb),
        in_specs=_glu_specs(S, k, nb, lambda bi, j: (bi, j)),
        out_specs=pl.BlockSpec((1, S, LANE), lambda bi, j: (bi, 0, j)),
        out_shape=jax.ShapeDtypeStruct((B, S, c2 // 2), MXU_DTYPE),
        compiler_params=_cparams(2),
    )(u, u, w, w, b, b)


def conv_glu_bwd(u, w, b, da):
    B, S, c2 = u.shape
    k = w.shape[0]
    nb = c2 // 2 // LANE

    def body(ug_ref, uv_ref, wg_ref, wv_ref, bg_ref, bv_ref, d_ref, dug_ref, duv_ref, dwg_ref, dwv_ref, dbg_ref, dbv_ref):
        bi = pl.program_id(1)
        ug, uv, wg, wv = ug_ref[0].astype(F32), uv_ref[0].astype(F32), wg_ref[...], wv_ref[...]
        gate, val = _conv(ug, wg, bg_ref[...]), _conv(uv, wv, bv_ref[...])
        sg = jax.nn.sigmoid(gate)
        dav = d_ref[0].astype(F32)
        dug, dwg, dbg = _conv_bwd(ug, wg, dav * val * (sg * (1.0 + gate * (1.0 - sg))))
        duv, dwv, dbv = _conv_bwd(uv, wv, dav * gate * sg)
        dug_ref[0] = dug.astype(dug_ref.dtype)
        duv_ref[0] = duv.astype(duv_ref.dtype)

        @pl.when(bi == 0)
        def _():
            dwg_ref[...], dwv_ref[...], dbg_ref[...], dbv_ref[...] = dwg, dwv, dbg, dbv

        @pl.when(bi > 0)
        def _():
            dwg_ref[...] += dwg
            dwv_ref[...] += dwv
            dbg_ref[...] += dbg
            dbv_ref[...] += dbv

    tok = pl.BlockSpec((1, S, LANE), lambda j, bi: (bi, 0, j))
    outs = pl.pallas_call(
        body,
        name="conv_glu_bwd",
        grid=(nb, B),
        in_specs=_glu_specs(S, k, nb, lambda j, bi: (bi, j)) + [tok],
        out_specs=[tok, tok] + [pl.BlockSpec((rows, LANE), lambda j, bi: (0, j)) for rows in (k, k, 1, 1)],
        out_shape=[jax.ShapeDtypeStruct((B, S, c2 // 2), MXU_DTYPE)] * 2
        + [jax.ShapeDtypeStruct((rows, c2 // 2), F32) for rows in (k, k, 1, 1)],
        compiler_params=_cparams(2),
    )(u, u, w, w, b, b, da)
    dug, duv, dwg, dwv, dbg, dbv = outs
    return dug, duv, jnp.concatenate([dwg, dwv], axis=1), jnp.concatenate([dbg, dbv], axis=1)


N_PAIR = SSD_HEADS // SSD_GROUPS // 2


def _ssd_chunk(prev, xs, bms, cms, dtraw, dt_bias, a_log, d_skip):
    L = CHUNK
    lane = lax.broadcasted_iota(jnp.int32, (1, LANE), 1)
    sub = lax.broadcasted_iota(jnp.int32, (LANE, 1), 0)
    row = lax.broadcasted_iota(jnp.int32, (L, L), 0)
    col = lax.broadcasted_iota(jnp.int32, (L, L), 1)
    tri = (row >= col).astype(F32)
    first = lane < SSD_HEAD_DIM

    dt = _softplus(dtraw + dt_bias)
    a = dt * (-jnp.exp(a_log))
    acs = jnp.dot(tri, a, precision=HIGHEST, preferred_element_type=F32)
    acs_t = acs.T
    a_end = jnp.sum(a, axis=0, keepdims=True)

    def lane_of(v, h):
        return jnp.sum(jnp.where(lane == h, v, 0.0), axis=1, keepdims=True)

    def expand(v, ha):
        return jnp.where(first, lane_of(v, ha), lane_of(v, ha + 1))

    ys, news = [], []
    for g in range(SSD_GROUPS):
        bm, cm = bms[g].astype(MXU_DTYPE), cms[g].astype(MXU_DTYPE)
        cb = lax.dot_general(cm, bm, (((1,), (1,)), ((), ())), preferred_element_type=F32)
        for j in range(N_PAIR):
            ha = g * (SSD_HEADS // SSD_GROUPS) + 2 * j
            x, state = xs[g * N_PAIR + j], prev[g * N_PAIR + j]
            dt_e, acs_e, end_e = expand(dt, ha), expand(acs, ha), expand(a_end, ha)
            xdt = x * dt_e
            y = jnp.dot(cm, state.astype(MXU_DTYPE), preferred_element_type=F32) * jnp.exp(acs_e)
            st = lax.dot_general(bm, (xdt * jnp.exp(end_e - acs_e)).astype(MXU_DTYPE), (((0,), (0,)), ((), ())), preferred_element_type=F32)
            news.append(state * jnp.exp(end_e) + st)
            for hh in range(2):
                h = ha + hh
                seg = lane_of(acs, h) - jnp.sum(jnp.where(sub == h, acs_t, 0.0), axis=0, keepdims=True)
                decay = jnp.exp(jnp.where(row >= col, seg, -jnp.inf))
                xh = jnp.where(first if hh == 0 else jnp.logical_not(first), xdt, 0.0)
                y = y + jnp.dot((cb * decay).astype(MXU_DTYPE), xh.astype(MXU_DTYPE), preferred_element_type=F32)
            ys.append(y + x * expand(d_skip, ha))
    return ys, news


N_PAIRS = SSD_GROUPS * N_PAIR


def _ssd_specs(chunk_of):
    bc = SSD_GROUPS * SSD_STATE
    par = pl.BlockSpec((1, LANE), lambda *ids: (0, 0))
    return [
        pl.BlockSpec((1, CHUNK, D_SSD), lambda *ids: (*chunk_of(*ids), 0)),
        pl.BlockSpec((1, CHUNK, bc), lambda *ids: (*chunk_of(*ids), D_SSD // bc)),
        pl.BlockSpec((1, CHUNK, bc), lambda *ids: (*chunk_of(*ids), D_SSD // bc + 1)),
        pl.BlockSpec((1, CHUNK, LANE), lambda *ids: (*chunk_of(*ids), OFF_DT // LANE)),
        par,
        par,
        par,
    ]


def _pair_cols(ref):
    return [ref[0, :, k * LANE : (k + 1) * LANE] for k in range(N_PAIRS)]


def _group_cols(ref):
    return [ref[0, :, g * SSD_STATE : (g + 1) * SSD_STATE] for g in range(SSD_GROUPS)]


def ssd_fwd(xbc_act, proj, dt_bias, a_log, d_skip):
    B, S, _ = xbc_act.shape
    nc = S // CHUNK

    def body(xs_ref, bm_ref, cm_ref, dt_ref, db_ref, al_ref, ds_ref, y_ref, st_ref, state):
        @pl.when(pl.program_id(1) == 0)
        def _():
            state[...] = jnp.zeros_like(state)

        prev = [state[k] for k in range(N_PAIRS)]
        for k in range(N_PAIRS):
            st_ref[0, 0, k] = prev[k]
        ys, news = _ssd_chunk(prev, _pair_cols(xs_ref), _group_cols(bm_ref), _group_cols(cm_ref), dt_ref[0], db_ref[...], al_ref[...], ds_ref[...])
        for k in range(N_PAIRS):
            y_ref[0, :, k * LANE : (k + 1) * LANE] = ys[k]
            state[k] = news[k]

    return pl.pallas_call(
        body,
        name="ssd_fwd",
        grid=(B, nc),
        in_specs=_ssd_specs(lambda b, c: (b, c)),
        out_specs=[
            pl.BlockSpec((1, CHUNK, D_SSD), lambda b, c: (b, c, 0)),
            pl.BlockSpec((1, 1, N_PAIRS, SSD_STATE, LANE), lambda b, c: (b, c, 0, 0, 0)),
        ],
        out_shape=[
            jax.ShapeDtypeStruct((B, S, D_SSD), F32),
            jax.ShapeDtypeStruct((B, nc, N_PAIRS, SSD_STATE, LANE), F32),
        ],
        scratch_shapes=[pltpu.VMEM((N_PAIRS, SSD_STATE, LANE), F32)],
        compiler_params=_cparams(2),
    )(xbc_act, xbc_act, xbc_act, proj, dt_bias, a_log, d_skip)


def ssd_bwd(xbc_act, proj, dt_bias, a_log, d_skip, states, dy):
    B, S, _ = xbc_act.shape
    nc = S // CHUNK
    bc = SSD_GROUPS * SSD_STATE
    chunk_of = lambda b, cr: (b, nc - 1 - cr)

    def body(xs_ref, bm_ref, cm_ref, dt_ref, db_ref, al_ref, ds_ref, st_ref, dy_ref,
             dxs_ref, dbm_ref, dcm_ref, ddt_ref, ddb_ref, dal_ref, dds_ref, dstate):
        b, cr = pl.program_id(0), pl.program_id(1)

        @pl.when(cr == 0)
        def _():
            dstate[...] = jnp.zeros_like(dstate)

        prev = [st_ref[0, 0, k] for k in range(N_PAIRS)]
        _, vjp = jax.vjp(
            _ssd_chunk, prev, _pair_cols(xs_ref), _group_cols(bm_ref), _group_cols(cm_ref), dt_ref[0], db_ref[...], al_ref[...], ds_ref[...]
        )
        dprev, dxs, dbm, dcm, ddt, ddb, dal, dds = vjp((_pair_cols(dy_ref), [dstate[k] for k in range(N_PAIRS)]))
        for k in range(N_PAIRS):
            dstate[k] = dprev[k]
            dxs_ref[0, :, k * LANE : (k + 1) * LANE] = dxs[k]
        for g in range(SSD_GROUPS):
            dbm_ref[0, :, g * SSD_STATE : (g + 1) * SSD_STATE] = dbm[g]
            dcm_ref[0, :, g * SSD_STATE : (g + 1) * SSD_STATE] = dcm[g]
        ddt_ref[0] = ddt
        first = (b == 0) & (cr == 0)

        @pl.when(first)
        def _():
            ddb_ref[...] = ddb
            dal_ref[...] = dal
            dds_ref[...] = dds

        @pl.when(jnp.logical_not(first))
        def _():
            ddb_ref[...] += ddb
            dal_ref[...] += dal
            dds_ref[...] += dds

    par = pl.BlockSpec((1, LANE), lambda *ids: (0, 0))
    in_specs = _ssd_specs(chunk_of) + [
        pl.BlockSpec((1, 1, N_PAIRS, SSD_STATE, LANE), lambda b, cr: (*chunk_of(b, cr), 0, 0, 0)),
        pl.BlockSpec((1, CHUNK, D_SSD), lambda b, cr: (*chunk_of(b, cr), 0)),
    ]
    return pl.pallas_call(
        body,
        name="ssd_bwd",
        grid=(B, nc),
        in_specs=in_specs,
        out_specs=[
            pl.BlockSpec((1, CHUNK, D_SSD), lambda b, cr: (*chunk_of(b, cr), 0)),
            pl.BlockSpec((1, CHUNK, bc), lambda b, cr: (*chunk_of(b, cr), 0)),
            pl.BlockSpec((1, CHUNK, bc), lambda b, cr: (*chunk_of(b, cr), 0)),
            pl.BlockSpec((1, CHUNK, LANE), lambda b, cr: (*chunk_of(b, cr), 0)),
            par,
            par,
            par,
        ],
        out_shape=[
            jax.ShapeDtypeStruct((B, S, D_SSD), F32),
            jax.ShapeDtypeStruct((B, S, bc), F32),
            jax.ShapeDtypeStruct((B, S, bc), F32),
            jax.ShapeDtypeStruct((B, S, LANE), F32),
            jax.ShapeDtypeStruct((1, LANE), F32),
            jax.ShapeDtypeStruct((1, LANE), F32),
            jax.ShapeDtypeStruct((1, LANE), F32),
        ],
        scratch_shapes=[pltpu.VMEM((N_PAIRS, SSD_STATE, LANE), F32)],
        compiler_params=_cparams(2),
    )(xbc_act, xbc_act, xbc_act, proj, dt_bias, a_log, d_skip, states, dy)


ATT_SCALE = (QK_NOPE + QK_ROPE) ** -0.5
ATT_TILE = 512


ATT_HEADS_FWD = 4
ATT_HEADS_BWD = 2
LOG2E = 1.4426950408889634
Q_PRESCALE = ATT_SCALE * LOG2E


def fn_rope_q(q, cos_t, sin_t, rot):
    parts = []
    for h in range(MLA_HEADS):
        qr = q[:, h * Q_HEAD_PAD + LANE : (h + 1) * Q_HEAD_PAD]
        qr = qr * cos_t + jnp.dot(qr, rot, precision=HIGHEST, preferred_element_type=F32) * sin_t
        parts += [q[:, h * Q_HEAD_PAD : h * Q_HEAD_PAD + LANE] * Q_PRESCALE, qr * Q_PRESCALE]
    return (jnp.concatenate(parts, axis=1),)


def _head_cols(ref, hh, width):
    return ref[0, :, hh * width : (hh + 1) * width]


def _causal_mask(s):
    return jnp.where(lax.broadcasted_iota(jnp.int32, s.shape, 0) >= lax.broadcasted_iota(jnp.int32, s.shape, 1), s, -jnp.inf)


def _nt(a, b):
    return lax.dot_general(a, b, (((1,), (1,)), ((), ())), preferred_element_type=F32)


def _tn(a, b):
    return lax.dot_general(a, b, (((0,), (0,)), ((), ())), preferred_element_type=F32)


def _per_tile(idx, n, variant):
    for v in range(n):

        @pl.when(idx == v)
        def _(v=v):
            variant(v)


def attention_fwd(qs, kv, krr):
    B, S, _ = qs.shape
    t = min(ATT_TILE, S)
    n = S // t
    hps = ATT_HEADS_FWD

    def body(q_ref, k_ref, kr_ref, o_ref, lse_ref):
        def variant(v):
            spans = ([(0, v * t, False)] if v else []) + [(v * t, (v + 1) * t, True)]
            for hh in range(hps):
                q = _head_cols(q_ref, hh, Q_HEAD_PAD)
                c0 = hh * 2 * LANE
                ss = []
                for a, b, diag in spans:
                    s = _nt(q, jnp.concatenate([k_ref[0, a:b, c0 : c0 + LANE], kr_ref[0, a:b]], axis=1))
                    ss.append(_causal_mask(s) if diag else s)
                m = functools.reduce(jnp.maximum, [jnp.max(s, axis=1, keepdims=True) for s in ss])
                ps = [jnp.exp2(s - m) for s in ss]
                l = functools.reduce(jnp.add, [jnp.sum(p, axis=1, keepdims=True) for p in ps])
                acc = functools.reduce(
                    jnp.add,
                    [jnp.dot(p.astype(MXU_DTYPE), k_ref[0, a:b, c0 + LANE : c0 + 2 * LANE], preferred_element_type=F32) for p, (a, b, _) in zip(ps, spans)],
                )
                o_ref[0, :, hh * V_DIM : (hh + 1) * V_DIM] = acc / l
                lse_ref[0, hh] = m + jnp.log2(l)

        _per_tile(pl.program_id(2), n, variant)

    return pl.pallas_call(
        body,
        name="attention_fwd",
        grid=(B, MLA_HEADS // hps, n),
        in_specs=[
            pl.BlockSpec((1, t, hps * Q_HEAD_PAD), lambda b, h, i: (b, i, h)),
            pl.BlockSpec((1, S, hps * 2 * LANE), lambda b, h, i: (b, 0, h)),
            pl.BlockSpec((1, S, LANE), lambda b, h, i: (b, 0, 0)),
        ],
        out_specs=[
            pl.BlockSpec((1, t, hps * V_DIM), lambda b, h, i: (b, i, h)),
            pl.BlockSpec((1, hps, t, 1), lambda b, h, i: (b, h, i, 0)),
        ],
        out_shape=[jax.ShapeDtypeStruct((B, S, D_ATT), F32), jax.ShapeDtypeStruct((B, MLA_HEADS, S, 1), F32)],
        compiler_params=_cparams(3),
    )(qs, kv, krr)


def attention_bwd(qs, kv, krr, cos_t, sin_t, rot_t, o, lse, do):
    B, S, _ = qs.shape
    t = min(ATT_TILE, S)
    n = S // t
    hps = ATT_HEADS_BWD

    def body(q_ref, c_ref, s_ref, rott_ref, k_ref, kr_ref, o_ref, lse_ref, do_ref, dq_ref, dkv_ref, dkr_ref, dk_acc, dv_acc):
        i = pl.program_id(2)

        @pl.when(i == 0)
        def _():
            dk_acc[...] = jnp.zeros_like(dk_acc)
            dv_acc[...] = jnp.zeros_like(dv_acc)

        def variant(v):
            spans = ([(0, v * t, False)] if v else []) + [(v * t, (v + 1) * t, True)]
            for hh in range(hps):
                q = _head_cols(q_ref, hh, Q_HEAD_PAD)
                dov = _head_cols(do_ref, hh, V_DIM)
                delta = jnp.sum(_head_cols(o_ref, hh, V_DIM) * dov, axis=1, keepdims=True)
                dob = dov.astype(MXU_DTYPE)
                c0 = hh * 2 * LANE
                acc = None
                for a, b, diag in spans:
                    k = jnp.concatenate([k_ref[0, a:b, c0 : c0 + LANE], kr_ref[0, a:b]], axis=1)
                    s = _nt(q, k)
                    p = jnp.exp2((_causal_mask(s) if diag else s) - lse_ref[0, hh])
                    ds = (p * (_nt(dob, k_ref[0, a:b, c0 + LANE : c0 + 2 * LANE]) - delta) * ATT_SCALE).astype(MXU_DTYPE)
                    part = jnp.dot(ds, k, preferred_element_type=F32)
                    acc = part if acc is None else acc + part
                    dk_acc[hh, a:b] += _tn(ds, q)
                    dv_acc[hh, a:b] += _tn(p.astype(MXU_DTYPE), dob)
                r = acc[:, LANE:]
                dr = r * c_ref[0] + jnp.dot(r * s_ref[0], rott_ref[...], precision=HIGHEST, preferred_element_type=F32)
                dq_ref[0, :, hh * Q_HEAD_PAD : (hh + 1) * Q_HEAD_PAD] = jnp.concatenate([acc[:, :LANE], dr], axis=1).astype(dq_ref.dtype)

        _per_tile(i, n, variant)

        @pl.when(i == n - 1)
        def _():
            dkr = None
            for hh in range(hps):
                dk = dk_acc[hh] * (1.0 / Q_PRESCALE)
                dkv_ref[0, :, hh * 2 * LANE : (hh + 1) * 2 * LANE] = jnp.concatenate([dk[:, :LANE], dv_acc[hh]], axis=1).astype(dkv_ref.dtype)
                dkr = dk[:, LANE:] if dkr is None else dkr + dk[:, LANE:]
            dkr_ref[0, 0] = dkr

    return pl.pallas_call(
        body,
        name="attention_bwd",
        grid=(B, MLA_HEADS // hps, n),
        in_specs=[
            pl.BlockSpec((1, t, hps * Q_HEAD_PAD), lambda b, h, i: (b, i, h)),
            pl.BlockSpec((1, t, LANE), lambda b, h, i: (b, i, 0)),
            pl.BlockSpec((1, t, LANE), lambda b, h, i: (b, i, 0)),
            pl.BlockSpec((LANE, LANE), lambda b, h, i: (0, 0)),
            pl.BlockSpec((1, S, hps * 2 * LANE), lambda b, h, i: (b, 0, h)),
            pl.BlockSpec((1, S, LANE), lambda b, h, i: (b, 0, 0)),
            pl.BlockSpec((1, t, hps * V_DIM), lambda b, h, i: (b, i, h)),
            pl.BlockSpec((1, hps, t, 1), lambda b, h, i: (b, h, i, 0)),
            pl.BlockSpec((1, t, hps * V_DIM), lambda b, h, i: (b, i, h)),
        ],
        out_specs=[
            pl.BlockSpec((1, t, hps * Q_HEAD_PAD), lambda b, h, i: (b, i, h)),
            pl.BlockSpec((1, S, hps * 2 * LANE), lambda b, h, i: (b, 0, h)),
            pl.BlockSpec((1, 1, S, LANE), lambda b, h, i: (b, h, 0, 0)),
        ],
        out_shape=[
            jax.ShapeDtypeStruct(qs.shape, MXU_DTYPE),
            jax.ShapeDtypeStruct(kv.shape, MXU_DTYPE),
            jax.ShapeDtypeStruct((B, MLA_HEADS // hps, S, LANE), F32),
        ],
        scratch_shapes=[pltpu.VMEM((hps, S, Q_HEAD_PAD), F32), pltpu.VMEM((hps, S, V_DIM), F32)],
        compiler_params=_cparams(3),
    )(qs, cos_t, sin_t, rot_t, kv, krr, o, lse, do)


_IN_SPLITS = np.cumsum([D_SSD, D_XBC, SSD_HEADS, Q_RANK, KV_RANK]).tolist()


def _pad_last(t, n):
    return jnp.pad(t, [(0, 0)] * (t.ndim - 1) + [(0, n - t.shape[-1])])


def win_to_kernel(w):
    z, xbc, dt, cq, ckv, kr = jnp.split(w, _IN_SPLITS, axis=-1)
    return jnp.concatenate([z, xbc, ckv, _pad_last(kr, LANE), _pad_last(dt, LANE), cq], axis=-1)


def win_from_kernel(g):
    z, xbc = g[..., :OFF_XBC], g[..., OFF_XBC:OFF_CKV]
    ckv, kr = g[..., OFF_CKV:OFF_KR], g[..., OFF_KR : OFF_KR + QK_ROPE]
    dt, cq = g[..., OFF_DT : OFF_DT + SSD_HEADS], g[..., OFF_CQ:]
    return jnp.concatenate([z, xbc, dt, cq, ckv, kr], axis=-1)


def wuq_to_kernel(w):
    w = w.reshape(*w.shape[:-1], MLA_HEADS, QK_NOPE + QK_ROPE)
    return _pad_last(w, Q_HEAD_PAD).reshape(*w.shape[:-2], MLA_HEADS * Q_HEAD_PAD)


def wuq_from_kernel(g):
    g = g.reshape(*g.shape[:-1], MLA_HEADS, Q_HEAD_PAD)[..., : QK_NOPE + QK_ROPE]
    return g.reshape(*g.shape[:-2], MLA_HEADS * (QK_NOPE + QK_ROPE))


def _lane_pad_row(v):
    return _pad_last(v, LANE)[None, :]


def rope_tables(positions):
    inv_freq = jnp.asarray(1.0 / (ROPE_BASE ** (np.arange(0, QK_ROPE, 2, dtype=np.float32) / QK_ROPE)))
    ang = positions.astype(F32)[..., None] * inv_freq
    cos, sin = jnp.cos(ang), jnp.sin(ang)
    zeros = jnp.zeros(cos.shape[:-1] + (LANE - QK_ROPE,), F32)
    rot = np.zeros((LANE, LANE), np.float32)
    half = QK_ROPE // 2
    for j in range(half):
        rot[j + half, j] = -1.0
        rot[j, j + half] = 1.0
    return jnp.concatenate([cos, cos, zeros], -1), jnp.concatenate([sin, sin, zeros], -1), jnp.asarray(rot), jnp.asarray(rot.T)


BIG = ("w_in", "w_uq", "w_ukv", "w_out", "w_up", "w_down")


def local_step(x, target, positions, mod_raw, W):
    B, S, D = x.shape
    cos_t, sin_t, rot, rot_t = rope_tables(positions)
    row = lambda v: v.reshape(1, -1)
    w_in_k, w_uq_k = win_to_kernel(W["w_in"]), wuq_to_kernel(W["w_uq"])
    Ls = []
    for l in range(DEPTH):
        mods = [mod_raw[l, :, k][:, None, :] for k in range(6)]
        bias = [row(W["b_ada"][l, k * D : (k + 1) * D]) for k in range(6)]
        Ls.append(
            dict(
                mods=mods,
                bias=bias,
                w_in=Layer(w_in_k, l),
                w_uq=Layer(w_uq_k, l),
                w_ukv=Layer(W["w_ukv"], l),
                w_out=Layer(W["w_out"], l),
                w_up=Layer(W["w_up"], l),
                w_down=Layer(W["w_down"], l),
                conv_w=W["conv_w"][l],
                conv_b=row(W["conv_b"][l]),
                conv_ff_w=W["conv_ff_w"][l],
                conv_ff_b=row(W["conv_ff_b"][l]),
                dt_bias=_lane_pad_row(W["dt_bias"][l]),
                a_log=_lane_pad_row(W["a_log"][l]),
                d_skip=_lane_pad_row(W["d_skip"][l]),
                norm_mix=row(W["norm_mix"][l]),
                ssd_norm=row(W["ssd_norm"][l]),
                q_norm=row(W["q_norm"][l]),
                kv_norm=row(W["kv_norm"][l]),
                attn_norm=row(W["attn_norm"][l]),
                norm_mlp=row(W["norm_mlp"][l]),
            )
        )
    fnorm = row(W["final_norm"])

    p0 = Ls[0]
    (h1,) = rowwise_fwd(
        "prenorm_fwd", fn_prenorm, [x], [p0["mods"][1], p0["mods"][0]], [p0["norm_mix"], p0["bias"][1], p0["bias"][0]], [(D, MXU_DTYPE)]
    )
    xin = x
    for l, p in enumerate(Ls):
        s = p["saved"] = dict(xin=xin, h1=h1)
        s["proj"] = proj = matmul(f"w_in_fwd{l}", h1, p["w_in"])
        s["xbc_act"] = xbc_act = conv_silu_fwd(proj, p["conv_w"], p["conv_b"])
        s["yscan"], s["states"] = ssd_fwd(xbc_act, proj, p["dt_bias"], p["a_log"], p["d_skip"])
        mla_in = [Col(proj, Q_RANK, OFF_CQ // Q_RANK), Col(proj, KV_RANK, OFF_CKV // KV_RANK), Col(proj, LANE, OFF_KR // LANE), cos_t, sin_t]
        s["cqn"], s["ckvn"], s["krr"] = rowwise_fwd(
            f"mla_prep_fwd{l}", fn_mla_prep, mla_in, [], [p["q_norm"], p["kv_norm"], rot], [(Q_RANK, MXU_DTYPE), (KV_RANK, MXU_DTYPE), (LANE, MXU_DTYPE)]
        )
        q = matmul(f"w_uq_fwd{l}", s["cqn"], p["w_uq"], out_dtype=MXU_DTYPE)
        (s["qs"],) = rowwise_fwd(f"rope_q_fwd{l}", fn_rope_q, [q, cos_t, sin_t], [], [rot], [(q.shape[-1], MXU_DTYPE)])
        s["kv"] = matmul(f"w_ukv_fwd{l}", s["ckvn"], p["w_ukv"], out_dtype=MXU_DTYPE)
        s["o"], s["lse"] = attention_fwd(s["qs"], s["kv"], s["krr"])
        (s["ycat"],) = rowwise_fwd(
            f"mix_fwd{l}", fn_mix, [s["yscan"], Col(proj, D_SSD, 0), s["o"]], [], [p["ssd_norm"], p["attn_norm"]], [(D_SSD + D_ATT, MXU_DTYPE)]
        )
        s["ymix"] = matmul(f"w_out_fwd{l}", s["ycat"], p["w_out"])
        s["x1"], s["h2"] = rowwise_fwd(
            f"mid_fwd{l}",
            fn_resid_prenorm,
            [xin, s["ymix"]],
            [p["mods"][2], p["mods"][4], p["mods"][3]],
            [p["norm_mlp"], p["bias"][2], p["bias"][4], p["bias"][3]],
            [(D, F32), (D, MXU_DTYPE)],
        )
        s["u"] = matmul(f"w_up_fwd{l}", s["h2"], p["w_up"], out_dtype=MXU_DTYPE)
        s["a"] = conv_glu_fwd(s["u"], p["conv_ff_w"], p["conv_ff_b"])
        s["ff"] = matmul(f"w_down_fwd{l}", s["a"], p["w_down"])
        if l + 1 < DEPTH:
            n = Ls[l + 1]
            xin, h1 = rowwise_fwd(
                f"join_fwd{l}",
                fn_resid_prenorm,
                [s["x1"], s["ff"]],
                [p["mods"][5], n["mods"][1], n["mods"][0]],
                [n["norm_mix"], p["bias"][5], n["bias"][1], n["bias"][0]],
                [(D, F32), (D, MXU_DTYPE)],
            )

    G = {k: [None] * DEPTH for k in W if k not in ("final_norm", "w_ada")}
    dmod = [[None] * 6 for _ in range(DEPTH)]
    dbias = [[None] * 6 for _ in range(DEPTH)]
    last = Ls[-1]
    sl = last["saved"]
    loss, dx1, dff, dmod[-1][5], G["final_norm"], dbias[-1][5] = final_fwdbwd(
        sl["x1"], sl["ff"], target, last["mods"][5], fnorm, last["bias"][5]
    )
    grad_x = None
    for l in reversed(range(DEPTH)):
        p = Ls[l]
        s = p["saved"]
        da = matmul(f"w_down_dgrad{l}", dff, p["w_down"], tb=True)
        G["w_down"][l] = matmul(f"w_down_wgrad{l}", s["a"], dff, ta=True, out_dtype=WIRE_DTYPE)
        du_gate, du_val, G["conv_ff_w"][l], dcb = conv_glu_bwd(s["u"], p["conv_ff_w"], p["conv_ff_b"], da)
        G["conv_ff_b"][l] = dcb[0]
        dh2 = matmul(f"w_up_dgrad{l}", [du_gate, du_val], p["w_up"], tb=True)
        G["w_up"][l] = jnp.concatenate(
            [matmul(f"w_up_wgrad{l}_{half}", s["h2"], d, ta=True, out_dtype=WIRE_DTYPE) for half, d in (("gate", du_gate), ("val", du_val))],
            axis=1,
        )
        dxb, dymix, dmod[l][2], dmod[l][4], dmod[l][3], G["norm_mlp"][l], dbias[l][2], dbias[l][4], dbias[l][3] = rowwise_bwd(
            f"mid_bwd{l}",
            fn_resid_prenorm,
            [s["xin"], s["ymix"]],
            [p["mods"][2], p["mods"][4], p["mods"][3]],
            [p["norm_mlp"], p["bias"][2], p["bias"][4], p["bias"][3]],
            [dx1, dh2],
            [True, True],
            [True] * 4,
            mxu_only=(1,),
        )
        dycat = matmul(f"w_out_dgrad{l}", dymix, p["w_out"], tb=True)
        G["w_out"][l] = matmul(f"w_out_wgrad{l}", s["ycat"], dymix, ta=True, out_dtype=WIRE_DTYPE)
        dyscan, dz, do, G["ssd_norm"][l], G["attn_norm"][l] = rowwise_bwd(
            f"mix_bwd{l}", fn_mix, [s["yscan"], Col(s["proj"], D_SSD, 0), s["o"]], [], [p["ssd_norm"], p["attn_norm"]], [dycat], [True] * 3, [True] * 2, mxu_only=(1,)
        )
        dq, dkv, dkrr = attention_bwd(s["qs"], s["kv"], s["krr"], cos_t, sin_t, rot_t, s["o"], s["lse"], do)
        dcqn = matmul(f"w_uq_dgrad{l}", dq, p["w_uq"], tb=True)
        G["w_uq"][l] = wuq_from_kernel(matmul(f"w_uq_wgrad{l}", s["cqn"], dq, ta=True, out_dtype=WIRE_DTYPE))
        dckvn = matmul(f"w_ukv_dgrad{l}", dkv, p["w_ukv"], tb=True)
        G["w_ukv"][l] = matmul(f"w_ukv_wgrad{l}", s["ckvn"], dkv, ta=True, out_dtype=WIRE_DTYPE)
        proj = s["proj"]
        mla_in = [Col(proj, Q_RANK, OFF_CQ // Q_RANK), Col(proj, KV_RANK, OFF_CKV // KV_RANK), Col(proj, LANE, OFF_KR // LANE), cos_t, sin_t]
        dcq, dckv, dkr, G["q_norm"][l], G["kv_norm"][l] = rowwise_bwd(
            f"mla_prep_bwd{l}",
            fn_mla_prep,
            mla_in,
            [],
            [p["q_norm"], p["kv_norm"], rot],
            [dcqn, dckvn, dkrr],
            [True, True, True, False, False],
            [True, True, False],
            mxu_only=(0, 1, 2),
        )
        dxs, dbm, dcm, ddt, ddb, dal, dds = ssd_bwd(s["xbc_act"], proj, p["dt_bias"], p["a_log"], p["d_skip"], s["states"], dyscan)
        G["dt_bias"][l], G["a_log"][l], G["d_skip"][l] = ddb[0, :SSD_HEADS], dal[0, :SSD_HEADS], dds[0, :SSD_HEADS]
        dxbc, G["conv_w"][l], dcb = conv_silu_bwd(proj, p["conv_w"], p["conv_b"], [dxs, dbm, dcm])
        G["conv_b"][l] = dcb[0]
        dproj = jnp.concatenate([dz, dxbc, dckv, dkr, ddt.astype(MXU_DTYPE), dcq], axis=-1)
        dh1 = matmul(f"w_in_dgrad{l}", dproj, p["w_in"], tb=True)
        G["w_in"][l] = win_from_kernel(matmul(f"w_in_wgrad{l}", s["h1"], dproj, ta=True, out_dtype=WIRE_DTYPE))
        if l > 0:
            q = Ls[l - 1]
            sq = q["saved"]
            dx1, dff, dmod[l - 1][5], dmod[l][1], dmod[l][0], G["norm_mix"][l], dbias[l - 1][5], dbias[l][1], dbias[l][0] = rowwise_bwd(
                f"join_bwd{l - 1}",
                fn_resid_prenorm,
                [sq["x1"], sq["ff"]],
                [q["mods"][5], p["mods"][1], p["mods"][0]],
                [p["norm_mix"], q["bias"][5], p["bias"][1], p["bias"][0]],
                [dxb, dh1],
                [True, True],
                [True] * 4,
                mxu_only=(1,),
            )
        else:
            grad_x, dmod[0][1], dmod[0][0], G["norm_mix"][0], dbias[0][1], dbias[0][0] = rowwise_bwd(
                "prenorm_bwd",
                fn_prenorm,
                [x],
                [p["mods"][1], p["mods"][0]],
                [p["norm_mix"], p["bias"][1], p["bias"][0]],
                [dh1],
                [True],
                [True] * 3,
                adds={0: dxb},
            )
    for l in range(DEPTH):
        G["b_ada"][l] = jnp.concatenate([d[0] for d in dbias[l]])
        for k in ("norm_mix", "ssd_norm", "q_norm", "kv_norm", "attn_norm", "norm_mlp"):
            G[k][l] = G[k][l][0]
    grads = {k: (v if k in BIG else jnp.stack(v) if isinstance(v, list) else v[0]) for k, v in G.items()}
    dmod_raw = jnp.stack([jnp.stack([d[:, 0, :] for d in dmod[l]], axis=1) for l in range(DEPTH)])
    return loss, grad_x, grads, dmod_raw


MESH = pl.DeviceIdType.MESH
ANY = pl.BlockSpec(memory_space=pl.ANY)
PACK_W = 1024
PACK_TILE = 768


def _place():
    x, y, c = lax.axis_index("x"), lax.axis_index("y"), lax.axis_index("c")
    chips = [(1 - x, y), (x, 1 - y), (1 - x, 1 - y)]
    return x, y, c, chips


def _remote(src, dst, send_sem, recv_sem, to):
    return pltpu.make_async_remote_copy(src_ref=src, dst_ref=dst, send_sem=send_sem, recv_sem=recv_sem, device_id=to, device_id_type=MESH)


def all_gather8(name, v):
    m_per, n = v.shape

    def body(x_ref, out_ref, send_sems, recv_sems, local_sem):
        x, y, c, chips = _place()
        me, sibling = (x, y, c), (x, y, 1 - c)

        def rows(px, py, pc):
            return out_ref.at[pl.ds((4 * px + 2 * py + pc) * m_per, m_per), :]

        def copy(k, block, to, src=None):
            return _remote(rows(*block) if src is None else src, rows(*block), send_sems.at[k], recv_sems.at[k], to)

        mine = pltpu.make_async_copy(x_ref, rows(*me), local_sem)
        mine.start()
        first = [copy(0, me, sibling, src=x_ref)]
        first += [copy(1 + j, me, (*chip, c), src=x_ref) for j, chip in enumerate(chips)]
        for cp in first:
            cp.start()
        passed = [copy(4 + j, (*chip, c), sibling) for j, chip in enumerate(chips)]
        for j, chip in enumerate(chips):
            copy(1 + j, (*chip, c), me).wait_recv()
            passed[j].start()
        copy(0, sibling, me).wait_recv()
        for j, chip in enumerate(chips):
            copy(4 + j, (*chip, 1 - c), me).wait_recv()
        for cp in first + passed:
            cp.wait_send()
        mine.wait()

    return pl.pallas_call(
        body,
        name=name,
        out_shape=jax.ShapeDtypeStruct((N_DEV * m_per, n), v.dtype),
        in_specs=[pl.BlockSpec(memory_space=pltpu.VMEM)],
        out_specs=pl.BlockSpec(memory_space=pltpu.VMEM),
        scratch_shapes=[pltpu.SemaphoreType.DMA((7,)), pltpu.SemaphoreType.DMA((7,)), pltpu.SemaphoreType.DMA],
        compiler_params=pltpu.CompilerParams(vmem_limit_bytes=VMEM_LIMIT),
    )(v)


def gather_weights(pack):
    R, n = pack.shape
    rh = R // 2

    def body(x_ref, out_ref, send_sems, recv_sems):
        x, y, c, chips = _place()
        me = 2 * x + y

        def half(chip, hc):
            return out_ref.at[chip, pl.ds(hc * rh, rh), :]

        src = x_ref.at[pl.ds(c * rh, rh), :]
        first = [_remote(src, half(me, c), send_sems.at[j], recv_sems.at[j], (px, py, c)) for j, (px, py) in enumerate(chips)]
        for cp in first:
            cp.start()
        passed = []
        for j, (px, py) in enumerate(chips):
            got = half(2 * px + py, c)
            _remote(got, got, send_sems.at[j], recv_sems.at[j], (px, py, c)).wait_recv()
            cp = _remote(got, got, send_sems.at[3 + j], recv_sems.at[3 + j], (x, y, 1 - c))
            cp.start()
            passed.append(cp)
        for j, (px, py) in enumerate(chips):
            got = half(2 * px + py, 1 - c)
            _remote(got, got, send_sems.at[3 + j], recv_sems.at[3 + j], (x, y, 1 - c)).wait_recv()
        for cp in first + passed:
            cp.wait_send()

    return pl.pallas_call(
        body,
        name="gather_weights",
        out_shape=jax.ShapeDtypeStruct((N_CHIPS, R, n), pack.dtype),
        in_specs=[ANY],
        out_specs=ANY,
        scratch_shapes=[pltpu.SemaphoreType.DMA((6,)), pltpu.SemaphoreType.DMA((6,))],
    )(pack)


def swap_halves(g):
    n_slot, R, n = g.shape
    rh = R // 2

    def body(g_ref, got_ref, send_sem, recv_sem):
        x, y, c, _ = _place()
        cp = _remote(g_ref.at[:, pl.ds((1 - c) * rh, rh), :], got_ref, send_sem, recv_sem, (x, y, 1 - c))
        cp.start()
        cp.wait()

    return pl.pallas_call(
        body,
        name="swap_halves",
        out_shape=jax.ShapeDtypeStruct((n_slot, rh, n), g.dtype),
        in_specs=[ANY],
        out_specs=ANY,
        scratch_shapes=[pltpu.SemaphoreType.DMA, pltpu.SemaphoreType.DMA],
    )(g)


def scatter_chips(buf):
    def body(s_ref, got_ref, send_sems, recv_sems):
        x, y, c, chips = _place()
        me = 2 * x + y
        cps = [_remote(s_ref.at[2 * px + py], got_ref.at[me], send_sems.at[j], recv_sems.at[j], (px, py, c)) for j, (px, py) in enumerate(chips)]
        for cp in cps:
            cp.start()
        for j, (px, py) in enumerate(chips):
            got = got_ref.at[2 * px + py]
            _remote(got, got, send_sems.at[j], recv_sems.at[j], (px, py, c)).wait_recv()
        for cp in cps:
            cp.wait_send()

    return pl.pallas_call(
        body,
        name="scatter_chips",
        out_shape=jax.ShapeDtypeStruct(buf.shape, buf.dtype),
        in_specs=[ANY],
        out_specs=ANY,
        scratch_shapes=[pltpu.SemaphoreType.DMA((3,)), pltpu.SemaphoreType.DMA((3,))],
    )(buf)


def swap_with_sibling(h):
    def body(h_ref, got_ref, send_sem, recv_sem):
        x, y, c, _ = _place()
        cp = _remote(h_ref, got_ref, send_sem, recv_sem, (x, y, 1 - c))
        cp.start()
        cp.wait()

    return pl.pallas_call(
        body,
        name="swap_with_sibling",
        out_shape=jax.ShapeDtypeStruct(h.shape, h.dtype),
        in_specs=[ANY],
        out_specs=ANY,
        scratch_shapes=[pltpu.SemaphoreType.DMA, pltpu.SemaphoreType.DMA],
    )(h)


def chip_sum(g, got, core, chip):
    n_slot, R, n = g.shape
    rh = R // 2
    nb = rh // PACK_TILE

    def body(pos, g_ref, got_ref, wire_ref, own_ref):
        k = pl.program_id(1)
        s = g_ref[0].astype(F32) + got_ref[0].astype(F32)
        wire_ref[0] = s.astype(wire_ref.dtype)

        @pl.when(k == pos[1])
        def _():
            own_ref[...] = s

    grid_spec = pltpu.PrefetchScalarGridSpec(
        num_scalar_prefetch=1,
        grid=(nb, n_slot),
        in_specs=[
            pl.BlockSpec((1, PACK_TILE, n), lambda i, k, pos: (k, pos[0] * nb + i, 0)),
            pl.BlockSpec((1, PACK_TILE, n), lambda i, k, pos: (k, i, 0)),
        ],
        out_specs=[
            pl.BlockSpec((1, PACK_TILE, n), lambda i, k, pos: (k, i, 0)),
            pl.BlockSpec((PACK_TILE, n), lambda i, k, pos: (i, 0)),
        ],
    )
    return pl.pallas_call(
        body,
        name="chip_sum",
        grid_spec=grid_spec,
        out_shape=[jax.ShapeDtypeStruct((n_slot, rh, n), WIRE_DTYPE), jax.ShapeDtypeStruct((rh, n), F32)],
        compiler_params=_cparams(2),
    )(jnp.stack([core, chip]).astype(jnp.int32), g, got)


def mesh_sum(own, got, chip):
    rh, n = own.shape
    n_slot = got.shape[0]

    def body(pos, own_ref, *refs):
        out_ref = refs[-1]
        acc = own_ref[...]
        for k in range(n_slot):
            acc = acc + jnp.where(k != pos[0], refs[k][0].astype(F32), 0.0)
        out_ref[...] = acc

    grid_spec = pltpu.PrefetchScalarGridSpec(
        num_scalar_prefetch=1,
        grid=(rh // PACK_TILE,),
        in_specs=[pl.BlockSpec((PACK_TILE, n), lambda i, pos: (i, 0))]
        + [pl.BlockSpec((1, PACK_TILE, n), lambda i, pos, k=k: (k, i, 0)) for k in range(n_slot)],
        out_specs=pl.BlockSpec((PACK_TILE, n), lambda i, pos: (i, 0)),
    )
    return pl.pallas_call(
        body, name="mesh_sum", grid_spec=grid_spec, out_shape=jax.ShapeDtypeStruct((rh, n), F32), compiler_params=_cparams(1)
    )(jnp.stack([chip]).astype(jnp.int32), own, *([got] * n_slot))


def sum_devices(name, v):
    def body(v_ref, o_ref):
        acc = v_ref[0]
        for d in range(1, N_DEV):
            acc = acc + v_ref[d]
        o_ref[...] = acc

    return pl.pallas_call(body, name=name, out_shape=jax.ShapeDtypeStruct(v.shape[1:], F32))(v)


def adamw(name, w, g, m, v):
    shape = w.shape
    w2, g2, m2, v2 = (t.reshape(-1, shape[-1]) for t in (w, g, m, v))
    rows, n = w2.shape
    tr = _pick(rows, (256, 128, 64, 32, 16, 8))
    c1 = 1.0 / (1.0 - ADAM_B1**ADAM_STEP)
    c2 = 1.0 / (1.0 - ADAM_B2**ADAM_STEP)

    def body(w_ref, g_ref, m_ref, v_ref, d_ref, nm_ref, nv_ref):
        gv = g_ref[...]
        nm = ADAM_B1 * m_ref[...] + (1.0 - ADAM_B1) * gv
        nv = ADAM_B2 * v_ref[...] + (1.0 - ADAM_B2) * jnp.square(gv)
        d_ref[...] = -ADAM_LR * ((nm * c1) / (jnp.sqrt(nv * c2) + ADAM_EPS) + ADAM_WD * w_ref[...])
        nm_ref[...] = nm
        nv_ref[...] = nv

    spec = pl.BlockSpec((tr, n), lambda i: (i, 0))
    outs = pl.pallas_call(
        body,
        name=name,
        grid=(rows // tr,),
        in_specs=[spec] * 4,
        out_specs=[spec] * 3,
        out_shape=[jax.ShapeDtypeStruct((rows, n), F32)] * 3,
        compiler_params=_cparams(1),
    )(w2, g2, m2, v2)
    return [o.reshape(shape) for o in outs]


def ada_fwd(c_all, w_ada):
    n_tok, d = c_all.shape
    depth, _, cols = w_ada.shape
    tn = _pick(cols, (512, 384, 256, 128))

    def body(c_ref, w_ref, o_ref):
        o_ref[0] = jnp.dot(_silu(c_ref[...]).astype(MXU_DTYPE), w_ref[0].astype(MXU_DTYPE), preferred_element_type=F32)

    return pl.pallas_call(
        body,
        name="ada_fwd",
        grid=(depth, cols // tn),
        in_specs=[pl.BlockSpec((n_tok, d), lambda l, j: (0, 0)), pl.BlockSpec((1, d, tn), lambda l, j: (l, 0, j))],
        out_specs=pl.BlockSpec((1, n_tok, tn), lambda l, j: (l, 0, j)),
        out_shape=jax.ShapeDtypeStruct((depth, n_tok, cols), F32),
        compiler_params=_cparams(2),
    )(c_all, w_ada)


COL_SHARDED = ("w_in", "w_uq", "w_ukv", "w_up")


PART_ROWS = 16


def _part_rows(shape):
    n = int(np.prod(shape)) // PACK_W
    return -(-n // PART_ROWS) * PART_ROWS, n


def _with_fill(parts, axis, rows):
    out, used = [], 0
    for p in parts:
        pad = -p.shape[axis] % PART_ROWS
        out.append(p)
        if pad:
            out.append(jnp.zeros(p.shape[:axis] + (pad,) + p.shape[axis + 1 :], p.dtype))
        used += p.shape[axis] + pad
    out.append(jnp.zeros(parts[0].shape[:axis] + (rows - used,) + parts[0].shape[axis + 1 :], parts[0].dtype))
    return jnp.concatenate(out, axis=axis)


def _pack_rows(parts, rows):
    return _with_fill([p.reshape(-1, PACK_W) for p in parts], 0, rows)


def _big_rows(shards):
    n = sum(_part_rows(shards[k].shape)[0] for k in BIG)
    return -(-n // (2 * PACK_TILE)) * 2 * PACK_TILE


def unpack_gathered(gathered, shard_shapes):
    out, r0 = {}, 0
    for k in BIG:
        shp = shard_shapes[k]
        step, n = _part_rows(shp)
        seg = gathered[:, r0 : r0 + n].reshape(N_CHIPS, *shp)
        r0 += step
        if k in COL_SHARDED:
            out[k] = seg.transpose(1, 2, 0, 3).reshape(shp[0], shp[1], N_CHIPS * shp[2])
        else:
            out[k] = seg.transpose(1, 0, 2, 3).reshape(shp[0], N_CHIPS * shp[1], shp[2])
    return out


def pack_full_grads(grads, shard_shapes, rows):
    parts = []
    for k in BIG:
        _, rows_k, cols_k = shard_shapes[k]
        layers = grads[k]
        if (rows_k * cols_k // PACK_W) % PART_ROWS:
            layers = [jnp.concatenate(layers, axis=0)]
        for g in layers:
            if k in COL_SHARDED:
                g = g.reshape(g.shape[0], N_CHIPS, cols_k).transpose(1, 0, 2)
            else:
                g = g.reshape(-1, N_CHIPS, rows_k, cols_k).transpose(1, 0, 2, 3)
            parts.append(g.reshape(N_CHIPS, -1, PACK_W))
    return _with_fill(parts, 1, rows)


def unpack_shards(buf, shard_shapes):
    out, r0 = {}, 0
    for k in BIG:
        step, n = _part_rows(shard_shapes[k])
        out[k] = buf[r0 : r0 + n].reshape(shard_shapes[k])
        r0 += step
    return out


def _flat_pack(arrs, row_multiple=8):
    flat = jnp.concatenate([a.reshape(-1) for a in arrs])
    per = PACK_W * row_multiple
    n = -(-flat.shape[0] // per) * per
    return jnp.pad(flat, (0, n - flat.shape[0])).reshape(-1, PACK_W)


def _flat_unpack(buf, shapes):
    flat, out, o = buf.reshape(-1), [], 0
    for s in shapes:
        n = int(np.prod(s))
        out.append(flat[o : o + n].reshape(s))
        o += n
    return out


WEIGHTS = (
    "w_ada", "b_ada", "norm_mix", "w_in", "conv_w", "conv_b", "dt_bias", "a_log", "d_skip", "ssd_norm", "q_norm", "w_uq",
    "kv_norm", "w_ukv", "attn_norm", "w_out", "norm_mlp", "w_up", "conv_ff_w", "conv_ff_b", "w_down", "final_norm",
)
REPLICATED = ("b_ada", "norm_mix", "conv_b", "dt_bias", "a_log", "d_skip", "ssd_norm", "q_norm", "kv_norm", "attn_norm",
              "norm_mlp", "conv_ff_b", "final_norm")
CONV_SHARDED = ("conv_w", "conv_ff_w")


def kernel(x, c, positions, w_ada, b_ada, norm_mix, w_in, conv_w, conv_b, dt_bias, a_log, d_skip, ssd_norm, q_norm, w_uq, kv_norm, w_ukv, attn_norm, w_out, norm_mlp, w_up, conv_ff_w, conv_ff_b, w_down, final_norm, loss_target, m_w_ada, m_b_ada, m_norm_mix, m_w_in, m_conv_w, m_conv_b, m_dt_bias, m_a_log, m_d_skip, m_ssd_norm, m_q_norm, m_w_uq, m_kv_norm, m_w_ukv, m_attn_norm, m_w_out, m_norm_mlp, m_w_up, m_conv_ff_w, m_conv_ff_b, m_w_down, m_final_norm, v_w_ada, v_b_ada, v_norm_mix, v_w_in, v_conv_w, v_conv_b, v_dt_bias, v_a_log, v_d_skip, v_ssd_norm, v_q_norm, v_w_uq, v_kv_norm, v_w_ukv, v_attn_norm, v_w_out, v_norm_mlp, v_w_up, v_conv_ff_w, v_conv_ff_b, v_w_down, v_final_norm):
    loc = locals()
    Wl = {k: loc[k] for k in WEIGHTS}
    Ml = {k: loc["m_" + k] for k in WEIGHTS}
    Vl = {k: loc["v_" + k] for k in WEIGHTS}
    xi, yi, ci = lax.axis_index("x"), lax.axis_index("y"), lax.axis_index("c")
    chip = 2 * xi + yi
    dev = 2 * chip + ci
    B, S, D = x.shape
    n_tok = N_DEV * B

    c_all = all_gather8("gather_c", c.reshape(8, -1)).reshape(n_tok, D)
    mod_cols = ada_fwd(c_all, w_ada)
    cols = mod_cols.shape[-1]
    half = n_tok // 2
    mod_mine = lax.dynamic_slice_in_dim(mod_cols, ci * half, half, axis=1)
    small_in = _flat_pack([mod_mine, conv_w, conv_ff_w])
    n_mod = mod_mine.size // PACK_W
    small_all = all_gather8("gather_mod", small_in).reshape(N_CHIPS, 2, -1, PACK_W)
    mod_all = small_all[:, :, :n_mod].reshape(N_CHIPS, 2, DEPTH, half, cols).transpose(2, 1, 3, 0, 4).reshape(DEPTH, n_tok, N_CHIPS * cols)
    mod_raw = lax.dynamic_slice_in_dim(mod_all, dev * B, B, axis=1).reshape(DEPTH, B, 6, D)
    conv_parts = [_flat_unpack(small_all[k, 0, n_mod:], [conv_w.shape, conv_ff_w.shape]) for k in range(N_CHIPS)]
    conv_full = {name: jnp.concatenate([conv_parts[k][i] for k in range(N_CHIPS)], axis=-1) for i, name in enumerate(CONV_SHARDED)}

    shard_shapes = {k: Wl[k].shape for k in BIG}
    rows = _big_rows(Wl)
    pack = _pack_rows([Wl[k].astype(MXU_DTYPE) for k in BIG], rows)
    gathered = lax.dynamic_update_slice_in_dim(gather_weights(pack), pack[None], chip, axis=0)
    W = unpack_gathered(gathered, shard_shapes)
    W.update({k: Wl[k] for k in REPLICATED})
    W.update(conv_full)

    loss_lanes, grad_x, grads, dmod_raw = local_step(x, loss_target, positions, mod_raw, W)
    loss = lax.psum(loss_lanes[0, 0], ("x", "y", "c"))

    gpack = pack_full_grads(grads, shard_shapes, rows)
    wire, own = chip_sum(gpack, swap_halves(gpack), ci, chip)
    mine = mesh_sum(own, scatter_chips(wire), chip)
    theirs = swap_with_sibling(mine)
    both = jnp.concatenate([jnp.where(ci == 0, mine, theirs), jnp.where(ci == 0, theirs, mine)], axis=0)
    g_shard = unpack_shards(both, shard_shapes)

    small_names = REPLICATED + CONV_SHARDED
    small_out = _flat_pack([grads[k] for k in small_names] + [dmod_raw])
    small_got = all_gather8("gather_small", small_out).reshape(N_DEV, -1, PACK_W)
    small_sum = _flat_unpack(sum_devices("sum_small", small_got), [grads[k].shape for k in small_names])
    G = dict(zip(small_names, small_sum))
    for name in CONV_SHARDED:
        width = Wl[name].shape[-1]
        G[name] = lax.dynamic_slice_in_dim(G[name], chip * width, width, axis=-1)
    n_small = sum(grads[k].size for k in small_names)
    dmod_all = small_got.reshape(N_DEV, -1)[:, n_small : n_small + dmod_raw.size].reshape(N_DEV, DEPTH, B, 6 * D)
    dmod_all = dmod_all.transpose(1, 0, 2, 3).reshape(DEPTH, n_tok, 6 * D)
    dmod_cols = lax.dynamic_slice_in_dim(dmod_all, chip * cols, cols, axis=-1)
    G["w_ada"] = jnp.stack([matmul(f"w_ada_wgrad{l}", c_all, dmod_cols[l], ta=True, a_act=_silu) for l in range(DEPTH)])
    G.update(g_shard)

    deltas, new_m, new_v = {}, {}, {}
    small_upd = adamw("adamw_small", *[_flat_pack([t[k] for k in REPLICATED]) for t in (Wl, G, Ml, Vl)])
    for res, t in zip((deltas, new_m, new_v), small_upd):
        res.update(zip(REPLICATED, _flat_unpack(t, [Wl[k].shape for k in REPLICATED])))
    for k in WEIGHTS:
        if k not in REPLICATED:
            deltas[k], new_m[k], new_v[k] = adamw("adamw_" + k, Wl[k], G[k], Ml[k], Vl[k])
    return (loss, grad_x, *[G[k] for k in WEIGHTS], *[deltas[k] for k in WEIGHTS], *[new_m[k] for k in WEIGHTS], *[new_v[k] for k in WEIGHTS])
```
